```python
import jax, jax.numpy as jnp
from jax import lax
import numpy as np

D_MODEL = 2048
BATCH = 8
SEQ = 2048
DEPTH = 4

HEAD_DIM = 128
SB_HEADS = D_MODEL // (2 * HEAD_DIM)
GDN_HEADS = D_MODEL // (2 * HEAD_DIM)
SB_WIDTH = SB_HEADS * HEAD_DIM
GDN_WIDTH = GDN_HEADS * HEAD_DIM
D_MIX = SB_WIDTH + GDN_WIDTH
N_IN = 3 * SB_WIDTH + 4 * GDN_WIDTH + 2 * GDN_HEADS
SB_BLOCK = 128
SHORT_CONV = 4
CHUNK = 64
N_MEM = 256
X_HEADS = 4
X_HEAD_DIM = 128
X_WIDTH = X_HEADS * X_HEAD_DIM
D_FF = (11 * D_MODEL) // 4
FFN_CONV = 3
EPS = 1e-6

kernel_name = "hybrid_stickbreak_gdn_memxattn_convffn"


def rmsnorm(x, g):
    xf = x.astype(jnp.float32)
    y = xf * lax.rsqrt(jnp.mean(xf * xf, axis=-1, keepdims=True) + EPS)
    return (y * g.astype(jnp.float32)).astype(x.dtype)


def l2norm(x):
    return x * lax.rsqrt(jnp.sum(x * x, axis=-1, keepdims=True) + EPS)


def causal_dwconv(x, w):
    k = w.shape[0]
    return lax.conv_general_dilated(
        x, w[:, None, :].astype(x.dtype), window_strides=(1,), padding=[(k - 1, 0)],
        dimension_numbers=('NWC', 'WIO', 'NWC'), feature_group_count=x.shape[-1])


def stick_breaking_attention(q, k, v):
    b, s, h, dh = q.shape
    scale = dh ** -0.5
    n_blk = s // SB_BLOCK
    qb = jnp.moveaxis(q.reshape(b, n_blk, SB_BLOCK, h, dh), 1, 0)
    key_pos = jnp.arange(s)

    def block(args):
        q_blk, blk_idx = args
        z = jnp.einsum('bqhd,bshd->bhqs', q_blk, k) * scale
        q_pos = blk_idx * SB_BLOCK + jnp.arange(SB_BLOCK)
        valid = key_pos[None, :] < q_pos[:, None]
        log_beta = jax.nn.log_sigmoid(z)
        log_stay = jnp.where(valid, jax.nn.log_sigmoid(-z), 0.0)
        later = lax.cumsum(log_stay, axis=3, reverse=True) - log_stay
        weights = jnp.where(valid, jnp.exp(log_beta + later), 0.0)
        return jnp.einsum('bhqs,bshd->bqhd', weights, v)

    out = lax.map(block, (qb, jnp.arange(n_blk)))
    return jnp.moveaxis(out, 0, 1).reshape(b, s, h, dh)


def gated_delta_rule(q, k, v, beta, g):
    b, s, h, dk = q.shape
    dv = v.shape[-1]
    n = s // CHUNK

    def chunks(t):
        t = jnp.swapaxes(t, 1, 2)
        return t.reshape(b, h, n, CHUNK, *t.shape[3:])

    q, k, v, beta, g = chunks(q), chunks(k), chunks(v), chunks(beta), chunks(g)
    g_cum = jnp.cumsum(g, axis=-1)
    idx = jnp.arange(CHUNK)
    lower_incl = idx[:, None] >= idx[None, :]
    strict = idx[:, None] > idx[None, :]
    decay = jnp.exp(jnp.where(lower_incl, g_cum[..., :, None] - g_cum[..., None, :], -jnp.inf))
    k_beta = k * beta[..., None]
    v_beta = v * beta[..., None]
    lmat = jnp.where(strict, jnp.einsum('bhnid,bhnjd->bhnij', k_beta, k) * decay, 0.0)
    eye = jnp.eye(CHUNK, dtype=jnp.float32)
    t_inv = lax.linalg.triangular_solve(eye + lmat, jnp.broadcast_to(eye, lmat.shape),
                                        left_side=True, lower=True, unit_diagonal=True)
    u = t_inv @ v_beta
    w = t_inv @ (k_beta * jnp.exp(g_cum)[..., None])
    attn_intra = jnp.where(lower_incl, jnp.einsum('bhnid,bhnjd->bhnij', q, k) * decay, 0.0)
    q_decay = q * jnp.exp(g_cum)[..., None]
    k_tail = k * jnp.exp(g_cum[..., -1:] - g_cum)[..., None]
    chunk_decay = jnp.exp(g_cum[..., -1])

    def step(state, xs):
        u_c, w_c, qd_c, a_c, kt_c, cd_c = xs
        v_new = u_c - w_c @ state
        o = qd_c @ state + a_c @ v_new
        state = state * cd_c[..., None, None] + jnp.swapaxes(kt_c, -1, -2) @ v_new
        return state, o

    xs = tuple(jnp.moveaxis(t, 2, 0) for t in (u, w, q_decay, attn_intra, k_tail, chunk_decay))
    state0 = jnp.zeros((b, h, dk, dv), jnp.float32)
    _, o = lax.scan(step, state0, xs)
    o = jnp.moveaxis(o, 0, 2).reshape(b, h, s, dv)
    return jnp.swapaxes(o, 1, 2)


def gated_deltanet(q, k, v, z, b_logit, a_logit, conv_w, a_log, dt_bias, norm_g):
    bsz, s, _ = q.shape
    qkv = jnp.concatenate([q, k, v], axis=-1).astype(jnp.float32)
    qkv = jax.nn.silu(causal_dwconv(qkv, conv_w.astype(jnp.float32)))
    q, k, v = jnp.split(qkv, 3, axis=-1)
    heads = lambda t: t.reshape(bsz, s, GDN_HEADS, HEAD_DIM)
    q = l2norm(heads(q)) * (HEAD_DIM ** -0.5)
    k = l2norm(heads(k))
    v = heads(v)
    beta = jax.nn.sigmoid(b_logit.astype(jnp.float32))
    g = -jnp.exp(a_log.astype(jnp.float32)) * jax.nn.softplus(
        a_logit.astype(jnp.float32) + dt_bias.astype(jnp.float32))
    o = gated_delta_rule(q, k, v, beta, g)
    o = rmsnorm(o, norm_g) * jax.nn.silu(heads(z.astype(jnp.float32)))
    return o.reshape(bsz, s, GDN_WIDTH)


def memory_cross_attention(h, mem_n, w_q, w_kv, w_o):
    bsz, s, _ = h.shape
    m = mem_n.shape[1]
    q = (h @ w_q).reshape(bsz, s, X_HEADS, X_HEAD_DIM)
    k, v = jnp.split(mem_n @ w_kv, 2, axis=-1)
    k = k.reshape(bsz, m, X_HEADS, X_HEAD_DIM)
    v = v.reshape(bsz, m, X_HEADS, X_HEAD_DIM)
    logits = jnp.einsum('bshd,bmhd->bhsm', q.astype(jnp.float32), k.astype(jnp.float32)) * (X_HEAD_DIM ** -0.5)
    p = jax.nn.softmax(logits, axis=-1)
    o = jnp.einsum('bhsm,bmhd->bshd', p, v.astype(jnp.float32))
    return o.reshape(bsz, s, X_WIDTH).astype(h.dtype) @ w_o


def conv_ffn(h, w_up, conv_w, conv_b, w_down):
    u = causal_dwconv(h @ w_up, conv_w) + conv_b
    gate, up = jnp.split(u, 2, axis=-1)
    return (jax.nn.silu(gate) * up) @ w_down


def _fwd_setup_inputs(seed: int = 0) -> dict:
    key = jax.random.key(seed)
    ks = iter(jax.random.split(key, 32))
    f32 = jnp.float32
    nrm = lambda shape, scale: jax.random.normal(next(ks), shape, f32) * scale
    gain = lambda shape: 1.0 + 0.02 * jax.random.normal(next(ks), shape, f32)
    dt = jnp.exp(jax.random.uniform(next(ks), (DEPTH, GDN_HEADS), f32, np.log(1e-3), np.log(1e-1)))
    return {
        "x": jax.random.normal(next(ks), (BATCH, SEQ, D_MODEL), f32),
        "mem": jax.random.normal(next(ks), (BATCH, N_MEM, D_MODEL), f32),
        "mix_norm": gain((DEPTH, D_MODEL)),
        "w_in": nrm((DEPTH, D_MODEL, N_IN), D_MODEL ** -0.5),
        "gdn_conv": nrm((DEPTH, SHORT_CONV, 3 * GDN_WIDTH), SHORT_CONV ** -0.5),
        "gdn_a_log": jnp.log(jax.random.uniform(next(ks), (DEPTH, GDN_HEADS), f32, 1.0, 16.0)),
        "gdn_dt_bias": dt + jnp.log(-jnp.expm1(-dt)),
        "gdn_norm": gain((DEPTH, HEAD_DIM)),
        "w_out": nrm((DEPTH, D_MIX, D_MODEL), D_MIX ** -0.5),
        "xattn_norm": gain((DEPTH, D_MODEL)),
        "mem_norm": gain((DEPTH, D_MODEL)),
        "w_xq": nrm((DEPTH, D_MODEL, X_WIDTH), D_MODEL ** -0.5),
        "w_xkv": nrm((DEPTH, D_MODEL, 2 * X_WIDTH), D_MODEL ** -0.5),
        "w_xo": nrm((DEPTH, X_WIDTH, D_MODEL), X_WIDTH ** -0.5),
        "ffn_norm": gain((DEPTH, D_MODEL)),
        "w_up": nrm((DEPTH, D_MODEL, 2 * D_FF), D_MODEL ** -0.5),
        "ffn_conv": nrm((DEPTH, FFN_CONV, 2 * D_FF), FFN_CONV ** -0.5),
        "ffn_conv_bias": nrm((DEPTH, 2 * D_FF), 0.02),
        "w_down": nrm((DEPTH, D_FF, D_MODEL), D_FF ** -0.5),
        "final_norm": gain((D_MODEL,)),
    }


def _fwd_reference(x, mem, mix_norm, w_in, gdn_conv, gdn_a_log, gdn_dt_bias, gdn_norm, w_out,
              xattn_norm, mem_norm, w_xq, w_xkv, w_xo, ffn_norm, w_up, ffn_conv,
              ffn_conv_bias, w_down, final_norm):
    bsz, s, _ = x.shape
    cuts = [SB_WIDTH, 2 * SB_WIDTH, 3 * SB_WIDTH,
            3 * SB_WIDTH + GDN_WIDTH, 3 * SB_WIDTH + 2 * GDN_WIDTH,
            3 * SB_WIDTH + 3 * GDN_WIDTH, 3 * SB_WIDTH + 4 * GDN_WIDTH,
            3 * SB_WIDTH + 4 * GDN_WIDTH + GDN_HEADS]
    sb_heads = lambda t: t.astype(jnp.float32).reshape(bsz, s, SB_HEADS, HEAD_DIM)
    for l in range(DEPTH):
        h = rmsnorm(x, mix_norm[l])
        proj = h @ w_in[l]
        sq, sk, sv, gq, gk, gv, gz, gb, ga = jnp.split(proj, cuts, axis=-1)
        sb_out = stick_breaking_attention(sb_heads(sq), sb_heads(sk), sb_heads(sv))
        sb_out = sb_out.reshape(bsz, s, SB_WIDTH)
        gdn_out = gated_deltanet(gq, gk, gv, gz, gb, ga, gdn_conv[l], gdn_a_log[l],
                                 gdn_dt_bias[l], gdn_norm[l])
        mixed = jnp.concatenate([sb_out, gdn_out], axis=-1).astype(x.dtype)
        x = x + mixed @ w_out[l]
        mem_n = rmsnorm(mem, mem_norm[l])
        x = x + memory_cross_attention(rmsnorm(x, xattn_norm[l]), mem_n, w_xq[l], w_xkv[l], w_xo[l])
        x = x + conv_ffn(rmsnorm(x, ffn_norm[l]), w_up[l], ffn_conv[l], ffn_conv_bias[l], w_down[l])
    return rmsnorm(x, final_norm)


import jax as _jax
import jax.numpy as _jnp

TWIN_FORMAT = 'train_step'
FWD_PARAMS = ['x', 'mem', 'mix_norm', 'w_in', 'gdn_conv', 'gdn_a_log', 'gdn_dt_bias', 'gdn_norm', 'w_out', 'xattn_norm', 'mem_norm', 'w_xq', 'w_xkv', 'w_xo', 'ffn_norm', 'w_up', 'ffn_conv', 'ffn_conv_bias', 'w_down', 'final_norm']
TWIN_WEIGHTS = ['mix_norm', 'w_in', 'gdn_conv', 'gdn_a_log', 'gdn_dt_bias', 'gdn_norm', 'w_out', 'xattn_norm', 'mem_norm', 'w_xq', 'w_xkv', 'w_xo', 'ffn_norm', 'w_up', 'ffn_conv', 'ffn_conv_bias', 'w_down', 'final_norm']
TWIN_DIFF_INPUT = 'x'
TWIN_INPUTS = ['x', 'mem', 'mix_norm', 'w_in', 'gdn_conv', 'gdn_a_log', 'gdn_dt_bias', 'gdn_norm', 'w_out', 'xattn_norm', 'mem_norm', 'w_xq', 'w_xkv', 'w_xo', 'ffn_norm', 'w_up', 'ffn_conv', 'ffn_conv_bias', 'w_down', 'final_norm', 'loss_target', 'm_mix_norm', 'm_w_in', 'm_gdn_conv', 'm_gdn_a_log', 'm_gdn_dt_bias', 'm_gdn_norm', 'm_w_out', 'm_xattn_norm', 'm_mem_norm', 'm_w_xq', 'm_w_xkv', 'm_w_xo', 'm_ffn_norm', 'm_w_up', 'm_ffn_conv', 'm_ffn_conv_bias', 'm_w_down', 'm_final_norm', 'v_mix_norm', 'v_w_in', 'v_gdn_conv', 'v_gdn_a_log', 'v_gdn_dt_bias', 'v_gdn_norm', 'v_w_out', 'v_xattn_norm', 'v_mem_norm', 'v_w_xq', 'v_w_xkv', 'v_w_xo', 'v_ffn_norm', 'v_w_up', 'v_ffn_conv', 'v_ffn_conv_bias', 'v_w_down', 'v_final_norm']
TWIN_OUTPUTS = ['loss', 'grad_x', 'grad_mix_norm', 'grad_w_in', 'grad_gdn_conv', 'grad_gdn_a_log', 'grad_gdn_dt_bias', 'grad_gdn_norm', 'grad_w_out', 'grad_xattn_norm', 'grad_mem_norm', 'grad_w_xq', 'grad_w_xkv', 'grad_w_xo', 'grad_ffn_norm', 'grad_w_up', 'grad_ffn_conv', 'grad_ffn_conv_bias', 'grad_w_down', 'grad_final_norm', 'delta_mix_norm', 'delta_w_in', 'delta_gdn_conv', 'delta_gdn_a_log', 'delta_gdn_dt_bias', 'delta_gdn_norm', 'delta_w_out', 'delta_xattn_norm', 'delta_mem_norm', 'delta_w_xq', 'delta_w_xkv', 'delta_w_xo', 'delta_ffn_norm', 'delta_w_up', 'delta_ffn_conv', 'delta_ffn_conv_bias', 'delta_w_down', 'delta_final_norm', 'new_m_mix_norm', 'new_m_w_in', 'new_m_gdn_conv', 'new_m_gdn_a_log', 'new_m_gdn_dt_bias', 'new_m_gdn_norm', 'new_m_w_out', 'new_m_xattn_norm', 'new_m_mem_norm', 'new_m_w_xq', 'new_m_w_xkv', 'new_m_w_xo', 'new_m_ffn_norm', 'new_m_w_up', 'new_m_ffn_conv', 'new_m_ffn_conv_bias', 'new_m_w_down', 'new_m_final_norm', 'new_v_mix_norm', 'new_v_w_in', 'new_v_gdn_conv', 'new_v_gdn_a_log', 'new_v_gdn_dt_bias', 'new_v_gdn_norm', 'new_v_w_out', 'new_v_xattn_norm', 'new_v_mem_norm', 'new_v_w_xq', 'new_v_w_xkv', 'new_v_w_xo', 'new_v_ffn_norm', 'new_v_w_up', 'new_v_ffn_conv', 'new_v_ffn_conv_bias', 'new_v_w_down', 'new_v_final_norm']
TWIN_LEAF_KINDS = {'loss': 'loss', 'grad_x': 'grad_x', 'grad_mix_norm': 'grad_w', 'grad_w_in': 'grad_w', 'grad_gdn_conv': 'grad_w', 'grad_gdn_a_log': 'grad_w', 'grad_gdn_dt_bias': 'grad_w', 'grad_gdn_norm': 'grad_w', 'grad_w_out': 'grad_w', 'grad_xattn_norm': 'grad_w', 'grad_mem_norm': 'grad_w', 'grad_w_xq': 'grad_w', 'grad_w_xkv': 'grad_w', 'grad_w_xo': 'grad_w', 'grad_ffn_norm': 'grad_w', 'grad_w_up': 'grad_w', 'grad_ffn_conv': 'grad_w', 'grad_ffn_conv_bias': 'grad_w', 'grad_w_down': 'grad_w', 'grad_final_norm': 'grad_w', 'delta_mix_norm': 'delta_w', 'delta_w_in': 'delta_w', 'delta_gdn_conv': 'delta_w', 'delta_gdn_a_log': 'delta_w', 'delta_gdn_dt_bias': 'delta_w', 'delta_gdn_norm': 'delta_w', 'delta_w_out': 'delta_w', 'delta_xattn_norm': 'delta_w', 'delta_mem_norm': 'delta_w', 'delta_w_xq': 'delta_w', 'delta_w_xkv': 'delta_w', 'delta_w_xo': 'delta_w', 'delta_ffn_norm': 'delta_w', 'delta_w_up': 'delta_w', 'delta_ffn_conv': 'delta_w', 'delta_ffn_conv_bias': 'delta_w', 'delta_w_down': 'delta_w', 'delta_final_norm': 'delta_w', 'new_m_mix_norm': 'new_m', 'new_m_w_in': 'new_m', 'new_m_gdn_conv': 'new_m', 'new_m_gdn_a_log': 'new_m', 'new_m_gdn_dt_bias': 'new_m', 'new_m_gdn_norm': 'new_m', 'new_m_w_out': 'new_m', 'new_m_xattn_norm': 'new_m', 'new_m_mem_norm': 'new_m', 'new_m_w_xq': 'new_m', 'new_m_w_xkv': 'new_m', 'new_m_w_xo': 'new_m', 'new_m_ffn_norm': 'new_m', 'new_m_w_up': 'new_m', 'new_m_ffn_conv': 'new_m', 'new_m_ffn_conv_bias': 'new_m', 'new_m_w_down': 'new_m', 'new_m_final_norm': 'new_m', 'new_v_mix_norm': 'new_v', 'new_v_w_in': 'new_v', 'new_v_gdn_conv': 'new_v', 'new_v_gdn_a_log': 'new_v', 'new_v_gdn_dt_bias': 'new_v', 'new_v_gdn_norm': 'new_v', 'new_v_w_out': 'new_v', 'new_v_xattn_norm': 'new_v', 'new_v_mem_norm': 'new_v', 'new_v_w_xq': 'new_v', 'new_v_w_xkv': 'new_v', 'new_v_w_xo': 'new_v', 'new_v_ffn_norm': 'new_v', 'new_v_w_up': 'new_v', 'new_v_ffn_conv': 'new_v', 'new_v_ffn_conv_bias': 'new_v', 'new_v_w_down': 'new_v', 'new_v_final_norm': 'new_v'}


def _forward(args):
    return _fwd_reference(*[args[k] for k in FWD_PARAMS])


def _output_shape():
    out = _jax.eval_shape(lambda: _forward(_fwd_setup_inputs(0)))
    return out.shape, out.dtype

N_MICROBATCH = 1
ADAM_LR = 0.001
ADAM_B1 = 0.9
ADAM_B2 = 0.999
ADAM_EPS = 1e-08
ADAM_WD = 0.01
ADAM_STEP = 10
PER_EXAMPLE_BATCH_AXIS = {'x': 0, 'mem': 0, 'loss_target': 0}
SHARED_INPUTS = []
_WEIGHT_DTYPES = {'mix_norm': _jnp.float32, 'w_in': _jnp.float32, 'gdn_conv': _jnp.float32, 'gdn_a_log': _jnp.float32, 'gdn_dt_bias': _jnp.float32, 'gdn_norm': _jnp.float32, 'w_out': _jnp.float32, 'xattn_norm': _jnp.float32, 'mem_norm': _jnp.float32, 'w_xq': _jnp.float32, 'w_xkv': _jnp.float32, 'w_xo': _jnp.float32, 'ffn_norm': _jnp.float32, 'w_up': _jnp.float32, 'ffn_conv': _jnp.float32, 'ffn_conv_bias': _jnp.float32, 'w_down': _jnp.float32, 'final_norm': _jnp.float32}
MOMENT_SCALE = {'mix_norm': 5.245241e-02, 'w_in': 2.769214e-02, 'gdn_conv': 2.678080e-02, 'gdn_a_log': 1.104529e-01, 'gdn_dt_bias': 1.055310e-01, 'gdn_norm': 9.659963e-02, 'w_out': 3.665180e-02, 'xattn_norm': 5.940767e-03, 'mem_norm': 9.181247e-03, 'w_xq': 1.193196e-02, 'w_xkv': 1.225452e-02, 'w_xo': 6.240586e-03, 'ffn_norm': 4.318876e-02, 'w_up': 1.827661e-02, 'ffn_conv': 1.821261e-02, 'ffn_conv_bias': 1.819064e-02, 'w_down': 2.988651e-02, 'final_norm': 7.997111e+00}


def _to_microbatches(a, axis):
    t = _jnp.moveaxis(a, axis, 0)
    t = t.reshape((N_MICROBATCH, t.shape[0] // N_MICROBATCH) + t.shape[1:])
    return _jnp.moveaxis(t, 1, axis + 1)


def setup_inputs(seed: int = 0) -> dict:
    inp = _fwd_setup_inputs(seed)
    key = _jax.random.fold_in(_jax.random.key(seed), 7919)
    shape, _ = _output_shape()
    out = dict(inp)
    out["loss_target"] = _jax.random.normal(_jax.random.fold_in(key, 0), shape, _jnp.float32)
    for i, name in enumerate(TWIN_WEIGHTS):
        w = inp[name].astype(_jnp.float32)
        if MOMENT_SCALE is None:
            s = _jnp.sqrt(_jnp.mean(_jnp.square(w)) + 1e-30)
        else:
            s = MOMENT_SCALE[name]
        km, kv = _jax.random.split(_jax.random.fold_in(key, i + 1))
        out[name] = w
        out["m_" + name] = s * _jax.random.normal(km, w.shape, _jnp.float32)
        out["v_" + name] = (s * s) * _jax.random.uniform(kv, w.shape, _jnp.float32, 0.5, 1.5)
    if N_MICROBATCH > 1:
        for name, axis in PER_EXAMPLE_BATCH_AXIS.items():
            out[name] = _to_microbatches(out[name], axis)
    return {'x': out['x'], 'mem': out['mem'], 'mix_norm': out['mix_norm'], 'w_in': out['w_in'], 'gdn_conv': out['gdn_conv'], 'gdn_a_log': out['gdn_a_log'], 'gdn_dt_bias': out['gdn_dt_bias'], 'gdn_norm': out['gdn_norm'], 'w_out': out['w_out'], 'xattn_norm': out['xattn_norm'], 'mem_norm': out['mem_norm'], 'w_xq': out['w_xq'], 'w_xkv': out['w_xkv'], 'w_xo': out['w_xo'], 'ffn_norm': out['ffn_norm'], 'w_up': out['w_up'], 'ffn_conv': out['ffn_conv'], 'ffn_conv_bias': out['ffn_conv_bias'], 'w_down': out['w_down'], 'final_norm': out['final_norm'], 'loss_target': out['loss_target'], 'm_mix_norm': out['m_mix_norm'], 'm_w_in': out['m_w_in'], 'm_gdn_conv': out['m_gdn_conv'], 'm_gdn_a_log': out['m_gdn_a_log'], 'm_gdn_dt_bias': out['m_gdn_dt_bias'], 'm_gdn_norm': out['m_gdn_norm'], 'm_w_out': out['m_w_out'], 'm_xattn_norm': out['m_xattn_norm'], 'm_mem_norm': out['m_mem_norm'], 'm_w_xq': out['m_w_xq'], 'm_w_xkv': out['m_w_xkv'], 'm_w_xo': out['m_w_xo'], 'm_ffn_norm': out['m_ffn_norm'], 'm_w_up': out['m_w_up'], 'm_ffn_conv': out['m_ffn_conv'], 'm_ffn_conv_bias': out['m_ffn_conv_bias'], 'm_w_down': out['m_w_down'], 'm_final_norm': out['m_final_norm'], 'v_mix_norm': out['v_mix_norm'], 'v_w_in': out['v_w_in'], 'v_gdn_conv': out['v_gdn_conv'], 'v_gdn_a_log': out['v_gdn_a_log'], 'v_gdn_dt_bias': out['v_gdn_dt_bias'], 'v_gdn_norm': out['v_gdn_norm'], 'v_w_out': out['v_w_out'], 'v_xattn_norm': out['v_xattn_norm'], 'v_mem_norm': out['v_mem_norm'], 'v_w_xq': out['v_w_xq'], 'v_w_xkv': out['v_w_xkv'], 'v_w_xo': out['v_w_xo'], 'v_ffn_norm': out['v_ffn_norm'], 'v_w_up': out['v_w_up'], 'v_ffn_conv': out['v_ffn_conv'], 'v_ffn_conv_bias': out['v_ffn_conv_bias'], 'v_w_down': out['v_w_down'], 'v_final_norm': out['v_final_norm']}


def _loss(weights, diff, rest, loss_target):
    with _jax.named_scope("forward"):
        args = {**rest, TWIN_DIFF_INPUT: diff, **{k: w.astype(_WEIGHT_DTYPES[k]) for k, w in weights.items()}}
        y = _forward(args)
    with _jax.named_scope("loss_head"):
        err = _jnp.square(y.astype(_jnp.float32) - loss_target)
        return 0.5 * _jnp.sum(_jnp.mean(err, axis=-1)) if err.ndim else 0.5 * err


def _adamw(w, g, m, v):
    m = ADAM_B1 * m + (1.0 - ADAM_B1) * g
    v = ADAM_B2 * v + (1.0 - ADAM_B2) * _jnp.square(g)
    m_hat = m / (1.0 - ADAM_B1 ** ADAM_STEP)
    v_hat = v / (1.0 - ADAM_B2 ** ADAM_STEP)
    delta = -ADAM_LR * (m_hat / (_jnp.sqrt(v_hat) + ADAM_EPS) + ADAM_WD * w)
    return delta, m, v


def reference(x, mem, mix_norm, w_in, gdn_conv, gdn_a_log, gdn_dt_bias, gdn_norm, w_out, xattn_norm, mem_norm, w_xq, w_xkv, w_xo, ffn_norm, w_up, ffn_conv, ffn_conv_bias, w_down, final_norm, loss_target, m_mix_norm, m_w_in, m_gdn_conv, m_gdn_a_log, m_gdn_dt_bias, m_gdn_norm, m_w_out, m_xattn_norm, m_mem_norm, m_w_xq, m_w_xkv, m_w_xo, m_ffn_norm, m_w_up, m_ffn_conv, m_ffn_conv_bias, m_w_down, m_final_norm, v_mix_norm, v_w_in, v_gdn_conv, v_gdn_a_log, v_gdn_dt_bias, v_gdn_norm, v_w_out, v_xattn_norm, v_mem_norm, v_w_xq, v_w_xkv, v_w_xo, v_ffn_norm, v_w_up, v_ffn_conv, v_ffn_conv_bias, v_w_down, v_final_norm):
    given = dict(x=x, mem=mem, mix_norm=mix_norm, w_in=w_in, gdn_conv=gdn_conv, gdn_a_log=gdn_a_log, gdn_dt_bias=gdn_dt_bias, gdn_norm=gdn_norm, w_out=w_out, xattn_norm=xattn_norm, mem_norm=mem_norm, w_xq=w_xq, w_xkv=w_xkv, w_xo=w_xo, ffn_norm=ffn_norm, w_up=w_up, ffn_conv=ffn_conv, ffn_conv_bias=ffn_conv_bias, w_down=w_down, final_norm=final_norm, loss_target=loss_target, m_mix_norm=m_mix_norm, m_w_in=m_w_in, m_gdn_conv=m_gdn_conv, m_gdn_a_log=m_gdn_a_log, m_gdn_dt_bias=m_gdn_dt_bias, m_gdn_norm=m_gdn_norm, m_w_out=m_w_out, m_xattn_norm=m_xattn_norm, m_mem_norm=m_mem_norm, m_w_xq=m_w_xq, m_w_xkv=m_w_xkv, m_w_xo=m_w_xo, m_ffn_norm=m_ffn_norm, m_w_up=m_w_up, m_ffn_conv=m_ffn_conv, m_ffn_conv_bias=m_ffn_conv_bias, m_w_down=m_w_down, m_final_norm=m_final_norm, v_mix_norm=v_mix_norm, v_w_in=v_w_in, v_gdn_conv=v_gdn_conv, v_gdn_a_log=v_gdn_a_log, v_gdn_dt_bias=v_gdn_dt_bias, v_gdn_norm=v_gdn_norm, v_w_out=v_w_out, v_xattn_norm=v_xattn_norm, v_mem_norm=v_mem_norm, v_w_xq=v_w_xq, v_w_xkv=v_w_xkv, v_w_xo=v_w_xo, v_ffn_norm=v_ffn_norm, v_w_up=v_w_up, v_ffn_conv=v_ffn_conv, v_ffn_conv_bias=v_ffn_conv_bias, v_w_down=v_w_down, v_final_norm=v_final_norm)
    weights = {n: given[n] for n in TWIN_WEIGHTS}
    shared = {n: given[n] for n in SHARED_INPUTS}
    per_example = {n: given[n] for n in ['x', 'mem']}
    grad_fn = _jax.value_and_grad(_loss, argnums=(0, 1))

    def one_microbatch(ex, loss_target):
        ex = dict(ex)
        diff = ex.pop(TWIN_DIFF_INPUT)
        return grad_fn(weights, diff, {**shared, **ex}, loss_target)

    if N_MICROBATCH == 1:
        loss, (grad_w, grad_x) = one_microbatch(per_example, given["loss_target"])
    else:
        def body(carry, xs):
            loss_sum, grad_sum = carry
            l_k, (gw_k, gx_k) = one_microbatch(xs[0], xs[1])
            with _jax.named_scope("update"):
                return (loss_sum + l_k, _jax.tree.map(_jnp.add, grad_sum, gw_k)), gx_k

        init = (_jnp.zeros((), _jnp.float32), _jax.tree.map(_jnp.zeros_like, weights))
        (loss, grad_w), grad_x = _jax.lax.scan(body, init, (per_example, given["loss_target"]))
    with _jax.named_scope("update"):
        delta_w, new_m, new_v = {}, {}, {}
        for n in TWIN_WEIGHTS:
            delta_w[n], new_m[n], new_v[n] = _adamw(weights[n], grad_w[n], given["m_" + n], given["v_" + n])
    return (loss, grad_x, *[grad_w[n] for n in TWIN_WEIGHTS], *[delta_w[n] for n in TWIN_WEIGHTS],
            *[new_m[n] for n in TWIN_WEIGHTS], *[new_v[n] for n in TWIN_WEIGHTS])
```

```python
import functools
import math

import jax
import jax.numpy as jnp
from jax import lax
from jax.experimental import pallas as pl
from jax.experimental.pallas import tpu as pltpu

F32 = jnp.float32
BF16 = jnp.bfloat16

HEAD_DIM = 128
CHUNK = 64
SB_BLOCK = 128
SHORT_CONV = 4
FFN_CONV = 3
X_HEADS = 4
EPS = 1e-6
LANES = 128
VMEM_LIMIT = 56 * 2**20

ADAM_LR, ADAM_B1, ADAM_B2, ADAM_EPS, ADAM_WD, ADAM_STEP = 0.001, 0.9, 0.999, 1e-08, 0.01, 10

HI = lax.Precision.HIGHEST


def _params(sem):
    return pltpu.CompilerParams(dimension_semantics=sem, vmem_limit_bytes=VMEM_LIMIT)


def _tile(n, want):
    if n <= want:
        return n
    t = (want // LANES) * LANES
    while t > LANES and n % t:
        t -= LANES
    assert n % t == 0, (n, want)
    return t


def _sigmoid(x):
    return 1.0 / (1.0 + jnp.exp(-x))


def _softplus(x):
    return jnp.maximum(x, 0.0) + jnp.log(1.0 + jnp.exp(-jnp.abs(x)))


def _matmul(a, b, *, name, ta=False, tb=False, out_dtype=F32, res=None, tm=512, tn=512, tk=2048,
            dims=None, a_spec=None, b_spec=None, o_spec=None, out_shape=None):
    if dims is None:
        M, K = (a.shape[1], a.shape[0]) if ta else a.shape
        N = b.shape[0] if tb else b.shape[1]
    else:
        M, N, K = dims
    tm, tn, tk = _tile(M, tm), _tile(N, tn), _tile(K, tk)
    nk = K // tk
    dn = (((0 if ta else 1,), (1 if tb else 0,)), ((), ()))

    def body(*refs):
        a_ref, b_ref = refs[0], refs[1]
        r_ref = refs[2] if res is not None else None
        o_ref = refs[3] if res is not None else refs[2]
        p = lax.dot_general(a_ref[...].astype(BF16), b_ref[...].astype(BF16), dn, preferred_element_type=F32)

        def finish(acc):
            if r_ref is not None:
                acc = acc + r_ref[...].astype(F32)
            o_ref[...] = acc.astype(o_ref.dtype)

        if nk == 1:
            finish(p)
        else:
            acc_ref = refs[-1]
            k = pl.program_id(2)

            @pl.when(k == 0)
            def _():
                acc_ref[...] = p

            @pl.when(jnp.logical_and(k > 0, k < nk - 1))
            def _():
                acc_ref[...] += p

            @pl.when(k == nk - 1)
            def _():
                finish(acc_ref[...] + p)

    if a_spec is None:
        a_spec = pl.BlockSpec((tk, tm), lambda i, j, k: (k, i)) if ta else pl.BlockSpec((tm, tk), lambda i, j, k: (i, k))
    if b_spec is None:
        b_spec = pl.BlockSpec((tn, tk), lambda i, j, k: (j, k)) if tb else pl.BlockSpec((tk, tn), lambda i, j, k: (k, j))
    if o_spec is None:
        o_spec = pl.BlockSpec((tm, tn), lambda i, j, k: (i, j))
    if out_shape is None:
        out_shape = jax.ShapeDtypeStruct((M, N), out_dtype)
    in_specs, args = [a_spec, b_spec], [a, b]
    if res is not None:
        in_specs.append(pl.BlockSpec((tm, tn), lambda i, j, k: (i, j)))
        args.append(res)
    return pl.pallas_call(
        body, grid=(M // tm, N // tn, nk), in_specs=in_specs, out_specs=o_spec, out_shape=out_shape,
        scratch_shapes=[pltpu.VMEM((tm, tn), F32)] if nk > 1 else [],
        compiler_params=_params(("parallel", "parallel", "arbitrary")), name=name)(*args)


def _rms_fwd(x, g, *, name):
    R, D = x.shape
    tr = _tile(R, 256)

    def body(x_ref, g_ref, o_ref):
        xv = x_ref[...]
        rstd = lax.rsqrt(jnp.mean(xv * xv, axis=-1, keepdims=True) + EPS)
        o_ref[...] = (xv * rstd * g_ref[...]).astype(o_ref.dtype)

    return pl.pallas_call(
        body, grid=(R // tr,), in_specs=[pl.BlockSpec((tr, D), lambda i: (i, 0)), pl.BlockSpec((1, D), lambda i: (0, 0))],
        out_specs=pl.BlockSpec((tr, D), lambda i: (i, 0)), out_shape=jax.ShapeDtypeStruct((R, D), BF16),
        compiler_params=_params(("parallel",)), name=name)(x, g)


def _rms_bwd(x, g, dh, dres, *, name):
    R, D = x.shape
    tr = _tile(R, 256)

    def body(*refs):
        if dres is None:
            x_ref, g_ref, dh_ref, dx_ref, dg_ref = refs
        else:
            x_ref, g_ref, dh_ref, dr_ref, dx_ref, dg_ref = refs
        xv = x_ref[...]
        dhv = dh_ref[...].astype(F32)
        rstd = lax.rsqrt(jnp.mean(xv * xv, axis=-1, keepdims=True) + EPS)
        xhat = xv * rstd
        t = dhv * g_ref[...]
        dx = rstd * (t - xhat * jnp.mean(t * xhat, axis=-1, keepdims=True))
        if dres is not None:
            dx = dx + dr_ref[...]
        dx_ref[...] = dx
        part = jnp.sum(dhv * xhat, axis=0, keepdims=True)

        @pl.when(pl.program_id(0) == 0)
        def _():
            dg_ref[...] = part

        @pl.when(pl.program_id(0) > 0)
        def _():
            dg_ref[...] += part

    row = pl.BlockSpec((tr, D), lambda i: (i, 0))
    vec = pl.BlockSpec((1, D), lambda i: (0, 0))
    in_specs = [row, vec, row] + ([row] if dres is not None else [])
    args = [x, g, dh] + ([dres] if dres is not None else [])
    return pl.pallas_call(
        body, grid=(R // tr,), in_specs=in_specs, out_specs=[row, vec],
        out_shape=[jax.ShapeDtypeStruct((R, D), F32), jax.ShapeDtypeStruct((1, D), F32)],
        compiler_params=_params(("arbitrary",)), name=name)(*args)


def _loss_head(x, g, tgt, *, name):
    R, D = x.shape
    tr = _tile(R, 256)

    def body(x_ref, g_ref, t_ref, l_ref, dx_ref, dg_ref):
        xv = x_ref[...]
        rstd = lax.rsqrt(jnp.mean(xv * xv, axis=-1, keepdims=True) + EPS)
        xhat = xv * rstd
        err = xhat * g_ref[...] - t_ref[...]
        dy = err * (1.0 / D)
        t = dy * g_ref[...]
        dx_ref[...] = rstd * (t - xhat * jnp.mean(t * xhat, axis=-1, keepdims=True))
        part = jnp.sum(dy * xhat, axis=0, keepdims=True)
        lpart = jnp.zeros((8, LANES), F32) + 0.5 * jnp.sum(jnp.mean(err * err, axis=-1, keepdims=True))

        @pl.when(pl.program_id(0) == 0)
        def _():
            dg_ref[...] = part
            l_ref[...] = lpart

        @pl.when(pl.program_id(0) > 0)
        def _():
            dg_ref[...] += part
            l_ref[...] += lpart

    row = pl.BlockSpec((tr, D), lambda i: (i, 0))
    vec = pl.BlockSpec((1, D), lambda i: (0, 0))
    return pl.pallas_call(
        body, grid=(R // tr,), in_specs=[row, vec, row],
        out_specs=[pl.BlockSpec((8, LANES), lambda i: (0, 0)), row, vec],
        out_shape=[jax.ShapeDtypeStruct((8, LANES), F32), jax.ShapeDtypeStruct((R, D), F32), jax.ShapeDtypeStruct((1, D), F32)],
        compiler_params=_params(("arbitrary",)), name=name)(x, g, tgt)


def _shift_down(x, s):
    if s == 0:
        return x
    row = lax.broadcasted_iota(jnp.int32, x.shape, 0)
    return jnp.where(row >= s, pltpu.roll(x, s, 0), 0.0)


def _shift_up(x, s):
    if s == 0:
        return x
    n = x.shape[0]
    row = lax.broadcasted_iota(jnp.int32, x.shape, 0)
    return jnp.where(row < n - s, pltpu.roll(x, n - s, 0), 0.0)


def _dwconv(x, w):
    k = w.shape[0]
    acc = x * w[k - 1:k, :]
    for i in range(k - 1):
        acc = acc + _shift_down(x, k - 1 - i) * w[i:i + 1, :]
    return acc


def _dwconv_bwd(x, w, dc):
    k = w.shape[0]
    dx = dc * w[k - 1:k, :]
    dws = []
    for i in range(k - 1):
        s = k - 1 - i
        dx = dx + _shift_up(dc, s) * w[i:i + 1, :]
        dws.append(jnp.sum(dc * _shift_down(x, s), axis=0, keepdims=True))
    dws.append(jnp.sum(dc * x, axis=0, keepdims=True))
    return dx, jnp.concatenate(dws, axis=0)


def _ffn_act_fwd(u, cw, cb, *, name):
    S, F2 = u.shape
    F = F2 // 2
    tc = _tile(F, 256)
    nb = F // tc

    def body(ug_ref, uu_ref, wg_ref, wu_ref, bg_ref, bu_ref, o_ref):
        cg = _dwconv(ug_ref[...], wg_ref[...]) + bg_ref[...]
        cu = _dwconv(uu_ref[...], wu_ref[...]) + bu_ref[...]
        o_ref[...] = (cg * _sigmoid(cg) * cu).astype(o_ref.dtype)

    blk = lambda r, off: pl.BlockSpec((r, tc), lambda j: (0, j + off))
    return pl.pallas_call(
        body, grid=(nb,), in_specs=[blk(S, 0), blk(S, nb), blk(FFN_CONV, 0), blk(FFN_CONV, nb), blk(1, 0), blk(1, nb)],
        out_specs=blk(S, 0), out_shape=jax.ShapeDtypeStruct((S, F), BF16),
        compiler_params=_params(("parallel",)), name=name)(u, u, cw, cw, cb, cb)


def _ffn_act_bwd(u, cw, cb, dact, *, name):
    S, F2 = u.shape
    F = F2 // 2
    tc = _tile(F, 256)
    nb = F // tc

    def body(ug_ref, uu_ref, wg_ref, wu_ref, bg_ref, bu_ref, da_ref, du_ref, dw_ref, db_ref):
        ug, uu = ug_ref[...], uu_ref[...]
        cg = _dwconv(ug, wg_ref[...]) + bg_ref[...]
        cu = _dwconv(uu, wu_ref[...]) + bu_ref[...]
        sg = _sigmoid(cg)
        da = da_ref[...].astype(F32)
        dcu = da * (cg * sg)
        dcg = da * cu * (sg * (1.0 + cg * (1.0 - sg)))
        dxg, dwg = _dwconv_bwd(ug, wg_ref[...], dcg)
        dxu, dwu = _dwconv_bwd(uu, wu_ref[...], dcu)
        du_ref[0] = dxg.astype(du_ref.dtype)
        du_ref[1] = dxu.astype(du_ref.dtype)
        dw_ref[0] = dwg
        dw_ref[1] = dwu
        db_ref[0] = jnp.sum(dcg, axis=0, keepdims=True)
        db_ref[1] = jnp.sum(dcu, axis=0, keepdims=True)

    blk = lambda r, off: pl.BlockSpec((r, tc), lambda j: (0, j + off))
    blk3 = lambda r: pl.BlockSpec((2, r, tc), lambda j: (0, 0, j))
    return pl.pallas_call(
        body, grid=(nb,),
        in_specs=[blk(S, 0), blk(S, nb), blk(FFN_CONV, 0), blk(FFN_CONV, nb), blk(1, 0), blk(1, nb), blk(S, 0)],
        out_specs=[blk3(S), blk3(FFN_CONV), blk3(1)],
        out_shape=[jax.ShapeDtypeStruct((2, S, F), BF16), jax.ShapeDtypeStruct((2, FFN_CONV, F), F32),
                   jax.ShapeDtypeStruct((2, 1, F), F32)],
        compiler_params=_params(("parallel",)), name=name)(u, u, cw, cw, cb, cb, dact)


def _xattn_fwd(q, kv, *, name):
    S, XW = q.shape
    M = kv.shape[0]
    nh = XW // HEAD_DIM
    tq = _tile(S, 512)
    scale = HEAD_DIM ** -0.5

    def body(q_ref, k_ref, v_ref, o_ref):
        z = lax.dot_general(q_ref[...], k_ref[...], (((1,), (1,)), ((), ())), preferred_element_type=F32) * scale
        e = jnp.exp(z - jnp.max(z, axis=-1, keepdims=True))
        p = e / jnp.sum(e, axis=-1, keepdims=True)
        o_ref[...] = jnp.dot(p.astype(BF16), v_ref[...], preferred_element_type=F32).astype(o_ref.dtype)

    return pl.pallas_call(
        body, grid=(nh, S // tq),
        in_specs=[pl.BlockSpec((tq, HEAD_DIM), lambda h, i: (i, h)), pl.BlockSpec((M, HEAD_DIM), lambda h, i: (0, h)),
                  pl.BlockSpec((M, HEAD_DIM), lambda h, i: (0, nh + h))],
        out_specs=pl.BlockSpec((tq, HEAD_DIM), lambda h, i: (i, h)), out_shape=jax.ShapeDtypeStruct((S, XW), BF16),
        compiler_params=_params(("parallel", "parallel")), name=name)(q, kv, kv)


def _xattn_bwd(q, kv, do, *, name):
    S, XW = q.shape
    M = kv.shape[0]
    nh = XW // HEAD_DIM
    tq = _tile(S, 512)
    scale = HEAD_DIM ** -0.5
    nt = (((1,), (1,)), ((), ()))
    tn = (((0,), (0,)), ((), ()))

    def body(q_ref, k_ref, v_ref, do_ref, dq_ref, dk_ref, dv_ref):
        qv, kvv, vv = q_ref[...], k_ref[...], v_ref[...]
        dov = do_ref[...].astype(BF16)
        z = lax.dot_general(qv, kvv, nt, preferred_element_type=F32) * scale
        e = jnp.exp(z - jnp.max(z, axis=-1, keepdims=True))
        p = e / jnp.sum(e, axis=-1, keepdims=True)
        dp = lax.dot_general(dov, vv, nt, preferred_element_type=F32)
        ds = (p * (dp - jnp.sum(dp * p, axis=-1, keepdims=True)) * scale).astype(BF16)
        dq_ref[...] = jnp.dot(ds, kvv, preferred_element_type=F32).astype(dq_ref.dtype)
        dk = lax.dot_general(ds, qv, tn, preferred_element_type=F32)
        dv = lax.dot_general(p.astype(BF16), dov, tn, preferred_element_type=F32)

        @pl.when(pl.program_id(1) == 0)
        def _():
            dk_ref[...] = dk
            dv_ref[...] = dv

        @pl.when(pl.program_id(1) > 0)
        def _():
            dk_ref[...] += dk
            dv_ref[...] += dv

    qs = pl.BlockSpec((tq, HEAD_DIM), lambda h, i: (i, h))
    ms = pl.BlockSpec((M, HEAD_DIM), lambda h, i: (0, h))
    return pl.pallas_call(
        body, grid=(nh, S // tq),
        in_specs=[qs, ms, pl.BlockSpec((M, HEAD_DIM), lambda h, i: (0, nh + h)), qs],
        out_specs=[qs, ms, ms],
        out_shape=[jax.ShapeDtypeStruct((S, XW), BF16), jax.ShapeDtypeStruct((M, XW), F32), jax.ShapeDtypeStruct((M, XW), F32)],
        compiler_params=_params(("parallel", "arbitrary")), name=name)(q, kv, kv, do)


_NN = (((1,), (0,)), ((), ()))
_NT = (((1,), (1,)), ((), ()))
_TN = (((0,), (0,)), ((), ()))


def _dot(a, b, dn=_NN):
    return lax.dot_general(a.astype(BF16), b.astype(BF16), dn, preferred_element_type=F32)


def _dot_hi(a, b, dn=_NN):
    return lax.dot_general(a, b, dn, preferred_element_type=F32, precision=HI)


def _dot_split(a, b01, dn=_NN):
    hi = a.astype(BF16)
    lo = (a - hi.astype(F32)).astype(BF16)
    return (lax.dot_general(hi, b01, dn, preferred_element_type=F32)
            + lax.dot_general(lo, b01, dn, preferred_element_type=F32))


def _sb_fwd(proj, nh, *, name):
    S = proj.shape[0]
    B = SB_BLOCK
    nq = S // B
    scale = HEAD_DIM ** -0.5

    def body(q_ref, k_ref, v_ref, o_ref, tot_ref):
        i = pl.program_id(1)
        q = q_ref[...].astype(BF16)
        row = lax.broadcasted_iota(jnp.int32, (B, B), 0)
        col = lax.broadcasted_iota(jnp.int32, (B, B), 1)
        after = (row > col).astype(BF16)

        def step(t, carry):
            acc, out = carry
            j = i - t
            off = pl.multiple_of(j * B, B)
            kb = k_ref[pl.ds(off, B), :].astype(BF16)
            vb = v_ref[pl.ds(off, B), :].astype(BF16)
            z = lax.dot_general(q, kb, _NT, preferred_element_type=F32) * scale
            valid = jnp.logical_or(col < row, j < i)
            ls = jnp.where(valid, -_softplus(z), 0.0)
            later = _dot_split(ls, after) + acc
            w = jnp.where(valid, jnp.exp(ls + z + later), 0.0)
            out = out + jnp.dot(w.astype(BF16), vb, preferred_element_type=F32)
            return acc + jnp.sum(ls, axis=1, keepdims=True), out

        acc, out = lax.fori_loop(0, i + 1, step, (jnp.zeros((B, 1), F32), jnp.zeros((B, HEAD_DIM), F32)))
        o_ref[...] = out.astype(o_ref.dtype)
        tot_ref[...] = acc

    return pl.pallas_call(
        body, grid=(nh, nq),
        in_specs=[pl.BlockSpec((B, HEAD_DIM), lambda h, i: (i, h)),
                  pl.BlockSpec((S, HEAD_DIM), lambda h, i: (0, nh + h)),
                  pl.BlockSpec((S, HEAD_DIM), lambda h, i: (0, 2 * nh + h))],
        out_specs=[pl.BlockSpec((B, HEAD_DIM), lambda h, i: (i, h)), pl.BlockSpec((None, B, 1), lambda h, i: (h, i, 0))],
        out_shape=[jax.ShapeDtypeStruct((S, nh * HEAD_DIM), BF16), jax.ShapeDtypeStruct((nh, S, 1), F32)],
        compiler_params=_params(("parallel", "parallel")), name=name)(proj, proj, proj)


def _sb_bwd(proj, tot, dmix, nh, *, name):
    S = proj.shape[0]
    B = SB_BLOCK
    nq = S // B
    scale = HEAD_DIM ** -0.5

    def body(q_ref, k_ref, v_ref, tot_ref, do_ref, dq_ref, dk_ref, dv_ref, dk_acc, dv_acc):
        i = pl.program_id(1)

        @pl.when(i == 0)
        def _():
            dk_acc[...] = jnp.zeros_like(dk_acc)
            dv_acc[...] = jnp.zeros_like(dv_acc)

        q = q_ref[...].astype(BF16)
        do = do_ref[...].astype(BF16)
        tot = tot_ref[...]
        row = lax.broadcasted_iota(jnp.int32, (B, B), 0)
        col = lax.broadcasted_iota(jnp.int32, (B, B), 1)
        after = (row > col).astype(BF16)
        before = (row < col).astype(BF16)

        def step(j, carry):
            pre, g_sum, dq = carry
            off = pl.multiple_of(j * B, B)
            kb = k_ref[pl.ds(off, B), :].astype(BF16)
            vb = v_ref[pl.ds(off, B), :].astype(BF16)
            z = lax.dot_general(q, kb, _NT, preferred_element_type=F32) * scale
            valid = jnp.logical_or(col < row, j < i)
            ls = jnp.where(valid, -_softplus(z), 0.0)
            lb = ls + z
            rs = jnp.sum(ls, axis=1, keepdims=True)
            later = _dot_split(ls, after) + (tot - pre - rs)
            w = jnp.where(valid, jnp.exp(lb + later), 0.0)
            g = lax.dot_general(do, vb, _NT, preferred_element_type=F32) * w
            dls = _dot_split(g, before) + g_sum
            sig = jnp.exp(lb)
            dz = (jnp.where(valid, g * (1.0 - sig) - dls * sig, 0.0) * scale).astype(BF16)
            dq = dq + jnp.dot(dz, kb, preferred_element_type=F32)
            dk_acc[pl.ds(off, B), :] += lax.dot_general(dz, q, _TN, preferred_element_type=F32)
            dv_acc[pl.ds(off, B), :] += lax.dot_general(w.astype(BF16), do, _TN, preferred_element_type=F32)
            return pre + rs, g_sum + jnp.sum(g, axis=1, keepdims=True), dq

        zero = jnp.zeros((B, 1), F32)
        _, _, dq = lax.fori_loop(0, i + 1, step, (zero, zero, jnp.zeros((B, HEAD_DIM), F32)))
        dq_ref[...] = dq.astype(dq_ref.dtype)

        @pl.when(i == nq - 1)
        def _():
            dk_ref[...] = dk_acc[...].astype(dk_ref.dtype)
            dv_ref[...] = dv_acc[...].astype(dv_ref.dtype)

    qs = pl.BlockSpec((B, HEAD_DIM), lambda h, i: (i, h))
    full = pl.BlockSpec((S, HEAD_DIM), lambda h, i: (0, h))
    o = jax.ShapeDtypeStruct((S, nh * HEAD_DIM), BF16)
    return pl.pallas_call(
        body, grid=(nh, nq),
        in_specs=[qs, pl.BlockSpec((S, HEAD_DIM), lambda h, i: (0, nh + h)),
                  pl.BlockSpec((S, HEAD_DIM), lambda h, i: (0, 2 * nh + h)),
                  pl.BlockSpec((None, B, 1), lambda h, i: (h, i, 0)), qs],
        out_specs=[qs, full, full], out_shape=[o, o, o],
        scratch_shapes=[pltpu.VMEM((S, HEAD_DIM), F32), pltpu.VMEM((S, HEAD_DIM), F32)],
        compiler_params=_params(("parallel", "arbitrary")), name=name)(proj, proj, proj, tot, dmix)


def _gdn_qkv_fwd(proj, conv_w, nh, *, name):
    S = proj.shape[0]
    GW = nh * HEAD_DIM
    scale = HEAD_DIM ** -0.5

    def body(x_ref, w_ref, o_ref):
        sec = pl.program_id(0) // nh
        c = _dwconv(x_ref[...], w_ref[...])
        s = c * _sigmoid(c)
        r = lax.rsqrt(jnp.sum(s * s, axis=1, keepdims=True) + EPS)
        fac = jnp.where(sec == 0, scale, 1.0)
        o_ref[...] = jnp.where(sec == 2, s, s * (r * fac))

    return pl.pallas_call(
        body, grid=(3 * nh,),
        in_specs=[pl.BlockSpec((S, HEAD_DIM), lambda j: (0, 3 * nh + j)), pl.BlockSpec((SHORT_CONV, HEAD_DIM), lambda j: (0, j))],
        out_specs=pl.BlockSpec((None, S, HEAD_DIM), lambda j: (j // nh, 0, j % nh)),
        out_shape=jax.ShapeDtypeStruct((3, S, GW), F32),
        compiler_params=_params(("parallel",)), name=name)(proj, conv_w)


def _gdn_qkv_bwd(proj, conv_w, dqkv, nh, *, name):
    S = proj.shape[0]
    GW = nh * HEAD_DIM
    scale = HEAD_DIM ** -0.5

    def body(x_ref, w_ref, d_ref, dx_ref, dw_ref):
        sec = pl.program_id(0) // nh
        x, w = x_ref[...], w_ref[...]
        c = _dwconv(x, w)
        sg = _sigmoid(c)
        s = c * sg
        r = lax.rsqrt(jnp.sum(s * s, axis=1, keepdims=True) + EPS)
        sh = s * r
        d = d_ref[...]
        fac = jnp.where(sec == 0, scale, 1.0)
        dn = (r * fac) * (d - sh * jnp.sum(d * sh, axis=1, keepdims=True))
        ds = jnp.where(sec == 2, d, dn)
        dx, dw = _dwconv_bwd(x, w, ds * (sg * (1.0 + c * (1.0 - sg))))
        dx_ref[...] = dx.astype(dx_ref.dtype)
        dw_ref[...] = dw

    return pl.pallas_call(
        body, grid=(3 * nh,),
        in_specs=[pl.BlockSpec((S, HEAD_DIM), lambda j: (0, 3 * nh + j)), pl.BlockSpec((SHORT_CONV, HEAD_DIM), lambda j: (0, j)),
                  pl.BlockSpec((None, S, HEAD_DIM), lambda j: (j // nh, 0, j % nh))],
        out_specs=[pl.BlockSpec((S, HEAD_DIM), lambda j: (0, j)), pl.BlockSpec((SHORT_CONV, HEAD_DIM), lambda j: (0, j))],
        out_shape=[jax.ShapeDtypeStruct((S, 3 * GW), BF16), jax.ShapeDtypeStruct((SHORT_CONV, 3 * GW), F32)],
        compiler_params=_params(("parallel",)), name=name)(proj, conv_w, dqkv)


def _gdn_gates_fwd(proj, ab, nh, *, name):
    S = proj.shape[0]
    C = CHUNK

    def body(x_ref, ab_ref, o_ref):
        ri = lax.broadcasted_iota(jnp.int32, (C, C), 0)
        ci = lax.broadcasted_iota(jnp.int32, (C, C), 1)
        ltri = (ri >= ci).astype(F32)
        lane = lax.broadcasted_iota(jnp.int32, (C, LANES), 1)
        a_coef = -jnp.exp(ab_ref[0:1, :])
        dt = ab_ref[1:2, :]

        def chunk(n, _):
            rows = pl.ds(pl.multiple_of(n * C, C), C)
            x = x_ref[rows, :]
            beta = _sigmoid(x)
            g = jnp.where(jnp.logical_and(lane >= nh, lane < 2 * nh), a_coef * _softplus(x + dt), 0.0)
            gc = _dot_hi(ltri, pltpu.roll(g, nh, 1))
            o_ref[rows, :] = jnp.where(lane < nh, beta, g) + gc
            return 0

        lax.fori_loop(0, S // C, chunk, 0)

    return pl.pallas_call(
        body, grid=(1,),
        in_specs=[pl.BlockSpec((S, LANES), lambda i: (0, 7 * nh)), pl.BlockSpec((2, LANES), lambda i: (0, 0))],
        out_specs=pl.BlockSpec((S, LANES), lambda i: (0, 0)), out_shape=jax.ShapeDtypeStruct((S, LANES), F32),
        compiler_params=_params(("arbitrary",)), name=name)(proj, ab)


def _gdn_gates_bwd(proj, ab, dgt, nh, *, name):
    S = proj.shape[0]
    C = CHUNK

    def body(x_ref, ab_ref, d_ref, dx_ref, dab_ref):
        ri = lax.broadcasted_iota(jnp.int32, (C, C), 0)
        ci = lax.broadcasted_iota(jnp.int32, (C, C), 1)
        utri = (ri <= ci).astype(F32)
        lane = lax.broadcasted_iota(jnp.int32, (C, LANES), 1)
        is_b = lane < nh
        is_a = jnp.logical_and(lane >= nh, lane < 2 * nh)
        a_coef = -jnp.exp(ab_ref[0:1, :])
        dt = ab_ref[1:2, :]

        def chunk(n, carry):
            da_log, ddt = carry
            rows = pl.ds(pl.multiple_of(n * C, C), C)
            x = x_ref[rows, :]
            d = d_ref[rows, :]
            beta = _sigmoid(x)
            dg = pltpu.roll(_dot_hi(utri, jnp.where(lane >= 2 * nh, d, 0.0)), LANES - nh, 1)
            dg = jnp.where(is_a, dg, 0.0)
            dxa = dg * a_coef * _sigmoid(x + dt)
            dxb = jnp.where(is_b, d * beta * (1.0 - beta), 0.0)
            dx_ref[rows, :] = (dxa + dxb).astype(dx_ref.dtype)
            da_log = da_log + jnp.sum(dg * a_coef * _softplus(x + dt), axis=0, keepdims=True)
            return da_log, ddt + jnp.sum(dxa, axis=0, keepdims=True)

        zero = jnp.zeros((1, LANES), F32)
        da_log, ddt = lax.fori_loop(0, S // C, chunk, (zero, zero))
        dab_ref[0:1, :] = da_log
        dab_ref[1:2, :] = ddt

    return pl.pallas_call(
        body, grid=(1,),
        in_specs=[pl.BlockSpec((S, LANES), lambda i: (0, 7 * nh)), pl.BlockSpec((2, LANES), lambda i: (0, 0)),
                  pl.BlockSpec((S, LANES), lambda i: (0, 0))],
        out_specs=[pl.BlockSpec((S, LANES), lambda i: (0, 0)), pl.BlockSpec((2, LANES), lambda i: (0, 0))],
        out_shape=[jax.ShapeDtypeStruct((S, LANES), BF16), jax.ShapeDtypeStruct((2, LANES), F32)],
        compiler_params=_params(("arbitrary",)), name=name)(proj, ab, dgt)


def _unit_lower_inverse(lmat):
    C = lmat.shape[0]
    ri = lax.broadcasted_iota(jnp.int32, (C, C), 0)
    ci = lax.broadcasted_iota(jnp.int32, (C, C), 1)
    nmat = -lmat
    p = jnp.where(ri == ci, 1.0, 0.0) + nmat
    for _ in range(int(math.log2(C)) - 1):
        nmat = _dot_hi(nmat, nmat)
        p = p + _dot_hi(p, nmat)
    return p


def _gdn_chunk_common(qkv_ref, beta_ref, gc_ref, gr_ref):
    C = CHUNK
    q, k, v = qkv_ref[0], qkv_ref[1], qkv_ref[2]
    beta, gc = beta_ref[...], gc_ref[...]
    ri = lax.broadcasted_iota(jnp.int32, (C, C), 0)
    ci = lax.broadcasted_iota(jnp.int32, (C, C), 1)
    incl, strict = ri >= ci, ri > ci
    decay = jnp.where(incl, jnp.exp(jnp.where(incl, gc[:, :C] - gr_ref[...], 0.0)), 0.0)
    egc = jnp.exp(gc)
    kb, vb = k * beta, v * beta
    lmat = jnp.where(strict, _dot(kb, k, _NT) * decay, 0.0)
    tinv = _unit_lower_inverse(lmat)
    kbg = kb * egc
    u = _dot(tinv, vb)
    w = _dot(tinv, kbg)
    amat = _dot(q, k, _NT) * decay
    glast = gc[C - 1:C, :]
    ekt = jnp.exp(glast - gc)
    return dict(q=q, k=k, v=v, beta=beta, decay=decay, egc=egc, kb=kb, vb=vb, lmat=lmat, tinv=tinv, kbg=kbg, u=u, w=w,
                amat=amat, qd=q * egc, ekt=ekt, kt=k * ekt, cd=jnp.exp(glast), strict=strict, incl=incl)


def _gdn_chunk_specs(nh, nc, rev):
    C = CHUNK
    cn = (lambda n: nc - 1 - n) if rev else (lambda n: n)
    return [pl.BlockSpec((3, C, HEAD_DIM), lambda h, n: (0, cn(n), h)),
            pl.BlockSpec((None, C, LANES), lambda h, n: (h, cn(n), 0)),
            pl.BlockSpec((None, C, LANES), lambda h, n: (h, cn(n), 0)),
            pl.BlockSpec((None, None, 1, C), lambda h, n: (h, cn(n), 0, 0))]


def _gdn_chunk_fwd(qkv, beta_bc, gc_bc, gc_row, *, name):
    _, S, GW = qkv.shape
    nh, C = GW // HEAD_DIM, CHUNK
    nc = S // C

    def body(qkv_ref, beta_ref, gc_ref, gr_ref, o_ref, st_ref, state):
        @pl.when(pl.program_id(1) == 0)
        def _():
            state[...] = jnp.zeros_like(state)

        m = _gdn_chunk_common(qkv_ref, beta_ref, gc_ref, gr_ref)
        s0 = state[...]
        st_ref[...] = s0
        v_new = m["u"] - _dot(m["w"], s0)
        o_ref[...] = _dot(m["qd"], s0) + _dot(m["amat"], v_new)
        state[...] = s0 * m["cd"] + _dot(m["kt"], v_new, _TN)

    return pl.pallas_call(
        body, grid=(nh, nc), in_specs=_gdn_chunk_specs(nh, nc, False),
        out_specs=[pl.BlockSpec((C, HEAD_DIM), lambda h, n: (n, h)),
                   pl.BlockSpec((None, None, HEAD_DIM, HEAD_DIM), lambda h, n: (h, n, 0, 0))],
        out_shape=[jax.ShapeDtypeStruct((S, GW), F32), jax.ShapeDtypeStruct((nh, nc, HEAD_DIM, HEAD_DIM), F32)],
        scratch_shapes=[pltpu.VMEM((HEAD_DIM, HEAD_DIM), F32)],
        compiler_params=_params(("parallel", "arbitrary")), name=name)(qkv, beta_bc, gc_bc, gc_row)


def _gdn_chunk_bwd(qkv, beta_bc, gc_bc, gc_row, states, do, *, name):
    _, S, GW = qkv.shape
    nh, C = GW // HEAD_DIM, CHUNK
    nc = S // C

    def body(qkv_ref, beta_ref, gc_ref, gr_ref, st_ref, do_ref, dqkv_ref, dbeta_ref, dgc_ref, dstate):
        @pl.when(pl.program_id(1) == 0)
        def _():
            dstate[...] = jnp.zeros_like(dstate)

        m = _gdn_chunk_common(qkv_ref, beta_ref, gc_ref, gr_ref)
        q, k, v, beta, decay, egc = m["q"], m["k"], m["v"], m["beta"], m["decay"], m["egc"]
        tinv, kt, cd = m["tinv"], m["kt"], m["cd"]
        s0, dsn, dout = st_ref[...], dstate[...], do_ref[...]
        v_new = m["u"] - _dot(m["w"], s0)
        dvn = _dot(m["amat"], dout, _TN) + _dot(kt, dsn)
        dqd = _dot(dout, s0, _NT)
        damat = jnp.where(m["incl"], _dot(dout, v_new, _NT), 0.0)
        dkt = _dot(v_new, dsn, _NT)
        dcd = jnp.sum(dsn * s0)
        dstate[...] = _dot(m["qd"], dout, _TN) + dsn * cd - _dot(m["w"], dvn, _TN)
        dw = -_dot(dvn, s0, _NT)
        dvb = _dot(tinv, dvn, _TN)
        dkbg = _dot(tinv, dw, _TN)
        dtinv = _dot(dvn, m["vb"], _NT) + _dot(dw, m["kbg"], _NT)
        dl = jnp.where(m["strict"], -_dot_hi(_dot_hi(tinv, dtinv, _TN), tinv, _NT), 0.0)
        dkk = dl * decay
        dqk = damat * decay
        dkb = _dot(dkk, k) + dkbg * egc
        dk = _dot(dkk, m["kb"], _TN) + _dot(dqk, q, _TN) + dkt * m["ekt"] + dkb * beta
        dq = _dot(dqk, k) + dqd * egc
        mm = dl * m["lmat"] + damat * m["amat"]
        ones = jnp.ones((C, LANES), F32)
        rk = jnp.sum(dkt * kt, axis=1, keepdims=True)
        dgc = (_dot_hi(mm, ones) - _dot_hi(mm, ones, _TN) + jnp.sum(dqd * m["qd"], axis=1, keepdims=True) - rk
               + jnp.sum(dkbg * m["kbg"], axis=1, keepdims=True))
        dglast = jnp.sum(rk) + dcd * cd[0:1, 0:1]
        rowi = lax.broadcasted_iota(jnp.int32, (C, LANES), 0)
        dgc_ref[...] = dgc + jnp.where(rowi == C - 1, dglast, 0.0)
        dbeta_ref[...] = jnp.zeros((C, LANES), F32) + (jnp.sum(dkb * k, axis=1, keepdims=True)
                                                       + jnp.sum(dvb * v, axis=1, keepdims=True))
        dqkv_ref[0] = dq
        dqkv_ref[1] = dk
        dqkv_ref[2] = dvb * beta

    rc = lambda n: nc - 1 - n
    col = pl.BlockSpec((None, C, LANES), lambda h, n: (h, rc(n), 0))
    return pl.pallas_call(
        body, grid=(nh, nc),
        in_specs=_gdn_chunk_specs(nh, nc, True) + [
            pl.BlockSpec((None, None, HEAD_DIM, HEAD_DIM), lambda h, n: (h, rc(n), 0, 0)),
            pl.BlockSpec((C, HEAD_DIM), lambda h, n: (rc(n), h))],
        out_specs=[pl.BlockSpec((3, C, HEAD_DIM), lambda h, n: (0, rc(n), h)), col, col],
        out_shape=[jax.ShapeDtypeStruct((3, S, GW), F32), jax.ShapeDtypeStruct((nh, S, LANES), F32),
                   jax.ShapeDtypeStruct((nh, S, LANES), F32)],
        scratch_shapes=[pltpu.VMEM((HEAD_DIM, HEAD_DIM), F32)],
        compiler_params=_params(("parallel", "arbitrary")), name=name)(qkv, beta_bc, gc_bc, gc_row, states, do)


def _gdn_post_fwd(o, proj, ng, nh, *, name):
    S, GW = o.shape

    def body(o_ref, z_ref, g_ref, y_ref):
        ov, z = o_ref[...], z_ref[...]
        rstd = lax.rsqrt(jnp.mean(ov * ov, axis=-1, keepdims=True) + EPS)
        y_ref[...] = (ov * rstd * g_ref[...] * (z * _sigmoid(z))).astype(y_ref.dtype)

    blk = pl.BlockSpec((S, HEAD_DIM), lambda h: (0, h))
    return pl.pallas_call(
        body, grid=(nh,), in_specs=[blk, pl.BlockSpec((S, HEAD_DIM), lambda h: (0, 6 * nh + h)), pl.BlockSpec((1, HEAD_DIM), lambda h: (0, 0))],
        out_specs=blk, out_shape=jax.ShapeDtypeStruct((S, GW), BF16),
        compiler_params=_params(("parallel",)), name=name)(o, proj, ng)


def _gdn_post_bwd(o, proj, ng, dmix, nh, *, name):
    S, GW = o.shape

    def body(o_ref, z_ref, g_ref, d_ref, do_ref, dz_ref, dg_ref):
        ov, z, d = o_ref[...], z_ref[...], d_ref[...].astype(F32)
        rstd = lax.rsqrt(jnp.mean(ov * ov, axis=-1, keepdims=True) + EPS)
        oh = ov * rstd
        sz = _sigmoid(z)
        dy = d * (z * sz)
        dz_ref[...] = (d * (oh * g_ref[...]) * (sz * (1.0 + z * (1.0 - sz)))).astype(dz_ref.dtype)
        t = dy * g_ref[...]
        do_ref[...] = rstd * (t - oh * jnp.mean(t * oh, axis=-1, keepdims=True))
        part = jnp.sum(dy * oh, axis=0, keepdims=True)

        @pl.when(pl.program_id(0) == 0)
        def _():
            dg_ref[...] = part

        @pl.when(pl.program_id(0) > 0)
        def _():
            dg_ref[...] += part

    blk = pl.BlockSpec((S, HEAD_DIM), lambda h: (0, h))
    vec = pl.BlockSpec((1, HEAD_DIM), lambda h: (0, 0))
    return pl.pallas_call(
        body, grid=(nh,),
        in_specs=[blk, pl.BlockSpec((S, HEAD_DIM), lambda h: (0, 6 * nh + h)), vec, pl.BlockSpec((S, HEAD_DIM), lambda h: (0, nh + h))],
        out_specs=[blk, blk, vec],
        out_shape=[jax.ShapeDtypeStruct((S, GW), F32), jax.ShapeDtypeStruct((S, GW), BF16), jax.ShapeDtypeStruct((1, HEAD_DIM), F32)],
        compiler_params=_params(("arbitrary",)), name=name)(o, proj, ng, dmix)


def _gdn_forward(proj, conv_w, ab, ng, nh, tag):
    S = proj.shape[0]
    nc = S // CHUNK
    qkv = _gdn_qkv_fwd(proj, conv_w, nh, name=f"gdn_qkv_fwd{tag}")
    gates = _gdn_gates_fwd(proj, ab, nh, name=f"gdn_gates_fwd{tag}")
    beta_t = gates[:, 0:nh].T
    gc_t = gates[:, 2 * nh:3 * nh].T
    beta_bc = jnp.broadcast_to(beta_t[:, :, None], (nh, S, LANES))
    gc_bc = jnp.broadcast_to(gc_t[:, :, None], (nh, S, LANES))
    gc_row = gc_t.reshape(nh, nc, 1, CHUNK)
    o, states = _gdn_chunk_fwd(qkv, beta_bc, gc_bc, gc_row, name=f"gdn_chunk_fwd{tag}")
    y = _gdn_post_fwd(o, proj, ng, nh, name=f"gdn_post_fwd{tag}")
    return y, (qkv, beta_bc, gc_bc, gc_row, states, o)


def _gdn_backward(proj, conv_w, ab, ng, saved, dmix, nh, tag):
    qkv, beta_bc, gc_bc, gc_row, states, o = saved
    S = proj.shape[0]
    do, dz, dng = _gdn_post_bwd(o, proj, ng, dmix, nh, name=f"gdn_post_bwd{tag}")
    dqkv, dbeta_bc, dgc_bc = _gdn_chunk_bwd(qkv, beta_bc, gc_bc, gc_row, states, do, name=f"gdn_chunk_bwd{tag}")
    dx_qkv, dconv = _gdn_qkv_bwd(proj, conv_w, dqkv, nh, name=f"gdn_qkv_bwd{tag}")
    dgt = jnp.concatenate([dbeta_bc[:, :, 0].T, jnp.zeros((S, nh), F32), dgc_bc[:, :, 0].T,
                           jnp.zeros((S, LANES - 3 * nh), F32)], axis=1)
    dx_g, dab = _gdn_gates_bwd(proj, ab, dgt, nh, name=f"gdn_gates_bwd{tag}")
    return dx_qkv, dz, dx_g, dconv, dab, dng


def _row_tile(r, cap=128):
    t = cap
    while r % t:
        t //= 2
    assert t >= 8, r
    return t


def _adamw(w, g, m, v, *, name):
    L, r, c = w.shape
    tr = _row_tile(r)

    def body(w_ref, g_ref, m_ref, v_ref, d_ref, m2_ref, v2_ref):
        gv = g_ref[...]
        m2 = ADAM_B1 * m_ref[...] + (1.0 - ADAM_B1) * gv
        v2 = ADAM_B2 * v_ref[...] + (1.0 - ADAM_B2) * (gv * gv)
        m_hat = m2 / (1.0 - ADAM_B1 ** ADAM_STEP)
        v_hat = v2 / (1.0 - ADAM_B2 ** ADAM_STEP)
        d_ref[...] = -ADAM_LR * (m_hat / (jnp.sqrt(v_hat) + ADAM_EPS) + ADAM_WD * w_ref[...])
        m2_ref[...] = m2
        v2_ref[...] = v2

    blk = pl.BlockSpec((None, tr, c), lambda l, i: (l, i, 0))
    o = jax.ShapeDtypeStruct(w.shape, F32)
    return pl.pallas_call(
        body, grid=(L, r // tr), in_specs=[blk] * 4, out_specs=[blk] * 3, out_shape=[o, o, o],
        compiler_params=_params(("parallel", "parallel")), name=name)(w, g, m, v)


def _sum_half(g, rbuf, cvec, *, name):
    _, r, c = g.shape
    h = r // 2
    tr = _row_tile(h)
    nb = h // tr

    def body(c_ref, g_ref, r_ref, o_ref):
        o_ref[...] = (g_ref[...] + r_ref[...]).astype(o_ref.dtype)

    blk = pl.BlockSpec((None, tr, c), lambda s, i, c_ref: (s, i, 0))
    return pl.pallas_call(
        body,
        grid_spec=pltpu.PrefetchScalarGridSpec(
            num_scalar_prefetch=1, grid=(4, nb),
            in_specs=[pl.BlockSpec((None, tr, c), lambda s, i, c_ref: (s, c_ref[0] * nb + i, 0)), blk], out_specs=blk),
        out_shape=jax.ShapeDtypeStruct((4, h, c), BF16),
        compiler_params=_params(("parallel", "parallel")), name=name)(cvec, g, rbuf)


def _sum_chips(rb, prev, l, nl, *, name):
    _, h, c = rb.shape
    tr = _row_tile(h)

    def body(*refs):
        r_ref, o_ref = refs[0], refs[-1]
        acc = r_ref[0].astype(F32)
        for s in range(1, 4):
            acc = acc + r_ref[s].astype(F32)
        o_ref[...] = acc

    in_specs = [pl.BlockSpec((4, tr, c), lambda i: (0, i, 0))]
    args = [rb]
    if prev is not None:
        in_specs.append(pl.BlockSpec(memory_space=pltpu.HBM))
        args.append(prev)
    return pl.pallas_call(
        body, grid=(h // tr,), in_specs=in_specs, out_specs=pl.BlockSpec((None, tr, c), lambda i: (l, i, 0)),
        out_shape=jax.ShapeDtypeStruct((nl, h, c), F32), input_output_aliases={1: 0} if prev is not None else {},
        compiler_params=_params(("parallel",)), name=name)(*args)


_MESH = pl.DeviceIdType.MESH
_HBM = pl.BlockSpec(memory_space=pltpu.HBM)


def _place():
    x, y, c = lax.axis_index("x"), lax.axis_index("y"), lax.axis_index("c")
    return x, y, c, [(1 - x, y), (x, 1 - y), (1 - x, 1 - y)]


def _gather_weights(ws, l, *, name):
    n = len(ws)

    def body(*refs):
        w_refs, o_refs = refs[:n], refs[n:2 * n]
        ici_s, ici_r, fwd_s, fwd_r, loc = refs[2 * n:]
        x, y, c, chips = _place()
        k = 2 * x + y
        started, local = [], []
        for a in range(n):
            h = w_refs[a].shape[1] // 2
            mine = pltpu.make_async_copy(w_refs[a].at[l], o_refs[a].at[k], loc.at[a])
            mine.start()
            local.append(mine)
            for j, (cx, cy) in enumerate(chips):
                cp = pltpu.make_async_remote_copy(
                    src_ref=w_refs[a].at[l, pl.ds(c * h, h), :], dst_ref=o_refs[a].at[k, pl.ds(c * h, h), :],
                    send_sem=ici_s.at[3 * a + j], recv_sem=ici_r.at[3 * a + j], device_id=(cx, cy, c), device_id_type=_MESH)
                cp.start()
                started.append(cp)
        for a in range(n):
            h = w_refs[a].shape[1] // 2
            for j, (cx, cy) in enumerate(chips):
                landed = o_refs[a].at[2 * cx + cy, pl.ds(c * h, h), :]
                pltpu.make_async_remote_copy(
                    src_ref=landed, dst_ref=landed, send_sem=ici_s.at[3 * a + j], recv_sem=ici_r.at[3 * a + j],
                    device_id=(x, y, c), device_id_type=_MESH).wait_recv()
                fw = pltpu.make_async_remote_copy(
                    src_ref=landed, dst_ref=landed, send_sem=fwd_s.at[3 * a + j], recv_sem=fwd_r.at[3 * a + j],
                    device_id=(x, y, 1 - c), device_id_type=_MESH)
                fw.start()
                started.append(fw)
        for a in range(n):
            h = w_refs[a].shape[1] // 2
            for j, (cx, cy) in enumerate(chips):
                other = o_refs[a].at[2 * cx + cy, pl.ds((1 - c) * h, h), :]
                pltpu.make_async_remote_copy(
                    src_ref=other, dst_ref=other, send_sem=fwd_s.at[3 * a + j], recv_sem=fwd_r.at[3 * a + j],
                    device_id=(x, y, c), device_id_type=_MESH).wait_recv()
        for cp in started:
            cp.wait_send()
        for cp in local:
            cp.wait()

    return pl.pallas_call(
        body, in_specs=[_HBM] * n, out_specs=[_HBM] * n,
        out_shape=[jax.ShapeDtypeStruct((4,) + w.shape[1:], w.dtype) for w in ws],
        scratch_shapes=[pltpu.SemaphoreType.DMA((3 * n,))] * 4 + [pltpu.SemaphoreType.DMA((n,))],
        name=name)(*ws)


def _exchange_sibling(gs, *, name):
    n = len(gs)

    def body(*refs):
        g_refs, o_refs = refs[:n], refs[n:2 * n]
        send, recv = refs[2 * n:]
        x, y, c, _ = _place()
        cps = []
        for a in range(n):
            h = g_refs[a].shape[1] // 2
            cp = pltpu.make_async_remote_copy(
                src_ref=g_refs[a].at[:, pl.ds((1 - c) * h, h), :], dst_ref=o_refs[a], send_sem=send.at[a], recv_sem=recv.at[a],
                device_id=(x, y, 1 - c), device_id_type=_MESH)
            cp.start()
            cps.append(cp)
        for cp in cps:
            cp.wait_recv()
        for cp in cps:
            cp.wait_send()

    return pl.pallas_call(
        body, in_specs=[_HBM] * n, out_specs=[_HBM] * n,
        out_shape=[jax.ShapeDtypeStruct((4, g.shape[1] // 2, g.shape[2]), g.dtype) for g in gs],
        scratch_shapes=[pltpu.SemaphoreType.DMA((n,))] * 2, name=name)(*gs)


def _exchange_chips(ps, *, name):
    n = len(ps)

    def body(*refs):
        p_refs, o_refs = refs[:n], refs[n:2 * n]
        send, recv, loc = refs[2 * n:]
        x, y, c, chips = _place()
        k = 2 * x + y
        cps, local = [], []
        for a in range(n):
            mine = pltpu.make_async_copy(p_refs[a].at[k], o_refs[a].at[k], loc.at[a])
            mine.start()
            local.append(mine)
            for j, (cx, cy) in enumerate(chips):
                cp = pltpu.make_async_remote_copy(
                    src_ref=p_refs[a].at[2 * cx + cy], dst_ref=o_refs[a].at[k], send_sem=send.at[3 * a + j],
                    recv_sem=recv.at[3 * a + j], device_id=(cx, cy, c), device_id_type=_MESH)
                cp.start()
                cps.append(cp)
        for a in range(n):
            for j, (cx, cy) in enumerate(chips):
                slot = o_refs[a].at[2 * cx + cy]
                pltpu.make_async_remote_copy(
                    src_ref=slot, dst_ref=slot, send_sem=send.at[3 * a + j], recv_sem=recv.at[3 * a + j],
                    device_id=(x, y, c), device_id_type=_MESH).wait_recv()
        for cp in cps:
            cp.wait_send()
        for cp in local:
            cp.wait()

    return pl.pallas_call(
        body, in_specs=[_HBM] * n, out_specs=[_HBM] * n,
        out_shape=[jax.ShapeDtypeStruct(p.shape, p.dtype) for p in ps],
        scratch_shapes=[pltpu.SemaphoreType.DMA((3 * n,))] * 2 + [pltpu.SemaphoreType.DMA((n,))], name=name)(*ps)


def _share_halves(gs, *, name):
    n = len(gs)

    def body(*refs):
        g_refs, o_refs = refs[:n], refs[n:2 * n]
        send, recv, loc = refs[2 * n:]
        x, y, c, _ = _place()
        cps, local = [], []
        for a in range(n):
            mine = pltpu.make_async_copy(g_refs[a], o_refs[a].at[:, c], loc.at[a])
            mine.start()
            local.append(mine)
            cp = pltpu.make_async_remote_copy(
                src_ref=g_refs[a], dst_ref=o_refs[a].at[:, c], send_sem=send.at[a], recv_sem=recv.at[a],
                device_id=(x, y, 1 - c), device_id_type=_MESH)
            cp.start()
            cps.append(cp)
        for a in range(n):
            theirs = o_refs[a].at[:, 1 - c]
            pltpu.make_async_remote_copy(
                src_ref=theirs, dst_ref=theirs, send_sem=send.at[a], recv_sem=recv.at[a],
                device_id=(x, y, c), device_id_type=_MESH).wait_recv()
        for cp in cps:
            cp.wait_send()
        for cp in local:
            cp.wait()

    return pl.pallas_call(
        body, in_specs=[_HBM] * n, out_specs=[_HBM] * n,
        out_shape=[jax.ShapeDtypeStruct((g.shape[0], 2) + g.shape[1:], g.dtype) for g in gs],
        scratch_shapes=[pltpu.SemaphoreType.DMA((n,))] * 3, name=name)(*gs)


def _allreduce_small(v, *, name):
    R = v.shape[0]

    def body(v_ref, o_ref, buf, send, recv, loc):
        x, y, c = lax.axis_index("x"), lax.axis_index("y"), lax.axis_index("c")
        me = 4 * x + 2 * y + c
        mine = pltpu.make_async_copy(v_ref, buf.at[me], loc)
        mine.start()
        cps = []
        for d in range(1, 8):
            px = 1 - x if d & 4 else x
            py = 1 - y if d & 2 else y
            pc = 1 - c if d & 1 else c
            cp = pltpu.make_async_remote_copy(
                src_ref=v_ref, dst_ref=buf.at[me], send_sem=send.at[d - 1], recv_sem=recv.at[d - 1],
                device_id=(px, py, pc), device_id_type=_MESH)
            cp.start()
            cps.append((cp, 4 * px + 2 * py + pc))
        for d in range(1, 8):
            cp, peer = cps[d - 1]
            pltpu.make_async_remote_copy(
                src_ref=buf.at[peer], dst_ref=buf.at[peer], send_sem=send.at[d - 1], recv_sem=recv.at[d - 1],
                device_id=(x, y, c), device_id_type=_MESH).wait_recv()
        for cp, _ in cps:
            cp.wait_send()
        mine.wait()
        acc = buf[0]
        for i in range(1, 8):
            acc = acc + buf[i]
        o_ref[...] = acc

    return pl.pallas_call(
        body, in_specs=[pl.BlockSpec(memory_space=pltpu.VMEM)], out_specs=pl.BlockSpec(memory_space=pltpu.VMEM),
        out_shape=jax.ShapeDtypeStruct((R, LANES), F32),
        scratch_shapes=[pltpu.VMEM((8, R, LANES), F32), pltpu.SemaphoreType.DMA((7,)), pltpu.SemaphoreType.DMA((7,)),
                        pltpu.SemaphoreType.DMA],
        compiler_params=pltpu.CompilerParams(vmem_limit_bytes=VMEM_LIMIT), name=name)(v)


def _pack(arrs, row_multiple=8):
    rows = []
    for a in arrs:
        flat = a.reshape(-1)
        flat = jnp.pad(flat, (0, (-flat.shape[0]) % LANES))
        rows.append(flat.reshape(-1, LANES))
    buf = jnp.concatenate(rows, axis=0)
    return jnp.pad(buf, ((0, (-buf.shape[0]) % row_multiple), (0, 0)))


def _unpack(buf, shapes):
    out, r = [], 0
    for s in shapes:
        size = math.prod(s)
        nr = -(-size // LANES)
        out.append(buf[r:r + nr].reshape(-1)[:size].reshape(s))
        r += nr
    return out


def kernel(x, mem, mix_norm, w_in, gdn_conv, gdn_a_log, gdn_dt_bias, gdn_norm, w_out, xattn_norm, mem_norm, w_xq, w_xkv, w_xo, ffn_norm, w_up, ffn_conv, ffn_conv_bias, w_down, final_norm, loss_target, m_mix_norm, m_w_in, m_gdn_conv, m_gdn_a_log, m_gdn_dt_bias, m_gdn_norm, m_w_out, m_xattn_norm, m_mem_norm, m_w_xq, m_w_xkv, m_w_xo, m_ffn_norm, m_w_up, m_ffn_conv, m_ffn_conv_bias, m_w_down, m_final_norm, v_mix_norm, v_w_in, v_gdn_conv, v_gdn_a_log, v_gdn_dt_bias, v_gdn_norm, v_w_out, v_xattn_norm, v_mem_norm, v_w_xq, v_w_xkv, v_w_xo, v_ffn_norm, v_w_up, v_ffn_conv, v_ffn_conv_bias, v_w_down, v_final_norm):
    L = w_in.shape[0]
    _, S, D = x.shape
    nh = D // (2 * HEAD_DIM)
    GW = nh * HEAD_DIM
    n_in = 7 * GW + 2 * nh
    NP = 7 * GW + LANES
    XW = X_HEADS * HEAD_DIM
    F = w_down.shape[1] * 4
    cs_in = w_in.shape[2]
    cs_up = w_up.shape[2]
    cs_xo = w_xo.shape[2]
    tu = _tile(cs_up, 1408)
    per = cs_up // tu
    fper = F // tu
    assert n_in == 4 * cs_in and F % tu == 0 and 2 * F == 4 * cs_up

    xi, yi, ci = lax.axis_index("x"), lax.axis_index("y"), lax.axis_index("c")
    chip = 2 * xi + yi
    cvec = jnp.reshape(ci, (1,)).astype(jnp.int32)

    cs_gc, cs_fc = gdn_conv.shape[2], ffn_conv.shape[2]
    keep = jnp.where(ci == 0, 1.0, 0.0).astype(F32)
    gc_full = lax.dynamic_update_slice(jnp.zeros((L, SHORT_CONV, 4 * cs_gc), F32), gdn_conv * keep, (0, 0, chip * cs_gc))
    fc_full = lax.dynamic_update_slice(jnp.zeros((L, FFN_CONV, 4 * cs_fc), F32), ffn_conv * keep, (0, 0, chip * cs_fc))
    conv_all = _allreduce_small(_pack([gc_full, fc_full]), name="allgather_conv")
    gdn_conv_full, ffn_conv_full = _unpack(conv_all, [gc_full.shape, fc_full.shape])

    big = [w_in, w_out, w_xq, w_xkv, w_xo, w_up, w_down]
    big_bf16 = [w.astype(BF16) for w in big]
    ab = jnp.zeros((L, 2, LANES), F32).at[:, 0, nh:2 * nh].set(gdn_a_log).at[:, 1, nh:2 * nh].set(gdn_dt_bias)

    def vec(p, l):
        return p[l:l + 1]

    xo_fwd_b = pl.BlockSpec((None, XW, cs_xo), lambda i, j, k: (j, 0, 0))
    xo_dg_b = pl.BlockSpec((None, XW, cs_xo), lambda i, j, k: (k, 0, 0))
    xo_wg_o = pl.BlockSpec((None, XW, cs_xo), lambda i, j, k: (j, 0, 0))
    up_fwd_b = pl.BlockSpec((None, D, tu), lambda i, j, k: (j // per, 0, j % per))

    def fwd_layer(l, xc, wts):
        g_in, g_out, g_xq, g_xkv, g_xo, g_up, g_down = wts
        w_in_l = jnp.concatenate([g_in[0], g_in[1], g_in[2], g_in[3], jnp.zeros((D, NP - n_in), BF16)], axis=1)
        w_out_l, w_xq_l, w_xkv_l, w_down_l = g_out.reshape(2 * GW, D), g_xq.reshape(D, XW), g_xkv.reshape(D, 2 * XW), g_down.reshape(F, D)
        s = dict(x0=xc, w_in=w_in_l, w_out=w_out_l, w_xq=w_xq_l, w_xkv=w_xkv_l, w_xo=g_xo, w_up=g_up, w_down=w_down_l)
        s["h"] = _rms_fwd(xc, vec(mix_norm, l), name="rms_mix_fwd")
        s["proj"] = _matmul(s["h"], w_in_l, tn=2432, tk=D, name="mm_in_fwd")
        s["sb"], s["tot"] = _sb_fwd(s["proj"], nh, name="sb_fwd")
        gdn_out, s["gdn"] = _gdn_forward(s["proj"], gdn_conv_full[l], ab[l], vec(gdn_norm, l), nh, "")
        s["mixed"] = jnp.concatenate([s["sb"], gdn_out], axis=1)
        s["x1"] = _matmul(s["mixed"], w_out_l, res=xc, tk=2 * GW, name="mm_out_fwd")
        s["memn"] = _rms_fwd(mem[0], vec(mem_norm, l), name="rms_mem_fwd")
        s["kv"] = _matmul(s["memn"], w_xkv_l, out_dtype=BF16, tk=D, name="mm_xkv_fwd")
        s["hq"] = _rms_fwd(s["x1"], vec(xattn_norm, l), name="rms_xattn_fwd")
        s["q"] = _matmul(s["hq"], w_xq_l, out_dtype=BF16, tk=D, name="mm_xq_fwd")
        s["xo"] = _xattn_fwd(s["q"], s["kv"], name="xattn_fwd")
        s["x2"] = _matmul(s["xo"], g_xo, res=s["x1"], dims=(S, D, XW), tn=cs_xo, tk=XW, b_spec=xo_fwd_b, name="mm_xo_fwd")
        s["hf"] = _rms_fwd(s["x2"], vec(ffn_norm, l), name="rms_ffn_fwd")
        s["u"] = _matmul(s["hf"], g_up, dims=(S, 2 * F, D), tn=tu, tk=D, b_spec=up_fwd_b, name="mm_up_fwd")
        s["act"] = _ffn_act_fwd(s["u"], ffn_conv_full[l], ffn_conv_bias[l:l + 1], name="ffn_act_fwd")
        x3 = _matmul(s["act"], w_down_l, res=s["x2"], tk=tu, name="mm_down_fwd")
        return x3, s

    def bwd_layer(l, s, dx3):
        dact = _matmul(dx3, s["w_down"], tb=True, tk=D, name="mm_down_dgrad")
        d_down = _matmul(s["act"], dx3, ta=True, tk=S, name="mm_down_wgrad")
        du3, dcw3, dcb3 = _ffn_act_bwd(s["u"], ffn_conv_full[l], ffn_conv_bias[l:l + 1], dact, name="ffn_act_bwd")
        dhf = _matmul(du3, s["w_up"], tb=True, dims=(S, D, 2 * F), tk=tu,
                      a_spec=pl.BlockSpec((None, _tile(S, 512), tu), lambda i, j, k: (k // fper, i, k % fper)),
                      b_spec=pl.BlockSpec((None, _tile(D, 512), tu), lambda i, j, k: (k // per, j, k % per)), name="mm_up_dgrad")
        d_up = _matmul(s["hf"], du3, ta=True, dims=(D, 2 * F, S), tn=tu, tk=S,
                       b_spec=pl.BlockSpec((None, S, tu), lambda i, j, k: (j // fper, 0, j % fper)),
                       o_spec=pl.BlockSpec((None, _tile(D, 512), tu), lambda i, j, k: (j // per, i, j % per)),
                       out_shape=jax.ShapeDtypeStruct((4, D, cs_up), F32), name="mm_up_wgrad")
        dx2, dg_ffn = _rms_bwd(s["x2"], vec(ffn_norm, l), dhf, dx3, name="rms_bwd")
        dxo = _matmul(dx2, s["w_xo"], tb=True, dims=(S, XW, D), tn=XW, tk=cs_xo, b_spec=xo_dg_b, name="mm_xo_dgrad")
        d_xo = _matmul(s["xo"], dx2, ta=True, dims=(XW, D, S), tm=XW, tn=cs_xo, tk=S, o_spec=xo_wg_o,
                       out_shape=jax.ShapeDtypeStruct((4, XW, cs_xo), F32), name="mm_xo_wgrad")
        dq, dk, dv = _xattn_bwd(s["q"], s["kv"], dxo, name="xattn_bwd")
        dkv = jnp.concatenate([dk, dv], axis=1)
        dhq = _matmul(dq, s["w_xq"], tb=True, tk=XW, name="mm_xq_dgrad")
        d_xq = _matmul(s["hq"], dq, ta=True, tk=S, name="mm_xq_wgrad")
        dmemn = _matmul(dkv, s["w_xkv"], tb=True, tk=2 * XW, name="mm_xkv_dgrad")
        d_xkv = _matmul(s["memn"], dkv, ta=True, tk=mem.shape[1], name="mm_xkv_wgrad")
        _, dg_mem = _rms_bwd(mem[0], vec(mem_norm, l), dmemn, None, name="rms_mem_bwd")
        dx1, dg_xattn = _rms_bwd(s["x1"], vec(xattn_norm, l), dhq, dx2, name="rms_bwd")
        dmix = _matmul(dx1, s["w_out"], tb=True, tk=D, name="mm_out_dgrad")
        d_out = _matmul(s["mixed"], dx1, ta=True, tk=S, name="mm_out_wgrad")
        dq_s, dk_s, dv_s = _sb_bwd(s["proj"], s["tot"], dmix, nh, name="sb_bwd")
        dx_qkv, dz, dx_g, dconv, dab, dng = _gdn_backward(s["proj"], gdn_conv_full[l], ab[l], vec(gdn_norm, l), s["gdn"], dmix, nh, "")
        dproj = jnp.concatenate([dq_s, dk_s, dv_s, dx_qkv, dz, dx_g], axis=1)
        dh = _matmul(dproj, s["w_in"], tb=True, tk=2432, name="mm_in_dgrad")
        d_in = _matmul(s["h"], dproj, ta=True, tn=2432, tk=S, name="mm_in_wgrad")
        dx0, dg_mix = _rms_bwd(s["x0"], vec(mix_norm, l), dh, dx1, name="rms_bwd")
        slabs = [jnp.stack([d_in[:, i * cs_in:(i + 1) * cs_in] for i in range(4)]), d_out.reshape(4, -1, D),
                 d_xq.reshape(4, -1, XW), d_xkv.reshape(4, -1, 2 * XW), d_xo, d_up, d_down.reshape(4, -1, D)]
        small = [dg_mix, dconv, dab, dng, dg_xattn, dg_mem, dg_ffn,
                 jnp.concatenate([dcw3[0], dcw3[1]], axis=1), jnp.concatenate([dcb3[0], dcb3[1]], axis=1)]
        return dx0, slabs, small

    xc = x[0]
    saved = []
    for l in range(L):
        wts = _gather_weights(big_bf16, l, name=f"gather_weights_{l}")
        xc, s = fwd_layer(l, xc, wts)
        saved.append(s)
    loss_blk, dxc, dg_final = _loss_head(xc, final_norm[None, :], loss_target[0], name="loss_head")

    sums = [None] * 7
    small_by_layer = [None] * L
    for l in reversed(range(L)):
        dxc, slabs, small_by_layer[l] = bwd_layer(l, saved[l], dxc)
        saved[l] = None
        from_sibling = _exchange_sibling(slabs, name="reduce_to_sibling")
        partial = [_sum_half(g, r, cvec, name="sum_sibling") for g, r in zip(slabs, from_sibling)]
        from_chips = _exchange_chips(partial, name="reduce_to_chip")
        sums = [_sum_chips(rb, prev, l, L, name=f"sum_chips_{l}") for rb, prev in zip(from_chips, sums)]
    halves = _share_halves(sums, name="share_halves")
    grads_big = [h.reshape(w.shape) for h, w in zip(halves, big)]

    small_flat = [a for l in range(L) for a in small_by_layer[l]] + [dg_final, loss_blk[0:1]]
    red = _unpack(_allreduce_small(_pack(small_flat), name="allreduce_small"), [a.shape for a in small_flat])
    per_layer = [red[9 * l:9 * l + 9] for l in range(L)]
    col = lambda i: jnp.concatenate([p[i] for p in per_layer], axis=0)
    stk = lambda i: jnp.stack([p[i] for p in per_layer])
    g_conv_full, g_ab, g_fconv_full = stk(1), stk(2), stk(7)
    grads_small = dict(
        mix_norm=col(0), gdn_conv=lax.dynamic_slice(g_conv_full, (0, 0, chip * cs_gc), (L, SHORT_CONV, cs_gc)),
        gdn_a_log=g_ab[:, 0, nh:2 * nh], gdn_dt_bias=g_ab[:, 1, nh:2 * nh], gdn_norm=col(3), xattn_norm=col(4),
        mem_norm=col(5), ffn_norm=col(6), ffn_conv=lax.dynamic_slice(g_fconv_full, (0, 0, chip * cs_fc), (L, FFN_CONV, cs_fc)),
        ffn_conv_bias=col(8), final_norm=red[-2][0])
    loss = red[-1][0, 0]

    names_small = ["mix_norm", "gdn_conv", "gdn_a_log", "gdn_dt_bias", "gdn_norm", "xattn_norm", "mem_norm", "ffn_norm",
                   "ffn_conv", "ffn_conv_bias", "final_norm"]
    w_small = dict(mix_norm=mix_norm, gdn_conv=gdn_conv, gdn_a_log=gdn_a_log, gdn_dt_bias=gdn_dt_bias, gdn_norm=gdn_norm,
                   xattn_norm=xattn_norm, mem_norm=mem_norm, ffn_norm=ffn_norm, ffn_conv=ffn_conv, ffn_conv_bias=ffn_conv_bias,
                   final_norm=final_norm)
    m_small = dict(mix_norm=m_mix_norm, gdn_conv=m_gdn_conv, gdn_a_log=m_gdn_a_log, gdn_dt_bias=m_gdn_dt_bias, gdn_norm=m_gdn_norm,
                   xattn_norm=m_xattn_norm, mem_norm=m_mem_norm, ffn_norm=m_ffn_norm, ffn_conv=m_ffn_conv,
                   ffn_conv_bias=m_ffn_conv_bias, final_norm=m_final_norm)
    v_small = dict(mix_norm=v_mix_norm, gdn_conv=v_gdn_conv, gdn_a_log=v_gdn_a_log, gdn_dt_bias=v_gdn_dt_bias, gdn_norm=v_gdn_norm,
                   xattn_norm=v_xattn_norm, mem_norm=v_mem_norm, ffn_norm=v_ffn_norm, ffn_conv=v_ffn_conv,
                   ffn_conv_bias=v_ffn_conv_bias, final_norm=v_final_norm)
    shapes_small = [w_small[n].shape for n in names_small]
    packed = [_pack([d[n] for n in names_small], row_multiple=128)[None] for d in (w_small, grads_small, m_small, v_small)]
    upd_small = [_unpack(o[0], shapes_small) for o in _adamw(*packed, name="adamw_small")]
    delta, new_m, new_v = [dict(zip(names_small, u)) for u in upd_small]
    grads = dict(grads_small)
    big_names = ["w_in", "w_out", "w_xq", "w_xkv", "w_xo", "w_up", "w_down"]
    big_m = [m_w_in, m_w_out, m_w_xq, m_w_xkv, m_w_xo, m_w_up, m_w_down]
    big_v = [v_w_in, v_w_out, v_w_xq, v_w_xkv, v_w_xo, v_w_up, v_w_down]
    for n, w, g, m, v in zip(big_names, big, grads_big, big_m, big_v):
        grads[n] = g
        delta[n], new_m[n], new_v[n] = _adamw(w, g, m, v, name=f"adamw_{n}")

    order = ["mix_norm", "w_in", "gdn_conv", "gdn_a_log", "gdn_dt_bias", "gdn_norm", "w_out", "xattn_norm", "mem_norm", "w_xq",
             "w_xkv", "w_xo", "ffn_norm", "w_up", "ffn_conv", "ffn_conv_bias", "w_down", "final_norm"]
    return (loss, dxc[None], *[grads[n] for n in order], *[delta[n] for n in order], *[new_m[n] for n in order],
            *[new_v[n] for n in order])
```

```python
import functools
import math

import jax
import jax.numpy as jnp
from jax import lax
from jax.experimental import pallas as pl
from jax.experimental.pallas import tpu as pltpu

F32 = jnp.float32
BF16 = jnp.bfloat16

HEAD_DIM = 128
CHUNK = 64
GDN_CPB = 4
SB_TQ, SB_TK = 256, 512
SHORT_CONV = 4
FFN_CONV = 3
X_HEADS = 4
EPS = 1e-6
LANES = 128
VMEM_LIMIT = 56 * 2**20

ADAM_LR, ADAM_B1, ADAM_B2, ADAM_EPS, ADAM_WD, ADAM_STEP = 0.001, 0.9, 0.999, 1e-08, 0.01, 10

HI = lax.Precision.HIGHEST


def _params(sem):
    return pltpu.CompilerParams(dimension_semantics=sem, vmem_limit_bytes=VMEM_LIMIT)


def _tile(n, want):
    if n <= want:
        return n
    t = (want // LANES) * LANES
    while t > LANES and n % t:
        t -= LANES
    assert n % t == 0, (n, want)
    return t


def _sigmoid(x):
    return 1.0 / (1.0 + jnp.exp(-x))


def _softplus(x):
    return jnp.maximum(x, 0.0) + jnp.log(1.0 + jnp.exp(-jnp.abs(x)))


def _matmul(a, b, *, name, ta=False, tb=False, out_dtype=F32, res=None, tm=512, tn=512, tk=2048,
            dims=None, a_spec=None, b_spec=None, o_spec=None, out_shape=None):
    if dims is None:
        M, K = (a.shape[1], a.shape[0]) if ta else a.shape
        N = b.shape[0] if tb else b.shape[1]
    else:
        M, N, K = dims
    tm, tn, tk = _tile(M, tm), _tile(N, tn), _tile(K, tk)
    nk = K // tk
    dn = (((0 if ta else 1,), (1 if tb else 0,)), ((), ()))

    def body(*refs):
        a_ref, b_ref = refs[0], refs[1]
        r_ref = refs[2] if res is not None else None
        o_ref = refs[3] if res is not None else refs[2]
        p = lax.dot_general(a_ref[...].astype(BF16), b_ref[...].astype(BF16), dn, preferred_element_type=F32)

        def finish(acc):
            if r_ref is not None:
                acc = acc + r_ref[...].astype(F32)
            o_ref[...] = acc.astype(o_ref.dtype)

        if nk == 1:
            finish(p)
        else:
            acc_ref = refs[-1]
            k = pl.program_id(2)

            @pl.when(k == 0)
            def _():
                acc_ref[...] = p

            @pl.when(jnp.logical_and(k > 0, k < nk - 1))
            def _():
                acc_ref[...] += p

            @pl.when(k == nk - 1)
            def _():
                finish(acc_ref[...] + p)

    if a_spec is None:
        a_spec = pl.BlockSpec((tk, tm), lambda i, j, k: (k, i)) if ta else pl.BlockSpec((tm, tk), lambda i, j, k: (i, k))
    if b_spec is None:
        b_spec = pl.BlockSpec((tn, tk), lambda i, j, k: (j, k)) if tb else pl.BlockSpec((tk, tn), lambda i, j, k: (k, j))
    if o_spec is None:
        o_spec = pl.BlockSpec((tm, tn), lambda i, j, k: (i, j))
    if out_shape is None:
        out_shape = jax.ShapeDtypeStruct((M, N), out_dtype)
    in_specs, args = [a_spec, b_spec], [a, b]
    if res is not None:
        in_specs.append(pl.BlockSpec((tm, tn), lambda i, j, k: (i, j)))
        args.append(res)
    return pl.pallas_call(
        body, grid=(M // tm, N // tn, nk), in_specs=in_specs, out_specs=o_spec, out_shape=out_shape,
        scratch_shapes=[pltpu.VMEM((tm, tn), F32)] if nk > 1 else [],
        compiler_params=_params(("parallel", "parallel", "arbitrary")), name=name)(*args)


def _rms_fwd(x, g, *, name):
    R, D = x.shape
    tr = _tile(R, 256)

    def body(x_ref, g_ref, o_ref):
        xv = x_ref[...]
        rstd = lax.rsqrt(jnp.mean(xv * xv, axis=-1, keepdims=True) + EPS)
        o_ref[...] = (xv * rstd * g_ref[...]).astype(o_ref.dtype)

    return pl.pallas_call(
        body, grid=(R // tr,), in_specs=[pl.BlockSpec((tr, D), lambda i: (i, 0)), pl.BlockSpec((1, D), lambda i: (0, 0))],
        out_specs=pl.BlockSpec((tr, D), lambda i: (i, 0)), out_shape=jax.ShapeDtypeStruct((R, D), BF16),
        compiler_params=_params(("parallel",)), name=name)(x, g)


def _rms_bwd(x, g, dh, dres, *, name):
    R, D = x.shape
    tr = _tile(R, 256)

    def body(*refs):
        if dres is None:
            x_ref, g_ref, dh_ref, dx_ref, dg_ref = refs
        else:
            x_ref, g_ref, dh_ref, dr_ref, dx_ref, dg_ref = refs
        xv = x_ref[...]
        dhv = dh_ref[...].astype(F32)
        rstd = lax.rsqrt(jnp.mean(xv * xv, axis=-1, keepdims=True) + EPS)
        xhat = xv * rstd
        t = dhv * g_ref[...]
        dx = rstd * (t - xhat * jnp.mean(t * xhat, axis=-1, keepdims=True))
        if dres is not None:
            dx = dx + dr_ref[...]
        dx_ref[...] = dx
        part = jnp.sum(dhv * xhat, axis=0, keepdims=True)

        @pl.when(pl.program_id(0) == 0)
        def _():
            dg_ref[...] = part

        @pl.when(pl.program_id(0) > 0)
        def _():
            dg_ref[...] += part

    row = pl.BlockSpec((tr, D), lambda i: (i, 0))
    vec = pl.BlockSpec((1, D), lambda i: (0, 0))
    in_specs = [row, vec, row] + ([row] if dres is not None else [])
    args = [x, g, dh] + ([dres] if dres is not None else [])
    return pl.pallas_call(
        body, grid=(R // tr,), in_specs=in_specs, out_specs=[row, vec],
        out_shape=[jax.ShapeDtypeStruct((R, D), F32), jax.ShapeDtypeStruct((1, D), F32)],
        compiler_params=_params(("arbitrary",)), name=name)(*args)


def _loss_head(x, g, tgt, *, name):
    R, D = x.shape
    tr = _tile(R, 256)

    def body(x_ref, g_ref, t_ref, l_ref, dx_ref, dg_ref):
        xv = x_ref[...]
        rstd = lax.rsqrt(jnp.mean(xv * xv, axis=-1, keepdims=True) + EPS)
        xhat = xv * rstd
        err = xhat * g_ref[...] - t_ref[...]
        dy = err * (1.0 / D)
        t = dy * g_ref[...]
        dx_ref[...] = rstd * (t - xhat * jnp.mean(t * xhat, axis=-1, keepdims=True))
        part = jnp.sum(dy * xhat, axis=0, keepdims=True)
        lpart = jnp.zeros((8, LANES), F32) + 0.5 * jnp.sum(jnp.mean(err * err, axis=-1, keepdims=True))

        @pl.when(pl.program_id(0) == 0)
        def _():
            dg_ref[...] = part
            l_ref[...] = lpart

        @pl.when(pl.program_id(0) > 0)
        def _():
            dg_ref[...] += part
            l_ref[...] += lpart

    row = pl.BlockSpec((tr, D), lambda i: (i, 0))
    vec = pl.BlockSpec((1, D), lambda i: (0, 0))
    return pl.pallas_call(
        body, grid=(R // tr,), in_specs=[row, vec, row],
        out_specs=[pl.BlockSpec((8, LANES), lambda i: (0, 0)), row, vec],
        out_shape=[jax.ShapeDtypeStruct((8, LANES), F32), jax.ShapeDtypeStruct((R, D), F32), jax.ShapeDtypeStruct((1, D), F32)],
        compiler_params=_params(("arbitrary",)), name=name)(x, g, tgt)


def _shift_down(x, s):
    if s == 0:
        return x
    row = lax.broadcasted_iota(jnp.int32, x.shape, 0)
    return jnp.where(row >= s, pltpu.roll(x, s, 0), 0.0)


def _shift_up(x, s):
    if s == 0:
        return x
    n = x.shape[0]
    row = lax.broadcasted_iota(jnp.int32, x.shape, 0)
    return jnp.where(row < n - s, pltpu.roll(x, n - s, 0), 0.0)


def _dwconv(x, w):
    k = w.shape[0]
    acc = x * w[k - 1:k, :]
    for i in range(k - 1):
        acc = acc + _shift_down(x, k - 1 - i) * w[i:i + 1, :]
    return acc


def _dwconv_bwd(x, w, dc):
    k = w.shape[0]
    dx = dc * w[k - 1:k, :]
    dws = []
    for i in range(k - 1):
        s = k - 1 - i
        dx = dx + _shift_up(dc, s) * w[i:i + 1, :]
        dws.append(jnp.sum(dc * _shift_down(x, s), axis=0, keepdims=True))
    dws.append(jnp.sum(dc * x, axis=0, keepdims=True))
    return dx, jnp.concatenate(dws, axis=0)


def _ffn_act_fwd(u, cw, cb, *, name):
    S, F2 = u.shape
    F = F2 // 2
    tc = _tile(F, 256)
    nb = F // tc

    def body(ug_ref, uu_ref, wg_ref, wu_ref, bg_ref, bu_ref, o_ref):
        cg = _dwconv(ug_ref[...], wg_ref[...]) + bg_ref[...]
        cu = _dwconv(uu_ref[...], wu_ref[...]) + bu_ref[...]
        o_ref[...] = (cg * _sigmoid(cg) * cu).astype(o_ref.dtype)

    blk = lambda r, off: pl.BlockSpec((r, tc), lambda j: (0, j + off))
    return pl.pallas_call(
        body, grid=(nb,), in_specs=[blk(S, 0), blk(S, nb), blk(FFN_CONV, 0), blk(FFN_CONV, nb), blk(1, 0), blk(1, nb)],
        out_specs=blk(S, 0), out_shape=jax.ShapeDtypeStruct((S, F), BF16),
        compiler_params=_params(("parallel",)), name=name)(u, u, cw, cw, cb, cb)


def _ffn_act_bwd(u, cw, cb, dact, *, name):
    S, F2 = u.shape
    F = F2 // 2
    tc = _tile(F, 256)
    nb = F // tc

    def body(ug_ref, uu_ref, wg_ref, wu_ref, bg_ref, bu_ref, da_ref, du_ref, dw_ref, db_ref):
        ug, uu = ug_ref[...], uu_ref[...]
        cg = _dwconv(ug, wg_ref[...]) + bg_ref[...]
        cu = _dwconv(uu, wu_ref[...]) + bu_ref[...]
        sg = _sigmoid(cg)
        da = da_ref[...].astype(F32)
        dcu = da * (cg * sg)
        dcg = da * cu * (sg * (1.0 + cg * (1.0 - sg)))
        dxg, dwg = _dwconv_bwd(ug, wg_ref[...], dcg)
        dxu, dwu = _dwconv_bwd(uu, wu_ref[...], dcu)
        du_ref[0] = dxg.astype(du_ref.dtype)
        du_ref[1] = dxu.astype(du_ref.dtype)
        dw_ref[0] = dwg
        dw_ref[1] = dwu
        db_ref[0] = jnp.sum(dcg, axis=0, keepdims=True)
        db_ref[1] = jnp.sum(dcu, axis=0, keepdims=True)

    blk = lambda r, off: pl.BlockSpec((r, tc), lambda j: (0, j + off))
    blk3 = lambda r: pl.BlockSpec((2, r, tc), lambda j: (0, 0, j))
    return pl.pallas_call(
        body, grid=(nb,),
        in_specs=[blk(S, 0), blk(S, nb), blk(FFN_CONV, 0), blk(FFN_CONV, nb), blk(1, 0), blk(1, nb), blk(S, 0)],
        out_specs=[blk3(S), blk3(FFN_CONV), blk3(1)],
        out_shape=[jax.ShapeDtypeStruct((2, S, F), BF16), jax.ShapeDtypeStruct((2, FFN_CONV, F), F32),
                   jax.ShapeDtypeStruct((2, 1, F), F32)],
        compiler_params=_params(("parallel",)), name=name)(u, u, cw, cw, cb, cb, dact)


def _xattn_fwd(q, kv, *, name):
    S, XW = q.shape
    M = kv.shape[0]
    nh = XW // HEAD_DIM
    tq = _tile(S, 512)
    scale = HEAD_DIM ** -0.5

    def body(q_ref, k_ref, v_ref, o_ref):
        z = lax.dot_general(q_ref[...], k_ref[...], (((1,), (1,)), ((), ())), preferred_element_type=F32) * scale
        e = jnp.exp(z - jnp.max(z, axis=-1, keepdims=True))
        p = e / jnp.sum(e, axis=-1, keepdims=True)
        o_ref[...] = jnp.dot(p.astype(BF16), v_ref[...], preferred_element_type=F32).astype(o_ref.dtype)

    return pl.pallas_call(
        body, grid=(nh, S // tq),
        in_specs=[pl.BlockSpec((tq, HEAD_DIM), lambda h, i: (i, h)), pl.BlockSpec((M, HEAD_DIM), lambda h, i: (0, h)),
                  pl.BlockSpec((M, HEAD_DIM), lambda h, i: (0, nh + h))],
        out_specs=pl.BlockSpec((tq, HEAD_DIM), lambda h, i: (i, h)), out_shape=jax.ShapeDtypeStruct((S, XW), BF16),
        compiler_params=_params(("parallel", "parallel")), name=name)(q, kv, kv)


def _xattn_bwd(q, kv, do, *, name):
    S, XW = q.shape
    M = kv.shape[0]
    nh = XW // HEAD_DIM
    tq = _tile(S, 512)
    scale = HEAD_DIM ** -0.5
    nt = (((1,), (1,)), ((), ()))
    tn = (((0,), (0,)), ((), ()))

    def body(q_ref, k_ref, v_ref, do_ref, dq_ref, dk_ref, dv_ref):
        qv, kvv, vv = q_ref[...], k_ref[...], v_ref[...]
        dov = do_ref[...].astype(BF16)
        z = lax.dot_general(qv, kvv, nt, preferred_element_type=F32) * scale
        e = jnp.exp(z - jnp.max(z, axis=-1, keepdims=True))
        p = e / jnp.sum(e, axis=-1, keepdims=True)
        dp = lax.dot_general(dov, vv, nt, preferred_element_type=F32)
        ds = (p * (dp - jnp.sum(dp * p, axis=-1, keepdims=True)) * scale).astype(BF16)
        dq_ref[...] = jnp.dot(ds, kvv, preferred_element_type=F32).astype(dq_ref.dtype)
        dk = lax.dot_general(ds, qv, tn, preferred_element_type=F32)
        dv = lax.dot_general(p.astype(BF16), dov, tn, preferred_element_type=F32)

        @pl.when(pl.program_id(1) == 0)
        def _():
            dk_ref[...] = dk
            dv_ref[...] = dv

        @pl.when(pl.program_id(1) > 0)
        def _():
            dk_ref[...] += dk
            dv_ref[...] += dv

    qs = pl.BlockSpec((tq, HEAD_DIM), lambda h, i: (i, h))
    ms = pl.BlockSpec((M, HEAD_DIM), lambda h, i: (0, h))
    return pl.pallas_call(
        body, grid=(nh, S // tq),
        in_specs=[qs, ms, pl.BlockSpec((M, HEAD_DIM), lambda h, i: (0, nh + h)), qs],
        out_specs=[qs, ms, ms],
        out_shape=[jax.ShapeDtypeStruct((S, XW), BF16), jax.ShapeDtypeStruct((M, XW), F32), jax.ShapeDtypeStruct((M, XW), F32)],
        compiler_params=_params(("parallel", "arbitrary")), name=name)(q, kv, kv, do)


_NN = (((1,), (0,)), ((), ()))
_NT = (((1,), (1,)), ((), ()))
_TN = (((0,), (0,)), ((), ()))


def _dot(a, b, dn=_NN):
    return lax.dot_general(a.astype(BF16), b.astype(BF16), dn, preferred_element_type=F32)


def _dot_hi(a, b, dn=_NN):
    return lax.dot_general(a, b, dn, preferred_element_type=F32, precision=HI)


def _dot_split(a, b01, dn=_NN):
    hi = a.astype(BF16)
    lo = (a - hi.astype(F32)).astype(BF16)
    return (lax.dot_general(hi, b01, dn, preferred_element_type=F32)
            + lax.dot_general(lo, b01, dn, preferred_element_type=F32))


def _after_matrix(n, transpose=False):
    row = lax.broadcasted_iota(jnp.int32, (n, n), 0)
    col = lax.broadcasted_iota(jnp.int32, (n, n), 1)
    return (row < col if transpose else row > col).astype(BF16)


def _sb_fwd(proj, nh, *, name):
    S = proj.shape[0]
    TQ, TK = min(SB_TQ, S), min(SB_TK, S)
    nq = S // TQ
    scale = HEAD_DIM ** -0.5

    def body(q_ref, k_ref, v_ref, o_ref, tot_ref):
        i = pl.program_id(1)
        q = q_ref[...].astype(BF16)
        qpos = i * TQ + lax.broadcasted_iota(jnp.int32, (TQ, TK), 0)
        kcol = lax.broadcasted_iota(jnp.int32, (TQ, TK), 1)
        after = _after_matrix(TK)
        nt = ((i + 1) * TQ + TK - 1) // TK

        def step(t, carry):
            acc, out = carry
            off = pl.multiple_of((nt - 1 - t) * TK, TK)
            kb = k_ref[pl.ds(off, TK), :].astype(BF16)
            vb = v_ref[pl.ds(off, TK), :].astype(BF16)
            z = lax.dot_general(q, kb, _NT, preferred_element_type=F32) * scale
            valid = kcol + off < qpos
            ls = jnp.where(valid, -_softplus(z), 0.0)
            later = _dot_split(ls, after) + acc
            w = jnp.where(valid, jnp.exp(ls + z + later), 0.0)
            out = out + jnp.dot(w.astype(BF16), vb, preferred_element_type=F32)
            return acc + jnp.sum(ls, axis=1, keepdims=True), out

        acc, out = lax.fori_loop(0, nt, step, (jnp.zeros((TQ, 1), F32), jnp.zeros((TQ, HEAD_DIM), F32)))
        o_ref[...] = out.astype(o_ref.dtype)
        tot_ref[...] = acc

    return pl.pallas_call(
        body, grid=(nh, nq),
        in_specs=[pl.BlockSpec((TQ, HEAD_DIM), lambda h, i: (i, h)),
                  pl.BlockSpec((S, HEAD_DIM), lambda h, i: (0, nh + h)),
                  pl.BlockSpec((S, HEAD_DIM), lambda h, i: (0, 2 * nh + h))],
        out_specs=[pl.BlockSpec((TQ, HEAD_DIM), lambda h, i: (i, h)), pl.BlockSpec((None, TQ, 1), lambda h, i: (h, i, 0))],
        out_shape=[jax.ShapeDtypeStruct((S, nh * HEAD_DIM), BF16), jax.ShapeDtypeStruct((nh, S, 1), F32)],
        compiler_params=_params(("parallel", "parallel")), name=name)(proj, proj, proj)


def _sb_bwd(proj, tot, dmix, nh, *, name):
    S = proj.shape[0]
    TQ, TK = min(SB_TQ, S), min(SB_TK, S)
    nq = S // TQ
    scale = HEAD_DIM ** -0.5

    def body(q_ref, k_ref, v_ref, tot_ref, do_ref, dq_ref, dk_ref, dv_ref, dk_acc, dv_acc):
        i = pl.program_id(1)

        @pl.when(i == 0)
        def _():
            dk_acc[...] = jnp.zeros_like(dk_acc)
            dv_acc[...] = jnp.zeros_like(dv_acc)

        q = q_ref[...].astype(BF16)
        do = do_ref[...].astype(BF16)
        tot = tot_ref[...]
        qpos = i * TQ + lax.broadcasted_iota(jnp.int32, (TQ, TK), 0)
        kcol = lax.broadcasted_iota(jnp.int32, (TQ, TK), 1)
        after = _after_matrix(TK)
        before = _after_matrix(TK, transpose=True)
        nt = ((i + 1) * TQ + TK - 1) // TK

        def step(j, carry):
            pre, g_sum, dq = carry
            off = pl.multiple_of(j * TK, TK)
            kb = k_ref[pl.ds(off, TK), :].astype(BF16)
            vb = v_ref[pl.ds(off, TK), :].astype(BF16)
            z = lax.dot_general(q, kb, _NT, preferred_element_type=F32) * scale
            valid = kcol + off < qpos
            ls = jnp.where(valid, -_softplus(z), 0.0)
            lb = ls + z
            rs = jnp.sum(ls, axis=1, keepdims=True)
            later = _dot_split(ls, after) + (tot - pre - rs)
            w = jnp.where(valid, jnp.exp(lb + later), 0.0)
            g = lax.dot_general(do, vb, _NT, preferred_element_type=F32) * w
            dls = _dot_split(g, before) + g_sum
            sig = jnp.exp(lb)
            dz = (jnp.where(valid, g * (1.0 - sig) - dls * sig, 0.0) * scale).astype(BF16)
            dq = dq + jnp.dot(dz, kb, preferred_element_type=F32)
            dk_acc[pl.ds(off, TK), :] += lax.dot_general(dz, q, _TN, preferred_element_type=F32)
            dv_acc[pl.ds(off, TK), :] += lax.dot_general(w.astype(BF16), do, _TN, preferred_element_type=F32)
            return pre + rs, g_sum + jnp.sum(g, axis=1, keepdims=True), dq

        zero = jnp.zeros((TQ, 1), F32)
        _, _, dq = lax.fori_loop(0, nt, step, (zero, zero, jnp.zeros((TQ, HEAD_DIM), F32)))
        dq_ref[...] = dq.astype(dq_ref.dtype)

        @pl.when(i == nq - 1)
        def _():
            dk_ref[...] = dk_acc[...].astype(dk_ref.dtype)
            dv_ref[...] = dv_acc[...].astype(dv_ref.dtype)

    qs = pl.BlockSpec((TQ, HEAD_DIM), lambda h, i: (i, h))
    full = pl.BlockSpec((S, HEAD_DIM), lambda h, i: (0, h))
    o = jax.ShapeDtypeStruct((S, nh * HEAD_DIM), BF16)
    return pl.pallas_call(
        body, grid=(nh, nq),
        in_specs=[qs, pl.BlockSpec((S, HEAD_DIM), lambda h, i: (0, nh + h)),
                  pl.BlockSpec((S, HEAD_DIM), lambda h, i: (0, 2 * nh + h)),
                  pl.BlockSpec((None, TQ, 1), lambda h, i: (h, i, 0)), qs],
        out_specs=[qs, full, full], out_shape=[o, o, o],
        scratch_shapes=[pltpu.VMEM((S, HEAD_DIM), F32), pltpu.VMEM((S, HEAD_DIM), F32)],
        compiler_params=_params(("parallel", "arbitrary")), name=name)(proj, proj, proj, tot, dmix)


def _gdn_qkv_fwd(proj, conv_w, nh, *, name):
    S = proj.shape[0]
    GW = nh * HEAD_DIM
    scale = HEAD_DIM ** -0.5

    def body(x_ref, w_ref, o_ref):
        sec = pl.program_id(0) // nh
        c = _dwconv(x_ref[...], w_ref[...])
        s = c * _sigmoid(c)
        r = lax.rsqrt(jnp.sum(s * s, axis=1, keepdims=True) + EPS)
        fac = jnp.where(sec == 0, scale, 1.0)
        o_ref[...] = jnp.where(sec == 2, s, s * (r * fac))

    return pl.pallas_call(
        body, grid=(3 * nh,),
        in_specs=[pl.BlockSpec((S, HEAD_DIM), lambda j: (0, 3 * nh + j)), pl.BlockSpec((SHORT_CONV, HEAD_DIM), lambda j: (0, j))],
        out_specs=pl.BlockSpec((None, S, HEAD_DIM), lambda j: (j // nh, 0, j % nh)),
        out_shape=jax.ShapeDtypeStruct((3, S, GW), F32),
        compiler_params=_params(("parallel",)), name=name)(proj, conv_w)


def _gdn_qkv_bwd(proj, conv_w, dqkv, nh, *, name):
    S = proj.shape[0]
    GW = nh * HEAD_DIM
    scale = HEAD_DIM ** -0.5

    def body(x_ref, w_ref, d_ref, dx_ref, dw_ref):
        sec = pl.program_id(0) // nh
        x, w = x_ref[...], w_ref[...]
        c = _dwconv(x, w)
        sg = _sigmoid(c)
        s = c * sg
        r = lax.rsqrt(jnp.sum(s * s, axis=1, keepdims=True) + EPS)
        sh = s * r
        d = d_ref[...]
        fac = jnp.where(sec == 0, scale, 1.0)
        dn = (r * fac) * (d - sh * jnp.sum(d * sh, axis=1, keepdims=True))
        ds = jnp.where(sec == 2, d, dn)
        dx, dw = _dwconv_bwd(x, w, ds * (sg * (1.0 + c * (1.0 - sg))))
        dx_ref[...] = dx.astype(dx_ref.dtype)
        dw_ref[...] = dw

    return pl.pallas_call(
        body, grid=(3 * nh,),
        in_specs=[pl.BlockSpec((S, HEAD_DIM), lambda j: (0, 3 * nh + j)), pl.BlockSpec((SHORT_CONV, HEAD_DIM), lambda j: (0, j)),
                  pl.BlockSpec((None, S, HEAD_DIM), lambda j: (j // nh, 0, j % nh))],
        out_specs=[pl.BlockSpec((S, HEAD_DIM), lambda j: (0, j)), pl.BlockSpec((SHORT_CONV, HEAD_DIM), lambda j: (0, j))],
        out_shape=[jax.ShapeDtypeStruct((S, 3 * GW), BF16), jax.ShapeDtypeStruct((SHORT_CONV, 3 * GW), F32)],
        compiler_params=_params(("parallel",)), name=name)(proj, conv_w, dqkv)


def _gdn_gates_fwd(proj, ab, nh, *, name):
    S = proj.shape[0]
    C = CHUNK

    def body(x_ref, ab_ref, o_ref):
        ri = lax.broadcasted_iota(jnp.int32, (C, C), 0)
        ci = lax.broadcasted_iota(jnp.int32, (C, C), 1)
        ltri = (ri >= ci).astype(F32)
        lane = lax.broadcasted_iota(jnp.int32, (C, LANES), 1)
        a_coef = -jnp.exp(ab_ref[0:1, :])
        dt = ab_ref[1:2, :]

        def chunk(n, _):
            rows = pl.ds(pl.multiple_of(n * C, C), C)
            x = x_ref[rows, :]
            beta = _sigmoid(x)
            g = jnp.where(jnp.logical_and(lane >= nh, lane < 2 * nh), a_coef * _softplus(x + dt), 0.0)
            gc = _dot_hi(ltri, pltpu.roll(g, nh, 1))
            o_ref[rows, :] = jnp.where(lane < nh, beta, g) + gc
            return 0

        lax.fori_loop(0, S // C, chunk, 0)

    return pl.pallas_call(
        body, grid=(1,),
        in_specs=[pl.BlockSpec((S, LANES), lambda i: (0, 7 * nh)), pl.BlockSpec((2, LANES), lambda i: (0, 0))],
        out_specs=pl.BlockSpec((S, LANES), lambda i: (0, 0)), out_shape=jax.ShapeDtypeStruct((S, LANES), F32),
        compiler_params=_params(("arbitrary",)), name=name)(proj, ab)


def _gdn_gates_bwd(proj, ab, dgt, nh, *, name):
    S = proj.shape[0]
    C = CHUNK

    def body(x_ref, ab_ref, d_ref, dx_ref, dab_ref):
        ri = lax.broadcasted_iota(jnp.int32, (C, C), 0)
        ci = lax.broadcasted_iota(jnp.int32, (C, C), 1)
        utri = (ri <= ci).astype(F32)
        lane = lax.broadcasted_iota(jnp.int32, (C, LANES), 1)
        is_b = lane < nh
        is_a = jnp.logical_and(lane >= nh, lane < 2 * nh)
        a_coef = -jnp.exp(ab_ref[0:1, :])
        dt = ab_ref[1:2, :]

        def chunk(n, carry):
            da_log, ddt = carry
            rows = pl.ds(pl.multiple_of(n * C, C), C)
            x = x_ref[rows, :]
            d = d_ref[rows, :]
            beta = _sigmoid(x)
            dg = pltpu.roll(_dot_hi(utri, jnp.where(lane >= 2 * nh, d, 0.0)), LANES - nh, 1)
            dg = jnp.where(is_a, dg, 0.0)
            dxa = dg * a_coef * _sigmoid(x + dt)
            dxb = jnp.where(is_b, d * beta * (1.0 - beta), 0.0)
            dx_ref[rows, :] = (dxa + dxb).astype(dx_ref.dtype)
            da_log = da_log + jnp.sum(dg * a_coef * _softplus(x + dt), axis=0, keepdims=True)
            return da_log, ddt + jnp.sum(dxa, axis=0, keepdims=True)

        zero = jnp.zeros((1, LANES), F32)
        da_log, ddt = lax.fori_loop(0, S // C, chunk, (zero, zero))
        dab_ref[0:1, :] = da_log
        dab_ref[1:2, :] = ddt

    return pl.pallas_call(
        body, grid=(1,),
        in_specs=[pl.BlockSpec((S, LANES), lambda i: (0, 7 * nh)), pl.BlockSpec((2, LANES), lambda i: (0, 0)),
                  pl.BlockSpec((S, LANES), lambda i: (0, 0))],
        out_specs=[pl.BlockSpec((S, LANES), lambda i: (0, 0)), pl.BlockSpec((2, LANES), lambda i: (0, 0))],
        out_shape=[jax.ShapeDtypeStruct((S, LANES), BF16), jax.ShapeDtypeStruct((2, LANES), F32)],
        compiler_params=_params(("arbitrary",)), name=name)(proj, ab, dgt)


def _unit_lower_inverse(lmat):
    C = lmat.shape[0]
    ri = lax.broadcasted_iota(jnp.int32, (C, C), 0)
    ci = lax.broadcasted_iota(jnp.int32, (C, C), 1)
    nmat = -lmat
    p = jnp.where(ri == ci, 1.0, 0.0) + nmat
    for _ in range(int(math.log2(C)) - 1):
        nmat = _dot_hi(nmat, nmat)
        p = p + _dot_hi(p, nmat)
    return p


def _gdn_chunk_common(q, k, v, gates, gc_row, h, nh):
    C = CHUNK
    lane = lax.broadcasted_iota(jnp.int32, (C, LANES), 1)
    beta = jnp.sum(jnp.where(lane == h, gates, 0.0), axis=1, keepdims=True)
    gc = jnp.sum(jnp.where(lane == 2 * nh + h, gates, 0.0), axis=1, keepdims=True)
    ri = lax.broadcasted_iota(jnp.int32, (C, C), 0)
    ci = lax.broadcasted_iota(jnp.int32, (C, C), 1)
    incl, strict = ri >= ci, ri > ci
    decay = jnp.where(incl, jnp.exp(jnp.where(incl, gc - gc_row, 0.0)), 0.0)
    egc = jnp.exp(gc)
    kb, vb = k * beta, v * beta
    lmat = jnp.where(strict, _dot(kb, k, _NT) * decay, 0.0)
    tinv = _unit_lower_inverse(lmat)
    kbg = kb * egc
    u = _dot(tinv, vb)
    w = _dot(tinv, kbg)
    amat = _dot(q, k, _NT) * decay
    glast = gc[C - 1:C, :]
    ekt = jnp.exp(glast - gc)
    return dict(q=q, k=k, v=v, beta=beta, decay=decay, egc=egc, kb=kb, vb=vb, lmat=lmat, tinv=tinv, kbg=kbg, u=u, w=w,
                amat=amat, qd=q * egc, ekt=ekt, kt=k * ekt, cd=jnp.exp(glast), strict=strict, incl=incl)


def _gdn_chunk_specs(nh, nb, cpb, rev):
    C = CHUNK
    cn = (lambda n: nb - 1 - n) if rev else (lambda n: n)
    return [pl.BlockSpec((3, cpb * C, HEAD_DIM), lambda h, n: (0, cn(n), h)),
            pl.BlockSpec((cpb * C, LANES), lambda h, n: (cn(n), 0)),
            pl.BlockSpec((None, cpb, 1, C), lambda h, n: (h, cn(n), 0, 0))]


def _gdn_chunk_fwd(qkv, gates, gc_row, *, name):
    _, S, GW = qkv.shape
    nh, C = GW // HEAD_DIM, CHUNK
    nc = S // C
    cpb = min(GDN_CPB, nc)
    nb = nc // cpb

    def body(qkv_ref, gates_ref, gr_ref, o_ref, st_ref, state):
        @pl.when(pl.program_id(1) == 0)
        def _():
            state[...] = jnp.zeros_like(state)

        h = pl.program_id(0)
        s0 = state[...]
        for b in range(cpb):
            rows = slice(b * C, (b + 1) * C)
            m = _gdn_chunk_common(qkv_ref[0, rows, :], qkv_ref[1, rows, :], qkv_ref[2, rows, :], gates_ref[rows, :],
                                  gr_ref[b], h, nh)
            st_ref[b] = s0
            v_new = m["u"] - _dot(m["w"], s0)
            o_ref[rows, :] = _dot(m["qd"], s0) + _dot(m["amat"], v_new)
            s0 = s0 * m["cd"] + _dot(m["kt"], v_new, _TN)
        state[...] = s0

    return pl.pallas_call(
        body, grid=(nh, nb), in_specs=_gdn_chunk_specs(nh, nb, cpb, False),
        out_specs=[pl.BlockSpec((cpb * C, HEAD_DIM), lambda h, n: (n, h)),
                   pl.BlockSpec((None, cpb, HEAD_DIM, HEAD_DIM), lambda h, n: (h, n, 0, 0))],
        out_shape=[jax.ShapeDtypeStruct((S, GW), F32), jax.ShapeDtypeStruct((nh, nc, HEAD_DIM, HEAD_DIM), F32)],
        scratch_shapes=[pltpu.VMEM((HEAD_DIM, HEAD_DIM), F32)],
        compiler_params=_params(("parallel", "arbitrary")), name=name)(qkv, gates, gc_row)


def _gdn_chunk_bwd(qkv, gates, gc_row, states, do, *, name):
    _, S, GW = qkv.shape
    nh, C = GW // HEAD_DIM, CHUNK
    nc = S // C
    cpb = min(GDN_CPB, nc)
    nb = nc // cpb

    def body(qkv_ref, gates_ref, gr_ref, st_ref, do_ref, dqkv_ref, dgt_ref, dstate):
        @pl.when(pl.program_id(1) == 0)
        def _():
            dstate[...] = jnp.zeros_like(dstate)

        h = pl.program_id(0)
        dsn = dstate[...]
        for b in reversed(range(cpb)):
            rows = slice(b * C, (b + 1) * C)
            m = _gdn_chunk_common(qkv_ref[0, rows, :], qkv_ref[1, rows, :], qkv_ref[2, rows, :], gates_ref[rows, :],
                                  gr_ref[b], h, nh)
            dq, dk, dv, dgt, dsn = _gdn_chunk_grad(m, st_ref[b], dsn, do_ref[rows, :], h, nh)
            dqkv_ref[0, rows, :] = dq
            dqkv_ref[1, rows, :] = dk
            dqkv_ref[2, rows, :] = dv
            dgt_ref[rows, :] = dgt
        dstate[...] = dsn

    rc = lambda n: nb - 1 - n
    return pl.pallas_call(
        body, grid=(nh, nb),
        in_specs=_gdn_chunk_specs(nh, nb, cpb, True) + [
            pl.BlockSpec((None, cpb, HEAD_DIM, HEAD_DIM), lambda h, n: (h, rc(n), 0, 0)),
            pl.BlockSpec((cpb * C, HEAD_DIM), lambda h, n: (rc(n), h))],
        out_specs=[pl.BlockSpec((3, cpb * C, HEAD_DIM), lambda h, n: (0, rc(n), h)),
                   pl.BlockSpec((None, cpb * C, LANES), lambda h, n: (h, rc(n), 0))],
        out_shape=[jax.ShapeDtypeStruct((3, S, GW), F32), jax.ShapeDtypeStruct((nh, S, LANES), F32)],
        scratch_shapes=[pltpu.VMEM((HEAD_DIM, HEAD_DIM), F32)],
        compiler_params=_params(("parallel", "arbitrary")), name=name)(qkv, gates, gc_row, states, do)


def _gdn_chunk_grad(m, s0, dsn, dout, h, nh):
    C = CHUNK
    q, k, v, beta, decay, egc = m["q"], m["k"], m["v"], m["beta"], m["decay"], m["egc"]
    tinv, kt, cd = m["tinv"], m["kt"], m["cd"]
    v_new = m["u"] - _dot(m["w"], s0)
    dvn = _dot(m["amat"], dout, _TN) + _dot(kt, dsn)
    dqd = _dot(dout, s0, _NT)
    damat = jnp.where(m["incl"], _dot(dout, v_new, _NT), 0.0)
    dkt = _dot(v_new, dsn, _NT)
    dcd = jnp.sum(dsn * s0)
    ds0 = _dot(m["qd"], dout, _TN) + dsn * cd - _dot(m["w"], dvn, _TN)
    dw = -_dot(dvn, s0, _NT)
    dvb = _dot(tinv, dvn, _TN)
    dkbg = _dot(tinv, dw, _TN)
    dtinv = _dot(dvn, m["vb"], _NT) + _dot(dw, m["kbg"], _NT)
    dl = jnp.where(m["strict"], -_dot_hi(_dot_hi(tinv, dtinv, _TN), tinv, _NT), 0.0)
    dkk = dl * decay
    dqk = damat * decay
    dkb = _dot(dkk, k) + dkbg * egc
    dk = _dot(dkk, m["kb"], _TN) + _dot(dqk, q, _TN) + dkt * m["ekt"] + dkb * beta
    dq = _dot(dqk, k) + dqd * egc
    mm = dl * m["lmat"] + damat * m["amat"]
    ones = jnp.ones((C, LANES), F32)
    rk = jnp.sum(dkt * kt, axis=1, keepdims=True)
    dgc = (_dot_hi(mm, ones) - _dot_hi(mm, ones, _TN) + jnp.sum(dqd * m["qd"], axis=1, keepdims=True) - rk
           + jnp.sum(dkbg * m["kbg"], axis=1, keepdims=True))
    dglast = jnp.sum(rk) + dcd * cd
    rowi = lax.broadcasted_iota(jnp.int32, (C, LANES), 0)
    lane = lax.broadcasted_iota(jnp.int32, (C, LANES), 1)
    dgc = dgc + jnp.where(rowi == C - 1, dglast, 0.0)
    dbeta = jnp.sum(dkb * k, axis=1, keepdims=True) + jnp.sum(dvb * v, axis=1, keepdims=True)
    dgt = jnp.where(lane == h, dbeta, 0.0) + jnp.where(lane == 2 * nh + h, dgc, 0.0)
    return dq, dk, dvb * beta, dgt, ds0


def _gdn_post_fwd(o, proj, ng, nh, *, name):
    S, GW = o.shape

    def body(o_ref, z_ref, g_ref, y_ref):
        ov, z = o_ref[...], z_ref[...]
        rstd = lax.rsqrt(jnp.mean(ov * ov, axis=-1, keepdims=True) + EPS)
        y_ref[...] = (ov * rstd * g_ref[...] * (z * _sigmoid(z))).astype(y_ref.dtype)

    blk = pl.BlockSpec((S, HEAD_DIM), lambda h: (0, h))
    return pl.pallas_call(
        body, grid=(nh,), in_specs=[blk, pl.BlockSpec((S, HEAD_DIM), lambda h: (0, 6 * nh + h)), pl.BlockSpec((1, HEAD_DIM), lambda h: (0, 0))],
        out_specs=blk, out_shape=jax.ShapeDtypeStruct((S, GW), BF16),
        compiler_params=_params(("parallel",)), name=name)(o, proj, ng)


def _gdn_post_bwd(o, proj, ng, dmix, nh, *, name):
    S, GW = o.shape

    def body(o_ref, z_ref, g_ref, d_ref, do_ref, dz_ref, dg_ref):
        ov, z, d = o_ref[...], z_ref[...], d_ref[...].astype(F32)
        rstd = lax.rsqrt(jnp.mean(ov * ov, axis=-1, keepdims=True) + EPS)
        oh = ov * rstd
        sz = _sigmoid(z)
        dy = d * (z * sz)
        dz_ref[...] = (d * (oh * g_ref[...]) * (sz * (1.0 + z * (1.0 - sz)))).astype(dz_ref.dtype)
        t = dy * g_ref[...]
        do_ref[...] = rstd * (t - oh * jnp.mean(t * oh, axis=-1, keepdims=True))
        part = jnp.sum(dy * oh, axis=0, keepdims=True)

        @pl.when(pl.program_id(0) == 0)
        def _():
            dg_ref[...] = part

        @pl.when(pl.program_id(0) > 0)
        def _():
            dg_ref[...] += part

    blk = pl.BlockSpec((S, HEAD_DIM), lambda h: (0, h))
    vec = pl.BlockSpec((1, HEAD_DIM), lambda h: (0, 0))
    return pl.pallas_call(
        body, grid=(nh,),
        in_specs=[blk, pl.BlockSpec((S, HEAD_DIM), lambda h: (0, 6 * nh + h)), vec, pl.BlockSpec((S, HEAD_DIM), lambda h: (0, nh + h))],
        out_specs=[blk, blk, vec],
        out_shape=[jax.ShapeDtypeStruct((S, GW), F32), jax.ShapeDtypeStruct((S, GW), BF16), jax.ShapeDtypeStruct((1, HEAD_DIM), F32)],
        compiler_params=_params(("arbitrary",)), name=name)(o, proj, ng, dmix)


def _gdn_forward(proj, conv_w, ab, ng, nh, tag):
    S = proj.shape[0]
    nc = S // CHUNK
    qkv = _gdn_qkv_fwd(proj, conv_w, nh, name=f"gdn_qkv_fwd{tag}")
    gates = _gdn_gates_fwd(proj, ab, nh, name=f"gdn_gates_fwd{tag}")
    gc_row = gates[:, 2 * nh:3 * nh].T.reshape(nh, nc, 1, CHUNK)
    o, states = _gdn_chunk_fwd(qkv, gates, gc_row, name=f"gdn_chunk_fwd{tag}")
    y = _gdn_post_fwd(o, proj, ng, nh, name=f"gdn_post_fwd{tag}")
    return y, (qkv, gates, gc_row, states, o)


def _gdn_backward(proj, conv_w, ab, ng, saved, dmix, nh, tag):
    qkv, gates, gc_row, states, o = saved
    do, dz, dng = _gdn_post_bwd(o, proj, ng, dmix, nh, name=f"gdn_post_bwd{tag}")
    dqkv, dgt_heads = _gdn_chunk_bwd(qkv, gates, gc_row, states, do, name=f"gdn_chunk_bwd{tag}")
    dx_qkv, dconv = _gdn_qkv_bwd(proj, conv_w, dqkv, nh, name=f"gdn_qkv_bwd{tag}")
    dx_g, dab = _gdn_gates_bwd(proj, ab, jnp.sum(dgt_heads, axis=0), nh, name=f"gdn_gates_bwd{tag}")
    return dx_qkv, dz, dx_g, dconv, dab, dng


def _row_tile(r, cap=128):
    t = cap
    while r % t:
        t //= 2
    assert t >= 8, r
    return t


def _adamw(w, g, m, v, *, name, halves=False):
    L, r, c = w.shape
    tr = _row_tile(r // 2 if halves else r)
    nbh = (r // 2) // tr

    def body(w_ref, g_ref, m_ref, v_ref, g_out, d_ref, m2_ref, v2_ref):
        gv = g_ref[...]
        g_out[...] = gv
        m2 = ADAM_B1 * m_ref[...] + (1.0 - ADAM_B1) * gv
        v2 = ADAM_B2 * v_ref[...] + (1.0 - ADAM_B2) * (gv * gv)
        m_hat = m2 / (1.0 - ADAM_B1 ** ADAM_STEP)
        v_hat = v2 / (1.0 - ADAM_B2 ** ADAM_STEP)
        d_ref[...] = -ADAM_LR * (m_hat / (jnp.sqrt(v_hat) + ADAM_EPS) + ADAM_WD * w_ref[...])
        m2_ref[...] = m2
        v2_ref[...] = v2

    blk = pl.BlockSpec((None, tr, c), lambda l, i: (l, i, 0))
    g_blk = pl.BlockSpec((None, None, tr, c), lambda l, i: (i // nbh, l, i % nbh, 0)) if halves else blk
    o = jax.ShapeDtypeStruct(w.shape, F32)
    return pl.pallas_call(
        body, grid=(L, r // tr), in_specs=[blk, g_blk, blk, blk], out_specs=[blk] * 4, out_shape=[o, o, o, o],
        compiler_params=_params(("parallel", "parallel")), name=name)(w, g, m, v)


def _sum_half(g, rbuf, cvec, *, name):
    _, r, c = g.shape
    h = r // 2
    tr = _row_tile(h)
    nb = h // tr

    def body(c_ref, g_ref, r_ref, o_ref):
        o_ref[...] = (g_ref[...] + r_ref[...]).astype(o_ref.dtype)

    blk = pl.BlockSpec((None, tr, c), lambda s, i, c_ref: (s, i, 0))
    return pl.pallas_call(
        body,
        grid_spec=pltpu.PrefetchScalarGridSpec(
            num_scalar_prefetch=1, grid=(4, nb),
            in_specs=[pl.BlockSpec((None, tr, c), lambda s, i, c_ref: (s, c_ref[0] * nb + i, 0)), blk], out_specs=blk),
        out_shape=jax.ShapeDtypeStruct((4, h, c), BF16),
        compiler_params=_params(("parallel", "parallel")), name=name)(cvec, g, rbuf)


def _sum_chips(rb, prev, l, nl, *, name):
    _, h, c = rb.shape
    tr = _row_tile(h)

    def body(*refs):
        r_ref, o_ref = refs[0], refs[-1]
        acc = r_ref[0].astype(F32)
        for s in range(1, 4):
            acc = acc + r_ref[s].astype(F32)
        o_ref[...] = acc

    in_specs = [pl.BlockSpec((4, tr, c), lambda i: (0, i, 0))]
    args = [rb]
    if prev is not None:
        in_specs.append(pl.BlockSpec(memory_space=pltpu.HBM))
        args.append(prev)
    return pl.pallas_call(
        body, grid=(h // tr,), in_specs=in_specs, out_specs=pl.BlockSpec((None, tr, c), lambda i: (l, i, 0)),
        out_shape=jax.ShapeDtypeStruct((nl, h, c), F32), input_output_aliases={1: 0} if prev is not None else {},
        compiler_params=_params(("parallel",)), name=name)(*args)


_MESH = pl.DeviceIdType.MESH
_HBM = pl.BlockSpec(memory_space=pltpu.HBM)


def _place():
    x, y, c = lax.axis_index("x"), lax.axis_index("y"), lax.axis_index("c")
    return x, y, c, [(1 - x, y), (x, 1 - y), (1 - x, 1 - y)]


def _gather_weights(ws, l, *, name):
    n = len(ws)

    def body(*refs):
        w_refs, o_refs = refs[:n], refs[n:2 * n]
        ici_s, ici_r, fwd_s, fwd_r, loc = refs[2 * n:]
        x, y, c, chips = _place()
        k = 2 * x + y
        started, local = [], []
        for a in range(n):
            h = w_refs[a].shape[1] // 2
            mine = pltpu.make_async_copy(w_refs[a].at[l], o_refs[a].at[k], loc.at[a])
            mine.start()
            local.append(mine)
            for j, (cx, cy) in enumerate(chips):
                cp = pltpu.make_async_remote_copy(
                    src_ref=w_refs[a].at[l, pl.ds(c * h, h), :], dst_ref=o_refs[a].at[k, pl.ds(c * h, h), :],
                    send_sem=ici_s.at[3 * a + j], recv_sem=ici_r.at[3 * a + j], device_id=(cx, cy, c), device_id_type=_MESH)
                cp.start()
                started.append(cp)
        for a in range(n):
            h = w_refs[a].shape[1] // 2
            for j, (cx, cy) in enumerate(chips):
                landed = o_refs[a].at[2 * cx + cy, pl.ds(c * h, h), :]
                pltpu.make_async_remote_copy(
                    src_ref=landed, dst_ref=landed, send_sem=ici_s.at[3 * a + j], recv_sem=ici_r.at[3 * a + j],
                    device_id=(x, y, c), device_id_type=_MESH).wait_recv()
                fw = pltpu.make_async_remote_copy(
                    src_ref=landed, dst_ref=landed, send_sem=fwd_s.at[3 * a + j], recv_sem=fwd_r.at[3 * a + j],
                    device_id=(x, y, 1 - c), device_id_type=_MESH)
                fw.start()
                started.append(fw)
        for a in range(n):
            h = w_refs[a].shape[1] // 2
            for j, (cx, cy) in enumerate(chips):
                other = o_refs[a].at[2 * cx + cy, pl.ds((1 - c) * h, h), :]
                pltpu.make_async_remote_copy(
                    src_ref=other, dst_ref=other, send_sem=fwd_s.at[3 * a + j], recv_sem=fwd_r.at[3 * a + j],
                    device_id=(x, y, c), device_id_type=_MESH).wait_recv()
        for cp in started:
            cp.wait_send()
        for cp in local:
            cp.wait()

    return pl.pallas_call(
        body, in_specs=[_HBM] * n, out_specs=[_HBM] * n,
        out_shape=[jax.ShapeDtypeStruct((4,) + w.shape[1:], w.dtype) for w in ws],
        scratch_shapes=[pltpu.SemaphoreType.DMA((3 * n,))] * 4 + [pltpu.SemaphoreType.DMA((n,))],
        name=name)(*ws)


def _exchange_sibling(gs, *, name):
    n = len(gs)

    def body(*refs):
        g_refs, o_refs = refs[:n], refs[n:2 * n]
        send, recv = refs[2 * n:]
        x, y, c, _ = _place()
        cps = []
        for a in range(n):
            h = g_refs[a].shape[1] // 2
            cp = pltpu.make_async_remote_copy(
                src_ref=g_refs[a].at[:, pl.ds((1 - c) * h, h), :], dst_ref=o_refs[a], send_sem=send.at[a], recv_sem=recv.at[a],
                device_id=(x, y, 1 - c), device_id_type=_MESH)
            cp.start()
            cps.append(cp)
        for cp in cps:
            cp.wait_recv()
        for cp in cps:
            cp.wait_send()

    return pl.pallas_call(
        body, in_specs=[_HBM] * n, out_specs=[_HBM] * n,
        out_shape=[jax.ShapeDtypeStruct((4, g.shape[1] // 2, g.shape[2]), g.dtype) for g in gs],
        scratch_shapes=[pltpu.SemaphoreType.DMA((n,))] * 2, name=name)(*gs)


def _exchange_chips(ps, *, name):
    n = len(ps)

    def body(*refs):
        p_refs, o_refs = refs[:n], refs[n:2 * n]
        send, recv, loc = refs[2 * n:]
        x, y, c, chips = _place()
        k = 2 * x + y
        cps, local = [], []
        for a in range(n):
            mine = pltpu.make_async_copy(p_refs[a].at[k], o_refs[a].at[k], loc.at[a])
            mine.start()
            local.append(mine)
            for j, (cx, cy) in enumerate(chips):
                cp = pltpu.make_async_remote_copy(
                    src_ref=p_refs[a].at[2 * cx + cy], dst_ref=o_refs[a].at[k], send_sem=send.at[3 * a + j],
                    recv_sem=recv.at[3 * a + j], device_id=(cx, cy, c), device_id_type=_MESH)
                cp.start()
                cps.append(cp)
        for a in range(n):
            for j, (cx, cy) in enumerate(chips):
                slot = o_refs[a].at[2 * cx + cy]
                pltpu.make_async_remote_copy(
                    src_ref=slot, dst_ref=slot, send_sem=send.at[3 * a + j], recv_sem=recv.at[3 * a + j],
                    device_id=(x, y, c), device_id_type=_MESH).wait_recv()
        for cp in cps:
            cp.wait_send()
        for cp in local:
            cp.wait()

    return pl.pallas_call(
        body, in_specs=[_HBM] * n, out_specs=[_HBM] * n,
        out_shape=[jax.ShapeDtypeStruct(p.shape, p.dtype) for p in ps],
        scratch_shapes=[pltpu.SemaphoreType.DMA((3 * n,))] * 2 + [pltpu.SemaphoreType.DMA((n,))], name=name)(*ps)


def _share_halves(gs, *, name):
    n = len(gs)

    def body(*refs):
        g_refs, o_refs = refs[:n], refs[n:2 * n]
        send, recv, loc = refs[2 * n:]
        x, y, c, _ = _place()
        cps, local = [], []
        for a in range(n):
            mine = pltpu.make_async_copy(g_refs[a], o_refs[a].at[c], loc.at[a])
            mine.start()
            local.append(mine)
            cp = pltpu.make_async_remote_copy(
                src_ref=g_refs[a], dst_ref=o_refs[a].at[c], send_sem=send.at[a], recv_sem=recv.at[a],
                device_id=(x, y, 1 - c), device_id_type=_MESH)
            cp.start()
            cps.append(cp)
        for a in range(n):
            theirs = o_refs[a].at[1 - c]
            pltpu.make_async_remote_copy(
                src_ref=theirs, dst_ref=theirs, send_sem=send.at[a], recv_sem=recv.at[a],
                device_id=(x, y, c), device_id_type=_MESH).wait_recv()
        for cp in cps:
            cp.wait_send()
        for cp in local:
            cp.wait()

    return pl.pallas_call(
        body, in_specs=[_HBM] * n, out_specs=[_HBM] * n,
        out_shape=[jax.ShapeDtypeStruct((2,) + g.shape, g.dtype) for g in gs],
        scratch_shapes=[pltpu.SemaphoreType.DMA((n,))] * 3, name=name)(*gs)


def _allreduce_small(v, *, name):
    R = v.shape[0]

    def body(v_ref, o_ref, buf, send, recv, loc):
        x, y, c = lax.axis_index("x"), lax.axis_index("y"), lax.axis_index("c")
        me = 4 * x + 2 * y + c
        mine = pltpu.make_async_copy(v_ref, buf.at[me], loc)
        mine.start()
        cps = []
        for d in range(1, 8):
            px = 1 - x if d & 4 else x
            py = 1 - y if d & 2 else y
            pc = 1 - c if d & 1 else c
            cp = pltpu.make_async_remote_copy(
                src_ref=v_ref, dst_ref=buf.at[me], send_sem=send.at[d - 1], recv_sem=recv.at[d - 1],
                device_id=(px, py, pc), device_id_type=_MESH)
            cp.start()
            cps.append((cp, 4 * px + 2 * py + pc))
        for d in range(1, 8):
            cp, peer = cps[d - 1]
            pltpu.make_async_remote_copy(
                src_ref=buf.at[peer], dst_ref=buf.at[peer], send_sem=send.at[d - 1], recv_sem=recv.at[d - 1],
                device_id=(x, y, c), device_id_type=_MESH).wait_recv()
        for cp, _ in cps:
            cp.wait_send()
        mine.wait()
        acc = buf[0]
        for i in range(1, 8):
            acc = acc + buf[i]
        o_ref[...] = acc

    return pl.pallas_call(
        body, in_specs=[pl.BlockSpec(memory_space=pltpu.VMEM)], out_specs=pl.BlockSpec(memory_space=pltpu.VMEM),
        out_shape=jax.ShapeDtypeStruct((R, LANES), F32),
        scratch_shapes=[pltpu.VMEM((8, R, LANES), F32), pltpu.SemaphoreType.DMA((7,)), pltpu.SemaphoreType.DMA((7,)),
                        pltpu.SemaphoreType.DMA],
        compiler_params=pltpu.CompilerParams(vmem_limit_bytes=VMEM_LIMIT), name=name)(v)


def _pack(arrs, row_multiple=8):
    rows = []
    for a in arrs:
        flat = a.reshape(-1)
        flat = jnp.pad(flat, (0, (-flat.shape[0]) % LANES))
        rows.append(flat.reshape(-1, LANES))
    buf = jnp.concatenate(rows, axis=0)
    return jnp.pad(buf, ((0, (-buf.shape[0]) % row_multiple), (0, 0)))


def _unpack(buf, shapes):
    out, r = [], 0
    for s in shapes:
        size = math.prod(s)
        nr = -(-size // LANES)
        out.append(buf[r:r + nr].reshape(-1)[:size].reshape(s))
        r += nr
    return out


def kernel(x, mem, mix_norm, w_in, gdn_conv, gdn_a_log, gdn_dt_bias, gdn_norm, w_out, xattn_norm, mem_norm, w_xq, w_xkv, w_xo, ffn_norm, w_up, ffn_conv, ffn_conv_bias, w_down, final_norm, loss_target, m_mix_norm, m_w_in, m_gdn_conv, m_gdn_a_log, m_gdn_dt_bias, m_gdn_norm, m_w_out, m_xattn_norm, m_mem_norm, m_w_xq, m_w_xkv, m_w_xo, m_ffn_norm, m_w_up, m_ffn_conv, m_ffn_conv_bias, m_w_down, m_final_norm, v_mix_norm, v_w_in, v_gdn_conv, v_gdn_a_log, v_gdn_dt_bias, v_gdn_norm, v_w_out, v_xattn_norm, v_mem_norm, v_w_xq, v_w_xkv, v_w_xo, v_ffn_norm, v_w_up, v_ffn_conv, v_ffn_conv_bias, v_w_down, v_final_norm):
    L = w_in.shape[0]
    _, S, D = x.shape
    nh = D // (2 * HEAD_DIM)
    GW = nh * HEAD_DIM
    n_in = 7 * GW + 2 * nh
    NP = 7 * GW + LANES
    XW = X_HEADS * HEAD_DIM
    F = w_down.shape[1] * 4
    cs_in = w_in.shape[2]
    cs_up = w_up.shape[2]
    cs_xo = w_xo.shape[2]
    tu = _tile(cs_up, 1408)
    per = cs_up // tu
    fper = F // tu
    assert n_in == 4 * cs_in and F % tu == 0 and 2 * F == 4 * cs_up

    xi, yi, ci = lax.axis_index("x"), lax.axis_index("y"), lax.axis_index("c")
    chip = 2 * xi + yi
    cvec = jnp.reshape(ci, (1,)).astype(jnp.int32)

    cs_gc, cs_fc = gdn_conv.shape[2], ffn_conv.shape[2]
    keep = jnp.where(ci == 0, 1.0, 0.0).astype(F32)
    gc_full = lax.dynamic_update_slice(jnp.zeros((L, SHORT_CONV, 4 * cs_gc), F32), gdn_conv * keep, (0, 0, chip * cs_gc))
    fc_full = lax.dynamic_update_slice(jnp.zeros((L, FFN_CONV, 4 * cs_fc), F32), ffn_conv * keep, (0, 0, chip * cs_fc))
    conv_all = _allreduce_small(_pack([gc_full, fc_full]), name="allgather_conv")
    gdn_conv_full, ffn_conv_full = _unpack(conv_all, [gc_full.shape, fc_full.shape])

    big = [w_in, w_out, w_xq, w_xkv, w_xo, w_up, w_down]
    big_bf16 = [w.astype(BF16) for w in big]
    ab = jnp.zeros((L, 2, LANES), F32).at[:, 0, nh:2 * nh].set(gdn_a_log).at[:, 1, nh:2 * nh].set(gdn_dt_bias)

    def vec(p, l):
        return p[l:l + 1]

    xo_fwd_b = pl.BlockSpec((None, XW, cs_xo), lambda i, j, k: (j, 0, 0))
    xo_dg_b = pl.BlockSpec((None, XW, cs_xo), lambda i, j, k: (k, 0, 0))
    xo_wg_o = pl.BlockSpec((None, XW, cs_xo), lambda i, j, k: (j, 0, 0))
    up_fwd_b = pl.BlockSpec((None, D, tu), lambda i, j, k: (j // per, 0, j % per))

    def fwd_layer(l, xc, wts):
        g_in, g_out, g_xq, g_xkv, g_xo, g_up, g_down = wts
        w_in_l = jnp.concatenate([g_in[0], g_in[1], g_in[2], g_in[3], jnp.zeros((D, NP - n_in), BF16)], axis=1)
        w_out_l, w_xq_l, w_xkv_l, w_down_l = g_out.reshape(2 * GW, D), g_xq.reshape(D, XW), g_xkv.reshape(D, 2 * XW), g_down.reshape(F, D)
        s = dict(x0=xc, w_in=w_in_l, w_out=w_out_l, w_xq=w_xq_l, w_xkv=w_xkv_l, w_xo=g_xo, w_up=g_up, w_down=w_down_l)
        s["h"] = _rms_fwd(xc, vec(mix_norm, l), name="rms_mix_fwd")
        s["proj"] = _matmul(s["h"], w_in_l, tn=2432, tk=D, name="mm_in_fwd")
        s["sb"], s["tot"] = _sb_fwd(s["proj"], nh, name="sb_fwd")
        gdn_out, s["gdn"] = _gdn_forward(s["proj"], gdn_conv_full[l], ab[l], vec(gdn_norm, l), nh, "")
        s["mixed"] = jnp.concatenate([s["sb"], gdn_out], axis=1)
        s["x1"] = _matmul(s["mixed"], w_out_l, res=xc, tk=2 * GW, name="mm_out_fwd")
        s["memn"] = _rms_fwd(mem[0], vec(mem_norm, l), name="rms_mem_fwd")
        s["kv"] = _matmul(s["memn"], w_xkv_l, out_dtype=BF16, tk=D, name="mm_xkv_fwd")
        s["hq"] = _rms_fwd(s["x1"], vec(xattn_norm, l), name="rms_xattn_fwd")
        s["q"] = _matmul(s["hq"], w_xq_l, out_dtype=BF16, tk=D, name="mm_xq_fwd")
        s["xo"] = _xattn_fwd(s["q"], s["kv"], name="xattn_fwd")
        s["x2"] = _matmul(s["xo"], g_xo, res=s["x1"], dims=(S, D, XW), tn=cs_xo, tk=XW, b_spec=xo_fwd_b, name="mm_xo_fwd")
        s["hf"] = _rms_fwd(s["x2"], vec(ffn_norm, l), name="rms_ffn_fwd")
        s["u"] = _matmul(s["hf"], g_up, dims=(S, 2 * F, D), tn=tu, tk=D, b_spec=up_fwd_b, name="mm_up_fwd")
        s["act"] = _ffn_act_fwd(s["u"], ffn_conv_full[l], ffn_conv_bias[l:l + 1], name="ffn_act_fwd")
        x3 = _matmul(s["act"], w_down_l, res=s["x2"], tk=tu, name="mm_down_fwd")
        return x3, s

    def bwd_layer(l, s, dx3):
        dact = _matmul(dx3, s["w_down"], tb=True, tk=D, name="mm_down_dgrad")
        d_down = _matmul(s["act"], dx3, ta=True, tk=S, name="mm_down_wgrad")
        du3, dcw3, dcb3 = _ffn_act_bwd(s["u"], ffn_conv_full[l], ffn_conv_bias[l:l + 1], dact, name="ffn_act_bwd")
        dhf = _matmul(du3, s["w_up"], tb=True, dims=(S, D, 2 * F), tk=tu,
                      a_spec=pl.BlockSpec((None, _tile(S, 512), tu), lambda i, j, k: (k // fper, i, k % fper)),
                      b_spec=pl.BlockSpec((None, _tile(D, 512), tu), lambda i, j, k: (k // per, j, k % per)), name="mm_up_dgrad")
        d_up = _matmul(s["hf"], du3, ta=True, dims=(D, 2 * F, S), tn=tu, tk=S,
                       b_spec=pl.BlockSpec((None, S, tu), lambda i, j, k: (j // fper, 0, j % fper)),
                       o_spec=pl.BlockSpec((None, _tile(D, 512), tu), lambda i, j, k: (j // per, i, j % per)),
                       out_shape=jax.ShapeDtypeStruct((4, D, cs_up), F32), name="mm_up_wgrad")
        dx2, dg_ffn = _rms_bwd(s["x2"], vec(ffn_norm, l), dhf, dx3, name="rms_bwd")
        dxo = _matmul(dx2, s["w_xo"], tb=True, dims=(S, XW, D), tn=XW, tk=cs_xo, b_spec=xo_dg_b, name="mm_xo_dgrad")
        d_xo = _matmul(s["xo"], dx2, ta=True, dims=(XW, D, S), tm=XW, tn=cs_xo, tk=S, o_spec=xo_wg_o,
                       out_shape=jax.ShapeDtypeStruct((4, XW, cs_xo), F32), name="mm_xo_wgrad")
        dq, dk, dv = _xattn_bwd(s["q"], s["kv"], dxo, name="xattn_bwd")
        dkv = jnp.concatenate([dk, dv], axis=1)
        dhq = _matmul(dq, s["w_xq"], tb=True, tk=XW, name="mm_xq_dgrad")
        d_xq = _matmul(s["hq"], dq, ta=True, tk=S, name="mm_xq_wgrad")
        dmemn = _matmul(dkv, s["w_xkv"], tb=True, tk=2 * XW, name="mm_xkv_dgrad")
        d_xkv = _matmul(s["memn"], dkv, ta=True, tk=mem.shape[1], name="mm_xkv_wgrad")
        _, dg_mem = _rms_bwd(mem[0], vec(mem_norm, l), dmemn, None, name="rms_mem_bwd")
        dx1, dg_xattn = _rms_bwd(s["x1"], vec(xattn_norm, l), dhq, dx2, name="rms_bwd")
        dmix = _matmul(dx1, s["w_out"], tb=True, tk=D, name="mm_out_dgrad")
        d_out = _matmul(s["mixed"], dx1, ta=True, tk=S, name="mm_out_wgrad")
        dq_s, dk_s, dv_s = _sb_bwd(s["proj"], s["tot"], dmix, nh, name="sb_bwd")
        dx_qkv, dz, dx_g, dconv, dab, dng = _gdn_backward(s["proj"], gdn_conv_full[l], ab[l], vec(gdn_norm, l), s["gdn"], dmix, nh, "")
        dproj = jnp.concatenate([dq_s, dk_s, dv_s, dx_qkv, dz, dx_g], axis=1)
        dh = _matmul(dproj, s["w_in"], tb=True, tk=2432, name="mm_in_dgrad")
        d_in = _matmul(s["h"], dproj, ta=True, tn=2432, tk=S, name="mm_in_wgrad")
        dx0, dg_mix = _rms_bwd(s["x0"], vec(mix_norm, l), dh, dx1, name="rms_bwd")
        slabs = [jnp.stack([d_in[:, i * cs_in:(i + 1) * cs_in] for i in range(4)]), d_out.reshape(4, -1, D),
                 d_xq.reshape(4, -1, XW), d_xkv.reshape(4, -1, 2 * XW), d_xo, d_up, d_down.reshape(4, -1, D)]
        small = [dg_mix, dconv, dab, dng, dg_xattn, dg_mem, dg_ffn,
                 jnp.concatenate([dcw3[0], dcw3[1]], axis=1), jnp.concatenate([dcb3[0], dcb3[1]], axis=1)]
        return dx0, slabs, small

    xc = x[0]
    saved = []
    for l in range(L):
        wts = _gather_weights(big_bf16, l, name=f"gather_weights_{l}")
        xc, s = fwd_layer(l, xc, wts)
        saved.append(s)
    loss_blk, dxc, dg_final = _loss_head(xc, final_norm[None, :], loss_target[0], name="loss_head")

    sums = [None] * 7
    small_by_layer = [None] * L
    for l in reversed(range(L)):
        dxc, slabs, small_by_layer[l] = bwd_layer(l, saved[l], dxc)
        saved[l] = None
        from_sibling = _exchange_sibling(slabs, name="reduce_to_sibling")
        partial = [_sum_half(g, r, cvec, name="sum_sibling") for g, r in zip(slabs, from_sibling)]
        from_chips = _exchange_chips(partial, name="reduce_to_chip")
        sums = [_sum_chips(rb, prev, l, L, name=f"sum_chips_{l}") for rb, prev in zip(from_chips, sums)]
    halves = _share_halves(sums, name="share_halves")

    small_flat = [a for l in range(L) for a in small_by_layer[l]] + [dg_final, loss_blk[0:1]]
    red = _unpack(_allreduce_small(_pack(small_flat), name="allreduce_small"), [a.shape for a in small_flat])
    per_layer = [red[9 * l:9 * l + 9] for l in range(L)]
    col = lambda i: jnp.concatenate([p[i] for p in per_layer], axis=0)
    stk = lambda i: jnp.stack([p[i] for p in per_layer])
    g_conv_full, g_ab, g_fconv_full = stk(1), stk(2), stk(7)
    grads_small = dict(
        mix_norm=col(0), gdn_conv=lax.dynamic_slice(g_conv_full, (0, 0, chip * cs_gc), (L, SHORT_CONV, cs_gc)),
        gdn_a_log=g_ab[:, 0, nh:2 * nh], gdn_dt_bias=g_ab[:, 1, nh:2 * nh], gdn_norm=col(3), xattn_norm=col(4),
        mem_norm=col(5), ffn_norm=col(6), ffn_conv=lax.dynamic_slice(g_fconv_full, (0, 0, chip * cs_fc), (L, FFN_CONV, cs_fc)),
        ffn_conv_bias=col(8), final_norm=red[-2][0])
    loss = red[-1][0, 0]

    names_small = ["mix_norm", "gdn_conv", "gdn_a_log", "gdn_dt_bias", "gdn_norm", "xattn_norm", "mem_norm", "ffn_norm",
                   "ffn_conv", "ffn_conv_bias", "final_norm"]
    w_small = dict(mix_norm=mix_norm, gdn_conv=gdn_conv, gdn_a_log=gdn_a_log, gdn_dt_bias=gdn_dt_bias, gdn_norm=gdn_norm,
                   xattn_norm=xattn_norm, mem_norm=mem_norm, ffn_norm=ffn_norm, ffn_conv=ffn_conv, ffn_conv_bias=ffn_conv_bias,
                   final_norm=final_norm)
    m_small = dict(mix_norm=m_mix_norm, gdn_conv=m_gdn_conv, gdn_a_log=m_gdn_a_log, gdn_dt_bias=m_gdn_dt_bias, gdn_norm=m_gdn_norm,
                   xattn_norm=m_xattn_norm, mem_norm=m_mem_norm, ffn_norm=m_ffn_norm, ffn_conv=m_ffn_conv,
                   ffn_conv_bias=m_ffn_conv_bias, final_norm=m_final_norm)
    v_small = dict(mix_norm=v_mix_norm, gdn_conv=v_gdn_conv, gdn_a_log=v_gdn_a_log, gdn_dt_bias=v_gdn_dt_bias, gdn_norm=v_gdn_norm,
                   xattn_norm=v_xattn_norm, mem_norm=v_mem_norm, ffn_norm=v_ffn_norm, ffn_conv=v_ffn_conv,
                   ffn_conv_bias=v_ffn_conv_bias, final_norm=v_final_norm)
    shapes_small = [w_small[n].shape for n in names_small]
    packed = [_pack([d[n] for n in names_small], row_multiple=128)[None] for d in (w_small, grads_small, m_small, v_small)]
    upd_small = [_unpack(o[0], shapes_small) for o in _adamw(*packed, name="adamw_small")[1:]]
    delta, new_m, new_v = [dict(zip(names_small, u)) for u in upd_small]
    grads = dict(grads_small)
    big_names = ["w_in", "w_out", "w_xq", "w_xkv", "w_xo", "w_up", "w_down"]
    big_m = [m_w_in, m_w_out, m_w_xq, m_w_xkv, m_w_xo, m_w_up, m_w_down]
    big_v = [v_w_in, v_w_out, v_w_xq, v_w_xkv, v_w_xo, v_w_up, v_w_down]
    for n, w, g, m, v in zip(big_names, big, halves, big_m, big_v):
        grads[n], delta[n], new_m[n], new_v[n] = _adamw(w, g, m, v, halves=True, name=f"adamw_{n}")

    order = ["mix_norm", "w_in", "gdn_conv", "gdn_a_log", "gdn_dt_bias", "gdn_norm", "w_out", "xattn_norm", "mem_norm", "w_xq",
             "w_xkv", "w_xo", "ffn_norm", "w_up", "ffn_conv", "ffn_conv_bias", "w_down", "final_norm"]
    return (loss, dxc[None], *[grads[n] for n in order], *[delta[n] for n in order], *[new_m[n] for n in order],
            *[new_v[n] for n in order])
```

```python
import functools
import math

import jax
import jax.numpy as jnp
from jax import lax
from jax.experimental import pallas as pl
from jax.experimental.pallas import tpu as pltpu

F32 = jnp.float32
BF16 = jnp.bfloat16

HEAD_DIM = 128
CHUNK = 64
GDN_CPB = 4
SB_TQ, SB_TK = 256, 512
SHORT_CONV = 4
FFN_CONV = 3
X_HEADS = 4
EPS = 1e-6
LANES = 128
VMEM_LIMIT = 56 * 2**20

ADAM_LR, ADAM_B1, ADAM_B2, ADAM_EPS, ADAM_WD, ADAM_STEP = 0.001, 0.9, 0.999, 1e-08, 0.01, 10

HI = lax.Precision.HIGHEST


def _params(sem):
    return pltpu.CompilerParams(dimension_semantics=sem, vmem_limit_bytes=VMEM_LIMIT)


def _tile(n, want):
    if n <= want:
        return n
    t = (want // LANES) * LANES
    while t > LANES and n % t:
        t -= LANES
    assert n % t == 0, (n, want)
    return t


def _sigmoid(x):
    return 1.0 / (1.0 + jnp.exp(-x))


def _softplus(x):
    return jnp.maximum(x, 0.0) + jnp.log(1.0 + jnp.exp(-jnp.abs(x)))


def _matmul(a, b, *, name, ta=False, tb=False, out_dtype=F32, res=None, tm=512, tn=512, tk=2048,
            dims=None, a_spec=None, b_spec=None, o_spec=None, out_shape=None):
    if dims is None:
        M, K = (a.shape[1], a.shape[0]) if ta else a.shape
        N = b.shape[0] if tb else b.shape[1]
    else:
        M, N, K = dims
    tm, tn, tk = _tile(M, tm), _tile(N, tn), _tile(K, tk)
    nk = K // tk
    dn = (((0 if ta else 1,), (1 if tb else 0,)), ((), ()))

    def body(*refs):
        a_ref, b_ref = refs[0], refs[1]
        r_ref = refs[2] if res is not None else None
        o_ref = refs[3] if res is not None else refs[2]
        p = lax.dot_general(a_ref[...].astype(BF16), b_ref[...].astype(BF16), dn, preferred_element_type=F32)

        def finish(acc):
            if r_ref is not None:
                acc = acc + r_ref[...].astype(F32)
            o_ref[...] = acc.astype(o_ref.dtype)

        if nk == 1:
            finish(p)
        else:
            acc_ref = refs[-1]
            k = pl.program_id(2)

            @pl.when(k == 0)
            def _():
                acc_ref[...] = p

            @pl.when(jnp.logical_and(k > 0, k < nk - 1))
            def _():
                acc_ref[...] += p

            @pl.when(k == nk - 1)
            def _():
                finish(acc_ref[...] + p)

    if a_spec is None:
        a_spec = pl.BlockSpec((tk, tm), lambda i, j, k: (k, i)) if ta else pl.BlockSpec((tm, tk), lambda i, j, k: (i, k))
    if b_spec is None:
        b_spec = pl.BlockSpec((tn, tk), lambda i, j, k: (j, k)) if tb else pl.BlockSpec((tk, tn), lambda i, j, k: (k, j))
    if o_spec is None:
        o_spec = pl.BlockSpec((tm, tn), lambda i, j, k: (i, j))
    if out_shape is None:
        out_shape = jax.ShapeDtypeStruct((M, N), out_dtype)
    in_specs, args = [a_spec, b_spec], [a, b]
    if res is not None:
        in_specs.append(pl.BlockSpec((tm, tn), lambda i, j, k: (i, j)))
        args.append(res)
    return pl.pallas_call(
        body, grid=(M // tm, N // tn, nk), in_specs=in_specs, out_specs=o_spec, out_shape=out_shape,
        scratch_shapes=[pltpu.VMEM((tm, tn), F32)] if nk > 1 else [],
        compiler_params=_params(("parallel", "parallel", "arbitrary")), name=name)(*args)


def _rms_fwd(x, g, *, name):
    R, D = x.shape
    tr = _tile(R, 256)

    def body(x_ref, g_ref, o_ref):
        xv = x_ref[...]
        rstd = lax.rsqrt(jnp.mean(xv * xv, axis=-1, keepdims=True) + EPS)
        o_ref[...] = (xv * rstd * g_ref[...]).astype(o_ref.dtype)

    return pl.pallas_call(
        body, grid=(R // tr,), in_specs=[pl.BlockSpec((tr, D), lambda i: (i, 0)), pl.BlockSpec((1, D), lambda i: (0, 0))],
        out_specs=pl.BlockSpec((tr, D), lambda i: (i, 0)), out_shape=jax.ShapeDtypeStruct((R, D), BF16),
        compiler_params=_params(("parallel",)), name=name)(x, g)


def _rms_bwd(x, g, dh, dres, *, name):
    R, D = x.shape
    tr = _tile(R, 256)

    def body(*refs):
        if dres is None:
            x_ref, g_ref, dh_ref, dx_ref, dg_ref = refs
        else:
            x_ref, g_ref, dh_ref, dr_ref, dx_ref, dg_ref = refs
        xv = x_ref[...]
        dhv = dh_ref[...].astype(F32)
        rstd = lax.rsqrt(jnp.mean(xv * xv, axis=-1, keepdims=True) + EPS)
        xhat = xv * rstd
        t = dhv * g_ref[...]
        dx = rstd * (t - xhat * jnp.mean(t * xhat, axis=-1, keepdims=True))
        if dres is not None:
            dx = dx + dr_ref[...]
        dx_ref[...] = dx
        part = jnp.sum(dhv * xhat, axis=0, keepdims=True)

        @pl.when(pl.program_id(0) == 0)
        def _():
            dg_ref[...] = part

        @pl.when(pl.program_id(0) > 0)
        def _():
            dg_ref[...] += part

    row = pl.BlockSpec((tr, D), lambda i: (i, 0))
    vec = pl.BlockSpec((1, D), lambda i: (0, 0))
    in_specs = [row, vec, row] + ([row] if dres is not None else [])
    args = [x, g, dh] + ([dres] if dres is not None else [])
    return pl.pallas_call(
        body, grid=(R // tr,), in_specs=in_specs, out_specs=[row, vec],
        out_shape=[jax.ShapeDtypeStruct((R, D), F32), jax.ShapeDtypeStruct((1, D), F32)],
        compiler_params=_params(("arbitrary",)), name=name)(*args)


def _loss_head(x, g, tgt, *, name):
    R, D = x.shape
    tr = _tile(R, 256)

    def body(x_ref, g_ref, t_ref, l_ref, dx_ref, dg_ref):
        xv = x_ref[...]
        rstd = lax.rsqrt(jnp.mean(xv * xv, axis=-1, keepdims=True) + EPS)
        xhat = xv * rstd
        err = xhat * g_ref[...] - t_ref[...]
        dy = err * (1.0 / D)
        t = dy * g_ref[...]
        dx_ref[...] = rstd * (t - xhat * jnp.mean(t * xhat, axis=-1, keepdims=True))
        part = jnp.sum(dy * xhat, axis=0, keepdims=True)
        lpart = jnp.zeros((8, LANES), F32) + 0.5 * jnp.sum(jnp.mean(err * err, axis=-1, keepdims=True))

        @pl.when(pl.program_id(0) == 0)
        def _():
            dg_ref[...] = part
            l_ref[...] = lpart

        @pl.when(pl.program_id(0) > 0)
        def _():
            dg_ref[...] += part
            l_ref[...] += lpart

    row = pl.BlockSpec((tr, D), lambda i: (i, 0))
    vec = pl.BlockSpec((1, D), lambda i: (0, 0))
    return pl.pallas_call(
        body, grid=(R // tr,), in_specs=[row, vec, row],
        out_specs=[pl.BlockSpec((8, LANES), lambda i: (0, 0)), row, vec],
        out_shape=[jax.ShapeDtypeStruct((8, LANES), F32), jax.ShapeDtypeStruct((R, D), F32), jax.ShapeDtypeStruct((1, D), F32)],
        compiler_params=_params(("arbitrary",)), name=name)(x, g, tgt)


def _shift_down(x, s):
    if s == 0:
        return x
    row = lax.broadcasted_iota(jnp.int32, x.shape, 0)
    return jnp.where(row >= s, pltpu.roll(x, s, 0), 0.0)


def _shift_up(x, s):
    if s == 0:
        return x
    n = x.shape[0]
    row = lax.broadcasted_iota(jnp.int32, x.shape, 0)
    return jnp.where(row < n - s, pltpu.roll(x, n - s, 0), 0.0)


def _dwconv(x, w):
    k = w.shape[0]
    acc = x * w[k - 1:k, :]
    for i in range(k - 1):
        acc = acc + _shift_down(x, k - 1 - i) * w[i:i + 1, :]
    return acc


def _dwconv_bwd(x, w, dc):
    k = w.shape[0]
    dx = dc * w[k - 1:k, :]
    dws = []
    for i in range(k - 1):
        s = k - 1 - i
        dx = dx + _shift_up(dc, s) * w[i:i + 1, :]
        dws.append(jnp.sum(dc * _shift_down(x, s), axis=0, keepdims=True))
    dws.append(jnp.sum(dc * x, axis=0, keepdims=True))
    return dx, jnp.concatenate(dws, axis=0)


def _ffn_act_fwd(u, cw, cb, *, name):
    S, F2 = u.shape
    F = F2 // 2
    tc = _tile(F, 256)
    nb = F // tc

    def body(ug_ref, uu_ref, wg_ref, wu_ref, bg_ref, bu_ref, o_ref):
        cg = _dwconv(ug_ref[...], wg_ref[...]) + bg_ref[...]
        cu = _dwconv(uu_ref[...], wu_ref[...]) + bu_ref[...]
        o_ref[...] = (cg * _sigmoid(cg) * cu).astype(o_ref.dtype)

    blk = lambda r, off: pl.BlockSpec((r, tc), lambda j: (0, j + off))
    return pl.pallas_call(
        body, grid=(nb,), in_specs=[blk(S, 0), blk(S, nb), blk(FFN_CONV, 0), blk(FFN_CONV, nb), blk(1, 0), blk(1, nb)],
        out_specs=blk(S, 0), out_shape=jax.ShapeDtypeStruct((S, F), BF16),
        compiler_params=_params(("parallel",)), name=name)(u, u, cw, cw, cb, cb)


def _ffn_act_bwd(u, cw, cb, dact, *, name):
    S, F2 = u.shape
    F = F2 // 2
    tc = _tile(F, 256)
    nb = F // tc

    def body(ug_ref, uu_ref, wg_ref, wu_ref, bg_ref, bu_ref, da_ref, du_ref, dw_ref, db_ref):
        ug, uu = ug_ref[...], uu_ref[...]
        cg = _dwconv(ug, wg_ref[...]) + bg_ref[...]
        cu = _dwconv(uu, wu_ref[...]) + bu_ref[...]
        sg = _sigmoid(cg)
        da = da_ref[...].astype(F32)
        dcu = da * (cg * sg)
        dcg = da * cu * (sg * (1.0 + cg * (1.0 - sg)))
        dxg, dwg = _dwconv_bwd(ug, wg_ref[...], dcg)
        dxu, dwu = _dwconv_bwd(uu, wu_ref[...], dcu)
        du_ref[0] = dxg.astype(du_ref.dtype)
        du_ref[1] = dxu.astype(du_ref.dtype)
        dw_ref[0] = dwg
        dw_ref[1] = dwu
        db_ref[0] = jnp.sum(dcg, axis=0, keepdims=True)
        db_ref[1] = jnp.sum(dcu, axis=0, keepdims=True)

    blk = lambda r, off: pl.BlockSpec((r, tc), lambda j: (0, j + off))
    blk3 = lambda r: pl.BlockSpec((2, r, tc), lambda j: (0, 0, j))
    return pl.pallas_call(
        body, grid=(nb,),
        in_specs=[blk(S, 0), blk(S, nb), blk(FFN_CONV, 0), blk(FFN_CONV, nb), blk(1, 0), blk(1, nb), blk(S, 0)],
        out_specs=[blk3(S), blk3(FFN_CONV), blk3(1)],
        out_shape=[jax.ShapeDtypeStruct((2, S, F), BF16), jax.ShapeDtypeStruct((2, FFN_CONV, F), F32),
                   jax.ShapeDtypeStruct((2, 1, F), F32)],
        compiler_params=_params(("parallel",)), name=name)(u, u, cw, cw, cb, cb, dact)


def _xattn_fwd(q, kv, *, name):
    S, XW = q.shape
    M = kv.shape[0]
    nh = XW // HEAD_DIM
    tq = _tile(S, 512)
    scale = HEAD_DIM ** -0.5

    def body(q_ref, k_ref, v_ref, o_ref):
        z = lax.dot_general(q_ref[...], k_ref[...], (((1,), (1,)), ((), ())), preferred_element_type=F32) * scale
        e = jnp.exp(z - jnp.max(z, axis=-1, keepdims=True))
        p = e / jnp.sum(e, axis=-1, keepdims=True)
        o_ref[...] = jnp.dot(p.astype(BF16), v_ref[...], preferred_element_type=F32).astype(o_ref.dtype)

    return pl.pallas_call(
        body, grid=(nh, S // tq),
        in_specs=[pl.BlockSpec((tq, HEAD_DIM), lambda h, i: (i, h)), pl.BlockSpec((M, HEAD_DIM), lambda h, i: (0, h)),
                  pl.BlockSpec((M, HEAD_DIM), lambda h, i: (0, nh + h))],
        out_specs=pl.BlockSpec((tq, HEAD_DIM), lambda h, i: (i, h)), out_shape=jax.ShapeDtypeStruct((S, XW), BF16),
        compiler_params=_params(("parallel", "parallel")), name=name)(q, kv, kv)


def _xattn_bwd(q, kv, do, *, name):
    S, XW = q.shape
    M = kv.shape[0]
    nh = XW // HEAD_DIM
    tq = _tile(S, 512)
    scale = HEAD_DIM ** -0.5
    nt = (((1,), (1,)), ((), ()))
    tn = (((0,), (0,)), ((), ()))

    def body(q_ref, k_ref, v_ref, do_ref, dq_ref, dk_ref, dv_ref):
        qv, kvv, vv = q_ref[...], k_ref[...], v_ref[...]
        dov = do_ref[...].astype(BF16)
        z = lax.dot_general(qv, kvv, nt, preferred_element_type=F32) * scale
        e = jnp.exp(z - jnp.max(z, axis=-1, keepdims=True))
        p = e / jnp.sum(e, axis=-1, keepdims=True)
        dp = lax.dot_general(dov, vv, nt, preferred_element_type=F32)
        ds = (p * (dp - jnp.sum(dp * p, axis=-1, keepdims=True)) * scale).astype(BF16)
        dq_ref[...] = jnp.dot(ds, kvv, preferred_element_type=F32).astype(dq_ref.dtype)
        dk = lax.dot_general(ds, qv, tn, preferred_element_type=F32)
        dv = lax.dot_general(p.astype(BF16), dov, tn, preferred_element_type=F32)

        @pl.when(pl.program_id(1) == 0)
        def _():
            dk_ref[...] = dk
            dv_ref[...] = dv

        @pl.when(pl.program_id(1) > 0)
        def _():
            dk_ref[...] += dk
            dv_ref[...] += dv

    qs = pl.BlockSpec((tq, HEAD_DIM), lambda h, i: (i, h))
    ms = pl.BlockSpec((M, HEAD_DIM), lambda h, i: (0, h))
    return pl.pallas_call(
        body, grid=(nh, S // tq),
        in_specs=[qs, ms, pl.BlockSpec((M, HEAD_DIM), lambda h, i: (0, nh + h)), qs],
        out_specs=[qs, ms, ms],
        out_shape=[jax.ShapeDtypeStruct((S, XW), BF16), jax.ShapeDtypeStruct((M, XW), F32), jax.ShapeDtypeStruct((M, XW), F32)],
        compiler_params=_params(("parallel", "arbitrary")), name=name)(q, kv, kv, do)


_NN = (((1,), (0,)), ((), ()))
_NT = (((1,), (1,)), ((), ()))
_TN = (((0,), (0,)), ((), ()))


def _dot(a, b, dn=_NN):
    return lax.dot_general(a.astype(BF16), b.astype(BF16), dn, preferred_element_type=F32)


def _dot_hi(a, b, dn=_NN):
    return lax.dot_general(a, b, dn, preferred_element_type=F32, precision=HI)


def _dot_split(a, b01, dn=_NN):
    hi = a.astype(BF16)
    lo = (a - hi.astype(F32)).astype(BF16)
    return (lax.dot_general(hi, b01, dn, preferred_element_type=F32)
            + lax.dot_general(lo, b01, dn, preferred_element_type=F32))


def _after_matrix(n, transpose=False):
    row = lax.broadcasted_iota(jnp.int32, (n, n), 0)
    col = lax.broadcasted_iota(jnp.int32, (n, n), 1)
    return (row < col if transpose else row > col).astype(BF16)


def _sb_fwd(proj, nh, *, name):
    S = proj.shape[0]
    TQ, TK = min(SB_TQ, S), min(SB_TK, S)
    nq = S // TQ
    scale = HEAD_DIM ** -0.5

    def body(q_ref, k_ref, v_ref, o_ref, tot_ref):
        i = pl.program_id(1)
        q = q_ref[...].astype(BF16)
        qpos = i * TQ + lax.broadcasted_iota(jnp.int32, (TQ, TK), 0)
        kcol = lax.broadcasted_iota(jnp.int32, (TQ, TK), 1)
        after = _after_matrix(TK)
        nt = ((i + 1) * TQ + TK - 1) // TK

        def step(t, carry):
            acc, out = carry
            off = pl.multiple_of((nt - 1 - t) * TK, TK)
            kb = k_ref[pl.ds(off, TK), :].astype(BF16)
            vb = v_ref[pl.ds(off, TK), :].astype(BF16)
            z = lax.dot_general(q, kb, _NT, preferred_element_type=F32) * scale
            valid = kcol + off < qpos
            ls = jnp.where(valid, -_softplus(z), 0.0)
            later = _dot_split(ls, after) + acc
            w = jnp.where(valid, jnp.exp(ls + z + later), 0.0)
            out = out + jnp.dot(w.astype(BF16), vb, preferred_element_type=F32)
            return acc + jnp.sum(ls, axis=1, keepdims=True), out

        acc, out = lax.fori_loop(0, nt, step, (jnp.zeros((TQ, 1), F32), jnp.zeros((TQ, HEAD_DIM), F32)))
        o_ref[...] = out.astype(o_ref.dtype)
        tot_ref[...] = acc

    return pl.pallas_call(
        body, grid=(nh, nq),
        in_specs=[pl.BlockSpec((TQ, HEAD_DIM), lambda h, i: (i, h)),
                  pl.BlockSpec((S, HEAD_DIM), lambda h, i: (0, nh + h)),
                  pl.BlockSpec((S, HEAD_DIM), lambda h, i: (0, 2 * nh + h))],
        out_specs=[pl.BlockSpec((TQ, HEAD_DIM), lambda h, i: (i, h)), pl.BlockSpec((None, TQ, 1), lambda h, i: (h, i, 0))],
        out_shape=[jax.ShapeDtypeStruct((S, nh * HEAD_DIM), BF16), jax.ShapeDtypeStruct((nh, S, 1), F32)],
        compiler_params=_params(("parallel", "parallel")), name=name)(proj, proj, proj)


def _sb_bwd(proj, tot, dmix, nh, *, name):
    S = proj.shape[0]
    TQ, TK = min(SB_TQ, S), min(SB_TK, S)
    nq = S // TQ
    scale = HEAD_DIM ** -0.5

    def body(q_ref, k_ref, v_ref, tot_ref, do_ref, dq_ref, dk_ref, dv_ref, dk_acc, dv_acc):
        i = pl.program_id(1)

        @pl.when(i == 0)
        def _():
            dk_acc[...] = jnp.zeros_like(dk_acc)
            dv_acc[...] = jnp.zeros_like(dv_acc)

        q = q_ref[...].astype(BF16)
        do = do_ref[...].astype(BF16)
        tot = tot_ref[...]
        qpos = i * TQ + lax.broadcasted_iota(jnp.int32, (TQ, TK), 0)
        kcol = lax.broadcasted_iota(jnp.int32, (TQ, TK), 1)
        after = _after_matrix(TK)
        before = _after_matrix(TK, transpose=True)
        nt = ((i + 1) * TQ + TK - 1) // TK

        def step(j, carry):
            pre, g_sum, dq = carry
            off = pl.multiple_of(j * TK, TK)
            kb = k_ref[pl.ds(off, TK), :].astype(BF16)
            vb = v_ref[pl.ds(off, TK), :].astype(BF16)
            z = lax.dot_general(q, kb, _NT, preferred_element_type=F32) * scale
            valid = kcol + off < qpos
            ls = jnp.where(valid, -_softplus(z), 0.0)
            lb = ls + z
            rs = jnp.sum(ls, axis=1, keepdims=True)
            later = _dot_split(ls, after) + (tot - pre - rs)
            w = jnp.where(valid, jnp.exp(lb + later), 0.0)
            g = lax.dot_general(do, vb, _NT, preferred_element_type=F32) * w
            dls = _dot_split(g, before) + g_sum
            sig = jnp.exp(lb)
            dz = (jnp.where(valid, g * (1.0 - sig) - dls * sig, 0.0) * scale).astype(BF16)
            dq = dq + jnp.dot(dz, kb, preferred_element_type=F32)
            dk_acc[pl.ds(off, TK), :] += lax.dot_general(dz, q, _TN, preferred_element_type=F32)
            dv_acc[pl.ds(off, TK), :] += lax.dot_general(w.astype(BF16), do, _TN, preferred_element_type=F32)
            return pre + rs, g_sum + jnp.sum(g, axis=1, keepdims=True), dq

        zero = jnp.zeros((TQ, 1), F32)
        _, _, dq = lax.fori_loop(0, nt, step, (zero, zero, jnp.zeros((TQ, HEAD_DIM), F32)))
        dq_ref[...] = dq.astype(dq_ref.dtype)

        @pl.when(i == nq - 1)
        def _():
            dk_ref[...] = dk_acc[...].astype(dk_ref.dtype)
            dv_ref[...] = dv_acc[...].astype(dv_ref.dtype)

    qs = pl.BlockSpec((TQ, HEAD_DIM), lambda h, i: (i, h))
    full = pl.BlockSpec((S, HEAD_DIM), lambda h, i: (0, h))
    o = jax.ShapeDtypeStruct((S, nh * HEAD_DIM), BF16)
    return pl.pallas_call(
        body, grid=(nh, nq),
        in_specs=[qs, pl.BlockSpec((S, HEAD_DIM), lambda h, i: (0, nh + h)),
                  pl.BlockSpec((S, HEAD_DIM), lambda h, i: (0, 2 * nh + h)),
                  pl.BlockSpec((None, TQ, 1), lambda h, i: (h, i, 0)), qs],
        out_specs=[qs, full, full], out_shape=[o, o, o],
        scratch_shapes=[pltpu.VMEM((S, HEAD_DIM), F32), pltpu.VMEM((S, HEAD_DIM), F32)],
        compiler_params=_params(("parallel", "arbitrary")), name=name)(proj, proj, proj, tot, dmix)


def _gdn_qkv_fwd(proj, conv_w, nh, *, name):
    S = proj.shape[0]
    GW = nh * HEAD_DIM
    scale = HEAD_DIM ** -0.5

    def body(x_ref, w_ref, o_ref):
        sec = pl.program_id(0) // nh
        c = _dwconv(x_ref[...], w_ref[...])
        s = c * _sigmoid(c)
        r = lax.rsqrt(jnp.sum(s * s, axis=1, keepdims=True) + EPS)
        fac = jnp.where(sec == 0, scale, 1.0)
        o_ref[...] = jnp.where(sec == 2, s, s * (r * fac))

    return pl.pallas_call(
        body, grid=(3 * nh,),
        in_specs=[pl.BlockSpec((S, HEAD_DIM), lambda j: (0, 3 * nh + j)), pl.BlockSpec((SHORT_CONV, HEAD_DIM), lambda j: (0, j))],
        out_specs=pl.BlockSpec((None, S, HEAD_DIM), lambda j: (j // nh, 0, j % nh)),
        out_shape=jax.ShapeDtypeStruct((3, S, GW), F32),
        compiler_params=_params(("parallel",)), name=name)(proj, conv_w)


def _gdn_qkv_bwd(proj, conv_w, dqkv, nh, *, name):
    S = proj.shape[0]
    GW = nh * HEAD_DIM
    scale = HEAD_DIM ** -0.5

    def body(x_ref, w_ref, d_ref, dx_ref, dw_ref):
        sec = pl.program_id(0) // nh
        x, w = x_ref[...], w_ref[...]
        c = _dwconv(x, w)
        sg = _sigmoid(c)
        s = c * sg
        r = lax.rsqrt(jnp.sum(s * s, axis=1, keepdims=True) + EPS)
        sh = s * r
        d = d_ref[...]
        fac = jnp.where(sec == 0, scale, 1.0)
        dn = (r * fac) * (d - sh * jnp.sum(d * sh, axis=1, keepdims=True))
        ds = jnp.where(sec == 2, d, dn)
        dx, dw = _dwconv_bwd(x, w, ds * (sg * (1.0 + c * (1.0 - sg))))
        dx_ref[...] = dx.astype(dx_ref.dtype)
        dw_ref[...] = dw

    return pl.pallas_call(
        body, grid=(3 * nh,),
        in_specs=[pl.BlockSpec((S, HEAD_DIM), lambda j: (0, 3 * nh + j)), pl.BlockSpec((SHORT_CONV, HEAD_DIM), lambda j: (0, j)),
                  pl.BlockSpec((None, S, HEAD_DIM), lambda j: (j // nh, 0, j % nh))],
        out_specs=[pl.BlockSpec((S, HEAD_DIM), lambda j: (0, j)), pl.BlockSpec((SHORT_CONV, HEAD_DIM), lambda j: (0, j))],
        out_shape=[jax.ShapeDtypeStruct((S, 3 * GW), BF16), jax.ShapeDtypeStruct((SHORT_CONV, 3 * GW), F32)],
        compiler_params=_params(("parallel",)), name=name)(proj, conv_w, dqkv)


def _gdn_gates_fwd(proj, ab, nh, *, name):
    S = proj.shape[0]
    C = CHUNK

    def body(x_ref, ab_ref, o_ref):
        ri = lax.broadcasted_iota(jnp.int32, (C, C), 0)
        ci = lax.broadcasted_iota(jnp.int32, (C, C), 1)
        ltri = (ri >= ci).astype(F32)
        lane = lax.broadcasted_iota(jnp.int32, (C, LANES), 1)
        a_coef = -jnp.exp(ab_ref[0:1, :])
        dt = ab_ref[1:2, :]

        def chunk(n, _):
            rows = pl.ds(pl.multiple_of(n * C, C), C)
            x = x_ref[rows, :]
            beta = _sigmoid(x)
            g = jnp.where(jnp.logical_and(lane >= nh, lane < 2 * nh), a_coef * _softplus(x + dt), 0.0)
            gc = _dot_hi(ltri, pltpu.roll(g, nh, 1))
            o_ref[rows, :] = jnp.where(lane < nh, beta, g) + gc
            return 0

        lax.fori_loop(0, S // C, chunk, 0)

    return pl.pallas_call(
        body, grid=(1,),
        in_specs=[pl.BlockSpec((S, LANES), lambda i: (0, 7 * nh)), pl.BlockSpec((2, LANES), lambda i: (0, 0))],
        out_specs=pl.BlockSpec((S, LANES), lambda i: (0, 0)), out_shape=jax.ShapeDtypeStruct((S, LANES), F32),
        compiler_params=_params(("arbitrary",)), name=name)(proj, ab)


def _gdn_gates_bwd(proj, ab, dgt, nh, *, name):
    S = proj.shape[0]
    C = CHUNK

    def body(x_ref, ab_ref, d_ref, dx_ref, dab_ref):
        ri = lax.broadcasted_iota(jnp.int32, (C, C), 0)
        ci = lax.broadcasted_iota(jnp.int32, (C, C), 1)
        utri = (ri <= ci).astype(F32)
        lane = lax.broadcasted_iota(jnp.int32, (C, LANES), 1)
        is_b = lane < nh
        is_a = jnp.logical_and(lane >= nh, lane < 2 * nh)
        a_coef = -jnp.exp(ab_ref[0:1, :])
        dt = ab_ref[1:2, :]

        def chunk(n, carry):
            da_log, ddt = carry
            rows = pl.ds(pl.multiple_of(n * C, C), C)
            x = x_ref[rows, :]
            d = d_ref[rows, :]
            beta = _sigmoid(x)
            dg = pltpu.roll(_dot_hi(utri, jnp.where(lane >= 2 * nh, d, 0.0)), LANES - nh, 1)
            dg = jnp.where(is_a, dg, 0.0)
            dxa = dg * a_coef * _sigmoid(x + dt)
            dxb = jnp.where(is_b, d * beta * (1.0 - beta), 0.0)
            dx_ref[rows, :] = (dxa + dxb).astype(dx_ref.dtype)
            da_log = da_log + jnp.sum(dg * a_coef * _softplus(x + dt), axis=0, keepdims=True)
            return da_log, ddt + jnp.sum(dxa, axis=0, keepdims=True)

        zero = jnp.zeros((1, LANES), F32)
        da_log, ddt = lax.fori_loop(0, S // C, chunk, (zero, zero))
        dab_ref[0:1, :] = da_log
        dab_ref[1:2, :] = ddt

    return pl.pallas_call(
        body, grid=(1,),
        in_specs=[pl.BlockSpec((S, LANES), lambda i: (0, 7 * nh)), pl.BlockSpec((2, LANES), lambda i: (0, 0)),
                  pl.BlockSpec((S, LANES), lambda i: (0, 0))],
        out_specs=[pl.BlockSpec((S, LANES), lambda i: (0, 0)), pl.BlockSpec((2, LANES), lambda i: (0, 0))],
        out_shape=[jax.ShapeDtypeStruct((S, LANES), BF16), jax.ShapeDtypeStruct((2, LANES), F32)],
        compiler_params=_params(("arbitrary",)), name=name)(proj, ab, dgt)


def _unit_lower_inverse(lmat):
    C = lmat.shape[0]
    ri = lax.broadcasted_iota(jnp.int32, (C, C), 0)
    ci = lax.broadcasted_iota(jnp.int32, (C, C), 1)
    nmat = -lmat
    p = jnp.where(ri == ci, 1.0, 0.0) + nmat
    for _ in range(int(math.log2(C)) - 1):
        nmat = _dot_hi(nmat, nmat)
        p = p + _dot_hi(p, nmat)
    return p


def _gdn_chunk_common(q, k, v, gates, gc_row, h, nh):
    C = CHUNK
    lane = lax.broadcasted_iota(jnp.int32, (C, LANES), 1)
    beta = jnp.sum(jnp.where(lane == h, gates, 0.0), axis=1, keepdims=True)
    gc = jnp.sum(jnp.where(lane == 2 * nh + h, gates, 0.0), axis=1, keepdims=True)
    ri = lax.broadcasted_iota(jnp.int32, (C, C), 0)
    ci = lax.broadcasted_iota(jnp.int32, (C, C), 1)
    incl, strict = ri >= ci, ri > ci
    decay = jnp.where(incl, jnp.exp(jnp.where(incl, gc - gc_row, 0.0)), 0.0)
    egc = jnp.exp(gc)
    kb, vb = k * beta, v * beta
    lmat = jnp.where(strict, _dot(kb, k, _NT) * decay, 0.0)
    tinv = _unit_lower_inverse(lmat)
    kbg = kb * egc
    u = _dot(tinv, vb)
    w = _dot(tinv, kbg)
    amat = _dot(q, k, _NT) * decay
    glast = gc[C - 1:C, :]
    ekt = jnp.exp(glast - gc)
    return dict(q=q, k=k, v=v, beta=beta, decay=decay, egc=egc, kb=kb, vb=vb, lmat=lmat, tinv=tinv, kbg=kbg, u=u, w=w,
                amat=amat, qd=q * egc, ekt=ekt, kt=k * ekt, cd=jnp.exp(glast), strict=strict, incl=incl)


def _gdn_chunk_specs(nh, nb, cpb, rev):
    C = CHUNK
    cn = (lambda n: nb - 1 - n) if rev else (lambda n: n)
    return [pl.BlockSpec((3, cpb * C, HEAD_DIM), lambda h, n: (0, cn(n), h)),
            pl.BlockSpec((cpb * C, LANES), lambda h, n: (cn(n), 0)),
            pl.BlockSpec((None, cpb, 1, C), lambda h, n: (h, cn(n), 0, 0))]


def _gdn_chunk_fwd(qkv, gates, gc_row, *, name):
    _, S, GW = qkv.shape
    nh, C = GW // HEAD_DIM, CHUNK
    nc = S // C
    cpb = min(GDN_CPB, nc)
    nb = nc // cpb

    def body(qkv_ref, gates_ref, gr_ref, o_ref, st_ref, state):
        @pl.when(pl.program_id(1) == 0)
        def _():
            state[...] = jnp.zeros_like(state)

        h = pl.program_id(0)
        s0 = state[...]
        for b in range(cpb):
            rows = slice(b * C, (b + 1) * C)
            m = _gdn_chunk_common(qkv_ref[0, rows, :], qkv_ref[1, rows, :], qkv_ref[2, rows, :], gates_ref[rows, :],
                                  gr_ref[b], h, nh)
            st_ref[b] = s0
            v_new = m["u"] - _dot(m["w"], s0)
            o_ref[rows, :] = _dot(m["qd"], s0) + _dot(m["amat"], v_new)
            s0 = s0 * m["cd"] + _dot(m["kt"], v_new, _TN)
        state[...] = s0

    return pl.pallas_call(
        body, grid=(nh, nb), in_specs=_gdn_chunk_specs(nh, nb, cpb, False),
        out_specs=[pl.BlockSpec((cpb * C, HEAD_DIM), lambda h, n: (n, h)),
                   pl.BlockSpec((None, cpb, HEAD_DIM, HEAD_DIM), lambda h, n: (h, n, 0, 0))],
        out_shape=[jax.ShapeDtypeStruct((S, GW), F32), jax.ShapeDtypeStruct((nh, nc, HEAD_DIM, HEAD_DIM), F32)],
        scratch_shapes=[pltpu.VMEM((HEAD_DIM, HEAD_DIM), F32)],
        compiler_params=_params(("parallel", "arbitrary")), name=name)(qkv, gates, gc_row)


def _gdn_chunk_bwd(qkv, gates, gc_row, states, do, *, name):
    _, S, GW = qkv.shape
    nh, C = GW // HEAD_DIM, CHUNK
    nc = S // C
    cpb = min(GDN_CPB, nc)
    nb = nc // cpb

    def body(qkv_ref, gates_ref, gr_ref, st_ref, do_ref, dqkv_ref, dgt_ref, dstate):
        @pl.when(pl.program_id(1) == 0)
        def _():
            dstate[...] = jnp.zeros_like(dstate)

        h = pl.program_id(0)
        dsn = dstate[...]
        for b in reversed(range(cpb)):
            rows = slice(b * C, (b + 1) * C)
            m = _gdn_chunk_common(qkv_ref[0, rows, :], qkv_ref[1, rows, :], qkv_ref[2, rows, :], gates_ref[rows, :],
                                  gr_ref[b], h, nh)
            dq, dk, dv, dgt, dsn = _gdn_chunk_grad(m, st_ref[b], dsn, do_ref[rows, :], h, nh)
            dqkv_ref[0, rows, :] = dq
            dqkv_ref[1, rows, :] = dk
            dqkv_ref[2, rows, :] = dv
            dgt_ref[rows, :] = dgt
        dstate[...] = dsn

    rc = lambda n: nb - 1 - n
    return pl.pallas_call(
        body, grid=(nh, nb),
        in_specs=_gdn_chunk_specs(nh, nb, cpb, True) + [
            pl.BlockSpec((None, cpb, HEAD_DIM, HEAD_DIM), lambda h, n: (h, rc(n), 0, 0)),
            pl.BlockSpec((cpb * C, HEAD_DIM), lambda h, n: (rc(n), h))],
        out_specs=[pl.BlockSpec((3, cpb * C, HEAD_DIM), lambda h, n: (0, rc(n), h)),
                   pl.BlockSpec((None, cpb * C, LANES), lambda h, n: (h, rc(n), 0))],
        out_shape=[jax.ShapeDtypeStruct((3, S, GW), F32), jax.ShapeDtypeStruct((nh, S, LANES), F32)],
        scratch_shapes=[pltpu.VMEM((HEAD_DIM, HEAD_DIM), F32)],
        compiler_params=_params(("parallel", "arbitrary")), name=name)(qkv, gates, gc_row, states, do)


def _gdn_chunk_grad(m, s0, dsn, dout, h, nh):
    C = CHUNK
    q, k, v, beta, decay, egc = m["q"], m["k"], m["v"], m["beta"], m["decay"], m["egc"]
    tinv, kt, cd = m["tinv"], m["kt"], m["cd"]
    v_new = m["u"] - _dot(m["w"], s0)
    dvn = _dot(m["amat"], dout, _TN) + _dot(kt, dsn)
    dqd = _dot(dout, s0, _NT)
    damat = jnp.where(m["incl"], _dot(dout, v_new, _NT), 0.0)
    dkt = _dot(v_new, dsn, _NT)
    dcd = jnp.sum(dsn * s0)
    ds0 = _dot(m["qd"], dout, _TN) + dsn * cd - _dot(m["w"], dvn, _TN)
    dw = -_dot(dvn, s0, _NT)
    dvb = _dot(tinv, dvn, _TN)
    dkbg = _dot(tinv, dw, _TN)
    dtinv = _dot(dvn, m["vb"], _NT) + _dot(dw, m["kbg"], _NT)
    dl = jnp.where(m["strict"], -_dot_hi(_dot_hi(tinv, dtinv, _TN), tinv, _NT), 0.0)
    dkk = dl * decay
    dqk = damat * decay
    dkb = _dot(dkk, k) + dkbg * egc
    dk = _dot(dkk, m["kb"], _TN) + _dot(dqk, q, _TN) + dkt * m["ekt"] + dkb * beta
    dq = _dot(dqk, k) + dqd * egc
    mm = dl * m["lmat"] + damat * m["amat"]
    ones = jnp.ones((C, LANES), F32)
    rk = jnp.sum(dkt * kt, axis=1, keepdims=True)
    dgc = (_dot_hi(mm, ones) - _dot_hi(mm, ones, _TN) + jnp.sum(dqd * m["qd"], axis=1, keepdims=True) - rk
           + jnp.sum(dkbg * m["kbg"], axis=1, keepdims=True))
    dglast = jnp.sum(rk) + dcd * cd
    rowi = lax.broadcasted_iota(jnp.int32, (C, LANES), 0)
    lane = lax.broadcasted_iota(jnp.int32, (C, LANES), 1)
    dgc = dgc + jnp.where(rowi == C - 1, dglast, 0.0)
    dbeta = jnp.sum(dkb * k, axis=1, keepdims=True) + jnp.sum(dvb * v, axis=1, keepdims=True)
    dgt = jnp.where(lane == h, dbeta, 0.0) + jnp.where(lane == 2 * nh + h, dgc, 0.0)
    return dq, dk, dvb * beta, dgt, ds0


def _gdn_post_fwd(o, proj, ng, nh, *, name):
    S, GW = o.shape

    def body(o_ref, z_ref, g_ref, y_ref):
        ov, z = o_ref[...], z_ref[...]
        rstd = lax.rsqrt(jnp.mean(ov * ov, axis=-1, keepdims=True) + EPS)
        y_ref[...] = (ov * rstd * g_ref[...] * (z * _sigmoid(z))).astype(y_ref.dtype)

    blk = pl.BlockSpec((S, HEAD_DIM), lambda h: (0, h))
    return pl.pallas_call(
        body, grid=(nh,), in_specs=[blk, pl.BlockSpec((S, HEAD_DIM), lambda h: (0, 6 * nh + h)), pl.BlockSpec((1, HEAD_DIM), lambda h: (0, 0))],
        out_specs=blk, out_shape=jax.ShapeDtypeStruct((S, GW), BF16),
        compiler_params=_params(("parallel",)), name=name)(o, proj, ng)


def _gdn_post_bwd(o, proj, ng, dmix, nh, *, name):
    S, GW = o.shape

    def body(o_ref, z_ref, g_ref, d_ref, do_ref, dz_ref, dg_ref):
        ov, z, d = o_ref[...], z_ref[...], d_ref[...].astype(F32)
        rstd = lax.rsqrt(jnp.mean(ov * ov, axis=-1, keepdims=True) + EPS)
        oh = ov * rstd
        sz = _sigmoid(z)
        dy = d * (z * sz)
        dz_ref[...] = (d * (oh * g_ref[...]) * (sz * (1.0 + z * (1.0 - sz)))).astype(dz_ref.dtype)
        t = dy * g_ref[...]
        do_ref[...] = rstd * (t - oh * jnp.mean(t * oh, axis=-1, keepdims=True))
        part = jnp.sum(dy * oh, axis=0, keepdims=True)

        @pl.when(pl.program_id(0) == 0)
        def _():
            dg_ref[...] = part

        @pl.when(pl.program_id(0) > 0)
        def _():
            dg_ref[...] += part

    blk = pl.BlockSpec((S, HEAD_DIM), lambda h: (0, h))
    vec = pl.BlockSpec((1, HEAD_DIM), lambda h: (0, 0))
    return pl.pallas_call(
        body, grid=(nh,),
        in_specs=[blk, pl.BlockSpec((S, HEAD_DIM), lambda h: (0, 6 * nh + h)), vec, pl.BlockSpec((S, HEAD_DIM), lambda h: (0, nh + h))],
        out_specs=[blk, blk, vec],
        out_shape=[jax.ShapeDtypeStruct((S, GW), F32), jax.ShapeDtypeStruct((S, GW), BF16), jax.ShapeDtypeStruct((1, HEAD_DIM), F32)],
        compiler_params=_params(("arbitrary",)), name=name)(o, proj, ng, dmix)


def _gdn_forward(proj, conv_w, ab, ng, nh, tag):
    S = proj.shape[0]
    nc = S // CHUNK
    qkv = _gdn_qkv_fwd(proj, conv_w, nh, name=f"gdn_qkv_fwd{tag}")
    gates = _gdn_gates_fwd(proj, ab, nh, name=f"gdn_gates_fwd{tag}")
    gc_row = gates[:, 2 * nh:3 * nh].T.reshape(nh, nc, 1, CHUNK)
    o, states = _gdn_chunk_fwd(qkv, gates, gc_row, name=f"gdn_chunk_fwd{tag}")
    y = _gdn_post_fwd(o, proj, ng, nh, name=f"gdn_post_fwd{tag}")
    return y, (qkv, gates, gc_row, states, o)


def _gdn_backward(proj, conv_w, ab, ng, saved, dmix, nh, tag):
    qkv, gates, gc_row, states, o = saved
    do, dz, dng = _gdn_post_bwd(o, proj, ng, dmix, nh, name=f"gdn_post_bwd{tag}")
    dqkv, dgt_heads = _gdn_chunk_bwd(qkv, gates, gc_row, states, do, name=f"gdn_chunk_bwd{tag}")
    dx_qkv, dconv = _gdn_qkv_bwd(proj, conv_w, dqkv, nh, name=f"gdn_qkv_bwd{tag}")
    dx_g, dab = _gdn_gates_bwd(proj, ab, jnp.sum(dgt_heads, axis=0), nh, name=f"gdn_gates_bwd{tag}")
    return dx_qkv, dz, dx_g, dconv, dab, dng


def _row_tile(r, cap=128):
    t = cap
    while r % t:
        t //= 2
    assert t >= 8, r
    return t


def _adamw_update(gv, w_ref, m_ref, v_ref, d_ref, m2_ref, v2_ref):
    m2 = ADAM_B1 * m_ref[...] + (1.0 - ADAM_B1) * gv
    v2 = ADAM_B2 * v_ref[...] + (1.0 - ADAM_B2) * (gv * gv)
    m_hat = m2 / (1.0 - ADAM_B1 ** ADAM_STEP)
    v_hat = v2 / (1.0 - ADAM_B2 ** ADAM_STEP)
    d_ref[...] = -ADAM_LR * (m_hat / (jnp.sqrt(v_hat) + ADAM_EPS) + ADAM_WD * w_ref[...])
    m2_ref[...] = m2
    v2_ref[...] = v2


def _adamw(w, g, m, v, *, name):
    L, r, c = w.shape
    tr = _row_tile(r)

    def body(w_ref, g_ref, m_ref, v_ref, d_ref, m2_ref, v2_ref):
        _adamw_update(g_ref[...], w_ref, m_ref, v_ref, d_ref, m2_ref, v2_ref)

    blk = pl.BlockSpec((None, tr, c), lambda l, i: (l, i, 0))
    o = jax.ShapeDtypeStruct(w.shape, F32)
    return pl.pallas_call(
        body, grid=(L, r // tr), in_specs=[blk] * 4, out_specs=[blk] * 3, out_shape=[o, o, o],
        compiler_params=_params(("parallel", "parallel")), name=name)(w, g, m, v)


def _adamw_halves(w, g_own, g_sib, cvec, m, v, *, name):
    L, r, c = w.shape
    tr = _row_tile(r // 2)
    nbh = (r // 2) // tr

    def body(c_ref, w_ref, go_ref, gs_ref, m_ref, v_ref, g_out, d_ref, m2_ref, v2_ref):
        gv = jnp.where(pl.program_id(1) // nbh == c_ref[0], go_ref[...], gs_ref[...])
        g_out[...] = gv
        _adamw_update(gv, w_ref, m_ref, v_ref, d_ref, m2_ref, v2_ref)

    lo = lambda i: jnp.minimum(i, nbh - 1)
    hi = lambda i: jnp.maximum(i - nbh, 0)
    blk = pl.BlockSpec((None, tr, c), lambda l, i, c_ref: (l, i, 0))
    own = pl.BlockSpec((None, tr, c), lambda l, i, c_ref: (l, jnp.where(c_ref[0] == 0, lo(i), hi(i)), 0))
    sib = pl.BlockSpec((None, tr, c), lambda l, i, c_ref: (l, jnp.where(c_ref[0] == 0, hi(i), lo(i)), 0))
    o = jax.ShapeDtypeStruct(w.shape, F32)
    return pl.pallas_call(
        body,
        grid_spec=pltpu.PrefetchScalarGridSpec(
            num_scalar_prefetch=1, grid=(L, r // tr), in_specs=[blk, own, sib, blk, blk], out_specs=[blk] * 4),
        out_shape=[o, o, o, o],
        compiler_params=_params(("parallel", "arbitrary")), name=name)(cvec, w, g_own, g_sib, m, v)


def _sum_half(g, rbuf, cvec, *, name):
    _, r, c = g.shape
    h = r // 2
    tr = _row_tile(h)
    nb = h // tr

    def body(c_ref, g_ref, r_ref, o_ref):
        o_ref[...] = (g_ref[...] + r_ref[...]).astype(o_ref.dtype)

    blk = pl.BlockSpec((None, tr, c), lambda s, i, c_ref: (s, i, 0))
    return pl.pallas_call(
        body,
        grid_spec=pltpu.PrefetchScalarGridSpec(
            num_scalar_prefetch=1, grid=(4, nb),
            in_specs=[pl.BlockSpec((None, tr, c), lambda s, i, c_ref: (s, c_ref[0] * nb + i, 0)), blk], out_specs=blk),
        out_shape=jax.ShapeDtypeStruct((4, h, c), BF16),
        compiler_params=_params(("parallel", "parallel")), name=name)(cvec, g, rbuf)


def _sum_chips(rb, prev, l, nl, *, name):
    _, h, c = rb.shape
    tr = _row_tile(h)

    def body(*refs):
        r_ref, o_ref = refs[0], refs[-1]
        acc = r_ref[0].astype(F32)
        for s in range(1, 4):
            acc = acc + r_ref[s].astype(F32)
        o_ref[...] = acc

    in_specs = [pl.BlockSpec((4, tr, c), lambda i: (0, i, 0))]
    args = [rb]
    if prev is not None:
        in_specs.append(pl.BlockSpec(memory_space=pltpu.HBM))
        args.append(prev)
    return pl.pallas_call(
        body, grid=(h // tr,), in_specs=in_specs, out_specs=pl.BlockSpec((None, tr, c), lambda i: (l, i, 0)),
        out_shape=jax.ShapeDtypeStruct((nl, h, c), F32), input_output_aliases={1: 0} if prev is not None else {},
        compiler_params=_params(("parallel",)), name=name)(*args)


_MESH = pl.DeviceIdType.MESH
_HBM = pl.BlockSpec(memory_space=pltpu.HBM)


def _place():
    x, y, c = lax.axis_index("x"), lax.axis_index("y"), lax.axis_index("c")
    return x, y, c, [(1 - x, y), (x, 1 - y), (1 - x, 1 - y)]


def _cast_place(w, l, kvec, *, name):
    _, r, c = w.shape
    tr = _row_tile(r, 256)

    def body(k_ref, w_ref, o_ref):
        o_ref[...] = w_ref[...].astype(o_ref.dtype)

    return pl.pallas_call(
        body,
        grid_spec=pltpu.PrefetchScalarGridSpec(
            num_scalar_prefetch=1, grid=(r // tr,),
            in_specs=[pl.BlockSpec((None, tr, c), lambda i, k_ref: (l, i, 0))],
            out_specs=pl.BlockSpec((None, tr, c), lambda i, k_ref: (k_ref[0], i, 0))),
        out_shape=jax.ShapeDtypeStruct((4, r, c), BF16),
        compiler_params=_params(("parallel",)), name=name)(kvec, w)


def _gather_weights(ws, *, name):
    n = len(ws)

    def body(*refs):
        o_refs = refs[n:2 * n]
        ici_s, ici_r, fwd_s, fwd_r = refs[2 * n:]
        x, y, c, chips = _place()
        k = 2 * x + y
        started = []
        for a in range(n):
            h = o_refs[a].shape[1] // 2
            for j, (cx, cy) in enumerate(chips):
                mine = o_refs[a].at[k, pl.ds(c * h, h), :]
                cp = pltpu.make_async_remote_copy(
                    src_ref=mine, dst_ref=mine, send_sem=ici_s.at[3 * a + j], recv_sem=ici_r.at[3 * a + j],
                    device_id=(cx, cy, c), device_id_type=_MESH)
                cp.start()
                started.append(cp)
        for a in range(n):
            h = o_refs[a].shape[1] // 2
            for j, (cx, cy) in enumerate(chips):
                landed = o_refs[a].at[2 * cx + cy, pl.ds(c * h, h), :]
                pltpu.make_async_remote_copy(
                    src_ref=landed, dst_ref=landed, send_sem=ici_s.at[3 * a + j], recv_sem=ici_r.at[3 * a + j],
                    device_id=(x, y, c), device_id_type=_MESH).wait_recv()
                fw = pltpu.make_async_remote_copy(
                    src_ref=landed, dst_ref=landed, send_sem=fwd_s.at[3 * a + j], recv_sem=fwd_r.at[3 * a + j],
                    device_id=(x, y, 1 - c), device_id_type=_MESH)
                fw.start()
                started.append(fw)
        for a in range(n):
            h = o_refs[a].shape[1] // 2
            for j, (cx, cy) in enumerate(chips):
                other = o_refs[a].at[2 * cx + cy, pl.ds((1 - c) * h, h), :]
                pltpu.make_async_remote_copy(
                    src_ref=other, dst_ref=other, send_sem=fwd_s.at[3 * a + j], recv_sem=fwd_r.at[3 * a + j],
                    device_id=(x, y, c), device_id_type=_MESH).wait_recv()
        for cp in started:
            cp.wait_send()

    return pl.pallas_call(
        body, in_specs=[_HBM] * n, out_specs=[_HBM] * n,
        out_shape=[jax.ShapeDtypeStruct(w.shape, w.dtype) for w in ws],
        input_output_aliases={a: a for a in range(n)},
        scratch_shapes=[pltpu.SemaphoreType.DMA((3 * n,))] * 4, name=name)(*ws)


def _exchange_sibling(gs, *, name):
    n = len(gs)

    def body(*refs):
        g_refs, o_refs = refs[:n], refs[n:2 * n]
        send, recv = refs[2 * n:]
        x, y, c, _ = _place()
        cps = []
        for a in range(n):
            h = g_refs[a].shape[1] // 2
            cp = pltpu.make_async_remote_copy(
                src_ref=g_refs[a].at[:, pl.ds((1 - c) * h, h), :], dst_ref=o_refs[a], send_sem=send.at[a], recv_sem=recv.at[a],
                device_id=(x, y, 1 - c), device_id_type=_MESH)
            cp.start()
            cps.append(cp)
        for cp in cps:
            cp.wait_recv()
        for cp in cps:
            cp.wait_send()

    return pl.pallas_call(
        body, in_specs=[_HBM] * n, out_specs=[_HBM] * n,
        out_shape=[jax.ShapeDtypeStruct((4, g.shape[1] // 2, g.shape[2]), g.dtype) for g in gs],
        scratch_shapes=[pltpu.SemaphoreType.DMA((n,))] * 2, name=name)(*gs)


def _exchange_chips(ps, *, name):
    n = len(ps)

    def body(*refs):
        p_refs, o_refs = refs[:n], refs[n:2 * n]
        send, recv, loc = refs[2 * n:]
        x, y, c, chips = _place()
        k = 2 * x + y
        cps, local = [], []
        for a in range(n):
            mine = pltpu.make_async_copy(p_refs[a].at[k], o_refs[a].at[k], loc.at[a])
            mine.start()
            local.append(mine)
            for j, (cx, cy) in enumerate(chips):
                cp = pltpu.make_async_remote_copy(
                    src_ref=p_refs[a].at[2 * cx + cy], dst_ref=o_refs[a].at[k], send_sem=send.at[3 * a + j],
                    recv_sem=recv.at[3 * a + j], device_id=(cx, cy, c), device_id_type=_MESH)
                cp.start()
                cps.append(cp)
        for a in range(n):
            for j, (cx, cy) in enumerate(chips):
                slot = o_refs[a].at[2 * cx + cy]
                pltpu.make_async_remote_copy(
                    src_ref=slot, dst_ref=slot, send_sem=send.at[3 * a + j], recv_sem=recv.at[3 * a + j],
                    device_id=(x, y, c), device_id_type=_MESH).wait_recv()
        for cp in cps:
            cp.wait_send()
        for cp in local:
            cp.wait()

    return pl.pallas_call(
        body, in_specs=[_HBM] * n, out_specs=[_HBM] * n,
        out_shape=[jax.ShapeDtypeStruct(p.shape, p.dtype) for p in ps],
        scratch_shapes=[pltpu.SemaphoreType.DMA((3 * n,))] * 2 + [pltpu.SemaphoreType.DMA((n,))], name=name)(*ps)


def _swap_with_sibling(gs, *, name):
    n = len(gs)

    def body(*refs):
        g_refs, o_refs = refs[:n], refs[n:2 * n]
        send, recv = refs[2 * n:]
        x, y, c, _ = _place()
        cps = []
        for a in range(n):
            cp = pltpu.make_async_remote_copy(
                src_ref=g_refs[a], dst_ref=o_refs[a], send_sem=send.at[a], recv_sem=recv.at[a],
                device_id=(x, y, 1 - c), device_id_type=_MESH)
            cp.start()
            cps.append(cp)
        for cp in cps:
            cp.wait_recv()
        for cp in cps:
            cp.wait_send()

    return pl.pallas_call(
        body, in_specs=[_HBM] * n, out_specs=[_HBM] * n,
        out_shape=[jax.ShapeDtypeStruct(g.shape, g.dtype) for g in gs],
        scratch_shapes=[pltpu.SemaphoreType.DMA((n,))] * 2, name=name)(*gs)


def _allreduce_small(v, *, name):
    R = v.shape[0]

    def body(v_ref, o_ref, buf, send, recv, loc):
        x, y, c = lax.axis_index("x"), lax.axis_index("y"), lax.axis_index("c")
        me = 4 * x + 2 * y + c
        mine = pltpu.make_async_copy(v_ref, buf.at[me], loc)
        mine.start()
        cps = []
        for d in range(1, 8):
            px = 1 - x if d & 4 else x
            py = 1 - y if d & 2 else y
            pc = 1 - c if d & 1 else c
            cp = pltpu.make_async_remote_copy(
                src_ref=v_ref, dst_ref=buf.at[me], send_sem=send.at[d - 1], recv_sem=recv.at[d - 1],
                device_id=(px, py, pc), device_id_type=_MESH)
            cp.start()
            cps.append((cp, 4 * px + 2 * py + pc))
        for d in range(1, 8):
            cp, peer = cps[d - 1]
            pltpu.make_async_remote_copy(
                src_ref=buf.at[peer], dst_ref=buf.at[peer], send_sem=send.at[d - 1], recv_sem=recv.at[d - 1],
                device_id=(x, y, c), device_id_type=_MESH).wait_recv()
        for cp, _ in cps:
            cp.wait_send()
        mine.wait()
        acc = buf[0]
        for i in range(1, 8):
            acc = acc + buf[i]
        o_ref[...] = acc

    return pl.pallas_call(
        body, in_specs=[pl.BlockSpec(memory_space=pltpu.VMEM)], out_specs=pl.BlockSpec(memory_space=pltpu.VMEM),
        out_shape=jax.ShapeDtypeStruct((R, LANES), F32),
        scratch_shapes=[pltpu.VMEM((8, R, LANES), F32), pltpu.SemaphoreType.DMA((7,)), pltpu.SemaphoreType.DMA((7,)),
                        pltpu.SemaphoreType.DMA],
        compiler_params=pltpu.CompilerParams(vmem_limit_bytes=VMEM_LIMIT), name=name)(v)


def _pack(arrs, row_multiple=8):
    rows = []
    for a in arrs:
        flat = a.reshape(-1)
        flat = jnp.pad(flat, (0, (-flat.shape[0]) % LANES))
        rows.append(flat.reshape(-1, LANES))
    buf = jnp.concatenate(rows, axis=0)
    return jnp.pad(buf, ((0, (-buf.shape[0]) % row_multiple), (0, 0)))


def _unpack(buf, shapes):
    out, r = [], 0
    for s in shapes:
        size = math.prod(s)
        nr = -(-size // LANES)
        out.append(buf[r:r + nr].reshape(-1)[:size].reshape(s))
        r += nr
    return out


def kernel(x, mem, mix_norm, w_in, gdn_conv, gdn_a_log, gdn_dt_bias, gdn_norm, w_out, xattn_norm, mem_norm, w_xq, w_xkv, w_xo, ffn_norm, w_up, ffn_conv, ffn_conv_bias, w_down, final_norm, loss_target, m_mix_norm, m_w_in, m_gdn_conv, m_gdn_a_log, m_gdn_dt_bias, m_gdn_norm, m_w_out, m_xattn_norm, m_mem_norm, m_w_xq, m_w_xkv, m_w_xo, m_ffn_norm, m_w_up, m_ffn_conv, m_ffn_conv_bias, m_w_down, m_final_norm, v_mix_norm, v_w_in, v_gdn_conv, v_gdn_a_log, v_gdn_dt_bias, v_gdn_norm, v_w_out, v_xattn_norm, v_mem_norm, v_w_xq, v_w_xkv, v_w_xo, v_ffn_norm, v_w_up, v_ffn_conv, v_ffn_conv_bias, v_w_down, v_final_norm):
    L = w_in.shape[0]
    _, S, D = x.shape
    nh = D // (2 * HEAD_DIM)
    GW = nh * HEAD_DIM
    n_in = 7 * GW + 2 * nh
    NP = 7 * GW + LANES
    XW = X_HEADS * HEAD_DIM
    F = w_down.shape[1] * 4
    cs_in = w_in.shape[2]
    cs_up = w_up.shape[2]
    cs_xo = w_xo.shape[2]
    tu = _tile(cs_up, 1408)
    per = cs_up // tu
    fper = F // tu
    assert n_in == 4 * cs_in and F % tu == 0 and 2 * F == 4 * cs_up

    xi, yi, ci = lax.axis_index("x"), lax.axis_index("y"), lax.axis_index("c")
    chip = 2 * xi + yi
    cvec = jnp.reshape(ci, (1,)).astype(jnp.int32)

    cs_gc, cs_fc = gdn_conv.shape[2], ffn_conv.shape[2]
    keep = jnp.where(ci == 0, 1.0, 0.0).astype(F32)
    gc_full = lax.dynamic_update_slice(jnp.zeros((L, SHORT_CONV, 4 * cs_gc), F32), gdn_conv * keep, (0, 0, chip * cs_gc))
    fc_full = lax.dynamic_update_slice(jnp.zeros((L, FFN_CONV, 4 * cs_fc), F32), ffn_conv * keep, (0, 0, chip * cs_fc))
    conv_all = _allreduce_small(_pack([gc_full, fc_full]), name="allgather_conv")
    gdn_conv_full, ffn_conv_full = _unpack(conv_all, [gc_full.shape, fc_full.shape])

    big = [w_in, w_out, w_xq, w_xkv, w_xo, w_up, w_down]
    kvec = jnp.reshape(chip, (1,)).astype(jnp.int32)
    ab = jnp.zeros((L, 2, LANES), F32).at[:, 0, nh:2 * nh].set(gdn_a_log).at[:, 1, nh:2 * nh].set(gdn_dt_bias)

    def vec(p, l):
        return p[l:l + 1]

    xo_fwd_b = pl.BlockSpec((None, XW, cs_xo), lambda i, j, k: (j, 0, 0))
    xo_dg_b = pl.BlockSpec((None, XW, cs_xo), lambda i, j, k: (k, 0, 0))
    xo_wg_o = pl.BlockSpec((None, XW, cs_xo), lambda i, j, k: (j, 0, 0))
    up_fwd_b = pl.BlockSpec((None, D, tu), lambda i, j, k: (j // per, 0, j % per))

    def fwd_layer(l, xc, wts):
        g_in, g_out, g_xq, g_xkv, g_xo, g_up, g_down = wts
        w_in_l = jnp.concatenate([g_in[0], g_in[1], g_in[2], g_in[3], jnp.zeros((D, NP - n_in), BF16)], axis=1)
        w_out_l, w_xq_l, w_xkv_l, w_down_l = g_out.reshape(2 * GW, D), g_xq.reshape(D, XW), g_xkv.reshape(D, 2 * XW), g_down.reshape(F, D)
        s = dict(x0=xc, w_in=w_in_l, w_out=w_out_l, w_xq=w_xq_l, w_xkv=w_xkv_l, w_xo=g_xo, w_up=g_up, w_down=w_down_l)
        s["h"] = _rms_fwd(xc, vec(mix_norm, l), name="rms_mix_fwd")
        s["proj"] = _matmul(s["h"], w_in_l, tn=2432, tk=D, name="mm_in_fwd")
        s["sb"], s["tot"] = _sb_fwd(s["proj"], nh, name="sb_fwd")
        gdn_out, s["gdn"] = _gdn_forward(s["proj"], gdn_conv_full[l], ab[l], vec(gdn_norm, l), nh, "")
        s["mixed"] = jnp.concatenate([s["sb"], gdn_out], axis=1)
        s["x1"] = _matmul(s["mixed"], w_out_l, res=xc, tk=2 * GW, name="mm_out_fwd")
        s["memn"] = _rms_fwd(mem[0], vec(mem_norm, l), name="rms_mem_fwd")
        s["kv"] = _matmul(s["memn"], w_xkv_l, out_dtype=BF16, tk=D, name="mm_xkv_fwd")
        s["hq"] = _rms_fwd(s["x1"], vec(xattn_norm, l), name="rms_xattn_fwd")
        s["q"] = _matmul(s["hq"], w_xq_l, out_dtype=BF16, tk=D, name="mm_xq_fwd")
        s["xo"] = _xattn_fwd(s["q"], s["kv"], name="xattn_fwd")
        s["x2"] = _matmul(s["xo"], g_xo, res=s["x1"], dims=(S, D, XW), tn=cs_xo, tk=XW, b_spec=xo_fwd_b, name="mm_xo_fwd")
        s["hf"] = _rms_fwd(s["x2"], vec(ffn_norm, l), name="rms_ffn_fwd")
        s["u"] = _matmul(s["hf"], g_up, dims=(S, 2 * F, D), tn=tu, tk=D, b_spec=up_fwd_b, name="mm_up_fwd")
        s["act"] = _ffn_act_fwd(s["u"], ffn_conv_full[l], ffn_conv_bias[l:l + 1], name="ffn_act_fwd")
        x3 = _matmul(s["act"], w_down_l, res=s["x2"], tk=tu, name="mm_down_fwd")
        return x3, s

    def bwd_layer(l, s, dx3):
        dact = _matmul(dx3, s["w_down"], tb=True, tk=D, name="mm_down_dgrad")
        d_down = _matmul(s["act"], dx3, ta=True, tk=S, name="mm_down_wgrad")
        du3, dcw3, dcb3 = _ffn_act_bwd(s["u"], ffn_conv_full[l], ffn_conv_bias[l:l + 1], dact, name="ffn_act_bwd")
        dhf = _matmul(du3, s["w_up"], tb=True, dims=(S, D, 2 * F), tk=tu,
                      a_spec=pl.BlockSpec((None, _tile(S, 512), tu), lambda i, j, k: (k // fper, i, k % fper)),
                      b_spec=pl.BlockSpec((None, _tile(D, 512), tu), lambda i, j, k: (k // per, j, k % per)), name="mm_up_dgrad")
        d_up = _matmul(s["hf"], du3, ta=True, dims=(D, 2 * F, S), tn=tu, tk=S,
                       b_spec=pl.BlockSpec((None, S, tu), lambda i, j, k: (j // fper, 0, j % fper)),
                       o_spec=pl.BlockSpec((None, _tile(D, 512), tu), lambda i, j, k: (j // per, i, j % per)),
                       out_shape=jax.ShapeDtypeStruct((4, D, cs_up), F32), name="mm_up_wgrad")
        dx2, dg_ffn = _rms_bwd(s["x2"], vec(ffn_norm, l), dhf, dx3, name="rms_bwd")
        dxo = _matmul(dx2, s["w_xo"], tb=True, dims=(S, XW, D), tn=XW, tk=cs_xo, b_spec=xo_dg_b, name="mm_xo_dgrad")
        d_xo = _matmul(s["xo"], dx2, ta=True, dims=(XW, D, S), tm=XW, tn=cs_xo, tk=S, o_spec=xo_wg_o,
                       out_shape=jax.ShapeDtypeStruct((4, XW, cs_xo), F32), name="mm_xo_wgrad")
        dq, dk, dv = _xattn_bwd(s["q"], s["kv"], dxo, name="xattn_bwd")
        dkv = jnp.concatenate([dk, dv], axis=1)
        dhq = _matmul(dq, s["w_xq"], tb=True, tk=XW, name="mm_xq_dgrad")
        d_xq = _matmul(s["hq"], dq, ta=True, tk=S, name="mm_xq_wgrad")
        dmemn = _matmul(dkv, s["w_xkv"], tb=True, tk=2 * XW, name="mm_xkv_dgrad")
        d_xkv = _matmul(s["memn"], dkv, ta=True, tk=mem.shape[1], name="mm_xkv_wgrad")
        _, dg_mem = _rms_bwd(mem[0], vec(mem_norm, l), dmemn, None, name="rms_mem_bwd")
        dx1, dg_xattn = _rms_bwd(s["x1"], vec(xattn_norm, l), dhq, dx2, name="rms_bwd")
        dmix = _matmul(dx1, s["w_out"], tb=True, tk=D, name="mm_out_dgrad")
        d_out = _matmul(s["mixed"], dx1, ta=True, tk=S, name="mm_out_wgrad")
        dq_s, dk_s, dv_s = _sb_bwd(s["proj"], s["tot"], dmix, nh, name="sb_bwd")
        dx_qkv, dz, dx_g, dconv, dab, dng = _gdn_backward(s["proj"], gdn_conv_full[l], ab[l], vec(gdn_norm, l), s["gdn"], dmix, nh, "")
        dproj = jnp.concatenate([dq_s, dk_s, dv_s, dx_qkv, dz, dx_g], axis=1)
        dh = _matmul(dproj, s["w_in"], tb=True, tk=2432, name="mm_in_dgrad")
        d_in = _matmul(s["h"], dproj, ta=True, tn=2432, tk=S, name="mm_in_wgrad")
        dx0, dg_mix = _rms_bwd(s["x0"], vec(mix_norm, l), dh, dx1, name="rms_bwd")
        slabs = [jnp.stack([d_in[:, i * cs_in:(i + 1) * cs_in] for i in range(4)]), d_out.reshape(4, -1, D),
                 d_xq.reshape(4, -1, XW), d_xkv.reshape(4, -1, 2 * XW), d_xo, d_up, d_down.reshape(4, -1, D)]
        small = [dg_mix, dconv, dab, dng, dg_xattn, dg_mem, dg_ffn,
                 jnp.concatenate([dcw3[0], dcw3[1]], axis=1), jnp.concatenate([dcb3[0], dcb3[1]], axis=1)]
        return dx0, slabs, small

    xc = x[0]
    saved = []
    for l in range(L):
        placed = [_cast_place(w, l, kvec, name=f"cast_place_{l}") for w in big]
        wts = _gather_weights(placed, name="gather_weights")
        xc, s = fwd_layer(l, xc, wts)
        saved.append(s)
    loss_blk, dxc, dg_final = _loss_head(xc, final_norm[None, :], loss_target[0], name="loss_head")

    sums = [None] * 7
    small_by_layer = [None] * L
    for l in reversed(range(L)):
        dxc, slabs, small_by_layer[l] = bwd_layer(l, saved[l], dxc)
        saved[l] = None
        from_sibling = _exchange_sibling(slabs, name="reduce_to_sibling")
        partial = [_sum_half(g, r, cvec, name="sum_sibling") for g, r in zip(slabs, from_sibling)]
        from_chips = _exchange_chips(partial, name="reduce_to_chip")
        sums = [_sum_chips(rb, prev, l, L, name=f"sum_chips_{l}") for rb, prev in zip(from_chips, sums)]
    from_sib = _swap_with_sibling(sums, name="swap_halves")

    small_flat = [a for l in range(L) for a in small_by_layer[l]] + [dg_final, loss_blk[0:1]]
    red = _unpack(_allreduce_small(_pack(small_flat), name="allreduce_small"), [a.shape for a in small_flat])
    per_layer = [red[9 * l:9 * l + 9] for l in range(L)]
    col = lambda i: jnp.concatenate([p[i] for p in per_layer], axis=0)
    stk = lambda i: jnp.stack([p[i] for p in per_layer])
    g_conv_full, g_ab, g_fconv_full = stk(1), stk(2), stk(7)
    grads_small = dict(
        mix_norm=col(0), gdn_conv=lax.dynamic_slice(g_conv_full, (0, 0, chip * cs_gc), (L, SHORT_CONV, cs_gc)),
        gdn_a_log=g_ab[:, 0, nh:2 * nh], gdn_dt_bias=g_ab[:, 1, nh:2 * nh], gdn_norm=col(3), xattn_norm=col(4),
        mem_norm=col(5), ffn_norm=col(6), ffn_conv=lax.dynamic_slice(g_fconv_full, (0, 0, chip * cs_fc), (L, FFN_CONV, cs_fc)),
        ffn_conv_bias=col(8), final_norm=red[-2][0])
    loss = red[-1][0, 0]

    names_small = ["mix_norm", "gdn_conv", "gdn_a_log", "gdn_dt_bias", "gdn_norm", "xattn_norm", "mem_norm", "ffn_norm",
                   "ffn_conv", "ffn_conv_bias", "final_norm"]
    w_small = dict(mix_norm=mix_norm, gdn_conv=gdn_conv, gdn_a_log=gdn_a_log, gdn_dt_bias=gdn_dt_bias, gdn_norm=gdn_norm,
                   xattn_norm=xattn_norm, mem_norm=mem_norm, ffn_norm=ffn_norm, ffn_conv=ffn_conv, ffn_conv_bias=ffn_conv_bias,
                   final_norm=final_norm)
    m_small = dict(mix_norm=m_mix_norm, gdn_conv=m_gdn_conv, gdn_a_log=m_gdn_a_log, gdn_dt_bias=m_gdn_dt_bias, gdn_norm=m_gdn_norm,
                   xattn_norm=m_xattn_norm, mem_norm=m_mem_norm, ffn_norm=m_ffn_norm, ffn_conv=m_ffn_conv,
                   ffn_conv_bias=m_ffn_conv_bias, final_norm=m_final_norm)
    v_small = dict(mix_norm=v_mix_norm, gdn_conv=v_gdn_conv, gdn_a_log=v_gdn_a_log, gdn_dt_bias=v_gdn_dt_bias, gdn_norm=v_gdn_norm,
                   xattn_norm=v_xattn_norm, mem_norm=v_mem_norm, ffn_norm=v_ffn_norm, ffn_conv=v_ffn_conv,
                   ffn_conv_bias=v_ffn_conv_bias, final_norm=v_final_norm)
    shapes_small = [w_small[n].shape for n in names_small]
    packed = [_pack([d[n] for n in names_small], row_multiple=128)[None] for d in (w_small, grads_small, m_small, v_small)]
    upd_small = [_unpack(o[0], shapes_small) for o in _adamw(*packed, name="adamw_small")]
    delta, new_m, new_v = [dict(zip(names_small, u)) for u in upd_small]
    grads = dict(grads_small)
    big_names = ["w_in", "w_out", "w_xq", "w_xkv", "w_xo", "w_up", "w_down"]
    big_m = [m_w_in, m_w_out, m_w_xq, m_w_xkv, m_w_xo, m_w_up, m_w_down]
    big_v = [v_w_in, v_w_out, v_w_xq, v_w_xkv, v_w_xo, v_w_up, v_w_down]
    for n, w, go, gs, m, v in zip(big_names, big, sums, from_sib, big_m, big_v):
        grads[n], delta[n], new_m[n], new_v[n] = _adamw_halves(w, go, gs, cvec, m, v, name=f"adamw_{n}")

    order = ["mix_norm", "w_in", "gdn_conv", "gdn_a_log", "gdn_dt_bias", "gdn_norm", "w_out", "xattn_norm", "mem_norm", "w_xq",
             "w_xkv", "w_xo", "ffn_norm", "w_up", "ffn_conv", "ffn_conv_bias", "w_down", "final_norm"]
    return (loss, dxc[None], *[grads[n] for n in order], *[delta[n] for n in order], *[new_m[n] for n in order],
            *[new_v[n] for n in order])
```

```python
import functools
import math

import jax
import jax.numpy as jnp
from jax import lax
from jax.experimental import pallas as pl
from jax.experimental.pallas import tpu as pltpu

F32 = jnp.float32
BF16 = jnp.bfloat16

HEAD_DIM = 128
CHUNK = 64
GDN_CPB = 4
SB_TQ, SB_TK = 256, 512
SHORT_CONV = 4
FFN_CONV = 3
X_HEADS = 4
EPS = 1e-6
LANES = 128
VMEM_LIMIT = 56 * 2**20

ADAM_LR, ADAM_B1, ADAM_B2, ADAM_EPS, ADAM_WD, ADAM_STEP = 0.001, 0.9, 0.999, 1e-08, 0.01, 10

HI = lax.Precision.HIGH


def _params(sem):
    return pltpu.CompilerParams(dimension_semantics=sem, vmem_limit_bytes=VMEM_LIMIT)


def _tile(n, want):
    if n <= want:
        return n
    t = (want // LANES) * LANES
    while t > LANES and n % t:
        t -= LANES
    assert n % t == 0, (n, want)
    return t


def _sigmoid(x):
    return 1.0 / (1.0 + jnp.exp(-x))


def _softplus(x):
    return jnp.maximum(x, 0.0) + jnp.log(1.0 + jnp.exp(-jnp.abs(x)))


def _matmul(a, b, *, name, ta=False, tb=False, out_dtype=F32, res=None, tm=512, tn=512, tk=2048,
            dims=None, a_spec=None, b_spec=None, o_spec=None, out_shape=None):
    if dims is None:
        M, K = (a.shape[1], a.shape[0]) if ta else a.shape
        N = b.shape[0] if tb else b.shape[1]
    else:
        M, N, K = dims
    tm, tn, tk = _tile(M, tm), _tile(N, tn), _tile(K, tk)
    nk = K // tk
    dn = (((0 if ta else 1,), (1 if tb else 0,)), ((), ()))

    def body(*refs):
        a_ref, b_ref = refs[0], refs[1]
        r_ref = refs[2] if res is not None else None
        o_ref = refs[3] if res is not None else refs[2]
        p = lax.dot_general(a_ref[...].astype(BF16), b_ref[...].astype(BF16), dn, preferred_element_type=F32)

        def finish(acc):
            if r_ref is not None:
                acc = acc + r_ref[...].astype(F32)
            o_ref[...] = acc.astype(o_ref.dtype)

        if nk == 1:
            finish(p)
        else:
            acc_ref = refs[-1]
            k = pl.program_id(2)

            @pl.when(k == 0)
            def _():
                acc_ref[...] = p

            @pl.when(jnp.logical_and(k > 0, k < nk - 1))
            def _():
                acc_ref[...] += p

            @pl.when(k == nk - 1)
            def _():
                finish(acc_ref[...] + p)

    if a_spec is None:
        a_spec = pl.BlockSpec((tk, tm), lambda i, j, k: (k, i)) if ta else pl.BlockSpec((tm, tk), lambda i, j, k: (i, k))
    if b_spec is None:
        b_spec = pl.BlockSpec((tn, tk), lambda i, j, k: (j, k)) if tb else pl.BlockSpec((tk, tn), lambda i, j, k: (k, j))
    if o_spec is None:
        o_spec = pl.BlockSpec((tm, tn), lambda i, j, k: (i, j))
    if out_shape is None:
        out_shape = jax.ShapeDtypeStruct((M, N), out_dtype)
    in_specs, args = [a_spec, b_spec], [a, b]
    if res is not None:
        in_specs.append(pl.BlockSpec((tm, tn), lambda i, j, k: (i, j)))
        args.append(res)
    return pl.pallas_call(
        body, grid=(M // tm, N // tn, nk), in_specs=in_specs, out_specs=o_spec, out_shape=out_shape,
        scratch_shapes=[pltpu.VMEM((tm, tn), F32)] if nk > 1 else [],
        compiler_params=_params(("parallel", "parallel", "arbitrary")), name=name)(*args)


def _rms_fwd(x, g, *, name):
    R, D = x.shape
    tr = _tile(R, 256)

    def body(x_ref, g_ref, o_ref):
        xv = x_ref[...]
        rstd = lax.rsqrt(jnp.mean(xv * xv, axis=-1, keepdims=True) + EPS)
        o_ref[...] = (xv * rstd * g_ref[...]).astype(o_ref.dtype)

    return pl.pallas_call(
        body, grid=(R // tr,), in_specs=[pl.BlockSpec((tr, D), lambda i: (i, 0)), pl.BlockSpec((1, D), lambda i: (0, 0))],
        out_specs=pl.BlockSpec((tr, D), lambda i: (i, 0)), out_shape=jax.ShapeDtypeStruct((R, D), BF16),
        compiler_params=_params(("parallel",)), name=name)(x, g)


def _rms_bwd(x, g, dh, dres, *, name):
    R, D = x.shape
    tr = _tile(R, 256)

    def body(*refs):
        if dres is None:
            x_ref, g_ref, dh_ref, dx_ref, dg_ref = refs
        else:
            x_ref, g_ref, dh_ref, dr_ref, dx_ref, dg_ref = refs
        xv = x_ref[...]
        dhv = dh_ref[...].astype(F32)
        rstd = lax.rsqrt(jnp.mean(xv * xv, axis=-1, keepdims=True) + EPS)
        xhat = xv * rstd
        t = dhv * g_ref[...]
        dx = rstd * (t - xhat * jnp.mean(t * xhat, axis=-1, keepdims=True))
        if dres is not None:
            dx = dx + dr_ref[...]
        dx_ref[...] = dx
        part = jnp.sum(dhv * xhat, axis=0, keepdims=True)

        @pl.when(pl.program_id(0) == 0)
        def _():
            dg_ref[...] = part

        @pl.when(pl.program_id(0) > 0)
        def _():
            dg_ref[...] += part

    row = pl.BlockSpec((tr, D), lambda i: (i, 0))
    vec = pl.BlockSpec((1, D), lambda i: (0, 0))
    in_specs = [row, vec, row] + ([row] if dres is not None else [])
    args = [x, g, dh] + ([dres] if dres is not None else [])
    return pl.pallas_call(
        body, grid=(R // tr,), in_specs=in_specs, out_specs=[row, vec],
        out_shape=[jax.ShapeDtypeStruct((R, D), F32), jax.ShapeDtypeStruct((1, D), F32)],
        compiler_params=_params(("arbitrary",)), name=name)(*args)


def _loss_head(x, g, tgt, *, name):
    R, D = x.shape
    tr = _tile(R, 256)

    def body(x_ref, g_ref, t_ref, l_ref, dx_ref, dg_ref):
        xv = x_ref[...]
        rstd = lax.rsqrt(jnp.mean(xv * xv, axis=-1, keepdims=True) + EPS)
        xhat = xv * rstd
        err = xhat * g_ref[...] - t_ref[...]
        dy = err * (1.0 / D)
        t = dy * g_ref[...]
        dx_ref[...] = rstd * (t - xhat * jnp.mean(t * xhat, axis=-1, keepdims=True))
        part = jnp.sum(dy * xhat, axis=0, keepdims=True)
        lpart = jnp.zeros((8, LANES), F32) + 0.5 * jnp.sum(jnp.mean(err * err, axis=-1, keepdims=True))

        @pl.when(pl.program_id(0) == 0)
        def _():
            dg_ref[...] = part
            l_ref[...] = lpart

        @pl.when(pl.program_id(0) > 0)
        def _():
            dg_ref[...] += part
            l_ref[...] += lpart

    row = pl.BlockSpec((tr, D), lambda i: (i, 0))
    vec = pl.BlockSpec((1, D), lambda i: (0, 0))
    return pl.pallas_call(
        body, grid=(R // tr,), in_specs=[row, vec, row],
        out_specs=[pl.BlockSpec((8, LANES), lambda i: (0, 0)), row, vec],
        out_shape=[jax.ShapeDtypeStruct((8, LANES), F32), jax.ShapeDtypeStruct((R, D), F32), jax.ShapeDtypeStruct((1, D), F32)],
        compiler_params=_params(("arbitrary",)), name=name)(x, g, tgt)


def _shift_down(x, s):
    if s == 0:
        return x
    row = lax.broadcasted_iota(jnp.int32, x.shape, 0)
    return jnp.where(row >= s, pltpu.roll(x, s, 0), 0.0)


def _shift_up(x, s):
    if s == 0:
        return x
    n = x.shape[0]
    row = lax.broadcasted_iota(jnp.int32, x.shape, 0)
    return jnp.where(row < n - s, pltpu.roll(x, n - s, 0), 0.0)


def _dwconv(x, w):
    k = w.shape[0]
    acc = x * w[k - 1:k, :]
    for i in range(k - 1):
        acc = acc + _shift_down(x, k - 1 - i) * w[i:i + 1, :]
    return acc


def _dwconv_bwd(x, w, dc):
    k = w.shape[0]
    dx = dc * w[k - 1:k, :]
    dws = []
    for i in range(k - 1):
        s = k - 1 - i
        dx = dx + _shift_up(dc, s) * w[i:i + 1, :]
        dws.append(jnp.sum(dc * _shift_down(x, s), axis=0, keepdims=True))
    dws.append(jnp.sum(dc * x, axis=0, keepdims=True))
    return dx, jnp.concatenate(dws, axis=0)


def _ffn_act_fwd(u, cw, cb, *, name):
    S, F2 = u.shape
    F = F2 // 2
    tc = _tile(F, 256)
    nb = F // tc

    def body(ug_ref, uu_ref, wg_ref, wu_ref, bg_ref, bu_ref, o_ref):
        cg = _dwconv(ug_ref[...], wg_ref[...]) + bg_ref[...]
        cu = _dwconv(uu_ref[...], wu_ref[...]) + bu_ref[...]
        o_ref[...] = (cg * _sigmoid(cg) * cu).astype(o_ref.dtype)

    blk = lambda r, off: pl.BlockSpec((r, tc), lambda j: (0, j + off))
    return pl.pallas_call(
        body, grid=(nb,), in_specs=[blk(S, 0), blk(S, nb), blk(FFN_CONV, 0), blk(FFN_CONV, nb), blk(1, 0), blk(1, nb)],
        out_specs=blk(S, 0), out_shape=jax.ShapeDtypeStruct((S, F), BF16),
        compiler_params=_params(("parallel",)), name=name)(u, u, cw, cw, cb, cb)


def _ffn_act_bwd(u, cw, cb, dact, *, name):
    S, F2 = u.shape
    F = F2 // 2
    tc = _tile(F, 256)
    nb = F // tc

    def body(ug_ref, uu_ref, wg_ref, wu_ref, bg_ref, bu_ref, da_ref, du_ref, dw_ref, db_ref):
        ug, uu = ug_ref[...], uu_ref[...]
        cg = _dwconv(ug, wg_ref[...]) + bg_ref[...]
        cu = _dwconv(uu, wu_ref[...]) + bu_ref[...]
        sg = _sigmoid(cg)
        da = da_ref[...].astype(F32)
        dcu = da * (cg * sg)
        dcg = da * cu * (sg * (1.0 + cg * (1.0 - sg)))
        dxg, dwg = _dwconv_bwd(ug, wg_ref[...], dcg)
        dxu, dwu = _dwconv_bwd(uu, wu_ref[...], dcu)
        du_ref[0] = dxg.astype(du_ref.dtype)
        du_ref[1] = dxu.astype(du_ref.dtype)
        dw_ref[0] = dwg
        dw_ref[1] = dwu
        db_ref[0] = jnp.sum(dcg, axis=0, keepdims=True)
        db_ref[1] = jnp.sum(dcu, axis=0, keepdims=True)

    blk = lambda r, off: pl.BlockSpec((r, tc), lambda j: (0, j + off))
    blk3 = lambda r: pl.BlockSpec((2, r, tc), lambda j: (0, 0, j))
    return pl.pallas_call(
        body, grid=(nb,),
        in_specs=[blk(S, 0), blk(S, nb), blk(FFN_CONV, 0), blk(FFN_CONV, nb), blk(1, 0), blk(1, nb), blk(S, 0)],
        out_specs=[blk3(S), blk3(FFN_CONV), blk3(1)],
        out_shape=[jax.ShapeDtypeStruct((2, S, F), BF16), jax.ShapeDtypeStruct((2, FFN_CONV, F), F32),
                   jax.ShapeDtypeStruct((2, 1, F), F32)],
        compiler_params=_params(("parallel",)), name=name)(u, u, cw, cw, cb, cb, dact)


def _xattn_fwd(q, kv, *, name):
    S, XW = q.shape
    M = kv.shape[0]
    nh = XW // HEAD_DIM
    tq = _tile(S, 512)
    scale = HEAD_DIM ** -0.5

    def body(q_ref, k_ref, v_ref, o_ref):
        z = lax.dot_general(q_ref[...], k_ref[...], (((1,), (1,)), ((), ())), preferred_element_type=F32) * scale
        e = jnp.exp(z - jnp.max(z, axis=-1, keepdims=True))
        p = e / jnp.sum(e, axis=-1, keepdims=True)
        o_ref[...] = jnp.dot(p.astype(BF16), v_ref[...], preferred_element_type=F32).astype(o_ref.dtype)

    return pl.pallas_call(
        body, grid=(nh, S // tq),
        in_specs=[pl.BlockSpec((tq, HEAD_DIM), lambda h, i: (i, h)), pl.BlockSpec((M, HEAD_DIM), lambda h, i: (0, h)),
                  pl.BlockSpec((M, HEAD_DIM), lambda h, i: (0, nh + h))],
        out_specs=pl.BlockSpec((tq, HEAD_DIM), lambda h, i: (i, h)), out_shape=jax.ShapeDtypeStruct((S, XW), BF16),
        compiler_params=_params(("parallel", "parallel")), name=name)(q, kv, kv)


def _xattn_bwd(q, kv, do, *, name):
    S, XW = q.shape
    M = kv.shape[0]
    nh = XW // HEAD_DIM
    tq = _tile(S, 512)
    scale = HEAD_DIM ** -0.5
    nt = (((1,), (1,)), ((), ()))
    tn = (((0,), (0,)), ((), ()))

    def body(q_ref, k_ref, v_ref, do_ref, dq_ref, dk_ref, dv_ref):
        qv, kvv, vv = q_ref[...], k_ref[...], v_ref[...]
        dov = do_ref[...].astype(BF16)
        z = lax.dot_general(qv, kvv, nt, preferred_element_type=F32) * scale
        e = jnp.exp(z - jnp.max(z, axis=-1, keepdims=True))
        p = e / jnp.sum(e, axis=-1, keepdims=True)
        dp = lax.dot_general(dov, vv, nt, preferred_element_type=F32)
        ds = (p * (dp - jnp.sum(dp * p, axis=-1, keepdims=True)) * scale).astype(BF16)
        dq_ref[...] = jnp.dot(ds, kvv, preferred_element_type=F32).astype(dq_ref.dtype)
        dk = lax.dot_general(ds, qv, tn, preferred_element_type=F32)
        dv = lax.dot_general(p.astype(BF16), dov, tn, preferred_element_type=F32)

        @pl.when(pl.program_id(1) == 0)
        def _():
            dk_ref[...] = dk
            dv_ref[...] = dv

        @pl.when(pl.program_id(1) > 0)
        def _():
            dk_ref[...] += dk
            dv_ref[...] += dv

    qs = pl.BlockSpec((tq, HEAD_DIM), lambda h, i: (i, h))
    ms = pl.BlockSpec((M, HEAD_DIM), lambda h, i: (0, h))
    return pl.pallas_call(
        body, grid=(nh, S // tq),
        in_specs=[qs, ms, pl.BlockSpec((M, HEAD_DIM), lambda h, i: (0, nh + h)), qs],
        out_specs=[qs, ms, ms],
        out_shape=[jax.ShapeDtypeStruct((S, XW), BF16), jax.ShapeDtypeStruct((M, XW), F32), jax.ShapeDtypeStruct((M, XW), F32)],
        compiler_params=_params(("parallel", "arbitrary")), name=name)(q, kv, kv, do)


_NN = (((1,), (0,)), ((), ()))
_NT = (((1,), (1,)), ((), ()))
_TN = (((0,), (0,)), ((), ()))


def _batched(dn, a):
    if a.ndim == 2:
        return dn
    (ca,), (cb,) = dn[0]
    return (((ca + 1,), (cb + 1,)), ((0,), (0,)))


def _dot(a, b, dn=_NN):
    return lax.dot_general(a.astype(BF16), b.astype(BF16), _batched(dn, a), preferred_element_type=F32)


def _dot_hi(a, b, dn=_NN):
    return lax.dot_general(a, b, _batched(dn, a), preferred_element_type=F32, precision=HI)


def _dot_split(a, b01, dn=_NN):
    hi = a.astype(BF16)
    lo = (a - hi.astype(F32)).astype(BF16)
    return (lax.dot_general(hi, b01, dn, preferred_element_type=F32)
            + lax.dot_general(lo, b01, dn, preferred_element_type=F32))


def _after_matrix(n, transpose=False):
    row = lax.broadcasted_iota(jnp.int32, (n, n), 0)
    col = lax.broadcasted_iota(jnp.int32, (n, n), 1)
    return (row < col if transpose else row > col).astype(BF16)


def _sb_fwd(proj, nh, *, name):
    S = proj.shape[0]
    TQ, TK = min(SB_TQ, S), min(SB_TK, S)
    nq = S // TQ
    scale = HEAD_DIM ** -0.5

    def body(q_ref, k_ref, v_ref, o_ref, tot_ref):
        i = pl.program_id(1)
        q = q_ref[...].astype(BF16)
        qpos = i * TQ + lax.broadcasted_iota(jnp.int32, (TQ, TK), 0)
        kcol = lax.broadcasted_iota(jnp.int32, (TQ, TK), 1)
        after = _after_matrix(TK)
        nt = ((i + 1) * TQ + TK - 1) // TK

        def step(t, carry):
            acc, out = carry
            off = pl.multiple_of((nt - 1 - t) * TK, TK)
            kb = k_ref[pl.ds(off, TK), :].astype(BF16)
            vb = v_ref[pl.ds(off, TK), :].astype(BF16)
            z = lax.dot_general(q, kb, _NT, preferred_element_type=F32) * scale
            valid = kcol + off < qpos
            ls = jnp.where(valid, -_softplus(z), 0.0)
            later = _dot_split(ls, after) + acc
            w = jnp.where(valid, jnp.exp(ls + z + later), 0.0)
            out = out + jnp.dot(w.astype(BF16), vb, preferred_element_type=F32)
            return acc + jnp.sum(ls, axis=1, keepdims=True), out

        acc, out = lax.fori_loop(0, nt, step, (jnp.zeros((TQ, 1), F32), jnp.zeros((TQ, HEAD_DIM), F32)))
        o_ref[...] = out.astype(o_ref.dtype)
        tot_ref[...] = acc

    return pl.pallas_call(
        body, grid=(nh, nq),
        in_specs=[pl.BlockSpec((TQ, HEAD_DIM), lambda h, i: (i, h)),
                  pl.BlockSpec((S, HEAD_DIM), lambda h, i: (0, nh + h)),
                  pl.BlockSpec((S, HEAD_DIM), lambda h, i: (0, 2 * nh + h))],
        out_specs=[pl.BlockSpec((TQ, HEAD_DIM), lambda h, i: (i, h)), pl.BlockSpec((None, TQ, 1), lambda h, i: (h, i, 0))],
        out_shape=[jax.ShapeDtypeStruct((S, nh * HEAD_DIM), BF16), jax.ShapeDtypeStruct((nh, S, 1), F32)],
        compiler_params=_params(("parallel", "parallel")), name=name)(proj, proj, proj)


def _sb_bwd(proj, tot, dmix, nh, *, name):
    S = proj.shape[0]
    TQ, TK = min(SB_TQ, S), min(SB_TK, S)
    nq = S // TQ
    scale = HEAD_DIM ** -0.5

    def body(q_ref, k_ref, v_ref, tot_ref, do_ref, dq_ref, dk_ref, dv_ref, dk_acc, dv_acc):
        i = pl.program_id(1)

        @pl.when(i == 0)
        def _():
            dk_acc[...] = jnp.zeros_like(dk_acc)
            dv_acc[...] = jnp.zeros_like(dv_acc)

        q = q_ref[...].astype(BF16)
        do = do_ref[...].astype(BF16)
        tot = tot_ref[...]
        qpos = i * TQ + lax.broadcasted_iota(jnp.int32, (TQ, TK), 0)
        kcol = lax.broadcasted_iota(jnp.int32, (TQ, TK), 1)
        after = _after_matrix(TK)
        before = _after_matrix(TK, transpose=True)
        nt = ((i + 1) * TQ + TK - 1) // TK

        def step(j, carry):
            pre, g_sum, dq = carry
            off = pl.multiple_of(j * TK, TK)
            kb = k_ref[pl.ds(off, TK), :].astype(BF16)
            vb = v_ref[pl.ds(off, TK), :].astype(BF16)
            z = lax.dot_general(q, kb, _NT, preferred_element_type=F32) * scale
            valid = kcol + off < qpos
            ls = jnp.where(valid, -_softplus(z), 0.0)
            lb = ls + z
            rs = jnp.sum(ls, axis=1, keepdims=True)
            later = _dot_split(ls, after) + (tot - pre - rs)
            w = jnp.where(valid, jnp.exp(lb + later), 0.0)
            g = lax.dot_general(do, vb, _NT, preferred_element_type=F32) * w
            dls = _dot_split(g, before) + g_sum
            sig = jnp.exp(lb)
            dz = (jnp.where(valid, g * (1.0 - sig) - dls * sig, 0.0) * scale).astype(BF16)
            dq = dq + jnp.dot(dz, kb, preferred_element_type=F32)
            dk_acc[pl.ds(off, TK), :] += lax.dot_general(dz, q, _TN, preferred_element_type=F32)
            dv_acc[pl.ds(off, TK), :] += lax.dot_general(w.astype(BF16), do, _TN, preferred_element_type=F32)
            return pre + rs, g_sum + jnp.sum(g, axis=1, keepdims=True), dq

        zero = jnp.zeros((TQ, 1), F32)
        _, _, dq = lax.fori_loop(0, nt, step, (zero, zero, jnp.zeros((TQ, HEAD_DIM), F32)))
        dq_ref[...] = dq.astype(dq_ref.dtype)

        @pl.when(i == nq - 1)
        def _():
            dk_ref[...] = dk_acc[...].astype(dk_ref.dtype)
            dv_ref[...] = dv_acc[...].astype(dv_ref.dtype)

    qs = pl.BlockSpec((TQ, HEAD_DIM), lambda h, i: (i, h))
    full = pl.BlockSpec((S, HEAD_DIM), lambda h, i: (0, h))
    o = jax.ShapeDtypeStruct((S, nh * HEAD_DIM), BF16)
    return pl.pallas_call(
        body, grid=(nh, nq),
        in_specs=[qs, pl.BlockSpec((S, HEAD_DIM), lambda h, i: (0, nh + h)),
                  pl.BlockSpec((S, HEAD_DIM), lambda h, i: (0, 2 * nh + h)),
                  pl.BlockSpec((None, TQ, 1), lambda h, i: (h, i, 0)), qs],
        out_specs=[qs, full, full], out_shape=[o, o, o],
        scratch_shapes=[pltpu.VMEM((S, HEAD_DIM), F32), pltpu.VMEM((S, HEAD_DIM), F32)],
        compiler_params=_params(("parallel", "arbitrary")), name=name)(proj, proj, proj, tot, dmix)


def _gdn_qkv_fwd(proj, conv_w, nh, *, name):
    S = proj.shape[0]
    GW = nh * HEAD_DIM
    scale = HEAD_DIM ** -0.5

    def body(x_ref, w_ref, o_ref):
        sec = pl.program_id(0) // nh
        c = _dwconv(x_ref[...], w_ref[...])
        s = c * _sigmoid(c)
        r = lax.rsqrt(jnp.sum(s * s, axis=1, keepdims=True) + EPS)
        fac = jnp.where(sec == 0, scale, 1.0)
        o_ref[...] = jnp.where(sec == 2, s, s * (r * fac))

    return pl.pallas_call(
        body, grid=(3 * nh,),
        in_specs=[pl.BlockSpec((S, HEAD_DIM), lambda j: (0, 3 * nh + j)), pl.BlockSpec((SHORT_CONV, HEAD_DIM), lambda j: (0, j))],
        out_specs=pl.BlockSpec((None, S, HEAD_DIM), lambda j: (j // nh, 0, j % nh)),
        out_shape=jax.ShapeDtypeStruct((3, S, GW), F32),
        compiler_params=_params(("parallel",)), name=name)(proj, conv_w)


def _gdn_qkv_bwd(proj, conv_w, dqkv, nh, *, name):
    S = proj.shape[0]
    GW = nh * HEAD_DIM
    scale = HEAD_DIM ** -0.5

    def body(x_ref, w_ref, d_ref, dx_ref, dw_ref):
        sec = pl.program_id(0) // nh
        x, w = x_ref[...], w_ref[...]
        c = _dwconv(x, w)
        sg = _sigmoid(c)
        s = c * sg
        r = lax.rsqrt(jnp.sum(s * s, axis=1, keepdims=True) + EPS)
        sh = s * r
        d = d_ref[...]
        fac = jnp.where(sec == 0, scale, 1.0)
        dn = (r * fac) * (d - sh * jnp.sum(d * sh, axis=1, keepdims=True))
        ds = jnp.where(sec == 2, d, dn)
        dx, dw = _dwconv_bwd(x, w, ds * (sg * (1.0 + c * (1.0 - sg))))
        dx_ref[...] = dx.astype(dx_ref.dtype)
        dw_ref[...] = dw

    return pl.pallas_call(
        body, grid=(3 * nh,),
        in_specs=[pl.BlockSpec((S, HEAD_DIM), lambda j: (0, 3 * nh + j)), pl.BlockSpec((SHORT_CONV, HEAD_DIM), lambda j: (0, j)),
                  pl.BlockSpec((None, S, HEAD_DIM), lambda j: (j // nh, 0, j % nh))],
        out_specs=[pl.BlockSpec((S, HEAD_DIM), lambda j: (0, j)), pl.BlockSpec((SHORT_CONV, HEAD_DIM), lambda j: (0, j))],
        out_shape=[jax.ShapeDtypeStruct((S, 3 * GW), BF16), jax.ShapeDtypeStruct((SHORT_CONV, 3 * GW), F32)],
        compiler_params=_params(("parallel",)), name=name)(proj, conv_w, dqkv)


def _gdn_gates_fwd(proj, ab, nh, *, name):
    S = proj.shape[0]
    C = CHUNK

    def body(x_ref, ab_ref, o_ref):
        ri = lax.broadcasted_iota(jnp.int32, (C, C), 0)
        ci = lax.broadcasted_iota(jnp.int32, (C, C), 1)
        ltri = (ri >= ci).astype(F32)
        lane = lax.broadcasted_iota(jnp.int32, (C, LANES), 1)
        a_coef = -jnp.exp(ab_ref[0:1, :])
        dt = ab_ref[1:2, :]

        def chunk(n, _):
            rows = pl.ds(pl.multiple_of(n * C, C), C)
            x = x_ref[rows, :]
            beta = _sigmoid(x)
            g = jnp.where(jnp.logical_and(lane >= nh, lane < 2 * nh), a_coef * _softplus(x + dt), 0.0)
            gc = _dot_hi(ltri, pltpu.roll(g, nh, 1))
            o_ref[rows, :] = jnp.where(lane < nh, beta, g) + gc
            return 0

        lax.fori_loop(0, S // C, chunk, 0)

    return pl.pallas_call(
        body, grid=(1,),
        in_specs=[pl.BlockSpec((S, LANES), lambda i: (0, 7 * nh)), pl.BlockSpec((2, LANES), lambda i: (0, 0))],
        out_specs=pl.BlockSpec((S, LANES), lambda i: (0, 0)), out_shape=jax.ShapeDtypeStruct((S, LANES), F32),
        compiler_params=_params(("arbitrary",)), name=name)(proj, ab)


def _gdn_gates_bwd(proj, ab, dgt, nh, *, name):
    S = proj.shape[0]
    C = CHUNK

    def body(x_ref, ab_ref, d_ref, dx_ref, dab_ref):
        ri = lax.broadcasted_iota(jnp.int32, (C, C), 0)
        ci = lax.broadcasted_iota(jnp.int32, (C, C), 1)
        utri = (ri <= ci).astype(F32)
        lane = lax.broadcasted_iota(jnp.int32, (C, LANES), 1)
        is_b = lane < nh
        is_a = jnp.logical_and(lane >= nh, lane < 2 * nh)
        a_coef = -jnp.exp(ab_ref[0:1, :])
        dt = ab_ref[1:2, :]

        def chunk(n, carry):
            da_log, ddt = carry
            rows = pl.ds(pl.multiple_of(n * C, C), C)
            x = x_ref[rows, :]
            d = d_ref[rows, :]
            beta = _sigmoid(x)
            dg = pltpu.roll(_dot_hi(utri, jnp.where(lane >= 2 * nh, d, 0.0)), LANES - nh, 1)
            dg = jnp.where(is_a, dg, 0.0)
            dxa = dg * a_coef * _sigmoid(x + dt)
            dxb = jnp.where(is_b, d * beta * (1.0 - beta), 0.0)
            dx_ref[rows, :] = (dxa + dxb).astype(dx_ref.dtype)
            da_log = da_log + jnp.sum(dg * a_coef * _softplus(x + dt), axis=0, keepdims=True)
            return da_log, ddt + jnp.sum(dxa, axis=0, keepdims=True)

        zero = jnp.zeros((1, LANES), F32)
        da_log, ddt = lax.fori_loop(0, S // C, chunk, (zero, zero))
        dab_ref[0:1, :] = da_log
        dab_ref[1:2, :] = ddt

    return pl.pallas_call(
        body, grid=(1,),
        in_specs=[pl.BlockSpec((S, LANES), lambda i: (0, 7 * nh)), pl.BlockSpec((2, LANES), lambda i: (0, 0)),
                  pl.BlockSpec((S, LANES), lambda i: (0, 0))],
        out_specs=[pl.BlockSpec((S, LANES), lambda i: (0, 0)), pl.BlockSpec((2, LANES), lambda i: (0, 0))],
        out_shape=[jax.ShapeDtypeStruct((S, LANES), BF16), jax.ShapeDtypeStruct((2, LANES), F32)],
        compiler_params=_params(("arbitrary",)), name=name)(proj, ab, dgt)


def _unit_lower_inverse(lmat):
    C = lmat.shape[-1]
    ri = lax.broadcasted_iota(jnp.int32, lmat.shape, lmat.ndim - 2)
    ci = lax.broadcasted_iota(jnp.int32, lmat.shape, lmat.ndim - 1)
    nmat = -lmat
    p = jnp.where(ri == ci, 1.0, 0.0) + nmat
    for _ in range(int(math.log2(C)) - 1):
        nmat = _dot_hi(nmat, nmat)
        p = p + _dot_hi(p, nmat)
    return p


def _gdn_chunk_common(q, k, v, gates, gc_row, h, nh, tinv=None):
    C = CHUNK
    lane = lax.broadcasted_iota(jnp.int32, gates.shape, gates.ndim - 1)
    beta = jnp.sum(jnp.where(lane == h, gates, 0.0), axis=-1, keepdims=True)
    gc = jnp.sum(jnp.where(lane == 2 * nh + h, gates, 0.0), axis=-1, keepdims=True)
    sq = gates.shape[:-1] + (C,)
    ri = lax.broadcasted_iota(jnp.int32, sq, len(sq) - 2)
    ci = lax.broadcasted_iota(jnp.int32, sq, len(sq) - 1)
    incl, strict = ri >= ci, ri > ci
    decay = jnp.where(incl, jnp.exp(jnp.where(incl, gc - gc_row, 0.0)), 0.0)
    egc = jnp.exp(gc)
    kb, vb = k * beta, v * beta
    lmat = jnp.where(strict, _dot(kb, k, _NT) * decay, 0.0)
    kbg = kb * egc
    u = w = None
    if tinv is None:
        tinv = _unit_lower_inverse(lmat)
        u = _dot(tinv, vb)
        w = _dot(tinv, kbg)
    amat = _dot(q, k, _NT) * decay
    glast = gc[..., C - 1:C, :]
    ekt = jnp.exp(glast - gc)
    return dict(q=q, k=k, v=v, beta=beta, decay=decay, egc=egc, kb=kb, vb=vb, lmat=lmat, tinv=tinv, kbg=kbg, u=u, w=w,
                amat=amat, qd=q * egc, ekt=ekt, kt=k * ekt, cd=jnp.exp(glast), strict=strict, incl=incl)


def _gdn_chunk_specs(nh, S, nc):
    return [pl.BlockSpec((3, S, HEAD_DIM), lambda h: (0, 0, h)),
            pl.BlockSpec((S, LANES), lambda h: (0, 0)),
            pl.BlockSpec((None, nc, 1, CHUNK), lambda h: (h, 0, 0, 0))]


def _gdn_state_free(qkv_ref, gates_ref, gr_ref, g, nb, h, nh):
    C = CHUNK
    rows = pl.ds(pl.multiple_of(g * (nb * C), nb * C), nb * C)
    part = lambda x: x.reshape(nb, C, x.shape[-1])
    return rows, _gdn_chunk_common(part(qkv_ref[0, rows, :]), part(qkv_ref[1, rows, :]), part(qkv_ref[2, rows, :]),
                                   part(gates_ref[rows, :]), gr_ref[pl.ds(g * nb, nb)], h, nh)


def _gdn_chunk_fwd(qkv, gates, gc_row, *, name):
    _, S, GW = qkv.shape
    nh, C = GW // HEAD_DIM, CHUNK
    nc = S // C
    nb = min(GDN_CPB, nc)
    flat = lambda x: x.reshape(nb * C, x.shape[-1])

    def body(qkv_ref, gates_ref, gr_ref, o_ref, st_ref, u_s, w_s, a_s, qd_s, kt_s, cd_s):
        h = pl.program_id(0)

        def group(g, _):
            rows, m = _gdn_state_free(qkv_ref, gates_ref, gr_ref, g, nb, h, nh)
            u_s[rows, :] = flat(m["u"])
            w_s[rows, :] = flat(m["w"])
            a_s[rows, :] = flat(m["amat"])
            qd_s[rows, :] = flat(m["qd"])
            kt_s[rows, :] = flat(m["kt"])
            cd_s[pl.ds(g * nb, nb)] = jnp.broadcast_to(m["cd"], (nb, 8, LANES))
            return 0

        lax.fori_loop(0, nc // nb, group, 0)

        def chunk(n, s0):
            rows = pl.ds(pl.multiple_of(n * C, C), C)
            st_ref[n] = s0
            v_new = u_s[rows, :] - _dot(w_s[rows, :], s0)
            o_ref[rows, :] = _dot(qd_s[rows, :], s0) + _dot(a_s[rows, :], v_new)
            return s0 * cd_s[n][0:1, :] + _dot(kt_s[rows, :], v_new, _TN)

        lax.fori_loop(0, nc, chunk, jnp.zeros((HEAD_DIM, HEAD_DIM), F32))

    seq = pltpu.VMEM((S, HEAD_DIM), F32)
    return pl.pallas_call(
        body, grid=(nh,), in_specs=_gdn_chunk_specs(nh, S, nc),
        out_specs=[pl.BlockSpec((S, HEAD_DIM), lambda h: (0, h)),
                   pl.BlockSpec((None, nc, HEAD_DIM, HEAD_DIM), lambda h: (h, 0, 0, 0))],
        out_shape=[jax.ShapeDtypeStruct((S, GW), F32), jax.ShapeDtypeStruct((nh, nc, HEAD_DIM, HEAD_DIM), F32)],
        scratch_shapes=[seq, seq, pltpu.VMEM((S, C), F32), seq, seq, pltpu.VMEM((nc, 8, LANES), F32)],
        compiler_params=_params(("parallel",)), name=name)(qkv, gates, gc_row)


def _gdn_chunk_bwd(qkv, gates, gc_row, states, do, *, name):
    _, S, GW = qkv.shape
    nh, C = GW // HEAD_DIM, CHUNK
    nc = S // C
    nb = min(GDN_CPB, nc)
    flat = lambda x: x.reshape(nb * C, x.shape[-1])
    part = lambda x: x.reshape(nb, C, x.shape[-1])

    def body(qkv_ref, gates_ref, gr_ref, st_ref, do_ref, dqkv_ref, dgt_ref,
             t_s, vn_s, w_s, a_s, qd_s, kt_s, cd_s, dvn_s, dkt_s, dcd_s):
        h = pl.program_id(0)

        def group(g, _):
            rows, m = _gdn_state_free(qkv_ref, gates_ref, gr_ref, g, nb, h, nh)
            t_s[rows, :] = flat(m["tinv"])
            vn_s[rows, :] = flat(m["u"])
            w_s[rows, :] = flat(m["w"])
            a_s[rows, :] = flat(m["amat"])
            qd_s[rows, :] = flat(m["qd"])
            kt_s[rows, :] = flat(m["kt"])
            cd_s[pl.ds(g * nb, nb)] = jnp.broadcast_to(m["cd"], (nb, 8, LANES))
            return 0

        lax.fori_loop(0, nc // nb, group, 0)

        def chunk(t, dsn):
            n = nc - 1 - t
            rows = pl.ds(pl.multiple_of(n * C, C), C)
            s0, dout, w = st_ref[n], do_ref[rows, :], w_s[rows, :]
            v_new = vn_s[rows, :] - _dot(w, s0)
            dvn = _dot(a_s[rows, :], dout, _TN) + _dot(kt_s[rows, :], dsn)
            vn_s[rows, :] = v_new
            dvn_s[rows, :] = dvn
            dkt_s[rows, :] = _dot(v_new, dsn, _NT)
            dcd_s[n] = jnp.zeros((8, LANES), F32) + jnp.sum(dsn * s0)
            return _dot(qd_s[rows, :], dout, _TN) + dsn * cd_s[n][0:1, :] - _dot(w, dvn, _TN)

        lax.fori_loop(0, nc, chunk, jnp.zeros((HEAD_DIM, HEAD_DIM), F32))

        def rest(g, _):
            rows, m = _gdn_state_free_again(qkv_ref, gates_ref, gr_ref, t_s, g, nb, h, nh)
            chunks = pl.ds(g * nb, nb)
            dq, dk, dv, dgt = _gdn_chunk_grad(m, st_ref[chunks], part(vn_s[rows, :]), part(dvn_s[rows, :]),
                                              part(dkt_s[rows, :]), dcd_s[chunks][:, 0:1, 0:1], part(do_ref[rows, :]), h, nh)
            dqkv_ref[0, rows, :] = flat(dq)
            dqkv_ref[1, rows, :] = flat(dk)
            dqkv_ref[2, rows, :] = flat(dv)
            dgt_ref[rows, :] = flat(dgt)
            return 0

        lax.fori_loop(0, nc // nb, rest, 0)

    seq = pltpu.VMEM((S, HEAD_DIM), F32)
    small = pltpu.VMEM((nc, 8, LANES), F32)
    return pl.pallas_call(
        body, grid=(nh,),
        in_specs=_gdn_chunk_specs(nh, S, nc) + [
            pl.BlockSpec((None, nc, HEAD_DIM, HEAD_DIM), lambda h: (h, 0, 0, 0)),
            pl.BlockSpec((S, HEAD_DIM), lambda h: (0, h))],
        out_specs=[pl.BlockSpec((3, S, HEAD_DIM), lambda h: (0, 0, h)), pl.BlockSpec((None, S, LANES), lambda h: (h, 0, 0))],
        out_shape=[jax.ShapeDtypeStruct((3, S, GW), F32), jax.ShapeDtypeStruct((nh, S, LANES), F32)],
        scratch_shapes=[pltpu.VMEM((S, C), F32), seq, seq, pltpu.VMEM((S, C), F32), seq, seq, small, seq, seq, small],
        compiler_params=_params(("parallel",)), name=name)(qkv, gates, gc_row, states, do)


def _gdn_state_free_again(qkv_ref, gates_ref, gr_ref, t_s, g, nb, h, nh):
    C = CHUNK
    rows = pl.ds(pl.multiple_of(g * (nb * C), nb * C), nb * C)
    part = lambda x: x.reshape(nb, C, x.shape[-1])
    m = _gdn_chunk_common(part(qkv_ref[0, rows, :]), part(qkv_ref[1, rows, :]), part(qkv_ref[2, rows, :]),
                          part(gates_ref[rows, :]), gr_ref[pl.ds(g * nb, nb)], h, nh, tinv=part(t_s[rows, :]))
    return rows, m


def _gdn_chunk_grad(m, s0, v_new, dvn, dkt, dcd, dout, h, nh):
    C = CHUNK
    q, k, v, beta, decay, egc = m["q"], m["k"], m["v"], m["beta"], m["decay"], m["egc"]
    tinv, kt, cd = m["tinv"], m["kt"], m["cd"]
    dqd = _dot(dout, s0, _NT)
    damat = jnp.where(m["incl"], _dot(dout, v_new, _NT), 0.0)
    dw = -_dot(dvn, s0, _NT)
    dvb = _dot(tinv, dvn, _TN)
    dkbg = _dot(tinv, dw, _TN)
    dtinv = _dot(dvn, m["vb"], _NT) + _dot(dw, m["kbg"], _NT)
    dl = jnp.where(m["strict"], -_dot_hi(_dot_hi(tinv, dtinv, _TN), tinv, _NT), 0.0)
    dkk = dl * decay
    dqk = damat * decay
    dkb = _dot(dkk, k) + dkbg * egc
    dk = _dot(dkk, m["kb"], _TN) + _dot(dqk, q, _TN) + dkt * m["ekt"] + dkb * beta
    dq = _dot(dqk, k) + dqd * egc
    mm = dl * m["lmat"] + damat * m["amat"]
    ones = jnp.ones(q.shape, F32)
    rk = jnp.sum(dkt * kt, axis=-1, keepdims=True)
    dgc = (_dot_hi(mm, ones) - _dot_hi(mm, ones, _TN) + jnp.sum(dqd * m["qd"], axis=-1, keepdims=True) - rk
           + jnp.sum(dkbg * m["kbg"], axis=-1, keepdims=True))
    dglast = jnp.sum(rk, axis=-2, keepdims=True) + dcd * cd
    rowi = lax.broadcasted_iota(jnp.int32, q.shape, q.ndim - 2)
    lane = lax.broadcasted_iota(jnp.int32, q.shape, q.ndim - 1)
    dgc = dgc + jnp.where(rowi == C - 1, dglast, 0.0)
    dbeta = jnp.sum(dkb * k, axis=-1, keepdims=True) + jnp.sum(dvb * v, axis=-1, keepdims=True)
    dgt = jnp.where(lane == h, dbeta, 0.0) + jnp.where(lane == 2 * nh + h, dgc, 0.0)
    return dq, dk, dvb * beta, dgt


def _gdn_post_fwd(o, proj, ng, nh, *, name):
    S, GW = o.shape

    def body(o_ref, z_ref, g_ref, y_ref):
        ov, z = o_ref[...], z_ref[...]
        rstd = lax.rsqrt(jnp.mean(ov * ov, axis=-1, keepdims=True) + EPS)
        y_ref[...] = (ov * rstd * g_ref[...] * (z * _sigmoid(z))).astype(y_ref.dtype)

    blk = pl.BlockSpec((S, HEAD_DIM), lambda h: (0, h))
    return pl.pallas_call(
        body, grid=(nh,), in_specs=[blk, pl.BlockSpec((S, HEAD_DIM), lambda h: (0, 6 * nh + h)), pl.BlockSpec((1, HEAD_DIM), lambda h: (0, 0))],
        out_specs=blk, out_shape=jax.ShapeDtypeStruct((S, GW), BF16),
        compiler_params=_params(("parallel",)), name=name)(o, proj, ng)


def _gdn_post_bwd(o, proj, ng, dmix, nh, *, name):
    S, GW = o.shape

    def body(o_ref, z_ref, g_ref, d_ref, do_ref, dz_ref, dg_ref):
        ov, z, d = o_ref[...], z_ref[...], d_ref[...].astype(F32)
        rstd = lax.rsqrt(jnp.mean(ov * ov, axis=-1, keepdims=True) + EPS)
        oh = ov * rstd
        sz = _sigmoid(z)
        dy = d * (z * sz)
        dz_ref[...] = (d * (oh * g_ref[...]) * (sz * (1.0 + z * (1.0 - sz)))).astype(dz_ref.dtype)
        t = dy * g_ref[...]
        do_ref[...] = rstd * (t - oh * jnp.mean(t * oh, axis=-1, keepdims=True))
        part = jnp.sum(dy * oh, axis=0, keepdims=True)

        @pl.when(pl.program_id(0) == 0)
        def _():
            dg_ref[...] = part

        @pl.when(pl.program_id(0) > 0)
        def _():
            dg_ref[...] += part

    blk = pl.BlockSpec((S, HEAD_DIM), lambda h: (0, h))
    vec = pl.BlockSpec((1, HEAD_DIM), lambda h: (0, 0))
    return pl.pallas_call(
        body, grid=(nh,),
        in_specs=[blk, pl.BlockSpec((S, HEAD_DIM), lambda h: (0, 6 * nh + h)), vec, pl.BlockSpec((S, HEAD_DIM), lambda h: (0, nh + h))],
        out_specs=[blk, blk, vec],
        out_shape=[jax.ShapeDtypeStruct((S, GW), F32), jax.ShapeDtypeStruct((S, GW), BF16), jax.ShapeDtypeStruct((1, HEAD_DIM), F32)],
        compiler_params=_params(("arbitrary",)), name=name)(o, proj, ng, dmix)


def _gdn_forward(proj, conv_w, ab, ng, nh, tag):
    S = proj.shape[0]
    nc = S // CHUNK
    qkv = _gdn_qkv_fwd(proj, conv_w, nh, name=f"gdn_qkv_fwd{tag}")
    gates = _gdn_gates_fwd(proj, ab, nh, name=f"gdn_gates_fwd{tag}")
    gc_row = gates[:, 2 * nh:3 * nh].T.reshape(nh, nc, 1, CHUNK)
    o, states = _gdn_chunk_fwd(qkv, gates, gc_row, name=f"gdn_chunk_fwd{tag}")
    y = _gdn_post_fwd(o, proj, ng, nh, name=f"gdn_post_fwd{tag}")
    return y, (qkv, gates, gc_row, states, o)


def _gdn_backward(proj, conv_w, ab, ng, saved, dmix, nh, tag):
    qkv, gates, gc_row, states, o = saved
    do, dz, dng = _gdn_post_bwd(o, proj, ng, dmix, nh, name=f"gdn_post_bwd{tag}")
    dqkv, dgt_heads = _gdn_chunk_bwd(qkv, gates, gc_row, states, do, name=f"gdn_chunk_bwd{tag}")
    dx_qkv, dconv = _gdn_qkv_bwd(proj, conv_w, dqkv, nh, name=f"gdn_qkv_bwd{tag}")
    dx_g, dab = _gdn_gates_bwd(proj, ab, jnp.sum(dgt_heads, axis=0), nh, name=f"gdn_gates_bwd{tag}")
    return dx_qkv, dz, dx_g, dconv, dab, dng


def _row_tile(r, cap=128):
    t = cap
    while r % t:
        t //= 2
    assert t >= 8, r
    return t


def _adamw_update(gv, w_ref, m_ref, v_ref, d_ref, m2_ref, v2_ref):
    m2 = ADAM_B1 * m_ref[...] + (1.0 - ADAM_B1) * gv
    v2 = ADAM_B2 * v_ref[...] + (1.0 - ADAM_B2) * (gv * gv)
    m_hat = m2 / (1.0 - ADAM_B1 ** ADAM_STEP)
    v_hat = v2 / (1.0 - ADAM_B2 ** ADAM_STEP)
    d_ref[...] = -ADAM_LR * (m_hat / (jnp.sqrt(v_hat) + ADAM_EPS) + ADAM_WD * w_ref[...])
    m2_ref[...] = m2
    v2_ref[...] = v2


def _adamw(w, g, m, v, *, name):
    L, r, c = w.shape
    tr = _row_tile(r)

    def body(w_ref, g_ref, m_ref, v_ref, d_ref, m2_ref, v2_ref):
        _adamw_update(g_ref[...], w_ref, m_ref, v_ref, d_ref, m2_ref, v2_ref)

    blk = pl.BlockSpec((None, tr, c), lambda l, i: (l, i, 0))
    o = jax.ShapeDtypeStruct(w.shape, F32)
    return pl.pallas_call(
        body, grid=(L, r // tr), in_specs=[blk] * 4, out_specs=[blk] * 3, out_shape=[o, o, o],
        compiler_params=_params(("parallel", "parallel")), name=name)(w, g, m, v)


def _adamw_halves(w, g_own, g_sib, cvec, m, v, *, name):
    L, r, c = w.shape
    tr = _row_tile(r // 2)
    nbh = (r // 2) // tr

    def body(c_ref, w_ref, go_ref, gs_ref, m_ref, v_ref, g_out, d_ref, m2_ref, v2_ref):
        gv = jnp.where(pl.program_id(1) // nbh == c_ref[0], go_ref[...], gs_ref[...])
        g_out[...] = gv
        _adamw_update(gv, w_ref, m_ref, v_ref, d_ref, m2_ref, v2_ref)

    lo = lambda i: jnp.minimum(i, nbh - 1)
    hi = lambda i: jnp.maximum(i - nbh, 0)
    blk = pl.BlockSpec((None, tr, c), lambda l, i, c_ref: (l, i, 0))
    own = pl.BlockSpec((None, tr, c), lambda l, i, c_ref: (l, jnp.where(c_ref[0] == 0, lo(i), hi(i)), 0))
    sib = pl.BlockSpec((None, tr, c), lambda l, i, c_ref: (l, jnp.where(c_ref[0] == 0, hi(i), lo(i)), 0))
    o = jax.ShapeDtypeStruct(w.shape, F32)
    return pl.pallas_call(
        body,
        grid_spec=pltpu.PrefetchScalarGridSpec(
            num_scalar_prefetch=1, grid=(L, r // tr), in_specs=[blk, own, sib, blk, blk], out_specs=[blk] * 4),
        out_shape=[o, o, o, o],
        compiler_params=_params(("parallel", "arbitrary")), name=name)(cvec, w, g_own, g_sib, m, v)


def _sum_half(g, rbuf, cvec, *, name):
    _, r, c = g.shape
    h = r // 2
    tr = _row_tile(h)
    nb = h // tr

    def body(c_ref, g_ref, r_ref, o_ref):
        o_ref[...] = (g_ref[...] + r_ref[...]).astype(o_ref.dtype)

    blk = pl.BlockSpec((None, tr, c), lambda s, i, c_ref: (s, i, 0))
    return pl.pallas_call(
        body,
        grid_spec=pltpu.PrefetchScalarGridSpec(
            num_scalar_prefetch=1, grid=(4, nb),
            in_specs=[pl.BlockSpec((None, tr, c), lambda s, i, c_ref: (s, c_ref[0] * nb + i, 0)), blk], out_specs=blk),
        out_shape=jax.ShapeDtypeStruct((4, h, c), BF16),
        compiler_params=_params(("parallel", "parallel")), name=name)(cvec, g, rbuf)


def _sum_chips(rb, prev, l, nl, *, name):
    _, h, c = rb.shape
    tr = _row_tile(h)

    def body(*refs):
        r_ref, o_ref = refs[0], refs[-1]
        acc = r_ref[0].astype(F32)
        for s in range(1, 4):
            acc = acc + r_ref[s].astype(F32)
        o_ref[...] = acc

    in_specs = [pl.BlockSpec((4, tr, c), lambda i: (0, i, 0))]
    args = [rb]
    if prev is not None:
        in_specs.append(pl.BlockSpec(memory_space=pltpu.HBM))
        args.append(prev)
    return pl.pallas_call(
        body, grid=(h // tr,), in_specs=in_specs, out_specs=pl.BlockSpec((None, tr, c), lambda i: (l, i, 0)),
        out_shape=jax.ShapeDtypeStruct((nl, h, c), F32), input_output_aliases={1: 0} if prev is not None else {},
        compiler_params=_params(("parallel",)), name=name)(*args)


_MESH = pl.DeviceIdType.MESH
_HBM = pl.BlockSpec(memory_space=pltpu.HBM)


def _place():
    x, y, c = lax.axis_index("x"), lax.axis_index("y"), lax.axis_index("c")
    return x, y, c, [(1 - x, y), (x, 1 - y), (1 - x, 1 - y)]


def _cast_place(w, l, kvec, *, name):
    _, r, c = w.shape
    tr = _row_tile(r, 256)

    def body(k_ref, w_ref, o_ref):
        o_ref[...] = w_ref[...].astype(o_ref.dtype)

    return pl.pallas_call(
        body,
        grid_spec=pltpu.PrefetchScalarGridSpec(
            num_scalar_prefetch=1, grid=(r // tr,),
            in_specs=[pl.BlockSpec((None, tr, c), lambda i, k_ref: (l, i, 0))],
            out_specs=pl.BlockSpec((None, tr, c), lambda i, k_ref: (k_ref[0], i, 0))),
        out_shape=jax.ShapeDtypeStruct((4, r, c), BF16),
        compiler_params=_params(("parallel",)), name=name)(kvec, w)


def _gather_weights(ws, *, name):
    n = len(ws)

    def body(*refs):
        o_refs = refs[n:2 * n]
        ici_s, ici_r, fwd_s, fwd_r = refs[2 * n:]
        x, y, c, chips = _place()
        k = 2 * x + y
        started = []
        for a in range(n):
            h = o_refs[a].shape[1] // 2
            for j, (cx, cy) in enumerate(chips):
                mine = o_refs[a].at[k, pl.ds(c * h, h), :]
                cp = pltpu.make_async_remote_copy(
                    src_ref=mine, dst_ref=mine, send_sem=ici_s.at[3 * a + j], recv_sem=ici_r.at[3 * a + j],
                    device_id=(cx, cy, c), device_id_type=_MESH)
                cp.start()
                started.append(cp)
        for a in range(n):
            h = o_refs[a].shape[1] // 2
            for j, (cx, cy) in enumerate(chips):
                landed = o_refs[a].at[2 * cx + cy, pl.ds(c * h, h), :]
                pltpu.make_async_remote_copy(
                    src_ref=landed, dst_ref=landed, send_sem=ici_s.at[3 * a + j], recv_sem=ici_r.at[3 * a + j],
                    device_id=(x, y, c), device_id_type=_MESH).wait_recv()
                fw = pltpu.make_async_remote_copy(
                    src_ref=landed, dst_ref=landed, send_sem=fwd_s.at[3 * a + j], recv_sem=fwd_r.at[3 * a + j],
                    device_id=(x, y, 1 - c), device_id_type=_MESH)
                fw.start()
                started.append(fw)
        for a in range(n):
            h = o_refs[a].shape[1] // 2
            for j, (cx, cy) in enumerate(chips):
                other = o_refs[a].at[2 * cx + cy, pl.ds((1 - c) * h, h), :]
                pltpu.make_async_remote_copy(
                    src_ref=other, dst_ref=other, send_sem=fwd_s.at[3 * a + j], recv_sem=fwd_r.at[3 * a + j],
                    device_id=(x, y, c), device_id_type=_MESH).wait_recv()
        for cp in started:
            cp.wait_send()

    return pl.pallas_call(
        body, in_specs=[_HBM] * n, out_specs=[_HBM] * n,
        out_shape=[jax.ShapeDtypeStruct(w.shape, w.dtype) for w in ws],
        input_output_aliases={a: a for a in range(n)},
        scratch_shapes=[pltpu.SemaphoreType.DMA((3 * n,))] * 4, name=name)(*ws)


def _exchange_sibling(gs, *, name):
    n = len(gs)

    def body(*refs):
        g_refs, o_refs = refs[:n], refs[n:2 * n]
        send, recv = refs[2 * n:]
        x, y, c, _ = _place()
        cps = []
        for a in range(n):
            h = g_refs[a].shape[1] // 2
            cp = pltpu.make_async_remote_copy(
                src_ref=g_refs[a].at[:, pl.ds((1 - c) * h, h), :], dst_ref=o_refs[a], send_sem=send.at[a], recv_sem=recv.at[a],
                device_id=(x, y, 1 - c), device_id_type=_MESH)
            cp.start()
            cps.append(cp)
        for cp in cps:
            cp.wait_recv()
        for cp in cps:
            cp.wait_send()

    return pl.pallas_call(
        body, in_specs=[_HBM] * n, out_specs=[_HBM] * n,
        out_shape=[jax.ShapeDtypeStruct((4, g.shape[1] // 2, g.shape[2]), g.dtype) for g in gs],
        scratch_shapes=[pltpu.SemaphoreType.DMA((n,))] * 2, name=name)(*gs)


def _exchange_chips(ps, *, name):
    n = len(ps)

    def body(*refs):
        p_refs, o_refs = refs[:n], refs[n:2 * n]
        send, recv, loc = refs[2 * n:]
        x, y, c, chips = _place()
        k = 2 * x + y
        cps, local = [], []
        for a in range(n):
            mine = pltpu.make_async_copy(p_refs[a].at[k], o_refs[a].at[k], loc.at[a])
            mine.start()
            local.append(mine)
            for j, (cx, cy) in enumerate(chips):
                cp = pltpu.make_async_remote_copy(
                    src_ref=p_refs[a].at[2 * cx + cy], dst_ref=o_refs[a].at[k], send_sem=send.at[3 * a + j],
                    recv_sem=recv.at[3 * a + j], device_id=(cx, cy, c), device_id_type=_MESH)
                cp.start()
                cps.append(cp)
        for a in range(n):
            for j, (cx, cy) in enumerate(chips):
                slot = o_refs[a].at[2 * cx + cy]
                pltpu.make_async_remote_copy(
                    src_ref=slot, dst_ref=slot, send_sem=send.at[3 * a + j], recv_sem=recv.at[3 * a + j],
                    device_id=(x, y, c), device_id_type=_MESH).wait_recv()
        for cp in cps:
            cp.wait_send()
        for cp in local:
            cp.wait()

    return pl.pallas_call(
        body, in_specs=[_HBM] * n, out_specs=[_HBM] * n,
        out_shape=[jax.ShapeDtypeStruct(p.shape, p.dtype) for p in ps],
        scratch_shapes=[pltpu.SemaphoreType.DMA((3 * n,))] * 2 + [pltpu.SemaphoreType.DMA((n,))], name=name)(*ps)


def _swap_with_sibling(gs, *, name):
    n = len(gs)

    def body(*refs):
        g_refs, o_refs = refs[:n], refs[n:2 * n]
        send, recv = refs[2 * n:]
        x, y, c, _ = _place()
        cps = []
        for a in range(n):
            cp = pltpu.make_async_remote_copy(
                src_ref=g_refs[a], dst_ref=o_refs[a], send_sem=send.at[a], recv_sem=recv.at[a],
                device_id=(x, y, 1 - c), device_id_type=_MESH)
            cp.start()
            cps.append(cp)
        for cp in cps:
            cp.wait_recv()
        for cp in cps:
            cp.wait_send()

    return pl.pallas_call(
        body, in_specs=[_HBM] * n, out_specs=[_HBM] * n,
        out_shape=[jax.ShapeDtypeStruct(g.shape, g.dtype) for g in gs],
        scratch_shapes=[pltpu.SemaphoreType.DMA((n,))] * 2, name=name)(*gs)


def _allreduce_small(v, *, name):
    R = v.shape[0]

    def body(v_ref, o_ref, buf, send, recv, loc):
        x, y, c = lax.axis_index("x"), lax.axis_index("y"), lax.axis_index("c")
        me = 4 * x + 2 * y + c
        mine = pltpu.make_async_copy(v_ref, buf.at[me], loc)
        mine.start()
        cps = []
        for d in range(1, 8):
            px = 1 - x if d & 4 else x
            py = 1 - y if d & 2 else y
            pc = 1 - c if d & 1 else c
            cp = pltpu.make_async_remote_copy(
                src_ref=v_ref, dst_ref=buf.at[me], send_sem=send.at[d - 1], recv_sem=recv.at[d - 1],
                device_id=(px, py, pc), device_id_type=_MESH)
            cp.start()
            cps.append((cp, 4 * px + 2 * py + pc))
        for d in range(1, 8):
            cp, peer = cps[d - 1]
            pltpu.make_async_remote_copy(
                src_ref=buf.at[peer], dst_ref=buf.at[peer], send_sem=send.at[d - 1], recv_sem=recv.at[d - 1],
                device_id=(x, y, c), device_id_type=_MESH).wait_recv()
        for cp, _ in cps:
            cp.wait_send()
        mine.wait()
        acc = buf[0]
        for i in range(1, 8):
            acc = acc + buf[i]
        o_ref[...] = acc

    return pl.pallas_call(
        body, in_specs=[pl.BlockSpec(memory_space=pltpu.VMEM)], out_specs=pl.BlockSpec(memory_space=pltpu.VMEM),
        out_shape=jax.ShapeDtypeStruct((R, LANES), F32),
        scratch_shapes=[pltpu.VMEM((8, R, LANES), F32), pltpu.SemaphoreType.DMA((7,)), pltpu.SemaphoreType.DMA((7,)),
                        pltpu.SemaphoreType.DMA],
        compiler_params=pltpu.CompilerParams(vmem_limit_bytes=VMEM_LIMIT), name=name)(v)


def _pack(arrs, row_multiple=8):
    rows = []
    for a in arrs:
        flat = a.reshape(-1)
        flat = jnp.pad(flat, (0, (-flat.shape[0]) % LANES))
        rows.append(flat.reshape(-1, LANES))
    buf = jnp.concatenate(rows, axis=0)
    return jnp.pad(buf, ((0, (-buf.shape[0]) % row_multiple), (0, 0)))


def _unpack(buf, shapes):
    out, r = [], 0
    for s in shapes:
        size = math.prod(s)
        nr = -(-size // LANES)
        out.append(buf[r:r + nr].reshape(-1)[:size].reshape(s))
        r += nr
    return out


def kernel(x, mem, mix_norm, w_in, gdn_conv, gdn_a_log, gdn_dt_bias, gdn_norm, w_out, xattn_norm, mem_norm, w_xq, w_xkv, w_xo, ffn_norm, w_up, ffn_conv, ffn_conv_bias, w_down, final_norm, loss_target, m_mix_norm, m_w_in, m_gdn_conv, m_gdn_a_log, m_gdn_dt_bias, m_gdn_norm, m_w_out, m_xattn_norm, m_mem_norm, m_w_xq, m_w_xkv, m_w_xo, m_ffn_norm, m_w_up, m_ffn_conv, m_ffn_conv_bias, m_w_down, m_final_norm, v_mix_norm, v_w_in, v_gdn_conv, v_gdn_a_log, v_gdn_dt_bias, v_gdn_norm, v_w_out, v_xattn_norm, v_mem_norm, v_w_xq, v_w_xkv, v_w_xo, v_ffn_norm, v_w_up, v_ffn_conv, v_ffn_conv_bias, v_w_down, v_final_norm):
    L = w_in.shape[0]
    _, S, D = x.shape
    nh = D // (2 * HEAD_DIM)
    GW = nh * HEAD_DIM
    n_in = 7 * GW + 2 * nh
    NP = 7 * GW + LANES
    XW = X_HEADS * HEAD_DIM
    F = w_down.shape[1] * 4
    cs_in = w_in.shape[2]
    cs_up = w_up.shape[2]
    cs_xo = w_xo.shape[2]
    tu = _tile(cs_up, 1408)
    per = cs_up // tu
    fper = F // tu
    assert n_in == 4 * cs_in and F % tu == 0 and 2 * F == 4 * cs_up

    xi, yi, ci = lax.axis_index("x"), lax.axis_index("y"), lax.axis_index("c")
    chip = 2 * xi + yi
    cvec = jnp.reshape(ci, (1,)).astype(jnp.int32)

    cs_gc, cs_fc = gdn_conv.shape[2], ffn_conv.shape[2]
    keep = jnp.where(ci == 0, 1.0, 0.0).astype(F32)
    gc_full = lax.dynamic_update_slice(jnp.zeros((L, SHORT_CONV, 4 * cs_gc), F32), gdn_conv * keep, (0, 0, chip * cs_gc))
    fc_full = lax.dynamic_update_slice(jnp.zeros((L, FFN_CONV, 4 * cs_fc), F32), ffn_conv * keep, (0, 0, chip * cs_fc))
    conv_all = _allreduce_small(_pack([gc_full, fc_full]), name="allgather_conv")
    gdn_conv_full, ffn_conv_full = _unpack(conv_all, [gc_full.shape, fc_full.shape])

    big = [w_in, w_out, w_xq, w_xkv, w_xo, w_up, w_down]
    kvec = jnp.reshape(chip, (1,)).astype(jnp.int32)
    ab = jnp.zeros((L, 2, LANES), F32).at[:, 0, nh:2 * nh].set(gdn_a_log).at[:, 1, nh:2 * nh].set(gdn_dt_bias)

    def vec(p, l):
        return p[l:l + 1]

    xo_fwd_b = pl.BlockSpec((None, XW, cs_xo), lambda i, j, k: (j, 0, 0))
    xo_dg_b = pl.BlockSpec((None, XW, cs_xo), lambda i, j, k: (k, 0, 0))
    xo_wg_o = pl.BlockSpec((None, XW, cs_xo), lambda i, j, k: (j, 0, 0))
    up_fwd_b = pl.BlockSpec((None, D, tu), lambda i, j, k: (j // per, 0, j % per))

    def fwd_layer(l, xc, wts):
        g_in, g_out, g_xq, g_xkv, g_xo, g_up, g_down = wts
        w_in_l = jnp.concatenate([g_in[0], g_in[1], g_in[2], g_in[3], jnp.zeros((D, NP - n_in), BF16)], axis=1)
        w_out_l, w_xq_l, w_xkv_l, w_down_l = g_out.reshape(2 * GW, D), g_xq.reshape(D, XW), g_xkv.reshape(D, 2 * XW), g_down.reshape(F, D)
        s = dict(x0=xc, w_in=w_in_l, w_out=w_out_l, w_xq=w_xq_l, w_xkv=w_xkv_l, w_xo=g_xo, w_up=g_up, w_down=w_down_l)
        s["h"] = _rms_fwd(xc, vec(mix_norm, l), name="rms_mix_fwd")
        s["proj"] = _matmul(s["h"], w_in_l, tn=2432, tk=D, name="mm_in_fwd")
        s["sb"], s["tot"] = _sb_fwd(s["proj"], nh, name="sb_fwd")
        gdn_out, s["gdn"] = _gdn_forward(s["proj"], gdn_conv_full[l], ab[l], vec(gdn_norm, l), nh, "")
        s["mixed"] = jnp.concatenate([s["sb"], gdn_out], axis=1)
        s["x1"] = _matmul(s["mixed"], w_out_l, res=xc, tk=2 * GW, name="mm_out_fwd")
        s["memn"] = _rms_fwd(mem[0], vec(mem_norm, l), name="rms_mem_fwd")
        s["kv"] = _matmul(s["memn"], w_xkv_l, out_dtype=BF16, tk=D, name="mm_xkv_fwd")
        s["hq"] = _rms_fwd(s["x1"], vec(xattn_norm, l), name="rms_xattn_fwd")
        s["q"] = _matmul(s["hq"], w_xq_l, out_dtype=BF16, tk=D, name="mm_xq_fwd")
        s["xo"] = _xattn_fwd(s["q"], s["kv"], name="xattn_fwd")
        s["x2"] = _matmul(s["xo"], g_xo, res=s["x1"], dims=(S, D, XW), tn=cs_xo, tk=XW, b_spec=xo_fwd_b, name="mm_xo_fwd")
        s["hf"] = _rms_fwd(s["x2"], vec(ffn_norm, l), name="rms_ffn_fwd")
        s["u"] = _matmul(s["hf"], g_up, dims=(S, 2 * F, D), tn=tu, tk=D, b_spec=up_fwd_b, name="mm_up_fwd")
        s["act"] = _ffn_act_fwd(s["u"], ffn_conv_full[l], ffn_conv_bias[l:l + 1], name="ffn_act_fwd")
        x3 = _matmul(s["act"], w_down_l, res=s["x2"], tk=tu, name="mm_down_fwd")
        return x3, s

    def bwd_layer(l, s, dx3):
        dact = _matmul(dx3, s["w_down"], tb=True, tk=D, name="mm_down_dgrad")
        d_down = _matmul(s["act"], dx3, ta=True, tk=S, name="mm_down_wgrad")
        du3, dcw3, dcb3 = _ffn_act_bwd(s["u"], ffn_conv_full[l], ffn_conv_bias[l:l + 1], dact, name="ffn_act_bwd")
        dhf = _matmul(du3, s["w_up"], tb=True, dims=(S, D, 2 * F), tk=tu,
                      a_spec=pl.BlockSpec((None, _tile(S, 512), tu), lambda i, j, k: (k // fper, i, k % fper)),
                      b_spec=pl.BlockSpec((None, _tile(D, 512), tu), lambda i, j, k: (k // per, j, k % per)), name="mm_up_dgrad")
        d_up = _matmul(s["hf"], du3, ta=True, dims=(D, 2 * F, S), tn=tu, tk=S,
                       b_spec=pl.BlockSpec((None, S, tu), lambda i, j, k: (j // fper, 0, j % fper)),
                       o_spec=pl.BlockSpec((None, _tile(D, 512), tu), lambda i, j, k: (j // per, i, j % per)),
                       out_shape=jax.ShapeDtypeStruct((4, D, cs_up), F32), name="mm_up_wgrad")
        dx2, dg_ffn = _rms_bwd(s["x2"], vec(ffn_norm, l), dhf, dx3, name="rms_bwd")
        dxo = _matmul(dx2, s["w_xo"], tb=True, dims=(S, XW, D), tn=XW, tk=cs_xo, b_spec=xo_dg_b, name="mm_xo_dgrad")
        d_xo = _matmul(s["xo"], dx2, ta=True, dims=(XW, D, S), tm=XW, tn=cs_xo, tk=S, o_spec=xo_wg_o,
                       out_shape=jax.ShapeDtypeStruct((4, XW, cs_xo), F32), name="mm_xo_wgrad")
        dq, dk, dv = _xattn_bwd(s["q"], s["kv"], dxo, name="xattn_bwd")
        dkv = jnp.concatenate([dk, dv], axis=1)
        dhq = _matmul(dq, s["w_xq"], tb=True, tk=XW, name="mm_xq_dgrad")
        d_xq = _matmul(s["hq"], dq, ta=True, tk=S, name="mm_xq_wgrad")
        dmemn = _matmul(dkv, s["w_xkv"], tb=True, tk=2 * XW, name="mm_xkv_dgrad")
        d_xkv = _matmul(s["memn"], dkv, ta=True, tk=mem.shape[1], name="mm_xkv_wgrad")
        _, dg_mem = _rms_bwd(mem[0], vec(mem_norm, l), dmemn, None, name="rms_mem_bwd")
        dx1, dg_xattn = _rms_bwd(s["x1"], vec(xattn_norm, l), dhq, dx2, name="rms_bwd")
        dmix = _matmul(dx1, s["w_out"], tb=True, tk=D, name="mm_out_dgrad")
        d_out = _matmul(s["mixed"], dx1, ta=True, tk=S, name="mm_out_wgrad")
        dq_s, dk_s, dv_s = _sb_bwd(s["proj"], s["tot"], dmix, nh, name="sb_bwd")
        dx_qkv, dz, dx_g, dconv, dab, dng = _gdn_backward(s["proj"], gdn_conv_full[l], ab[l], vec(gdn_norm, l), s["gdn"], dmix, nh, "")
        dproj = jnp.concatenate([dq_s, dk_s, dv_s, dx_qkv, dz, dx_g], axis=1)
        dh = _matmul(dproj, s["w_in"], tb=True, tk=2432, name="mm_in_dgrad")
        d_in = _matmul(s["h"], dproj, ta=True, tn=2432, tk=S, name="mm_in_wgrad")
        dx0, dg_mix = _rms_bwd(s["x0"], vec(mix_norm, l), dh, dx1, name="rms_bwd")
        slabs = [jnp.stack([d_in[:, i * cs_in:(i + 1) * cs_in] for i in range(4)]), d_out.reshape(4, -1, D),
                 d_xq.reshape(4, -1, XW), d_xkv.reshape(4, -1, 2 * XW), d_xo, d_up, d_down.reshape(4, -1, D)]
        small = [dg_mix, dconv, dab, dng, dg_xattn, dg_mem, dg_ffn,
                 jnp.concatenate([dcw3[0], dcw3[1]], axis=1), jnp.concatenate([dcb3[0], dcb3[1]], axis=1)]
        return dx0, slabs, small

    xc = x[0]
    saved = []
    for l in range(L):
        placed = [_cast_place(w, l, kvec, name=f"cast_place_{l}") for w in big]
        wts = _gather_weights(placed, name="gather_weights")
        xc, s = fwd_layer(l, xc, wts)
        saved.append(s)
    loss_blk, dxc, dg_final = _loss_head(xc, final_norm[None, :], loss_target[0], name="loss_head")

    sums = [None] * 7
    small_by_layer = [None] * L
    for l in reversed(range(L)):
        dxc, slabs, small_by_layer[l] = bwd_layer(l, saved[l], dxc)
        saved[l] = None
        from_sibling = _exchange_sibling(slabs, name="reduce_to_sibling")
        partial = [_sum_half(g, r, cvec, name="sum_sibling") for g, r in zip(slabs, from_sibling)]
        from_chips = _exchange_chips(partial, name="reduce_to_chip")
        sums = [_sum_chips(rb, prev, l, L, name=f"sum_chips_{l}") for rb, prev in zip(from_chips, sums)]
    from_sib = _swap_with_sibling(sums, name="swap_halves")

    small_flat = [a for l in range(L) for a in small_by_layer[l]] + [dg_final, loss_blk[0:1]]
    red = _unpack(_allreduce_small(_pack(small_flat), name="allreduce_small"), [a.shape for a in small_flat])
    per_layer = [red[9 * l:9 * l + 9] for l in range(L)]
    col = lambda i: jnp.concatenate([p[i] for p in per_layer], axis=0)
    stk = lambda i: jnp.stack([p[i] for p in per_layer])
    g_conv_full, g_ab, g_fconv_full = stk(1), stk(2), stk(7)
    grads_small = dict(
        mix_norm=col(0), gdn_conv=lax.dynamic_slice(g_conv_full, (0, 0, chip * cs_gc), (L, SHORT_CONV, cs_gc)),
        gdn_a_log=g_ab[:, 0, nh:2 * nh], gdn_dt_bias=g_ab[:, 1, nh:2 * nh], gdn_norm=col(3), xattn_norm=col(4),
        mem_norm=col(5), ffn_norm=col(6), ffn_conv=lax.dynamic_slice(g_fconv_full, (0, 0, chip * cs_fc), (L, FFN_CONV, cs_fc)),
        ffn_conv_bias=col(8), final_norm=red[-2][0])
    loss = red[-1][0, 0]

    names_small = ["mix_norm", "gdn_conv", "gdn_a_log", "gdn_dt_bias", "gdn_norm", "xattn_norm", "mem_norm", "ffn_norm",
                   "ffn_conv", "ffn_conv_bias", "final_norm"]
    w_small = dict(mix_norm=mix_norm, gdn_conv=gdn_conv, gdn_a_log=gdn_a_log, gdn_dt_bias=gdn_dt_bias, gdn_norm=gdn_norm,
                   xattn_norm=xattn_norm, mem_norm=mem_norm, ffn_norm=ffn_norm, ffn_conv=ffn_conv, ffn_conv_bias=ffn_conv_bias,
                   final_norm=final_norm)
    m_small = dict(mix_norm=m_mix_norm, gdn_conv=m_gdn_conv, gdn_a_log=m_gdn_a_log, gdn_dt_bias=m_gdn_dt_bias, gdn_norm=m_gdn_norm,
                   xattn_norm=m_xattn_norm, mem_norm=m_mem_norm, ffn_norm=m_ffn_norm, ffn_conv=m_ffn_conv,
                   ffn_conv_bias=m_ffn_conv_bias, final_norm=m_final_norm)
    v_small = dict(mix_norm=v_mix_norm, gdn_conv=v_gdn_conv, gdn_a_log=v_gdn_a_log, gdn_dt_bias=v_gdn_dt_bias, gdn_norm=v_gdn_norm,
                   xattn_norm=v_xattn_norm, mem_norm=v_mem_norm, ffn_norm=v_ffn_norm, ffn_conv=v_ffn_conv,
                   ffn_conv_bias=v_ffn_conv_bias, final_norm=v_final_norm)
    shapes_small = [w_small[n].shape for n in names_small]
    packed = [_pack([d[n] for n in names_small], row_multiple=128)[None] for d in (w_small, grads_small, m_small, v_small)]
    upd_small = [_unpack(o[0], shapes_small) for o in _adamw(*packed, name="adamw_small")]
    delta, new_m, new_v = [dict(zip(names_small, u)) for u in upd_small]
    grads = dict(grads_small)
    big_names = ["w_in", "w_out", "w_xq", "w_xkv", "w_xo", "w_up", "w_down"]
    big_m = [m_w_in, m_w_out, m_w_xq, m_w_xkv, m_w_xo, m_w_up, m_w_down]
    big_v = [v_w_in, v_w_out, v_w_xq, v_w_xkv, v_w_xo, v_w_up, v_w_down]
    for n, w, go, gs, m, v in zip(big_names, big, sums, from_sib, big_m, big_v):
        grads[n], delta[n], new_m[n], new_v[n] = _adamw_halves(w, go, gs, cvec, m, v, name=f"adamw_{n}")

    order = ["mix_norm", "w_in", "gdn_conv", "gdn_a_log", "gdn_dt_bias", "gdn_norm", "w_out", "xattn_norm", "mem_norm", "w_xq",
             "w_xkv", "w_xo", "ffn_norm", "w_up", "ffn_conv", "ffn_conv_bias", "w_down", "final_norm"]
    return (loss, dxc[None], *[grads[n] for n in order], *[delta[n] for n in order], *[new_m[n] for n in order],
            *[new_v[n] for n in order])
```

```python
import functools
import math

import jax
import jax.numpy as jnp
from jax import lax
from jax.experimental import pallas as pl
from jax.experimental.pallas import tpu as pltpu

F32 = jnp.float32
BF16 = jnp.bfloat16

HEAD_DIM = 128
CHUNK = 64
GDN_CPB = 4
SB_TQ, SB_TK = 256, 512
SHORT_CONV = 4
FFN_CONV = 3
X_HEADS = 4
EPS = 1e-6
LANES = 128
VMEM_LIMIT = 56 * 2**20

ADAM_LR, ADAM_B1, ADAM_B2, ADAM_EPS, ADAM_WD, ADAM_STEP = 0.001, 0.9, 0.999, 1e-08, 0.01, 10

HI = lax.Precision.HIGH


def _params(sem):
    return pltpu.CompilerParams(dimension_semantics=sem, vmem_limit_bytes=VMEM_LIMIT)


def _tile(n, want):
    if n <= want:
        return n
    t = (want // LANES) * LANES
    while t > LANES and n % t:
        t -= LANES
    assert n % t == 0, (n, want)
    return t


def _sigmoid(x):
    return 1.0 / (1.0 + jnp.exp(-x))


def _softplus(x):
    return jnp.maximum(x, 0.0) + jnp.log(1.0 + jnp.exp(-jnp.abs(x)))


def _matmul(a, b, *, name, ta=False, tb=False, out_dtype=F32, res=None, tm=512, tn=512, tk=2048,
            dims=None, a_spec=None, b_spec=None, o_spec=None, out_shape=None):
    if dims is None:
        M, K = (a.shape[1], a.shape[0]) if ta else a.shape
        N = b.shape[0] if tb else b.shape[1]
    else:
        M, N, K = dims
    tm, tn, tk = _tile(M, tm), _tile(N, tn), _tile(K, tk)
    nk = K // tk
    dn = (((0 if ta else 1,), (1 if tb else 0,)), ((), ()))

    def body(*refs):
        a_ref, b_ref = refs[0], refs[1]
        r_ref = refs[2] if res is not None else None
        o_ref = refs[3] if res is not None else refs[2]
        p = lax.dot_general(a_ref[...].astype(BF16), b_ref[...].astype(BF16), dn, preferred_element_type=F32)

        def finish(acc):
            if r_ref is not None:
                acc = acc + r_ref[...].astype(F32)
            o_ref[...] = acc.astype(o_ref.dtype)

        if nk == 1:
            finish(p)
        else:
            acc_ref = refs[-1]
            k = pl.program_id(2)

            @pl.when(k == 0)
            def _():
                acc_ref[...] = p

            @pl.when(jnp.logical_and(k > 0, k < nk - 1))
            def _():
                acc_ref[...] += p

            @pl.when(k == nk - 1)
            def _():
                finish(acc_ref[...] + p)

    if a_spec is None:
        a_spec = pl.BlockSpec((tk, tm), lambda i, j, k: (k, i)) if ta else pl.BlockSpec((tm, tk), lambda i, j, k: (i, k))
    if b_spec is None:
        b_spec = pl.BlockSpec((tn, tk), lambda i, j, k: (j, k)) if tb else pl.BlockSpec((tk, tn), lambda i, j, k: (k, j))
    if o_spec is None:
        o_spec = pl.BlockSpec((tm, tn), lambda i, j, k: (i, j))
    if out_shape is None:
        out_shape = jax.ShapeDtypeStruct((M, N), out_dtype)
    in_specs, args = [a_spec, b_spec], [a, b]
    if res is not None:
        in_specs.append(pl.BlockSpec((tm, tn), lambda i, j, k: (i, j)))
        args.append(res)
    return pl.pallas_call(
        body, grid=(M // tm, N // tn, nk), in_specs=in_specs, out_specs=o_spec, out_shape=out_shape,
        scratch_shapes=[pltpu.VMEM((tm, tn), F32)] if nk > 1 else [],
        compiler_params=_params(("parallel", "parallel", "arbitrary")), name=name)(*args)


def _rms_fwd(x, g, *, name):
    R, D = x.shape
    tr = _tile(R, 256)

    def body(x_ref, g_ref, o_ref):
        xv = x_ref[...]
        rstd = lax.rsqrt(jnp.mean(xv * xv, axis=-1, keepdims=True) + EPS)
        o_ref[...] = (xv * rstd * g_ref[...]).astype(o_ref.dtype)

    return pl.pallas_call(
        body, grid=(R // tr,), in_specs=[pl.BlockSpec((tr, D), lambda i: (i, 0)), pl.BlockSpec((1, D), lambda i: (0, 0))],
        out_specs=pl.BlockSpec((tr, D), lambda i: (i, 0)), out_shape=jax.ShapeDtypeStruct((R, D), BF16),
        compiler_params=_params(("parallel",)), name=name)(x, g)


def _rms_bwd(x, g, dh, dres, *, name):
    R, D = x.shape
    tr = _tile(R, 256)

    def body(*refs):
        if dres is None:
            x_ref, g_ref, dh_ref, dx_ref, dg_ref = refs
        else:
            x_ref, g_ref, dh_ref, dr_ref, dx_ref, dg_ref = refs
        xv = x_ref[...]
        dhv = dh_ref[...].astype(F32)
        rstd = lax.rsqrt(jnp.mean(xv * xv, axis=-1, keepdims=True) + EPS)
        xhat = xv * rstd
        t = dhv * g_ref[...]
        dx = rstd * (t - xhat * jnp.mean(t * xhat, axis=-1, keepdims=True))
        if dres is not None:
            dx = dx + dr_ref[...]
        dx_ref[...] = dx
        part = jnp.sum(dhv * xhat, axis=0, keepdims=True)

        @pl.when(pl.program_id(0) == 0)
        def _():
            dg_ref[...] = part

        @pl.when(pl.program_id(0) > 0)
        def _():
            dg_ref[...] += part

    row = pl.BlockSpec((tr, D), lambda i: (i, 0))
    vec = pl.BlockSpec((1, D), lambda i: (0, 0))
    in_specs = [row, vec, row] + ([row] if dres is not None else [])
    args = [x, g, dh] + ([dres] if dres is not None else [])
    return pl.pallas_call(
        body, grid=(R // tr,), in_specs=in_specs, out_specs=[row, vec],
        out_shape=[jax.ShapeDtypeStruct((R, D), F32), jax.ShapeDtypeStruct((1, D), F32)],
        compiler_params=_params(("arbitrary",)), name=name)(*args)


def _loss_head(x, g, tgt, *, name):
    R, D = x.shape
    tr = _tile(R, 256)

    def body(x_ref, g_ref, t_ref, l_ref, dx_ref, dg_ref):
        xv = x_ref[...]
        rstd = lax.rsqrt(jnp.mean(xv * xv, axis=-1, keepdims=True) + EPS)
        xhat = xv * rstd
        err = xhat * g_ref[...] - t_ref[...]
        dy = err * (1.0 / D)
        t = dy * g_ref[...]
        dx_ref[...] = rstd * (t - xhat * jnp.mean(t * xhat, axis=-1, keepdims=True))
        part = jnp.sum(dy * xhat, axis=0, keepdims=True)
        lpart = jnp.zeros((8, LANES), F32) + 0.5 * jnp.sum(jnp.mean(err * err, axis=-1, keepdims=True))

        @pl.when(pl.program_id(0) == 0)
        def _():
            dg_ref[...] = part
            l_ref[...] = lpart

        @pl.when(pl.program_id(0) > 0)
        def _():
            dg_ref[...] += part
            l_ref[...] += lpart

    row = pl.BlockSpec((tr, D), lambda i: (i, 0))
    vec = pl.BlockSpec((1, D), lambda i: (0, 0))
    return pl.pallas_call(
        body, grid=(R // tr,), in_specs=[row, vec, row],
        out_specs=[pl.BlockSpec((8, LANES), lambda i: (0, 0)), row, vec],
        out_shape=[jax.ShapeDtypeStruct((8, LANES), F32), jax.ShapeDtypeStruct((R, D), F32), jax.ShapeDtypeStruct((1, D), F32)],
        compiler_params=_params(("arbitrary",)), name=name)(x, g, tgt)


def _shift_down(x, s):
    if s == 0:
        return x
    row = lax.broadcasted_iota(jnp.int32, x.shape, 0)
    return jnp.where(row >= s, pltpu.roll(x, s, 0), 0.0)


def _shift_up(x, s):
    if s == 0:
        return x
    n = x.shape[0]
    row = lax.broadcasted_iota(jnp.int32, x.shape, 0)
    return jnp.where(row < n - s, pltpu.roll(x, n - s, 0), 0.0)


def _dwconv(x, w):
    k = w.shape[0]
    acc = x * w[k - 1:k, :]
    for i in range(k - 1):
        acc = acc + _shift_down(x, k - 1 - i) * w[i:i + 1, :]
    return acc


def _dwconv_bwd(x, w, dc):
    k = w.shape[0]
    dx = dc * w[k - 1:k, :]
    dws = []
    for i in range(k - 1):
        s = k - 1 - i
        dx = dx + _shift_up(dc, s) * w[i:i + 1, :]
        dws.append(jnp.sum(dc * _shift_down(x, s), axis=0, keepdims=True))
    dws.append(jnp.sum(dc * x, axis=0, keepdims=True))
    return dx, jnp.concatenate(dws, axis=0)


def _ffn_act_fwd(u, cw, cb, *, name):
    S, F2 = u.shape
    F = F2 // 2
    tc = _tile(F, 256)
    nb = F // tc

    def body(ug_ref, uu_ref, wg_ref, wu_ref, bg_ref, bu_ref, o_ref):
        cg = _dwconv(ug_ref[...], wg_ref[...]) + bg_ref[...]
        cu = _dwconv(uu_ref[...], wu_ref[...]) + bu_ref[...]
        o_ref[...] = (cg * _sigmoid(cg) * cu).astype(o_ref.dtype)

    blk = lambda r, off: pl.BlockSpec((r, tc), lambda j: (0, j + off))
    return pl.pallas_call(
        body, grid=(nb,), in_specs=[blk(S, 0), blk(S, nb), blk(FFN_CONV, 0), blk(FFN_CONV, nb), blk(1, 0), blk(1, nb)],
        out_specs=blk(S, 0), out_shape=jax.ShapeDtypeStruct((S, F), BF16),
        compiler_params=_params(("parallel",)), name=name)(u, u, cw, cw, cb, cb)


def _ffn_act_bwd(u, cw, cb, dact, *, name):
    S, F2 = u.shape
    F = F2 // 2
    tc = _tile(F, 256)
    nb = F // tc

    def body(ug_ref, uu_ref, wg_ref, wu_ref, bg_ref, bu_ref, da_ref, du_ref, dw_ref, db_ref):
        ug, uu = ug_ref[...], uu_ref[...]
        cg = _dwconv(ug, wg_ref[...]) + bg_ref[...]
        cu = _dwconv(uu, wu_ref[...]) + bu_ref[...]
        sg = _sigmoid(cg)
        da = da_ref[...].astype(F32)
        dcu = da * (cg * sg)
        dcg = da * cu * (sg * (1.0 + cg * (1.0 - sg)))
        dxg, dwg = _dwconv_bwd(ug, wg_ref[...], dcg)
        dxu, dwu = _dwconv_bwd(uu, wu_ref[...], dcu)
        du_ref[0] = dxg.astype(du_ref.dtype)
        du_ref[1] = dxu.astype(du_ref.dtype)
        dw_ref[0] = dwg
        dw_ref[1] = dwu
        db_ref[0] = jnp.sum(dcg, axis=0, keepdims=True)
        db_ref[1] = jnp.sum(dcu, axis=0, keepdims=True)

    blk = lambda r, off: pl.BlockSpec((r, tc), lambda j: (0, j + off))
    blk3 = lambda r: pl.BlockSpec((2, r, tc), lambda j: (0, 0, j))
    return pl.pallas_call(
        body, grid=(nb,),
        in_specs=[blk(S, 0), blk(S, nb), blk(FFN_CONV, 0), blk(FFN_CONV, nb), blk(1, 0), blk(1, nb), blk(S, 0)],
        out_specs=[blk3(S), blk3(FFN_CONV), blk3(1)],
        out_shape=[jax.ShapeDtypeStruct((2, S, F), BF16), jax.ShapeDtypeStruct((2, FFN_CONV, F), F32),
                   jax.ShapeDtypeStruct((2, 1, F), F32)],
        compiler_params=_params(("parallel",)), name=name)(u, u, cw, cw, cb, cb, dact)


def _xattn_fwd(q, kv, *, name):
    S, XW = q.shape
    M = kv.shape[0]
    nh = XW // HEAD_DIM
    tq = _tile(S, 512)
    scale = HEAD_DIM ** -0.5

    def body(q_ref, k_ref, v_ref, o_ref):
        z = lax.dot_general(q_ref[...], k_ref[...], (((1,), (1,)), ((), ())), preferred_element_type=F32) * scale
        e = jnp.exp(z - jnp.max(z, axis=-1, keepdims=True))
        p = e / jnp.sum(e, axis=-1, keepdims=True)
        o_ref[...] = jnp.dot(p.astype(BF16), v_ref[...], preferred_element_type=F32).astype(o_ref.dtype)

    return pl.pallas_call(
        body, grid=(nh, S // tq),
        in_specs=[pl.BlockSpec((tq, HEAD_DIM), lambda h, i: (i, h)), pl.BlockSpec((M, HEAD_DIM), lambda h, i: (0, h)),
                  pl.BlockSpec((M, HEAD_DIM), lambda h, i: (0, nh + h))],
        out_specs=pl.BlockSpec((tq, HEAD_DIM), lambda h, i: (i, h)), out_shape=jax.ShapeDtypeStruct((S, XW), BF16),
        compiler_params=_params(("parallel", "parallel")), name=name)(q, kv, kv)


def _xattn_bwd(q, kv, do, *, name):
    S, XW = q.shape
    M = kv.shape[0]
    nh = XW // HEAD_DIM
    tq = _tile(S, 512)
    scale = HEAD_DIM ** -0.5
    nt = (((1,), (1,)), ((), ()))
    tn = (((0,), (0,)), ((), ()))

    def body(q_ref, k_ref, v_ref, do_ref, dq_ref, dk_ref, dv_ref):
        qv, kvv, vv = q_ref[...], k_ref[...], v_ref[...]
        dov = do_ref[...].astype(BF16)
        z = lax.dot_general(qv, kvv, nt, preferred_element_type=F32) * scale
        e = jnp.exp(z - jnp.max(z, axis=-1, keepdims=True))
        p = e / jnp.sum(e, axis=-1, keepdims=True)
        dp = lax.dot_general(dov, vv, nt, preferred_element_type=F32)
        ds = (p * (dp - jnp.sum(dp * p, axis=-1, keepdims=True)) * scale).astype(BF16)
        dq_ref[...] = jnp.dot(ds, kvv, preferred_element_type=F32).astype(dq_ref.dtype)
        dk = lax.dot_general(ds, qv, tn, preferred_element_type=F32)
        dv = lax.dot_general(p.astype(BF16), dov, tn, preferred_element_type=F32)

        @pl.when(pl.program_id(1) == 0)
        def _():
            dk_ref[...] = dk
            dv_ref[...] = dv

        @pl.when(pl.program_id(1) > 0)
        def _():
            dk_ref[...] += dk
            dv_ref[...] += dv

    qs = pl.BlockSpec((tq, HEAD_DIM), lambda h, i: (i, h))
    ms = pl.BlockSpec((M, HEAD_DIM), lambda h, i: (0, h))
    return pl.pallas_call(
        body, grid=(nh, S // tq),
        in_specs=[qs, ms, pl.BlockSpec((M, HEAD_DIM), lambda h, i: (0, nh + h)), qs],
        out_specs=[qs, ms, ms],
        out_shape=[jax.ShapeDtypeStruct((S, XW), BF16), jax.ShapeDtypeStruct((M, XW), F32), jax.ShapeDtypeStruct((M, XW), F32)],
        compiler_params=_params(("parallel", "arbitrary")), name=name)(q, kv, kv, do)


_NN = (((1,), (0,)), ((), ()))
_NT = (((1,), (1,)), ((), ()))
_TN = (((0,), (0,)), ((), ()))


def _batched(dn, a):
    if a.ndim == 2:
        return dn
    (ca,), (cb,) = dn[0]
    return (((ca + 1,), (cb + 1,)), ((0,), (0,)))


def _dot(a, b, dn=_NN):
    return lax.dot_general(a.astype(BF16), b.astype(BF16), _batched(dn, a), preferred_element_type=F32)


def _dot_hi(a, b, dn=_NN):
    return lax.dot_general(a, b, _batched(dn, a), preferred_element_type=F32, precision=HI)


def _dot_split(a, b01, dn=_NN):
    hi = a.astype(BF16)
    lo = (a - hi.astype(F32)).astype(BF16)
    return (lax.dot_general(hi, b01, dn, preferred_element_type=F32)
            + lax.dot_general(lo, b01, dn, preferred_element_type=F32))


def _after_matrix(n, transpose=False):
    row = lax.broadcasted_iota(jnp.int32, (n, n), 0)
    col = lax.broadcasted_iota(jnp.int32, (n, n), 1)
    return (row < col if transpose else row > col).astype(BF16)


def _sb_fwd(proj, nh, *, name):
    S = proj.shape[0]
    TQ, TK = min(SB_TQ, S), min(SB_TK, S)
    nq = S // TQ
    scale = HEAD_DIM ** -0.5

    def body(q_ref, k_ref, v_ref, o_ref, tot_ref):
        i = pl.program_id(1)
        q = q_ref[...].astype(BF16)
        qpos = i * TQ + lax.broadcasted_iota(jnp.int32, (TQ, TK), 0)
        kcol = lax.broadcasted_iota(jnp.int32, (TQ, TK), 1)
        after = _after_matrix(TK)
        nt = ((i + 1) * TQ + TK - 1) // TK

        def step(t, carry):
            acc, out = carry
            off = pl.multiple_of((nt - 1 - t) * TK, TK)
            kb = k_ref[pl.ds(off, TK), :].astype(BF16)
            vb = v_ref[pl.ds(off, TK), :].astype(BF16)
            z = lax.dot_general(q, kb, _NT, preferred_element_type=F32) * scale
            valid = kcol + off < qpos
            ls = jnp.where(valid, -_softplus(z), 0.0)
            later = _dot_split(ls, after) + acc
            w = jnp.where(valid, jnp.exp(ls + z + later), 0.0)
            out = out + jnp.dot(w.astype(BF16), vb, preferred_element_type=F32)
            return acc + jnp.sum(ls, axis=1, keepdims=True), out

        acc, out = lax.fori_loop(0, nt, step, (jnp.zeros((TQ, 1), F32), jnp.zeros((TQ, HEAD_DIM), F32)))
        o_ref[...] = out.astype(o_ref.dtype)
        tot_ref[...] = acc

    return pl.pallas_call(
        body, grid=(nh, nq),
        in_specs=[pl.BlockSpec((TQ, HEAD_DIM), lambda h, i: (i, h)),
                  pl.BlockSpec((S, HEAD_DIM), lambda h, i: (0, nh + h)),
                  pl.BlockSpec((S, HEAD_DIM), lambda h, i: (0, 2 * nh + h))],
        out_specs=[pl.BlockSpec((TQ, HEAD_DIM), lambda h, i: (i, h)), pl.BlockSpec((None, TQ, 1), lambda h, i: (h, i, 0))],
        out_shape=[jax.ShapeDtypeStruct((S, nh * HEAD_DIM), BF16), jax.ShapeDtypeStruct((nh, S, 1), F32)],
        compiler_params=_params(("parallel", "parallel")), name=name)(proj, proj, proj)


def _sb_bwd(proj, tot, dmix, nh, *, name):
    S = proj.shape[0]
    TQ, TK = min(SB_TQ, S), min(SB_TK, S)
    nq = S // TQ
    scale = HEAD_DIM ** -0.5

    def body(q_ref, k_ref, v_ref, tot_ref, do_ref, dq_ref, dk_ref, dv_ref, dk_acc, dv_acc):
        i = pl.program_id(1)

        @pl.when(i == 0)
        def _():
            dk_acc[...] = jnp.zeros_like(dk_acc)
            dv_acc[...] = jnp.zeros_like(dv_acc)

        q = q_ref[...].astype(BF16)
        do = do_ref[...].astype(BF16)
        tot = tot_ref[...]
        qpos = i * TQ + lax.broadcasted_iota(jnp.int32, (TQ, TK), 0)
        kcol = lax.broadcasted_iota(jnp.int32, (TQ, TK), 1)
        after = _after_matrix(TK)
        before = _after_matrix(TK, transpose=True)
        nt = ((i + 1) * TQ + TK - 1) // TK

        def step(j, carry):
            pre, g_sum, dq = carry
            off = pl.multiple_of(j * TK, TK)
            kb = k_ref[pl.ds(off, TK), :].astype(BF16)
            vb = v_ref[pl.ds(off, TK), :].astype(BF16)
            z = lax.dot_general(q, kb, _NT, preferred_element_type=F32) * scale
            valid = kcol + off < qpos
            ls = jnp.where(valid, -_softplus(z), 0.0)
            lb = ls + z
            rs = jnp.sum(ls, axis=1, keepdims=True)
            later = _dot_split(ls, after) + (tot - pre - rs)
            w = jnp.where(valid, jnp.exp(lb + later), 0.0)
            g = lax.dot_general(do, vb, _NT, preferred_element_type=F32) * w
            dls = _dot_split(g, before) + g_sum
            sig = jnp.exp(lb)
            dz = (jnp.where(valid, g * (1.0 - sig) - dls * sig, 0.0) * scale).astype(BF16)
            dq = dq + jnp.dot(dz, kb, preferred_element_type=F32)
            dk_acc[pl.ds(off, TK), :] += lax.dot_general(dz, q, _TN, preferred_element_type=F32)
            dv_acc[pl.ds(off, TK), :] += lax.dot_general(w.astype(BF16), do, _TN, preferred_element_type=F32)
            return pre + rs, g_sum + jnp.sum(g, axis=1, keepdims=True), dq

        zero = jnp.zeros((TQ, 1), F32)
        _, _, dq = lax.fori_loop(0, nt, step, (zero, zero, jnp.zeros((TQ, HEAD_DIM), F32)))
        dq_ref[...] = dq.astype(dq_ref.dtype)

        @pl.when(i == nq - 1)
        def _():
            dk_ref[...] = dk_acc[...].astype(dk_ref.dtype)
            dv_ref[...] = dv_acc[...].astype(dv_ref.dtype)

    qs = pl.BlockSpec((TQ, HEAD_DIM), lambda h, i: (i, h))
    full = pl.BlockSpec((S, HEAD_DIM), lambda h, i: (0, h))
    o = jax.ShapeDtypeStruct((S, nh * HEAD_DIM), BF16)
    return pl.pallas_call(
        body, grid=(nh, nq),
        in_specs=[qs, pl.BlockSpec((S, HEAD_DIM), lambda h, i: (0, nh + h)),
                  pl.BlockSpec((S, HEAD_DIM), lambda h, i: (0, 2 * nh + h)),
                  pl.BlockSpec((None, TQ, 1), lambda h, i: (h, i, 0)), qs],
        out_specs=[qs, full, full], out_shape=[o, o, o],
        scratch_shapes=[pltpu.VMEM((S, HEAD_DIM), F32), pltpu.VMEM((S, HEAD_DIM), F32)],
        compiler_params=_params(("parallel", "arbitrary")), name=name)(proj, proj, proj, tot, dmix)


def _gdn_qkv_fwd(proj, conv_w, nh, *, name):
    S = proj.shape[0]
    GW = nh * HEAD_DIM
    scale = HEAD_DIM ** -0.5

    def body(x_ref, w_ref, o_ref):
        sec = pl.program_id(0) // nh
        c = _dwconv(x_ref[...], w_ref[...])
        s = c * _sigmoid(c)
        r = lax.rsqrt(jnp.sum(s * s, axis=1, keepdims=True) + EPS)
        fac = jnp.where(sec == 0, scale, 1.0)
        o_ref[...] = jnp.where(sec == 2, s, s * (r * fac))

    return pl.pallas_call(
        body, grid=(3 * nh,),
        in_specs=[pl.BlockSpec((S, HEAD_DIM), lambda j: (0, 3 * nh + j)), pl.BlockSpec((SHORT_CONV, HEAD_DIM), lambda j: (0, j))],
        out_specs=pl.BlockSpec((None, S, HEAD_DIM), lambda j: (j // nh, 0, j % nh)),
        out_shape=jax.ShapeDtypeStruct((3, S, GW), F32),
        compiler_params=_params(("parallel",)), name=name)(proj, conv_w)


def _gdn_qkv_bwd(proj, conv_w, dqkv, nh, *, name):
    S = proj.shape[0]
    GW = nh * HEAD_DIM
    scale = HEAD_DIM ** -0.5

    def body(x_ref, w_ref, d_ref, dx_ref, dw_ref):
        sec = pl.program_id(0) // nh
        x, w = x_ref[...], w_ref[...]
        c = _dwconv(x, w)
        sg = _sigmoid(c)
        s = c * sg
        r = lax.rsqrt(jnp.sum(s * s, axis=1, keepdims=True) + EPS)
        sh = s * r
        d = d_ref[...]
        fac = jnp.where(sec == 0, scale, 1.0)
        dn = (r * fac) * (d - sh * jnp.sum(d * sh, axis=1, keepdims=True))
        ds = jnp.where(sec == 2, d, dn)
        dx, dw = _dwconv_bwd(x, w, ds * (sg * (1.0 + c * (1.0 - sg))))
        dx_ref[...] = dx.astype(dx_ref.dtype)
        dw_ref[...] = dw

    return pl.pallas_call(
        body, grid=(3 * nh,),
        in_specs=[pl.BlockSpec((S, HEAD_DIM), lambda j: (0, 3 * nh + j)), pl.BlockSpec((SHORT_CONV, HEAD_DIM), lambda j: (0, j)),
                  pl.BlockSpec((None, S, HEAD_DIM), lambda j: (j // nh, 0, j % nh))],
        out_specs=[pl.BlockSpec((S, HEAD_DIM), lambda j: (0, j)), pl.BlockSpec((SHORT_CONV, HEAD_DIM), lambda j: (0, j))],
        out_shape=[jax.ShapeDtypeStruct((S, 3 * GW), BF16), jax.ShapeDtypeStruct((SHORT_CONV, 3 * GW), F32)],
        compiler_params=_params(("parallel",)), name=name)(proj, conv_w, dqkv)


def _gdn_gates_fwd(proj, ab, nh, *, name):
    S = proj.shape[0]
    C = CHUNK

    def body(x_ref, ab_ref, o_ref):
        ri = lax.broadcasted_iota(jnp.int32, (C, C), 0)
        ci = lax.broadcasted_iota(jnp.int32, (C, C), 1)
        ltri = (ri >= ci).astype(F32)
        lane = lax.broadcasted_iota(jnp.int32, (C, LANES), 1)
        a_coef = -jnp.exp(ab_ref[0:1, :])
        dt = ab_ref[1:2, :]

        def chunk(n, _):
            rows = pl.ds(pl.multiple_of(n * C, C), C)
            x = x_ref[rows, :]
            beta = _sigmoid(x)
            g = jnp.where(jnp.logical_and(lane >= nh, lane < 2 * nh), a_coef * _softplus(x + dt), 0.0)
            gc = _dot_hi(ltri, pltpu.roll(g, nh, 1))
            o_ref[rows, :] = jnp.where(lane < nh, beta, g) + gc
            return 0

        lax.fori_loop(0, S // C, chunk, 0)

    return pl.pallas_call(
        body, grid=(1,),
        in_specs=[pl.BlockSpec((S, LANES), lambda i: (0, 7 * nh)), pl.BlockSpec((2, LANES), lambda i: (0, 0))],
        out_specs=pl.BlockSpec((S, LANES), lambda i: (0, 0)), out_shape=jax.ShapeDtypeStruct((S, LANES), F32),
        compiler_params=_params(("arbitrary",)), name=name)(proj, ab)


def _gdn_gates_bwd(proj, ab, dgt, nh, *, name):
    S = proj.shape[0]
    C = CHUNK

    def body(x_ref, ab_ref, d_ref, dx_ref, dab_ref):
        ri = lax.broadcasted_iota(jnp.int32, (C, C), 0)
        ci = lax.broadcasted_iota(jnp.int32, (C, C), 1)
        utri = (ri <= ci).astype(F32)
        lane = lax.broadcasted_iota(jnp.int32, (C, LANES), 1)
        is_b = lane < nh
        is_a = jnp.logical_and(lane >= nh, lane < 2 * nh)
        a_coef = -jnp.exp(ab_ref[0:1, :])
        dt = ab_ref[1:2, :]

        def chunk(n, carry):
            da_log, ddt = carry
            rows = pl.ds(pl.multiple_of(n * C, C), C)
            x = x_ref[rows, :]
            d = d_ref[rows, :]
            beta = _sigmoid(x)
            dg = pltpu.roll(_dot_hi(utri, jnp.where(lane >= 2 * nh, d, 0.0)), LANES - nh, 1)
            dg = jnp.where(is_a, dg, 0.0)
            dxa = dg * a_coef * _sigmoid(x + dt)
            dxb = jnp.where(is_b, d * beta * (1.0 - beta), 0.0)
            dx_ref[rows, :] = (dxa + dxb).astype(dx_ref.dtype)
            da_log = da_log + jnp.sum(dg * a_coef * _softplus(x + dt), axis=0, keepdims=True)
            return da_log, ddt + jnp.sum(dxa, axis=0, keepdims=True)

        zero = jnp.zeros((1, LANES), F32)
        da_log, ddt = lax.fori_loop(0, S // C, chunk, (zero, zero))
        dab_ref[0:1, :] = da_log
        dab_ref[1:2, :] = ddt

    return pl.pallas_call(
        body, grid=(1,),
        in_specs=[pl.BlockSpec((S, LANES), lambda i: (0, 7 * nh)), pl.BlockSpec((2, LANES), lambda i: (0, 0)),
                  pl.BlockSpec((S, LANES), lambda i: (0, 0))],
        out_specs=[pl.BlockSpec((S, LANES), lambda i: (0, 0)), pl.BlockSpec((2, LANES), lambda i: (0, 0))],
        out_shape=[jax.ShapeDtypeStruct((S, LANES), BF16), jax.ShapeDtypeStruct((2, LANES), F32)],
        compiler_params=_params(("arbitrary",)), name=name)(proj, ab, dgt)


def _unit_lower_inverse(lmat):
    C = lmat.shape[-1]
    ri = lax.broadcasted_iota(jnp.int32, lmat.shape, lmat.ndim - 2)
    ci = lax.broadcasted_iota(jnp.int32, lmat.shape, lmat.ndim - 1)
    nmat = -lmat
    p = jnp.where(ri == ci, 1.0, 0.0) + nmat
    for _ in range(int(math.log2(C)) - 1):
        nmat = _dot_hi(nmat, nmat)
        p = p + _dot_hi(p, nmat)
    return p


def _gdn_chunk_common(q, k, v, gates, gc_row, h, nh, tinv=None):
    C = CHUNK
    lane = lax.broadcasted_iota(jnp.int32, gates.shape, gates.ndim - 1)
    beta = jnp.sum(jnp.where(lane == h, gates, 0.0), axis=-1, keepdims=True)
    gc = jnp.sum(jnp.where(lane == 2 * nh + h, gates, 0.0), axis=-1, keepdims=True)
    sq = gates.shape[:-1] + (C,)
    ri = lax.broadcasted_iota(jnp.int32, sq, len(sq) - 2)
    ci = lax.broadcasted_iota(jnp.int32, sq, len(sq) - 1)
    incl, strict = ri >= ci, ri > ci
    decay = jnp.where(incl, jnp.exp(jnp.where(incl, gc - gc_row, 0.0)), 0.0)
    egc = jnp.exp(gc)
    kb, vb = k * beta, v * beta
    lmat = jnp.where(strict, _dot(kb, k, _NT) * decay, 0.0)
    kbg = kb * egc
    u = w = None
    if tinv is None:
        tinv = _unit_lower_inverse(lmat)
        u = _dot(tinv, vb)
        w = _dot(tinv, kbg)
    amat = _dot(q, k, _NT) * decay
    glast = gc[..., C - 1:C, :]
    ekt = jnp.exp(glast - gc)
    return dict(q=q, k=k, v=v, beta=beta, decay=decay, egc=egc, kb=kb, vb=vb, lmat=lmat, tinv=tinv, kbg=kbg, u=u, w=w,
                amat=amat, qd=q * egc, ekt=ekt, kt=k * ekt, cd=jnp.exp(glast), strict=strict, incl=incl)


def _gdn_chunk_specs(nh, S, nc):
    return [pl.BlockSpec((3, S, HEAD_DIM), lambda h: (0, 0, h)),
            pl.BlockSpec((S, LANES), lambda h: (0, 0)),
            pl.BlockSpec((None, nc, 1, CHUNK), lambda h: (h, 0, 0, 0))]


def _gdn_state_free(qkv_ref, gates_ref, gr_ref, g, nb, h, nh):
    C = CHUNK
    rows = pl.ds(pl.multiple_of(g * (nb * C), nb * C), nb * C)
    part = lambda x: x.reshape(nb, C, x.shape[-1])
    return rows, _gdn_chunk_common(part(qkv_ref[0, rows, :]), part(qkv_ref[1, rows, :]), part(qkv_ref[2, rows, :]),
                                   part(gates_ref[rows, :]), gr_ref[pl.ds(g * nb, nb)], h, nh)


def _gdn_chunk_fwd(qkv, gates, gc_row, *, name):
    _, S, GW = qkv.shape
    nh, C = GW // HEAD_DIM, CHUNK
    nc = S // C
    nb = min(GDN_CPB, nc)
    flat = lambda x: x.reshape(nb * C, x.shape[-1])

    def body(qkv_ref, gates_ref, gr_ref, o_ref, st_ref, u_s, w_s, a_s, qd_s, kt_s, cd_s):
        h = pl.program_id(0)

        def group(g, _):
            rows, m = _gdn_state_free(qkv_ref, gates_ref, gr_ref, g, nb, h, nh)
            u_s[rows, :] = flat(m["u"])
            w_s[rows, :] = flat(m["w"])
            a_s[rows, :] = flat(m["amat"])
            qd_s[rows, :] = flat(m["qd"])
            kt_s[rows, :] = flat(m["kt"])
            cd_s[pl.ds(g * nb, nb)] = jnp.broadcast_to(m["cd"], (nb, 8, LANES))
            return 0

        lax.fori_loop(0, nc // nb, group, 0)

        def chunk(n, s0):
            rows = pl.ds(pl.multiple_of(n * C, C), C)
            st_ref[n] = s0
            v_new = u_s[rows, :] - _dot(w_s[rows, :], s0)
            o_ref[rows, :] = _dot(qd_s[rows, :], s0) + _dot(a_s[rows, :], v_new)
            return s0 * cd_s[n][0:1, :] + _dot(kt_s[rows, :], v_new, _TN)

        lax.fori_loop(0, nc, chunk, jnp.zeros((HEAD_DIM, HEAD_DIM), F32))

    seq = pltpu.VMEM((S, HEAD_DIM), F32)
    return pl.pallas_call(
        body, grid=(nh,), in_specs=_gdn_chunk_specs(nh, S, nc),
        out_specs=[pl.BlockSpec((S, HEAD_DIM), lambda h: (0, h)),
                   pl.BlockSpec((None, nc, HEAD_DIM, HEAD_DIM), lambda h: (h, 0, 0, 0))],
        out_shape=[jax.ShapeDtypeStruct((S, GW), F32), jax.ShapeDtypeStruct((nh, nc, HEAD_DIM, HEAD_DIM), F32)],
        scratch_shapes=[seq, seq, pltpu.VMEM((S, C), F32), seq, seq, pltpu.VMEM((nc, 8, LANES), F32)],
        compiler_params=_params(("parallel",)), name=name)(qkv, gates, gc_row)


def _gdn_chunk_bwd(qkv, gates, gc_row, states, do, *, name):
    _, S, GW = qkv.shape
    nh, C = GW // HEAD_DIM, CHUNK
    nc = S // C
    nb = min(GDN_CPB, nc)
    flat = lambda x: x.reshape(nb * C, x.shape[-1])
    part = lambda x: x.reshape(nb, C, x.shape[-1])

    def body(qkv_ref, gates_ref, gr_ref, st_ref, do_ref, dqkv_ref, dgt_ref,
             t_s, vn_s, w_s, a_s, qd_s, kt_s, cd_s, dvn_s, dkt_s, dcd_s):
        h = pl.program_id(0)

        def group(g, _):
            rows, m = _gdn_state_free(qkv_ref, gates_ref, gr_ref, g, nb, h, nh)
            t_s[rows, :] = flat(m["tinv"])
            vn_s[rows, :] = flat(m["u"])
            w_s[rows, :] = flat(m["w"])
            a_s[rows, :] = flat(m["amat"])
            qd_s[rows, :] = flat(m["qd"])
            kt_s[rows, :] = flat(m["kt"])
            cd_s[pl.ds(g * nb, nb)] = jnp.broadcast_to(m["cd"], (nb, 8, LANES))
            return 0

        lax.fori_loop(0, nc // nb, group, 0)

        def chunk(t, dsn):
            n = nc - 1 - t
            rows = pl.ds(pl.multiple_of(n * C, C), C)
            s0, dout, w = st_ref[n], do_ref[rows, :], w_s[rows, :]
            v_new = vn_s[rows, :] - _dot(w, s0)
            dvn = _dot(a_s[rows, :], dout, _TN) + _dot(kt_s[rows, :], dsn)
            vn_s[rows, :] = v_new
            dvn_s[rows, :] = dvn
            dkt_s[rows, :] = _dot(v_new, dsn, _NT)
            dcd_s[n] = jnp.zeros((8, LANES), F32) + jnp.sum(dsn * s0)
            return _dot(qd_s[rows, :], dout, _TN) + dsn * cd_s[n][0:1, :] - _dot(w, dvn, _TN)

        lax.fori_loop(0, nc, chunk, jnp.zeros((HEAD_DIM, HEAD_DIM), F32))

        def rest(g, _):
            rows, m = _gdn_state_free_again(qkv_ref, gates_ref, gr_ref, t_s, g, nb, h, nh)
            chunks = pl.ds(g * nb, nb)
            dq, dk, dv, dgt = _gdn_chunk_grad(m, st_ref[chunks], part(vn_s[rows, :]), part(dvn_s[rows, :]),
                                              part(dkt_s[rows, :]), dcd_s[chunks][:, 0:1, 0:1], part(do_ref[rows, :]), h, nh)
            dqkv_ref[0, rows, :] = flat(dq)
            dqkv_ref[1, rows, :] = flat(dk)
            dqkv_ref[2, rows, :] = flat(dv)
            dgt_ref[rows, :] = flat(dgt)
            return 0

        lax.fori_loop(0, nc // nb, rest, 0)

    seq = pltpu.VMEM((S, HEAD_DIM), F32)
    small = pltpu.VMEM((nc, 8, LANES), F32)
    return pl.pallas_call(
        body, grid=(nh,),
        in_specs=_gdn_chunk_specs(nh, S, nc) + [
            pl.BlockSpec((None, nc, HEAD_DIM, HEAD_DIM), lambda h: (h, 0, 0, 0)),
            pl.BlockSpec((S, HEAD_DIM), lambda h: (0, h))],
        out_specs=[pl.BlockSpec((3, S, HEAD_DIM), lambda h: (0, 0, h)), pl.BlockSpec((None, S, LANES), lambda h: (h, 0, 0))],
        out_shape=[jax.ShapeDtypeStruct((3, S, GW), F32), jax.ShapeDtypeStruct((nh, S, LANES), F32)],
        scratch_shapes=[pltpu.VMEM((S, C), F32), seq, seq, pltpu.VMEM((S, C), F32), seq, seq, small, seq, seq, small],
        compiler_params=_params(("parallel",)), name=name)(qkv, gates, gc_row, states, do)


def _gdn_state_free_again(qkv_ref, gates_ref, gr_ref, t_s, g, nb, h, nh):
    C = CHUNK
    rows = pl.ds(pl.multiple_of(g * (nb * C), nb * C), nb * C)
    part = lambda x: x.reshape(nb, C, x.shape[-1])
    m = _gdn_chunk_common(part(qkv_ref[0, rows, :]), part(qkv_ref[1, rows, :]), part(qkv_ref[2, rows, :]),
                          part(gates_ref[rows, :]), gr_ref[pl.ds(g * nb, nb)], h, nh, tinv=part(t_s[rows, :]))
    return rows, m


def _gdn_chunk_grad(m, s0, v_new, dvn, dkt, dcd, dout, h, nh):
    C = CHUNK
    q, k, v, beta, decay, egc = m["q"], m["k"], m["v"], m["beta"], m["decay"], m["egc"]
    tinv, kt, cd = m["tinv"], m["kt"], m["cd"]
    dqd = _dot(dout, s0, _NT)
    damat = jnp.where(m["incl"], _dot(dout, v_new, _NT), 0.0)
    dw = -_dot(dvn, s0, _NT)
    dvb = _dot(tinv, dvn, _TN)
    dkbg = _dot(tinv, dw, _TN)
    dtinv = _dot(dvn, m["vb"], _NT) + _dot(dw, m["kbg"], _NT)
    dl = jnp.where(m["strict"], -_dot_hi(_dot_hi(tinv, dtinv, _TN), tinv, _NT), 0.0)
    dkk = dl * decay
    dqk = damat * decay
    dkb = _dot(dkk, k) + dkbg * egc
    dk = _dot(dkk, m["kb"], _TN) + _dot(dqk, q, _TN) + dkt * m["ekt"] + dkb * beta
    dq = _dot(dqk, k) + dqd * egc
    mm = dl * m["lmat"] + damat * m["amat"]
    ones = jnp.ones(q.shape, F32)
    rk = jnp.sum(dkt * kt, axis=-1, keepdims=True)
    dgc = (_dot_hi(mm, ones) - _dot_hi(mm, ones, _TN) + jnp.sum(dqd * m["qd"], axis=-1, keepdims=True) - rk
           + jnp.sum(dkbg * m["kbg"], axis=-1, keepdims=True))
    dglast = jnp.sum(rk, axis=-2, keepdims=True) + dcd * cd
    rowi = lax.broadcasted_iota(jnp.int32, q.shape, q.ndim - 2)
    lane = lax.broadcasted_iota(jnp.int32, q.shape, q.ndim - 1)
    dgc = dgc + jnp.where(rowi == C - 1, dglast, 0.0)
    dbeta = jnp.sum(dkb * k, axis=-1, keepdims=True) + jnp.sum(dvb * v, axis=-1, keepdims=True)
    dgt = jnp.where(lane == h, dbeta, 0.0) + jnp.where(lane == 2 * nh + h, dgc, 0.0)
    return dq, dk, dvb * beta, dgt


def _gdn_post_fwd(o, proj, ng, nh, *, name):
    S, GW = o.shape

    def body(o_ref, z_ref, g_ref, y_ref):
        ov, z = o_ref[...], z_ref[...]
        rstd = lax.rsqrt(jnp.mean(ov * ov, axis=-1, keepdims=True) + EPS)
        y_ref[...] = (ov * rstd * g_ref[...] * (z * _sigmoid(z))).astype(y_ref.dtype)

    blk = pl.BlockSpec((S, HEAD_DIM), lambda h: (0, h))
    return pl.pallas_call(
        body, grid=(nh,), in_specs=[blk, pl.BlockSpec((S, HEAD_DIM), lambda h: (0, 6 * nh + h)), pl.BlockSpec((1, HEAD_DIM), lambda h: (0, 0))],
        out_specs=blk, out_shape=jax.ShapeDtypeStruct((S, GW), BF16),
        compiler_params=_params(("parallel",)), name=name)(o, proj, ng)


def _gdn_post_bwd(o, proj, ng, dmix, nh, *, name):
    S, GW = o.shape

    def body(o_ref, z_ref, g_ref, d_ref, do_ref, dz_ref, dg_ref):
        ov, z, d = o_ref[...], z_ref[...], d_ref[...].astype(F32)
        rstd = lax.rsqrt(jnp.mean(ov * ov, axis=-1, keepdims=True) + EPS)
        oh = ov * rstd
        sz = _sigmoid(z)
        dy = d * (z * sz)
        dz_ref[...] = (d * (oh * g_ref[...]) * (sz * (1.0 + z * (1.0 - sz)))).astype(dz_ref.dtype)
        t = dy * g_ref[...]
        do_ref[...] = rstd * (t - oh * jnp.mean(t * oh, axis=-1, keepdims=True))
        part = jnp.sum(dy * oh, axis=0, keepdims=True)

        @pl.when(pl.program_id(0) == 0)
        def _():
            dg_ref[...] = part

        @pl.when(pl.program_id(0) > 0)
        def _():
            dg_ref[...] += part

    blk = pl.BlockSpec((S, HEAD_DIM), lambda h: (0, h))
    vec = pl.BlockSpec((1, HEAD_DIM), lambda h: (0, 0))
    return pl.pallas_call(
        body, grid=(nh,),
        in_specs=[blk, pl.BlockSpec((S, HEAD_DIM), lambda h: (0, 6 * nh + h)), vec, pl.BlockSpec((S, HEAD_DIM), lambda h: (0, nh + h))],
        out_specs=[blk, blk, vec],
        out_shape=[jax.ShapeDtypeStruct((S, GW), F32), jax.ShapeDtypeStruct((S, GW), BF16), jax.ShapeDtypeStruct((1, HEAD_DIM), F32)],
        compiler_params=_params(("arbitrary",)), name=name)(o, proj, ng, dmix)


def _gdn_forward(proj, conv_w, ab, ng, nh, tag):
    S = proj.shape[0]
    nc = S // CHUNK
    qkv = _gdn_qkv_fwd(proj, conv_w, nh, name=f"gdn_qkv_fwd{tag}")
    gates = _gdn_gates_fwd(proj, ab, nh, name=f"gdn_gates_fwd{tag}")
    gc_row = gates[:, 2 * nh:3 * nh].T.reshape(nh, nc, 1, CHUNK)
    o, states = _gdn_chunk_fwd(qkv, gates, gc_row, name=f"gdn_chunk_fwd{tag}")
    y = _gdn_post_fwd(o, proj, ng, nh, name=f"gdn_post_fwd{tag}")
    return y, (qkv, gates, gc_row, states, o)


def _gdn_backward(proj, conv_w, ab, ng, saved, dmix, nh, tag):
    qkv, gates, gc_row, states, o = saved
    do, dz, dng = _gdn_post_bwd(o, proj, ng, dmix, nh, name=f"gdn_post_bwd{tag}")
    dqkv, dgt_heads = _gdn_chunk_bwd(qkv, gates, gc_row, states, do, name=f"gdn_chunk_bwd{tag}")
    dx_qkv, dconv = _gdn_qkv_bwd(proj, conv_w, dqkv, nh, name=f"gdn_qkv_bwd{tag}")
    dx_g, dab = _gdn_gates_bwd(proj, ab, jnp.sum(dgt_heads, axis=0), nh, name=f"gdn_gates_bwd{tag}")
    return dx_qkv, dz, dx_g, dconv, dab, dng


def _row_tile(r, cap=128):
    t = cap
    while r % t:
        t //= 2
    assert t >= 8, r
    return t


def _adamw_update(gv, w_ref, m_ref, v_ref, d_ref, m2_ref, v2_ref):
    m2 = ADAM_B1 * m_ref[...] + (1.0 - ADAM_B1) * gv
    v2 = ADAM_B2 * v_ref[...] + (1.0 - ADAM_B2) * (gv * gv)
    m_hat = m2 / (1.0 - ADAM_B1 ** ADAM_STEP)
    v_hat = v2 / (1.0 - ADAM_B2 ** ADAM_STEP)
    d_ref[...] = -ADAM_LR * (m_hat / (jnp.sqrt(v_hat) + ADAM_EPS) + ADAM_WD * w_ref[...])
    m2_ref[...] = m2
    v2_ref[...] = v2


def _adamw(w, g, m, v, *, name):
    L, r, c = w.shape
    tr = _row_tile(r)

    def body(w_ref, g_ref, m_ref, v_ref, d_ref, m2_ref, v2_ref):
        _adamw_update(g_ref[...], w_ref, m_ref, v_ref, d_ref, m2_ref, v2_ref)

    blk = pl.BlockSpec((None, tr, c), lambda l, i: (l, i, 0))
    o = jax.ShapeDtypeStruct(w.shape, F32)
    return pl.pallas_call(
        body, grid=(L, r // tr), in_specs=[blk] * 4, out_specs=[blk] * 3, out_shape=[o, o, o],
        compiler_params=_params(("parallel", "parallel")), name=name)(w, g, m, v)


def _adamw_halves(w, g_own, g_sib, cvec, m, v, *, name):
    L, r, c = w.shape
    tr = _row_tile(r // 2)
    nbh = (r // 2) // tr

    def body(c_ref, w_ref, go_ref, gs_ref, m_ref, v_ref, g_out, d_ref, m2_ref, v2_ref):
        gv = jnp.where(pl.program_id(1) // nbh == c_ref[0], go_ref[...], gs_ref[...])
        g_out[...] = gv
        _adamw_update(gv, w_ref, m_ref, v_ref, d_ref, m2_ref, v2_ref)

    lo = lambda i: jnp.minimum(i, nbh - 1)
    hi = lambda i: jnp.maximum(i - nbh, 0)
    blk = pl.BlockSpec((None, tr, c), lambda l, i, c_ref: (l, i, 0))
    own = pl.BlockSpec((None, tr, c), lambda l, i, c_ref: (l, jnp.where(c_ref[0] == 0, lo(i), hi(i)), 0))
    sib = pl.BlockSpec((None, tr, c), lambda l, i, c_ref: (l, jnp.where(c_ref[0] == 0, hi(i), lo(i)), 0))
    o = jax.ShapeDtypeStruct(w.shape, F32)
    return pl.pallas_call(
        body,
        grid_spec=pltpu.PrefetchScalarGridSpec(
            num_scalar_prefetch=1, grid=(L, r // tr), in_specs=[blk, own, sib, blk, blk], out_specs=[blk] * 4),
        out_shape=[o, o, o, o],
        compiler_params=_params(("parallel", "arbitrary")), name=name)(cvec, w, g_own, g_sib, m, v)


def _sum_half(g, rbuf, cvec, *, name):
    _, r, c = g.shape
    h = r // 2
    tr = _row_tile(h)
    nb = h // tr

    def body(c_ref, g_ref, r_ref, o_ref):
        o_ref[...] = (g_ref[...] + r_ref[...]).astype(o_ref.dtype)

    blk = pl.BlockSpec((None, tr, c), lambda s, i, c_ref: (s, i, 0))
    return pl.pallas_call(
        body,
        grid_spec=pltpu.PrefetchScalarGridSpec(
            num_scalar_prefetch=1, grid=(4, nb),
            in_specs=[pl.BlockSpec((None, tr, c), lambda s, i, c_ref: (s, c_ref[0] * nb + i, 0)), blk], out_specs=blk),
        out_shape=jax.ShapeDtypeStruct((4, h, c), BF16),
        compiler_params=_params(("parallel", "parallel")), name=name)(cvec, g, rbuf)


def _sum_chips(p, rb, kvec, prev, l, nl, *, name):
    _, h, c = rb.shape
    tr = _row_tile(h)

    def body(k_ref, p_ref, r1, r2, r3, *rest):
        o_ref = rest[-1]
        o_ref[...] = ((p_ref[...].astype(F32) + r1[...].astype(F32)) + r2[...].astype(F32)) + r3[...].astype(F32)

    slot = lambda s: pl.BlockSpec((None, tr, c), lambda i, k_ref: ((k_ref[0] + s) % 4, i, 0))
    in_specs = [slot(0), slot(1), slot(2), slot(3)]
    args = [kvec, p, rb, rb, rb]
    if prev is not None:
        in_specs.append(pl.BlockSpec(memory_space=pltpu.HBM))
        args.append(prev)
    return pl.pallas_call(
        body,
        grid_spec=pltpu.PrefetchScalarGridSpec(
            num_scalar_prefetch=1, grid=(h // tr,), in_specs=in_specs,
            out_specs=pl.BlockSpec((None, tr, c), lambda i, k_ref: (l, i, 0))),
        out_shape=jax.ShapeDtypeStruct((nl, h, c), F32), input_output_aliases={5: 0} if prev is not None else {},
        compiler_params=_params(("parallel",)), name=name)(*args)


_MESH = pl.DeviceIdType.MESH
_HBM = pl.BlockSpec(memory_space=pltpu.HBM)


def _place():
    x, y, c = lax.axis_index("x"), lax.axis_index("y"), lax.axis_index("c")
    return x, y, c, [(1 - x, y), (x, 1 - y), (1 - x, 1 - y)]


def _cast_place(w, l, kvec, *, name):
    _, r, c = w.shape
    tr = _row_tile(r, 256)

    def body(k_ref, w_ref, o_ref):
        o_ref[...] = w_ref[...].astype(o_ref.dtype)

    return pl.pallas_call(
        body,
        grid_spec=pltpu.PrefetchScalarGridSpec(
            num_scalar_prefetch=1, grid=(r // tr,),
            in_specs=[pl.BlockSpec((None, tr, c), lambda i, k_ref: (l, i, 0))],
            out_specs=pl.BlockSpec((None, tr, c), lambda i, k_ref: (k_ref[0], i, 0))),
        out_shape=jax.ShapeDtypeStruct((4, r, c), BF16),
        compiler_params=_params(("parallel",)), name=name)(kvec, w)


_SEM = pl.BlockSpec(memory_space=pltpu.SEMAPHORE)
_ANY = pl.BlockSpec(memory_space=pl.ANY)
_EFFECT = pltpu.SideEffectType.DATAFLOW_SIDE_EFFECTING


def _in_hbm(a):
    return pltpu.with_memory_space_constraint(a, pltpu.HBM)


def _gather_copies(w_refs, send, recv, landing):
    x, y, c, chips = _place()
    k = 2 * x + y
    cps = []
    for a, w in enumerate(w_refs):
        h = w.shape[1] // 2
        for j, (cx, cy) in enumerate(chips):
            cps.append(pltpu.make_async_remote_copy(
                src_ref=w.at[k, pl.ds(c * h, h), :], dst_ref=w.at[2 * cx + cy if landing else k, pl.ds(c * h, h), :],
                send_sem=send.at[3 * a + j], recv_sem=recv.at[3 * a + j], device_id=(cx, cy, c), device_id_type=_MESH))
    return cps


def _gather_start(ws, after, *, name):
    n = len(ws)

    def body(*refs):
        send, recv = refs[n + 1], refs[n + 2]
        o_refs, token = refs[n + 3:2 * n + 3], refs[2 * n + 3]
        for cp in _gather_copies(o_refs, send, recv, landing=False):
            cp.start()
        token[...] = jnp.zeros_like(token)

    out = pl.pallas_call(
        body, in_specs=[_HBM] * n + [_ANY], out_specs=[_SEM, _SEM] + [_HBM] * n + [pl.BlockSpec(memory_space=pltpu.VMEM)],
        out_shape=[pltpu.SemaphoreType.DMA((3 * n,)), pltpu.SemaphoreType.DMA((3 * n,))]
        + [pltpu.HBM(w.shape, w.dtype) for w in ws] + [jax.ShapeDtypeStruct((8, LANES), F32)],
        input_output_aliases={a: 2 + a for a in range(n)},
        compiler_params=pltpu.CompilerParams(has_side_effects=_EFFECT), name=name)(*[_in_hbm(w) for w in ws], after)
    return out[0], out[1], list(out[2:2 + n]), out[2 + n]


def _gather_wait(send, recv, ws, after, *, name):
    n = len(ws)

    def body(*refs):
        for cp in _gather_copies(refs[:n], refs[n], refs[n + 1], landing=True):
            cp.wait_send()
            cp.wait_recv()

    return list(pl.pallas_call(
        body, in_specs=[_HBM] * n + [_SEM, _SEM, _ANY], out_specs=[_HBM] * n,
        out_shape=[pltpu.HBM(w.shape, w.dtype) for w in ws], input_output_aliases={a: a for a in range(n)},
        compiler_params=pltpu.CompilerParams(has_side_effects=_EFFECT), name=name)(*ws, send, recv, after))


def _gather_to_sibling(ws, *, name):
    n = len(ws)

    def body(*refs):
        o_refs = refs[n:2 * n]
        send, recv = refs[2 * n:]
        x, y, c, chips = _place()
        cps = []
        for a in range(n):
            h = o_refs[a].shape[1] // 2
            for j, (cx, cy) in enumerate(chips):
                landed = o_refs[a].at[2 * cx + cy, pl.ds(c * h, h), :]
                cp = pltpu.make_async_remote_copy(
                    src_ref=landed, dst_ref=landed, send_sem=send.at[3 * a + j], recv_sem=recv.at[3 * a + j],
                    device_id=(x, y, 1 - c), device_id_type=_MESH)
                cp.start()
                cps.append(cp)
        for a in range(n):
            h = o_refs[a].shape[1] // 2
            for j, (cx, cy) in enumerate(chips):
                other = o_refs[a].at[2 * cx + cy, pl.ds((1 - c) * h, h), :]
                pltpu.make_async_remote_copy(
                    src_ref=other, dst_ref=other, send_sem=send.at[3 * a + j], recv_sem=recv.at[3 * a + j],
                    device_id=(x, y, c), device_id_type=_MESH).wait_recv()
        for cp in cps:
            cp.wait_send()

    return list(pl.pallas_call(
        body, in_specs=[_HBM] * n, out_specs=[_HBM] * n,
        out_shape=[jax.ShapeDtypeStruct(w.shape, w.dtype) for w in ws],
        input_output_aliases={a: a for a in range(n)},
        scratch_shapes=[pltpu.SemaphoreType.DMA((3 * n,))] * 2, name=name)(*ws))


def _exchange_sibling(gs, *, name):
    n = len(gs)

    def body(*refs):
        g_refs, o_refs = refs[:n], refs[n:2 * n]
        send, recv = refs[2 * n:]
        x, y, c, _ = _place()
        cps = []
        for a in range(n):
            h = g_refs[a].shape[1] // 2
            cp = pltpu.make_async_remote_copy(
                src_ref=g_refs[a].at[:, pl.ds((1 - c) * h, h), :], dst_ref=o_refs[a], send_sem=send.at[a], recv_sem=recv.at[a],
                device_id=(x, y, 1 - c), device_id_type=_MESH)
            cp.start()
            cps.append(cp)
        for cp in cps:
            cp.wait_recv()
        for cp in cps:
            cp.wait_send()

    return pl.pallas_call(
        body, in_specs=[_HBM] * n, out_specs=[_HBM] * n,
        out_shape=[jax.ShapeDtypeStruct((4, g.shape[1] // 2, g.shape[2]), g.dtype) for g in gs],
        scratch_shapes=[pltpu.SemaphoreType.DMA((n,))] * 2, name=name)(*gs)


def _reduce_copies(p_refs, r_refs, send, recv, landing):
    x, y, c, chips = _place()
    k = 2 * x + y
    cps = []
    for a, (p, r) in enumerate(zip(p_refs, r_refs)):
        for j, (cx, cy) in enumerate(chips):
            cps.append(pltpu.make_async_remote_copy(
                src_ref=p.at[2 * cx + cy], dst_ref=r.at[2 * cx + cy if landing else k], send_sem=send.at[3 * a + j],
                recv_sem=recv.at[3 * a + j], device_id=(cx, cy, c), device_id_type=_MESH))
    return cps


def _reduce_start(ps, after, *, name):
    n = len(ps)
    lands = [lax.empty(p.shape, p.dtype) for p in ps]

    def body(*refs):
        send, recv = refs[2 * n + 1], refs[2 * n + 2]
        p_out, r_out, token = refs[2 * n + 3:3 * n + 3], refs[3 * n + 3:4 * n + 3], refs[4 * n + 3]
        for cp in _reduce_copies(p_out, r_out, send, recv, landing=False):
            cp.start()
        token[...] = jnp.zeros_like(token)

    out = pl.pallas_call(
        body, in_specs=[_HBM] * (2 * n) + [_ANY],
        out_specs=[_SEM, _SEM] + [_HBM] * (2 * n) + [pl.BlockSpec(memory_space=pltpu.VMEM)],
        out_shape=[pltpu.SemaphoreType.DMA((3 * n,)), pltpu.SemaphoreType.DMA((3 * n,))]
        + [pltpu.HBM(p.shape, p.dtype) for p in ps] * 2 + [jax.ShapeDtypeStruct((8, LANES), F32)],
        input_output_aliases={a: 2 + a for a in range(2 * n)},
        compiler_params=pltpu.CompilerParams(has_side_effects=_EFFECT), name=name)(
            *[_in_hbm(p) for p in ps], *[_in_hbm(r) for r in lands], after)
    return out[0], out[1], list(out[2:2 + n]), list(out[2 + n:2 + 2 * n]), out[2 + 2 * n]


def _reduce_wait(send, recv, ps, lands, after, *, name):
    n = len(ps)

    def body(*refs):
        for cp in _reduce_copies(refs[:n], refs[n:2 * n], refs[2 * n], refs[2 * n + 1], landing=True):
            cp.wait_send()
            cp.wait_recv()

    out = pl.pallas_call(
        body, in_specs=[_HBM] * (2 * n) + [_SEM, _SEM, _ANY], out_specs=[_HBM] * (2 * n),
        out_shape=[pltpu.HBM(p.shape, p.dtype) for p in ps] * 2, input_output_aliases={a: a for a in range(2 * n)},
        compiler_params=pltpu.CompilerParams(has_side_effects=_EFFECT), name=name)(*ps, *lands, send, recv, after)
    return list(out[:n]), list(out[n:])


def _swap_with_sibling(gs, *, name):
    n = len(gs)

    def body(*refs):
        g_refs, o_refs = refs[:n], refs[n:2 * n]
        send, recv = refs[2 * n:]
        x, y, c, _ = _place()
        cps = []
        for a in range(n):
            cp = pltpu.make_async_remote_copy(
                src_ref=g_refs[a], dst_ref=o_refs[a], send_sem=send.at[a], recv_sem=recv.at[a],
                device_id=(x, y, 1 - c), device_id_type=_MESH)
            cp.start()
            cps.append(cp)
        for cp in cps:
            cp.wait_recv()
        for cp in cps:
            cp.wait_send()

    return pl.pallas_call(
        body, in_specs=[_HBM] * n, out_specs=[_HBM] * n,
        out_shape=[jax.ShapeDtypeStruct(g.shape, g.dtype) for g in gs],
        scratch_shapes=[pltpu.SemaphoreType.DMA((n,))] * 2, name=name)(*gs)


def _allreduce_small(v, *, name):
    R = v.shape[0]

    def body(v_ref, o_ref, buf, send, recv, loc):
        x, y, c = lax.axis_index("x"), lax.axis_index("y"), lax.axis_index("c")
        me = 4 * x + 2 * y + c
        mine = pltpu.make_async_copy(v_ref, buf.at[me], loc)
        mine.start()
        cps = []
        for d in range(1, 8):
            px = 1 - x if d & 4 else x
            py = 1 - y if d & 2 else y
            pc = 1 - c if d & 1 else c
            cp = pltpu.make_async_remote_copy(
                src_ref=v_ref, dst_ref=buf.at[me], send_sem=send.at[d - 1], recv_sem=recv.at[d - 1],
                device_id=(px, py, pc), device_id_type=_MESH)
            cp.start()
            cps.append((cp, 4 * px + 2 * py + pc))
        for d in range(1, 8):
            cp, peer = cps[d - 1]
            pltpu.make_async_remote_copy(
                src_ref=buf.at[peer], dst_ref=buf.at[peer], send_sem=send.at[d - 1], recv_sem=recv.at[d - 1],
                device_id=(x, y, c), device_id_type=_MESH).wait_recv()
        for cp, _ in cps:
            cp.wait_send()
        mine.wait()
        acc = buf[0]
        for i in range(1, 8):
            acc = acc + buf[i]
        o_ref[...] = acc

    return pl.pallas_call(
        body, in_specs=[pl.BlockSpec(memory_space=pltpu.VMEM)], out_specs=pl.BlockSpec(memory_space=pltpu.VMEM),
        out_shape=jax.ShapeDtypeStruct((R, LANES), F32),
        scratch_shapes=[pltpu.VMEM((8, R, LANES), F32), pltpu.SemaphoreType.DMA((7,)), pltpu.SemaphoreType.DMA((7,)),
                        pltpu.SemaphoreType.DMA],
        compiler_params=pltpu.CompilerParams(vmem_limit_bytes=VMEM_LIMIT), name=name)(v)


def _pack(arrs, row_multiple=8):
    rows = []
    for a in arrs:
        flat = a.reshape(-1)
        flat = jnp.pad(flat, (0, (-flat.shape[0]) % LANES))
        rows.append(flat.reshape(-1, LANES))
    buf = jnp.concatenate(rows, axis=0)
    return jnp.pad(buf, ((0, (-buf.shape[0]) % row_multiple), (0, 0)))


def _unpack(buf, shapes):
    out, r = [], 0
    for s in shapes:
        size = math.prod(s)
        nr = -(-size // LANES)
        out.append(buf[r:r + nr].reshape(-1)[:size].reshape(s))
        r += nr
    return out


def kernel(x, mem, mix_norm, w_in, gdn_conv, gdn_a_log, gdn_dt_bias, gdn_norm, w_out, xattn_norm, mem_norm, w_xq, w_xkv, w_xo, ffn_norm, w_up, ffn_conv, ffn_conv_bias, w_down, final_norm, loss_target, m_mix_norm, m_w_in, m_gdn_conv, m_gdn_a_log, m_gdn_dt_bias, m_gdn_norm, m_w_out, m_xattn_norm, m_mem_norm, m_w_xq, m_w_xkv, m_w_xo, m_ffn_norm, m_w_up, m_ffn_conv, m_ffn_conv_bias, m_w_down, m_final_norm, v_mix_norm, v_w_in, v_gdn_conv, v_gdn_a_log, v_gdn_dt_bias, v_gdn_norm, v_w_out, v_xattn_norm, v_mem_norm, v_w_xq, v_w_xkv, v_w_xo, v_ffn_norm, v_w_up, v_ffn_conv, v_ffn_conv_bias, v_w_down, v_final_norm):
    L = w_in.shape[0]
    _, S, D = x.shape
    nh = D // (2 * HEAD_DIM)
    GW = nh * HEAD_DIM
    n_in = 7 * GW + 2 * nh
    NP = 7 * GW + LANES
    XW = X_HEADS * HEAD_DIM
    F = w_down.shape[1] * 4
    cs_in = w_in.shape[2]
    cs_up = w_up.shape[2]
    cs_xo = w_xo.shape[2]
    tu = _tile(cs_up, 1408)
    per = cs_up // tu
    fper = F // tu
    assert n_in == 4 * cs_in and F % tu == 0 and 2 * F == 4 * cs_up

    xi, yi, ci = lax.axis_index("x"), lax.axis_index("y"), lax.axis_index("c")
    chip = 2 * xi + yi
    cvec = jnp.reshape(ci, (1,)).astype(jnp.int32)

    cs_gc, cs_fc = gdn_conv.shape[2], ffn_conv.shape[2]
    keep = jnp.where(ci == 0, 1.0, 0.0).astype(F32)
    gc_full = lax.dynamic_update_slice(jnp.zeros((L, SHORT_CONV, 4 * cs_gc), F32), gdn_conv * keep, (0, 0, chip * cs_gc))
    fc_full = lax.dynamic_update_slice(jnp.zeros((L, FFN_CONV, 4 * cs_fc), F32), ffn_conv * keep, (0, 0, chip * cs_fc))
    conv_all = _allreduce_small(_pack([gc_full, fc_full]), name="allgather_conv")
    gdn_conv_full, ffn_conv_full = _unpack(conv_all, [gc_full.shape, fc_full.shape])

    big = [w_in, w_out, w_xq, w_xkv, w_xo, w_up, w_down]
    kvec = jnp.reshape(chip, (1,)).astype(jnp.int32)
    ab = jnp.zeros((L, 2, LANES), F32).at[:, 0, nh:2 * nh].set(gdn_a_log).at[:, 1, nh:2 * nh].set(gdn_dt_bias)

    def vec(p, l):
        return p[l:l + 1]

    xo_fwd_b = pl.BlockSpec((None, XW, cs_xo), lambda i, j, k: (j, 0, 0))
    xo_dg_b = pl.BlockSpec((None, XW, cs_xo), lambda i, j, k: (k, 0, 0))
    xo_wg_o = pl.BlockSpec((None, XW, cs_xo), lambda i, j, k: (j, 0, 0))
    up_fwd_b = pl.BlockSpec((None, D, tu), lambda i, j, k: (j // per, 0, j % per))

    def fwd_layer(l, xc, wts, token):
        g_in, g_out, g_xq, g_xkv, g_xo, g_up, g_down = wts
        g_mix = vec(mix_norm, l) if token is None else vec(mix_norm, l) + token[0:1, 0:1]
        w_in_l = jnp.concatenate([g_in[0], g_in[1], g_in[2], g_in[3], jnp.zeros((D, NP - n_in), BF16)], axis=1)
        w_out_l, w_xq_l, w_xkv_l, w_down_l = g_out.reshape(2 * GW, D), g_xq.reshape(D, XW), g_xkv.reshape(D, 2 * XW), g_down.reshape(F, D)
        s = dict(x0=xc, w_in=w_in_l, w_out=w_out_l, w_xq=w_xq_l, w_xkv=w_xkv_l, w_xo=g_xo, w_up=g_up, w_down=w_down_l)
        s["h"] = _rms_fwd(xc, g_mix, name="rms_mix_fwd")
        s["proj"] = _matmul(s["h"], w_in_l, tn=2432, tk=D, name="mm_in_fwd")
        s["sb"], s["tot"] = _sb_fwd(s["proj"], nh, name="sb_fwd")
        gdn_out, s["gdn"] = _gdn_forward(s["proj"], gdn_conv_full[l], ab[l], vec(gdn_norm, l), nh, "")
        s["mixed"] = jnp.concatenate([s["sb"], gdn_out], axis=1)
        s["x1"] = _matmul(s["mixed"], w_out_l, res=xc, tk=2 * GW, name="mm_out_fwd")
        s["memn"] = _rms_fwd(mem[0], vec(mem_norm, l), name="rms_mem_fwd")
        s["kv"] = _matmul(s["memn"], w_xkv_l, out_dtype=BF16, tk=D, name="mm_xkv_fwd")
        s["hq"] = _rms_fwd(s["x1"], vec(xattn_norm, l), name="rms_xattn_fwd")
        s["q"] = _matmul(s["hq"], w_xq_l, out_dtype=BF16, tk=D, name="mm_xq_fwd")
        s["xo"] = _xattn_fwd(s["q"], s["kv"], name="xattn_fwd")
        s["x2"] = _matmul(s["xo"], g_xo, res=s["x1"], dims=(S, D, XW), tn=cs_xo, tk=XW, b_spec=xo_fwd_b, name="mm_xo_fwd")
        s["hf"] = _rms_fwd(s["x2"], vec(ffn_norm, l), name="rms_ffn_fwd")
        s["u"] = _matmul(s["hf"], g_up, dims=(S, 2 * F, D), tn=tu, tk=D, b_spec=up_fwd_b, name="mm_up_fwd")
        s["act"] = _ffn_act_fwd(s["u"], ffn_conv_full[l], ffn_conv_bias[l:l + 1], name="ffn_act_fwd")
        x3 = _matmul(s["act"], w_down_l, res=s["x2"], tk=tu, name="mm_down_fwd")
        return x3, s

    def bwd_layer(l, s, dx3):
        dact = _matmul(dx3, s["w_down"], tb=True, tk=D, name="mm_down_dgrad")
        d_down = _matmul(s["act"], dx3, ta=True, tk=S, name="mm_down_wgrad")
        du3, dcw3, dcb3 = _ffn_act_bwd(s["u"], ffn_conv_full[l], ffn_conv_bias[l:l + 1], dact, name="ffn_act_bwd")
        dhf = _matmul(du3, s["w_up"], tb=True, dims=(S, D, 2 * F), tk=tu,
                      a_spec=pl.BlockSpec((None, _tile(S, 512), tu), lambda i, j, k: (k // fper, i, k % fper)),
                      b_spec=pl.BlockSpec((None, _tile(D, 512), tu), lambda i, j, k: (k // per, j, k % per)), name="mm_up_dgrad")
        d_up = _matmul(s["hf"], du3, ta=True, dims=(D, 2 * F, S), tn=tu, tk=S,
                       b_spec=pl.BlockSpec((None, S, tu), lambda i, j, k: (j // fper, 0, j % fper)),
                       o_spec=pl.BlockSpec((None, _tile(D, 512), tu), lambda i, j, k: (j // per, i, j % per)),
                       out_shape=jax.ShapeDtypeStruct((4, D, cs_up), F32), name="mm_up_wgrad")
        dx2, dg_ffn = _rms_bwd(s["x2"], vec(ffn_norm, l), dhf, dx3, name="rms_bwd")
        dxo = _matmul(dx2, s["w_xo"], tb=True, dims=(S, XW, D), tn=XW, tk=cs_xo, b_spec=xo_dg_b, name="mm_xo_dgrad")
        d_xo = _matmul(s["xo"], dx2, ta=True, dims=(XW, D, S), tm=XW, tn=cs_xo, tk=S, o_spec=xo_wg_o,
                       out_shape=jax.ShapeDtypeStruct((4, XW, cs_xo), F32), name="mm_xo_wgrad")
        dq, dk, dv = _xattn_bwd(s["q"], s["kv"], dxo, name="xattn_bwd")
        dkv = jnp.concatenate([dk, dv], axis=1)
        dhq = _matmul(dq, s["w_xq"], tb=True, tk=XW, name="mm_xq_dgrad")
        d_xq = _matmul(s["hq"], dq, ta=True, tk=S, name="mm_xq_wgrad")
        dmemn = _matmul(dkv, s["w_xkv"], tb=True, tk=2 * XW, name="mm_xkv_dgrad")
        d_xkv = _matmul(s["memn"], dkv, ta=True, tk=mem.shape[1], name="mm_xkv_wgrad")
        _, dg_mem = _rms_bwd(mem[0], vec(mem_norm, l), dmemn, None, name="rms_mem_bwd")
        dx1, dg_xattn = _rms_bwd(s["x1"], vec(xattn_norm, l), dhq, dx2, name="rms_bwd")
        dmix = _matmul(dx1, s["w_out"], tb=True, tk=D, name="mm_out_dgrad")
        d_out = _matmul(s["mixed"], dx1, ta=True, tk=S, name="mm_out_wgrad")
        dq_s, dk_s, dv_s = _sb_bwd(s["proj"], s["tot"], dmix, nh, name="sb_bwd")
        dx_qkv, dz, dx_g, dconv, dab, dng = _gdn_backward(s["proj"], gdn_conv_full[l], ab[l], vec(gdn_norm, l), s["gdn"], dmix, nh, "")
        dproj = jnp.concatenate([dq_s, dk_s, dv_s, dx_qkv, dz, dx_g], axis=1)
        dh = _matmul(dproj, s["w_in"], tb=True, tk=2432, name="mm_in_dgrad")
        d_in = _matmul(s["h"], dproj, ta=True, tn=2432, tk=S, name="mm_in_wgrad")
        dx0, dg_mix = _rms_bwd(s["x0"], vec(mix_norm, l), dh, dx1, name="rms_bwd")
        slabs = [jnp.stack([d_in[:, i * cs_in:(i + 1) * cs_in] for i in range(4)]), d_out.reshape(4, -1, D),
                 d_xq.reshape(4, -1, XW), d_xkv.reshape(4, -1, 2 * XW), d_xo, d_up, d_down.reshape(4, -1, D)]
        small = [dg_mix, dconv, dab, dng, dg_xattn, dg_mem, dg_ffn,
                 jnp.concatenate([dcw3[0], dcw3[1]], axis=1), jnp.concatenate([dcb3[0], dcb3[1]], axis=1)]
        return dx0, slabs, small

    def start_gather(l, after):
        placed = [_cast_place(w, l, kvec, name=f"cast_place_{l}") for w in big]
        return _gather_start(placed, after, name=f"gather_start_{l}")

    xc = x[0]
    saved = []
    pending = start_gather(0, conv_all)
    for l in range(L):
        send, recv, ws, _ = pending
        ws = _gather_wait(send, recv, ws, xc, name=f"gather_wait_{l}")
        wts = _gather_to_sibling(ws, name="gather_to_sibling")
        token = None
        if l + 1 < L:
            pending = start_gather(l + 1, wts[0])
            token = pending[3]
        xc, s = fwd_layer(l, xc, wts, token)
        saved.append(s)
    loss_blk, dxc, dg_final = _loss_head(xc, final_norm[None, :], loss_target[0], name="loss_head")

    def finish_reduce(pending, l, sums, after):
        send, recv, ps, lands, _ = pending
        ps, lands = _reduce_wait(send, recv, ps, lands, after, name=f"reduce_wait_{l}")
        return [_sum_chips(p, rb, kvec, prev, l, L, name=f"sum_chips_{l}") for p, rb, prev in zip(ps, lands, sums)]

    sums = [None] * 7
    small_by_layer = [None] * L
    pending = None
    for l in reversed(range(L)):
        if pending is not None:
            dxc = dxc + pending[4][0, 0]
        dxc, slabs, small_by_layer[l] = bwd_layer(l, saved[l], dxc)
        saved[l] = None
        if pending is not None:
            sums = finish_reduce(pending, l + 1, sums, dxc)
        from_sibling = _exchange_sibling(slabs, name="reduce_to_sibling")
        partial = [_sum_half(g, r, cvec, name="sum_sibling") for g, r in zip(slabs, from_sibling)]
        pending = _reduce_start(partial, cvec, name=f"reduce_start_{l}")

    small_flat = [a for l in range(L) for a in small_by_layer[l]] + [dg_final, loss_blk[0:1]]
    red_buf = _allreduce_small(_pack(small_flat), name="allreduce_small")
    sums = finish_reduce(pending, 0, sums, red_buf)
    from_sib = _swap_with_sibling(sums, name="swap_halves")
    red = _unpack(red_buf, [a.shape for a in small_flat])
    per_layer = [red[9 * l:9 * l + 9] for l in range(L)]
    col = lambda i: jnp.concatenate([p[i] for p in per_layer], axis=0)
    stk = lambda i: jnp.stack([p[i] for p in per_layer])
    g_conv_full, g_ab, g_fconv_full = stk(1), stk(2), stk(7)
    grads_small = dict(
        mix_norm=col(0), gdn_conv=lax.dynamic_slice(g_conv_full, (0, 0, chip * cs_gc), (L, SHORT_CONV, cs_gc)),
        gdn_a_log=g_ab[:, 0, nh:2 * nh], gdn_dt_bias=g_ab[:, 1, nh:2 * nh], gdn_norm=col(3), xattn_norm=col(4),
        mem_norm=col(5), ffn_norm=col(6), ffn_conv=lax.dynamic_slice(g_fconv_full, (0, 0, chip * cs_fc), (L, FFN_CONV, cs_fc)),
        ffn_conv_bias=col(8), final_norm=red[-2][0])
    loss = red[-1][0, 0]

    names_small = ["mix_norm", "gdn_conv", "gdn_a_log", "gdn_dt_bias", "gdn_norm", "xattn_norm", "mem_norm", "ffn_norm",
                   "ffn_conv", "ffn_conv_bias", "final_norm"]
    w_small = dict(mix_norm=mix_norm, gdn_conv=gdn_conv, gdn_a_log=gdn_a_log, gdn_dt_bias=gdn_dt_bias, gdn_norm=gdn_norm,
                   xattn_norm=xattn_norm, mem_norm=mem_norm, ffn_norm=ffn_norm, ffn_conv=ffn_conv, ffn_conv_bias=ffn_conv_bias,
                   final_norm=final_norm)
    m_small = dict(mix_norm=m_mix_norm, gdn_conv=m_gdn_conv, gdn_a_log=m_gdn_a_log, gdn_dt_bias=m_gdn_dt_bias, gdn_norm=m_gdn_norm,
                   xattn_norm=m_xattn_norm, mem_norm=m_mem_norm, ffn_norm=m_ffn_norm, ffn_conv=m_ffn_conv,
                   ffn_conv_bias=m_ffn_conv_bias, final_norm=m_final_norm)
    v_small = dict(mix_norm=v_mix_norm, gdn_conv=v_gdn_conv, gdn_a_log=v_gdn_a_log, gdn_dt_bias=v_gdn_dt_bias, gdn_norm=v_gdn_norm,
                   xattn_norm=v_xattn_norm, mem_norm=v_mem_norm, ffn_norm=v_ffn_norm, ffn_conv=v_ffn_conv,
                   ffn_conv_bias=v_ffn_conv_bias, final_norm=v_final_norm)
    shapes_small = [w_small[n].shape for n in names_small]
    packed = [_pack([d[n] for n in names_small], row_multiple=128)[None] for d in (w_small, grads_small, m_small, v_small)]
    upd_small = [_unpack(o[0], shapes_small) for o in _adamw(*packed, name="adamw_small")]
    delta, new_m, new_v = [dict(zip(names_small, u)) for u in upd_small]
    grads = dict(grads_small)
    big_names = ["w_in", "w_out", "w_xq", "w_xkv", "w_xo", "w_up", "w_down"]
    big_m = [m_w_in, m_w_out, m_w_xq, m_w_xkv, m_w_xo, m_w_up, m_w_down]
    big_v = [v_w_in, v_w_out, v_w_xq, v_w_xkv, v_w_xo, v_w_up, v_w_down]
    for n, w, go, gs, m, v in zip(big_names, big, sums, from_sib, big_m, big_v):
        grads[n], delta[n], new_m[n], new_v[n] = _adamw_halves(w, go, gs, cvec, m, v, name=f"adamw_{n}")

    order = ["mix_norm", "w_in", "gdn_conv", "gdn_a_log", "gdn_dt_bias", "gdn_norm", "w_out", "xattn_norm", "mem_norm", "w_xq",
             "w_xkv", "w_xo", "ffn_norm", "w_up", "ffn_conv", "ffn_conv_bias", "w_down", "final_norm"]
    return (loss, dxc[None], *[grads[n] for n in order], *[delta[n] for n in order], *[new_m[n] for n in order],
            *[new_v[n] for n in order])
```

```python
import functools
import math

import jax
import jax.numpy as jnp
from jax import lax
from jax.experimental import pallas as pl
from jax.experimental.pallas import tpu as pltpu

F32 = jnp.float32
BF16 = jnp.bfloat16

HEAD_DIM = 128
CHUNK = 64
GDN_CPB = 4
SB_TQ, SB_TK = 256, 512
SHORT_CONV = 4
FFN_CONV = 3
X_HEADS = 4
EPS = 1e-6
LANES = 128
VMEM_LIMIT = 56 * 2**20

ADAM_LR, ADAM_B1, ADAM_B2, ADAM_EPS, ADAM_WD, ADAM_STEP = 0.001, 0.9, 0.999, 1e-08, 0.01, 10

HI = lax.Precision.HIGH


def _params(sem):
    return pltpu.CompilerParams(dimension_semantics=sem, vmem_limit_bytes=VMEM_LIMIT)


def _tile(n, want):
    if n <= want:
        return n
    t = (want // LANES) * LANES
    while t > LANES and n % t:
        t -= LANES
    assert n % t == 0, (n, want)
    return t


def _sigmoid(x):
    return 1.0 / (1.0 + jnp.exp(-x))


def _softplus(x):
    return jnp.maximum(x, 0.0) + jnp.log(1.0 + jnp.exp(-jnp.abs(x)))


def _matmul(a, b, *, name, ta=False, tb=False, out_dtype=F32, res=None, tm=512, tn=512, tk=2048,
            dims=None, a_spec=None, b_spec=None, o_spec=None, out_shape=None):
    if dims is None:
        M, K = (a.shape[1], a.shape[0]) if ta else a.shape
        N = b.shape[0] if tb else b.shape[1]
    else:
        M, N, K = dims
    tm, tn, tk = _tile(M, tm), _tile(N, tn), _tile(K, tk)
    nk = K // tk
    dn = (((0 if ta else 1,), (1 if tb else 0,)), ((), ()))

    def body(*refs):
        a_ref, b_ref = refs[0], refs[1]
        r_ref = refs[2] if res is not None else None
        o_ref = refs[3] if res is not None else refs[2]
        p = lax.dot_general(a_ref[...].astype(BF16), b_ref[...].astype(BF16), dn, preferred_element_type=F32)

        def finish(acc):
            if r_ref is not None:
                acc = acc + r_ref[...].astype(F32)
            o_ref[...] = acc.astype(o_ref.dtype)

        if nk == 1:
            finish(p)
        else:
            acc_ref = refs[-1]
            k = pl.program_id(2)

            @pl.when(k == 0)
            def _():
                acc_ref[...] = p

            @pl.when(jnp.logical_and(k > 0, k < nk - 1))
            def _():
                acc_ref[...] += p

            @pl.when(k == nk - 1)
            def _():
                finish(acc_ref[...] + p)

    if a_spec is None:
        a_spec = pl.BlockSpec((tk, tm), lambda i, j, k: (k, i)) if ta else pl.BlockSpec((tm, tk), lambda i, j, k: (i, k))
    if b_spec is None:
        b_spec = pl.BlockSpec((tn, tk), lambda i, j, k: (j, k)) if tb else pl.BlockSpec((tk, tn), lambda i, j, k: (k, j))
    if o_spec is None:
        o_spec = pl.BlockSpec((tm, tn), lambda i, j, k: (i, j))
    if out_shape is None:
        out_shape = jax.ShapeDtypeStruct((M, N), out_dtype)
    in_specs, args = [a_spec, b_spec], [a, b]
    if res is not None:
        in_specs.append(pl.BlockSpec((tm, tn), lambda i, j, k: (i, j)))
        args.append(res)
    return pl.pallas_call(
        body, grid=(M // tm, N // tn, nk), in_specs=in_specs, out_specs=o_spec, out_shape=out_shape,
        scratch_shapes=[pltpu.VMEM((tm, tn), F32)] if nk > 1 else [],
        compiler_params=_params(("parallel", "parallel", "arbitrary")), name=name)(*args)


def _rms_fwd(x, g, *, name):
    R, D = x.shape
    tr = _tile(R, 256)

    def body(x_ref, g_ref, o_ref):
        xv = x_ref[...]
        rstd = lax.rsqrt(jnp.mean(xv * xv, axis=-1, keepdims=True) + EPS)
        o_ref[...] = (xv * rstd * g_ref[...]).astype(o_ref.dtype)

    return pl.pallas_call(
        body, grid=(R // tr,), in_specs=[pl.BlockSpec((tr, D), lambda i: (i, 0)), pl.BlockSpec((1, D), lambda i: (0, 0))],
        out_specs=pl.BlockSpec((tr, D), lambda i: (i, 0)), out_shape=jax.ShapeDtypeStruct((R, D), BF16),
        compiler_params=_params(("parallel",)), name=name)(x, g)


def _rms_bwd(x, g, dh, dres, *, name):
    R, D = x.shape
    tr = _tile(R, 256)

    def body(*refs):
        if dres is None:
            x_ref, g_ref, dh_ref, dx_ref, dxb_ref, dg_ref = refs
        else:
            x_ref, g_ref, dh_ref, dr_ref, dx_ref, dxb_ref, dg_ref = refs
        xv = x_ref[...]
        dhv = dh_ref[...].astype(F32)
        rstd = lax.rsqrt(jnp.mean(xv * xv, axis=-1, keepdims=True) + EPS)
        xhat = xv * rstd
        t = dhv * g_ref[...]
        dx = rstd * (t - xhat * jnp.mean(t * xhat, axis=-1, keepdims=True))
        if dres is not None:
            dx = dx + dr_ref[...]
        dx_ref[...] = dx
        dxb_ref[...] = dx.astype(BF16)
        part = jnp.sum(dhv * xhat, axis=0, keepdims=True)

        @pl.when(pl.program_id(0) == 0)
        def _():
            dg_ref[...] = part

        @pl.when(pl.program_id(0) > 0)
        def _():
            dg_ref[...] += part

    row = pl.BlockSpec((tr, D), lambda i: (i, 0))
    vec = pl.BlockSpec((1, D), lambda i: (0, 0))
    in_specs = [row, vec, row] + ([row] if dres is not None else [])
    args = [x, g, dh] + ([dres] if dres is not None else [])
    return pl.pallas_call(
        body, grid=(R // tr,), in_specs=in_specs, out_specs=[row, row, vec],
        out_shape=[jax.ShapeDtypeStruct((R, D), F32), jax.ShapeDtypeStruct((R, D), BF16), jax.ShapeDtypeStruct((1, D), F32)],
        compiler_params=_params(("arbitrary",)), name=name)(*args)


def _loss_head(x, g, tgt, *, name):
    R, D = x.shape
    tr = _tile(R, 256)

    def body(x_ref, g_ref, t_ref, l_ref, dx_ref, dxb_ref, dg_ref):
        xv = x_ref[...]
        rstd = lax.rsqrt(jnp.mean(xv * xv, axis=-1, keepdims=True) + EPS)
        xhat = xv * rstd
        err = xhat * g_ref[...] - t_ref[...]
        dy = err * (1.0 / D)
        t = dy * g_ref[...]
        dx = rstd * (t - xhat * jnp.mean(t * xhat, axis=-1, keepdims=True))
        dx_ref[...] = dx
        dxb_ref[...] = dx.astype(BF16)
        part = jnp.sum(dy * xhat, axis=0, keepdims=True)
        lpart = jnp.zeros((8, LANES), F32) + 0.5 * jnp.sum(jnp.mean(err * err, axis=-1, keepdims=True))

        @pl.when(pl.program_id(0) == 0)
        def _():
            dg_ref[...] = part
            l_ref[...] = lpart

        @pl.when(pl.program_id(0) > 0)
        def _():
            dg_ref[...] += part
            l_ref[...] += lpart

    row = pl.BlockSpec((tr, D), lambda i: (i, 0))
    vec = pl.BlockSpec((1, D), lambda i: (0, 0))
    return pl.pallas_call(
        body, grid=(R // tr,), in_specs=[row, vec, row],
        out_specs=[pl.BlockSpec((8, LANES), lambda i: (0, 0)), row, row, vec],
        out_shape=[jax.ShapeDtypeStruct((8, LANES), F32), jax.ShapeDtypeStruct((R, D), F32),
                   jax.ShapeDtypeStruct((R, D), BF16), jax.ShapeDtypeStruct((1, D), F32)],
        compiler_params=_params(("arbitrary",)), name=name)(x, g, tgt)


def _shift_down(x, s):
    if s == 0:
        return x
    row = lax.broadcasted_iota(jnp.int32, x.shape, 0)
    return jnp.where(row >= s, pltpu.roll(x, s, 0), 0.0)


def _shift_up(x, s):
    if s == 0:
        return x
    n = x.shape[0]
    row = lax.broadcasted_iota(jnp.int32, x.shape, 0)
    return jnp.where(row < n - s, pltpu.roll(x, n - s, 0), 0.0)


def _dwconv(x, w):
    k = w.shape[0]
    acc = x * w[k - 1:k, :]
    for i in range(k - 1):
        acc = acc + _shift_down(x, k - 1 - i) * w[i:i + 1, :]
    return acc


def _dwconv_bwd(x, w, dc):
    k = w.shape[0]
    dx = dc * w[k - 1:k, :]
    dws = []
    for i in range(k - 1):
        s = k - 1 - i
        dx = dx + _shift_up(dc, s) * w[i:i + 1, :]
        dws.append(jnp.sum(dc * _shift_down(x, s), axis=0, keepdims=True))
    dws.append(jnp.sum(dc * x, axis=0, keepdims=True))
    return dx, jnp.concatenate(dws, axis=0)


def _ffn_act_fwd(u, cw, cb, *, name):
    S, F2 = u.shape
    F = F2 // 2
    tc = _tile(F, 256)
    nb = F // tc

    def body(ug_ref, uu_ref, wg_ref, wu_ref, bg_ref, bu_ref, o_ref):
        cg = _dwconv(ug_ref[...], wg_ref[...]) + bg_ref[...]
        cu = _dwconv(uu_ref[...], wu_ref[...]) + bu_ref[...]
        o_ref[...] = (cg * _sigmoid(cg) * cu).astype(o_ref.dtype)

    blk = lambda r, off: pl.BlockSpec((r, tc), lambda j: (0, j + off))
    return pl.pallas_call(
        body, grid=(nb,), in_specs=[blk(S, 0), blk(S, nb), blk(FFN_CONV, 0), blk(FFN_CONV, nb), blk(1, 0), blk(1, nb)],
        out_specs=blk(S, 0), out_shape=jax.ShapeDtypeStruct((S, F), BF16),
        compiler_params=_params(("parallel",)), name=name)(u, u, cw, cw, cb, cb)


def _ffn_act_bwd(u, cw, cb, dact, *, name):
    S, F2 = u.shape
    F = F2 // 2
    tc = _tile(F, 256)
    nb = F // tc

    def body(ug_ref, uu_ref, wg_ref, wu_ref, bg_ref, bu_ref, da_ref, du_ref, dw_ref, db_ref):
        ug, uu = ug_ref[...], uu_ref[...]
        cg = _dwconv(ug, wg_ref[...]) + bg_ref[...]
        cu = _dwconv(uu, wu_ref[...]) + bu_ref[...]
        sg = _sigmoid(cg)
        da = da_ref[...].astype(F32)
        dcu = da * (cg * sg)
        dcg = da * cu * (sg * (1.0 + cg * (1.0 - sg)))
        dxg, dwg = _dwconv_bwd(ug, wg_ref[...], dcg)
        dxu, dwu = _dwconv_bwd(uu, wu_ref[...], dcu)
        du_ref[0] = dxg.astype(du_ref.dtype)
        du_ref[1] = dxu.astype(du_ref.dtype)
        dw_ref[0] = dwg
        dw_ref[1] = dwu
        db_ref[0] = jnp.sum(dcg, axis=0, keepdims=True)
        db_ref[1] = jnp.sum(dcu, axis=0, keepdims=True)

    blk = lambda r, off: pl.BlockSpec((r, tc), lambda j: (0, j + off))
    blk3 = lambda r: pl.BlockSpec((2, r, tc), lambda j: (0, 0, j))
    return pl.pallas_call(
        body, grid=(nb,),
        in_specs=[blk(S, 0), blk(S, nb), blk(FFN_CONV, 0), blk(FFN_CONV, nb), blk(1, 0), blk(1, nb), blk(S, 0)],
        out_specs=[blk3(S), blk3(FFN_CONV), blk3(1)],
        out_shape=[jax.ShapeDtypeStruct((2, S, F), BF16), jax.ShapeDtypeStruct((2, FFN_CONV, F), F32),
                   jax.ShapeDtypeStruct((2, 1, F), F32)],
        compiler_params=_params(("parallel",)), name=name)(u, u, cw, cw, cb, cb, dact)


def _xattn_fwd(q, kv, *, name):
    S, XW = q.shape
    M = kv.shape[0]
    nh = XW // HEAD_DIM
    tq = _tile(S, 512)
    scale = HEAD_DIM ** -0.5

    def body(q_ref, k_ref, v_ref, o_ref):
        z = lax.dot_general(q_ref[...], k_ref[...], (((1,), (1,)), ((), ())), preferred_element_type=F32) * scale
        e = jnp.exp(z - jnp.max(z, axis=-1, keepdims=True))
        p = e / jnp.sum(e, axis=-1, keepdims=True)
        o_ref[...] = jnp.dot(p.astype(BF16), v_ref[...], preferred_element_type=F32).astype(o_ref.dtype)

    return pl.pallas_call(
        body, grid=(nh, S // tq),
        in_specs=[pl.BlockSpec((tq, HEAD_DIM), lambda h, i: (i, h)), pl.BlockSpec((M, HEAD_DIM), lambda h, i: (0, h)),
                  pl.BlockSpec((M, HEAD_DIM), lambda h, i: (0, nh + h))],
        out_specs=pl.BlockSpec((tq, HEAD_DIM), lambda h, i: (i, h)), out_shape=jax.ShapeDtypeStruct((S, XW), BF16),
        compiler_params=_params(("parallel", "parallel")), name=name)(q, kv, kv)


def _xattn_bwd(q, kv, do, *, name):
    S, XW = q.shape
    M = kv.shape[0]
    nh = XW // HEAD_DIM
    tq = _tile(S, 512)
    scale = HEAD_DIM ** -0.5
    nt = (((1,), (1,)), ((), ()))
    tn = (((0,), (0,)), ((), ()))

    def body(q_ref, k_ref, v_ref, do_ref, dq_ref, dk_ref, dv_ref):
        qv, kvv, vv = q_ref[...], k_ref[...], v_ref[...]
        dov = do_ref[...].astype(BF16)
        z = lax.dot_general(qv, kvv, nt, preferred_element_type=F32) * scale
        e = jnp.exp(z - jnp.max(z, axis=-1, keepdims=True))
        p = e / jnp.sum(e, axis=-1, keepdims=True)
        dp = lax.dot_general(dov, vv, nt, preferred_element_type=F32)
        ds = (p * (dp - jnp.sum(dp * p, axis=-1, keepdims=True)) * scale).astype(BF16)
        dq_ref[...] = jnp.dot(ds, kvv, preferred_element_type=F32).astype(dq_ref.dtype)
        dk = lax.dot_general(ds, qv, tn, preferred_element_type=F32)
        dv = lax.dot_general(p.astype(BF16), dov, tn, preferred_element_type=F32)

        @pl.when(pl.program_id(1) == 0)
        def _():
            dk_ref[...] = dk
            dv_ref[...] = dv

        @pl.when(pl.program_id(1) > 0)
        def _():
            dk_ref[...] += dk
            dv_ref[...] += dv

    qs = pl.BlockSpec((tq, HEAD_DIM), lambda h, i: (i, h))
    ms = pl.BlockSpec((M, HEAD_DIM), lambda h, i: (0, h))
    return pl.pallas_call(
        body, grid=(nh, S // tq),
        in_specs=[qs, ms, pl.BlockSpec((M, HEAD_DIM), lambda h, i: (0, nh + h)), qs],
        out_specs=[qs, ms, ms],
        out_shape=[jax.ShapeDtypeStruct((S, XW), BF16), jax.ShapeDtypeStruct((M, XW), F32), jax.ShapeDtypeStruct((M, XW), F32)],
        compiler_params=_params(("parallel", "arbitrary")), name=name)(q, kv, kv, do)


_NN = (((1,), (0,)), ((), ()))
_NT = (((1,), (1,)), ((), ()))
_TN = (((0,), (0,)), ((), ()))


def _batched(dn, a):
    if a.ndim == 2:
        return dn
    (ca,), (cb,) = dn[0]
    return (((ca + 1,), (cb + 1,)), ((0,), (0,)))


def _dot(a, b, dn=_NN):
    return lax.dot_general(a.astype(BF16), b.astype(BF16), _batched(dn, a), preferred_element_type=F32)


def _dot_hi(a, b, dn=_NN):
    return lax.dot_general(a, b, _batched(dn, a), preferred_element_type=F32, precision=HI)


def _dot_split(a, b01, dn=_NN):
    hi = a.astype(BF16)
    lo = (a - hi.astype(F32)).astype(BF16)
    return (lax.dot_general(hi, b01, dn, preferred_element_type=F32)
            + lax.dot_general(lo, b01, dn, preferred_element_type=F32))


def _after_matrix(n, transpose=False):
    row = lax.broadcasted_iota(jnp.int32, (n, n), 0)
    col = lax.broadcasted_iota(jnp.int32, (n, n), 1)
    return (row < col if transpose else row > col).astype(BF16)


def _sb_fwd(proj, nh, *, name):
    S = proj.shape[0]
    TQ, TK = min(SB_TQ, S), min(SB_TK, S)
    nq = S // TQ
    scale = HEAD_DIM ** -0.5

    def body(q_ref, k_ref, v_ref, o_ref, tot_ref):
        i = pl.program_id(1)
        q = q_ref[...].astype(BF16)
        qpos = i * TQ + lax.broadcasted_iota(jnp.int32, (TQ, TK), 0)
        kcol = lax.broadcasted_iota(jnp.int32, (TQ, TK), 1)
        after = _after_matrix(TK)
        nt = ((i + 1) * TQ + TK - 1) // TK

        def step(t, carry):
            acc, out = carry
            off = pl.multiple_of((nt - 1 - t) * TK, TK)
            kb = k_ref[pl.ds(off, TK), :].astype(BF16)
            vb = v_ref[pl.ds(off, TK), :].astype(BF16)
            z = lax.dot_general(q, kb, _NT, preferred_element_type=F32) * scale
            valid = kcol + off < qpos
            ls = jnp.where(valid, -_softplus(z), 0.0)
            later = _dot_split(ls, after) + acc
            w = jnp.where(valid, jnp.exp(ls + z + later), 0.0)
            out = out + jnp.dot(w.astype(BF16), vb, preferred_element_type=F32)
            return acc + jnp.sum(ls, axis=1, keepdims=True), out

        acc, out = lax.fori_loop(0, nt, step, (jnp.zeros((TQ, 1), F32), jnp.zeros((TQ, HEAD_DIM), F32)))
        o_ref[...] = out.astype(o_ref.dtype)
        tot_ref[...] = acc

    return pl.pallas_call(
        body, grid=(nh, nq),
        in_specs=[pl.BlockSpec((TQ, HEAD_DIM), lambda h, i: (i, h)),
                  pl.BlockSpec((S, HEAD_DIM), lambda h, i: (0, nh + h)),
                  pl.BlockSpec((S, HEAD_DIM), lambda h, i: (0, 2 * nh + h))],
        out_specs=[pl.BlockSpec((TQ, HEAD_DIM), lambda h, i: (i, h)), pl.BlockSpec((None, TQ, 1), lambda h, i: (h, i, 0))],
        out_shape=[jax.ShapeDtypeStruct((S, nh * HEAD_DIM), BF16), jax.ShapeDtypeStruct((nh, S, 1), F32)],
        compiler_params=_params(("parallel", "parallel")), name=name)(proj, proj, proj)


def _sb_bwd(proj, tot, dmix, nh, *, name):
    S = proj.shape[0]
    TQ, TK = min(SB_TQ, S), min(SB_TK, S)
    nq = S // TQ
    scale = HEAD_DIM ** -0.5

    def body(q_ref, k_ref, v_ref, tot_ref, do_ref, dq_ref, dk_ref, dv_ref, dk_acc, dv_acc):
        i = pl.program_id(1)

        @pl.when(i == 0)
        def _():
            dk_acc[...] = jnp.zeros_like(dk_acc)
            dv_acc[...] = jnp.zeros_like(dv_acc)

        q = q_ref[...].astype(BF16)
        do = do_ref[...].astype(BF16)
        tot = tot_ref[...]
        qpos = i * TQ + lax.broadcasted_iota(jnp.int32, (TQ, TK), 0)
        kcol = lax.broadcasted_iota(jnp.int32, (TQ, TK), 1)
        after = _after_matrix(TK)
        before = _after_matrix(TK, transpose=True)
        nt = ((i + 1) * TQ + TK - 1) // TK

        def step(j, carry):
            pre, g_sum, dq = carry
            off = pl.multiple_of(j * TK, TK)
            kb = k_ref[pl.ds(off, TK), :].astype(BF16)
            vb = v_ref[pl.ds(off, TK), :].astype(BF16)
            z = lax.dot_general(q, kb, _NT, preferred_element_type=F32) * scale
            valid = kcol + off < qpos
            ls = jnp.where(valid, -_softplus(z), 0.0)
            lb = ls + z
            rs = jnp.sum(ls, axis=1, keepdims=True)
            later = _dot_split(ls, after) + (tot - pre - rs)
            w = jnp.where(valid, jnp.exp(lb + later), 0.0)
            g = lax.dot_general(do, vb, _NT, preferred_element_type=F32) * w
            dls = _dot_split(g, before) + g_sum
            sig = jnp.exp(lb)
            dz = (jnp.where(valid, g * (1.0 - sig) - dls * sig, 0.0) * scale).astype(BF16)
            dq = dq + jnp.dot(dz, kb, preferred_element_type=F32)
            dk_acc[pl.ds(off, TK), :] += lax.dot_general(dz, q, _TN, preferred_element_type=F32)
            dv_acc[pl.ds(off, TK), :] += lax.dot_general(w.astype(BF16), do, _TN, preferred_element_type=F32)
            return pre + rs, g_sum + jnp.sum(g, axis=1, keepdims=True), dq

        zero = jnp.zeros((TQ, 1), F32)
        _, _, dq = lax.fori_loop(0, nt, step, (zero, zero, jnp.zeros((TQ, HEAD_DIM), F32)))
        dq_ref[...] = dq.astype(dq_ref.dtype)

        @pl.when(i == nq - 1)
        def _():
            dk_ref[...] = dk_acc[...].astype(dk_ref.dtype)
            dv_ref[...] = dv_acc[...].astype(dv_ref.dtype)

    qs = pl.BlockSpec((TQ, HEAD_DIM), lambda h, i: (i, h))
    full = pl.BlockSpec((S, HEAD_DIM), lambda h, i: (0, h))
    o = jax.ShapeDtypeStruct((S, nh * HEAD_DIM), BF16)
    return pl.pallas_call(
        body, grid=(nh, nq),
        in_specs=[qs, pl.BlockSpec((S, HEAD_DIM), lambda h, i: (0, nh + h)),
                  pl.BlockSpec((S, HEAD_DIM), lambda h, i: (0, 2 * nh + h)),
                  pl.BlockSpec((None, TQ, 1), lambda h, i: (h, i, 0)), qs],
        out_specs=[qs, full, full], out_shape=[o, o, o],
        scratch_shapes=[pltpu.VMEM((S, HEAD_DIM), F32), pltpu.VMEM((S, HEAD_DIM), F32)],
        compiler_params=_params(("parallel", "arbitrary")), name=name)(proj, proj, proj, tot, dmix)


def _gdn_qkv_fwd(proj, conv_w, nh, *, name):
    S = proj.shape[0]
    GW = nh * HEAD_DIM
    scale = HEAD_DIM ** -0.5

    def body(x_ref, w_ref, o_ref):
        sec = pl.program_id(0) // nh
        c = _dwconv(x_ref[...], w_ref[...])
        s = c * _sigmoid(c)
        r = lax.rsqrt(jnp.sum(s * s, axis=1, keepdims=True) + EPS)
        fac = jnp.where(sec == 0, scale, 1.0)
        o_ref[...] = jnp.where(sec == 2, s, s * (r * fac))

    return pl.pallas_call(
        body, grid=(3 * nh,),
        in_specs=[pl.BlockSpec((S, HEAD_DIM), lambda j: (0, 3 * nh + j)), pl.BlockSpec((SHORT_CONV, HEAD_DIM), lambda j: (0, j))],
        out_specs=pl.BlockSpec((None, S, HEAD_DIM), lambda j: (j // nh, 0, j % nh)),
        out_shape=jax.ShapeDtypeStruct((3, S, GW), F32),
        compiler_params=_params(("parallel",)), name=name)(proj, conv_w)


def _gdn_qkv_bwd(proj, conv_w, dqkv, nh, *, name):
    S = proj.shape[0]
    GW = nh * HEAD_DIM
    scale = HEAD_DIM ** -0.5

    def body(x_ref, w_ref, d_ref, dx_ref, dw_ref):
        sec = pl.program_id(0) // nh
        x, w = x_ref[...], w_ref[...]
        c = _dwconv(x, w)
        sg = _sigmoid(c)
        s = c * sg
        r = lax.rsqrt(jnp.sum(s * s, axis=1, keepdims=True) + EPS)
        sh = s * r
        d = d_ref[...]
        fac = jnp.where(sec == 0, scale, 1.0)
        dn = (r * fac) * (d - sh * jnp.sum(d * sh, axis=1, keepdims=True))
        ds = jnp.where(sec == 2, d, dn)
        dx, dw = _dwconv_bwd(x, w, ds * (sg * (1.0 + c * (1.0 - sg))))
        dx_ref[...] = dx.astype(dx_ref.dtype)
        dw_ref[...] = dw

    return pl.pallas_call(
        body, grid=(3 * nh,),
        in_specs=[pl.BlockSpec((S, HEAD_DIM), lambda j: (0, 3 * nh + j)), pl.BlockSpec((SHORT_CONV, HEAD_DIM), lambda j: (0, j)),
                  pl.BlockSpec((None, S, HEAD_DIM), lambda j: (j // nh, 0, j % nh))],
        out_specs=[pl.BlockSpec((S, HEAD_DIM), lambda j: (0, j)), pl.BlockSpec((SHORT_CONV, HEAD_DIM), lambda j: (0, j))],
        out_shape=[jax.ShapeDtypeStruct((S, 3 * GW), BF16), jax.ShapeDtypeStruct((SHORT_CONV, 3 * GW), F32)],
        compiler_params=_params(("parallel",)), name=name)(proj, conv_w, dqkv)


def _gdn_gates_fwd(proj, ab, nh, *, name):
    S = proj.shape[0]
    C = CHUNK

    def body(x_ref, ab_ref, o_ref):
        ri = lax.broadcasted_iota(jnp.int32, (C, C), 0)
        ci = lax.broadcasted_iota(jnp.int32, (C, C), 1)
        ltri = (ri >= ci).astype(F32)
        lane = lax.broadcasted_iota(jnp.int32, (C, LANES), 1)
        a_coef = -jnp.exp(ab_ref[0:1, :])
        dt = ab_ref[1:2, :]

        def chunk(n, _):
            rows = pl.ds(pl.multiple_of(n * C, C), C)
            x = x_ref[rows, :]
            beta = _sigmoid(x)
            g = jnp.where(jnp.logical_and(lane >= nh, lane < 2 * nh), a_coef * _softplus(x + dt), 0.0)
            gc = _dot_hi(ltri, pltpu.roll(g, nh, 1))
            o_ref[rows, :] = jnp.where(lane < nh, beta, g) + gc
            return 0

        lax.fori_loop(0, S // C, chunk, 0)

    return pl.pallas_call(
        body, grid=(1,),
        in_specs=[pl.BlockSpec((S, LANES), lambda i: (0, 7 * nh)), pl.BlockSpec((2, LANES), lambda i: (0, 0))],
        out_specs=pl.BlockSpec((S, LANES), lambda i: (0, 0)), out_shape=jax.ShapeDtypeStruct((S, LANES), F32),
        compiler_params=_params(("arbitrary",)), name=name)(proj, ab)


def _gdn_gates_bwd(proj, ab, dgt, nh, *, name):
    S = proj.shape[0]
    C = CHUNK

    def body(x_ref, ab_ref, d_ref, dx_ref, dab_ref):
        ri = lax.broadcasted_iota(jnp.int32, (C, C), 0)
        ci = lax.broadcasted_iota(jnp.int32, (C, C), 1)
        utri = (ri <= ci).astype(F32)
        lane = lax.broadcasted_iota(jnp.int32, (C, LANES), 1)
        is_b = lane < nh
        is_a = jnp.logical_and(lane >= nh, lane < 2 * nh)
        a_coef = -jnp.exp(ab_ref[0:1, :])
        dt = ab_ref[1:2, :]

        def chunk(n, carry):
            da_log, ddt = carry
            rows = pl.ds(pl.multiple_of(n * C, C), C)
            x = x_ref[rows, :]
            d = d_ref[rows, :]
            beta = _sigmoid(x)
            dg = pltpu.roll(_dot_hi(utri, jnp.where(lane >= 2 * nh, d, 0.0)), LANES - nh, 1)
            dg = jnp.where(is_a, dg, 0.0)
            dxa = dg * a_coef * _sigmoid(x + dt)
            dxb = jnp.where(is_b, d * beta * (1.0 - beta), 0.0)
            dx_ref[rows, :] = (dxa + dxb).astype(dx_ref.dtype)
            da_log = da_log + jnp.sum(dg * a_coef * _softplus(x + dt), axis=0, keepdims=True)
            return da_log, ddt + jnp.sum(dxa, axis=0, keepdims=True)

        zero = jnp.zeros((1, LANES), F32)
        da_log, ddt = lax.fori_loop(0, S // C, chunk, (zero, zero))
        dab_ref[0:1, :] = da_log
        dab_ref[1:2, :] = ddt

    return pl.pallas_call(
        body, grid=(1,),
        in_specs=[pl.BlockSpec((S, LANES), lambda i: (0, 7 * nh)), pl.BlockSpec((2, LANES), lambda i: (0, 0)),
                  pl.BlockSpec((S, LANES), lambda i: (0, 0))],
        out_specs=[pl.BlockSpec((S, LANES), lambda i: (0, 0)), pl.BlockSpec((2, LANES), lambda i: (0, 0))],
        out_shape=[jax.ShapeDtypeStruct((S, LANES), BF16), jax.ShapeDtypeStruct((2, LANES), F32)],
        compiler_params=_params(("arbitrary",)), name=name)(proj, ab, dgt)


def _unit_lower_inverse(lmat):
    C = lmat.shape[-1]
    ri = lax.broadcasted_iota(jnp.int32, lmat.shape, lmat.ndim - 2)
    ci = lax.broadcasted_iota(jnp.int32, lmat.shape, lmat.ndim - 1)
    nmat = -lmat
    p = jnp.where(ri == ci, 1.0, 0.0) + nmat
    for _ in range(int(math.log2(C)) - 1):
        nmat = _dot_hi(nmat, nmat)
        p = p + _dot_hi(p, nmat)
    return p


def _gdn_chunk_common(q, k, v, gates, gc_row, h, nh, tinv=None):
    C = CHUNK
    lane = lax.broadcasted_iota(jnp.int32, gates.shape, gates.ndim - 1)
    beta = jnp.sum(jnp.where(lane == h, gates, 0.0), axis=-1, keepdims=True)
    gc = jnp.sum(jnp.where(lane == 2 * nh + h, gates, 0.0), axis=-1, keepdims=True)
    sq = gates.shape[:-1] + (C,)
    ri = lax.broadcasted_iota(jnp.int32, sq, len(sq) - 2)
    ci = lax.broadcasted_iota(jnp.int32, sq, len(sq) - 1)
    incl, strict = ri >= ci, ri > ci
    decay = jnp.where(incl, jnp.exp(jnp.where(incl, gc - gc_row, 0.0)), 0.0)
    egc = jnp.exp(gc)
    kb, vb = k * beta, v * beta
    lmat = jnp.where(strict, _dot(kb, k, _NT) * decay, 0.0)
    kbg = kb * egc
    u = w = None
    if tinv is None:
        tinv = _unit_lower_inverse(lmat)
        u = _dot(tinv, vb)
        w = _dot(tinv, kbg)
    amat = _dot(q, k, _NT) * decay
    glast = gc[..., C - 1:C, :]
    ekt = jnp.exp(glast - gc)
    return dict(q=q, k=k, v=v, beta=beta, decay=decay, egc=egc, kb=kb, vb=vb, lmat=lmat, tinv=tinv, kbg=kbg, u=u, w=w,
                amat=amat, qd=q * egc, ekt=ekt, kt=k * ekt, cd=jnp.exp(glast), strict=strict, incl=incl)


def _gdn_chunk_specs(nh, S, nc):
    return [pl.BlockSpec((3, S, HEAD_DIM), lambda h: (0, 0, h)),
            pl.BlockSpec((S, LANES), lambda h: (0, 0)),
            pl.BlockSpec((None, nc, 1, CHUNK), lambda h: (h, 0, 0, 0))]


def _gdn_state_free(qkv_ref, gates_ref, gr_ref, g, nb, h, nh):
    C = CHUNK
    rows = pl.ds(pl.multiple_of(g * (nb * C), nb * C), nb * C)
    part = lambda x: x.reshape(nb, C, x.shape[-1])
    return rows, _gdn_chunk_common(part(qkv_ref[0, rows, :]), part(qkv_ref[1, rows, :]), part(qkv_ref[2, rows, :]),
                                   part(gates_ref[rows, :]), gr_ref[pl.ds(g * nb, nb)], h, nh)


def _gdn_chunk_fwd(qkv, gates, gc_row, *, name):
    _, S, GW = qkv.shape
    nh, C = GW // HEAD_DIM, CHUNK
    nc = S // C
    nb = min(GDN_CPB, nc)
    flat = lambda x: x.reshape(nb * C, x.shape[-1])

    def body(qkv_ref, gates_ref, gr_ref, o_ref, st_ref, u_s, w_s, a_s, qd_s, kt_s, cd_s):
        h = pl.program_id(0)

        def group(g, _):
            rows, m = _gdn_state_free(qkv_ref, gates_ref, gr_ref, g, nb, h, nh)
            u_s[rows, :] = flat(m["u"])
            w_s[rows, :] = flat(m["w"])
            a_s[rows, :] = flat(m["amat"])
            qd_s[rows, :] = flat(m["qd"])
            kt_s[rows, :] = flat(m["kt"])
            cd_s[pl.ds(g * nb, nb)] = jnp.broadcast_to(m["cd"], (nb, 8, LANES))
            return 0

        lax.fori_loop(0, nc // nb, group, 0)

        def chunk(n, s0):
            rows = pl.ds(pl.multiple_of(n * C, C), C)
            st_ref[n] = s0
            v_new = u_s[rows, :] - _dot(w_s[rows, :], s0)
            o_ref[rows, :] = _dot(qd_s[rows, :], s0) + _dot(a_s[rows, :], v_new)
            return s0 * cd_s[n][0:1, :] + _dot(kt_s[rows, :], v_new, _TN)

        lax.fori_loop(0, nc, chunk, jnp.zeros((HEAD_DIM, HEAD_DIM), F32))

    seq = pltpu.VMEM((S, HEAD_DIM), F32)
    return pl.pallas_call(
        body, grid=(nh,), in_specs=_gdn_chunk_specs(nh, S, nc),
        out_specs=[pl.BlockSpec((S, HEAD_DIM), lambda h: (0, h)),
                   pl.BlockSpec((None, nc, HEAD_DIM, HEAD_DIM), lambda h: (h, 0, 0, 0))],
        out_shape=[jax.ShapeDtypeStruct((S, GW), F32), jax.ShapeDtypeStruct((nh, nc, HEAD_DIM, HEAD_DIM), F32)],
        scratch_shapes=[seq, seq, pltpu.VMEM((S, C), F32), seq, seq, pltpu.VMEM((nc, 8, LANES), F32)],
        compiler_params=_params(("parallel",)), name=name)(qkv, gates, gc_row)


def _gdn_chunk_bwd(qkv, gates, gc_row, states, do, *, name):
    _, S, GW = qkv.shape
    nh, C = GW // HEAD_DIM, CHUNK
    nc = S // C
    nb = min(GDN_CPB, nc)
    flat = lambda x: x.reshape(nb * C, x.shape[-1])
    part = lambda x: x.reshape(nb, C, x.shape[-1])

    def body(qkv_ref, gates_ref, gr_ref, st_ref, do_ref, dqkv_ref, dgt_ref,
             t_s, vn_s, w_s, a_s, qd_s, kt_s, cd_s, dvn_s, dkt_s, dcd_s):
        h = pl.program_id(0)

        def group(g, _):
            rows, m = _gdn_state_free(qkv_ref, gates_ref, gr_ref, g, nb, h, nh)
            t_s[rows, :] = flat(m["tinv"])
            vn_s[rows, :] = flat(m["u"])
            w_s[rows, :] = flat(m["w"])
            a_s[rows, :] = flat(m["amat"])
            qd_s[rows, :] = flat(m["qd"])
            kt_s[rows, :] = flat(m["kt"])
            cd_s[pl.ds(g * nb, nb)] = jnp.broadcast_to(m["cd"], (nb, 8, LANES))
            return 0

        lax.fori_loop(0, nc // nb, group, 0)

        def chunk(t, dsn):
            n = nc - 1 - t
            rows = pl.ds(pl.multiple_of(n * C, C), C)
            s0, dout, w = st_ref[n], do_ref[rows, :], w_s[rows, :]
            v_new = vn_s[rows, :] - _dot(w, s0)
            dvn = _dot(a_s[rows, :], dout, _TN) + _dot(kt_s[rows, :], dsn)
            vn_s[rows, :] = v_new
            dvn_s[rows, :] = dvn
            dkt_s[rows, :] = _dot(v_new, dsn, _NT)
            dcd_s[n] = jnp.zeros((8, LANES), F32) + jnp.sum(dsn * s0)
            return _dot(qd_s[rows, :], dout, _TN) + dsn * cd_s[n][0:1, :] - _dot(w, dvn, _TN)

        lax.fori_loop(0, nc, chunk, jnp.zeros((HEAD_DIM, HEAD_DIM), F32))

        def rest(g, _):
            rows, m = _gdn_state_free_again(qkv_ref, gates_ref, gr_ref, t_s, g, nb, h, nh)
            chunks = pl.ds(g * nb, nb)
            dq, dk, dv, dgt = _gdn_chunk_grad(m, st_ref[chunks], part(vn_s[rows, :]), part(dvn_s[rows, :]),
                                              part(dkt_s[rows, :]), dcd_s[chunks][:, 0:1, 0:1], part(do_ref[rows, :]), h, nh)
            dqkv_ref[0, rows, :] = flat(dq)
            dqkv_ref[1, rows, :] = flat(dk)
            dqkv_ref[2, rows, :] = flat(dv)
            dgt_ref[rows, :] = flat(dgt)
            return 0

        lax.fori_loop(0, nc // nb, rest, 0)

    seq = pltpu.VMEM((S, HEAD_DIM), F32)
    small = pltpu.VMEM((nc, 8, LANES), F32)
    return pl.pallas_call(
        body, grid=(nh,),
        in_specs=_gdn_chunk_specs(nh, S, nc) + [
            pl.BlockSpec((None, nc, HEAD_DIM, HEAD_DIM), lambda h: (h, 0, 0, 0)),
            pl.BlockSpec((S, HEAD_DIM), lambda h: (0, h))],
        out_specs=[pl.BlockSpec((3, S, HEAD_DIM), lambda h: (0, 0, h)), pl.BlockSpec((None, S, LANES), lambda h: (h, 0, 0))],
        out_shape=[jax.ShapeDtypeStruct((3, S, GW), F32), jax.ShapeDtypeStruct((nh, S, LANES), F32)],
        scratch_shapes=[pltpu.VMEM((S, C), F32), seq, seq, pltpu.VMEM((S, C), F32), seq, seq, small, seq, seq, small],
        compiler_params=_params(("parallel",)), name=name)(qkv, gates, gc_row, states, do)


def _gdn_state_free_again(qkv_ref, gates_ref, gr_ref, t_s, g, nb, h, nh):
    C = CHUNK
    rows = pl.ds(pl.multiple_of(g * (nb * C), nb * C), nb * C)
    part = lambda x: x.reshape(nb, C, x.shape[-1])
    m = _gdn_chunk_common(part(qkv_ref[0, rows, :]), part(qkv_ref[1, rows, :]), part(qkv_ref[2, rows, :]),
                          part(gates_ref[rows, :]), gr_ref[pl.ds(g * nb, nb)], h, nh, tinv=part(t_s[rows, :]))
    return rows, m


def _gdn_chunk_grad(m, s0, v_new, dvn, dkt, dcd, dout, h, nh):
    C = CHUNK
    q, k, v, beta, decay, egc = m["q"], m["k"], m["v"], m["beta"], m["decay"], m["egc"]
    tinv, kt, cd = m["tinv"], m["kt"], m["cd"]
    dqd = _dot(dout, s0, _NT)
    damat = jnp.where(m["incl"], _dot(dout, v_new, _NT), 0.0)
    dw = -_dot(dvn, s0, _NT)
    dvb = _dot(tinv, dvn, _TN)
    dkbg = _dot(tinv, dw, _TN)
    dtinv = _dot(dvn, m["vb"], _NT) + _dot(dw, m["kbg"], _NT)
    dl = jnp.where(m["strict"], -_dot_hi(_dot_hi(tinv, dtinv, _TN), tinv, _NT), 0.0)
    dkk = dl * decay
    dqk = damat * decay
    dkb = _dot(dkk, k) + dkbg * egc
    dk = _dot(dkk, m["kb"], _TN) + _dot(dqk, q, _TN) + dkt * m["ekt"] + dkb * beta
    dq = _dot(dqk, k) + dqd * egc
    mm = dl * m["lmat"] + damat * m["amat"]
    ones = jnp.ones(q.shape, F32)
    rk = jnp.sum(dkt * kt, axis=-1, keepdims=True)
    dgc = (_dot_hi(mm, ones) - _dot_hi(mm, ones, _TN) + jnp.sum(dqd * m["qd"], axis=-1, keepdims=True) - rk
           + jnp.sum(dkbg * m["kbg"], axis=-1, keepdims=True))
    dglast = jnp.sum(rk, axis=-2, keepdims=True) + dcd * cd
    rowi = lax.broadcasted_iota(jnp.int32, q.shape, q.ndim - 2)
    lane = lax.broadcasted_iota(jnp.int32, q.shape, q.ndim - 1)
    dgc = dgc + jnp.where(rowi == C - 1, dglast, 0.0)
    dbeta = jnp.sum(dkb * k, axis=-1, keepdims=True) + jnp.sum(dvb * v, axis=-1, keepdims=True)
    dgt = jnp.where(lane == h, dbeta, 0.0) + jnp.where(lane == 2 * nh + h, dgc, 0.0)
    return dq, dk, dvb * beta, dgt


def _gdn_post_fwd(o, proj, ng, nh, *, name):
    S, GW = o.shape

    def body(o_ref, z_ref, g_ref, y_ref):
        ov, z = o_ref[...], z_ref[...]
        rstd = lax.rsqrt(jnp.mean(ov * ov, axis=-1, keepdims=True) + EPS)
        y_ref[...] = (ov * rstd * g_ref[...] * (z * _sigmoid(z))).astype(y_ref.dtype)

    blk = pl.BlockSpec((S, HEAD_DIM), lambda h: (0, h))
    return pl.pallas_call(
        body, grid=(nh,), in_specs=[blk, pl.BlockSpec((S, HEAD_DIM), lambda h: (0, 6 * nh + h)), pl.BlockSpec((1, HEAD_DIM), lambda h: (0, 0))],
        out_specs=blk, out_shape=jax.ShapeDtypeStruct((S, GW), BF16),
        compiler_params=_params(("parallel",)), name=name)(o, proj, ng)


def _gdn_post_bwd(o, proj, ng, dmix, nh, *, name):
    S, GW = o.shape

    def body(o_ref, z_ref, g_ref, d_ref, do_ref, dz_ref, dg_ref):
        ov, z, d = o_ref[...], z_ref[...], d_ref[...].astype(F32)
        rstd = lax.rsqrt(jnp.mean(ov * ov, axis=-1, keepdims=True) + EPS)
        oh = ov * rstd
        sz = _sigmoid(z)
        dy = d * (z * sz)
        dz_ref[...] = (d * (oh * g_ref[...]) * (sz * (1.0 + z * (1.0 - sz)))).astype(dz_ref.dtype)
        t = dy * g_ref[...]
        do_ref[...] = rstd * (t - oh * jnp.mean(t * oh, axis=-1, keepdims=True))
        part = jnp.sum(dy * oh, axis=0, keepdims=True)

        @pl.when(pl.program_id(0) == 0)
        def _():
            dg_ref[...] = part

        @pl.when(pl.program_id(0) > 0)
        def _():
            dg_ref[...] += part

    blk = pl.BlockSpec((S, HEAD_DIM), lambda h: (0, h))
    vec = pl.BlockSpec((1, HEAD_DIM), lambda h: (0, 0))
    return pl.pallas_call(
        body, grid=(nh,),
        in_specs=[blk, pl.BlockSpec((S, HEAD_DIM), lambda h: (0, 6 * nh + h)), vec, pl.BlockSpec((S, HEAD_DIM), lambda h: (0, nh + h))],
        out_specs=[blk, blk, vec],
        out_shape=[jax.ShapeDtypeStruct((S, GW), F32), jax.ShapeDtypeStruct((S, GW), BF16), jax.ShapeDtypeStruct((1, HEAD_DIM), F32)],
        compiler_params=_params(("arbitrary",)), name=name)(o, proj, ng, dmix)


def _gdn_forward(proj, conv_w, ab, ng, nh, tag):
    S = proj.shape[0]
    nc = S // CHUNK
    qkv = _gdn_qkv_fwd(proj, conv_w, nh, name=f"gdn_qkv_fwd{tag}")
    gates = _gdn_gates_fwd(proj, ab, nh, name=f"gdn_gates_fwd{tag}")
    gc_row = gates[:, 2 * nh:3 * nh].T.reshape(nh, nc, 1, CHUNK)
    o, states = _gdn_chunk_fwd(qkv, gates, gc_row, name=f"gdn_chunk_fwd{tag}")
    y = _gdn_post_fwd(o, proj, ng, nh, name=f"gdn_post_fwd{tag}")
    return y, (qkv, gates, gc_row, states, o)


def _gdn_backward(proj, conv_w, ab, ng, saved, dmix, nh, tag):
    qkv, gates, gc_row, states, o = saved
    do, dz, dng = _gdn_post_bwd(o, proj, ng, dmix, nh, name=f"gdn_post_bwd{tag}")
    dqkv, dgt_heads = _gdn_chunk_bwd(qkv, gates, gc_row, states, do, name=f"gdn_chunk_bwd{tag}")
    dx_qkv, dconv = _gdn_qkv_bwd(proj, conv_w, dqkv, nh, name=f"gdn_qkv_bwd{tag}")
    dx_g, dab = _gdn_gates_bwd(proj, ab, jnp.sum(dgt_heads, axis=0), nh, name=f"gdn_gates_bwd{tag}")
    return dx_qkv, dz, dx_g, dconv, dab, dng


def _row_tile(r, cap=128):
    t = cap
    while r % t:
        t //= 2
    assert t >= 8, r
    return t


def _adamw_update(gv, w_ref, m_ref, v_ref, d_ref, m2_ref, v2_ref):
    m2 = ADAM_B1 * m_ref[...] + (1.0 - ADAM_B1) * gv
    v2 = ADAM_B2 * v_ref[...] + (1.0 - ADAM_B2) * (gv * gv)
    m_hat = m2 / (1.0 - ADAM_B1 ** ADAM_STEP)
    v_hat = v2 / (1.0 - ADAM_B2 ** ADAM_STEP)
    d_ref[...] = -ADAM_LR * (m_hat / (jnp.sqrt(v_hat) + ADAM_EPS) + ADAM_WD * w_ref[...])
    m2_ref[...] = m2
    v2_ref[...] = v2


def _adamw(w, g, m, v, *, name):
    L, r, c = w.shape
    tr = _row_tile(r)

    def body(w_ref, g_ref, m_ref, v_ref, d_ref, m2_ref, v2_ref):
        _adamw_update(g_ref[...], w_ref, m_ref, v_ref, d_ref, m2_ref, v2_ref)

    blk = pl.BlockSpec((None, tr, c), lambda l, i: (l, i, 0))
    o = jax.ShapeDtypeStruct(w.shape, F32)
    return pl.pallas_call(
        body, grid=(L, r // tr), in_specs=[blk] * 4, out_specs=[blk] * 3, out_shape=[o, o, o],
        compiler_params=_params(("parallel", "parallel")), name=name)(w, g, m, v)


def _adamw_halves(w, g_own, g_sib, cvec, m, v, *, name):
    L, r, c = w.shape
    tr = _row_tile(r // 2)
    nbh = (r // 2) // tr

    def body(c_ref, w_ref, go_ref, gs_ref, m_ref, v_ref, g_out, d_ref, m2_ref, v2_ref):
        gv = jnp.where(pl.program_id(1) // nbh == c_ref[0], go_ref[...], gs_ref[...])
        g_out[...] = gv
        _adamw_update(gv, w_ref, m_ref, v_ref, d_ref, m2_ref, v2_ref)

    lo = lambda i: jnp.minimum(i, nbh - 1)
    hi = lambda i: jnp.maximum(i - nbh, 0)
    blk = pl.BlockSpec((None, tr, c), lambda l, i, c_ref: (l, i, 0))
    own = pl.BlockSpec((None, tr, c), lambda l, i, c_ref: (l, jnp.where(c_ref[0] == 0, lo(i), hi(i)), 0))
    sib = pl.BlockSpec((None, tr, c), lambda l, i, c_ref: (l, jnp.where(c_ref[0] == 0, hi(i), lo(i)), 0))
    o = jax.ShapeDtypeStruct(w.shape, F32)
    return pl.pallas_call(
        body,
        grid_spec=pltpu.PrefetchScalarGridSpec(
            num_scalar_prefetch=1, grid=(L, r // tr), in_specs=[blk, own, sib, blk, blk], out_specs=[blk] * 4),
        out_shape=[o, o, o, o],
        compiler_params=_params(("parallel", "arbitrary")), name=name)(cvec, w, g_own, g_sib, m, v)


def _sum_half(g, rbuf, cvec, *, name):
    _, r, c = g.shape
    h = r // 2
    tr = _row_tile(h)
    nb = h // tr

    def body(c_ref, g_ref, r_ref, o_ref):
        o_ref[...] = (g_ref[...] + r_ref[...]).astype(o_ref.dtype)

    blk = pl.BlockSpec((None, tr, c), lambda s, i, c_ref: (s, i, 0))
    return pl.pallas_call(
        body,
        grid_spec=pltpu.PrefetchScalarGridSpec(
            num_scalar_prefetch=1, grid=(4, nb),
            in_specs=[pl.BlockSpec((None, tr, c), lambda s, i, c_ref: (s, c_ref[0] * nb + i, 0)), blk], out_specs=blk),
        out_shape=jax.ShapeDtypeStruct((4, h, c), BF16),
        compiler_params=_params(("parallel", "parallel")), name=name)(cvec, g, rbuf)


def _sum_chips(p, rb, kvec, prev, l, nl, *, name):
    _, h, c = rb.shape
    tr = _row_tile(h)

    def body(k_ref, p_ref, r1, r2, r3, *rest):
        o_ref = rest[-1]
        o_ref[...] = ((p_ref[...].astype(F32) + r1[...].astype(F32)) + r2[...].astype(F32)) + r3[...].astype(F32)

    slot = lambda s: pl.BlockSpec((None, tr, c), lambda i, k_ref: ((k_ref[0] + s) % 4, i, 0))
    in_specs = [slot(0), slot(1), slot(2), slot(3)]
    args = [kvec, p, rb, rb, rb]
    if prev is not None:
        in_specs.append(pl.BlockSpec(memory_space=pltpu.HBM))
        args.append(prev)
    return pl.pallas_call(
        body,
        grid_spec=pltpu.PrefetchScalarGridSpec(
            num_scalar_prefetch=1, grid=(h // tr,), in_specs=in_specs,
            out_specs=pl.BlockSpec((None, tr, c), lambda i, k_ref: (l, i, 0))),
        out_shape=jax.ShapeDtypeStruct((nl, h, c), F32), input_output_aliases={5: 0} if prev is not None else {},
        compiler_params=_params(("parallel",)), name=name)(*args)


_MESH = pl.DeviceIdType.MESH
_HBM = pl.BlockSpec(memory_space=pltpu.HBM)


def _place():
    x, y, c = lax.axis_index("x"), lax.axis_index("y"), lax.axis_index("c")
    return x, y, c, [(1 - x, y), (x, 1 - y), (1 - x, 1 - y)]


def _cast_place(w, l, kvec, *, name):
    _, r, c = w.shape
    tr = _row_tile(r, 256)

    def body(k_ref, w_ref, o_ref):
        o_ref[...] = w_ref[...].astype(o_ref.dtype)

    return pl.pallas_call(
        body,
        grid_spec=pltpu.PrefetchScalarGridSpec(
            num_scalar_prefetch=1, grid=(r // tr,),
            in_specs=[pl.BlockSpec((None, tr, c), lambda i, k_ref: (l, i, 0))],
            out_specs=pl.BlockSpec((None, tr, c), lambda i, k_ref: (k_ref[0], i, 0))),
        out_shape=jax.ShapeDtypeStruct((4, r, c), BF16),
        compiler_params=_params(("parallel",)), name=name)(kvec, w)


_SEM = pl.BlockSpec(memory_space=pltpu.SEMAPHORE)
_ANY = pl.BlockSpec(memory_space=pl.ANY)
_EFFECT = pltpu.SideEffectType.DATAFLOW_SIDE_EFFECTING


def _in_hbm(a):
    return pltpu.with_memory_space_constraint(a, pltpu.HBM)


def _gather_copies(w_refs, send, recv, landing):
    x, y, c, chips = _place()
    k = 2 * x + y
    cps = []
    for a, w in enumerate(w_refs):
        h = w.shape[1] // 2
        for j, (cx, cy) in enumerate(chips):
            cps.append(pltpu.make_async_remote_copy(
                src_ref=w.at[k, pl.ds(c * h, h), :], dst_ref=w.at[2 * cx + cy if landing else k, pl.ds(c * h, h), :],
                send_sem=send.at[3 * a + j], recv_sem=recv.at[3 * a + j], device_id=(cx, cy, c), device_id_type=_MESH))
    return cps


def _gather_start(ws, after, *, name):
    n = len(ws)

    def body(*refs):
        send, recv = refs[n + 1], refs[n + 2]
        o_refs, token = refs[n + 3:2 * n + 3], refs[2 * n + 3]
        for cp in _gather_copies(o_refs, send, recv, landing=False):
            cp.start()
        token[...] = jnp.zeros_like(token)

    out = pl.pallas_call(
        body, in_specs=[_HBM] * n + [_ANY], out_specs=[_SEM, _SEM] + [_HBM] * n + [pl.BlockSpec(memory_space=pltpu.VMEM)],
        out_shape=[pltpu.SemaphoreType.DMA((3 * n,)), pltpu.SemaphoreType.DMA((3 * n,))]
        + [pltpu.HBM(w.shape, w.dtype) for w in ws] + [jax.ShapeDtypeStruct((8, LANES), F32)],
        input_output_aliases={a: 2 + a for a in range(n)},
        compiler_params=pltpu.CompilerParams(has_side_effects=_EFFECT), name=name)(*[_in_hbm(w) for w in ws], after)
    return out[0], out[1], list(out[2:2 + n]), out[2 + n]


def _gather_wait(send, recv, ws, after, *, name):
    n = len(ws)

    def body(*refs):
        for cp in _gather_copies(refs[:n], refs[n], refs[n + 1], landing=True):
            cp.wait_send()
            cp.wait_recv()

    return list(pl.pallas_call(
        body, in_specs=[_HBM] * n + [_SEM, _SEM, _ANY], out_specs=[_HBM] * n,
        out_shape=[pltpu.HBM(w.shape, w.dtype) for w in ws], input_output_aliases={a: a for a in range(n)},
        compiler_params=pltpu.CompilerParams(has_side_effects=_EFFECT), name=name)(*ws, send, recv, after))


def _gather_to_sibling(ws, *, name):
    n = len(ws)

    def body(*refs):
        o_refs = refs[n:2 * n]
        send, recv = refs[2 * n:]
        x, y, c, chips = _place()
        cps = []
        for a in range(n):
            h = o_refs[a].shape[1] // 2
            for j, (cx, cy) in enumerate(chips):
                landed = o_refs[a].at[2 * cx + cy, pl.ds(c * h, h), :]
                cp = pltpu.make_async_remote_copy(
                    src_ref=landed, dst_ref=landed, send_sem=send.at[3 * a + j], recv_sem=recv.at[3 * a + j],
                    device_id=(x, y, 1 - c), device_id_type=_MESH)
                cp.start()
                cps.append(cp)
        for a in range(n):
            h = o_refs[a].shape[1] // 2
            for j, (cx, cy) in enumerate(chips):
                other = o_refs[a].at[2 * cx + cy, pl.ds((1 - c) * h, h), :]
                pltpu.make_async_remote_copy(
                    src_ref=other, dst_ref=other, send_sem=send.at[3 * a + j], recv_sem=recv.at[3 * a + j],
                    device_id=(x, y, c), device_id_type=_MESH).wait_recv()
        for cp in cps:
            cp.wait_send()

    return list(pl.pallas_call(
        body, in_specs=[_HBM] * n, out_specs=[_HBM] * n,
        out_shape=[jax.ShapeDtypeStruct(w.shape, w.dtype) for w in ws],
        input_output_aliases={a: a for a in range(n)},
        scratch_shapes=[pltpu.SemaphoreType.DMA((3 * n,))] * 2, name=name)(*ws))


def _exchange_sibling(gs, *, name):
    n = len(gs)

    def body(*refs):
        g_refs, o_refs = refs[:n], refs[n:2 * n]
        send, recv = refs[2 * n:]
        x, y, c, _ = _place()
        cps = []
        for a in range(n):
            h = g_refs[a].shape[1] // 2
            cp = pltpu.make_async_remote_copy(
                src_ref=g_refs[a].at[:, pl.ds((1 - c) * h, h), :], dst_ref=o_refs[a], send_sem=send.at[a], recv_sem=recv.at[a],
                device_id=(x, y, 1 - c), device_id_type=_MESH)
            cp.start()
            cps.append(cp)
        for cp in cps:
            cp.wait_recv()
        for cp in cps:
            cp.wait_send()

    return pl.pallas_call(
        body, in_specs=[_HBM] * n, out_specs=[_HBM] * n,
        out_shape=[jax.ShapeDtypeStruct((4, g.shape[1] // 2, g.shape[2]), g.dtype) for g in gs],
        scratch_shapes=[pltpu.SemaphoreType.DMA((n,))] * 2, name=name)(*gs)


def _reduce_copies(p_refs, r_refs, send, recv, landing):
    x, y, c, chips = _place()
    k = 2 * x + y
    cps = []
    for a, (p, r) in enumerate(zip(p_refs, r_refs)):
        for j, (cx, cy) in enumerate(chips):
            cps.append(pltpu.make_async_remote_copy(
                src_ref=p.at[2 * cx + cy], dst_ref=r.at[2 * cx + cy if landing else k], send_sem=send.at[3 * a + j],
                recv_sem=recv.at[3 * a + j], device_id=(cx, cy, c), device_id_type=_MESH))
    return cps


def _reduce_start(ps, after, *, name):
    n = len(ps)
    lands = [lax.empty(p.shape, p.dtype) for p in ps]

    def body(*refs):
        send, recv = refs[2 * n + 1], refs[2 * n + 2]
        p_out, r_out, token = refs[2 * n + 3:3 * n + 3], refs[3 * n + 3:4 * n + 3], refs[4 * n + 3]
        for cp in _reduce_copies(p_out, r_out, send, recv, landing=False):
            cp.start()
        token[...] = jnp.zeros_like(token)

    out = pl.pallas_call(
        body, in_specs=[_HBM] * (2 * n) + [_ANY],
        out_specs=[_SEM, _SEM] + [_HBM] * (2 * n) + [pl.BlockSpec(memory_space=pltpu.VMEM)],
        out_shape=[pltpu.SemaphoreType.DMA((3 * n,)), pltpu.SemaphoreType.DMA((3 * n,))]
        + [pltpu.HBM(p.shape, p.dtype) for p in ps] * 2 + [jax.ShapeDtypeStruct((8, LANES), F32)],
        input_output_aliases={a: 2 + a for a in range(2 * n)},
        compiler_params=pltpu.CompilerParams(has_side_effects=_EFFECT), name=name)(
            *[_in_hbm(p) for p in ps], *[_in_hbm(r) for r in lands], after)
    return out[0], out[1], list(out[2:2 + n]), list(out[2 + n:2 + 2 * n]), out[2 + 2 * n]


def _reduce_wait(send, recv, ps, lands, after, *, name):
    n = len(ps)

    def body(*refs):
        for cp in _reduce_copies(refs[:n], refs[n:2 * n], refs[2 * n], refs[2 * n + 1], landing=True):
            cp.wait_send()
            cp.wait_recv()

    out = pl.pallas_call(
        body, in_specs=[_HBM] * (2 * n) + [_SEM, _SEM, _ANY], out_specs=[_HBM] * (2 * n),
        out_shape=[pltpu.HBM(p.shape, p.dtype) for p in ps] * 2, input_output_aliases={a: a for a in range(2 * n)},
        compiler_params=pltpu.CompilerParams(has_side_effects=_EFFECT), name=name)(*ps, *lands, send, recv, after)
    return list(out[:n]), list(out[n:])


def _swap_with_sibling(gs, *, name):
    n = len(gs)

    def body(*refs):
        g_refs, o_refs = refs[:n], refs[n:2 * n]
        send, recv = refs[2 * n:]
        x, y, c, _ = _place()
        cps = []
        for a in range(n):
            cp = pltpu.make_async_remote_copy(
                src_ref=g_refs[a], dst_ref=o_refs[a], send_sem=send.at[a], recv_sem=recv.at[a],
                device_id=(x, y, 1 - c), device_id_type=_MESH)
            cp.start()
            cps.append(cp)
        for cp in cps:
            cp.wait_recv()
        for cp in cps:
            cp.wait_send()

    return pl.pallas_call(
        body, in_specs=[_HBM] * n, out_specs=[_HBM] * n,
        out_shape=[jax.ShapeDtypeStruct(g.shape, g.dtype) for g in gs],
        scratch_shapes=[pltpu.SemaphoreType.DMA((n,))] * 2, name=name)(*gs)


def _allreduce_small(v, *, name):
    R = v.shape[0]

    def body(v_ref, o_ref, buf, send, recv, loc):
        x, y, c = lax.axis_index("x"), lax.axis_index("y"), lax.axis_index("c")
        me = 4 * x + 2 * y + c
        mine = pltpu.make_async_copy(v_ref, buf.at[me], loc)
        mine.start()
        cps = []
        for d in range(1, 8):
            px = 1 - x if d & 4 else x
            py = 1 - y if d & 2 else y
            pc = 1 - c if d & 1 else c
            cp = pltpu.make_async_remote_copy(
                src_ref=v_ref, dst_ref=buf.at[me], send_sem=send.at[d - 1], recv_sem=recv.at[d - 1],
                device_id=(px, py, pc), device_id_type=_MESH)
            cp.start()
            cps.append((cp, 4 * px + 2 * py + pc))
        for d in range(1, 8):
            cp, peer = cps[d - 1]
            pltpu.make_async_remote_copy(
                src_ref=buf.at[peer], dst_ref=buf.at[peer], send_sem=send.at[d - 1], recv_sem=recv.at[d - 1],
                device_id=(x, y, c), device_id_type=_MESH).wait_recv()
        for cp, _ in cps:
            cp.wait_send()
        mine.wait()
        acc = buf[0]
        for i in range(1, 8):
            acc = acc + buf[i]
        o_ref[...] = acc

    return pl.pallas_call(
        body, in_specs=[pl.BlockSpec(memory_space=pltpu.VMEM)], out_specs=pl.BlockSpec(memory_space=pltpu.VMEM),
        out_shape=jax.ShapeDtypeStruct((R, LANES), F32),
        scratch_shapes=[pltpu.VMEM((8, R, LANES), F32), pltpu.SemaphoreType.DMA((7,)), pltpu.SemaphoreType.DMA((7,)),
                        pltpu.SemaphoreType.DMA],
        compiler_params=pltpu.CompilerParams(vmem_limit_bytes=VMEM_LIMIT), name=name)(v)


def _pack(arrs, row_multiple=8):
    rows = []
    for a in arrs:
        flat = a.reshape(-1)
        flat = jnp.pad(flat, (0, (-flat.shape[0]) % LANES))
        rows.append(flat.reshape(-1, LANES))
    buf = jnp.concatenate(rows, axis=0)
    return jnp.pad(buf, ((0, (-buf.shape[0]) % row_multiple), (0, 0)))


def _unpack(buf, shapes):
    out, r = [], 0
    for s in shapes:
        size = math.prod(s)
        nr = -(-size // LANES)
        out.append(buf[r:r + nr].reshape(-1)[:size].reshape(s))
        r += nr
    return out


def kernel(x, mem, mix_norm, w_in, gdn_conv, gdn_a_log, gdn_dt_bias, gdn_norm, w_out, xattn_norm, mem_norm, w_xq, w_xkv, w_xo, ffn_norm, w_up, ffn_conv, ffn_conv_bias, w_down, final_norm, loss_target, m_mix_norm, m_w_in, m_gdn_conv, m_gdn_a_log, m_gdn_dt_bias, m_gdn_norm, m_w_out, m_xattn_norm, m_mem_norm, m_w_xq, m_w_xkv, m_w_xo, m_ffn_norm, m_w_up, m_ffn_conv, m_ffn_conv_bias, m_w_down, m_final_norm, v_mix_norm, v_w_in, v_gdn_conv, v_gdn_a_log, v_gdn_dt_bias, v_gdn_norm, v_w_out, v_xattn_norm, v_mem_norm, v_w_xq, v_w_xkv, v_w_xo, v_ffn_norm, v_w_up, v_ffn_conv, v_ffn_conv_bias, v_w_down, v_final_norm):
    L = w_in.shape[0]
    _, S, D = x.shape
    nh = D // (2 * HEAD_DIM)
    GW = nh * HEAD_DIM
    n_in = 7 * GW + 2 * nh
    NP = 7 * GW + LANES
    XW = X_HEADS * HEAD_DIM
    F = w_down.shape[1] * 4
    cs_in = w_in.shape[2]
    cs_up = w_up.shape[2]
    cs_xo = w_xo.shape[2]
    tu = _tile(cs_up, 1408)
    per = cs_up // tu
    fper = F // tu
    assert n_in == 4 * cs_in and F % tu == 0 and 2 * F == 4 * cs_up

    xi, yi, ci = lax.axis_index("x"), lax.axis_index("y"), lax.axis_index("c")
    chip = 2 * xi + yi
    cvec = jnp.reshape(ci, (1,)).astype(jnp.int32)

    cs_gc, cs_fc = gdn_conv.shape[2], ffn_conv.shape[2]
    keep = jnp.where(ci == 0, 1.0, 0.0).astype(F32)
    gc_full = lax.dynamic_update_slice(jnp.zeros((L, SHORT_CONV, 4 * cs_gc), F32), gdn_conv * keep, (0, 0, chip * cs_gc))
    fc_full = lax.dynamic_update_slice(jnp.zeros((L, FFN_CONV, 4 * cs_fc), F32), ffn_conv * keep, (0, 0, chip * cs_fc))
    conv_all = _allreduce_small(_pack([gc_full, fc_full]), name="allgather_conv")
    gdn_conv_full, ffn_conv_full = _unpack(conv_all, [gc_full.shape, fc_full.shape])

    big = [w_in, w_out, w_xq, w_xkv, w_xo, w_up, w_down]
    kvec = jnp.reshape(chip, (1,)).astype(jnp.int32)
    ab = jnp.zeros((L, 2, LANES), F32).at[:, 0, nh:2 * nh].set(gdn_a_log).at[:, 1, nh:2 * nh].set(gdn_dt_bias)

    def vec(p, l):
        return p[l:l + 1]

    xo_fwd_b = pl.BlockSpec((None, XW, cs_xo), lambda i, j, k: (j, 0, 0))
    xo_dg_b = pl.BlockSpec((None, XW, cs_xo), lambda i, j, k: (k, 0, 0))
    xo_wg_o = pl.BlockSpec((None, XW, cs_xo), lambda i, j, k: (j, 0, 0))
    up_fwd_b = pl.BlockSpec((None, D, tu), lambda i, j, k: (j // per, 0, j % per))

    def fwd_layer(l, xc, g_in, rest, token):
        g_mix = vec(mix_norm, l) if token is None else vec(mix_norm, l) + token[0:1, 0:1]
        w_in_l = jnp.concatenate([g_in[0], g_in[1], g_in[2], g_in[3], jnp.zeros((D, NP - n_in), BF16)], axis=1)
        s = dict(x0=xc, w_in=w_in_l)
        s["h"] = _rms_fwd(xc, g_mix, name="rms_mix_fwd")
        s["proj"] = _matmul(s["h"], w_in_l, tn=2432, tk=D, name="mm_in_fwd")
        s["sb"], s["tot"] = _sb_fwd(s["proj"], nh, name="sb_fwd")
        gdn_out, s["gdn"] = _gdn_forward(s["proj"], gdn_conv_full[l], ab[l], vec(gdn_norm, l), nh, "")
        g_out, g_xq, g_xkv, g_xo, g_up, g_down = rest(gdn_out)
        w_out_l, w_xq_l, w_xkv_l, w_down_l = g_out.reshape(2 * GW, D), g_xq.reshape(D, XW), g_xkv.reshape(D, 2 * XW), g_down.reshape(F, D)
        s.update(w_out=w_out_l, w_xq=w_xq_l, w_xkv=w_xkv_l, w_xo=g_xo, w_up=g_up, w_down=w_down_l)
        s["mixed"] = jnp.concatenate([s["sb"], gdn_out], axis=1)
        s["x1"] = _matmul(s["mixed"], w_out_l, res=xc, tk=2 * GW, name="mm_out_fwd")
        s["memn"] = _rms_fwd(mem[0], vec(mem_norm, l), name="rms_mem_fwd")
        s["kv"] = _matmul(s["memn"], w_xkv_l, out_dtype=BF16, tk=D, name="mm_xkv_fwd")
        s["hq"] = _rms_fwd(s["x1"], vec(xattn_norm, l), name="rms_xattn_fwd")
        s["q"] = _matmul(s["hq"], w_xq_l, out_dtype=BF16, tk=D, name="mm_xq_fwd")
        s["xo"] = _xattn_fwd(s["q"], s["kv"], name="xattn_fwd")
        s["x2"] = _matmul(s["xo"], g_xo, res=s["x1"], dims=(S, D, XW), tn=cs_xo, tk=XW, b_spec=xo_fwd_b, name="mm_xo_fwd")
        s["hf"] = _rms_fwd(s["x2"], vec(ffn_norm, l), name="rms_ffn_fwd")
        s["u"] = _matmul(s["hf"], g_up, dims=(S, 2 * F, D), tn=tu, tk=D, b_spec=up_fwd_b, name="mm_up_fwd")
        s["act"] = _ffn_act_fwd(s["u"], ffn_conv_full[l], ffn_conv_bias[l:l + 1], name="ffn_act_fwd")
        x3 = _matmul(s["act"], w_down_l, res=s["x2"], tm=1024, tn=1024, tk=tu, name="mm_down_fwd")
        return x3, s

    def bwd_ffn(l, s, dx3, dx3b):
        dact = _matmul(dx3b, s["w_down"], tb=True, tm=1024, tk=D, name="mm_down_dgrad")
        d_down = _matmul(s["act"], dx3b, ta=True, tn=1024, tk=S, name="mm_down_wgrad")
        du3, dcw3, dcb3 = _ffn_act_bwd(s["u"], ffn_conv_full[l], ffn_conv_bias[l:l + 1], dact, name="ffn_act_bwd")
        tq, td = _tile(S, 1024), _tile(D, 1024)
        dhf = _matmul(du3, s["w_up"], tb=True, dims=(S, D, 2 * F), tm=tq, tn=td, tk=tu,
                      a_spec=pl.BlockSpec((None, tq, tu), lambda i, j, k: (k // fper, i, k % fper)),
                      b_spec=pl.BlockSpec((None, td, tu), lambda i, j, k: (k // per, j, k % per)), name="mm_up_dgrad")
        d_up = _matmul(s["hf"], du3, ta=True, dims=(D, 2 * F, S), tn=tu, tk=S,
                       b_spec=pl.BlockSpec((None, S, tu), lambda i, j, k: (j // fper, 0, j % fper)),
                       o_spec=pl.BlockSpec((None, _tile(D, 512), tu), lambda i, j, k: (j // per, i, j % per)),
                       out_shape=jax.ShapeDtypeStruct((4, D, cs_up), F32), name="mm_up_wgrad")
        dx2, dx2b, dg_ffn = _rms_bwd(s["x2"], vec(ffn_norm, l), dhf, dx3, name="rms_bwd")
        small = [dg_ffn, jnp.concatenate([dcw3[0], dcw3[1]], axis=1), jnp.concatenate([dcb3[0], dcb3[1]], axis=1)]
        return dx2, dx2b, {5: d_up, 6: d_down.reshape(4, -1, D)}, small

    def bwd_rest(l, s, dx2, dx2b):
        dxo = _matmul(dx2b, s["w_xo"], tb=True, dims=(S, XW, D), tn=XW, tk=cs_xo, b_spec=xo_dg_b, name="mm_xo_dgrad")
        d_xo = _matmul(s["xo"], dx2b, ta=True, dims=(XW, D, S), tm=XW, tn=cs_xo, tk=S, o_spec=xo_wg_o,
                       out_shape=jax.ShapeDtypeStruct((4, XW, cs_xo), F32), name="mm_xo_wgrad")
        dq, dk, dv = _xattn_bwd(s["q"], s["kv"], dxo, name="xattn_bwd")
        dkv = jnp.concatenate([dk, dv], axis=1)
        dhq = _matmul(dq, s["w_xq"], tb=True, tk=XW, name="mm_xq_dgrad")
        d_xq = _matmul(s["hq"], dq, ta=True, tk=S, name="mm_xq_wgrad")
        dmemn = _matmul(dkv, s["w_xkv"], tb=True, tk=2 * XW, name="mm_xkv_dgrad")
        d_xkv = _matmul(s["memn"], dkv, ta=True, tk=mem.shape[1], name="mm_xkv_wgrad")
        _, _, dg_mem = _rms_bwd(mem[0], vec(mem_norm, l), dmemn, None, name="rms_mem_bwd")
        dx1, dx1b, dg_xattn = _rms_bwd(s["x1"], vec(xattn_norm, l), dhq, dx2, name="rms_bwd")
        dmix = _matmul(dx1b, s["w_out"], tb=True, tk=D, name="mm_out_dgrad")
        d_out = _matmul(s["mixed"], dx1b, ta=True, tk=S, name="mm_out_wgrad")
        dq_s, dk_s, dv_s = _sb_bwd(s["proj"], s["tot"], dmix, nh, name="sb_bwd")
        dx_qkv, dz, dx_g, dconv, dab, dng = _gdn_backward(s["proj"], gdn_conv_full[l], ab[l], vec(gdn_norm, l), s["gdn"], dmix, nh, "")
        dproj = jnp.concatenate([dq_s, dk_s, dv_s, dx_qkv, dz, dx_g], axis=1)
        dh = _matmul(dproj, s["w_in"], tb=True, tm=1024, tn=1024, tk=2432, name="mm_in_dgrad")
        d_in = _matmul(s["h"], dproj, ta=True, tn=2432, tk=S, name="mm_in_wgrad")
        dx0, dx0b, dg_mix = _rms_bwd(s["x0"], vec(mix_norm, l), dh, dx1, name="rms_bwd")
        slabs = {0: jnp.stack([d_in[:, i * cs_in:(i + 1) * cs_in] for i in range(4)]), 1: d_out.reshape(4, -1, D),
                 2: d_xq.reshape(4, -1, XW), 3: d_xkv.reshape(4, -1, 2 * XW), 4: d_xo}
        return dx0, dx0b, slabs, [dg_mix, dconv, dab, dng, dg_xattn, dg_mem]

    def start_gather(l, idxs, after, tag):
        placed = [_cast_place(big[i], l, kvec, name=f"cast_place_{l}") for i in idxs]
        return _gather_start(placed, after, name=f"gather_start_{l}{tag}")

    def end_gather(pending, after, l, tag):
        send, recv, ws, _ = pending
        ws = _gather_wait(send, recv, ws, after, name=f"gather_wait_{l}{tag}")
        return _gather_to_sibling(ws, name=f"gather_to_sibling{tag}")

    xc = x[0]
    saved = []
    first = start_gather(0, [0], conv_all, "a")
    second = start_gather(0, list(range(1, 7)), first[3], "b")
    for l in range(L):
        if l == 0:
            g_in = end_gather(first, xc, 0, "a")[0]
            rest = lambda after: end_gather(second, after, 0, "b")
            order = second[3]
        else:
            wts = end_gather(pending, xc, l, "")
            g_in, rest, order = wts[0], (lambda after, wts=wts: wts[1:]), wts[1]
        token = None
        if l + 1 < L:
            pending = start_gather(l + 1, list(range(7)), order, "")
            token = pending[3]
        xc, s = fwd_layer(l, xc, g_in, rest, token)
        saved.append(s)
    loss_blk, dxc, dxcb, dg_final = _loss_head(xc, final_norm[None, :], loss_target[0], name="loss_head")

    def start_reduce(slabs, l, tag):
        idxs = sorted(slabs)
        gl = [slabs[i] for i in idxs]
        from_sibling = _exchange_sibling(gl, name=f"reduce_to_sibling{tag}")
        partial = [_sum_half(g, r, cvec, name="sum_sibling") for g, r in zip(gl, from_sibling)]
        return idxs, l, tag, _reduce_start(partial, cvec, name=f"reduce_start_{l}{tag}")

    def finish_reduce(item, sums, after):
        idxs, l, tag, (send, recv, ps, lands, _) = item
        ps, lands = _reduce_wait(send, recv, ps, lands, after, name=f"reduce_wait_{l}{tag}")
        for i, p, rb in zip(idxs, ps, lands):
            sums[i] = _sum_chips(p, rb, kvec, sums[i], l, L, name=f"sum_chips_{l}")

    sums = [None] * 7
    small_by_layer = [None] * L
    in_flight = []
    for l in reversed(range(L)):
        if in_flight:
            dxcb = dxcb + in_flight[-1][3][4][0, 0].astype(BF16)
        dx2, dx2b, slabs_ffn, small_ffn = bwd_ffn(l, saved[l], dxc, dxcb)
        for item in in_flight:
            finish_reduce(item, sums, dx2)
        in_flight = [start_reduce(slabs_ffn, l, "f")]
        dx2b = dx2b + in_flight[-1][3][4][0, 0].astype(BF16)
        dxc, dxcb, slabs_rest, small_rest = bwd_rest(l, saved[l], dx2, dx2b)
        saved[l] = None
        in_flight.append(start_reduce(slabs_rest, l, "r"))
        small_by_layer[l] = small_rest + small_ffn

    small_flat = [a for l in range(L) for a in small_by_layer[l]] + [dg_final, loss_blk[0:1]]
    red_buf = _allreduce_small(_pack(small_flat), name="allreduce_small")
    for item in in_flight:
        finish_reduce(item, sums, red_buf)
    from_sib = _swap_with_sibling(sums, name="swap_halves")
    red = _unpack(red_buf, [a.shape for a in small_flat])
    per_layer = [red[9 * l:9 * l + 9] for l in range(L)]
    col = lambda i: jnp.concatenate([p[i] for p in per_layer], axis=0)
    stk = lambda i: jnp.stack([p[i] for p in per_layer])
    g_conv_full, g_ab, g_fconv_full = stk(1), stk(2), stk(7)
    grads_small = dict(
        mix_norm=col(0), gdn_conv=lax.dynamic_slice(g_conv_full, (0, 0, chip * cs_gc), (L, SHORT_CONV, cs_gc)),
        gdn_a_log=g_ab[:, 0, nh:2 * nh], gdn_dt_bias=g_ab[:, 1, nh:2 * nh], gdn_norm=col(3), xattn_norm=col(4),
        mem_norm=col(5), ffn_norm=col(6), ffn_conv=lax.dynamic_slice(g_fconv_full, (0, 0, chip * cs_fc), (L, FFN_CONV, cs_fc)),
        ffn_conv_bias=col(8), final_norm=red[-2][0])
    loss = red[-1][0, 0]

    names_small = ["mix_norm", "gdn_conv", "gdn_a_log", "gdn_dt_bias", "gdn_norm", "xattn_norm", "mem_norm", "ffn_norm",
                   "ffn_conv", "ffn_conv_bias", "final_norm"]
    w_small = dict(mix_norm=mix_norm, gdn_conv=gdn_conv, gdn_a_log=gdn_a_log, gdn_dt_bias=gdn_dt_bias, gdn_norm=gdn_norm,
                   xattn_norm=xattn_norm, mem_norm=mem_norm, ffn_norm=ffn_norm, ffn_conv=ffn_conv, ffn_conv_bias=ffn_conv_bias,
                   final_norm=final_norm)
    m_small = dict(mix_norm=m_mix_norm, gdn_conv=m_gdn_conv, gdn_a_log=m_gdn_a_log, gdn_dt_bias=m_gdn_dt_bias, gdn_norm=m_gdn_norm,
                   xattn_norm=m_xattn_norm, mem_norm=m_mem_norm, ffn_norm=m_ffn_norm, ffn_conv=m_ffn_conv,
                   ffn_conv_bias=m_ffn_conv_bias, final_norm=m_final_norm)
    v_small = dict(mix_norm=v_mix_norm, gdn_conv=v_gdn_conv, gdn_a_log=v_gdn_a_log, gdn_dt_bias=v_gdn_dt_bias, gdn_norm=v_gdn_norm,
                   xattn_norm=v_xattn_norm, mem_norm=v_mem_norm, ffn_norm=v_ffn_norm, ffn_conv=v_ffn_conv,
                   ffn_conv_bias=v_ffn_conv_bias, final_norm=v_final_norm)
    shapes_small = [w_small[n].shape for n in names_small]
    packed = [_pack([d[n] for n in names_small], row_multiple=128)[None] for d in (w_small, grads_small, m_small, v_small)]
    upd_small = [_unpack(o[0], shapes_small) for o in _adamw(*packed, name="adamw_small")]
    delta, new_m, new_v = [dict(zip(names_small, u)) for u in upd_small]
    grads = dict(grads_small)
    big_names = ["w_in", "w_out", "w_xq", "w_xkv", "w_xo", "w_up", "w_down"]
    big_m = [m_w_in, m_w_out, m_w_xq, m_w_xkv, m_w_xo, m_w_up, m_w_down]
    big_v = [v_w_in, v_w_out, v_w_xq, v_w_xkv, v_w_xo, v_w_up, v_w_down]
    for n, w, go, gs, m, v in zip(big_names, big, sums, from_sib, big_m, big_v):
        grads[n], delta[n], new_m[n], new_v[n] = _adamw_halves(w, go, gs, cvec, m, v, name=f"adamw_{n}")

    order = ["mix_norm", "w_in", "gdn_conv", "gdn_a_log", "gdn_dt_bias", "gdn_norm", "w_out", "xattn_norm", "mem_norm", "w_xq",
             "w_xkv", "w_xo", "ffn_norm", "w_up", "ffn_conv", "ffn_conv_bias", "w_down", "final_norm"]
    return (loss, dxc[None], *[grads[n] for n in order], *[delta[n] for n in order], *[new_m[n] for n in order],
            *[new_v[n] for n in order])
```

```python
import functools
import math

import jax
import jax.numpy as jnp
from jax import lax
from jax.experimental import pallas as pl
from jax.experimental.pallas import tpu as pltpu

F32 = jnp.float32
BF16 = jnp.bfloat16

HEAD_DIM = 128
CHUNK = 64
GDN_CPB = 4
SB_TQ, SB_TK = 256, 512
SHORT_CONV = 4
FFN_CONV = 3
X_HEADS = 4
EPS = 1e-6
LANES = 128
VMEM_LIMIT = 56 * 2**20

ADAM_LR, ADAM_B1, ADAM_B2, ADAM_EPS, ADAM_WD, ADAM_STEP = 0.001, 0.9, 0.999, 1e-08, 0.01, 10

HI = lax.Precision.HIGH


def _params(sem):
    return pltpu.CompilerParams(dimension_semantics=sem, vmem_limit_bytes=VMEM_LIMIT)


def _tile(n, want):
    if n <= want:
        return n
    t = (want // LANES) * LANES
    while t > LANES and n % t:
        t -= LANES
    assert n % t == 0, (n, want)
    return t


def _sigmoid(x):
    return jax.nn.sigmoid(x)


def _softplus(x):
    return jnp.maximum(x, 0.0) + jnp.log(1.0 + jnp.exp(-jnp.abs(x)))


def _matmul(a, b, *, name, ta=False, tb=False, out_dtype=F32, res=None, tm=512, tn=512, tk=2048,
            dims=None, a_spec=None, b_spec=None, o_spec=None, out_shape=None):
    if dims is None:
        M, K = (a.shape[1], a.shape[0]) if ta else a.shape
        N = b.shape[0] if tb else b.shape[1]
    else:
        M, N, K = dims
    tm, tn, tk = _tile(M, tm), _tile(N, tn), _tile(K, tk)
    nk = K // tk
    dn = (((0 if ta else 1,), (1 if tb else 0,)), ((), ()))

    def body(*refs):
        a_ref, b_ref = refs[0], refs[1]
        r_ref = refs[2] if res is not None else None
        o_ref = refs[3] if res is not None else refs[2]
        p = lax.dot_general(a_ref[...].astype(BF16), b_ref[...].astype(BF16), dn, preferred_element_type=F32)

        def finish(acc):
            if r_ref is not None:
                acc = acc + r_ref[...].astype(F32)
            o_ref[...] = acc.astype(o_ref.dtype)

        if nk == 1:
            finish(p)
        else:
            acc_ref = refs[-1]
            k = pl.program_id(2)

            @pl.when(k == 0)
            def _():
                acc_ref[...] = p

            @pl.when(jnp.logical_and(k > 0, k < nk - 1))
            def _():
                acc_ref[...] += p

            @pl.when(k == nk - 1)
            def _():
                finish(acc_ref[...] + p)

    if a_spec is None:
        a_spec = pl.BlockSpec((tk, tm), lambda i, j, k: (k, i)) if ta else pl.BlockSpec((tm, tk), lambda i, j, k: (i, k))
    if b_spec is None:
        b_spec = pl.BlockSpec((tn, tk), lambda i, j, k: (j, k)) if tb else pl.BlockSpec((tk, tn), lambda i, j, k: (k, j))
    if o_spec is None:
        o_spec = pl.BlockSpec((tm, tn), lambda i, j, k: (i, j))
    if out_shape is None:
        out_shape = jax.ShapeDtypeStruct((M, N), out_dtype)
    in_specs, args = [a_spec, b_spec], [a, b]
    if res is not None:
        in_specs.append(pl.BlockSpec((tm, tn), lambda i, j, k: (i, j)))
        args.append(res)
    return pl.pallas_call(
        body, grid=(M // tm, N // tn, nk), in_specs=in_specs, out_specs=o_spec, out_shape=out_shape,
        scratch_shapes=[pltpu.VMEM((tm, tn), F32)] if nk > 1 else [],
        compiler_params=_params(("parallel", "parallel", "arbitrary")), name=name)(*args)


def _rms_fwd(x, g, *, name):
    R, D = x.shape
    tr = _tile(R, 256)

    def body(x_ref, g_ref, o_ref):
        xv = x_ref[...]
        rstd = lax.rsqrt(jnp.mean(xv * xv, axis=-1, keepdims=True) + EPS)
        o_ref[...] = (xv * rstd * g_ref[...]).astype(o_ref.dtype)

    return pl.pallas_call(
        body, grid=(R // tr,), in_specs=[pl.BlockSpec((tr, D), lambda i: (i, 0)), pl.BlockSpec((1, D), lambda i: (0, 0))],
        out_specs=pl.BlockSpec((tr, D), lambda i: (i, 0)), out_shape=jax.ShapeDtypeStruct((R, D), BF16),
        compiler_params=_params(("parallel",)), name=name)(x, g)


def _rms_bwd(x, g, dh, dres, *, name):
    R, D = x.shape
    tr = _tile(R, 256)

    def body(*refs):
        if dres is None:
            x_ref, g_ref, dh_ref, dx_ref, dxb_ref, dg_ref = refs
        else:
            x_ref, g_ref, dh_ref, dr_ref, dx_ref, dxb_ref, dg_ref = refs
        xv = x_ref[...]
        dhv = dh_ref[...].astype(F32)
        rstd = lax.rsqrt(jnp.mean(xv * xv, axis=-1, keepdims=True) + EPS)
        xhat = xv * rstd
        t = dhv * g_ref[...]
        dx = rstd * (t - xhat * jnp.mean(t * xhat, axis=-1, keepdims=True))
        if dres is not None:
            dx = dx + dr_ref[...]
        dx_ref[...] = dx
        dxb_ref[...] = dx.astype(BF16)
        part = jnp.sum(dhv * xhat, axis=0, keepdims=True)

        @pl.when(pl.program_id(0) == 0)
        def _():
            dg_ref[...] = part

        @pl.when(pl.program_id(0) > 0)
        def _():
            dg_ref[...] += part

    row = pl.BlockSpec((tr, D), lambda i: (i, 0))
    vec = pl.BlockSpec((1, D), lambda i: (0, 0))
    in_specs = [row, vec, row] + ([row] if dres is not None else [])
    args = [x, g, dh] + ([dres] if dres is not None else [])
    return pl.pallas_call(
        body, grid=(R // tr,), in_specs=in_specs, out_specs=[row, row, vec],
        out_shape=[jax.ShapeDtypeStruct((R, D), F32), jax.ShapeDtypeStruct((R, D), BF16), jax.ShapeDtypeStruct((1, D), F32)],
        compiler_params=_params(("arbitrary",)), name=name)(*args)


def _loss_head(x, g, tgt, *, name):
    R, D = x.shape
    tr = _tile(R, 256)

    def body(x_ref, g_ref, t_ref, l_ref, dx_ref, dxb_ref, dg_ref):
        xv = x_ref[...]
        rstd = lax.rsqrt(jnp.mean(xv * xv, axis=-1, keepdims=True) + EPS)
        xhat = xv * rstd
        err = xhat * g_ref[...] - t_ref[...]
        dy = err * (1.0 / D)
        t = dy * g_ref[...]
        dx = rstd * (t - xhat * jnp.mean(t * xhat, axis=-1, keepdims=True))
        dx_ref[...] = dx
        dxb_ref[...] = dx.astype(BF16)
        part = jnp.sum(dy * xhat, axis=0, keepdims=True)
        lpart = jnp.zeros((8, LANES), F32) + 0.5 * jnp.sum(jnp.mean(err * err, axis=-1, keepdims=True))

        @pl.when(pl.program_id(0) == 0)
        def _():
            dg_ref[...] = part
            l_ref[...] = lpart

        @pl.when(pl.program_id(0) > 0)
        def _():
            dg_ref[...] += part
            l_ref[...] += lpart

    row = pl.BlockSpec((tr, D), lambda i: (i, 0))
    vec = pl.BlockSpec((1, D), lambda i: (0, 0))
    return pl.pallas_call(
        body, grid=(R // tr,), in_specs=[row, vec, row],
        out_specs=[pl.BlockSpec((8, LANES), lambda i: (0, 0)), row, row, vec],
        out_shape=[jax.ShapeDtypeStruct((8, LANES), F32), jax.ShapeDtypeStruct((R, D), F32),
                   jax.ShapeDtypeStruct((R, D), BF16), jax.ShapeDtypeStruct((1, D), F32)],
        compiler_params=_params(("arbitrary",)), name=name)(x, g, tgt)


def _shift_down(x, s):
    if s == 0:
        return x
    row = lax.broadcasted_iota(jnp.int32, x.shape, 0)
    return jnp.where(row >= s, pltpu.roll(x, s, 0), 0.0)


def _shift_up(x, s):
    if s == 0:
        return x
    n = x.shape[0]
    row = lax.broadcasted_iota(jnp.int32, x.shape, 0)
    return jnp.where(row < n - s, pltpu.roll(x, n - s, 0), 0.0)


def _dwconv(x, w):
    k = w.shape[0]
    acc = x * w[k - 1:k, :]
    for i in range(k - 1):
        acc = acc + _shift_down(x, k - 1 - i) * w[i:i + 1, :]
    return acc


def _dwconv_bwd(x, w, dc):
    k = w.shape[0]
    dx = dc * w[k - 1:k, :]
    dws = []
    for i in range(k - 1):
        s = k - 1 - i
        dx = dx + _shift_up(dc, s) * w[i:i + 1, :]
        dws.append(jnp.sum(dc * _shift_down(x, s), axis=0, keepdims=True))
    dws.append(jnp.sum(dc * x, axis=0, keepdims=True))
    return dx, jnp.concatenate(dws, axis=0)


def _ffn_act_fwd(u, cw, cb, *, name):
    S, F2 = u.shape
    F = F2 // 2
    tc = _tile(F, 256)
    nb = F // tc

    def body(ug_ref, uu_ref, wg_ref, wu_ref, bg_ref, bu_ref, o_ref):
        cg = _dwconv(ug_ref[...], wg_ref[...]) + bg_ref[...]
        cu = _dwconv(uu_ref[...], wu_ref[...]) + bu_ref[...]
        o_ref[...] = (cg * _sigmoid(cg) * cu).astype(o_ref.dtype)

    blk = lambda r, off: pl.BlockSpec((r, tc), lambda j: (0, j + off))
    return pl.pallas_call(
        body, grid=(nb,), in_specs=[blk(S, 0), blk(S, nb), blk(FFN_CONV, 0), blk(FFN_CONV, nb), blk(1, 0), blk(1, nb)],
        out_specs=blk(S, 0), out_shape=jax.ShapeDtypeStruct((S, F), BF16),
        compiler_params=_params(("parallel",)), name=name)(u, u, cw, cw, cb, cb)


def _ffn_act_bwd(u, cw, cb, dact, *, name):
    S, F2 = u.shape
    F = F2 // 2
    tc = _tile(F, 256)
    nb = F // tc

    def body(ug_ref, uu_ref, wg_ref, wu_ref, bg_ref, bu_ref, da_ref, du_ref, dw_ref, db_ref):
        ug, uu = ug_ref[...], uu_ref[...]
        cg = _dwconv(ug, wg_ref[...]) + bg_ref[...]
        cu = _dwconv(uu, wu_ref[...]) + bu_ref[...]
        sg = _sigmoid(cg)
        da = da_ref[...].astype(F32)
        dcu = da * (cg * sg)
        dcg = da * cu * (sg * (1.0 + cg * (1.0 - sg)))
        dxg, dwg = _dwconv_bwd(ug, wg_ref[...], dcg)
        dxu, dwu = _dwconv_bwd(uu, wu_ref[...], dcu)
        du_ref[0] = dxg.astype(du_ref.dtype)
        du_ref[1] = dxu.astype(du_ref.dtype)
        dw_ref[0] = dwg
        dw_ref[1] = dwu
        db_ref[0] = jnp.sum(dcg, axis=0, keepdims=True)
        db_ref[1] = jnp.sum(dcu, axis=0, keepdims=True)

    blk = lambda r, off: pl.BlockSpec((r, tc), lambda j: (0, j + off))
    blk3 = lambda r: pl.BlockSpec((2, r, tc), lambda j: (0, 0, j))
    return pl.pallas_call(
        body, grid=(nb,),
        in_specs=[blk(S, 0), blk(S, nb), blk(FFN_CONV, 0), blk(FFN_CONV, nb), blk(1, 0), blk(1, nb), blk(S, 0)],
        out_specs=[blk3(S), blk3(FFN_CONV), blk3(1)],
        out_shape=[jax.ShapeDtypeStruct((2, S, F), BF16), jax.ShapeDtypeStruct((2, FFN_CONV, F), F32),
                   jax.ShapeDtypeStruct((2, 1, F), F32)],
        compiler_params=_params(("parallel",)), name=name)(u, u, cw, cw, cb, cb, dact)


def _xattn_fwd(q, kv, *, name):
    S, XW = q.shape
    M = kv.shape[0]
    nh = XW // HEAD_DIM
    tq = _tile(S, 512)
    scale = HEAD_DIM ** -0.5

    def body(q_ref, k_ref, v_ref, o_ref):
        z = lax.dot_general(q_ref[...], k_ref[...], (((1,), (1,)), ((), ())), preferred_element_type=F32) * scale
        e = jnp.exp(z - jnp.max(z, axis=-1, keepdims=True))
        p = e / jnp.sum(e, axis=-1, keepdims=True)
        o_ref[...] = jnp.dot(p.astype(BF16), v_ref[...], preferred_element_type=F32).astype(o_ref.dtype)

    return pl.pallas_call(
        body, grid=(nh, S // tq),
        in_specs=[pl.BlockSpec((tq, HEAD_DIM), lambda h, i: (i, h)), pl.BlockSpec((M, HEAD_DIM), lambda h, i: (0, h)),
                  pl.BlockSpec((M, HEAD_DIM), lambda h, i: (0, nh + h))],
        out_specs=pl.BlockSpec((tq, HEAD_DIM), lambda h, i: (i, h)), out_shape=jax.ShapeDtypeStruct((S, XW), BF16),
        compiler_params=_params(("parallel", "parallel")), name=name)(q, kv, kv)


def _xattn_bwd(q, kv, do, *, name):
    S, XW = q.shape
    M = kv.shape[0]
    nh = XW // HEAD_DIM
    tq = _tile(S, 512)
    scale = HEAD_DIM ** -0.5
    nt = (((1,), (1,)), ((), ()))
    tn = (((0,), (0,)), ((), ()))

    def body(q_ref, k_ref, v_ref, do_ref, dq_ref, dk_ref, dv_ref):
        qv, kvv, vv = q_ref[...], k_ref[...], v_ref[...]
        dov = do_ref[...].astype(BF16)
        z = lax.dot_general(qv, kvv, nt, preferred_element_type=F32) * scale
        e = jnp.exp(z - jnp.max(z, axis=-1, keepdims=True))
        p = e / jnp.sum(e, axis=-1, keepdims=True)
        dp = lax.dot_general(dov, vv, nt, preferred_element_type=F32)
        ds = (p * (dp - jnp.sum(dp * p, axis=-1, keepdims=True)) * scale).astype(BF16)
        dq_ref[...] = jnp.dot(ds, kvv, preferred_element_type=F32).astype(dq_ref.dtype)
        dk = lax.dot_general(ds, qv, tn, preferred_element_type=F32)
        dv = lax.dot_general(p.astype(BF16), dov, tn, preferred_element_type=F32)

        @pl.when(pl.program_id(1) == 0)
        def _():
            dk_ref[...] = dk
            dv_ref[...] = dv

        @pl.when(pl.program_id(1) > 0)
        def _():
            dk_ref[...] += dk
            dv_ref[...] += dv

    qs = pl.BlockSpec((tq, HEAD_DIM), lambda h, i: (i, h))
    ms = pl.BlockSpec((M, HEAD_DIM), lambda h, i: (0, h))
    return pl.pallas_call(
        body, grid=(nh, S // tq),
        in_specs=[qs, ms, pl.BlockSpec((M, HEAD_DIM), lambda h, i: (0, nh + h)), qs],
        out_specs=[qs, ms, ms],
        out_shape=[jax.ShapeDtypeStruct((S, XW), BF16), jax.ShapeDtypeStruct((M, XW), F32), jax.ShapeDtypeStruct((M, XW), F32)],
        compiler_params=_params(("parallel", "arbitrary")), name=name)(q, kv, kv, do)


_NN = (((1,), (0,)), ((), ()))
_NT = (((1,), (1,)), ((), ()))
_TN = (((0,), (0,)), ((), ()))


def _batched(dn, a):
    if a.ndim == 2:
        return dn
    (ca,), (cb,) = dn[0]
    return (((ca + 1,), (cb + 1,)), ((0,), (0,)))


def _dot(a, b, dn=_NN):
    return lax.dot_general(a.astype(BF16), b.astype(BF16), _batched(dn, a), preferred_element_type=F32)


def _dot_hi(a, b, dn=_NN):
    return lax.dot_general(a, b, _batched(dn, a), preferred_element_type=F32, precision=HI)


def _dot_split(a, b01, dn=_NN):
    hi = a.astype(BF16)
    lo = (a - hi.astype(F32)).astype(BF16)
    return (lax.dot_general(hi, b01, dn, preferred_element_type=F32)
            + lax.dot_general(lo, b01, dn, preferred_element_type=F32))


def _after_matrix(n, transpose=False):
    row = lax.broadcasted_iota(jnp.int32, (n, n), 0)
    col = lax.broadcasted_iota(jnp.int32, (n, n), 1)
    return (row < col if transpose else row > col).astype(BF16)


def _sb_fwd(proj, nh, *, name):
    S = proj.shape[0]
    TQ, TK = min(SB_TQ, S), min(SB_TK, S)
    nq = S // TQ
    scale = HEAD_DIM ** -0.5

    def body(q_ref, k_ref, v_ref, o_ref, tot_ref):
        i = pl.program_id(1)
        q = q_ref[...].astype(BF16)
        qpos = i * TQ + lax.broadcasted_iota(jnp.int32, (TQ, TK), 0)
        kcol = lax.broadcasted_iota(jnp.int32, (TQ, TK), 1)
        after = _after_matrix(TK)
        nt = ((i + 1) * TQ + TK - 1) // TK

        def step(t, carry):
            acc, out = carry
            off = pl.multiple_of((nt - 1 - t) * TK, TK)
            kb = k_ref[pl.ds(off, TK), :].astype(BF16)
            vb = v_ref[pl.ds(off, TK), :].astype(BF16)
            z = lax.dot_general(q, kb, _NT, preferred_element_type=F32) * scale
            valid = kcol + off < qpos
            ls = jnp.where(valid, -_softplus(z), 0.0)
            later = _dot_split(ls, after) + acc
            w = jnp.where(valid, jnp.exp(ls + z + later), 0.0)
            out = out + jnp.dot(w.astype(BF16), vb, preferred_element_type=F32)
            return acc + jnp.sum(ls, axis=1, keepdims=True), out

        acc, out = lax.fori_loop(0, nt, step, (jnp.zeros((TQ, 1), F32), jnp.zeros((TQ, HEAD_DIM), F32)))
        o_ref[...] = out.astype(o_ref.dtype)
        tot_ref[...] = acc

    return pl.pallas_call(
        body, grid=(nh, nq),
        in_specs=[pl.BlockSpec((TQ, HEAD_DIM), lambda h, i: (i, h)),
                  pl.BlockSpec((S, HEAD_DIM), lambda h, i: (0, nh + h)),
                  pl.BlockSpec((S, HEAD_DIM), lambda h, i: (0, 2 * nh + h))],
        out_specs=[pl.BlockSpec((TQ, HEAD_DIM), lambda h, i: (i, h)), pl.BlockSpec((None, TQ, 1), lambda h, i: (h, i, 0))],
        out_shape=[jax.ShapeDtypeStruct((S, nh * HEAD_DIM), BF16), jax.ShapeDtypeStruct((nh, S, 1), F32)],
        compiler_params=_params(("parallel", "parallel")), name=name)(proj, proj, proj)


def _sb_bwd(proj, tot, dmix, nh, *, name):
    S = proj.shape[0]
    TQ, TK = min(SB_TQ, S), min(SB_TK, S)
    nq = S // TQ
    scale = HEAD_DIM ** -0.5

    def body(q_ref, k_ref, v_ref, tot_ref, do_ref, dq_ref, dk_ref, dv_ref, dk_acc, dv_acc):
        i = pl.program_id(1)

        @pl.when(i == 0)
        def _():
            dk_acc[...] = jnp.zeros_like(dk_acc)
            dv_acc[...] = jnp.zeros_like(dv_acc)

        q = q_ref[...].astype(BF16)
        do = do_ref[...].astype(BF16)
        tot = tot_ref[...]
        qpos = i * TQ + lax.broadcasted_iota(jnp.int32, (TQ, TK), 0)
        kcol = lax.broadcasted_iota(jnp.int32, (TQ, TK), 1)
        after = _after_matrix(TK)
        before = _after_matrix(TK, transpose=True)
        nt = ((i + 1) * TQ + TK - 1) // TK

        def step(j, carry):
            pre, g_sum, dq = carry
            off = pl.multiple_of(j * TK, TK)
            kb = k_ref[pl.ds(off, TK), :].astype(BF16)
            vb = v_ref[pl.ds(off, TK), :].astype(BF16)
            z = lax.dot_general(q, kb, _NT, preferred_element_type=F32) * scale
            valid = kcol + off < qpos
            ls = jnp.where(valid, -_softplus(z), 0.0)
            lb = ls + z
            rs = jnp.sum(ls, axis=1, keepdims=True)
            later = _dot_split(ls, after) + (tot - pre - rs)
            w = jnp.where(valid, jnp.exp(lb + later), 0.0)
            g = lax.dot_general(do, vb, _NT, preferred_element_type=F32) * w
            dls = _dot_split(g, before) + g_sum
            sig = jnp.exp(lb)
            dz = (jnp.where(valid, g * (1.0 - sig) - dls * sig, 0.0) * scale).astype(BF16)
            dq = dq + jnp.dot(dz, kb, preferred_element_type=F32)
            dk_acc[pl.ds(off, TK), :] += lax.dot_general(dz, q, _TN, preferred_element_type=F32)
            dv_acc[pl.ds(off, TK), :] += lax.dot_general(w.astype(BF16), do, _TN, preferred_element_type=F32)
            return pre + rs, g_sum + jnp.sum(g, axis=1, keepdims=True), dq

        zero = jnp.zeros((TQ, 1), F32)
        _, _, dq = lax.fori_loop(0, nt, step, (zero, zero, jnp.zeros((TQ, HEAD_DIM), F32)))
        dq_ref[...] = dq.astype(dq_ref.dtype)

        @pl.when(i == nq - 1)
        def _():
            dk_ref[...] = dk_acc[...].astype(dk_ref.dtype)
            dv_ref[...] = dv_acc[...].astype(dv_ref.dtype)

    qs = pl.BlockSpec((TQ, HEAD_DIM), lambda h, i: (i, h))
    full = pl.BlockSpec((S, HEAD_DIM), lambda h, i: (0, h))
    o = jax.ShapeDtypeStruct((S, nh * HEAD_DIM), BF16)
    return pl.pallas_call(
        body, grid=(nh, nq),
        in_specs=[qs, pl.BlockSpec((S, HEAD_DIM), lambda h, i: (0, nh + h)),
                  pl.BlockSpec((S, HEAD_DIM), lambda h, i: (0, 2 * nh + h)),
                  pl.BlockSpec((None, TQ, 1), lambda h, i: (h, i, 0)), qs],
        out_specs=[qs, full, full], out_shape=[o, o, o],
        scratch_shapes=[pltpu.VMEM((S, HEAD_DIM), F32), pltpu.VMEM((S, HEAD_DIM), F32)],
        compiler_params=_params(("parallel", "arbitrary")), name=name)(proj, proj, proj, tot, dmix)


def _gdn_qkv_fwd(proj, conv_w, nh, *, name):
    S = proj.shape[0]
    GW = nh * HEAD_DIM
    scale = HEAD_DIM ** -0.5

    def body(x_ref, w_ref, o_ref):
        sec = pl.program_id(0) // nh
        c = _dwconv(x_ref[...], w_ref[...])
        s = c * _sigmoid(c)
        r = lax.rsqrt(jnp.sum(s * s, axis=1, keepdims=True) + EPS)
        fac = jnp.where(sec == 0, scale, 1.0)
        o_ref[...] = jnp.where(sec == 2, s, s * (r * fac))

    return pl.pallas_call(
        body, grid=(3 * nh,),
        in_specs=[pl.BlockSpec((S, HEAD_DIM), lambda j: (0, 3 * nh + j)), pl.BlockSpec((SHORT_CONV, HEAD_DIM), lambda j: (0, j))],
        out_specs=pl.BlockSpec((None, S, HEAD_DIM), lambda j: (j // nh, 0, j % nh)),
        out_shape=jax.ShapeDtypeStruct((3, S, GW), F32),
        compiler_params=_params(("parallel",)), name=name)(proj, conv_w)


def _gdn_qkv_bwd(proj, conv_w, dqkv, nh, *, name):
    S = proj.shape[0]
    GW = nh * HEAD_DIM
    scale = HEAD_DIM ** -0.5

    def body(x_ref, w_ref, d_ref, dx_ref, dw_ref):
        sec = pl.program_id(0) // nh
        x, w = x_ref[...], w_ref[...]
        c = _dwconv(x, w)
        sg = _sigmoid(c)
        s = c * sg
        r = lax.rsqrt(jnp.sum(s * s, axis=1, keepdims=True) + EPS)
        sh = s * r
        d = d_ref[...]
        fac = jnp.where(sec == 0, scale, 1.0)
        dn = (r * fac) * (d - sh * jnp.sum(d * sh, axis=1, keepdims=True))
        ds = jnp.where(sec == 2, d, dn)
        dx, dw = _dwconv_bwd(x, w, ds * (sg * (1.0 + c * (1.0 - sg))))
        dx_ref[...] = dx.astype(dx_ref.dtype)
        dw_ref[...] = dw

    return pl.pallas_call(
        body, grid=(3 * nh,),
        in_specs=[pl.BlockSpec((S, HEAD_DIM), lambda j: (0, 3 * nh + j)), pl.BlockSpec((SHORT_CONV, HEAD_DIM), lambda j: (0, j)),
                  pl.BlockSpec((None, S, HEAD_DIM), lambda j: (j // nh, 0, j % nh))],
        out_specs=[pl.BlockSpec((S, HEAD_DIM), lambda j: (0, j)), pl.BlockSpec((SHORT_CONV, HEAD_DIM), lambda j: (0, j))],
        out_shape=[jax.ShapeDtypeStruct((S, 3 * GW), BF16), jax.ShapeDtypeStruct((SHORT_CONV, 3 * GW), F32)],
        compiler_params=_params(("parallel",)), name=name)(proj, conv_w, dqkv)


def _gdn_gates_fwd(proj, ab, nh, *, name):
    S = proj.shape[0]
    C = CHUNK

    def body(x_ref, ab_ref, o_ref):
        ri = lax.broadcasted_iota(jnp.int32, (C, C), 0)
        ci = lax.broadcasted_iota(jnp.int32, (C, C), 1)
        ltri = (ri >= ci).astype(F32)
        lane = lax.broadcasted_iota(jnp.int32, (C, LANES), 1)
        a_coef = -jnp.exp(ab_ref[0:1, :])
        dt = ab_ref[1:2, :]

        def chunk(n, _):
            rows = pl.ds(pl.multiple_of(n * C, C), C)
            x = x_ref[rows, :]
            beta = _sigmoid(x)
            g = jnp.where(jnp.logical_and(lane >= nh, lane < 2 * nh), a_coef * _softplus(x + dt), 0.0)
            gc = _dot_hi(ltri, pltpu.roll(g, nh, 1))
            o_ref[rows, :] = jnp.where(lane < nh, beta, g) + gc
            return 0

        lax.fori_loop(0, S // C, chunk, 0)

    return pl.pallas_call(
        body, grid=(1,),
        in_specs=[pl.BlockSpec((S, LANES), lambda i: (0, 7 * nh)), pl.BlockSpec((2, LANES), lambda i: (0, 0))],
        out_specs=pl.BlockSpec((S, LANES), lambda i: (0, 0)), out_shape=jax.ShapeDtypeStruct((S, LANES), F32),
        compiler_params=_params(("arbitrary",)), name=name)(proj, ab)


def _gdn_gates_bwd(proj, ab, dgt, nh, *, name):
    S = proj.shape[0]
    C = CHUNK

    def body(x_ref, ab_ref, d_ref, dx_ref, dab_ref):
        ri = lax.broadcasted_iota(jnp.int32, (C, C), 0)
        ci = lax.broadcasted_iota(jnp.int32, (C, C), 1)
        utri = (ri <= ci).astype(F32)
        lane = lax.broadcasted_iota(jnp.int32, (C, LANES), 1)
        is_b = lane < nh
        is_a = jnp.logical_and(lane >= nh, lane < 2 * nh)
        a_coef = -jnp.exp(ab_ref[0:1, :])
        dt = ab_ref[1:2, :]

        def chunk(n, carry):
            da_log, ddt = carry
            rows = pl.ds(pl.multiple_of(n * C, C), C)
            x = x_ref[rows, :]
            d = d_ref[rows, :]
            beta = _sigmoid(x)
            dg = pltpu.roll(_dot_hi(utri, jnp.where(lane >= 2 * nh, d, 0.0)), LANES - nh, 1)
            dg = jnp.where(is_a, dg, 0.0)
            dxa = dg * a_coef * _sigmoid(x + dt)
            dxb = jnp.where(is_b, d * beta * (1.0 - beta), 0.0)
            dx_ref[rows, :] = (dxa + dxb).astype(dx_ref.dtype)
            da_log = da_log + jnp.sum(dg * a_coef * _softplus(x + dt), axis=0, keepdims=True)
            return da_log, ddt + jnp.sum(dxa, axis=0, keepdims=True)

        zero = jnp.zeros((1, LANES), F32)
        da_log, ddt = lax.fori_loop(0, S // C, chunk, (zero, zero))
        dab_ref[0:1, :] = da_log
        dab_ref[1:2, :] = ddt

    return pl.pallas_call(
        body, grid=(1,),
        in_specs=[pl.BlockSpec((S, LANES), lambda i: (0, 7 * nh)), pl.BlockSpec((2, LANES), lambda i: (0, 0)),
                  pl.BlockSpec((S, LANES), lambda i: (0, 0))],
        out_specs=[pl.BlockSpec((S, LANES), lambda i: (0, 0)), pl.BlockSpec((2, LANES), lambda i: (0, 0))],
        out_shape=[jax.ShapeDtypeStruct((S, LANES), BF16), jax.ShapeDtypeStruct((2, LANES), F32)],
        compiler_params=_params(("arbitrary",)), name=name)(proj, ab, dgt)


def _unit_lower_inverse(lmat):
    C = lmat.shape[-1]
    ri = lax.broadcasted_iota(jnp.int32, lmat.shape, lmat.ndim - 2)
    ci = lax.broadcasted_iota(jnp.int32, lmat.shape, lmat.ndim - 1)
    nmat = -lmat
    p = jnp.where(ri == ci, 1.0, 0.0) + nmat
    for _ in range(int(math.log2(C)) - 1):
        nmat = _dot_hi(nmat, nmat)
        p = p + _dot_hi(p, nmat)
    return p


def _gdn_chunk_common(q, k, v, gates, gc_row, h, nh, tinv=None):
    C = CHUNK
    lane = lax.broadcasted_iota(jnp.int32, gates.shape, gates.ndim - 1)
    beta = jnp.sum(jnp.where(lane == h, gates, 0.0), axis=-1, keepdims=True)
    gc = jnp.sum(jnp.where(lane == 2 * nh + h, gates, 0.0), axis=-1, keepdims=True)
    sq = gates.shape[:-1] + (C,)
    ri = lax.broadcasted_iota(jnp.int32, sq, len(sq) - 2)
    ci = lax.broadcasted_iota(jnp.int32, sq, len(sq) - 1)
    incl, strict = ri >= ci, ri > ci
    decay = jnp.where(incl, jnp.exp(jnp.where(incl, gc - gc_row, 0.0)), 0.0)
    egc = jnp.exp(gc)
    kb, vb = k * beta, v * beta
    lmat = jnp.where(strict, _dot(kb, k, _NT) * decay, 0.0)
    kbg = kb * egc
    u = w = None
    if tinv is None:
        tinv = _unit_lower_inverse(lmat)
        u = _dot(tinv, vb)
        w = _dot(tinv, kbg)
    amat = _dot(q, k, _NT) * decay
    glast = gc[..., C - 1:C, :]
    ekt = jnp.exp(glast - gc)
    return dict(q=q, k=k, v=v, beta=beta, decay=decay, egc=egc, kb=kb, vb=vb, lmat=lmat, tinv=tinv, kbg=kbg, u=u, w=w,
                amat=amat, qd=q * egc, ekt=ekt, kt=k * ekt, cd=jnp.exp(glast), strict=strict, incl=incl)


def _gdn_chunk_specs(nh, S, nc):
    return [pl.BlockSpec((3, S, HEAD_DIM), lambda h: (0, 0, h)),
            pl.BlockSpec((S, LANES), lambda h: (0, 0)),
            pl.BlockSpec((None, nc, 1, CHUNK), lambda h: (h, 0, 0, 0))]


def _gdn_state_free(qkv_ref, gates_ref, gr_ref, g, nb, h, nh):
    C = CHUNK
    rows = pl.ds(pl.multiple_of(g * (nb * C), nb * C), nb * C)
    part = lambda x: x.reshape(nb, C, x.shape[-1])
    return rows, _gdn_chunk_common(part(qkv_ref[0, rows, :]), part(qkv_ref[1, rows, :]), part(qkv_ref[2, rows, :]),
                                   part(gates_ref[rows, :]), gr_ref[pl.ds(g * nb, nb)], h, nh)


def _gdn_chunk_fwd(qkv, gates, gc_row, *, name):
    _, S, GW = qkv.shape
    nh, C = GW // HEAD_DIM, CHUNK
    nc = S // C
    nb = min(GDN_CPB, nc)
    flat = lambda x: x.reshape(nb * C, x.shape[-1])

    def body(qkv_ref, gates_ref, gr_ref, o_ref, st_ref, u_s, w_s, a_s, qd_s, kt_s, cd_s):
        h = pl.program_id(0)

        def group(g, _):
            rows, m = _gdn_state_free(qkv_ref, gates_ref, gr_ref, g, nb, h, nh)
            u_s[rows, :] = flat(m["u"])
            w_s[rows, :] = flat(m["w"])
            a_s[rows, :] = flat(m["amat"])
            qd_s[rows, :] = flat(m["qd"])
            kt_s[rows, :] = flat(m["kt"])
            cd_s[pl.ds(g * nb, nb)] = jnp.broadcast_to(m["cd"], (nb, 8, LANES))
            return 0

        lax.fori_loop(0, nc // nb, group, 0)

        def chunk(n, s0):
            rows = pl.ds(pl.multiple_of(n * C, C), C)
            st_ref[n] = s0
            v_new = u_s[rows, :] - _dot(w_s[rows, :], s0)
            o_ref[rows, :] = _dot(qd_s[rows, :], s0) + _dot(a_s[rows, :], v_new)
            return s0 * cd_s[n][0:1, :] + _dot(kt_s[rows, :], v_new, _TN)

        lax.fori_loop(0, nc, chunk, jnp.zeros((HEAD_DIM, HEAD_DIM), F32))

    seq = pltpu.VMEM((S, HEAD_DIM), F32)
    return pl.pallas_call(
        body, grid=(nh,), in_specs=_gdn_chunk_specs(nh, S, nc),
        out_specs=[pl.BlockSpec((S, HEAD_DIM), lambda h: (0, h)),
                   pl.BlockSpec((None, nc, HEAD_DIM, HEAD_DIM), lambda h: (h, 0, 0, 0))],
        out_shape=[jax.ShapeDtypeStruct((S, GW), F32), jax.ShapeDtypeStruct((nh, nc, HEAD_DIM, HEAD_DIM), F32)],
        scratch_shapes=[seq, seq, pltpu.VMEM((S, C), F32), seq, seq, pltpu.VMEM((nc, 8, LANES), F32)],
        compiler_params=_params(("parallel",)), name=name)(qkv, gates, gc_row)


def _gdn_chunk_bwd(qkv, gates, gc_row, states, do, *, name):
    _, S, GW = qkv.shape
    nh, C = GW // HEAD_DIM, CHUNK
    nc = S // C
    nb = min(GDN_CPB, nc)
    flat = lambda x: x.reshape(nb * C, x.shape[-1])
    part = lambda x: x.reshape(nb, C, x.shape[-1])

    def body(qkv_ref, gates_ref, gr_ref, st_ref, do_ref, dqkv_ref, dgt_ref,
             t_s, vn_s, w_s, a_s, qd_s, kt_s, cd_s, dvn_s, dkt_s, dcd_s):
        h = pl.program_id(0)

        def group(g, _):
            rows, m = _gdn_state_free(qkv_ref, gates_ref, gr_ref, g, nb, h, nh)
            t_s[rows, :] = flat(m["tinv"])
            vn_s[rows, :] = flat(m["u"])
            w_s[rows, :] = flat(m["w"])
            a_s[rows, :] = flat(m["amat"])
            qd_s[rows, :] = flat(m["qd"])
            kt_s[rows, :] = flat(m["kt"])
            cd_s[pl.ds(g * nb, nb)] = jnp.broadcast_to(m["cd"], (nb, 8, LANES))
            return 0

        lax.fori_loop(0, nc // nb, group, 0)

        def chunk(t, dsn):
            n = nc - 1 - t
            rows = pl.ds(pl.multiple_of(n * C, C), C)
            s0, dout, w = st_ref[n], do_ref[rows, :], w_s[rows, :]
            v_new = vn_s[rows, :] - _dot(w, s0)
            dvn = _dot(a_s[rows, :], dout, _TN) + _dot(kt_s[rows, :], dsn)
            vn_s[rows, :] = v_new
            dvn_s[rows, :] = dvn
            dkt_s[rows, :] = _dot(v_new, dsn, _NT)
            dcd_s[n] = jnp.zeros((8, LANES), F32) + jnp.sum(dsn * s0)
            return _dot(qd_s[rows, :], dout, _TN) + dsn * cd_s[n][0:1, :] - _dot(w, dvn, _TN)

        lax.fori_loop(0, nc, chunk, jnp.zeros((HEAD_DIM, HEAD_DIM), F32))

        def rest(g, _):
            rows, m = _gdn_state_free_again(qkv_ref, gates_ref, gr_ref, t_s, g, nb, h, nh)
            chunks = pl.ds(g * nb, nb)
            dq, dk, dv, dgt = _gdn_chunk_grad(m, st_ref[chunks], part(vn_s[rows, :]), part(dvn_s[rows, :]),
                                              part(dkt_s[rows, :]), dcd_s[chunks][:, 0:1, 0:1], part(do_ref[rows, :]), h, nh)
            dqkv_ref[0, rows, :] = flat(dq)
            dqkv_ref[1, rows, :] = flat(dk)
            dqkv_ref[2, rows, :] = flat(dv)
            dgt_ref[rows, :] = flat(dgt)
            return 0

        lax.fori_loop(0, nc // nb, rest, 0)

    seq = pltpu.VMEM((S, HEAD_DIM), F32)
    small = pltpu.VMEM((nc, 8, LANES), F32)
    return pl.pallas_call(
        body, grid=(nh,),
        in_specs=_gdn_chunk_specs(nh, S, nc) + [
            pl.BlockSpec((None, nc, HEAD_DIM, HEAD_DIM), lambda h: (h, 0, 0, 0)),
            pl.BlockSpec((S, HEAD_DIM), lambda h: (0, h))],
        out_specs=[pl.BlockSpec((3, S, HEAD_DIM), lambda h: (0, 0, h)), pl.BlockSpec((None, S, LANES), lambda h: (h, 0, 0))],
        out_shape=[jax.ShapeDtypeStruct((3, S, GW), F32), jax.ShapeDtypeStruct((nh, S, LANES), F32)],
        scratch_shapes=[pltpu.VMEM((S, C), F32), seq, seq, pltpu.VMEM((S, C), F32), seq, seq, small, seq, seq, small],
        compiler_params=_params(("parallel",)), name=name)(qkv, gates, gc_row, states, do)


def _gdn_state_free_again(qkv_ref, gates_ref, gr_ref, t_s, g, nb, h, nh):
    C = CHUNK
    rows = pl.ds(pl.multiple_of(g * (nb * C), nb * C), nb * C)
    part = lambda x: x.reshape(nb, C, x.shape[-1])
    m = _gdn_chunk_common(part(qkv_ref[0, rows, :]), part(qkv_ref[1, rows, :]), part(qkv_ref[2, rows, :]),
                          part(gates_ref[rows, :]), gr_ref[pl.ds(g * nb, nb)], h, nh, tinv=part(t_s[rows, :]))
    return rows, m


def _gdn_chunk_grad(m, s0, v_new, dvn, dkt, dcd, dout, h, nh):
    C = CHUNK
    q, k, v, beta, decay, egc = m["q"], m["k"], m["v"], m["beta"], m["decay"], m["egc"]
    tinv, kt, cd = m["tinv"], m["kt"], m["cd"]
    dqd = _dot(dout, s0, _NT)
    damat = jnp.where(m["incl"], _dot(dout, v_new, _NT), 0.0)
    dw = -_dot(dvn, s0, _NT)
    dvb = _dot(tinv, dvn, _TN)
    dkbg = _dot(tinv, dw, _TN)
    dtinv = _dot(dvn, m["vb"], _NT) + _dot(dw, m["kbg"], _NT)
    dl = jnp.where(m["strict"], -_dot_hi(_dot_hi(tinv, dtinv, _TN), tinv, _NT), 0.0)
    dkk = dl * decay
    dqk = damat * decay
    dkb = _dot(dkk, k) + dkbg * egc
    dk = _dot(dkk, m["kb"], _TN) + _dot(dqk, q, _TN) + dkt * m["ekt"] + dkb * beta
    dq = _dot(dqk, k) + dqd * egc
    mm = dl * m["lmat"] + damat * m["amat"]
    ones = jnp.ones(q.shape, F32)
    rk = jnp.sum(dkt * kt, axis=-1, keepdims=True)
    dgc = (_dot_hi(mm, ones) - _dot_hi(mm, ones, _TN) + jnp.sum(dqd * m["qd"], axis=-1, keepdims=True) - rk
           + jnp.sum(dkbg * m["kbg"], axis=-1, keepdims=True))
    dglast = jnp.sum(rk, axis=-2, keepdims=True) + dcd * cd
    rowi = lax.broadcasted_iota(jnp.int32, q.shape, q.ndim - 2)
    lane = lax.broadcasted_iota(jnp.int32, q.shape, q.ndim - 1)
    dgc = dgc + jnp.where(rowi == C - 1, dglast, 0.0)
    dbeta = jnp.sum(dkb * k, axis=-1, keepdims=True) + jnp.sum(dvb * v, axis=-1, keepdims=True)
    dgt = jnp.where(lane == h, dbeta, 0.0) + jnp.where(lane == 2 * nh + h, dgc, 0.0)
    return dq, dk, dvb * beta, dgt


def _gdn_post_fwd(o, proj, ng, nh, *, name):
    S, GW = o.shape

    def body(o_ref, z_ref, g_ref, y_ref):
        ov, z = o_ref[...], z_ref[...]
        rstd = lax.rsqrt(jnp.mean(ov * ov, axis=-1, keepdims=True) + EPS)
        y_ref[...] = (ov * rstd * g_ref[...] * (z * _sigmoid(z))).astype(y_ref.dtype)

    blk = pl.BlockSpec((S, HEAD_DIM), lambda h: (0, h))
    return pl.pallas_call(
        body, grid=(nh,), in_specs=[blk, pl.BlockSpec((S, HEAD_DIM), lambda h: (0, 6 * nh + h)), pl.BlockSpec((1, HEAD_DIM), lambda h: (0, 0))],
        out_specs=blk, out_shape=jax.ShapeDtypeStruct((S, GW), BF16),
        compiler_params=_params(("parallel",)), name=name)(o, proj, ng)


def _gdn_post_bwd(o, proj, ng, dmix, nh, *, name):
    S, GW = o.shape

    def body(o_ref, z_ref, g_ref, d_ref, do_ref, dz_ref, dg_ref):
        ov, z, d = o_ref[...], z_ref[...], d_ref[...].astype(F32)
        rstd = lax.rsqrt(jnp.mean(ov * ov, axis=-1, keepdims=True) + EPS)
        oh = ov * rstd
        sz = _sigmoid(z)
        dy = d * (z * sz)
        dz_ref[...] = (d * (oh * g_ref[...]) * (sz * (1.0 + z * (1.0 - sz)))).astype(dz_ref.dtype)
        t = dy * g_ref[...]
        do_ref[...] = rstd * (t - oh * jnp.mean(t * oh, axis=-1, keepdims=True))
        part = jnp.sum(dy * oh, axis=0, keepdims=True)

        @pl.when(pl.program_id(0) == 0)
        def _():
            dg_ref[...] = part

        @pl.when(pl.program_id(0) > 0)
        def _():
            dg_ref[...] += part

    blk = pl.BlockSpec((S, HEAD_DIM), lambda h: (0, h))
    vec = pl.BlockSpec((1, HEAD_DIM), lambda h: (0, 0))
    return pl.pallas_call(
        body, grid=(nh,),
        in_specs=[blk, pl.BlockSpec((S, HEAD_DIM), lambda h: (0, 6 * nh + h)), vec, pl.BlockSpec((S, HEAD_DIM), lambda h: (0, nh + h))],
        out_specs=[blk, blk, vec],
        out_shape=[jax.ShapeDtypeStruct((S, GW), F32), jax.ShapeDtypeStruct((S, GW), BF16), jax.ShapeDtypeStruct((1, HEAD_DIM), F32)],
        compiler_params=_params(("arbitrary",)), name=name)(o, proj, ng, dmix)


def _gdn_forward(proj, conv_w, ab, ng, nh, tag):
    S = proj.shape[0]
    nc = S // CHUNK
    qkv = _gdn_qkv_fwd(proj, conv_w, nh, name=f"gdn_qkv_fwd{tag}")
    gates = _gdn_gates_fwd(proj, ab, nh, name=f"gdn_gates_fwd{tag}")
    gc_row = gates[:, 2 * nh:3 * nh].T.reshape(nh, nc, 1, CHUNK)
    o, states = _gdn_chunk_fwd(qkv, gates, gc_row, name=f"gdn_chunk_fwd{tag}")
    y = _gdn_post_fwd(o, proj, ng, nh, name=f"gdn_post_fwd{tag}")
    return y, (qkv, gates, gc_row, states, o)


def _gdn_backward(proj, conv_w, ab, ng, saved, dmix, nh, tag):
    qkv, gates, gc_row, states, o = saved
    do, dz, dng = _gdn_post_bwd(o, proj, ng, dmix, nh, name=f"gdn_post_bwd{tag}")
    dqkv, dgt_heads = _gdn_chunk_bwd(qkv, gates, gc_row, states, do, name=f"gdn_chunk_bwd{tag}")
    dx_qkv, dconv = _gdn_qkv_bwd(proj, conv_w, dqkv, nh, name=f"gdn_qkv_bwd{tag}")
    dx_g, dab = _gdn_gates_bwd(proj, ab, jnp.sum(dgt_heads, axis=0), nh, name=f"gdn_gates_bwd{tag}")
    return dx_qkv, dz, dx_g, dconv, dab, dng


def _row_tile(r, cap=128):
    t = cap
    while r % t:
        t //= 2
    assert t >= 8, r
    return t


def _adamw_update(gv, w_ref, m_ref, v_ref, d_ref, m2_ref, v2_ref):
    m2 = ADAM_B1 * m_ref[...] + (1.0 - ADAM_B1) * gv
    v2 = ADAM_B2 * v_ref[...] + (1.0 - ADAM_B2) * (gv * gv)
    m_hat = m2 / (1.0 - ADAM_B1 ** ADAM_STEP)
    v_hat = v2 / (1.0 - ADAM_B2 ** ADAM_STEP)
    d_ref[...] = -ADAM_LR * (m_hat / (jnp.sqrt(v_hat) + ADAM_EPS) + ADAM_WD * w_ref[...])
    m2_ref[...] = m2
    v2_ref[...] = v2


def _adamw(w, g, m, v, *, name):
    L, r, c = w.shape
    tr = _row_tile(r)

    def body(w_ref, g_ref, m_ref, v_ref, d_ref, m2_ref, v2_ref):
        _adamw_update(g_ref[...], w_ref, m_ref, v_ref, d_ref, m2_ref, v2_ref)

    blk = pl.BlockSpec((None, tr, c), lambda l, i: (l, i, 0))
    o = jax.ShapeDtypeStruct(w.shape, F32)
    return pl.pallas_call(
        body, grid=(L, r // tr), in_specs=[blk] * 4, out_specs=[blk] * 3, out_shape=[o, o, o],
        compiler_params=_params(("parallel", "parallel")), name=name)(w, g, m, v)


def _adamw_halves(w, g_own, g_sib, cvec, m, v, *, name):
    L, r, c = w.shape
    tr = _row_tile(r // 2)
    nbh = (r // 2) // tr

    def body(c_ref, w_ref, go_ref, gs_ref, m_ref, v_ref, g_out, d_ref, m2_ref, v2_ref):
        gv = jnp.where(pl.program_id(1) // nbh == c_ref[0], go_ref[...], gs_ref[...])
        g_out[...] = gv
        _adamw_update(gv, w_ref, m_ref, v_ref, d_ref, m2_ref, v2_ref)

    lo = lambda i: jnp.minimum(i, nbh - 1)
    hi = lambda i: jnp.maximum(i - nbh, 0)
    blk = pl.BlockSpec((None, tr, c), lambda l, i, c_ref: (l, i, 0))
    own = pl.BlockSpec((None, tr, c), lambda l, i, c_ref: (l, jnp.where(c_ref[0] == 0, lo(i), hi(i)), 0))
    sib = pl.BlockSpec((None, tr, c), lambda l, i, c_ref: (l, jnp.where(c_ref[0] == 0, hi(i), lo(i)), 0))
    o = jax.ShapeDtypeStruct(w.shape, F32)
    return pl.pallas_call(
        body,
        grid_spec=pltpu.PrefetchScalarGridSpec(
            num_scalar_prefetch=1, grid=(L, r // tr), in_specs=[blk, own, sib, blk, blk], out_specs=[blk] * 4),
        out_shape=[o, o, o, o],
        compiler_params=_params(("parallel", "arbitrary")), name=name)(cvec, w, g_own, g_sib, m, v)


def _sum_partials(p, rb, kc, prev, l, nl, *, name):
    _, _, h, c = rb.shape
    tr = _row_tile(h, 256)
    nb = h // tr
    others = [(0, 1), (1, 0), (1, 1), (2, 0), (2, 1), (3, 0), (3, 1)]

    def body(kc_ref, p_ref, *rest):
        acc = p_ref[...].astype(F32)
        for r_ref in rest[:7]:
            acc = acc + r_ref[...].astype(F32)
        rest[-1][...] = acc

    def slot(ds, dc):
        return pl.BlockSpec((None, None, tr, c), lambda i, kc_ref: ((kc_ref[0] + ds) % 4, (kc_ref[1] + dc) % 2, i, 0))

    in_specs = [pl.BlockSpec((None, tr, c), lambda i, kc_ref: (kc_ref[0], kc_ref[1] * nb + i, 0))]
    in_specs += [slot(ds, dc) for ds, dc in others]
    args = [kc, p] + [rb] * 7
    if prev is not None:
        in_specs.append(pl.BlockSpec(memory_space=pltpu.HBM))
        args.append(prev)
    return pl.pallas_call(
        body,
        grid_spec=pltpu.PrefetchScalarGridSpec(
            num_scalar_prefetch=1, grid=(nb,), in_specs=in_specs,
            out_specs=pl.BlockSpec((None, tr, c), lambda i, kc_ref: (l, i, 0))),
        out_shape=jax.ShapeDtypeStruct((nl, h, c), F32), input_output_aliases={9: 0} if prev is not None else {},
        compiler_params=_params(("parallel",)), name=name)(*args)


_MESH = pl.DeviceIdType.MESH
_HBM = pl.BlockSpec(memory_space=pltpu.HBM)


def _place():
    x, y, c = lax.axis_index("x"), lax.axis_index("y"), lax.axis_index("c")
    return x, y, c, [(1 - x, y), (x, 1 - y), (1 - x, 1 - y)]


def _cast_place(w, l, kvec, *, name):
    _, r, c = w.shape
    tr = _row_tile(r, 256)

    def body(k_ref, w_ref, o_ref):
        o_ref[...] = w_ref[...].astype(o_ref.dtype)

    return pl.pallas_call(
        body,
        grid_spec=pltpu.PrefetchScalarGridSpec(
            num_scalar_prefetch=1, grid=(r // tr,),
            in_specs=[pl.BlockSpec((None, tr, c), lambda i, k_ref: (l, i, 0))],
            out_specs=pl.BlockSpec((None, tr, c), lambda i, k_ref: (k_ref[0], i, 0))),
        out_shape=jax.ShapeDtypeStruct((4, r, c), BF16),
        compiler_params=_params(("parallel",)), name=name)(kvec, w)


_SEM = pl.BlockSpec(memory_space=pltpu.SEMAPHORE)
_ANY = pl.BlockSpec(memory_space=pl.ANY)
_EFFECT = pltpu.SideEffectType.DATAFLOW_SIDE_EFFECTING


def _in_hbm(a):
    return pltpu.with_memory_space_constraint(a, pltpu.HBM)


def _gather_copies(w_refs, send, recv, landing):
    x, y, c, chips = _place()
    k = 2 * x + y
    cps = []
    for a, w in enumerate(w_refs):
        h = w.shape[1] // 2
        for j, (cx, cy) in enumerate(chips):
            cps.append(pltpu.make_async_remote_copy(
                src_ref=w.at[k, pl.ds(c * h, h), :], dst_ref=w.at[2 * cx + cy if landing else k, pl.ds(c * h, h), :],
                send_sem=send.at[3 * a + j], recv_sem=recv.at[3 * a + j], device_id=(cx, cy, c), device_id_type=_MESH))
    return cps


def _gather_start(ws, after, *, name):
    n = len(ws)

    def body(*refs):
        send, recv = refs[n + 1], refs[n + 2]
        o_refs, token = refs[n + 3:2 * n + 3], refs[2 * n + 3]
        for cp in _gather_copies(o_refs, send, recv, landing=False):
            cp.start()
        token[...] = jnp.zeros_like(token)

    out = pl.pallas_call(
        body, in_specs=[_HBM] * n + [_ANY], out_specs=[_SEM, _SEM] + [_HBM] * n + [pl.BlockSpec(memory_space=pltpu.VMEM)],
        out_shape=[pltpu.SemaphoreType.DMA((3 * n,)), pltpu.SemaphoreType.DMA((3 * n,))]
        + [pltpu.HBM(w.shape, w.dtype) for w in ws] + [jax.ShapeDtypeStruct((8, LANES), F32)],
        input_output_aliases={a: 2 + a for a in range(n)},
        compiler_params=pltpu.CompilerParams(has_side_effects=_EFFECT), name=name)(*[_in_hbm(w) for w in ws], after)
    return out[0], out[1], list(out[2:2 + n]), out[2 + n]


def _gather_wait(send, recv, ws, after, *, name):
    n = len(ws)

    def body(*refs):
        for cp in _gather_copies(refs[:n], refs[n], refs[n + 1], landing=True):
            cp.wait_send()
            cp.wait_recv()

    return list(pl.pallas_call(
        body, in_specs=[_HBM] * n + [_SEM, _SEM, _ANY], out_specs=[_HBM] * n,
        out_shape=[pltpu.HBM(w.shape, w.dtype) for w in ws], input_output_aliases={a: a for a in range(n)},
        compiler_params=pltpu.CompilerParams(has_side_effects=_EFFECT), name=name)(*ws, send, recv, after))


def _gather_to_sibling(ws, *, name):
    n = len(ws)

    def body(*refs):
        o_refs = refs[n:2 * n]
        send, recv = refs[2 * n:]
        x, y, c, chips = _place()
        cps = []
        for a in range(n):
            h = o_refs[a].shape[1] // 2
            for j, (cx, cy) in enumerate(chips):
                landed = o_refs[a].at[2 * cx + cy, pl.ds(c * h, h), :]
                cp = pltpu.make_async_remote_copy(
                    src_ref=landed, dst_ref=landed, send_sem=send.at[3 * a + j], recv_sem=recv.at[3 * a + j],
                    device_id=(x, y, 1 - c), device_id_type=_MESH)
                cp.start()
                cps.append(cp)
        for a in range(n):
            h = o_refs[a].shape[1] // 2
            for j, (cx, cy) in enumerate(chips):
                other = o_refs[a].at[2 * cx + cy, pl.ds((1 - c) * h, h), :]
                pltpu.make_async_remote_copy(
                    src_ref=other, dst_ref=other, send_sem=send.at[3 * a + j], recv_sem=recv.at[3 * a + j],
                    device_id=(x, y, c), device_id_type=_MESH).wait_recv()
        for cp in cps:
            cp.wait_send()

    return list(pl.pallas_call(
        body, in_specs=[_HBM] * n, out_specs=[_HBM] * n,
        out_shape=[jax.ShapeDtypeStruct(w.shape, w.dtype) for w in ws],
        input_output_aliases={a: a for a in range(n)},
        scratch_shapes=[pltpu.SemaphoreType.DMA((3 * n,))] * 2, name=name)(*ws))


def _reduce_copies(p_refs, r_refs, send, recv, landing):
    x, y, c = lax.axis_index("x"), lax.axis_index("y"), lax.axis_index("c")
    k = 2 * x + y
    cps = []
    for a, (p, r) in enumerate(zip(p_refs, r_refs)):
        h = p.shape[1] // 2
        for d in range(1, 8):
            px = 1 - x if d & 4 else x
            py = 1 - y if d & 2 else y
            pc = 1 - c if d & 1 else c
            cps.append(pltpu.make_async_remote_copy(
                src_ref=p.at[2 * px + py, pl.ds(pc * h, h), :], dst_ref=r.at[2 * px + py, pc] if landing else r.at[k, c],
                send_sem=send.at[7 * a + d - 1], recv_sem=recv.at[7 * a + d - 1], device_id=(px, py, pc),
                device_id_type=_MESH))
    return cps


def _reduce_start(ps, after, *, name):
    n = len(ps)
    lands = [lax.empty((4, 2, p.shape[1] // 2, p.shape[2]), p.dtype) for p in ps]

    def body(*refs):
        send, recv = refs[2 * n + 1], refs[2 * n + 2]
        p_out, r_out, token = refs[2 * n + 3:3 * n + 3], refs[3 * n + 3:4 * n + 3], refs[4 * n + 3]
        for cp in _reduce_copies(p_out, r_out, send, recv, landing=False):
            cp.start()
        token[...] = jnp.zeros_like(token)

    out = pl.pallas_call(
        body, in_specs=[_HBM] * (2 * n) + [_ANY],
        out_specs=[_SEM, _SEM] + [_HBM] * (2 * n) + [pl.BlockSpec(memory_space=pltpu.VMEM)],
        out_shape=[pltpu.SemaphoreType.DMA((7 * n,)), pltpu.SemaphoreType.DMA((7 * n,))]
        + [pltpu.HBM(a.shape, a.dtype) for a in list(ps) + lands] + [jax.ShapeDtypeStruct((8, LANES), F32)],
        input_output_aliases={a: 2 + a for a in range(2 * n)},
        compiler_params=pltpu.CompilerParams(has_side_effects=_EFFECT), name=name)(
            *[_in_hbm(p) for p in ps], *[_in_hbm(r) for r in lands], after)
    return out[0], out[1], list(out[2:2 + n]), list(out[2 + n:2 + 2 * n]), out[2 + 2 * n]


def _reduce_wait(send, recv, ps, lands, after, *, name):
    n = len(ps)

    def body(*refs):
        for cp in _reduce_copies(refs[:n], refs[n:2 * n], refs[2 * n], refs[2 * n + 1], landing=True):
            cp.wait_send()
            cp.wait_recv()

    out = pl.pallas_call(
        body, in_specs=[_HBM] * (2 * n) + [_SEM, _SEM, _ANY], out_specs=[_HBM] * (2 * n),
        out_shape=[pltpu.HBM(a.shape, a.dtype) for a in list(ps) + list(lands)],
        input_output_aliases={a: a for a in range(2 * n)},
        compiler_params=pltpu.CompilerParams(has_side_effects=_EFFECT), name=name)(*ps, *lands, send, recv, after)
    return list(out[:n]), list(out[n:])


def _swap_with_sibling(gs, *, name):
    n = len(gs)

    def body(*refs):
        g_refs, o_refs = refs[:n], refs[n:2 * n]
        send, recv = refs[2 * n:]
        x, y, c, _ = _place()
        cps = []
        for a in range(n):
            cp = pltpu.make_async_remote_copy(
                src_ref=g_refs[a], dst_ref=o_refs[a], send_sem=send.at[a], recv_sem=recv.at[a],
                device_id=(x, y, 1 - c), device_id_type=_MESH)
            cp.start()
            cps.append(cp)
        for cp in cps:
            cp.wait_recv()
        for cp in cps:
            cp.wait_send()

    return pl.pallas_call(
        body, in_specs=[_HBM] * n, out_specs=[_HBM] * n,
        out_shape=[jax.ShapeDtypeStruct(g.shape, g.dtype) for g in gs],
        scratch_shapes=[pltpu.SemaphoreType.DMA((n,))] * 2, name=name)(*gs)


def _allreduce_small(v, *, name):
    R = v.shape[0]

    def body(v_ref, o_ref, buf, send, recv, loc):
        x, y, c = lax.axis_index("x"), lax.axis_index("y"), lax.axis_index("c")
        me = 4 * x + 2 * y + c
        mine = pltpu.make_async_copy(v_ref, buf.at[me], loc)
        mine.start()
        cps = []
        for d in range(1, 8):
            px = 1 - x if d & 4 else x
            py = 1 - y if d & 2 else y
            pc = 1 - c if d & 1 else c
            cp = pltpu.make_async_remote_copy(
                src_ref=v_ref, dst_ref=buf.at[me], send_sem=send.at[d - 1], recv_sem=recv.at[d - 1],
                device_id=(px, py, pc), device_id_type=_MESH)
            cp.start()
            cps.append((cp, 4 * px + 2 * py + pc))
        for d in range(1, 8):
            cp, peer = cps[d - 1]
            pltpu.make_async_remote_copy(
                src_ref=buf.at[peer], dst_ref=buf.at[peer], send_sem=send.at[d - 1], recv_sem=recv.at[d - 1],
                device_id=(x, y, c), device_id_type=_MESH).wait_recv()
        for cp, _ in cps:
            cp.wait_send()
        mine.wait()
        acc = buf[0]
        for i in range(1, 8):
            acc = acc + buf[i]
        o_ref[...] = acc

    return pl.pallas_call(
        body, in_specs=[pl.BlockSpec(memory_space=pltpu.VMEM)], out_specs=pl.BlockSpec(memory_space=pltpu.VMEM),
        out_shape=jax.ShapeDtypeStruct((R, LANES), F32),
        scratch_shapes=[pltpu.VMEM((8, R, LANES), F32), pltpu.SemaphoreType.DMA((7,)), pltpu.SemaphoreType.DMA((7,)),
                        pltpu.SemaphoreType.DMA],
        compiler_params=pltpu.CompilerParams(vmem_limit_bytes=VMEM_LIMIT), name=name)(v)


def _pack(arrs, row_multiple=8):
    rows = []
    for a in arrs:
        flat = a.reshape(-1)
        flat = jnp.pad(flat, (0, (-flat.shape[0]) % LANES))
        rows.append(flat.reshape(-1, LANES))
    buf = jnp.concatenate(rows, axis=0)
    return jnp.pad(buf, ((0, (-buf.shape[0]) % row_multiple), (0, 0)))


def _unpack(buf, shapes):
    out, r = [], 0
    for s in shapes:
        size = math.prod(s)
        nr = -(-size // LANES)
        out.append(buf[r:r + nr].reshape(-1)[:size].reshape(s))
        r += nr
    return out


def kernel(x, mem, mix_norm, w_in, gdn_conv, gdn_a_log, gdn_dt_bias, gdn_norm, w_out, xattn_norm, mem_norm, w_xq, w_xkv, w_xo, ffn_norm, w_up, ffn_conv, ffn_conv_bias, w_down, final_norm, loss_target, m_mix_norm, m_w_in, m_gdn_conv, m_gdn_a_log, m_gdn_dt_bias, m_gdn_norm, m_w_out, m_xattn_norm, m_mem_norm, m_w_xq, m_w_xkv, m_w_xo, m_ffn_norm, m_w_up, m_ffn_conv, m_ffn_conv_bias, m_w_down, m_final_norm, v_mix_norm, v_w_in, v_gdn_conv, v_gdn_a_log, v_gdn_dt_bias, v_gdn_norm, v_w_out, v_xattn_norm, v_mem_norm, v_w_xq, v_w_xkv, v_w_xo, v_ffn_norm, v_w_up, v_ffn_conv, v_ffn_conv_bias, v_w_down, v_final_norm):
    L = w_in.shape[0]
    _, S, D = x.shape
    nh = D // (2 * HEAD_DIM)
    GW = nh * HEAD_DIM
    n_in = 7 * GW + 2 * nh
    NP = 7 * GW + LANES
    XW = X_HEADS * HEAD_DIM
    F = w_down.shape[1] * 4
    cs_in = w_in.shape[2]
    cs_up = w_up.shape[2]
    cs_xo = w_xo.shape[2]
    tu = _tile(cs_up, 1408)
    per = cs_up // tu
    fper = F // tu
    assert n_in == 4 * cs_in and F % tu == 0 and 2 * F == 4 * cs_up

    xi, yi, ci = lax.axis_index("x"), lax.axis_index("y"), lax.axis_index("c")
    chip = 2 * xi + yi
    cvec = jnp.reshape(ci, (1,)).astype(jnp.int32)

    cs_gc, cs_fc = gdn_conv.shape[2], ffn_conv.shape[2]
    keep = jnp.where(ci == 0, 1.0, 0.0).astype(F32)
    gc_full = lax.dynamic_update_slice(jnp.zeros((L, SHORT_CONV, 4 * cs_gc), F32), gdn_conv * keep, (0, 0, chip * cs_gc))
    fc_full = lax.dynamic_update_slice(jnp.zeros((L, FFN_CONV, 4 * cs_fc), F32), ffn_conv * keep, (0, 0, chip * cs_fc))
    conv_all = _allreduce_small(_pack([gc_full, fc_full]), name="allgather_conv")
    gdn_conv_full, ffn_conv_full = _unpack(conv_all, [gc_full.shape, fc_full.shape])

    big = [w_in, w_out, w_xq, w_xkv, w_xo, w_up, w_down]
    kvec = jnp.reshape(chip, (1,)).astype(jnp.int32)
    kcvec = jnp.stack([chip, ci]).astype(jnp.int32)
    ab = jnp.zeros((L, 2, LANES), F32).at[:, 0, nh:2 * nh].set(gdn_a_log).at[:, 1, nh:2 * nh].set(gdn_dt_bias)

    def vec(p, l):
        return p[l:l + 1]

    xo_fwd_b = pl.BlockSpec((None, XW, cs_xo), lambda i, j, k: (j, 0, 0))
    xo_dg_b = pl.BlockSpec((None, XW, cs_xo), lambda i, j, k: (k, 0, 0))
    xo_wg_o = pl.BlockSpec((None, XW, cs_xo), lambda i, j, k: (j, 0, 0))
    up_fwd_b = pl.BlockSpec((None, D, tu), lambda i, j, k: (j // per, 0, j % per))

    def fwd_layer(l, xc, g_in, rest, token):
        g_mix = vec(mix_norm, l) if token is None else vec(mix_norm, l) + token[0:1, 0:1]
        w_in_l = jnp.concatenate([g_in[0], g_in[1], g_in[2], g_in[3], jnp.zeros((D, NP - n_in), BF16)], axis=1)
        s = dict(x0=xc, w_in=w_in_l)
        s["h"] = _rms_fwd(xc, g_mix, name="rms_mix_fwd")
        s["proj"] = _matmul(s["h"], w_in_l, tn=2432, tk=D, name="mm_in_fwd")
        s["sb"], s["tot"] = _sb_fwd(s["proj"], nh, name="sb_fwd")
        gdn_out, s["gdn"] = _gdn_forward(s["proj"], gdn_conv_full[l], ab[l], vec(gdn_norm, l), nh, "")
        g_out, g_xq, g_xkv, g_xo, g_up, g_down = rest(gdn_out)
        w_out_l, w_xq_l, w_xkv_l, w_down_l = g_out.reshape(2 * GW, D), g_xq.reshape(D, XW), g_xkv.reshape(D, 2 * XW), g_down.reshape(F, D)
        s.update(w_out=w_out_l, w_xq=w_xq_l, w_xkv=w_xkv_l, w_xo=g_xo, w_up=g_up, w_down=w_down_l)
        s["mixed"] = jnp.concatenate([s["sb"], gdn_out], axis=1)
        s["x1"] = _matmul(s["mixed"], w_out_l, res=xc, tk=2 * GW, name="mm_out_fwd")
        s["memn"] = _rms_fwd(mem[0], vec(mem_norm, l), name="rms_mem_fwd")
        s["kv"] = _matmul(s["memn"], w_xkv_l, out_dtype=BF16, tk=D, name="mm_xkv_fwd")
        s["hq"] = _rms_fwd(s["x1"], vec(xattn_norm, l), name="rms_xattn_fwd")
        s["q"] = _matmul(s["hq"], w_xq_l, out_dtype=BF16, tk=D, name="mm_xq_fwd")
        s["xo"] = _xattn_fwd(s["q"], s["kv"], name="xattn_fwd")
        s["x2"] = _matmul(s["xo"], g_xo, res=s["x1"], dims=(S, D, XW), tn=cs_xo, tk=XW, b_spec=xo_fwd_b, name="mm_xo_fwd")
        s["hf"] = _rms_fwd(s["x2"], vec(ffn_norm, l), name="rms_ffn_fwd")
        s["u"] = _matmul(s["hf"], g_up, dims=(S, 2 * F, D), tn=tu, tk=D, b_spec=up_fwd_b, name="mm_up_fwd")
        s["act"] = _ffn_act_fwd(s["u"], ffn_conv_full[l], ffn_conv_bias[l:l + 1], name="ffn_act_fwd")
        x3 = _matmul(s["act"], w_down_l, res=s["x2"], tm=1024, tn=1024, tk=tu, name="mm_down_fwd")
        return x3, s

    def bwd_ffn(l, s, dx3, dx3b):
        dact = _matmul(dx3b, s["w_down"], tb=True, tm=1024, tk=D, name="mm_down_dgrad")
        d_down = _matmul(s["act"], dx3b, ta=True, tn=1024, tk=S, out_dtype=BF16, name="mm_down_wgrad")
        du3, dcw3, dcb3 = _ffn_act_bwd(s["u"], ffn_conv_full[l], ffn_conv_bias[l:l + 1], dact, name="ffn_act_bwd")
        tq, td = _tile(S, 1024), _tile(D, 1024)
        dhf = _matmul(du3, s["w_up"], tb=True, dims=(S, D, 2 * F), tm=tq, tn=td, tk=tu,
                      a_spec=pl.BlockSpec((None, tq, tu), lambda i, j, k: (k // fper, i, k % fper)),
                      b_spec=pl.BlockSpec((None, td, tu), lambda i, j, k: (k // per, j, k % per)), name="mm_up_dgrad")
        d_up = _matmul(s["hf"], du3, ta=True, dims=(D, 2 * F, S), tn=tu, tk=S,
                       b_spec=pl.BlockSpec((None, S, tu), lambda i, j, k: (j // fper, 0, j % fper)),
                       o_spec=pl.BlockSpec((None, _tile(D, 512), tu), lambda i, j, k: (j // per, i, j % per)),
                       out_shape=jax.ShapeDtypeStruct((4, D, cs_up), BF16), name="mm_up_wgrad")
        dx2, dx2b, dg_ffn = _rms_bwd(s["x2"], vec(ffn_norm, l), dhf, dx3, name="rms_bwd")
        small = [dg_ffn, jnp.concatenate([dcw3[0], dcw3[1]], axis=1), jnp.concatenate([dcb3[0], dcb3[1]], axis=1)]
        return dx2, dx2b, {5: d_up, 6: d_down.reshape(4, -1, D)}, small

    def bwd_rest(l, s, dx2, dx2b):
        dxo = _matmul(dx2b, s["w_xo"], tb=True, dims=(S, XW, D), tn=XW, tk=cs_xo, b_spec=xo_dg_b, name="mm_xo_dgrad")
        d_xo = _matmul(s["xo"], dx2b, ta=True, dims=(XW, D, S), tm=XW, tn=cs_xo, tk=S, o_spec=xo_wg_o,
                       out_shape=jax.ShapeDtypeStruct((4, XW, cs_xo), BF16), name="mm_xo_wgrad")
        dq, dk, dv = _xattn_bwd(s["q"], s["kv"], dxo, name="xattn_bwd")
        dkv = jnp.concatenate([dk, dv], axis=1)
        dhq = _matmul(dq, s["w_xq"], tb=True, tk=XW, name="mm_xq_dgrad")
        d_xq = _matmul(s["hq"], dq, ta=True, tk=S, out_dtype=BF16, name="mm_xq_wgrad")
        dmemn = _matmul(dkv, s["w_xkv"], tb=True, tk=2 * XW, name="mm_xkv_dgrad")
        d_xkv = _matmul(s["memn"], dkv, ta=True, tk=mem.shape[1], out_dtype=BF16, name="mm_xkv_wgrad")
        _, _, dg_mem = _rms_bwd(mem[0], vec(mem_norm, l), dmemn, None, name="rms_mem_bwd")
        dx1, dx1b, dg_xattn = _rms_bwd(s["x1"], vec(xattn_norm, l), dhq, dx2, name="rms_bwd")
        dmix = _matmul(dx1b, s["w_out"], tb=True, tk=D, name="mm_out_dgrad")
        d_out = _matmul(s["mixed"], dx1b, ta=True, tk=S, out_dtype=BF16, name="mm_out_wgrad")
        dq_s, dk_s, dv_s = _sb_bwd(s["proj"], s["tot"], dmix, nh, name="sb_bwd")
        dx_qkv, dz, dx_g, dconv, dab, dng = _gdn_backward(s["proj"], gdn_conv_full[l], ab[l], vec(gdn_norm, l), s["gdn"], dmix, nh, "")
        dproj = jnp.concatenate([dq_s, dk_s, dv_s, dx_qkv, dz, dx_g], axis=1)
        dh = _matmul(dproj, s["w_in"], tb=True, tm=1024, tn=1024, tk=2432, name="mm_in_dgrad")
        d_in = _matmul(s["h"], dproj, ta=True, tn=2432, tk=S, out_dtype=BF16, name="mm_in_wgrad")
        dx0, dx0b, dg_mix = _rms_bwd(s["x0"], vec(mix_norm, l), dh, dx1, name="rms_bwd")
        slabs = {0: jnp.stack([d_in[:, i * cs_in:(i + 1) * cs_in] for i in range(4)]), 1: d_out.reshape(4, -1, D),
                 2: d_xq.reshape(4, -1, XW), 3: d_xkv.reshape(4, -1, 2 * XW), 4: d_xo}
        return dx0, dx0b, slabs, [dg_mix, dconv, dab, dng, dg_xattn, dg_mem]

    def start_gather(l, idxs, after, tag):
        placed = [_cast_place(big[i], l, kvec, name=f"cast_place_{l}") for i in idxs]
        return _gather_start(placed, after, name=f"gather_start_{l}{tag}")

    def end_gather(pending, after, l, tag):
        send, recv, ws, _ = pending
        ws = _gather_wait(send, recv, ws, after, name=f"gather_wait_{l}{tag}")
        return _gather_to_sibling(ws, name=f"gather_to_sibling{tag}")

    xc = x[0]
    saved = []
    first = start_gather(0, [0], conv_all, "a")
    second = start_gather(0, list(range(1, 7)), first[3], "b")
    for l in range(L):
        if l == 0:
            g_in = end_gather(first, xc, 0, "a")[0]
            rest = lambda after: end_gather(second, after, 0, "b")
            order = second[3]
        else:
            wts = end_gather(pending, xc, l, "")
            g_in, rest, order = wts[0], (lambda after, wts=wts: wts[1:]), wts[1]
        token = None
        if l + 1 < L:
            pending = start_gather(l + 1, list(range(7)), order, "")
            token = pending[3]
        xc, s = fwd_layer(l, xc, g_in, rest, token)
        saved.append(s)
    loss_blk, dxc, dxcb, dg_final = _loss_head(xc, final_norm[None, :], loss_target[0], name="loss_head")

    def start_reduce(slabs, l, tag):
        idxs = sorted(slabs)
        return idxs, l, tag, _reduce_start([slabs[i] for i in idxs], cvec, name=f"reduce_start_{l}{tag}")

    def finish_reduce(item, sums, after):
        idxs, l, tag, (send, recv, ps, lands, _) = item
        ps, lands = _reduce_wait(send, recv, ps, lands, after, name=f"reduce_wait_{l}{tag}")
        for i, p, rb in zip(idxs, ps, lands):
            sums[i] = _sum_partials(p, rb, kcvec, sums[i], l, L, name=f"sum_partials_{l}")

    sums = [None] * 7
    small_by_layer = [None] * L
    in_flight = []
    for l in reversed(range(L)):
        if in_flight:
            dxcb = dxcb + in_flight[-1][3][4][0, 0].astype(BF16)
        dx2, dx2b, slabs_ffn, small_ffn = bwd_ffn(l, saved[l], dxc, dxcb)
        for item in in_flight:
            finish_reduce(item, sums, dx2)
        in_flight = [start_reduce(slabs_ffn, l, "f")]
        dx2b = dx2b + in_flight[-1][3][4][0, 0].astype(BF16)
        dxc, dxcb, slabs_rest, small_rest = bwd_rest(l, saved[l], dx2, dx2b)
        saved[l] = None
        in_flight.append(start_reduce(slabs_rest, l, "r"))
        small_by_layer[l] = small_rest + small_ffn

    small_flat = [a for l in range(L) for a in small_by_layer[l]] + [dg_final, loss_blk[0:1]]
    red_buf = _allreduce_small(_pack(small_flat), name="allreduce_small")
    for item in in_flight:
        finish_reduce(item, sums, red_buf)
    from_sib = _swap_with_sibling(sums, name="swap_halves")
    red = _unpack(red_buf, [a.shape for a in small_flat])
    per_layer = [red[9 * l:9 * l + 9] for l in range(L)]
    col = lambda i: jnp.concatenate([p[i] for p in per_layer], axis=0)
    stk = lambda i: jnp.stack([p[i] for p in per_layer])
    g_conv_full, g_ab, g_fconv_full = stk(1), stk(2), stk(7)
    grads_small = dict(
        mix_norm=col(0), gdn_conv=lax.dynamic_slice(g_conv_full, (0, 0, chip * cs_gc), (L, SHORT_CONV, cs_gc)),
        gdn_a_log=g_ab[:, 0, nh:2 * nh], gdn_dt_bias=g_ab[:, 1, nh:2 * nh], gdn_norm=col(3), xattn_norm=col(4),
        mem_norm=col(5), ffn_norm=col(6), ffn_conv=lax.dynamic_slice(g_fconv_full, (0, 0, chip * cs_fc), (L, FFN_CONV, cs_fc)),
        ffn_conv_bias=col(8), final_norm=red[-2][0])
    loss = red[-1][0, 0]

    names_small = ["mix_norm", "gdn_conv", "gdn_a_log", "gdn_dt_bias", "gdn_norm", "xattn_norm", "mem_norm", "ffn_norm",
                   "ffn_conv", "ffn_conv_bias", "final_norm"]
    w_small = dict(mix_norm=mix_norm, gdn_conv=gdn_conv, gdn_a_log=gdn_a_log, gdn_dt_bias=gdn_dt_bias, gdn_norm=gdn_norm,
                   xattn_norm=xattn_norm, mem_norm=mem_norm, ffn_norm=ffn_norm, ffn_conv=ffn_conv, ffn_conv_bias=ffn_conv_bias,
                   final_norm=final_norm)
    m_small = dict(mix_norm=m_mix_norm, gdn_conv=m_gdn_conv, gdn_a_log=m_gdn_a_log, gdn_dt_bias=m_gdn_dt_bias, gdn_norm=m_gdn_norm,
                   xattn_norm=m_xattn_norm, mem_norm=m_mem_norm, ffn_norm=m_ffn_norm, ffn_conv=m_ffn_conv,
                   ffn_conv_bias=m_ffn_conv_bias, final_norm=m_final_norm)
    v_small = dict(mix_norm=v_mix_norm, gdn_conv=v_gdn_conv, gdn_a_log=v_gdn_a_log, gdn_dt_bias=v_gdn_dt_bias, gdn_norm=v_gdn_norm,
                   xattn_norm=v_xattn_norm, mem_norm=v_mem_norm, ffn_norm=v_ffn_norm, ffn_conv=v_ffn_conv,
                   ffn_conv_bias=v_ffn_conv_bias, final_norm=v_final_norm)
    shapes_small = [w_small[n].shape for n in names_small]
    packed = [_pack([d[n] for n in names_small], row_multiple=128)[None] for d in (w_small, grads_small, m_small, v_small)]
    upd_small = [_unpack(o[0], shapes_small) for o in _adamw(*packed, name="adamw_small")]
    delta, new_m, new_v = [dict(zip(names_small, u)) for u in upd_small]
    grads = dict(grads_small)
    big_names = ["w_in", "w_out", "w_xq", "w_xkv", "w_xo", "w_up", "w_down"]
    big_m = [m_w_in, m_w_out, m_w_xq, m_w_xkv, m_w_xo, m_w_up, m_w_down]
    big_v = [v_w_in, v_w_out, v_w_xq, v_w_xkv, v_w_xo, v_w_up, v_w_down]
    for n, w, go, gs, m, v in zip(big_names, big, sums, from_sib, big_m, big_v):
        grads[n], delta[n], new_m[n], new_v[n] = _adamw_halves(w, go, gs, cvec, m, v, name=f"adamw_{n}")

    order = ["mix_norm", "w_in", "gdn_conv", "gdn_a_log", "gdn_dt_bias", "gdn_norm", "w_out", "xattn_norm", "mem_norm", "w_xq",
             "w_xkv", "w_xo", "ffn_norm", "w_up", "ffn_conv", "ffn_conv_bias", "w_down", "final_norm"]
    return (loss, dxc[None], *[grads[n] for n in order], *[delta[n] for n in order], *[new_m[n] for n in order],
            *[new_v[n] for n in order])
```

```python
import functools
import math

import jax
import jax.numpy as jnp
from jax import lax
from jax.experimental import pallas as pl
from jax.experimental.pallas import tpu as pltpu

F32 = jnp.float32
BF16 = jnp.bfloat16

HEAD_DIM = 128
CHUNK = 64
GDN_CPB = 4
SB_TQ, SB_TK = 256, 512
SHORT_CONV = 4
FFN_CONV = 3
X_HEADS = 4
EPS = 1e-6
LANES = 128
VMEM_LIMIT = 56 * 2**20

ADAM_LR, ADAM_B1, ADAM_B2, ADAM_EPS, ADAM_WD, ADAM_STEP = 0.001, 0.9, 0.999, 1e-08, 0.01, 10

HI = lax.Precision.HIGH


def _params(sem):
    return pltpu.CompilerParams(dimension_semantics=sem, vmem_limit_bytes=VMEM_LIMIT)


def _tile(n, want):
    if n <= want:
        return n
    t = (want // LANES) * LANES
    while t > LANES and n % t:
        t -= LANES
    assert n % t == 0, (n, want)
    return t


def _sigmoid(x):
    return jax.nn.sigmoid(x)


def _softplus(x):
    return jnp.maximum(x, 0.0) + jnp.log(1.0 + jnp.exp(-jnp.abs(x)))


def _matmul(a, b, *, name, ta=False, tb=False, out_dtype=F32, res=None, tm=512, tn=512, tk=2048,
            dims=None, a_spec=None, b_spec=None, o_spec=None, out_shape=None):
    if dims is None:
        M, K = (a.shape[1], a.shape[0]) if ta else a.shape
        N = b.shape[0] if tb else b.shape[1]
    else:
        M, N, K = dims
    tm, tn, tk = _tile(M, tm), _tile(N, tn), _tile(K, tk)
    nk = K // tk
    dn = (((0 if ta else 1,), (1 if tb else 0,)), ((), ()))

    def body(*refs):
        a_ref, b_ref = refs[0], refs[1]
        r_ref = refs[2] if res is not None else None
        o_ref = refs[3] if res is not None else refs[2]
        p = lax.dot_general(a_ref[...].astype(BF16), b_ref[...].astype(BF16), dn, preferred_element_type=F32)

        def finish(acc):
            if r_ref is not None:
                acc = acc + r_ref[...].astype(F32)
            o_ref[...] = acc.astype(o_ref.dtype)

        if nk == 1:
            finish(p)
        else:
            acc_ref = refs[-1]
            k = pl.program_id(2)

            @pl.when(k == 0)
            def _():
                acc_ref[...] = p

            @pl.when(jnp.logical_and(k > 0, k < nk - 1))
            def _():
                acc_ref[...] += p

            @pl.when(k == nk - 1)
            def _():
                finish(acc_ref[...] + p)

    if a_spec is None:
        a_spec = pl.BlockSpec((tk, tm), lambda i, j, k: (k, i)) if ta else pl.BlockSpec((tm, tk), lambda i, j, k: (i, k))
    if b_spec is None:
        b_spec = pl.BlockSpec((tn, tk), lambda i, j, k: (j, k)) if tb else pl.BlockSpec((tk, tn), lambda i, j, k: (k, j))
    if o_spec is None:
        o_spec = pl.BlockSpec((tm, tn), lambda i, j, k: (i, j))
    if out_shape is None:
        out_shape = jax.ShapeDtypeStruct((M, N), out_dtype)
    in_specs, args = [a_spec, b_spec], [a, b]
    if res is not None:
        in_specs.append(pl.BlockSpec((tm, tn), lambda i, j, k: (i, j)))
        args.append(res)
    return pl.pallas_call(
        body, grid=(M // tm, N // tn, nk), in_specs=in_specs, out_specs=o_spec, out_shape=out_shape,
        scratch_shapes=[pltpu.VMEM((tm, tn), F32)] if nk > 1 else [],
        compiler_params=_params(("parallel", "parallel", "arbitrary")), name=name)(*args)


def _rms_fwd(x, g, *, name):
    R, D = x.shape
    tr = _tile(R, 256)

    def body(x_ref, g_ref, o_ref):
        xv = x_ref[...]
        rstd = lax.rsqrt(jnp.mean(xv * xv, axis=-1, keepdims=True) + EPS)
        o_ref[...] = (xv * rstd * g_ref[...]).astype(o_ref.dtype)

    return pl.pallas_call(
        body, grid=(R // tr,), in_specs=[pl.BlockSpec((tr, D), lambda i: (i, 0)), pl.BlockSpec((1, D), lambda i: (0, 0))],
        out_specs=pl.BlockSpec((tr, D), lambda i: (i, 0)), out_shape=jax.ShapeDtypeStruct((R, D), BF16),
        compiler_params=_params(("parallel",)), name=name)(x, g)


def _rms_bwd(x, g, dh, dres, *, name):
    R, D = x.shape
    tr = _tile(R, 256)

    def body(*refs):
        if dres is None:
            x_ref, g_ref, dh_ref, dx_ref, dxb_ref, dg_ref = refs
        else:
            x_ref, g_ref, dh_ref, dr_ref, dx_ref, dxb_ref, dg_ref = refs
        xv = x_ref[...]
        dhv = dh_ref[...].astype(F32)
        rstd = lax.rsqrt(jnp.mean(xv * xv, axis=-1, keepdims=True) + EPS)
        xhat = xv * rstd
        t = dhv * g_ref[...]
        dx = rstd * (t - xhat * jnp.mean(t * xhat, axis=-1, keepdims=True))
        if dres is not None:
            dx = dx + dr_ref[...]
        dx_ref[...] = dx
        dxb_ref[...] = dx.astype(BF16)
        part = jnp.sum(dhv * xhat, axis=0, keepdims=True)

        @pl.when(pl.program_id(0) == 0)
        def _():
            dg_ref[...] = part

        @pl.when(pl.program_id(0) > 0)
        def _():
            dg_ref[...] += part

    row = pl.BlockSpec((tr, D), lambda i: (i, 0))
    vec = pl.BlockSpec((1, D), lambda i: (0, 0))
    in_specs = [row, vec, row] + ([row] if dres is not None else [])
    args = [x, g, dh] + ([dres] if dres is not None else [])
    return pl.pallas_call(
        body, grid=(R // tr,), in_specs=in_specs, out_specs=[row, row, vec],
        out_shape=[jax.ShapeDtypeStruct((R, D), F32), jax.ShapeDtypeStruct((R, D), BF16), jax.ShapeDtypeStruct((1, D), F32)],
        compiler_params=_params(("arbitrary",)), name=name)(*args)


def _loss_head(x, g, tgt, *, name):
    R, D = x.shape
    tr = _tile(R, 256)

    def body(x_ref, g_ref, t_ref, l_ref, dx_ref, dxb_ref, dg_ref):
        xv = x_ref[...]
        rstd = lax.rsqrt(jnp.mean(xv * xv, axis=-1, keepdims=True) + EPS)
        xhat = xv * rstd
        err = xhat * g_ref[...] - t_ref[...]
        dy = err * (1.0 / D)
        t = dy * g_ref[...]
        dx = rstd * (t - xhat * jnp.mean(t * xhat, axis=-1, keepdims=True))
        dx_ref[...] = dx
        dxb_ref[...] = dx.astype(BF16)
        part = jnp.sum(dy * xhat, axis=0, keepdims=True)
        lpart = jnp.zeros((8, LANES), F32) + 0.5 * jnp.sum(jnp.mean(err * err, axis=-1, keepdims=True))

        @pl.when(pl.program_id(0) == 0)
        def _():
            dg_ref[...] = part
            l_ref[...] = lpart

        @pl.when(pl.program_id(0) > 0)
        def _():
            dg_ref[...] += part
            l_ref[...] += lpart

    row = pl.BlockSpec((tr, D), lambda i: (i, 0))
    vec = pl.BlockSpec((1, D), lambda i: (0, 0))
    return pl.pallas_call(
        body, grid=(R // tr,), in_specs=[row, vec, row],
        out_specs=[pl.BlockSpec((8, LANES), lambda i: (0, 0)), row, row, vec],
        out_shape=[jax.ShapeDtypeStruct((8, LANES), F32), jax.ShapeDtypeStruct((R, D), F32),
                   jax.ShapeDtypeStruct((R, D), BF16), jax.ShapeDtypeStruct((1, D), F32)],
        compiler_params=_params(("arbitrary",)), name=name)(x, g, tgt)


def _shift_down(x, s):
    if s == 0:
        return x
    row = lax.broadcasted_iota(jnp.int32, x.shape, 0)
    return jnp.where(row >= s, pltpu.roll(x, s, 0), 0.0)


def _shift_up(x, s):
    if s == 0:
        return x
    n = x.shape[0]
    row = lax.broadcasted_iota(jnp.int32, x.shape, 0)
    return jnp.where(row < n - s, pltpu.roll(x, n - s, 0), 0.0)


def _dwconv(x, w):
    k = w.shape[0]
    acc = x * w[k - 1:k, :]
    for i in range(k - 1):
        acc = acc + _shift_down(x, k - 1 - i) * w[i:i + 1, :]
    return acc


def _dwconv_bwd(x, w, dc):
    k = w.shape[0]
    dx = dc * w[k - 1:k, :]
    dws = []
    for i in range(k - 1):
        s = k - 1 - i
        dx = dx + _shift_up(dc, s) * w[i:i + 1, :]
        dws.append(jnp.sum(dc * _shift_down(x, s), axis=0, keepdims=True))
    dws.append(jnp.sum(dc * x, axis=0, keepdims=True))
    return dx, jnp.concatenate(dws, axis=0)


def _ffn_act_fwd(u, cw, cb, *, name):
    S, F2 = u.shape
    F = F2 // 2
    tc = _tile(F, 256)
    nb = F // tc

    def body(ug_ref, uu_ref, wg_ref, wu_ref, bg_ref, bu_ref, o_ref):
        cg = _dwconv(ug_ref[...], wg_ref[...]) + bg_ref[...]
        cu = _dwconv(uu_ref[...], wu_ref[...]) + bu_ref[...]
        o_ref[...] = (cg * _sigmoid(cg) * cu).astype(o_ref.dtype)

    blk = lambda r, off: pl.BlockSpec((r, tc), lambda j: (0, j + off))
    return pl.pallas_call(
        body, grid=(nb,), in_specs=[blk(S, 0), blk(S, nb), blk(FFN_CONV, 0), blk(FFN_CONV, nb), blk(1, 0), blk(1, nb)],
        out_specs=blk(S, 0), out_shape=jax.ShapeDtypeStruct((S, F), BF16),
        compiler_params=_params(("parallel",)), name=name)(u, u, cw, cw, cb, cb)


def _ffn_act_bwd(u, cw, cb, dact, *, name):
    S, F2 = u.shape
    F = F2 // 2
    tc = _tile(F, 256)
    nb = F // tc

    def body(ug_ref, uu_ref, wg_ref, wu_ref, bg_ref, bu_ref, da_ref, du_ref, dw_ref, db_ref):
        ug, uu = ug_ref[...], uu_ref[...]
        cg = _dwconv(ug, wg_ref[...]) + bg_ref[...]
        cu = _dwconv(uu, wu_ref[...]) + bu_ref[...]
        sg = _sigmoid(cg)
        da = da_ref[...].astype(F32)
        dcu = da * (cg * sg)
        dcg = da * cu * (sg * (1.0 + cg * (1.0 - sg)))
        dxg, dwg = _dwconv_bwd(ug, wg_ref[...], dcg)
        dxu, dwu = _dwconv_bwd(uu, wu_ref[...], dcu)
        du_ref[0] = dxg.astype(du_ref.dtype)
        du_ref[1] = dxu.astype(du_ref.dtype)
        dw_ref[0] = dwg
        dw_ref[1] = dwu
        db_ref[0] = jnp.sum(dcg, axis=0, keepdims=True)
        db_ref[1] = jnp.sum(dcu, axis=0, keepdims=True)

    blk = lambda r, off: pl.BlockSpec((r, tc), lambda j: (0, j + off))
    blk3 = lambda r: pl.BlockSpec((2, r, tc), lambda j: (0, 0, j))
    return pl.pallas_call(
        body, grid=(nb,),
        in_specs=[blk(S, 0), blk(S, nb), blk(FFN_CONV, 0), blk(FFN_CONV, nb), blk(1, 0), blk(1, nb), blk(S, 0)],
        out_specs=[blk3(S), blk3(FFN_CONV), blk3(1)],
        out_shape=[jax.ShapeDtypeStruct((2, S, F), BF16), jax.ShapeDtypeStruct((2, FFN_CONV, F), F32),
                   jax.ShapeDtypeStruct((2, 1, F), F32)],
        compiler_params=_params(("parallel",)), name=name)(u, u, cw, cw, cb, cb, dact)


def _xattn_fwd(q, kv, *, name):
    S, XW = q.shape
    M = kv.shape[0]
    nh = XW // HEAD_DIM
    tq = _tile(S, 512)
    scale = HEAD_DIM ** -0.5

    def body(q_ref, k_ref, v_ref, o_ref):
        z = lax.dot_general(q_ref[...], k_ref[...], (((1,), (1,)), ((), ())), preferred_element_type=F32) * scale
        e = jnp.exp(z - jnp.max(z, axis=-1, keepdims=True))
        p = e / jnp.sum(e, axis=-1, keepdims=True)
        o_ref[...] = jnp.dot(p.astype(BF16), v_ref[...], preferred_element_type=F32).astype(o_ref.dtype)

    return pl.pallas_call(
        body, grid=(nh, S // tq),
        in_specs=[pl.BlockSpec((tq, HEAD_DIM), lambda h, i: (i, h)), pl.BlockSpec((M, HEAD_DIM), lambda h, i: (0, h)),
                  pl.BlockSpec((M, HEAD_DIM), lambda h, i: (0, nh + h))],
        out_specs=pl.BlockSpec((tq, HEAD_DIM), lambda h, i: (i, h)), out_shape=jax.ShapeDtypeStruct((S, XW), BF16),
        compiler_params=_params(("parallel", "parallel")), name=name)(q, kv, kv)


def _xattn_bwd(q, kv, do, *, name):
    S, XW = q.shape
    M = kv.shape[0]
    nh = XW // HEAD_DIM
    tq = _tile(S, 512)
    scale = HEAD_DIM ** -0.5
    nt = (((1,), (1,)), ((), ()))
    tn = (((0,), (0,)), ((), ()))

    def body(q_ref, k_ref, v_ref, do_ref, dq_ref, dk_ref, dv_ref):
        qv, kvv, vv = q_ref[...], k_ref[...], v_ref[...]
        dov = do_ref[...].astype(BF16)
        z = lax.dot_general(qv, kvv, nt, preferred_element_type=F32) * scale
        e = jnp.exp(z - jnp.max(z, axis=-1, keepdims=True))
        p = e / jnp.sum(e, axis=-1, keepdims=True)
        dp = lax.dot_general(dov, vv, nt, preferred_element_type=F32)
        ds = (p * (dp - jnp.sum(dp * p, axis=-1, keepdims=True)) * scale).astype(BF16)
        dq_ref[...] = jnp.dot(ds, kvv, preferred_element_type=F32).astype(dq_ref.dtype)
        dk = lax.dot_general(ds, qv, tn, preferred_element_type=F32)
        dv = lax.dot_general(p.astype(BF16), dov, tn, preferred_element_type=F32)

        @pl.when(pl.program_id(1) == 0)
        def _():
            dk_ref[...] = dk
            dv_ref[...] = dv

        @pl.when(pl.program_id(1) > 0)
        def _():
            dk_ref[...] += dk
            dv_ref[...] += dv

    qs = pl.BlockSpec((tq, HEAD_DIM), lambda h, i: (i, h))
    ms = pl.BlockSpec((M, HEAD_DIM), lambda h, i: (0, h))
    return pl.pallas_call(
        body, grid=(nh, S // tq),
        in_specs=[qs, ms, pl.BlockSpec((M, HEAD_DIM), lambda h, i: (0, nh + h)), qs],
        out_specs=[qs, ms, ms],
        out_shape=[jax.ShapeDtypeStruct((S, XW), BF16), jax.ShapeDtypeStruct((M, XW), F32), jax.ShapeDtypeStruct((M, XW), F32)],
        compiler_params=_params(("parallel", "arbitrary")), name=name)(q, kv, kv, do)


_NN = (((1,), (0,)), ((), ()))
_NT = (((1,), (1,)), ((), ()))
_TN = (((0,), (0,)), ((), ()))


def _batched(dn, a):
    if a.ndim == 2:
        return dn
    (ca,), (cb,) = dn[0]
    return (((ca + 1,), (cb + 1,)), ((0,), (0,)))


def _dot(a, b, dn=_NN):
    return lax.dot_general(a.astype(BF16), b.astype(BF16), _batched(dn, a), preferred_element_type=F32)


def _dot_hi(a, b, dn=_NN):
    return lax.dot_general(a, b, _batched(dn, a), preferred_element_type=F32, precision=HI)


def _dot_split(a, b01, dn=_NN):
    hi = a.astype(BF16)
    lo = (a - hi.astype(F32)).astype(BF16)
    return (lax.dot_general(hi, b01, dn, preferred_element_type=F32)
            + lax.dot_general(lo, b01, dn, preferred_element_type=F32))


def _after_matrix(n, transpose=False):
    row = lax.broadcasted_iota(jnp.int32, (n, n), 0)
    col = lax.broadcasted_iota(jnp.int32, (n, n), 1)
    return (row < col if transpose else row > col).astype(BF16)


def _sb_fwd(proj, nh, *, name):
    S = proj.shape[0]
    TQ, TK = min(SB_TQ, S), min(SB_TK, S)
    nq = S // TQ
    scale = HEAD_DIM ** -0.5

    def body(q_ref, k_ref, v_ref, o_ref, tot_ref):
        i = pl.program_id(1)
        q = q_ref[...].astype(BF16)
        qpos = i * TQ + lax.broadcasted_iota(jnp.int32, (TQ, TK), 0)
        kcol = lax.broadcasted_iota(jnp.int32, (TQ, TK), 1)
        after = _after_matrix(TK)
        nt = ((i + 1) * TQ + TK - 1) // TK
        n_in = (i * TQ) // TK

        def make_step(masked):
            def step(j, carry):
                acc, out = carry
                off = pl.multiple_of(j * TK, TK)
                kb = k_ref[pl.ds(off, TK), :].astype(BF16)
                vb = v_ref[pl.ds(off, TK), :].astype(BF16)
                z = lax.dot_general(q, kb, _NT, preferred_element_type=F32) * scale
                ls = -_softplus(z)
                if masked:
                    valid = kcol + off < qpos
                    ls = jnp.where(valid, ls, 0.0)
                w = jnp.exp(ls + z + (_dot_split(ls, after) + acc))
                if masked:
                    w = jnp.where(valid, w, 0.0)
                out = out + jnp.dot(w.astype(BF16), vb, preferred_element_type=F32)
                return acc + jnp.sum(ls, axis=1, keepdims=True), out
            return step

        edge, inner = make_step(True), make_step(False)
        carry = (jnp.zeros((TQ, 1), F32), jnp.zeros((TQ, HEAD_DIM), F32))
        carry = lax.fori_loop(0, nt - n_in, lambda t, c: edge(nt - 1 - t, c), carry)
        acc, out = lax.fori_loop(0, n_in, lambda t, c: inner(n_in - 1 - t, c), carry)
        o_ref[...] = out.astype(o_ref.dtype)
        tot_ref[...] = acc

    return pl.pallas_call(
        body, grid=(nh, nq),
        in_specs=[pl.BlockSpec((TQ, HEAD_DIM), lambda h, i: (i, h)),
                  pl.BlockSpec((S, HEAD_DIM), lambda h, i: (0, nh + h)),
                  pl.BlockSpec((S, HEAD_DIM), lambda h, i: (0, 2 * nh + h))],
        out_specs=[pl.BlockSpec((TQ, HEAD_DIM), lambda h, i: (i, h)), pl.BlockSpec((None, TQ, 1), lambda h, i: (h, i, 0))],
        out_shape=[jax.ShapeDtypeStruct((S, nh * HEAD_DIM), BF16), jax.ShapeDtypeStruct((nh, S, 1), F32)],
        compiler_params=_params(("parallel", "parallel")), name=name)(proj, proj, proj)


def _sb_bwd(proj, tot, dmix, nh, *, name):
    S = proj.shape[0]
    TQ, TK = min(SB_TQ, S), min(SB_TK, S)
    nq = S // TQ
    scale = HEAD_DIM ** -0.5

    def body(q_ref, k_ref, v_ref, tot_ref, do_ref, dq_ref, dk_ref, dv_ref, dk_acc, dv_acc):
        i = pl.program_id(1)

        @pl.when(i == 0)
        def _():
            dk_acc[...] = jnp.zeros_like(dk_acc)
            dv_acc[...] = jnp.zeros_like(dv_acc)

        q = q_ref[...].astype(BF16)
        do = do_ref[...].astype(BF16)
        tot = tot_ref[...]
        qpos = i * TQ + lax.broadcasted_iota(jnp.int32, (TQ, TK), 0)
        kcol = lax.broadcasted_iota(jnp.int32, (TQ, TK), 1)
        after = _after_matrix(TK)
        before = _after_matrix(TK, transpose=True)
        nt = ((i + 1) * TQ + TK - 1) // TK
        n_in = (i * TQ) // TK

        def make_step(masked):
            def step(j, carry):
                pre, g_sum, dq = carry
                off = pl.multiple_of(j * TK, TK)
                kb = k_ref[pl.ds(off, TK), :].astype(BF16)
                vb = v_ref[pl.ds(off, TK), :].astype(BF16)
                z = lax.dot_general(q, kb, _NT, preferred_element_type=F32) * scale
                ls = -_softplus(z)
                if masked:
                    valid = kcol + off < qpos
                    ls = jnp.where(valid, ls, 0.0)
                lb = ls + z
                rs = jnp.sum(ls, axis=1, keepdims=True)
                w = jnp.exp(lb + (_dot_split(ls, after) + (tot - pre - rs)))
                if masked:
                    w = jnp.where(valid, w, 0.0)
                g = lax.dot_general(do, vb, _NT, preferred_element_type=F32) * w
                dls = _dot_split(g, before) + g_sum
                sig = jnp.exp(lb)
                dz = g * (1.0 - sig) - dls * sig
                if masked:
                    dz = jnp.where(valid, dz, 0.0)
                dz = (dz * scale).astype(BF16)
                dq = dq + jnp.dot(dz, kb, preferred_element_type=F32)
                dk_acc[pl.ds(off, TK), :] += lax.dot_general(dz, q, _TN, preferred_element_type=F32)
                dv_acc[pl.ds(off, TK), :] += lax.dot_general(w.astype(BF16), do, _TN, preferred_element_type=F32)
                return pre + rs, g_sum + jnp.sum(g, axis=1, keepdims=True), dq
            return step

        zero = jnp.zeros((TQ, 1), F32)
        carry = lax.fori_loop(0, n_in, make_step(False), (zero, zero, jnp.zeros((TQ, HEAD_DIM), F32)))
        _, _, dq = lax.fori_loop(n_in, nt, make_step(True), carry)
        dq_ref[...] = dq.astype(dq_ref.dtype)

        @pl.when(i == nq - 1)
        def _():
            dk_ref[...] = dk_acc[...].astype(dk_ref.dtype)
            dv_ref[...] = dv_acc[...].astype(dv_ref.dtype)

    qs = pl.BlockSpec((TQ, HEAD_DIM), lambda h, i: (i, h))
    full = pl.BlockSpec((S, HEAD_DIM), lambda h, i: (0, h))
    o = jax.ShapeDtypeStruct((S, nh * HEAD_DIM), BF16)
    return pl.pallas_call(
        body, grid=(nh, nq),
        in_specs=[qs, pl.BlockSpec((S, HEAD_DIM), lambda h, i: (0, nh + h)),
                  pl.BlockSpec((S, HEAD_DIM), lambda h, i: (0, 2 * nh + h)),
                  pl.BlockSpec((None, TQ, 1), lambda h, i: (h, i, 0)), qs],
        out_specs=[qs, full, full], out_shape=[o, o, o],
        scratch_shapes=[pltpu.VMEM((S, HEAD_DIM), F32), pltpu.VMEM((S, HEAD_DIM), F32)],
        compiler_params=_params(("parallel", "arbitrary")), name=name)(proj, proj, proj, tot, dmix)


def _gdn_qkv_fwd(proj, conv_w, nh, *, name):
    S = proj.shape[0]
    GW = nh * HEAD_DIM
    scale = HEAD_DIM ** -0.5

    def body(x_ref, w_ref, o_ref):
        sec = pl.program_id(0) // nh
        c = _dwconv(x_ref[...], w_ref[...])
        s = c * _sigmoid(c)
        r = lax.rsqrt(jnp.sum(s * s, axis=1, keepdims=True) + EPS)
        fac = jnp.where(sec == 0, scale, 1.0)
        o_ref[...] = jnp.where(sec == 2, s, s * (r * fac))

    return pl.pallas_call(
        body, grid=(3 * nh,),
        in_specs=[pl.BlockSpec((S, HEAD_DIM), lambda j: (0, 3 * nh + j)), pl.BlockSpec((SHORT_CONV, HEAD_DIM), lambda j: (0, j))],
        out_specs=pl.BlockSpec((None, S, HEAD_DIM), lambda j: (j // nh, 0, j % nh)),
        out_shape=jax.ShapeDtypeStruct((3, S, GW), F32),
        compiler_params=_params(("parallel",)), name=name)(proj, conv_w)


def _gdn_qkv_bwd(proj, conv_w, dqkv, nh, *, name):
    S = proj.shape[0]
    GW = nh * HEAD_DIM
    scale = HEAD_DIM ** -0.5

    def body(x_ref, w_ref, d_ref, dx_ref, dw_ref):
        sec = pl.program_id(0) // nh
        x, w = x_ref[...], w_ref[...]
        c = _dwconv(x, w)
        sg = _sigmoid(c)
        s = c * sg
        r = lax.rsqrt(jnp.sum(s * s, axis=1, keepdims=True) + EPS)
        sh = s * r
        d = d_ref[...]
        fac = jnp.where(sec == 0, scale, 1.0)
        dn = (r * fac) * (d - sh * jnp.sum(d * sh, axis=1, keepdims=True))
        ds = jnp.where(sec == 2, d, dn)
        dx, dw = _dwconv_bwd(x, w, ds * (sg * (1.0 + c * (1.0 - sg))))
        dx_ref[...] = dx.astype(dx_ref.dtype)
        dw_ref[...] = dw

    return pl.pallas_call(
        body, grid=(3 * nh,),
        in_specs=[pl.BlockSpec((S, HEAD_DIM), lambda j: (0, 3 * nh + j)), pl.BlockSpec((SHORT_CONV, HEAD_DIM), lambda j: (0, j)),
                  pl.BlockSpec((None, S, HEAD_DIM), lambda j: (j // nh, 0, j % nh))],
        out_specs=[pl.BlockSpec((S, HEAD_DIM), lambda j: (0, j)), pl.BlockSpec((SHORT_CONV, HEAD_DIM), lambda j: (0, j))],
        out_shape=[jax.ShapeDtypeStruct((S, 3 * GW), BF16), jax.ShapeDtypeStruct((SHORT_CONV, 3 * GW), F32)],
        compiler_params=_params(("parallel",)), name=name)(proj, conv_w, dqkv)


def _gdn_gates_fwd(proj, ab, nh, *, name):
    S = proj.shape[0]
    C = CHUNK

    def body(x_ref, ab_ref, o_ref):
        ri = lax.broadcasted_iota(jnp.int32, (C, C), 0)
        ci = lax.broadcasted_iota(jnp.int32, (C, C), 1)
        ltri = (ri >= ci).astype(F32)
        lane = lax.broadcasted_iota(jnp.int32, (C, LANES), 1)
        a_coef = -jnp.exp(ab_ref[0:1, :])
        dt = ab_ref[1:2, :]

        def chunk(n, _):
            rows = pl.ds(pl.multiple_of(n * C, C), C)
            x = x_ref[rows, :]
            beta = _sigmoid(x)
            g = jnp.where(jnp.logical_and(lane >= nh, lane < 2 * nh), a_coef * _softplus(x + dt), 0.0)
            gc = _dot_hi(ltri, pltpu.roll(g, nh, 1))
            o_ref[rows, :] = jnp.where(lane < nh, beta, g) + gc
            return 0

        lax.fori_loop(0, S // C, chunk, 0)

    return pl.pallas_call(
        body, grid=(1,),
        in_specs=[pl.BlockSpec((S, LANES), lambda i: (0, 7 * nh)), pl.BlockSpec((2, LANES), lambda i: (0, 0))],
        out_specs=pl.BlockSpec((S, LANES), lambda i: (0, 0)), out_shape=jax.ShapeDtypeStruct((S, LANES), F32),
        compiler_params=_params(("arbitrary",)), name=name)(proj, ab)


def _gdn_gates_bwd(proj, ab, dgt, nh, *, name):
    S = proj.shape[0]
    C = CHUNK

    def body(x_ref, ab_ref, d_ref, dx_ref, dab_ref):
        ri = lax.broadcasted_iota(jnp.int32, (C, C), 0)
        ci = lax.broadcasted_iota(jnp.int32, (C, C), 1)
        utri = (ri <= ci).astype(F32)
        lane = lax.broadcasted_iota(jnp.int32, (C, LANES), 1)
        is_b = lane < nh
        is_a = jnp.logical_and(lane >= nh, lane < 2 * nh)
        a_coef = -jnp.exp(ab_ref[0:1, :])
        dt = ab_ref[1:2, :]

        def chunk(n, carry):
            da_log, ddt = carry
            rows = pl.ds(pl.multiple_of(n * C, C), C)
            x = x_ref[rows, :]
            d = d_ref[rows, :]
            beta = _sigmoid(x)
            dg = pltpu.roll(_dot_hi(utri, jnp.where(lane >= 2 * nh, d, 0.0)), LANES - nh, 1)
            dg = jnp.where(is_a, dg, 0.0)
            dxa = dg * a_coef * _sigmoid(x + dt)
            dxb = jnp.where(is_b, d * beta * (1.0 - beta), 0.0)
            dx_ref[rows, :] = (dxa + dxb).astype(dx_ref.dtype)
            da_log = da_log + jnp.sum(dg * a_coef * _softplus(x + dt), axis=0, keepdims=True)
            return da_log, ddt + jnp.sum(dxa, axis=0, keepdims=True)

        zero = jnp.zeros((1, LANES), F32)
        da_log, ddt = lax.fori_loop(0, S // C, chunk, (zero, zero))
        dab_ref[0:1, :] = da_log
        dab_ref[1:2, :] = ddt

    return pl.pallas_call(
        body, grid=(1,),
        in_specs=[pl.BlockSpec((S, LANES), lambda i: (0, 7 * nh)), pl.BlockSpec((2, LANES), lambda i: (0, 0)),
                  pl.BlockSpec((S, LANES), lambda i: (0, 0))],
        out_specs=[pl.BlockSpec((S, LANES), lambda i: (0, 0)), pl.BlockSpec((2, LANES), lambda i: (0, 0))],
        out_shape=[jax.ShapeDtypeStruct((S, LANES), BF16), jax.ShapeDtypeStruct((2, LANES), F32)],
        compiler_params=_params(("arbitrary",)), name=name)(proj, ab, dgt)


def _unit_lower_inverse(lmat):
    C = lmat.shape[-1]
    ri = lax.broadcasted_iota(jnp.int32, lmat.shape, lmat.ndim - 2)
    ci = lax.broadcasted_iota(jnp.int32, lmat.shape, lmat.ndim - 1)
    nmat = -lmat
    p = jnp.where(ri == ci, 1.0, 0.0) + nmat
    for _ in range(int(math.log2(C)) - 1):
        nmat = _dot_hi(nmat, nmat)
        p = p + _dot_hi(p, nmat)
    return p


def _gdn_chunk_common(q, k, v, gates, gc_row, h, nh, tinv=None):
    C = CHUNK
    lane = lax.broadcasted_iota(jnp.int32, gates.shape, gates.ndim - 1)
    beta = jnp.sum(jnp.where(lane == h, gates, 0.0), axis=-1, keepdims=True)
    gc = jnp.sum(jnp.where(lane == 2 * nh + h, gates, 0.0), axis=-1, keepdims=True)
    sq = gates.shape[:-1] + (C,)
    ri = lax.broadcasted_iota(jnp.int32, sq, len(sq) - 2)
    ci = lax.broadcasted_iota(jnp.int32, sq, len(sq) - 1)
    incl, strict = ri >= ci, ri > ci
    decay = jnp.where(incl, jnp.exp(jnp.where(incl, gc - gc_row, 0.0)), 0.0)
    egc = jnp.exp(gc)
    kb, vb = k * beta, v * beta
    lmat = jnp.where(strict, _dot(kb, k, _NT) * decay, 0.0)
    kbg = kb * egc
    u = w = None
    if tinv is None:
        tinv = _unit_lower_inverse(lmat)
        u = _dot(tinv, vb)
        w = _dot(tinv, kbg)
    amat = _dot(q, k, _NT) * decay
    glast = gc[..., C - 1:C, :]
    ekt = jnp.exp(glast - gc)
    return dict(q=q, k=k, v=v, beta=beta, decay=decay, egc=egc, kb=kb, vb=vb, lmat=lmat, tinv=tinv, kbg=kbg, u=u, w=w,
                amat=amat, qd=q * egc, ekt=ekt, kt=k * ekt, cd=jnp.exp(glast), strict=strict, incl=incl)


def _gdn_chunk_specs(nh, S, nc):
    return [pl.BlockSpec((3, S, HEAD_DIM), lambda h: (0, 0, h)),
            pl.BlockSpec((S, LANES), lambda h: (0, 0)),
            pl.BlockSpec((None, nc, 1, CHUNK), lambda h: (h, 0, 0, 0))]


def _gdn_state_free(qkv_ref, gates_ref, gr_ref, g, nb, h, nh):
    C = CHUNK
    rows = pl.ds(pl.multiple_of(g * (nb * C), nb * C), nb * C)
    part = lambda x: x.reshape(nb, C, x.shape[-1])
    return rows, _gdn_chunk_common(part(qkv_ref[0, rows, :]), part(qkv_ref[1, rows, :]), part(qkv_ref[2, rows, :]),
                                   part(gates_ref[rows, :]), gr_ref[pl.ds(g * nb, nb)], h, nh)


def _gdn_chunk_fwd(qkv, gates, gc_row, *, name):
    _, S, GW = qkv.shape
    nh, C = GW // HEAD_DIM, CHUNK
    nc = S // C
    nb = min(GDN_CPB, nc)
    flat = lambda x: x.reshape(nb * C, x.shape[-1])

    def body(qkv_ref, gates_ref, gr_ref, o_ref, st_ref, u_s, w_s, a_s, qd_s, kt_s, cd_s):
        h = pl.program_id(0)

        def group(g, _):
            rows, m = _gdn_state_free(qkv_ref, gates_ref, gr_ref, g, nb, h, nh)
            u_s[rows, :] = flat(m["u"])
            w_s[rows, :] = flat(m["w"])
            a_s[rows, :] = flat(m["amat"])
            qd_s[rows, :] = flat(m["qd"])
            kt_s[rows, :] = flat(m["kt"])
            cd_s[pl.ds(g * nb, nb)] = jnp.broadcast_to(m["cd"], (nb, 8, LANES))
            return 0

        lax.fori_loop(0, nc // nb, group, 0)

        def chunk(n, s0):
            rows = pl.ds(pl.multiple_of(n * C, C), C)
            st_ref[n] = s0
            v_new = u_s[rows, :] - _dot(w_s[rows, :], s0)
            o_ref[rows, :] = _dot(qd_s[rows, :], s0) + _dot(a_s[rows, :], v_new)
            return s0 * cd_s[n][0:1, :] + _dot(kt_s[rows, :], v_new, _TN)

        lax.fori_loop(0, nc, chunk, jnp.zeros((HEAD_DIM, HEAD_DIM), F32))

    seq = pltpu.VMEM((S, HEAD_DIM), F32)
    return pl.pallas_call(
        body, grid=(nh,), in_specs=_gdn_chunk_specs(nh, S, nc),
        out_specs=[pl.BlockSpec((S, HEAD_DIM), lambda h: (0, h)),
                   pl.BlockSpec((None, nc, HEAD_DIM, HEAD_DIM), lambda h: (h, 0, 0, 0))],
        out_shape=[jax.ShapeDtypeStruct((S, GW), F32), jax.ShapeDtypeStruct((nh, nc, HEAD_DIM, HEAD_DIM), F32)],
        scratch_shapes=[seq, seq, pltpu.VMEM((S, C), F32), seq, seq, pltpu.VMEM((nc, 8, LANES), F32)],
        compiler_params=_params(("parallel",)), name=name)(qkv, gates, gc_row)


def _gdn_chunk_bwd(qkv, gates, gc_row, states, do, *, name):
    _, S, GW = qkv.shape
    nh, C = GW // HEAD_DIM, CHUNK
    nc = S // C
    nb = min(GDN_CPB, nc)
    flat = lambda x: x.reshape(nb * C, x.shape[-1])
    part = lambda x: x.reshape(nb, C, x.shape[-1])

    def body(qkv_ref, gates_ref, gr_ref, st_ref, do_ref, dqkv_ref, dgt_ref,
             t_s, vn_s, w_s, a_s, qd_s, kt_s, cd_s, dvn_s, dkt_s, dcd_s):
        h = pl.program_id(0)

        def group(g, _):
            rows, m = _gdn_state_free(qkv_ref, gates_ref, gr_ref, g, nb, h, nh)
            t_s[rows, :] = flat(m["tinv"])
            vn_s[rows, :] = flat(m["u"])
            w_s[rows, :] = flat(m["w"])
            a_s[rows, :] = flat(m["amat"])
            qd_s[rows, :] = flat(m["qd"])
            kt_s[rows, :] = flat(m["kt"])
            cd_s[pl.ds(g * nb, nb)] = jnp.broadcast_to(m["cd"], (nb, 8, LANES))
            return 0

        lax.fori_loop(0, nc // nb, group, 0)

        def chunk(t, dsn):
            n = nc - 1 - t
            rows = pl.ds(pl.multiple_of(n * C, C), C)
            s0, dout, w = st_ref[n], do_ref[rows, :], w_s[rows, :]
            v_new = vn_s[rows, :] - _dot(w, s0)
            dvn = _dot(a_s[rows, :], dout, _TN) + _dot(kt_s[rows, :], dsn)
            vn_s[rows, :] = v_new
            dvn_s[rows, :] = dvn
            dkt_s[rows, :] = _dot(v_new, dsn, _NT)
            dcd_s[n] = jnp.zeros((8, LANES), F32) + jnp.sum(dsn * s0)
            return _dot(qd_s[rows, :], dout, _TN) + dsn * cd_s[n][0:1, :] - _dot(w, dvn, _TN)

        lax.fori_loop(0, nc, chunk, jnp.zeros((HEAD_DIM, HEAD_DIM), F32))

        def rest(g, _):
            rows, m = _gdn_state_free_again(qkv_ref, gates_ref, gr_ref, t_s, g, nb, h, nh)
            chunks = pl.ds(g * nb, nb)
            dq, dk, dv, dgt = _gdn_chunk_grad(m, st_ref[chunks], part(vn_s[rows, :]), part(dvn_s[rows, :]),
                                              part(dkt_s[rows, :]), dcd_s[chunks][:, 0:1, 0:1], part(do_ref[rows, :]), h, nh)
            dqkv_ref[0, rows, :] = flat(dq)
            dqkv_ref[1, rows, :] = flat(dk)
            dqkv_ref[2, rows, :] = flat(dv)
            dgt_ref[rows, :] = flat(dgt)
            return 0

        lax.fori_loop(0, nc // nb, rest, 0)

    seq = pltpu.VMEM((S, HEAD_DIM), F32)
    small = pltpu.VMEM((nc, 8, LANES), F32)
    return pl.pallas_call(
        body, grid=(nh,),
        in_specs=_gdn_chunk_specs(nh, S, nc) + [
            pl.BlockSpec((None, nc, HEAD_DIM, HEAD_DIM), lambda h: (h, 0, 0, 0)),
            pl.BlockSpec((S, HEAD_DIM), lambda h: (0, h))],
        out_specs=[pl.BlockSpec((3, S, HEAD_DIM), lambda h: (0, 0, h)), pl.BlockSpec((None, S, LANES), lambda h: (h, 0, 0))],
        out_shape=[jax.ShapeDtypeStruct((3, S, GW), F32), jax.ShapeDtypeStruct((nh, S, LANES), F32)],
        scratch_shapes=[pltpu.VMEM((S, C), F32), seq, seq, pltpu.VMEM((S, C), F32), seq, seq, small, seq, seq, small],
        compiler_params=_params(("parallel",)), name=name)(qkv, gates, gc_row, states, do)


def _gdn_state_free_again(qkv_ref, gates_ref, gr_ref, t_s, g, nb, h, nh):
    C = CHUNK
    rows = pl.ds(pl.multiple_of(g * (nb * C), nb * C), nb * C)
    part = lambda x: x.reshape(nb, C, x.shape[-1])
    m = _gdn_chunk_common(part(qkv_ref[0, rows, :]), part(qkv_ref[1, rows, :]), part(qkv_ref[2, rows, :]),
                          part(gates_ref[rows, :]), gr_ref[pl.ds(g * nb, nb)], h, nh, tinv=part(t_s[rows, :]))
    return rows, m


def _gdn_chunk_grad(m, s0, v_new, dvn, dkt, dcd, dout, h, nh):
    C = CHUNK
    q, k, v, beta, decay, egc = m["q"], m["k"], m["v"], m["beta"], m["decay"], m["egc"]
    tinv, kt, cd = m["tinv"], m["kt"], m["cd"]
    dqd = _dot(dout, s0, _NT)
    damat = jnp.where(m["incl"], _dot(dout, v_new, _NT), 0.0)
    dw = -_dot(dvn, s0, _NT)
    dvb = _dot(tinv, dvn, _TN)
    dkbg = _dot(tinv, dw, _TN)
    dtinv = _dot(dvn, m["vb"], _NT) + _dot(dw, m["kbg"], _NT)
    dl = jnp.where(m["strict"], -_dot_hi(_dot_hi(tinv, dtinv, _TN), tinv, _NT), 0.0)
    dkk = dl * decay
    dqk = damat * decay
    dkb = _dot(dkk, k) + dkbg * egc
    dk = _dot(dkk, m["kb"], _TN) + _dot(dqk, q, _TN) + dkt * m["ekt"] + dkb * beta
    dq = _dot(dqk, k) + dqd * egc
    mm = dl * m["lmat"] + damat * m["amat"]
    ones = jnp.ones(q.shape, F32)
    rk = jnp.sum(dkt * kt, axis=-1, keepdims=True)
    dgc = (_dot_hi(mm, ones) - _dot_hi(mm, ones, _TN) + jnp.sum(dqd * m["qd"], axis=-1, keepdims=True) - rk
           + jnp.sum(dkbg * m["kbg"], axis=-1, keepdims=True))
    dglast = jnp.sum(rk, axis=-2, keepdims=True) + dcd * cd
    rowi = lax.broadcasted_iota(jnp.int32, q.shape, q.ndim - 2)
    lane = lax.broadcasted_iota(jnp.int32, q.shape, q.ndim - 1)
    dgc = dgc + jnp.where(rowi == C - 1, dglast, 0.0)
    dbeta = jnp.sum(dkb * k, axis=-1, keepdims=True) + jnp.sum(dvb * v, axis=-1, keepdims=True)
    dgt = jnp.where(lane == h, dbeta, 0.0) + jnp.where(lane == 2 * nh + h, dgc, 0.0)
    return dq, dk, dvb * beta, dgt


def _gdn_post_fwd(o, proj, ng, nh, *, name):
    S, GW = o.shape

    def body(o_ref, z_ref, g_ref, y_ref):
        ov, z = o_ref[...], z_ref[...]
        rstd = lax.rsqrt(jnp.mean(ov * ov, axis=-1, keepdims=True) + EPS)
        y_ref[...] = (ov * rstd * g_ref[...] * (z * _sigmoid(z))).astype(y_ref.dtype)

    blk = pl.BlockSpec((S, HEAD_DIM), lambda h: (0, h))
    return pl.pallas_call(
        body, grid=(nh,), in_specs=[blk, pl.BlockSpec((S, HEAD_DIM), lambda h: (0, 6 * nh + h)), pl.BlockSpec((1, HEAD_DIM), lambda h: (0, 0))],
        out_specs=blk, out_shape=jax.ShapeDtypeStruct((S, GW), BF16),
        compiler_params=_params(("parallel",)), name=name)(o, proj, ng)


def _gdn_post_bwd(o, proj, ng, dmix, nh, *, name):
    S, GW = o.shape

    def body(o_ref, z_ref, g_ref, d_ref, do_ref, dz_ref, dg_ref):
        ov, z, d = o_ref[...], z_ref[...], d_ref[...].astype(F32)
        rstd = lax.rsqrt(jnp.mean(ov * ov, axis=-1, keepdims=True) + EPS)
        oh = ov * rstd
        sz = _sigmoid(z)
        dy = d * (z * sz)
        dz_ref[...] = (d * (oh * g_ref[...]) * (sz * (1.0 + z * (1.0 - sz)))).astype(dz_ref.dtype)
        t = dy * g_ref[...]
        do_ref[...] = rstd * (t - oh * jnp.mean(t * oh, axis=-1, keepdims=True))
        part = jnp.sum(dy * oh, axis=0, keepdims=True)

        @pl.when(pl.program_id(0) == 0)
        def _():
            dg_ref[...] = part

        @pl.when(pl.program_id(0) > 0)
        def _():
            dg_ref[...] += part

    blk = pl.BlockSpec((S, HEAD_DIM), lambda h: (0, h))
    vec = pl.BlockSpec((1, HEAD_DIM), lambda h: (0, 0))
    return pl.pallas_call(
        body, grid=(nh,),
        in_specs=[blk, pl.BlockSpec((S, HEAD_DIM), lambda h: (0, 6 * nh + h)), vec, pl.BlockSpec((S, HEAD_DIM), lambda h: (0, nh + h))],
        out_specs=[blk, blk, vec],
        out_shape=[jax.ShapeDtypeStruct((S, GW), F32), jax.ShapeDtypeStruct((S, GW), BF16), jax.ShapeDtypeStruct((1, HEAD_DIM), F32)],
        compiler_params=_params(("arbitrary",)), name=name)(o, proj, ng, dmix)


def _gdn_forward(proj, conv_w, ab, ng, nh, tag):
    S = proj.shape[0]
    nc = S // CHUNK
    qkv = _gdn_qkv_fwd(proj, conv_w, nh, name=f"gdn_qkv_fwd{tag}")
    gates = _gdn_gates_fwd(proj, ab, nh, name=f"gdn_gates_fwd{tag}")
    gc_row = gates[:, 2 * nh:3 * nh].T.reshape(nh, nc, 1, CHUNK)
    o, states = _gdn_chunk_fwd(qkv, gates, gc_row, name=f"gdn_chunk_fwd{tag}")
    y = _gdn_post_fwd(o, proj, ng, nh, name=f"gdn_post_fwd{tag}")
    return y, (qkv, gates, gc_row, states, o)


def _gdn_backward(proj, conv_w, ab, ng, saved, dmix, nh, tag):
    qkv, gates, gc_row, states, o = saved
    do, dz, dng = _gdn_post_bwd(o, proj, ng, dmix, nh, name=f"gdn_post_bwd{tag}")
    dqkv, dgt_heads = _gdn_chunk_bwd(qkv, gates, gc_row, states, do, name=f"gdn_chunk_bwd{tag}")
    dx_qkv, dconv = _gdn_qkv_bwd(proj, conv_w, dqkv, nh, name=f"gdn_qkv_bwd{tag}")
    dx_g, dab = _gdn_gates_bwd(proj, ab, jnp.sum(dgt_heads, axis=0), nh, name=f"gdn_gates_bwd{tag}")
    return dx_qkv, dz, dx_g, dconv, dab, dng


def _row_tile(r, cap=128):
    t = cap
    while r % t:
        t //= 2
    assert t >= 8, r
    return t


def _adamw_update(gv, w_ref, m_ref, v_ref, d_ref, m2_ref, v2_ref):
    m2 = ADAM_B1 * m_ref[...] + (1.0 - ADAM_B1) * gv
    v2 = ADAM_B2 * v_ref[...] + (1.0 - ADAM_B2) * (gv * gv)
    m_hat = m2 / (1.0 - ADAM_B1 ** ADAM_STEP)
    v_hat = v2 / (1.0 - ADAM_B2 ** ADAM_STEP)
    d_ref[...] = -ADAM_LR * (m_hat / (jnp.sqrt(v_hat) + ADAM_EPS) + ADAM_WD * w_ref[...])
    m2_ref[...] = m2
    v2_ref[...] = v2


def _adamw(w, g, m, v, *, name):
    L, r, c = w.shape
    tr = _row_tile(r)

    def body(w_ref, g_ref, m_ref, v_ref, d_ref, m2_ref, v2_ref):
        _adamw_update(g_ref[...], w_ref, m_ref, v_ref, d_ref, m2_ref, v2_ref)

    blk = pl.BlockSpec((None, tr, c), lambda l, i: (l, i, 0))
    o = jax.ShapeDtypeStruct(w.shape, F32)
    return pl.pallas_call(
        body, grid=(L, r // tr), in_specs=[blk] * 4, out_specs=[blk] * 3, out_shape=[o, o, o],
        compiler_params=_params(("parallel", "parallel")), name=name)(w, g, m, v)


def _adamw_halves(w, g_own, g_sib, cvec, m, v, *, name):
    L, r, c = w.shape
    tr = _row_tile(r // 2)
    nbh = (r // 2) // tr

    def body(c_ref, w_ref, go_ref, gs_ref, m_ref, v_ref, g_out, d_ref, m2_ref, v2_ref):
        gv = jnp.where(pl.program_id(1) // nbh == c_ref[0], go_ref[...], gs_ref[...])
        g_out[...] = gv
        _adamw_update(gv, w_ref, m_ref, v_ref, d_ref, m2_ref, v2_ref)

    lo = lambda i: jnp.minimum(i, nbh - 1)
    hi = lambda i: jnp.maximum(i - nbh, 0)
    blk = pl.BlockSpec((None, tr, c), lambda l, i, c_ref: (l, i, 0))
    own = pl.BlockSpec((None, tr, c), lambda l, i, c_ref: (l, jnp.where(c_ref[0] == 0, lo(i), hi(i)), 0))
    sib = pl.BlockSpec((None, tr, c), lambda l, i, c_ref: (l, jnp.where(c_ref[0] == 0, hi(i), lo(i)), 0))
    o = jax.ShapeDtypeStruct(w.shape, F32)
    return pl.pallas_call(
        body,
        grid_spec=pltpu.PrefetchScalarGridSpec(
            num_scalar_prefetch=1, grid=(L, r // tr), in_specs=[blk, own, sib, blk, blk], out_specs=[blk] * 4),
        out_shape=[o, o, o, o],
        compiler_params=_params(("parallel", "arbitrary")), name=name)(cvec, w, g_own, g_sib, m, v)


def _sum_partials(p, rb, kc, prev, l, nl, *, name):
    _, _, h, c = rb.shape
    tr = _row_tile(h, 256)
    nb = h // tr
    others = [(0, 1), (1, 0), (1, 1), (2, 0), (2, 1), (3, 0), (3, 1)]

    def body(kc_ref, p_ref, *rest):
        acc = p_ref[...].astype(F32)
        for r_ref in rest[:7]:
            acc = acc + r_ref[...].astype(F32)
        rest[-1][...] = acc

    def slot(ds, dc):
        return pl.BlockSpec((None, None, tr, c), lambda i, kc_ref: ((kc_ref[0] + ds) % 4, (kc_ref[1] + dc) % 2, i, 0))

    in_specs = [pl.BlockSpec((None, tr, c), lambda i, kc_ref: (kc_ref[0], kc_ref[1] * nb + i, 0))]
    in_specs += [slot(ds, dc) for ds, dc in others]
    args = [kc, p] + [rb] * 7
    if prev is not None:
        in_specs.append(pl.BlockSpec(memory_space=pltpu.HBM))
        args.append(prev)
    return pl.pallas_call(
        body,
        grid_spec=pltpu.PrefetchScalarGridSpec(
            num_scalar_prefetch=1, grid=(nb,), in_specs=in_specs,
            out_specs=pl.BlockSpec((None, tr, c), lambda i, kc_ref: (l, i, 0))),
        out_shape=jax.ShapeDtypeStruct((nl, h, c), F32), input_output_aliases={9: 0} if prev is not None else {},
        compiler_params=_params(("parallel",)), name=name)(*args)


_MESH = pl.DeviceIdType.MESH
_HBM = pl.BlockSpec(memory_space=pltpu.HBM)


def _place():
    x, y, c = lax.axis_index("x"), lax.axis_index("y"), lax.axis_index("c")
    return x, y, c, [(1 - x, y), (x, 1 - y), (1 - x, 1 - y)]


def _cast_place(w, l, kvec, *, name):
    _, r, c = w.shape
    tr = _row_tile(r, 256)

    def body(k_ref, w_ref, o_ref):
        o_ref[...] = w_ref[...].astype(o_ref.dtype)

    return pl.pallas_call(
        body,
        grid_spec=pltpu.PrefetchScalarGridSpec(
            num_scalar_prefetch=1, grid=(r // tr,),
            in_specs=[pl.BlockSpec((None, tr, c), lambda i, k_ref: (l, i, 0))],
            out_specs=pl.BlockSpec((None, tr, c), lambda i, k_ref: (k_ref[0], i, 0))),
        out_shape=jax.ShapeDtypeStruct((4, r, c), BF16),
        compiler_params=_params(("parallel",)), name=name)(kvec, w)


_SEM = pl.BlockSpec(memory_space=pltpu.SEMAPHORE)
_ANY = pl.BlockSpec(memory_space=pl.ANY)
_EFFECT = pltpu.SideEffectType.DATAFLOW_SIDE_EFFECTING


def _in_hbm(a):
    return pltpu.with_memory_space_constraint(a, pltpu.HBM)


def _gather_copies(w_refs, send, recv, landing):
    x, y, c, chips = _place()
    k = 2 * x + y
    cps = []
    for a, w in enumerate(w_refs):
        h = w.shape[1] // 2
        for j, (cx, cy) in enumerate(chips):
            cps.append(pltpu.make_async_remote_copy(
                src_ref=w.at[k, pl.ds(c * h, h), :], dst_ref=w.at[2 * cx + cy if landing else k, pl.ds(c * h, h), :],
                send_sem=send.at[3 * a + j], recv_sem=recv.at[3 * a + j], device_id=(cx, cy, c), device_id_type=_MESH))
    return cps


def _gather_start(ws, after, *, name):
    n = len(ws)

    def body(*refs):
        send, recv = refs[n + 1], refs[n + 2]
        o_refs, token = refs[n + 3:2 * n + 3], refs[2 * n + 3]
        for cp in _gather_copies(o_refs, send, recv, landing=False):
            cp.start()
        token[...] = jnp.zeros_like(token)

    out = pl.pallas_call(
        body, in_specs=[_HBM] * n + [_ANY], out_specs=[_SEM, _SEM] + [_HBM] * n + [pl.BlockSpec(memory_space=pltpu.VMEM)],
        out_shape=[pltpu.SemaphoreType.DMA((3 * n,)), pltpu.SemaphoreType.DMA((3 * n,))]
        + [pltpu.HBM(w.shape, w.dtype) for w in ws] + [jax.ShapeDtypeStruct((8, LANES), F32)],
        input_output_aliases={a: 2 + a for a in range(n)},
        compiler_params=pltpu.CompilerParams(has_side_effects=_EFFECT), name=name)(*[_in_hbm(w) for w in ws], after)
    return out[0], out[1], list(out[2:2 + n]), out[2 + n]


def _gather_wait(send, recv, ws, after, *, name):
    n = len(ws)

    def body(*refs):
        for cp in _gather_copies(refs[:n], refs[n], refs[n + 1], landing=True):
            cp.wait_send()
            cp.wait_recv()

    return list(pl.pallas_call(
        body, in_specs=[_HBM] * n + [_SEM, _SEM, _ANY], out_specs=[_HBM] * n,
        out_shape=[pltpu.HBM(w.shape, w.dtype) for w in ws], input_output_aliases={a: a for a in range(n)},
        compiler_params=pltpu.CompilerParams(has_side_effects=_EFFECT), name=name)(*ws, send, recv, after))


def _gather_to_sibling(ws, *, name):
    n = len(ws)

    def body(*refs):
        o_refs = refs[n:2 * n]
        send, recv = refs[2 * n:]
        x, y, c, chips = _place()
        cps = []
        for a in range(n):
            h = o_refs[a].shape[1] // 2
            for j, (cx, cy) in enumerate(chips):
                landed = o_refs[a].at[2 * cx + cy, pl.ds(c * h, h), :]
                cp = pltpu.make_async_remote_copy(
                    src_ref=landed, dst_ref=landed, send_sem=send.at[3 * a + j], recv_sem=recv.at[3 * a + j],
                    device_id=(x, y, 1 - c), device_id_type=_MESH)
                cp.start()
                cps.append(cp)
        for a in range(n):
            h = o_refs[a].shape[1] // 2
            for j, (cx, cy) in enumerate(chips):
                other = o_refs[a].at[2 * cx + cy, pl.ds((1 - c) * h, h), :]
                pltpu.make_async_remote_copy(
                    src_ref=other, dst_ref=other, send_sem=send.at[3 * a + j], recv_sem=recv.at[3 * a + j],
                    device_id=(x, y, c), device_id_type=_MESH).wait_recv()
        for cp in cps:
            cp.wait_send()

    return list(pl.pallas_call(
        body, in_specs=[_HBM] * n, out_specs=[_HBM] * n,
        out_shape=[jax.ShapeDtypeStruct(w.shape, w.dtype) for w in ws],
        input_output_aliases={a: a for a in range(n)},
        scratch_shapes=[pltpu.SemaphoreType.DMA((3 * n,))] * 2, name=name)(*ws))


def _reduce_copies(p_refs, r_refs, send, recv, landing):
    x, y, c = lax.axis_index("x"), lax.axis_index("y"), lax.axis_index("c")
    k = 2 * x + y
    cps = []
    for a, (p, r) in enumerate(zip(p_refs, r_refs)):
        h = p.shape[1] // 2
        for d in range(1, 8):
            px = 1 - x if d & 4 else x
            py = 1 - y if d & 2 else y
            pc = 1 - c if d & 1 else c
            cps.append(pltpu.make_async_remote_copy(
                src_ref=p.at[2 * px + py, pl.ds(pc * h, h), :], dst_ref=r.at[2 * px + py, pc] if landing else r.at[k, c],
                send_sem=send.at[7 * a + d - 1], recv_sem=recv.at[7 * a + d - 1], device_id=(px, py, pc),
                device_id_type=_MESH))
    return cps


def _reduce_start(ps, after, *, name):
    n = len(ps)
    lands = [lax.empty((4, 2, p.shape[1] // 2, p.shape[2]), p.dtype) for p in ps]

    def body(*refs):
        send, recv = refs[2 * n + 1], refs[2 * n + 2]
        p_out, r_out, token = refs[2 * n + 3:3 * n + 3], refs[3 * n + 3:4 * n + 3], refs[4 * n + 3]
        for cp in _reduce_copies(p_out, r_out, send, recv, landing=False):
            cp.start()
        token[...] = jnp.zeros_like(token)

    out = pl.pallas_call(
        body, in_specs=[_HBM] * (2 * n) + [_ANY],
        out_specs=[_SEM, _SEM] + [_HBM] * (2 * n) + [pl.BlockSpec(memory_space=pltpu.VMEM)],
        out_shape=[pltpu.SemaphoreType.DMA((7 * n,)), pltpu.SemaphoreType.DMA((7 * n,))]
        + [pltpu.HBM(a.shape, a.dtype) for a in list(ps) + lands] + [jax.ShapeDtypeStruct((8, LANES), F32)],
        input_output_aliases={a: 2 + a for a in range(2 * n)},
        compiler_params=pltpu.CompilerParams(has_side_effects=_EFFECT), name=name)(
            *[_in_hbm(p) for p in ps], *[_in_hbm(r) for r in lands], after)
    return out[0], out[1], list(out[2:2 + n]), list(out[2 + n:2 + 2 * n]), out[2 + 2 * n]


def _reduce_wait(send, recv, ps, lands, after, *, name):
    n = len(ps)

    def body(*refs):
        for cp in _reduce_copies(refs[:n], refs[n:2 * n], refs[2 * n], refs[2 * n + 1], landing=True):
            cp.wait_send()
            cp.wait_recv()

    out = pl.pallas_call(
        body, in_specs=[_HBM] * (2 * n) + [_SEM, _SEM, _ANY], out_specs=[_HBM] * (2 * n),
        out_shape=[pltpu.HBM(a.shape, a.dtype) for a in list(ps) + list(lands)],
        input_output_aliases={a: a for a in range(2 * n)},
        compiler_params=pltpu.CompilerParams(has_side_effects=_EFFECT), name=name)(*ps, *lands, send, recv, after)
    return list(out[:n]), list(out[n:])


def _swap_with_sibling(gs, *, name):
    n = len(gs)

    def body(*refs):
        g_refs, o_refs = refs[:n], refs[n:2 * n]
        send, recv = refs[2 * n:]
        x, y, c, _ = _place()
        cps = []
        for a in range(n):
            cp = pltpu.make_async_remote_copy(
                src_ref=g_refs[a], dst_ref=o_refs[a], send_sem=send.at[a], recv_sem=recv.at[a],
                device_id=(x, y, 1 - c), device_id_type=_MESH)
            cp.start()
            cps.append(cp)
        for cp in cps:
            cp.wait_recv()
        for cp in cps:
            cp.wait_send()

    return pl.pallas_call(
        body, in_specs=[_HBM] * n, out_specs=[_HBM] * n,
        out_shape=[jax.ShapeDtypeStruct(g.shape, g.dtype) for g in gs],
        scratch_shapes=[pltpu.SemaphoreType.DMA((n,))] * 2, name=name)(*gs)


def _allreduce_small(v, *, name):
    R = v.shape[0]

    def body(v_ref, o_ref, buf, send, recv, loc):
        x, y, c = lax.axis_index("x"), lax.axis_index("y"), lax.axis_index("c")
        me = 4 * x + 2 * y + c
        mine = pltpu.make_async_copy(v_ref, buf.at[me], loc)
        mine.start()
        cps = []
        for d in range(1, 8):
            px = 1 - x if d & 4 else x
            py = 1 - y if d & 2 else y
            pc = 1 - c if d & 1 else c
            cp = pltpu.make_async_remote_copy(
                src_ref=v_ref, dst_ref=buf.at[me], send_sem=send.at[d - 1], recv_sem=recv.at[d - 1],
                device_id=(px, py, pc), device_id_type=_MESH)
            cp.start()
            cps.append((cp, 4 * px + 2 * py + pc))
        for d in range(1, 8):
            cp, peer = cps[d - 1]
            pltpu.make_async_remote_copy(
                src_ref=buf.at[peer], dst_ref=buf.at[peer], send_sem=send.at[d - 1], recv_sem=recv.at[d - 1],
                device_id=(x, y, c), device_id_type=_MESH).wait_recv()
        for cp, _ in cps:
            cp.wait_send()
        mine.wait()
        acc = buf[0]
        for i in range(1, 8):
            acc = acc + buf[i]
        o_ref[...] = acc

    return pl.pallas_call(
        body, in_specs=[pl.BlockSpec(memory_space=pltpu.VMEM)], out_specs=pl.BlockSpec(memory_space=pltpu.VMEM),
        out_shape=jax.ShapeDtypeStruct((R, LANES), F32),
        scratch_shapes=[pltpu.VMEM((8, R, LANES), F32), pltpu.SemaphoreType.DMA((7,)), pltpu.SemaphoreType.DMA((7,)),
                        pltpu.SemaphoreType.DMA],
        compiler_params=pltpu.CompilerParams(vmem_limit_bytes=VMEM_LIMIT), name=name)(v)


def _pack(arrs, row_multiple=8):
    rows = []
    for a in arrs:
        flat = a.reshape(-1)
        flat = jnp.pad(flat, (0, (-flat.shape[0]) % LANES))
        rows.append(flat.reshape(-1, LANES))
    buf = jnp.concatenate(rows, axis=0)
    return jnp.pad(buf, ((0, (-buf.shape[0]) % row_multiple), (0, 0)))


def _unpack(buf, shapes):
    out, r = [], 0
    for s in shapes:
        size = math.prod(s)
        nr = -(-size // LANES)
        out.append(buf[r:r + nr].reshape(-1)[:size].reshape(s))
        r += nr
    return out


def kernel(x, mem, mix_norm, w_in, gdn_conv, gdn_a_log, gdn_dt_bias, gdn_norm, w_out, xattn_norm, mem_norm, w_xq, w_xkv, w_xo, ffn_norm, w_up, ffn_conv, ffn_conv_bias, w_down, final_norm, loss_target, m_mix_norm, m_w_in, m_gdn_conv, m_gdn_a_log, m_gdn_dt_bias, m_gdn_norm, m_w_out, m_xattn_norm, m_mem_norm, m_w_xq, m_w_xkv, m_w_xo, m_ffn_norm, m_w_up, m_ffn_conv, m_ffn_conv_bias, m_w_down, m_final_norm, v_mix_norm, v_w_in, v_gdn_conv, v_gdn_a_log, v_gdn_dt_bias, v_gdn_norm, v_w_out, v_xattn_norm, v_mem_norm, v_w_xq, v_w_xkv, v_w_xo, v_ffn_norm, v_w_up, v_ffn_conv, v_ffn_conv_bias, v_w_down, v_final_norm):
    L = w_in.shape[0]
    _, S, D = x.shape
    nh = D // (2 * HEAD_DIM)
    GW = nh * HEAD_DIM
    n_in = 7 * GW + 2 * nh
    NP = 7 * GW + LANES
    XW = X_HEADS * HEAD_DIM
    F = w_down.shape[1] * 4
    cs_in = w_in.shape[2]
    cs_up = w_up.shape[2]
    cs_xo = w_xo.shape[2]
    tu = _tile(cs_up, 1408)
    per = cs_up // tu
    fper = F // tu
    assert n_in == 4 * cs_in and F % tu == 0 and 2 * F == 4 * cs_up

    xi, yi, ci = lax.axis_index("x"), lax.axis_index("y"), lax.axis_index("c")
    chip = 2 * xi + yi
    cvec = jnp.reshape(ci, (1,)).astype(jnp.int32)

    cs_gc, cs_fc = gdn_conv.shape[2], ffn_conv.shape[2]
    keep = jnp.where(ci == 0, 1.0, 0.0).astype(F32)
    gc_full = lax.dynamic_update_slice(jnp.zeros((L, SHORT_CONV, 4 * cs_gc), F32), gdn_conv * keep, (0, 0, chip * cs_gc))
    fc_full = lax.dynamic_update_slice(jnp.zeros((L, FFN_CONV, 4 * cs_fc), F32), ffn_conv * keep, (0, 0, chip * cs_fc))
    conv_all = _allreduce_small(_pack([gc_full, fc_full]), name="allgather_conv")
    gdn_conv_full, ffn_conv_full = _unpack(conv_all, [gc_full.shape, fc_full.shape])

    big = [w_in, w_out, w_xq, w_xkv, w_xo, w_up, w_down]
    kvec = jnp.reshape(chip, (1,)).astype(jnp.int32)
    kcvec = jnp.stack([chip, ci]).astype(jnp.int32)
    ab = jnp.zeros((L, 2, LANES), F32).at[:, 0, nh:2 * nh].set(gdn_a_log).at[:, 1, nh:2 * nh].set(gdn_dt_bias)

    def vec(p, l):
        return p[l:l + 1]

    xo_fwd_b = pl.BlockSpec((None, XW, cs_xo), lambda i, j, k: (j, 0, 0))
    xo_dg_b = pl.BlockSpec((None, XW, cs_xo), lambda i, j, k: (k, 0, 0))
    xo_wg_o = pl.BlockSpec((None, XW, cs_xo), lambda i, j, k: (j, 0, 0))
    up_fwd_b = pl.BlockSpec((None, D, tu), lambda i, j, k: (j // per, 0, j % per))

    def fwd_layer(l, xc, g_in, rest, token):
        g_mix = vec(mix_norm, l) if token is None else vec(mix_norm, l) + token[0:1, 0:1]
        w_in_l = jnp.concatenate([g_in[0], g_in[1], g_in[2], g_in[3], jnp.zeros((D, NP - n_in), BF16)], axis=1)
        s = dict(x0=xc, w_in=w_in_l)
        s["h"] = _rms_fwd(xc, g_mix, name="rms_mix_fwd")
        s["proj"] = _matmul(s["h"], w_in_l, tn=2432, tk=D, name="mm_in_fwd")
        s["sb"], s["tot"] = _sb_fwd(s["proj"], nh, name="sb_fwd")
        gdn_out, s["gdn"] = _gdn_forward(s["proj"], gdn_conv_full[l], ab[l], vec(gdn_norm, l), nh, "")
        g_out, g_xq, g_xkv, g_xo, g_up, g_down = rest(gdn_out)
        w_out_l, w_xq_l, w_xkv_l, w_down_l = g_out.reshape(2 * GW, D), g_xq.reshape(D, XW), g_xkv.reshape(D, 2 * XW), g_down.reshape(F, D)
        s.update(w_out=w_out_l, w_xq=w_xq_l, w_xkv=w_xkv_l, w_xo=g_xo, w_up=g_up, w_down=w_down_l)
        s["mixed"] = jnp.concatenate([s["sb"], gdn_out], axis=1)
        s["x1"] = _matmul(s["mixed"], w_out_l, res=xc, tm=1024, tn=1024, tk=2 * GW, name="mm_out_fwd")
        s["memn"] = _rms_fwd(mem[0], vec(mem_norm, l), name="rms_mem_fwd")
        s["kv"] = _matmul(s["memn"], w_xkv_l, out_dtype=BF16, tk=D, name="mm_xkv_fwd")
        s["hq"] = _rms_fwd(s["x1"], vec(xattn_norm, l), name="rms_xattn_fwd")
        s["q"] = _matmul(s["hq"], w_xq_l, out_dtype=BF16, tk=D, name="mm_xq_fwd")
        s["xo"] = _xattn_fwd(s["q"], s["kv"], name="xattn_fwd")
        s["x2"] = _matmul(s["xo"], g_xo, res=s["x1"], dims=(S, D, XW), tn=cs_xo, tk=XW, b_spec=xo_fwd_b, name="mm_xo_fwd")
        s["hf"] = _rms_fwd(s["x2"], vec(ffn_norm, l), name="rms_ffn_fwd")
        s["u"] = _matmul(s["hf"], g_up, dims=(S, 2 * F, D), tn=tu, tk=D, b_spec=up_fwd_b, name="mm_up_fwd")
        s["act"] = _ffn_act_fwd(s["u"], ffn_conv_full[l], ffn_conv_bias[l:l + 1], name="ffn_act_fwd")
        x3 = _matmul(s["act"], w_down_l, res=s["x2"], tm=1024, tn=1024, tk=tu, name="mm_down_fwd")
        return x3, s

    def bwd_ffn(l, s, dx3, dx3b):
        dact = _matmul(dx3b, s["w_down"], tb=True, tm=1024, tk=D, name="mm_down_dgrad")
        d_down = _matmul(s["act"], dx3b, ta=True, tn=1024, tk=S, out_dtype=BF16, name="mm_down_wgrad")
        du3, dcw3, dcb3 = _ffn_act_bwd(s["u"], ffn_conv_full[l], ffn_conv_bias[l:l + 1], dact, name="ffn_act_bwd")
        tq, td = _tile(S, 1024), _tile(D, 1024)
        dhf = _matmul(du3, s["w_up"], tb=True, dims=(S, D, 2 * F), tm=tq, tn=td, tk=tu,
                      a_spec=pl.BlockSpec((None, tq, tu), lambda i, j, k: (k // fper, i, k % fper)),
                      b_spec=pl.BlockSpec((None, td, tu), lambda i, j, k: (k // per, j, k % per)), name="mm_up_dgrad")
        d_up = _matmul(s["hf"], du3, ta=True, dims=(D, 2 * F, S), tn=tu, tk=S,
                       b_spec=pl.BlockSpec((None, S, tu), lambda i, j, k: (j // fper, 0, j % fper)),
                       o_spec=pl.BlockSpec((None, _tile(D, 512), tu), lambda i, j, k: (j // per, i, j % per)),
                       out_shape=jax.ShapeDtypeStruct((4, D, cs_up), BF16), name="mm_up_wgrad")
        dx2, dx2b, dg_ffn = _rms_bwd(s["x2"], vec(ffn_norm, l), dhf, dx3, name="rms_bwd")
        small = [dg_ffn, jnp.concatenate([dcw3[0], dcw3[1]], axis=1), jnp.concatenate([dcb3[0], dcb3[1]], axis=1)]
        return dx2, dx2b, {5: d_up, 6: d_down.reshape(4, -1, D)}, small

    def bwd_rest(l, s, dx2, dx2b):
        dxo = _matmul(dx2b, s["w_xo"], tb=True, dims=(S, XW, D), tn=XW, tk=cs_xo, b_spec=xo_dg_b, name="mm_xo_dgrad")
        d_xo = _matmul(s["xo"], dx2b, ta=True, dims=(XW, D, S), tm=XW, tn=cs_xo, tk=S, o_spec=xo_wg_o,
                       out_shape=jax.ShapeDtypeStruct((4, XW, cs_xo), BF16), name="mm_xo_wgrad")
        dq, dk, dv = _xattn_bwd(s["q"], s["kv"], dxo, name="xattn_bwd")
        dkv = jnp.concatenate([dk, dv], axis=1)
        dhq = _matmul(dq, s["w_xq"], tb=True, tk=XW, name="mm_xq_dgrad")
        d_xq = _matmul(s["hq"], dq, ta=True, tk=S, out_dtype=BF16, name="mm_xq_wgrad")
        dmemn = _matmul(dkv, s["w_xkv"], tb=True, tk=2 * XW, name="mm_xkv_dgrad")
        d_xkv = _matmul(s["memn"], dkv, ta=True, tk=mem.shape[1], out_dtype=BF16, name="mm_xkv_wgrad")
        _, _, dg_mem = _rms_bwd(mem[0], vec(mem_norm, l), dmemn, None, name="rms_mem_bwd")
        dx1, dx1b, dg_xattn = _rms_bwd(s["x1"], vec(xattn_norm, l), dhq, dx2, name="rms_bwd")
        dmix = _matmul(dx1b, s["w_out"], tb=True, tm=1024, tn=1024, tk=D, name="mm_out_dgrad")
        d_out = _matmul(s["mixed"], dx1b, ta=True, tm=1024, tn=1024, tk=S, out_dtype=BF16, name="mm_out_wgrad")
        dq_s, dk_s, dv_s = _sb_bwd(s["proj"], s["tot"], dmix, nh, name="sb_bwd")
        dx_qkv, dz, dx_g, dconv, dab, dng = _gdn_backward(s["proj"], gdn_conv_full[l], ab[l], vec(gdn_norm, l), s["gdn"], dmix, nh, "")
        dproj = jnp.concatenate([dq_s, dk_s, dv_s, dx_qkv, dz, dx_g], axis=1)
        dh = _matmul(dproj, s["w_in"], tb=True, tm=1024, tn=1024, tk=2432, name="mm_in_dgrad")
        d_in = _matmul(s["h"], dproj, ta=True, tn=2432, tk=S, out_dtype=BF16, name="mm_in_wgrad")
        dx0, dx0b, dg_mix = _rms_bwd(s["x0"], vec(mix_norm, l), dh, dx1, name="rms_bwd")
        slabs = {0: jnp.stack([d_in[:, i * cs_in:(i + 1) * cs_in] for i in range(4)]), 1: d_out.reshape(4, -1, D),
                 2: d_xq.reshape(4, -1, XW), 3: d_xkv.reshape(4, -1, 2 * XW), 4: d_xo}
        return dx0, dx0b, slabs, [dg_mix, dconv, dab, dng, dg_xattn, dg_mem]

    def start_gather(l, idxs, after, tag):
        placed = [_cast_place(big[i], l, kvec, name=f"cast_place_{l}") for i in idxs]
        return _gather_start(placed, after, name=f"gather_start_{l}{tag}")

    def end_gather(pending, after, l, tag):
        send, recv, ws, _ = pending
        ws = _gather_wait(send, recv, ws, after, name=f"gather_wait_{l}{tag}")
        return _gather_to_sibling(ws, name=f"gather_to_sibling{tag}")

    xc = x[0]
    saved = []
    first = start_gather(0, [0], conv_all, "a")
    second = start_gather(0, list(range(1, 7)), first[3], "b")
    for l in range(L):
        if l == 0:
            g_in = end_gather(first, xc, 0, "a")[0]
            rest = lambda after: end_gather(second, after, 0, "b")
            order = second[3]
        else:
            wts = end_gather(pending, xc, l, "")
            g_in, rest, order = wts[0], (lambda after, wts=wts: wts[1:]), wts[1]
        token = None
        if l + 1 < L:
            pending = start_gather(l + 1, list(range(7)), order, "")
            token = pending[3]
        xc, s = fwd_layer(l, xc, g_in, rest, token)
        saved.append(s)
    loss_blk, dxc, dxcb, dg_final = _loss_head(xc, final_norm[None, :], loss_target[0], name="loss_head")

    def start_reduce(slabs, l, tag):
        idxs = sorted(slabs)
        return idxs, l, tag, _reduce_start([slabs[i] for i in idxs], cvec, name=f"reduce_start_{l}{tag}")

    def finish_reduce(item, sums, after):
        idxs, l, tag, (send, recv, ps, lands, _) = item
        ps, lands = _reduce_wait(send, recv, ps, lands, after, name=f"reduce_wait_{l}{tag}")
        for i, p, rb in zip(idxs, ps, lands):
            sums[i] = _sum_partials(p, rb, kcvec, sums[i], l, L, name=f"sum_partials_{l}")

    sums = [None] * 7
    small_by_layer = [None] * L
    in_flight = []
    for l in reversed(range(L)):
        if in_flight:
            dxcb = dxcb + in_flight[-1][3][4][0, 0].astype(BF16)
        dx2, dx2b, slabs_ffn, small_ffn = bwd_ffn(l, saved[l], dxc, dxcb)
        for item in in_flight:
            finish_reduce(item, sums, dx2)
        in_flight = [start_reduce(slabs_ffn, l, "f")]
        dx2b = dx2b + in_flight[-1][3][4][0, 0].astype(BF16)
        dxc, dxcb, slabs_rest, small_rest = bwd_rest(l, saved[l], dx2, dx2b)
        saved[l] = None
        in_flight.append(start_reduce(slabs_rest, l, "r"))
        small_by_layer[l] = small_rest + small_ffn

    small_flat = [a for l in range(L) for a in small_by_layer[l]] + [dg_final, loss_blk[0:1]]
    red_buf = _allreduce_small(_pack(small_flat), name="allreduce_small")
    red = _unpack(red_buf, [a.shape for a in small_flat])
    per_layer = [red[9 * l:9 * l + 9] for l in range(L)]
    col = lambda i: jnp.concatenate([p[i] for p in per_layer], axis=0)
    stk = lambda i: jnp.stack([p[i] for p in per_layer])
    g_conv_full, g_ab, g_fconv_full = stk(1), stk(2), stk(7)
    grads_small = dict(
        mix_norm=col(0), gdn_conv=lax.dynamic_slice(g_conv_full, (0, 0, chip * cs_gc), (L, SHORT_CONV, cs_gc)),
        gdn_a_log=g_ab[:, 0, nh:2 * nh], gdn_dt_bias=g_ab[:, 1, nh:2 * nh], gdn_norm=col(3), xattn_norm=col(4),
        mem_norm=col(5), ffn_norm=col(6), ffn_conv=lax.dynamic_slice(g_fconv_full, (0, 0, chip * cs_fc), (L, FFN_CONV, cs_fc)),
        ffn_conv_bias=col(8), final_norm=red[-2][0])
    loss = red[-1][0, 0]

    names_small = ["mix_norm", "gdn_conv", "gdn_a_log", "gdn_dt_bias", "gdn_norm", "xattn_norm", "mem_norm", "ffn_norm",
                   "ffn_conv", "ffn_conv_bias", "final_norm"]
    w_small = dict(mix_norm=mix_norm, gdn_conv=gdn_conv, gdn_a_log=gdn_a_log, gdn_dt_bias=gdn_dt_bias, gdn_norm=gdn_norm,
                   xattn_norm=xattn_norm, mem_norm=mem_norm, ffn_norm=ffn_norm, ffn_conv=ffn_conv, ffn_conv_bias=ffn_conv_bias,
                   final_norm=final_norm)
    m_small = dict(mix_norm=m_mix_norm, gdn_conv=m_gdn_conv, gdn_a_log=m_gdn_a_log, gdn_dt_bias=m_gdn_dt_bias, gdn_norm=m_gdn_norm,
                   xattn_norm=m_xattn_norm, mem_norm=m_mem_norm, ffn_norm=m_ffn_norm, ffn_conv=m_ffn_conv,
                   ffn_conv_bias=m_ffn_conv_bias, final_norm=m_final_norm)
    v_small = dict(mix_norm=v_mix_norm, gdn_conv=v_gdn_conv, gdn_a_log=v_gdn_a_log, gdn_dt_bias=v_gdn_dt_bias, gdn_norm=v_gdn_norm,
                   xattn_norm=v_xattn_norm, mem_norm=v_mem_norm, ffn_norm=v_ffn_norm, ffn_conv=v_ffn_conv,
                   ffn_conv_bias=v_ffn_conv_bias, final_norm=v_final_norm)
    shapes_small = [w_small[n].shape for n in names_small]
    packed = [_pack([d[n] for n in names_small], row_multiple=128)[None] for d in (w_small, grads_small, m_small, v_small)]
    upd_small = [_unpack(o[0], shapes_small) for o in _adamw(*packed, name="adamw_small")]
    delta, new_m, new_v = [dict(zip(names_small, u)) for u in upd_small]
    grads = dict(grads_small)
    big_names = ["w_in", "w_out", "w_xq", "w_xkv", "w_xo", "w_up", "w_down"]
    big_m = [m_w_in, m_w_out, m_w_xq, m_w_xkv, m_w_xo, m_w_up, m_w_down]
    big_v = [v_w_in, v_w_out, v_w_xq, v_w_xkv, v_w_xo, v_w_up, v_w_down]
    after = red_buf
    for item in in_flight:
        finish_reduce(item, sums, after)
        idxs, tag = item[0], item[2]
        from_sib = _swap_with_sibling([sums[i] for i in idxs], name=f"swap_halves_{tag}")
        for i, gs in zip(idxs, from_sib):
            n = big_names[i]
            grads[n], delta[n], new_m[n], new_v[n] = _adamw_halves(big[i], sums[i], gs, cvec, big_m[i], big_v[i],
                                                                   name=f"adamw_{n}")
        after = delta[big_names[idxs[-1]]]

    order = ["mix_norm", "w_in", "gdn_conv", "gdn_a_log", "gdn_dt_bias", "gdn_norm", "w_out", "xattn_norm", "mem_norm", "w_xq",
             "w_xkv", "w_xo", "ffn_norm", "w_up", "ffn_conv", "ffn_conv_bias", "w_down", "final_norm"]
    return (loss, dxc[None], *[grads[n] for n in order], *[delta[n] for n in order], *[new_m[n] for n in order],
            *[new_v[n] for n in order])
```

```python
import functools
import math

import jax
import jax.numpy as jnp
from jax import lax
from jax.experimental import pallas as pl
from jax.experimental.pallas import tpu as pltpu

F32 = jnp.float32
BF16 = jnp.bfloat16

HEAD_DIM = 128
CHUNK = 64
GDN_CPB = 4
SB_TQ, SB_TK = 512, 512
SHORT_CONV = 4
FFN_CONV = 3
X_HEADS = 4
EPS = 1e-6
LANES = 128
VMEM_LIMIT = 56 * 2**20

ADAM_LR, ADAM_B1, ADAM_B2, ADAM_EPS, ADAM_WD, ADAM_STEP = 0.001, 0.9, 0.999, 1e-08, 0.01, 10

HI = lax.Precision.HIGH


def _params(sem):
    return pltpu.CompilerParams(dimension_semantics=sem, vmem_limit_bytes=VMEM_LIMIT)


def _tile(n, want):
    if n <= want:
        return n
    t = (want // LANES) * LANES
    while t > LANES and n % t:
        t -= LANES
    assert n % t == 0, (n, want)
    return t


def _sigmoid(x):
    return jax.nn.sigmoid(x)


def _softplus(x):
    return jnp.maximum(x, 0.0) + jnp.log(1.0 + jnp.exp(-jnp.abs(x)))


def _matmul(a, b, *, name, ta=False, tb=False, out_dtype=F32, res=None, tm=512, tn=512, tk=2048,
            dims=None, a_spec=None, b_spec=None, o_spec=None, out_shape=None):
    if dims is None:
        M, K = (a.shape[1], a.shape[0]) if ta else a.shape
        N = b.shape[0] if tb else b.shape[1]
    else:
        M, N, K = dims
    tm, tn, tk = _tile(M, tm), _tile(N, tn), _tile(K, tk)
    nk = K // tk
    dn = (((0 if ta else 1,), (1 if tb else 0,)), ((), ()))

    def body(*refs):
        a_ref, b_ref = refs[0], refs[1]
        r_ref = refs[2] if res is not None else None
        o_ref = refs[3] if res is not None else refs[2]
        p = lax.dot_general(a_ref[...].astype(BF16), b_ref[...].astype(BF16), dn, preferred_element_type=F32)

        def finish(acc):
            if r_ref is not None:
                acc = acc + r_ref[...].astype(F32)
            o_ref[...] = acc.astype(o_ref.dtype)

        if nk == 1:
            finish(p)
        else:
            acc_ref = refs[-1]
            k = pl.program_id(2)

            @pl.when(k == 0)
            def _():
                acc_ref[...] = p

            @pl.when(jnp.logical_and(k > 0, k < nk - 1))
            def _():
                acc_ref[...] += p

            @pl.when(k == nk - 1)
            def _():
                finish(acc_ref[...] + p)

    if a_spec is None:
        a_spec = pl.BlockSpec((tk, tm), lambda i, j, k: (k, i)) if ta else pl.BlockSpec((tm, tk), lambda i, j, k: (i, k))
    if b_spec is None:
        b_spec = pl.BlockSpec((tn, tk), lambda i, j, k: (j, k)) if tb else pl.BlockSpec((tk, tn), lambda i, j, k: (k, j))
    if o_spec is None:
        o_spec = pl.BlockSpec((tm, tn), lambda i, j, k: (i, j))
    if out_shape is None:
        out_shape = jax.ShapeDtypeStruct((M, N), out_dtype)
    in_specs, args = [a_spec, b_spec], [a, b]
    if res is not None:
        in_specs.append(pl.BlockSpec((tm, tn), lambda i, j, k: (i, j)))
        args.append(res)
    return pl.pallas_call(
        body, grid=(M // tm, N // tn, nk), in_specs=in_specs, out_specs=o_spec, out_shape=out_shape,
        scratch_shapes=[pltpu.VMEM((tm, tn), F32)] if nk > 1 else [],
        compiler_params=_params(("parallel", "parallel", "arbitrary")), name=name)(*args)


def _rms_fwd(x, g, *, name):
    R, D = x.shape
    tr = _tile(R, 256)

    def body(x_ref, g_ref, o_ref):
        xv = x_ref[...]
        rstd = lax.rsqrt(jnp.mean(xv * xv, axis=-1, keepdims=True) + EPS)
        o_ref[...] = (xv * rstd * g_ref[...]).astype(o_ref.dtype)

    return pl.pallas_call(
        body, grid=(R // tr,), in_specs=[pl.BlockSpec((tr, D), lambda i: (i, 0)), pl.BlockSpec((1, D), lambda i: (0, 0))],
        out_specs=pl.BlockSpec((tr, D), lambda i: (i, 0)), out_shape=jax.ShapeDtypeStruct((R, D), BF16),
        compiler_params=_params(("parallel",)), name=name)(x, g)


def _rms_bwd(x, g, dh, dres, *, name):
    R, D = x.shape
    tr = _tile(R, 256)

    def body(*refs):
        if dres is None:
            x_ref, g_ref, dh_ref, dx_ref, dxb_ref, dg_ref = refs
        else:
            x_ref, g_ref, dh_ref, dr_ref, dx_ref, dxb_ref, dg_ref = refs
        xv = x_ref[...]
        dhv = dh_ref[...].astype(F32)
        rstd = lax.rsqrt(jnp.mean(xv * xv, axis=-1, keepdims=True) + EPS)
        xhat = xv * rstd
        t = dhv * g_ref[...]
        dx = rstd * (t - xhat * jnp.mean(t * xhat, axis=-1, keepdims=True))
        if dres is not None:
            dx = dx + dr_ref[...]
        dx_ref[...] = dx
        dxb_ref[...] = dx.astype(BF16)
        part = jnp.sum(dhv * xhat, axis=0, keepdims=True)

        @pl.when(pl.program_id(0) == 0)
        def _():
            dg_ref[...] = part

        @pl.when(pl.program_id(0) > 0)
        def _():
            dg_ref[...] += part

    row = pl.BlockSpec((tr, D), lambda i: (i, 0))
    vec = pl.BlockSpec((1, D), lambda i: (0, 0))
    in_specs = [row, vec, row] + ([row] if dres is not None else [])
    args = [x, g, dh] + ([dres] if dres is not None else [])
    return pl.pallas_call(
        body, grid=(R // tr,), in_specs=in_specs, out_specs=[row, row, vec],
        out_shape=[jax.ShapeDtypeStruct((R, D), F32), jax.ShapeDtypeStruct((R, D), BF16), jax.ShapeDtypeStruct((1, D), F32)],
        compiler_params=_params(("arbitrary",)), name=name)(*args)


def _loss_head(x, g, tgt, *, name):
    R, D = x.shape
    tr = _tile(R, 256)

    def body(x_ref, g_ref, t_ref, l_ref, dx_ref, dxb_ref, dg_ref):
        xv = x_ref[...]
        rstd = lax.rsqrt(jnp.mean(xv * xv, axis=-1, keepdims=True) + EPS)
        xhat = xv * rstd
        err = xhat * g_ref[...] - t_ref[...]
        dy = err * (1.0 / D)
        t = dy * g_ref[...]
        dx = rstd * (t - xhat * jnp.mean(t * xhat, axis=-1, keepdims=True))
        dx_ref[...] = dx
        dxb_ref[...] = dx.astype(BF16)
        part = jnp.sum(dy * xhat, axis=0, keepdims=True)
        lpart = jnp.zeros((8, LANES), F32) + 0.5 * jnp.sum(jnp.mean(err * err, axis=-1, keepdims=True))

        @pl.when(pl.program_id(0) == 0)
        def _():
            dg_ref[...] = part
            l_ref[...] = lpart

        @pl.when(pl.program_id(0) > 0)
        def _():
            dg_ref[...] += part
            l_ref[...] += lpart

    row = pl.BlockSpec((tr, D), lambda i: (i, 0))
    vec = pl.BlockSpec((1, D), lambda i: (0, 0))
    return pl.pallas_call(
        body, grid=(R // tr,), in_specs=[row, vec, row],
        out_specs=[pl.BlockSpec((8, LANES), lambda i: (0, 0)), row, row, vec],
        out_shape=[jax.ShapeDtypeStruct((8, LANES), F32), jax.ShapeDtypeStruct((R, D), F32),
                   jax.ShapeDtypeStruct((R, D), BF16), jax.ShapeDtypeStruct((1, D), F32)],
        compiler_params=_params(("arbitrary",)), name=name)(x, g, tgt)


def _shift_down(x, s):
    if s == 0:
        return x
    row = lax.broadcasted_iota(jnp.int32, x.shape, 0)
    return jnp.where(row >= s, pltpu.roll(x, s, 0), 0.0)


def _shift_up(x, s):
    if s == 0:
        return x
    n = x.shape[0]
    row = lax.broadcasted_iota(jnp.int32, x.shape, 0)
    return jnp.where(row < n - s, pltpu.roll(x, n - s, 0), 0.0)


def _dwconv(x, w):
    k = w.shape[0]
    acc = x * w[k - 1:k, :]
    for i in range(k - 1):
        acc = acc + _shift_down(x, k - 1 - i) * w[i:i + 1, :]
    return acc


def _dwconv_bwd(x, w, dc):
    k = w.shape[0]
    dx = dc * w[k - 1:k, :]
    dws = []
    for i in range(k - 1):
        s = k - 1 - i
        dx = dx + _shift_up(dc, s) * w[i:i + 1, :]
        dws.append(jnp.sum(dc * _shift_down(x, s), axis=0, keepdims=True))
    dws.append(jnp.sum(dc * x, axis=0, keepdims=True))
    return dx, jnp.concatenate(dws, axis=0)


def _ffn_act_fwd(u, cw, cb, *, name):
    S, F2 = u.shape
    F = F2 // 2
    tc = _tile(F, 256)
    nb = F // tc

    def body(ug_ref, uu_ref, wg_ref, wu_ref, bg_ref, bu_ref, o_ref):
        cg = _dwconv(ug_ref[...], wg_ref[...]) + bg_ref[...]
        cu = _dwconv(uu_ref[...], wu_ref[...]) + bu_ref[...]
        o_ref[...] = (cg * _sigmoid(cg) * cu).astype(o_ref.dtype)

    blk = lambda r, off: pl.BlockSpec((r, tc), lambda j: (0, j + off))
    return pl.pallas_call(
        body, grid=(nb,), in_specs=[blk(S, 0), blk(S, nb), blk(FFN_CONV, 0), blk(FFN_CONV, nb), blk(1, 0), blk(1, nb)],
        out_specs=blk(S, 0), out_shape=jax.ShapeDtypeStruct((S, F), BF16),
        compiler_params=_params(("parallel",)), name=name)(u, u, cw, cw, cb, cb)


def _ffn_act_bwd(u, cw, cb, dact, *, name):
    S, F2 = u.shape
    F = F2 // 2
    tc = _tile(F, 256)
    nb = F // tc

    def body(ug_ref, uu_ref, wg_ref, wu_ref, bg_ref, bu_ref, da_ref, du_ref, dw_ref, db_ref):
        ug, uu = ug_ref[...], uu_ref[...]
        cg = _dwconv(ug, wg_ref[...]) + bg_ref[...]
        cu = _dwconv(uu, wu_ref[...]) + bu_ref[...]
        sg = _sigmoid(cg)
        da = da_ref[...].astype(F32)
        dcu = da * (cg * sg)
        dcg = da * cu * (sg * (1.0 + cg * (1.0 - sg)))
        dxg, dwg = _dwconv_bwd(ug, wg_ref[...], dcg)
        dxu, dwu = _dwconv_bwd(uu, wu_ref[...], dcu)
        du_ref[0] = dxg.astype(du_ref.dtype)
        du_ref[1] = dxu.astype(du_ref.dtype)
        dw_ref[0] = dwg
        dw_ref[1] = dwu
        db_ref[0] = jnp.sum(dcg, axis=0, keepdims=True)
        db_ref[1] = jnp.sum(dcu, axis=0, keepdims=True)

    blk = lambda r, off: pl.BlockSpec((r, tc), lambda j: (0, j + off))
    blk3 = lambda r: pl.BlockSpec((2, r, tc), lambda j: (0, 0, j))
    return pl.pallas_call(
        body, grid=(nb,),
        in_specs=[blk(S, 0), blk(S, nb), blk(FFN_CONV, 0), blk(FFN_CONV, nb), blk(1, 0), blk(1, nb), blk(S, 0)],
        out_specs=[blk3(S), blk3(FFN_CONV), blk3(1)],
        out_shape=[jax.ShapeDtypeStruct((2, S, F), BF16), jax.ShapeDtypeStruct((2, FFN_CONV, F), F32),
                   jax.ShapeDtypeStruct((2, 1, F), F32)],
        compiler_params=_params(("parallel",)), name=name)(u, u, cw, cw, cb, cb, dact)


def _xattn_fwd(q, kv, *, name):
    S, XW = q.shape
    M = kv.shape[0]
    nh = XW // HEAD_DIM
    tq = _tile(S, 512)
    scale = HEAD_DIM ** -0.5

    def body(q_ref, k_ref, v_ref, o_ref):
        z = lax.dot_general(q_ref[...], k_ref[...], (((1,), (1,)), ((), ())), preferred_element_type=F32) * scale
        e = jnp.exp(z - jnp.max(z, axis=-1, keepdims=True))
        p = e / jnp.sum(e, axis=-1, keepdims=True)
        o_ref[...] = jnp.dot(p.astype(BF16), v_ref[...], preferred_element_type=F32).astype(o_ref.dtype)

    return pl.pallas_call(
        body, grid=(nh, S // tq),
        in_specs=[pl.BlockSpec((tq, HEAD_DIM), lambda h, i: (i, h)), pl.BlockSpec((M, HEAD_DIM), lambda h, i: (0, h)),
                  pl.BlockSpec((M, HEAD_DIM), lambda h, i: (0, nh + h))],
        out_specs=pl.BlockSpec((tq, HEAD_DIM), lambda h, i: (i, h)), out_shape=jax.ShapeDtypeStruct((S, XW), BF16),
        compiler_params=_params(("parallel", "parallel")), name=name)(q, kv, kv)


def _xattn_bwd(q, kv, do, *, name):
    S, XW = q.shape
    M = kv.shape[0]
    nh = XW // HEAD_DIM
    tq = _tile(S, 512)
    scale = HEAD_DIM ** -0.5
    nt = (((1,), (1,)), ((), ()))
    tn = (((0,), (0,)), ((), ()))

    def body(q_ref, k_ref, v_ref, do_ref, dq_ref, dk_ref, dv_ref):
        qv, kvv, vv = q_ref[...], k_ref[...], v_ref[...]
        dov = do_ref[...].astype(BF16)
        z = lax.dot_general(qv, kvv, nt, preferred_element_type=F32) * scale
        e = jnp.exp(z - jnp.max(z, axis=-1, keepdims=True))
        p = e / jnp.sum(e, axis=-1, keepdims=True)
        dp = lax.dot_general(dov, vv, nt, preferred_element_type=F32)
        ds = (p * (dp - jnp.sum(dp * p, axis=-1, keepdims=True)) * scale).astype(BF16)
        dq_ref[...] = jnp.dot(ds, kvv, preferred_element_type=F32).astype(dq_ref.dtype)
        dk = lax.dot_general(ds, qv, tn, preferred_element_type=F32)
        dv = lax.dot_general(p.astype(BF16), dov, tn, preferred_element_type=F32)

        @pl.when(pl.program_id(1) == 0)
        def _():
            dk_ref[...] = dk
            dv_ref[...] = dv

        @pl.when(pl.program_id(1) > 0)
        def _():
            dk_ref[...] += dk
            dv_ref[...] += dv

    qs = pl.BlockSpec((tq, HEAD_DIM), lambda h, i: (i, h))
    ms = pl.BlockSpec((M, HEAD_DIM), lambda h, i: (0, h))
    return pl.pallas_call(
        body, grid=(nh, S // tq),
        in_specs=[qs, ms, pl.BlockSpec((M, HEAD_DIM), lambda h, i: (0, nh + h)), qs],
        out_specs=[qs, ms, ms],
        out_shape=[jax.ShapeDtypeStruct((S, XW), BF16), jax.ShapeDtypeStruct((M, XW), F32), jax.ShapeDtypeStruct((M, XW), F32)],
        compiler_params=_params(("parallel", "arbitrary")), name=name)(q, kv, kv, do)


_NN = (((1,), (0,)), ((), ()))
_NT = (((1,), (1,)), ((), ()))
_TN = (((0,), (0,)), ((), ()))


def _batched(dn, a):
    if a.ndim == 2:
        return dn
    (ca,), (cb,) = dn[0]
    return (((ca + 1,), (cb + 1,)), ((0,), (0,)))


def _dot(a, b, dn=_NN):
    return lax.dot_general(a.astype(BF16), b.astype(BF16), _batched(dn, a), preferred_element_type=F32)


def _dot_hi(a, b, dn=_NN):
    return lax.dot_general(a, b, _batched(dn, a), preferred_element_type=F32, precision=HI)


def _dot_split(a, b01, dn=_NN):
    hi = a.astype(BF16)
    lo = (a - hi.astype(F32)).astype(BF16)
    return (lax.dot_general(hi, b01, dn, preferred_element_type=F32)
            + lax.dot_general(lo, b01, dn, preferred_element_type=F32))


def _running_sums(x, carry, reverse):
    nb = x.shape[1] // LANES
    tri = _after_matrix(LANES, transpose=not reverse)
    blocks = [x[:, b * LANES:(b + 1) * LANES] for b in range(nb)]
    sums = [jnp.sum(blk, axis=1, keepdims=True) for blk in blocks]
    out = [None] * nb
    run = carry
    for b in (reversed(range(nb)) if reverse else range(nb)):
        out[b] = _dot_split(blocks[b], tri) + run
        run = run + sums[b]
    total = sums[0]
    for b in range(1, nb):
        total = total + sums[b]
    return jnp.concatenate(out, axis=1), total


def _after_matrix(n, transpose=False):
    row = lax.broadcasted_iota(jnp.int32, (n, n), 0)
    col = lax.broadcasted_iota(jnp.int32, (n, n), 1)
    return (row < col if transpose else row > col).astype(BF16)


def _sb_fwd(proj, nh, *, name):
    S = proj.shape[0]
    TQ, TK = min(SB_TQ, S), min(SB_TK, S)
    nq = S // TQ
    scale = HEAD_DIM ** -0.5

    def body(q_ref, k_ref, v_ref, o_ref, tot_ref):
        i = pl.program_id(1)
        q = q_ref[...].astype(BF16)
        qpos = i * TQ + lax.broadcasted_iota(jnp.int32, (TQ, TK), 0)
        kcol = lax.broadcasted_iota(jnp.int32, (TQ, TK), 1)
        nt = ((i + 1) * TQ + TK - 1) // TK
        n_in = (i * TQ) // TK

        def make_step(masked):
            def step(j, carry):
                acc, out = carry
                off = pl.multiple_of(j * TK, TK)
                kb = k_ref[pl.ds(off, TK), :].astype(BF16)
                vb = v_ref[pl.ds(off, TK), :].astype(BF16)
                z = lax.dot_general(q, kb, _NT, preferred_element_type=F32) * scale
                ls = -_softplus(z)
                if masked:
                    valid = kcol + off < qpos
                    ls = jnp.where(valid, ls, 0.0)
                later, rs = _running_sums(ls, acc, reverse=True)
                w = jnp.exp(ls + z + later)
                if masked:
                    w = jnp.where(valid, w, 0.0)
                out = out + jnp.dot(w.astype(BF16), vb, preferred_element_type=F32)
                return acc + rs, out
            return step

        edge, inner = make_step(True), make_step(False)
        carry = (jnp.zeros((TQ, 1), F32), jnp.zeros((TQ, HEAD_DIM), F32))
        carry = lax.fori_loop(0, nt - n_in, lambda t, c: edge(nt - 1 - t, c), carry)
        acc, out = lax.fori_loop(0, n_in, lambda t, c: inner(n_in - 1 - t, c), carry)
        o_ref[...] = out.astype(o_ref.dtype)
        tot_ref[...] = acc

    return pl.pallas_call(
        body, grid=(nh, nq),
        in_specs=[pl.BlockSpec((TQ, HEAD_DIM), lambda h, i: (i, h)),
                  pl.BlockSpec((S, HEAD_DIM), lambda h, i: (0, nh + h)),
                  pl.BlockSpec((S, HEAD_DIM), lambda h, i: (0, 2 * nh + h))],
        out_specs=[pl.BlockSpec((TQ, HEAD_DIM), lambda h, i: (i, h)), pl.BlockSpec((None, TQ, 1), lambda h, i: (h, i, 0))],
        out_shape=[jax.ShapeDtypeStruct((S, nh * HEAD_DIM), BF16), jax.ShapeDtypeStruct((nh, S, 1), F32)],
        compiler_params=_params(("parallel", "parallel")), name=name)(proj, proj, proj)


def _sb_bwd(proj, tot, dmix, nh, *, name):
    S = proj.shape[0]
    TQ, TK = min(SB_TQ, S), min(SB_TK, S)
    nq = S // TQ
    scale = HEAD_DIM ** -0.5

    def body(q_ref, k_ref, v_ref, tot_ref, do_ref, dq_ref, dk_ref, dv_ref, dk_acc, dv_acc):
        i = pl.program_id(1)

        @pl.when(i == 0)
        def _():
            dk_acc[...] = jnp.zeros_like(dk_acc)
            dv_acc[...] = jnp.zeros_like(dv_acc)

        q = q_ref[...].astype(BF16)
        do = do_ref[...].astype(BF16)
        tot = tot_ref[...]
        qpos = i * TQ + lax.broadcasted_iota(jnp.int32, (TQ, TK), 0)
        kcol = lax.broadcasted_iota(jnp.int32, (TQ, TK), 1)
        nt = ((i + 1) * TQ + TK - 1) // TK
        n_in = (i * TQ) // TK

        def make_step(masked):
            def step(j, carry):
                pre, g_sum, dq = carry
                off = pl.multiple_of(j * TK, TK)
                kb = k_ref[pl.ds(off, TK), :].astype(BF16)
                vb = v_ref[pl.ds(off, TK), :].astype(BF16)
                z = lax.dot_general(q, kb, _NT, preferred_element_type=F32) * scale
                ls = -_softplus(z)
                if masked:
                    valid = kcol + off < qpos
                    ls = jnp.where(valid, ls, 0.0)
                lb = ls + z
                rs = jnp.sum(ls, axis=1, keepdims=True)
                later, _ = _running_sums(ls, tot - pre - rs, reverse=True)
                w = jnp.exp(lb + later)
                if masked:
                    w = jnp.where(valid, w, 0.0)
                g = lax.dot_general(do, vb, _NT, preferred_element_type=F32) * w
                dls, gs = _running_sums(g, g_sum, reverse=False)
                sig = jnp.exp(lb)
                dz = g * (1.0 - sig) - dls * sig
                if masked:
                    dz = jnp.where(valid, dz, 0.0)
                dz = (dz * scale).astype(BF16)
                dq = dq + jnp.dot(dz, kb, preferred_element_type=F32)
                dk_acc[pl.ds(off, TK), :] += lax.dot_general(dz, q, _TN, preferred_element_type=F32)
                dv_acc[pl.ds(off, TK), :] += lax.dot_general(w.astype(BF16), do, _TN, preferred_element_type=F32)
                return pre + rs, g_sum + gs, dq
            return step

        zero = jnp.zeros((TQ, 1), F32)
        carry = lax.fori_loop(0, n_in, make_step(False), (zero, zero, jnp.zeros((TQ, HEAD_DIM), F32)))
        _, _, dq = lax.fori_loop(n_in, nt, make_step(True), carry)
        dq_ref[...] = dq.astype(dq_ref.dtype)

        @pl.when(i == nq - 1)
        def _():
            dk_ref[...] = dk_acc[...].astype(dk_ref.dtype)
            dv_ref[...] = dv_acc[...].astype(dv_ref.dtype)

    qs = pl.BlockSpec((TQ, HEAD_DIM), lambda h, i: (i, h))
    full = pl.BlockSpec((S, HEAD_DIM), lambda h, i: (0, h))
    o = jax.ShapeDtypeStruct((S, nh * HEAD_DIM), BF16)
    return pl.pallas_call(
        body, grid=(nh, nq),
        in_specs=[qs, pl.BlockSpec((S, HEAD_DIM), lambda h, i: (0, nh + h)),
                  pl.BlockSpec((S, HEAD_DIM), lambda h, i: (0, 2 * nh + h)),
                  pl.BlockSpec((None, TQ, 1), lambda h, i: (h, i, 0)), qs],
        out_specs=[qs, full, full], out_shape=[o, o, o],
        scratch_shapes=[pltpu.VMEM((S, HEAD_DIM), F32), pltpu.VMEM((S, HEAD_DIM), F32)],
        compiler_params=_params(("parallel", "arbitrary")), name=name)(proj, proj, proj, tot, dmix)


def _gdn_qkv_fwd(proj, conv_w, nh, *, name):
    S = proj.shape[0]
    GW = nh * HEAD_DIM
    scale = HEAD_DIM ** -0.5

    def body(x_ref, w_ref, o_ref):
        sec = pl.program_id(0) // nh
        c = _dwconv(x_ref[...], w_ref[...])
        s = c * _sigmoid(c)
        r = lax.rsqrt(jnp.sum(s * s, axis=1, keepdims=True) + EPS)
        fac = jnp.where(sec == 0, scale, 1.0)
        o_ref[...] = jnp.where(sec == 2, s, s * (r * fac))

    return pl.pallas_call(
        body, grid=(3 * nh,),
        in_specs=[pl.BlockSpec((S, HEAD_DIM), lambda j: (0, 3 * nh + j)), pl.BlockSpec((SHORT_CONV, HEAD_DIM), lambda j: (0, j))],
        out_specs=pl.BlockSpec((None, S, HEAD_DIM), lambda j: (j // nh, 0, j % nh)),
        out_shape=jax.ShapeDtypeStruct((3, S, GW), F32),
        compiler_params=_params(("parallel",)), name=name)(proj, conv_w)


def _gdn_qkv_bwd(proj, conv_w, dqkv, nh, *, name):
    S = proj.shape[0]
    GW = nh * HEAD_DIM
    scale = HEAD_DIM ** -0.5

    def body(x_ref, w_ref, d_ref, dx_ref, dw_ref):
        sec = pl.program_id(0) // nh
        x, w = x_ref[...], w_ref[...]
        c = _dwconv(x, w)
        sg = _sigmoid(c)
        s = c * sg
        r = lax.rsqrt(jnp.sum(s * s, axis=1, keepdims=True) + EPS)
        sh = s * r
        d = d_ref[...]
        fac = jnp.where(sec == 0, scale, 1.0)
        dn = (r * fac) * (d - sh * jnp.sum(d * sh, axis=1, keepdims=True))
        ds = jnp.where(sec == 2, d, dn)
        dx, dw = _dwconv_bwd(x, w, ds * (sg * (1.0 + c * (1.0 - sg))))
        dx_ref[...] = dx.astype(dx_ref.dtype)
        dw_ref[...] = dw

    return pl.pallas_call(
        body, grid=(3 * nh,),
        in_specs=[pl.BlockSpec((S, HEAD_DIM), lambda j: (0, 3 * nh + j)), pl.BlockSpec((SHORT_CONV, HEAD_DIM), lambda j: (0, j)),
                  pl.BlockSpec((None, S, HEAD_DIM), lambda j: (j // nh, 0, j % nh))],
        out_specs=[pl.BlockSpec((S, HEAD_DIM), lambda j: (0, j)), pl.BlockSpec((SHORT_CONV, HEAD_DIM), lambda j: (0, j))],
        out_shape=[jax.ShapeDtypeStruct((S, 3 * GW), BF16), jax.ShapeDtypeStruct((SHORT_CONV, 3 * GW), F32)],
        compiler_params=_params(("parallel",)), name=name)(proj, conv_w, dqkv)


def _gdn_gates_fwd(proj, ab, nh, *, name):
    S = proj.shape[0]
    C = CHUNK

    def body(x_ref, ab_ref, o_ref):
        ri = lax.broadcasted_iota(jnp.int32, (C, C), 0)
        ci = lax.broadcasted_iota(jnp.int32, (C, C), 1)
        ltri = (ri >= ci).astype(F32)
        lane = lax.broadcasted_iota(jnp.int32, (C, LANES), 1)
        a_coef = -jnp.exp(ab_ref[0:1, :])
        dt = ab_ref[1:2, :]

        def chunk(n, _):
            rows = pl.ds(pl.multiple_of(n * C, C), C)
            x = x_ref[rows, :]
            beta = _sigmoid(x)
            g = jnp.where(jnp.logical_and(lane >= nh, lane < 2 * nh), a_coef * _softplus(x + dt), 0.0)
            gc = _dot_hi(ltri, pltpu.roll(g, nh, 1))
            o_ref[rows, :] = jnp.where(lane < nh, beta, g) + gc
            return 0

        lax.fori_loop(0, S // C, chunk, 0)

    return pl.pallas_call(
        body, grid=(1,),
        in_specs=[pl.BlockSpec((S, LANES), lambda i: (0, 7 * nh)), pl.BlockSpec((2, LANES), lambda i: (0, 0))],
        out_specs=pl.BlockSpec((S, LANES), lambda i: (0, 0)), out_shape=jax.ShapeDtypeStruct((S, LANES), F32),
        compiler_params=_params(("arbitrary",)), name=name)(proj, ab)


def _gdn_gates_bwd(proj, ab, dgt, nh, *, name):
    S = proj.shape[0]
    C = CHUNK

    def body(x_ref, ab_ref, d_ref, dx_ref, dab_ref):
        ri = lax.broadcasted_iota(jnp.int32, (C, C), 0)
        ci = lax.broadcasted_iota(jnp.int32, (C, C), 1)
        utri = (ri <= ci).astype(F32)
        lane = lax.broadcasted_iota(jnp.int32, (C, LANES), 1)
        is_b = lane < nh
        is_a = jnp.logical_and(lane >= nh, lane < 2 * nh)
        a_coef = -jnp.exp(ab_ref[0:1, :])
        dt = ab_ref[1:2, :]

        def chunk(n, carry):
            da_log, ddt = carry
            rows = pl.ds(pl.multiple_of(n * C, C), C)
            x = x_ref[rows, :]
            d = d_ref[rows, :]
            beta = _sigmoid(x)
            dg = pltpu.roll(_dot_hi(utri, jnp.where(lane >= 2 * nh, d, 0.0)), LANES - nh, 1)
            dg = jnp.where(is_a, dg, 0.0)
            dxa = dg * a_coef * _sigmoid(x + dt)
            dxb = jnp.where(is_b, d * beta * (1.0 - beta), 0.0)
            dx_ref[rows, :] = (dxa + dxb).astype(dx_ref.dtype)
            da_log = da_log + jnp.sum(dg * a_coef * _softplus(x + dt), axis=0, keepdims=True)
            return da_log, ddt + jnp.sum(dxa, axis=0, keepdims=True)

        zero = jnp.zeros((1, LANES), F32)
        da_log, ddt = lax.fori_loop(0, S // C, chunk, (zero, zero))
        dab_ref[0:1, :] = da_log
        dab_ref[1:2, :] = ddt

    return pl.pallas_call(
        body, grid=(1,),
        in_specs=[pl.BlockSpec((S, LANES), lambda i: (0, 7 * nh)), pl.BlockSpec((2, LANES), lambda i: (0, 0)),
                  pl.BlockSpec((S, LANES), lambda i: (0, 0))],
        out_specs=[pl.BlockSpec((S, LANES), lambda i: (0, 0)), pl.BlockSpec((2, LANES), lambda i: (0, 0))],
        out_shape=[jax.ShapeDtypeStruct((S, LANES), BF16), jax.ShapeDtypeStruct((2, LANES), F32)],
        compiler_params=_params(("arbitrary",)), name=name)(proj, ab, dgt)


def _unit_lower_inverse(lmat):
    C = lmat.shape[-1]
    ri = lax.broadcasted_iota(jnp.int32, lmat.shape, lmat.ndim - 2)
    ci = lax.broadcasted_iota(jnp.int32, lmat.shape, lmat.ndim - 1)
    nmat = -lmat
    p = jnp.where(ri == ci, 1.0, 0.0) + nmat
    for _ in range(int(math.log2(C)) - 1):
        nmat = _dot_hi(nmat, nmat)
        p = p + _dot_hi(p, nmat)
    return p


def _gdn_chunk_common(q, k, v, gates, gc_row, h, nh, tinv=None):
    C = CHUNK
    lane = lax.broadcasted_iota(jnp.int32, gates.shape, gates.ndim - 1)
    beta = jnp.sum(jnp.where(lane == h, gates, 0.0), axis=-1, keepdims=True)
    gc = jnp.sum(jnp.where(lane == 2 * nh + h, gates, 0.0), axis=-1, keepdims=True)
    sq = gates.shape[:-1] + (C,)
    ri = lax.broadcasted_iota(jnp.int32, sq, len(sq) - 2)
    ci = lax.broadcasted_iota(jnp.int32, sq, len(sq) - 1)
    incl, strict = ri >= ci, ri > ci
    decay = jnp.where(incl, jnp.exp(jnp.where(incl, gc - gc_row, 0.0)), 0.0)
    egc = jnp.exp(gc)
    kb, vb = k * beta, v * beta
    lmat = jnp.where(strict, _dot(kb, k, _NT) * decay, 0.0)
    kbg = kb * egc
    u = w = None
    if tinv is None:
        tinv = _unit_lower_inverse(lmat)
        u = _dot(tinv, vb)
        w = _dot(tinv, kbg)
    amat = _dot(q, k, _NT) * decay
    glast = gc[..., C - 1:C, :]
    ekt = jnp.exp(glast - gc)
    return dict(q=q, k=k, v=v, beta=beta, decay=decay, egc=egc, kb=kb, vb=vb, lmat=lmat, tinv=tinv, kbg=kbg, u=u, w=w,
                amat=amat, qd=q * egc, ekt=ekt, kt=k * ekt, cd=jnp.exp(glast), strict=strict, incl=incl)


def _gdn_chunk_specs(nh, S, nc):
    return [pl.BlockSpec((3, S, HEAD_DIM), lambda h: (0, 0, h)),
            pl.BlockSpec((S, LANES), lambda h: (0, 0)),
            pl.BlockSpec((None, nc, 1, CHUNK), lambda h: (h, 0, 0, 0))]


def _gdn_state_free(qkv_ref, gates_ref, gr_ref, g, nb, h, nh):
    C = CHUNK
    rows = pl.ds(pl.multiple_of(g * (nb * C), nb * C), nb * C)
    part = lambda x: x.reshape(nb, C, x.shape[-1])
    return rows, _gdn_chunk_common(part(qkv_ref[0, rows, :]), part(qkv_ref[1, rows, :]), part(qkv_ref[2, rows, :]),
                                   part(gates_ref[rows, :]), gr_ref[pl.ds(g * nb, nb)], h, nh)


def _gdn_chunk_fwd(qkv, gates, gc_row, *, name):
    _, S, GW = qkv.shape
    nh, C = GW // HEAD_DIM, CHUNK
    nc = S // C
    nb = min(GDN_CPB, nc)
    flat = lambda x: x.reshape(nb * C, x.shape[-1])

    def body(qkv_ref, gates_ref, gr_ref, o_ref, st_ref, u_s, w_s, a_s, qd_s, kt_s, cd_s):
        h = pl.program_id(0)

        def group(g, _):
            rows, m = _gdn_state_free(qkv_ref, gates_ref, gr_ref, g, nb, h, nh)
            u_s[rows, :] = flat(m["u"])
            w_s[rows, :] = flat(m["w"])
            a_s[rows, :] = flat(m["amat"])
            qd_s[rows, :] = flat(m["qd"])
            kt_s[rows, :] = flat(m["kt"])
            cd_s[pl.ds(g * nb, nb)] = jnp.broadcast_to(m["cd"], (nb, 8, LANES))
            return 0

        lax.fori_loop(0, nc // nb, group, 0)

        def chunk(n, s0):
            rows = pl.ds(pl.multiple_of(n * C, C), C)
            st_ref[n] = s0
            v_new = u_s[rows, :] - _dot(w_s[rows, :], s0)
            o_ref[rows, :] = _dot(qd_s[rows, :], s0) + _dot(a_s[rows, :], v_new)
            return s0 * cd_s[n][0:1, :] + _dot(kt_s[rows, :], v_new, _TN)

        lax.fori_loop(0, nc, chunk, jnp.zeros((HEAD_DIM, HEAD_DIM), F32))

    seq = pltpu.VMEM((S, HEAD_DIM), F32)
    return pl.pallas_call(
        body, grid=(nh,), in_specs=_gdn_chunk_specs(nh, S, nc),
        out_specs=[pl.BlockSpec((S, HEAD_DIM), lambda h: (0, h)),
                   pl.BlockSpec((None, nc, HEAD_DIM, HEAD_DIM), lambda h: (h, 0, 0, 0))],
        out_shape=[jax.ShapeDtypeStruct((S, GW), F32), jax.ShapeDtypeStruct((nh, nc, HEAD_DIM, HEAD_DIM), F32)],
        scratch_shapes=[seq, seq, pltpu.VMEM((S, C), F32), seq, seq, pltpu.VMEM((nc, 8, LANES), F32)],
        compiler_params=_params(("parallel",)), name=name)(qkv, gates, gc_row)


def _gdn_chunk_bwd(qkv, gates, gc_row, states, do, *, name):
    _, S, GW = qkv.shape
    nh, C = GW // HEAD_DIM, CHUNK
    nc = S // C
    nb = min(GDN_CPB, nc)
    flat = lambda x: x.reshape(nb * C, x.shape[-1])
    part = lambda x: x.reshape(nb, C, x.shape[-1])

    def body(qkv_ref, gates_ref, gr_ref, st_ref, do_ref, dqkv_ref, dgt_ref,
             t_s, vn_s, w_s, a_s, qd_s, kt_s, cd_s, dvn_s, dkt_s, dcd_s):
        h = pl.program_id(0)

        def group(g, _):
            rows, m = _gdn_state_free(qkv_ref, gates_ref, gr_ref, g, nb, h, nh)
            t_s[rows, :] = flat(m["tinv"])
            vn_s[rows, :] = flat(m["u"])
            w_s[rows, :] = flat(m["w"])
            a_s[rows, :] = flat(m["amat"])
            qd_s[rows, :] = flat(m["qd"])
            kt_s[rows, :] = flat(m["kt"])
            cd_s[pl.ds(g * nb, nb)] = jnp.broadcast_to(m["cd"], (nb, 8, LANES))
            return 0

        lax.fori_loop(0, nc // nb, group, 0)

        def chunk(t, dsn):
            n = nc - 1 - t
            rows = pl.ds(pl.multiple_of(n * C, C), C)
            s0, dout, w = st_ref[n], do_ref[rows, :], w_s[rows, :]
            v_new = vn_s[rows, :] - _dot(w, s0)
            dvn = _dot(a_s[rows, :], dout, _TN) + _dot(kt_s[rows, :], dsn)
            vn_s[rows, :] = v_new
            dvn_s[rows, :] = dvn
            dkt_s[rows, :] = _dot(v_new, dsn, _NT)
            dcd_s[n] = jnp.zeros((8, LANES), F32) + jnp.sum(dsn * s0)
            return _dot(qd_s[rows, :], dout, _TN) + dsn * cd_s[n][0:1, :] - _dot(w, dvn, _TN)

        lax.fori_loop(0, nc, chunk, jnp.zeros((HEAD_DIM, HEAD_DIM), F32))

        def rest(g, _):
            rows, m = _gdn_state_free_again(qkv_ref, gates_ref, gr_ref, t_s, g, nb, h, nh)
            chunks = pl.ds(g * nb, nb)
            dq, dk, dv, dgt = _gdn_chunk_grad(m, st_ref[chunks], part(vn_s[rows, :]), part(dvn_s[rows, :]),
                                              part(dkt_s[rows, :]), dcd_s[chunks][:, 0:1, 0:1], part(do_ref[rows, :]), h, nh)
            dqkv_ref[0, rows, :] = flat(dq)
            dqkv_ref[1, rows, :] = flat(dk)
            dqkv_ref[2, rows, :] = flat(dv)
            dgt_ref[rows, :] = flat(dgt)
            return 0

        lax.fori_loop(0, nc // nb, rest, 0)

    seq = pltpu.VMEM((S, HEAD_DIM), F32)
    small = pltpu.VMEM((nc, 8, LANES), F32)
    return pl.pallas_call(
        body, grid=(nh,),
        in_specs=_gdn_chunk_specs(nh, S, nc) + [
            pl.BlockSpec((None, nc, HEAD_DIM, HEAD_DIM), lambda h: (h, 0, 0, 0)),
            pl.BlockSpec((S, HEAD_DIM), lambda h: (0, h))],
        out_specs=[pl.BlockSpec((3, S, HEAD_DIM), lambda h: (0, 0, h)), pl.BlockSpec((None, S, LANES), lambda h: (h, 0, 0))],
        out_shape=[jax.ShapeDtypeStruct((3, S, GW), F32), jax.ShapeDtypeStruct((nh, S, LANES), F32)],
        scratch_shapes=[pltpu.VMEM((S, C), F32), seq, seq, pltpu.VMEM((S, C), F32), seq, seq, small, seq, seq, small],
        compiler_params=_params(("parallel",)), name=name)(qkv, gates, gc_row, states, do)


def _gdn_state_free_again(qkv_ref, gates_ref, gr_ref, t_s, g, nb, h, nh):
    C = CHUNK
    rows = pl.ds(pl.multiple_of(g * (nb * C), nb * C), nb * C)
    part = lambda x: x.reshape(nb, C, x.shape[-1])
    m = _gdn_chunk_common(part(qkv_ref[0, rows, :]), part(qkv_ref[1, rows, :]), part(qkv_ref[2, rows, :]),
                          part(gates_ref[rows, :]), gr_ref[pl.ds(g * nb, nb)], h, nh, tinv=part(t_s[rows, :]))
    return rows, m


def _gdn_chunk_grad(m, s0, v_new, dvn, dkt, dcd, dout, h, nh):
    C = CHUNK
    q, k, v, beta, decay, egc = m["q"], m["k"], m["v"], m["beta"], m["decay"], m["egc"]
    tinv, kt, cd = m["tinv"], m["kt"], m["cd"]
    dqd = _dot(dout, s0, _NT)
    damat = jnp.where(m["incl"], _dot(dout, v_new, _NT), 0.0)
    dw = -_dot(dvn, s0, _NT)
    dvb = _dot(tinv, dvn, _TN)
    dkbg = _dot(tinv, dw, _TN)
    dtinv = _dot(dvn, m["vb"], _NT) + _dot(dw, m["kbg"], _NT)
    dl = jnp.where(m["strict"], -_dot_hi(_dot_hi(tinv, dtinv, _TN), tinv, _NT), 0.0)
    dkk = dl * decay
    dqk = damat * decay
    dkb = _dot(dkk, k) + dkbg * egc
    dk = _dot(dkk, m["kb"], _TN) + _dot(dqk, q, _TN) + dkt * m["ekt"] + dkb * beta
    dq = _dot(dqk, k) + dqd * egc
    mm = dl * m["lmat"] + damat * m["amat"]
    ones = jnp.ones(q.shape, F32)
    rk = jnp.sum(dkt * kt, axis=-1, keepdims=True)
    dgc = (_dot_hi(mm, ones) - _dot_hi(mm, ones, _TN) + jnp.sum(dqd * m["qd"], axis=-1, keepdims=True) - rk
           + jnp.sum(dkbg * m["kbg"], axis=-1, keepdims=True))
    dglast = jnp.sum(rk, axis=-2, keepdims=True) + dcd * cd
    rowi = lax.broadcasted_iota(jnp.int32, q.shape, q.ndim - 2)
    lane = lax.broadcasted_iota(jnp.int32, q.shape, q.ndim - 1)
    dgc = dgc + jnp.where(rowi == C - 1, dglast, 0.0)
    dbeta = jnp.sum(dkb * k, axis=-1, keepdims=True) + jnp.sum(dvb * v, axis=-1, keepdims=True)
    dgt = jnp.where(lane == h, dbeta, 0.0) + jnp.where(lane == 2 * nh + h, dgc, 0.0)
    return dq, dk, dvb * beta, dgt


def _gdn_post_fwd(o, proj, ng, nh, *, name):
    S, GW = o.shape

    def body(o_ref, z_ref, g_ref, y_ref):
        ov, z = o_ref[...], z_ref[...]
        rstd = lax.rsqrt(jnp.mean(ov * ov, axis=-1, keepdims=True) + EPS)
        y_ref[...] = (ov * rstd * g_ref[...] * (z * _sigmoid(z))).astype(y_ref.dtype)

    blk = pl.BlockSpec((S, HEAD_DIM), lambda h: (0, h))
    return pl.pallas_call(
        body, grid=(nh,), in_specs=[blk, pl.BlockSpec((S, HEAD_DIM), lambda h: (0, 6 * nh + h)), pl.BlockSpec((1, HEAD_DIM), lambda h: (0, 0))],
        out_specs=blk, out_shape=jax.ShapeDtypeStruct((S, GW), BF16),
        compiler_params=_params(("parallel",)), name=name)(o, proj, ng)


def _gdn_post_bwd(o, proj, ng, dmix, nh, *, name):
    S, GW = o.shape

    def body(o_ref, z_ref, g_ref, d_ref, do_ref, dz_ref, dg_ref):
        ov, z, d = o_ref[...], z_ref[...], d_ref[...].astype(F32)
        rstd = lax.rsqrt(jnp.mean(ov * ov, axis=-1, keepdims=True) + EPS)
        oh = ov * rstd
        sz = _sigmoid(z)
        dy = d * (z * sz)
        dz_ref[...] = (d * (oh * g_ref[...]) * (sz * (1.0 + z * (1.0 - sz)))).astype(dz_ref.dtype)
        t = dy * g_ref[...]
        do_ref[...] = rstd * (t - oh * jnp.mean(t * oh, axis=-1, keepdims=True))
        part = jnp.sum(dy * oh, axis=0, keepdims=True)

        @pl.when(pl.program_id(0) == 0)
        def _():
            dg_ref[...] = part

        @pl.when(pl.program_id(0) > 0)
        def _():
            dg_ref[...] += part

    blk = pl.BlockSpec((S, HEAD_DIM), lambda h: (0, h))
    vec = pl.BlockSpec((1, HEAD_DIM), lambda h: (0, 0))
    return pl.pallas_call(
        body, grid=(nh,),
        in_specs=[blk, pl.BlockSpec((S, HEAD_DIM), lambda h: (0, 6 * nh + h)), vec, pl.BlockSpec((S, HEAD_DIM), lambda h: (0, nh + h))],
        out_specs=[blk, blk, vec],
        out_shape=[jax.ShapeDtypeStruct((S, GW), F32), jax.ShapeDtypeStruct((S, GW), BF16), jax.ShapeDtypeStruct((1, HEAD_DIM), F32)],
        compiler_params=_params(("arbitrary",)), name=name)(o, proj, ng, dmix)


def _gdn_forward(proj, conv_w, ab, ng, nh, tag):
    S = proj.shape[0]
    nc = S // CHUNK
    qkv = _gdn_qkv_fwd(proj, conv_w, nh, name=f"gdn_qkv_fwd{tag}")
    gates = _gdn_gates_fwd(proj, ab, nh, name=f"gdn_gates_fwd{tag}")
    gc_row = gates[:, 2 * nh:3 * nh].T.reshape(nh, nc, 1, CHUNK)
    o, states = _gdn_chunk_fwd(qkv, gates, gc_row, name=f"gdn_chunk_fwd{tag}")
    y = _gdn_post_fwd(o, proj, ng, nh, name=f"gdn_post_fwd{tag}")
    return y, (qkv, gates, gc_row, states, o)


def _gdn_backward(proj, conv_w, ab, ng, saved, dmix, nh, tag):
    qkv, gates, gc_row, states, o = saved
    do, dz, dng = _gdn_post_bwd(o, proj, ng, dmix, nh, name=f"gdn_post_bwd{tag}")
    dqkv, dgt_heads = _gdn_chunk_bwd(qkv, gates, gc_row, states, do, name=f"gdn_chunk_bwd{tag}")
    dx_qkv, dconv = _gdn_qkv_bwd(proj, conv_w, dqkv, nh, name=f"gdn_qkv_bwd{tag}")
    dx_g, dab = _gdn_gates_bwd(proj, ab, jnp.sum(dgt_heads, axis=0), nh, name=f"gdn_gates_bwd{tag}")
    return dx_qkv, dz, dx_g, dconv, dab, dng


def _row_tile(r, cap=128):
    t = cap
    while r % t:
        t //= 2
    assert t >= 8, r
    return t


def _adamw_update(gv, w_ref, m_ref, v_ref, d_ref, m2_ref, v2_ref):
    m2 = ADAM_B1 * m_ref[...] + (1.0 - ADAM_B1) * gv
    v2 = ADAM_B2 * v_ref[...] + (1.0 - ADAM_B2) * (gv * gv)
    m_hat = m2 / (1.0 - ADAM_B1 ** ADAM_STEP)
    v_hat = v2 / (1.0 - ADAM_B2 ** ADAM_STEP)
    d_ref[...] = -ADAM_LR * (m_hat / (jnp.sqrt(v_hat) + ADAM_EPS) + ADAM_WD * w_ref[...])
    m2_ref[...] = m2
    v2_ref[...] = v2


def _adamw(w, g, m, v, *, name):
    L, r, c = w.shape
    tr = _row_tile(r)

    def body(w_ref, g_ref, m_ref, v_ref, d_ref, m2_ref, v2_ref):
        _adamw_update(g_ref[...], w_ref, m_ref, v_ref, d_ref, m2_ref, v2_ref)

    blk = pl.BlockSpec((None, tr, c), lambda l, i: (l, i, 0))
    o = jax.ShapeDtypeStruct(w.shape, F32)
    return pl.pallas_call(
        body, grid=(L, r // tr), in_specs=[blk] * 4, out_specs=[blk] * 3, out_shape=[o, o, o],
        compiler_params=_params(("parallel", "parallel")), name=name)(w, g, m, v)


def _adamw_halves(w, g_own, g_sib, cvec, m, v, *, name):
    L, r, c = w.shape
    tr = _row_tile(r // 2)
    nbh = (r // 2) // tr

    def body(c_ref, w_ref, go_ref, gs_ref, m_ref, v_ref, g_out, d_ref, m2_ref, v2_ref):
        gv = jnp.where(pl.program_id(1) // nbh == c_ref[0], go_ref[...], gs_ref[...])
        g_out[...] = gv
        _adamw_update(gv, w_ref, m_ref, v_ref, d_ref, m2_ref, v2_ref)

    lo = lambda i: jnp.minimum(i, nbh - 1)
    hi = lambda i: jnp.maximum(i - nbh, 0)
    blk = pl.BlockSpec((None, tr, c), lambda l, i, c_ref: (l, i, 0))
    own = pl.BlockSpec((None, tr, c), lambda l, i, c_ref: (l, jnp.where(c_ref[0] == 0, lo(i), hi(i)), 0))
    sib = pl.BlockSpec((None, tr, c), lambda l, i, c_ref: (l, jnp.where(c_ref[0] == 0, hi(i), lo(i)), 0))
    o = jax.ShapeDtypeStruct(w.shape, F32)
    return pl.pallas_call(
        body,
        grid_spec=pltpu.PrefetchScalarGridSpec(
            num_scalar_prefetch=1, grid=(L, r // tr), in_specs=[blk, own, sib, blk, blk], out_specs=[blk] * 4),
        out_shape=[o, o, o, o],
        compiler_params=_params(("parallel", "arbitrary")), name=name)(cvec, w, g_own, g_sib, m, v)


def _sum_partials(p, rb, kc, prev, l, nl, *, name):
    _, _, h, c = rb.shape
    tr = _row_tile(h, 256)
    nb = h // tr
    others = [(0, 1), (1, 0), (1, 1), (2, 0), (2, 1), (3, 0), (3, 1)]

    def body(kc_ref, p_ref, *rest):
        acc = p_ref[...].astype(F32)
        for r_ref in rest[:7]:
            acc = acc + r_ref[...].astype(F32)
        rest[-1][...] = acc

    def slot(ds, dc):
        return pl.BlockSpec((None, None, tr, c), lambda i, kc_ref: ((kc_ref[0] + ds) % 4, (kc_ref[1] + dc) % 2, i, 0))

    in_specs = [pl.BlockSpec((None, tr, c), lambda i, kc_ref: (kc_ref[0], kc_ref[1] * nb + i, 0))]
    in_specs += [slot(ds, dc) for ds, dc in others]
    args = [kc, p] + [rb] * 7
    if prev is not None:
        in_specs.append(pl.BlockSpec(memory_space=pltpu.HBM))
        args.append(prev)
    return pl.pallas_call(
        body,
        grid_spec=pltpu.PrefetchScalarGridSpec(
            num_scalar_prefetch=1, grid=(nb,), in_specs=in_specs,
            out_specs=pl.BlockSpec((None, tr, c), lambda i, kc_ref: (l, i, 0))),
        out_shape=jax.ShapeDtypeStruct((nl, h, c), F32), input_output_aliases={9: 0} if prev is not None else {},
        compiler_params=_params(("parallel",)), name=name)(*args)


_MESH = pl.DeviceIdType.MESH
_HBM = pl.BlockSpec(memory_space=pltpu.HBM)


def _place():
    x, y, c = lax.axis_index("x"), lax.axis_index("y"), lax.axis_index("c")
    return x, y, c, [(1 - x, y), (x, 1 - y), (1 - x, 1 - y)]


def _cast_place(w, l, kvec, *, name):
    _, r, c = w.shape
    tr = _row_tile(r, 256)

    def body(k_ref, w_ref, o_ref):
        o_ref[...] = w_ref[...].astype(o_ref.dtype)

    return pl.pallas_call(
        body,
        grid_spec=pltpu.PrefetchScalarGridSpec(
            num_scalar_prefetch=1, grid=(r // tr,),
            in_specs=[pl.BlockSpec((None, tr, c), lambda i, k_ref: (l, i, 0))],
            out_specs=pl.BlockSpec((None, tr, c), lambda i, k_ref: (k_ref[0], i, 0))),
        out_shape=jax.ShapeDtypeStruct((4, r, c), BF16),
        compiler_params=_params(("parallel",)), name=name)(kvec, w)


_SEM = pl.BlockSpec(memory_space=pltpu.SEMAPHORE)
_ANY = pl.BlockSpec(memory_space=pl.ANY)
_EFFECT = pltpu.SideEffectType.DATAFLOW_SIDE_EFFECTING


def _in_hbm(a):
    return pltpu.with_memory_space_constraint(a, pltpu.HBM)


def _gather_copies(w_refs, send, recv, landing):
    x, y, c, chips = _place()
    k = 2 * x + y
    cps = []
    for a, w in enumerate(w_refs):
        h = w.shape[1] // 2
        for j, (cx, cy) in enumerate(chips):
            cps.append(pltpu.make_async_remote_copy(
                src_ref=w.at[k, pl.ds(c * h, h), :], dst_ref=w.at[2 * cx + cy if landing else k, pl.ds(c * h, h), :],
                send_sem=send.at[3 * a + j], recv_sem=recv.at[3 * a + j], device_id=(cx, cy, c), device_id_type=_MESH))
    return cps


def _gather_start(ws, after, *, name):
    n = len(ws)

    def body(*refs):
        send, recv = refs[n + 1], refs[n + 2]
        o_refs, token = refs[n + 3:2 * n + 3], refs[2 * n + 3]
        for cp in _gather_copies(o_refs, send, recv, landing=False):
            cp.start()
        token[...] = jnp.zeros_like(token)

    out = pl.pallas_call(
        body, in_specs=[_HBM] * n + [_ANY], out_specs=[_SEM, _SEM] + [_HBM] * n + [pl.BlockSpec(memory_space=pltpu.VMEM)],
        out_shape=[pltpu.SemaphoreType.DMA((3 * n,)), pltpu.SemaphoreType.DMA((3 * n,))]
        + [pltpu.HBM(w.shape, w.dtype) for w in ws] + [jax.ShapeDtypeStruct((8, LANES), F32)],
        input_output_aliases={a: 2 + a for a in range(n)},
        compiler_params=pltpu.CompilerParams(has_side_effects=_EFFECT), name=name)(*[_in_hbm(w) for w in ws], after)
    return out[0], out[1], list(out[2:2 + n]), out[2 + n]


def _gather_wait(send, recv, ws, after, *, name):
    n = len(ws)

    def body(*refs):
        for cp in _gather_copies(refs[:n], refs[n], refs[n + 1], landing=True):
            cp.wait_send()
            cp.wait_recv()

    return list(pl.pallas_call(
        body, in_specs=[_HBM] * n + [_SEM, _SEM, _ANY], out_specs=[_HBM] * n,
        out_shape=[pltpu.HBM(w.shape, w.dtype) for w in ws], input_output_aliases={a: a for a in range(n)},
        compiler_params=pltpu.CompilerParams(has_side_effects=_EFFECT), name=name)(*ws, send, recv, after))


def _gather_to_sibling(ws, *, name):
    n = len(ws)

    def body(*refs):
        o_refs = refs[n:2 * n]
        send, recv = refs[2 * n:]
        x, y, c, chips = _place()
        cps = []
        for a in range(n):
            h = o_refs[a].shape[1] // 2
            for j, (cx, cy) in enumerate(chips):
                landed = o_refs[a].at[2 * cx + cy, pl.ds(c * h, h), :]
                cp = pltpu.make_async_remote_copy(
                    src_ref=landed, dst_ref=landed, send_sem=send.at[3 * a + j], recv_sem=recv.at[3 * a + j],
                    device_id=(x, y, 1 - c), device_id_type=_MESH)
                cp.start()
                cps.append(cp)
        for a in range(n):
            h = o_refs[a].shape[1] // 2
            for j, (cx, cy) in enumerate(chips):
                other = o_refs[a].at[2 * cx + cy, pl.ds((1 - c) * h, h), :]
                pltpu.make_async_remote_copy(
                    src_ref=other, dst_ref=other, send_sem=send.at[3 * a + j], recv_sem=recv.at[3 * a + j],
                    device_id=(x, y, c), device_id_type=_MESH).wait_recv()
        for cp in cps:
            cp.wait_send()

    return list(pl.pallas_call(
        body, in_specs=[_HBM] * n, out_specs=[_HBM] * n,
        out_shape=[jax.ShapeDtypeStruct(w.shape, w.dtype) for w in ws],
        input_output_aliases={a: a for a in range(n)},
        scratch_shapes=[pltpu.SemaphoreType.DMA((3 * n,))] * 2, name=name)(*ws))


def _reduce_copies(p_refs, r_refs, send, recv, landing):
    x, y, c = lax.axis_index("x"), lax.axis_index("y"), lax.axis_index("c")
    k = 2 * x + y
    cps = []
    for a, (p, r) in enumerate(zip(p_refs, r_refs)):
        h = p.shape[1] // 2
        for d in range(1, 8):
            px = 1 - x if d & 4 else x
            py = 1 - y if d & 2 else y
            pc = 1 - c if d & 1 else c
            cps.append(pltpu.make_async_remote_copy(
                src_ref=p.at[2 * px + py, pl.ds(pc * h, h), :], dst_ref=r.at[2 * px + py, pc] if landing else r.at[k, c],
                send_sem=send.at[7 * a + d - 1], recv_sem=recv.at[7 * a + d - 1], device_id=(px, py, pc),
                device_id_type=_MESH))
    return cps


def _reduce_start(ps, after, *, name):
    n = len(ps)
    lands = [lax.empty((4, 2, p.shape[1] // 2, p.shape[2]), p.dtype) for p in ps]

    def body(*refs):
        send, recv = refs[2 * n + 1], refs[2 * n + 2]
        p_out, r_out, token = refs[2 * n + 3:3 * n + 3], refs[3 * n + 3:4 * n + 3], refs[4 * n + 3]
        for cp in _reduce_copies(p_out, r_out, send, recv, landing=False):
            cp.start()
        token[...] = jnp.zeros_like(token)

    out = pl.pallas_call(
        body, in_specs=[_HBM] * (2 * n) + [_ANY],
        out_specs=[_SEM, _SEM] + [_HBM] * (2 * n) + [pl.BlockSpec(memory_space=pltpu.VMEM)],
        out_shape=[pltpu.SemaphoreType.DMA((7 * n,)), pltpu.SemaphoreType.DMA((7 * n,))]
        + [pltpu.HBM(a.shape, a.dtype) for a in list(ps) + lands] + [jax.ShapeDtypeStruct((8, LANES), F32)],
        input_output_aliases={a: 2 + a for a in range(2 * n)},
        compiler_params=pltpu.CompilerParams(has_side_effects=_EFFECT), name=name)(
            *[_in_hbm(p) for p in ps], *[_in_hbm(r) for r in lands], after)
    return out[0], out[1], list(out[2:2 + n]), list(out[2 + n:2 + 2 * n]), out[2 + 2 * n]


def _reduce_wait(send, recv, ps, lands, after, *, name):
    n = len(ps)

    def body(*refs):
        for cp in _reduce_copies(refs[:n], refs[n:2 * n], refs[2 * n], refs[2 * n + 1], landing=True):
            cp.wait_send()
            cp.wait_recv()

    out = pl.pallas_call(
        body, in_specs=[_HBM] * (2 * n) + [_SEM, _SEM, _ANY], out_specs=[_HBM] * (2 * n),
        out_shape=[pltpu.HBM(a.shape, a.dtype) for a in list(ps) + list(lands)],
        input_output_aliases={a: a for a in range(2 * n)},
        compiler_params=pltpu.CompilerParams(has_side_effects=_EFFECT), name=name)(*ps, *lands, send, recv, after)
    return list(out[:n]), list(out[n:])


def _swap_with_sibling(gs, *, name):
    n = len(gs)

    def body(*refs):
        g_refs, o_refs = refs[:n], refs[n:2 * n]
        send, recv = refs[2 * n:]
        x, y, c, _ = _place()
        cps = []
        for a in range(n):
            cp = pltpu.make_async_remote_copy(
                src_ref=g_refs[a], dst_ref=o_refs[a], send_sem=send.at[a], recv_sem=recv.at[a],
                device_id=(x, y, 1 - c), device_id_type=_MESH)
            cp.start()
            cps.append(cp)
        for cp in cps:
            cp.wait_recv()
        for cp in cps:
            cp.wait_send()

    return pl.pallas_call(
        body, in_specs=[_HBM] * n, out_specs=[_HBM] * n,
        out_shape=[jax.ShapeDtypeStruct(g.shape, g.dtype) for g in gs],
        scratch_shapes=[pltpu.SemaphoreType.DMA((n,))] * 2, name=name)(*gs)


def _allreduce_small(v, *, name):
    R = v.shape[0]

    def body(v_ref, o_ref, buf, send, recv, loc):
        x, y, c = lax.axis_index("x"), lax.axis_index("y"), lax.axis_index("c")
        me = 4 * x + 2 * y + c
        mine = pltpu.make_async_copy(v_ref, buf.at[me], loc)
        mine.start()
        cps = []
        for d in range(1, 8):
            px = 1 - x if d & 4 else x
            py = 1 - y if d & 2 else y
            pc = 1 - c if d & 1 else c
            cp = pltpu.make_async_remote_copy(
                src_ref=v_ref, dst_ref=buf.at[me], send_sem=send.at[d - 1], recv_sem=recv.at[d - 1],
                device_id=(px, py, pc), device_id_type=_MESH)
            cp.start()
            cps.append((cp, 4 * px + 2 * py + pc))
        for d in range(1, 8):
            cp, peer = cps[d - 1]
            pltpu.make_async_remote_copy(
                src_ref=buf.at[peer], dst_ref=buf.at[peer], send_sem=send.at[d - 1], recv_sem=recv.at[d - 1],
                device_id=(x, y, c), device_id_type=_MESH).wait_recv()
        for cp, _ in cps:
            cp.wait_send()
        mine.wait()
        acc = buf[0]
        for i in range(1, 8):
            acc = acc + buf[i]
        o_ref[...] = acc

    return pl.pallas_call(
        body, in_specs=[pl.BlockSpec(memory_space=pltpu.VMEM)], out_specs=pl.BlockSpec(memory_space=pltpu.VMEM),
        out_shape=jax.ShapeDtypeStruct((R, LANES), F32),
        scratch_shapes=[pltpu.VMEM((8, R, LANES), F32), pltpu.SemaphoreType.DMA((7,)), pltpu.SemaphoreType.DMA((7,)),
                        pltpu.SemaphoreType.DMA],
        compiler_params=pltpu.CompilerParams(vmem_limit_bytes=VMEM_LIMIT), name=name)(v)


def _pack(arrs, row_multiple=8):
    rows = []
    for a in arrs:
        flat = a.reshape(-1)
        flat = jnp.pad(flat, (0, (-flat.shape[0]) % LANES))
        rows.append(flat.reshape(-1, LANES))
    buf = jnp.concatenate(rows, axis=0)
    return jnp.pad(buf, ((0, (-buf.shape[0]) % row_multiple), (0, 0)))


def _unpack(buf, shapes):
    out, r = [], 0
    for s in shapes:
        size = math.prod(s)
        nr = -(-size // LANES)
        out.append(buf[r:r + nr].reshape(-1)[:size].reshape(s))
        r += nr
    return out


def kernel(x, mem, mix_norm, w_in, gdn_conv, gdn_a_log, gdn_dt_bias, gdn_norm, w_out, xattn_norm, mem_norm, w_xq, w_xkv, w_xo, ffn_norm, w_up, ffn_conv, ffn_conv_bias, w_down, final_norm, loss_target, m_mix_norm, m_w_in, m_gdn_conv, m_gdn_a_log, m_gdn_dt_bias, m_gdn_norm, m_w_out, m_xattn_norm, m_mem_norm, m_w_xq, m_w_xkv, m_w_xo, m_ffn_norm, m_w_up, m_ffn_conv, m_ffn_conv_bias, m_w_down, m_final_norm, v_mix_norm, v_w_in, v_gdn_conv, v_gdn_a_log, v_gdn_dt_bias, v_gdn_norm, v_w_out, v_xattn_norm, v_mem_norm, v_w_xq, v_w_xkv, v_w_xo, v_ffn_norm, v_w_up, v_ffn_conv, v_ffn_conv_bias, v_w_down, v_final_norm):
    L = w_in.shape[0]
    _, S, D = x.shape
    nh = D // (2 * HEAD_DIM)
    GW = nh * HEAD_DIM
    n_in = 7 * GW + 2 * nh
    NP = 7 * GW + LANES
    XW = X_HEADS * HEAD_DIM
    F = w_down.shape[1] * 4
    cs_in = w_in.shape[2]
    cs_up = w_up.shape[2]
    cs_xo = w_xo.shape[2]
    tu = _tile(cs_up, 1408)
    per = cs_up // tu
    fper = F // tu
    assert n_in == 4 * cs_in and F % tu == 0 and 2 * F == 4 * cs_up

    xi, yi, ci = lax.axis_index("x"), lax.axis_index("y"), lax.axis_index("c")
    chip = 2 * xi + yi
    cvec = jnp.reshape(ci, (1,)).astype(jnp.int32)

    cs_gc, cs_fc = gdn_conv.shape[2], ffn_conv.shape[2]
    keep = jnp.where(ci == 0, 1.0, 0.0).astype(F32)
    gc_full = lax.dynamic_update_slice(jnp.zeros((L, SHORT_CONV, 4 * cs_gc), F32), gdn_conv * keep, (0, 0, chip * cs_gc))
    fc_full = lax.dynamic_update_slice(jnp.zeros((L, FFN_CONV, 4 * cs_fc), F32), ffn_conv * keep, (0, 0, chip * cs_fc))
    conv_all = _allreduce_small(_pack([gc_full, fc_full]), name="allgather_conv")
    gdn_conv_full, ffn_conv_full = _unpack(conv_all, [gc_full.shape, fc_full.shape])

    big = [w_in, w_out, w_xq, w_xkv, w_xo, w_up, w_down]
    kvec = jnp.reshape(chip, (1,)).astype(jnp.int32)
    kcvec = jnp.stack([chip, ci]).astype(jnp.int32)
    ab = jnp.zeros((L, 2, LANES), F32).at[:, 0, nh:2 * nh].set(gdn_a_log).at[:, 1, nh:2 * nh].set(gdn_dt_bias)

    def vec(p, l):
        return p[l:l + 1]

    xo_fwd_b = pl.BlockSpec((None, XW, cs_xo), lambda i, j, k: (j, 0, 0))
    xo_dg_b = pl.BlockSpec((None, XW, cs_xo), lambda i, j, k: (k, 0, 0))
    xo_wg_o = pl.BlockSpec((None, XW, cs_xo), lambda i, j, k: (j, 0, 0))
    up_fwd_b = pl.BlockSpec((None, D, tu), lambda i, j, k: (j // per, 0, j % per))

    def fwd_layer(l, xc, g_in, rest, token):
        g_mix = vec(mix_norm, l) if token is None else vec(mix_norm, l) + token[0:1, 0:1]
        w_in_l = jnp.concatenate([g_in[0], g_in[1], g_in[2], g_in[3], jnp.zeros((D, NP - n_in), BF16)], axis=1)
        s = dict(x0=xc, w_in=w_in_l)
        s["h"] = _rms_fwd(xc, g_mix, name="rms_mix_fwd")
        s["proj"] = _matmul(s["h"], w_in_l, tn=2432, tk=D, name="mm_in_fwd")
        s["sb"], s["tot"] = _sb_fwd(s["proj"], nh, name="sb_fwd")
        gdn_out, s["gdn"] = _gdn_forward(s["proj"], gdn_conv_full[l], ab[l], vec(gdn_norm, l), nh, "")
        g_out, g_xq, g_xkv, g_xo, g_up, g_down = rest(gdn_out)
        w_out_l, w_xq_l, w_xkv_l, w_down_l = g_out.reshape(2 * GW, D), g_xq.reshape(D, XW), g_xkv.reshape(D, 2 * XW), g_down.reshape(F, D)
        s.update(w_out=w_out_l, w_xq=w_xq_l, w_xkv=w_xkv_l, w_xo=g_xo, w_up=g_up, w_down=w_down_l)
        s["mixed"] = jnp.concatenate([s["sb"], gdn_out], axis=1)
        s["x1"] = _matmul(s["mixed"], w_out_l, res=xc, tm=1024, tn=1024, tk=2 * GW, name="mm_out_fwd")
        s["memn"] = _rms_fwd(mem[0], vec(mem_norm, l), name="rms_mem_fwd")
        s["kv"] = _matmul(s["memn"], w_xkv_l, out_dtype=BF16, tk=D, name="mm_xkv_fwd")
        s["hq"] = _rms_fwd(s["x1"], vec(xattn_norm, l), name="rms_xattn_fwd")
        s["q"] = _matmul(s["hq"], w_xq_l, out_dtype=BF16, tk=D, name="mm_xq_fwd")
        s["xo"] = _xattn_fwd(s["q"], s["kv"], name="xattn_fwd")
        s["x2"] = _matmul(s["xo"], g_xo, res=s["x1"], dims=(S, D, XW), tn=cs_xo, tk=XW, b_spec=xo_fwd_b, name="mm_xo_fwd")
        s["hf"] = _rms_fwd(s["x2"], vec(ffn_norm, l), name="rms_ffn_fwd")
        s["u"] = _matmul(s["hf"], g_up, dims=(S, 2 * F, D), tn=tu, tk=D, b_spec=up_fwd_b, name="mm_up_fwd")
        s["act"] = _ffn_act_fwd(s["u"], ffn_conv_full[l], ffn_conv_bias[l:l + 1], name="ffn_act_fwd")
        x3 = _matmul(s["act"], w_down_l, res=s["x2"], tm=1024, tn=1024, tk=tu, name="mm_down_fwd")
        return x3, s

    def bwd_ffn(l, s, dx3, dx3b):
        dact = _matmul(dx3b, s["w_down"], tb=True, tm=1024, tk=D, name="mm_down_dgrad")
        d_down = _matmul(s["act"], dx3b, ta=True, tn=1024, tk=S, out_dtype=BF16, name="mm_down_wgrad")
        du3, dcw3, dcb3 = _ffn_act_bwd(s["u"], ffn_conv_full[l], ffn_conv_bias[l:l + 1], dact, name="ffn_act_bwd")
        tq, td = _tile(S, 1024), _tile(D, 1024)
        dhf = _matmul(du3, s["w_up"], tb=True, dims=(S, D, 2 * F), tm=tq, tn=td, tk=tu,
                      a_spec=pl.BlockSpec((None, tq, tu), lambda i, j, k: (k // fper, i, k % fper)),
                      b_spec=pl.BlockSpec((None, td, tu), lambda i, j, k: (k // per, j, k % per)), name="mm_up_dgrad")
        d_up = _matmul(s["hf"], du3, ta=True, dims=(D, 2 * F, S), tn=tu, tk=S,
                       b_spec=pl.BlockSpec((None, S, tu), lambda i, j, k: (j // fper, 0, j % fper)),
                       o_spec=pl.BlockSpec((None, _tile(D, 512), tu), lambda i, j, k: (j // per, i, j % per)),
                       out_shape=jax.ShapeDtypeStruct((4, D, cs_up), BF16), name="mm_up_wgrad")
        dx2, dx2b, dg_ffn = _rms_bwd(s["x2"], vec(ffn_norm, l), dhf, dx3, name="rms_bwd")
        small = [dg_ffn, jnp.concatenate([dcw3[0], dcw3[1]], axis=1), jnp.concatenate([dcb3[0], dcb3[1]], axis=1)]
        return dx2, dx2b, {5: d_up, 6: d_down.reshape(4, -1, D)}, small

    def bwd_rest(l, s, dx2, dx2b):
        dxo = _matmul(dx2b, s["w_xo"], tb=True, dims=(S, XW, D), tn=XW, tk=cs_xo, b_spec=xo_dg_b, name="mm_xo_dgrad")
        d_xo = _matmul(s["xo"], dx2b, ta=True, dims=(XW, D, S), tm=XW, tn=cs_xo, tk=S, o_spec=xo_wg_o,
                       out_shape=jax.ShapeDtypeStruct((4, XW, cs_xo), BF16), name="mm_xo_wgrad")
        dq, dk, dv = _xattn_bwd(s["q"], s["kv"], dxo, name="xattn_bwd")
        dkv = jnp.concatenate([dk, dv], axis=1)
        dhq = _matmul(dq, s["w_xq"], tb=True, tk=XW, name="mm_xq_dgrad")
        d_xq = _matmul(s["hq"], dq, ta=True, tk=S, out_dtype=BF16, name="mm_xq_wgrad")
        dmemn = _matmul(dkv, s["w_xkv"], tb=True, tk=2 * XW, name="mm_xkv_dgrad")
        d_xkv = _matmul(s["memn"], dkv, ta=True, tk=mem.shape[1], out_dtype=BF16, name="mm_xkv_wgrad")
        _, _, dg_mem = _rms_bwd(mem[0], vec(mem_norm, l), dmemn, None, name="rms_mem_bwd")
        dx1, dx1b, dg_xattn = _rms_bwd(s["x1"], vec(xattn_norm, l), dhq, dx2, name="rms_bwd")
        dmix = _matmul(dx1b, s["w_out"], tb=True, tm=1024, tn=1024, tk=D, name="mm_out_dgrad")
        d_out = _matmul(s["mixed"], dx1b, ta=True, tm=1024, tn=1024, tk=S, out_dtype=BF16, name="mm_out_wgrad")
        dq_s, dk_s, dv_s = _sb_bwd(s["proj"], s["tot"], dmix, nh, name="sb_bwd")
        dx_qkv, dz, dx_g, dconv, dab, dng = _gdn_backward(s["proj"], gdn_conv_full[l], ab[l], vec(gdn_norm, l), s["gdn"], dmix, nh, "")
        dproj = jnp.concatenate([dq_s, dk_s, dv_s, dx_qkv, dz, dx_g], axis=1)
        dh = _matmul(dproj, s["w_in"], tb=True, tm=1024, tn=1024, tk=2432, name="mm_in_dgrad")
        d_in = _matmul(s["h"], dproj, ta=True, tn=2432, tk=S, out_dtype=BF16, name="mm_in_wgrad")
        dx0, dx0b, dg_mix = _rms_bwd(s["x0"], vec(mix_norm, l), dh, dx1, name="rms_bwd")
        slabs = {0: jnp.stack([d_in[:, i * cs_in:(i + 1) * cs_in] for i in range(4)]), 1: d_out.reshape(4, -1, D),
                 2: d_xq.reshape(4, -1, XW), 3: d_xkv.reshape(4, -1, 2 * XW), 4: d_xo}
        return dx0, dx0b, slabs, [dg_mix, dconv, dab, dng, dg_xattn, dg_mem]

    def start_gather(l, idxs, after, tag):
        placed = [_cast_place(big[i], l, kvec, name=f"cast_place_{l}") for i in idxs]
        return _gather_start(placed, after, name=f"gather_start_{l}{tag}")

    def end_gather(pending, after, l, tag):
        send, recv, ws, _ = pending
        ws = _gather_wait(send, recv, ws, after, name=f"gather_wait_{l}{tag}")
        return _gather_to_sibling(ws, name=f"gather_to_sibling{tag}")

    xc = x[0]
    saved = []
    first = start_gather(0, [0], conv_all, "a")
    second = start_gather(0, list(range(1, 7)), first[3], "b")
    for l in range(L):
        if l == 0:
            g_in = end_gather(first, xc, 0, "a")[0]
            rest = lambda after: end_gather(second, after, 0, "b")
            order = second[3]
        else:
            wts = end_gather(pending, xc, l, "")
            g_in, rest, order = wts[0], (lambda after, wts=wts: wts[1:]), wts[1]
        token = None
        if l + 1 < L:
            pending = start_gather(l + 1, list(range(7)), order, "")
            token = pending[3]
        xc, s = fwd_layer(l, xc, g_in, rest, token)
        saved.append(s)
    loss_blk, dxc, dxcb, dg_final = _loss_head(xc, final_norm[None, :], loss_target[0], name="loss_head")

    def start_reduce(slabs, l, tag):
        idxs = sorted(slabs)
        return idxs, l, tag, _reduce_start([slabs[i] for i in idxs], cvec, name=f"reduce_start_{l}{tag}")

    def finish_reduce(item, sums, after):
        idxs, l, tag, (send, recv, ps, lands, _) = item
        ps, lands = _reduce_wait(send, recv, ps, lands, after, name=f"reduce_wait_{l}{tag}")
        for i, p, rb in zip(idxs, ps, lands):
            sums[i] = _sum_partials(p, rb, kcvec, sums[i], l, L, name=f"sum_partials_{l}")

    sums = [None] * 7
    small_by_layer = [None] * L
    in_flight = []
    for l in reversed(range(L)):
        if in_flight:
            dxcb = dxcb + in_flight[-1][3][4][0, 0].astype(BF16)
        dx2, dx2b, slabs_ffn, small_ffn = bwd_ffn(l, saved[l], dxc, dxcb)
        for item in in_flight:
            finish_reduce(item, sums, dx2)
        in_flight = [start_reduce(slabs_ffn, l, "f")]
        dx2b = dx2b + in_flight[-1][3][4][0, 0].astype(BF16)
        dxc, dxcb, slabs_rest, small_rest = bwd_rest(l, saved[l], dx2, dx2b)
        saved[l] = None
        in_flight.append(start_reduce(slabs_rest, l, "r"))
        small_by_layer[l] = small_rest + small_ffn

    small_flat = [a for l in range(L) for a in small_by_layer[l]] + [dg_final, loss_blk[0:1]]
    red_buf = _allreduce_small(_pack(small_flat), name="allreduce_small")
    red = _unpack(red_buf, [a.shape for a in small_flat])
    per_layer = [red[9 * l:9 * l + 9] for l in range(L)]
    col = lambda i: jnp.concatenate([p[i] for p in per_layer], axis=0)
    stk = lambda i: jnp.stack([p[i] for p in per_layer])
    g_conv_full, g_ab, g_fconv_full = stk(1), stk(2), stk(7)
    grads_small = dict(
        mix_norm=col(0), gdn_conv=lax.dynamic_slice(g_conv_full, (0, 0, chip * cs_gc), (L, SHORT_CONV, cs_gc)),
        gdn_a_log=g_ab[:, 0, nh:2 * nh], gdn_dt_bias=g_ab[:, 1, nh:2 * nh], gdn_norm=col(3), xattn_norm=col(4),
        mem_norm=col(5), ffn_norm=col(6), ffn_conv=lax.dynamic_slice(g_fconv_full, (0, 0, chip * cs_fc), (L, FFN_CONV, cs_fc)),
        ffn_conv_bias=col(8), final_norm=red[-2][0])
    loss = red[-1][0, 0]

    names_small = ["mix_norm", "gdn_conv", "gdn_a_log", "gdn_dt_bias", "gdn_norm", "xattn_norm", "mem_norm", "ffn_norm",
                   "ffn_conv", "ffn_conv_bias", "final_norm"]
    w_small = dict(mix_norm=mix_norm, gdn_conv=gdn_conv, gdn_a_log=gdn_a_log, gdn_dt_bias=gdn_dt_bias, gdn_norm=gdn_norm,
                   xattn_norm=xattn_norm, mem_norm=mem_norm, ffn_norm=ffn_norm, ffn_conv=ffn_conv, ffn_conv_bias=ffn_conv_bias,
                   final_norm=final_norm)
    m_small = dict(mix_norm=m_mix_norm, gdn_conv=m_gdn_conv, gdn_a_log=m_gdn_a_log, gdn_dt_bias=m_gdn_dt_bias, gdn_norm=m_gdn_norm,
                   xattn_norm=m_xattn_norm, mem_norm=m_mem_norm, ffn_norm=m_ffn_norm, ffn_conv=m_ffn_conv,
                   ffn_conv_bias=m_ffn_conv_bias, final_norm=m_final_norm)
    v_small = dict(mix_norm=v_mix_norm, gdn_conv=v_gdn_conv, gdn_a_log=v_gdn_a_log, gdn_dt_bias=v_gdn_dt_bias, gdn_norm=v_gdn_norm,
                   xattn_norm=v_xattn_norm, mem_norm=v_mem_norm, ffn_norm=v_ffn_norm, ffn_conv=v_ffn_conv,
                   ffn_conv_bias=v_ffn_conv_bias, final_norm=v_final_norm)
    shapes_small = [w_small[n].shape for n in names_small]
    packed = [_pack([d[n] for n in names_small], row_multiple=128)[None] for d in (w_small, grads_small, m_small, v_small)]
    upd_small = [_unpack(o[0], shapes_small) for o in _adamw(*packed, name="adamw_small")]
    delta, new_m, new_v = [dict(zip(names_small, u)) for u in upd_small]
    grads = dict(grads_small)
    big_names = ["w_in", "w_out", "w_xq", "w_xkv", "w_xo", "w_up", "w_down"]
    big_m = [m_w_in, m_w_out, m_w_xq, m_w_xkv, m_w_xo, m_w_up, m_w_down]
    big_v = [v_w_in, v_w_out, v_w_xq, v_w_xkv, v_w_xo, v_w_up, v_w_down]
    after = red_buf
    for item in in_flight:
        finish_reduce(item, sums, after)
        idxs, tag = item[0], item[2]
        from_sib = _swap_with_sibling([sums[i] for i in idxs], name=f"swap_halves_{tag}")
        for i, gs in zip(idxs, from_sib):
            n = big_names[i]
            grads[n], delta[n], new_m[n], new_v[n] = _adamw_halves(big[i], sums[i], gs, cvec, big_m[i], big_v[i],
                                                                   name=f"adamw_{n}")
        after = delta[big_names[idxs[-1]]]

    order = ["mix_norm", "w_in", "gdn_conv", "gdn_a_log", "gdn_dt_bias", "gdn_norm", "w_out", "xattn_norm", "mem_norm", "w_xq",
             "w_xkv", "w_xo", "ffn_norm", "w_up", "ffn_conv", "ffn_conv_bias", "w_down", "final_norm"]
    return (loss, dxc[None], *[grads[n] for n in order], *[delta[n] for n in order], *[new_m[n] for n in order],
            *[new_v[n] for n in order])
```

```python
import functools
import math

import jax
import jax.numpy as jnp
from jax import lax
from jax.experimental import pallas as pl
from jax.experimental.pallas import tpu as pltpu

F32 = jnp.float32
BF16 = jnp.bfloat16

HEAD_DIM = 128
CHUNK = 64
GDN_CPB = 16
SB_TQ, SB_TK = 512, 512
SHORT_CONV = 4
FFN_CONV = 3
X_HEADS = 4
EPS = 1e-6
LANES = 128
VMEM_LIMIT = 56 * 2**20

ADAM_LR, ADAM_B1, ADAM_B2, ADAM_EPS, ADAM_WD, ADAM_STEP = 0.001, 0.9, 0.999, 1e-08, 0.01, 10

HI = lax.Precision.HIGH


def _params(sem):
    return pltpu.CompilerParams(dimension_semantics=sem, vmem_limit_bytes=VMEM_LIMIT)


def _tile(n, want):
    if n <= want:
        return n
    t = (want // LANES) * LANES
    while t > LANES and n % t:
        t -= LANES
    assert n % t == 0, (n, want)
    return t


def _sigmoid(x):
    return jax.nn.sigmoid(x)


def _softplus(x):
    return jnp.maximum(x, 0.0) + jnp.log(1.0 + jnp.exp(-jnp.abs(x)))


def _matmul(a, b, *, name, ta=False, tb=False, out_dtype=F32, res=None, tm=512, tn=512, tk=2048,
            dims=None, a_spec=None, b_spec=None, o_spec=None, out_shape=None):
    if dims is None:
        M, K = (a.shape[1], a.shape[0]) if ta else a.shape
        N = b.shape[0] if tb else b.shape[1]
    else:
        M, N, K = dims
    tm, tn, tk = _tile(M, tm), _tile(N, tn), _tile(K, tk)
    nk = K // tk
    dn = (((0 if ta else 1,), (1 if tb else 0,)), ((), ()))

    def body(*refs):
        a_ref, b_ref = refs[0], refs[1]
        r_ref = refs[2] if res is not None else None
        o_ref = refs[3] if res is not None else refs[2]
        p = lax.dot_general(a_ref[...].astype(BF16), b_ref[...].astype(BF16), dn, preferred_element_type=F32)

        def finish(acc):
            if r_ref is not None:
                acc = acc + r_ref[...].astype(F32)
            o_ref[...] = acc.astype(o_ref.dtype)

        if nk == 1:
            finish(p)
        else:
            acc_ref = refs[-1]
            k = pl.program_id(2)

            @pl.when(k == 0)
            def _():
                acc_ref[...] = p

            @pl.when(jnp.logical_and(k > 0, k < nk - 1))
            def _():
                acc_ref[...] += p

            @pl.when(k == nk - 1)
            def _():
                finish(acc_ref[...] + p)

    if a_spec is None:
        a_spec = pl.BlockSpec((tk, tm), lambda i, j, k: (k, i)) if ta else pl.BlockSpec((tm, tk), lambda i, j, k: (i, k))
    if b_spec is None:
        b_spec = pl.BlockSpec((tn, tk), lambda i, j, k: (j, k)) if tb else pl.BlockSpec((tk, tn), lambda i, j, k: (k, j))
    if o_spec is None:
        o_spec = pl.BlockSpec((tm, tn), lambda i, j, k: (i, j))
    if out_shape is None:
        out_shape = jax.ShapeDtypeStruct((M, N), out_dtype)
    in_specs, args = [a_spec, b_spec], [a, b]
    if res is not None:
        in_specs.append(pl.BlockSpec((tm, tn), lambda i, j, k: (i, j)))
        args.append(res)
    return pl.pallas_call(
        body, grid=(M // tm, N // tn, nk), in_specs=in_specs, out_specs=o_spec, out_shape=out_shape,
        scratch_shapes=[pltpu.VMEM((tm, tn), F32)] if nk > 1 else [],
        compiler_params=_params(("parallel", "parallel", "arbitrary")), name=name)(*args)


def _rms_fwd(x, g, *, name):
    R, D = x.shape
    tr = _tile(R, 256)

    def body(x_ref, g_ref, o_ref):
        xv = x_ref[...]
        rstd = lax.rsqrt(jnp.mean(xv * xv, axis=-1, keepdims=True) + EPS)
        o_ref[...] = (xv * rstd * g_ref[...]).astype(o_ref.dtype)

    return pl.pallas_call(
        body, grid=(R // tr,), in_specs=[pl.BlockSpec((tr, D), lambda i: (i, 0)), pl.BlockSpec((1, D), lambda i: (0, 0))],
        out_specs=pl.BlockSpec((tr, D), lambda i: (i, 0)), out_shape=jax.ShapeDtypeStruct((R, D), BF16),
        compiler_params=_params(("parallel",)), name=name)(x, g)


def _rms_bwd(x, g, dh, dres, *, name):
    R, D = x.shape
    tr = _tile(R, 256)

    def body(*refs):
        if dres is None:
            x_ref, g_ref, dh_ref, dx_ref, dxb_ref, dg_ref = refs
        else:
            x_ref, g_ref, dh_ref, dr_ref, dx_ref, dxb_ref, dg_ref = refs
        xv = x_ref[...]
        dhv = dh_ref[...].astype(F32)
        rstd = lax.rsqrt(jnp.mean(xv * xv, axis=-1, keepdims=True) + EPS)
        xhat = xv * rstd
        t = dhv * g_ref[...]
        dx = rstd * (t - xhat * jnp.mean(t * xhat, axis=-1, keepdims=True))
        if dres is not None:
            dx = dx + dr_ref[...]
        dx_ref[...] = dx
        dxb_ref[...] = dx.astype(BF16)
        part = jnp.sum(dhv * xhat, axis=0, keepdims=True)

        @pl.when(pl.program_id(0) == 0)
        def _():
            dg_ref[...] = part

        @pl.when(pl.program_id(0) > 0)
        def _():
            dg_ref[...] += part

    row = pl.BlockSpec((tr, D), lambda i: (i, 0))
    vec = pl.BlockSpec((1, D), lambda i: (0, 0))
    in_specs = [row, vec, row] + ([row] if dres is not None else [])
    args = [x, g, dh] + ([dres] if dres is not None else [])
    return pl.pallas_call(
        body, grid=(R // tr,), in_specs=in_specs, out_specs=[row, row, vec],
        out_shape=[jax.ShapeDtypeStruct((R, D), F32), jax.ShapeDtypeStruct((R, D), BF16), jax.ShapeDtypeStruct((1, D), F32)],
        compiler_params=_params(("arbitrary",)), name=name)(*args)


def _loss_head(x, g, tgt, *, name):
    R, D = x.shape
    tr = _tile(R, 256)

    def body(x_ref, g_ref, t_ref, l_ref, dx_ref, dxb_ref, dg_ref):
        xv = x_ref[...]
        rstd = lax.rsqrt(jnp.mean(xv * xv, axis=-1, keepdims=True) + EPS)
        xhat = xv * rstd
        err = xhat * g_ref[...] - t_ref[...]
        dy = err * (1.0 / D)
        t = dy * g_ref[...]
        dx = rstd * (t - xhat * jnp.mean(t * xhat, axis=-1, keepdims=True))
        dx_ref[...] = dx
        dxb_ref[...] = dx.astype(BF16)
        part = jnp.sum(dy * xhat, axis=0, keepdims=True)
        lpart = jnp.zeros((8, LANES), F32) + 0.5 * jnp.sum(jnp.mean(err * err, axis=-1, keepdims=True))

        @pl.when(pl.program_id(0) == 0)
        def _():
            dg_ref[...] = part
            l_ref[...] = lpart

        @pl.when(pl.program_id(0) > 0)
        def _():
            dg_ref[...] += part
            l_ref[...] += lpart

    row = pl.BlockSpec((tr, D), lambda i: (i, 0))
    vec = pl.BlockSpec((1, D), lambda i: (0, 0))
    return pl.pallas_call(
        body, grid=(R // tr,), in_specs=[row, vec, row],
        out_specs=[pl.BlockSpec((8, LANES), lambda i: (0, 0)), row, row, vec],
        out_shape=[jax.ShapeDtypeStruct((8, LANES), F32), jax.ShapeDtypeStruct((R, D), F32),
                   jax.ShapeDtypeStruct((R, D), BF16), jax.ShapeDtypeStruct((1, D), F32)],
        compiler_params=_params(("arbitrary",)), name=name)(x, g, tgt)


def _shift_down(x, s):
    if s == 0:
        return x
    row = lax.broadcasted_iota(jnp.int32, x.shape, 0)
    return jnp.where(row >= s, pltpu.roll(x, s, 0), 0.0)


def _shift_up(x, s):
    if s == 0:
        return x
    n = x.shape[0]
    row = lax.broadcasted_iota(jnp.int32, x.shape, 0)
    return jnp.where(row < n - s, pltpu.roll(x, n - s, 0), 0.0)


def _dwconv(x, w):
    k = w.shape[0]
    acc = x * w[k - 1:k, :]
    for i in range(k - 1):
        acc = acc + _shift_down(x, k - 1 - i) * w[i:i + 1, :]
    return acc


def _dwconv_bwd(x, w, dc):
    k = w.shape[0]
    dx = dc * w[k - 1:k, :]
    dws = []
    for i in range(k - 1):
        s = k - 1 - i
        dx = dx + _shift_up(dc, s) * w[i:i + 1, :]
        dws.append(jnp.sum(dc * _shift_down(x, s), axis=0, keepdims=True))
    dws.append(jnp.sum(dc * x, axis=0, keepdims=True))
    return dx, jnp.concatenate(dws, axis=0)


def _ffn_act_fwd(u, cw, cb, *, name):
    S, F2 = u.shape
    F = F2 // 2
    tc = _tile(F, 256)
    nb = F // tc

    def body(ug_ref, uu_ref, wg_ref, wu_ref, bg_ref, bu_ref, o_ref):
        cg = _dwconv(ug_ref[...], wg_ref[...]) + bg_ref[...]
        cu = _dwconv(uu_ref[...], wu_ref[...]) + bu_ref[...]
        o_ref[...] = (cg * _sigmoid(cg) * cu).astype(o_ref.dtype)

    blk = lambda r, off: pl.BlockSpec((r, tc), lambda j: (0, j + off))
    return pl.pallas_call(
        body, grid=(nb,), in_specs=[blk(S, 0), blk(S, nb), blk(FFN_CONV, 0), blk(FFN_CONV, nb), blk(1, 0), blk(1, nb)],
        out_specs=blk(S, 0), out_shape=jax.ShapeDtypeStruct((S, F), BF16),
        compiler_params=_params(("parallel",)), name=name)(u, u, cw, cw, cb, cb)


def _ffn_act_bwd(u, cw, cb, dact, *, name):
    S, F2 = u.shape
    F = F2 // 2
    tc = _tile(F, 256)
    nb = F // tc

    def body(ug_ref, uu_ref, wg_ref, wu_ref, bg_ref, bu_ref, da_ref, du_ref, dw_ref, db_ref):
        ug, uu = ug_ref[...], uu_ref[...]
        cg = _dwconv(ug, wg_ref[...]) + bg_ref[...]
        cu = _dwconv(uu, wu_ref[...]) + bu_ref[...]
        sg = _sigmoid(cg)
        da = da_ref[...].astype(F32)
        dcu = da * (cg * sg)
        dcg = da * cu * (sg * (1.0 + cg * (1.0 - sg)))
        dxg, dwg = _dwconv_bwd(ug, wg_ref[...], dcg)
        dxu, dwu = _dwconv_bwd(uu, wu_ref[...], dcu)
        du_ref[0] = dxg.astype(du_ref.dtype)
        du_ref[1] = dxu.astype(du_ref.dtype)
        dw_ref[0] = dwg
        dw_ref[1] = dwu
        db_ref[0] = jnp.sum(dcg, axis=0, keepdims=True)
        db_ref[1] = jnp.sum(dcu, axis=0, keepdims=True)

    blk = lambda r, off: pl.BlockSpec((r, tc), lambda j: (0, j + off))
    blk3 = lambda r: pl.BlockSpec((2, r, tc), lambda j: (0, 0, j))
    return pl.pallas_call(
        body, grid=(nb,),
        in_specs=[blk(S, 0), blk(S, nb), blk(FFN_CONV, 0), blk(FFN_CONV, nb), blk(1, 0), blk(1, nb), blk(S, 0)],
        out_specs=[blk3(S), blk3(FFN_CONV), blk3(1)],
        out_shape=[jax.ShapeDtypeStruct((2, S, F), BF16), jax.ShapeDtypeStruct((2, FFN_CONV, F), F32),
                   jax.ShapeDtypeStruct((2, 1, F), F32)],
        compiler_params=_params(("parallel",)), name=name)(u, u, cw, cw, cb, cb, dact)


def _xattn_fwd(q, kv, *, name):
    S, XW = q.shape
    M = kv.shape[0]
    nh = XW // HEAD_DIM
    tq = _tile(S, 512)
    scale = HEAD_DIM ** -0.5

    def body(q_ref, k_ref, v_ref, o_ref):
        z = lax.dot_general(q_ref[...], k_ref[...], (((1,), (1,)), ((), ())), preferred_element_type=F32) * scale
        e = jnp.exp(z - jnp.max(z, axis=-1, keepdims=True))
        p = e / jnp.sum(e, axis=-1, keepdims=True)
        o_ref[...] = jnp.dot(p.astype(BF16), v_ref[...], preferred_element_type=F32).astype(o_ref.dtype)

    return pl.pallas_call(
        body, grid=(nh, S // tq),
        in_specs=[pl.BlockSpec((tq, HEAD_DIM), lambda h, i: (i, h)), pl.BlockSpec((M, HEAD_DIM), lambda h, i: (0, h)),
                  pl.BlockSpec((M, HEAD_DIM), lambda h, i: (0, nh + h))],
        out_specs=pl.BlockSpec((tq, HEAD_DIM), lambda h, i: (i, h)), out_shape=jax.ShapeDtypeStruct((S, XW), BF16),
        compiler_params=_params(("parallel", "parallel")), name=name)(q, kv, kv)


def _xattn_bwd(q, kv, do, *, name):
    S, XW = q.shape
    M = kv.shape[0]
    nh = XW // HEAD_DIM
    tq = _tile(S, 512)
    scale = HEAD_DIM ** -0.5
    nt = (((1,), (1,)), ((), ()))
    tn = (((0,), (0,)), ((), ()))

    def body(q_ref, k_ref, v_ref, do_ref, dq_ref, dk_ref, dv_ref):
        qv, kvv, vv = q_ref[...], k_ref[...], v_ref[...]
        dov = do_ref[...].astype(BF16)
        z = lax.dot_general(qv, kvv, nt, preferred_element_type=F32) * scale
        e = jnp.exp(z - jnp.max(z, axis=-1, keepdims=True))
        p = e / jnp.sum(e, axis=-1, keepdims=True)
        dp = lax.dot_general(dov, vv, nt, preferred_element_type=F32)
        ds = (p * (dp - jnp.sum(dp * p, axis=-1, keepdims=True)) * scale).astype(BF16)
        dq_ref[...] = jnp.dot(ds, kvv, preferred_element_type=F32).astype(dq_ref.dtype)
        dk = lax.dot_general(ds, qv, tn, preferred_element_type=F32)
        dv = lax.dot_general(p.astype(BF16), dov, tn, preferred_element_type=F32)

        @pl.when(pl.program_id(1) == 0)
        def _():
            dk_ref[...] = dk
            dv_ref[...] = dv

        @pl.when(pl.program_id(1) > 0)
        def _():
            dk_ref[...] += dk
            dv_ref[...] += dv

    qs = pl.BlockSpec((tq, HEAD_DIM), lambda h, i: (i, h))
    ms = pl.BlockSpec((M, HEAD_DIM), lambda h, i: (0, h))
    return pl.pallas_call(
        body, grid=(nh, S // tq),
        in_specs=[qs, ms, pl.BlockSpec((M, HEAD_DIM), lambda h, i: (0, nh + h)), qs],
        out_specs=[qs, ms, ms],
        out_shape=[jax.ShapeDtypeStruct((S, XW), BF16), jax.ShapeDtypeStruct((M, XW), F32), jax.ShapeDtypeStruct((M, XW), F32)],
        compiler_params=_params(("parallel", "arbitrary")), name=name)(q, kv, kv, do)


_NN = (((1,), (0,)), ((), ()))
_NT = (((1,), (1,)), ((), ()))
_TN = (((0,), (0,)), ((), ()))


def _batched(dn, a):
    if a.ndim == 2:
        return dn
    (ca,), (cb,) = dn[0]
    return (((ca + 1,), (cb + 1,)), ((0,), (0,)))


def _dot(a, b, dn=_NN):
    return lax.dot_general(a.astype(BF16), b.astype(BF16), _batched(dn, a), preferred_element_type=F32)


def _dot_hi(a, b, dn=_NN):
    return lax.dot_general(a, b, _batched(dn, a), preferred_element_type=F32, precision=HI)


def _dot_split(a, b01, dn=_NN):
    hi = a.astype(BF16)
    lo = (a - hi.astype(F32)).astype(BF16)
    return (lax.dot_general(hi, b01, dn, preferred_element_type=F32)
            + lax.dot_general(lo, b01, dn, preferred_element_type=F32))


def _running_sums(x, carry, reverse):
    nb = x.shape[1] // LANES
    tri = _after_matrix(LANES, transpose=not reverse)
    blocks = [x[:, b * LANES:(b + 1) * LANES] for b in range(nb)]
    sums = [jnp.sum(blk, axis=1, keepdims=True) for blk in blocks]
    out = [None] * nb
    run = carry
    for b in (reversed(range(nb)) if reverse else range(nb)):
        out[b] = _dot_split(blocks[b], tri) + run
        run = run + sums[b]
    total = sums[0]
    for b in range(1, nb):
        total = total + sums[b]
    return jnp.concatenate(out, axis=1), total


def _after_matrix(n, transpose=False):
    row = lax.broadcasted_iota(jnp.int32, (n, n), 0)
    col = lax.broadcasted_iota(jnp.int32, (n, n), 1)
    return (row < col if transpose else row > col).astype(BF16)


def _sb_fwd(proj, nh, *, name):
    S = proj.shape[0]
    TQ, TK = min(SB_TQ, S), min(SB_TK, S)
    nq = S // TQ
    scale = HEAD_DIM ** -0.5

    def body(q_ref, k_ref, v_ref, o_ref, tot_ref):
        i = pl.program_id(1)
        q = q_ref[...].astype(BF16)
        qpos = i * TQ + lax.broadcasted_iota(jnp.int32, (TQ, TK), 0)
        kcol = lax.broadcasted_iota(jnp.int32, (TQ, TK), 1)
        nt = ((i + 1) * TQ + TK - 1) // TK
        n_in = (i * TQ) // TK

        def make_step(masked):
            def step(j, carry):
                acc, out = carry
                off = pl.multiple_of(j * TK, TK)
                kb = k_ref[pl.ds(off, TK), :].astype(BF16)
                vb = v_ref[pl.ds(off, TK), :].astype(BF16)
                z = lax.dot_general(q, kb, _NT, preferred_element_type=F32) * scale
                ls = -_softplus(z)
                if masked:
                    valid = kcol + off < qpos
                    ls = jnp.where(valid, ls, 0.0)
                later, rs = _running_sums(ls, acc, reverse=True)
                w = jnp.exp(ls + z + later)
                if masked:
                    w = jnp.where(valid, w, 0.0)
                out = out + jnp.dot(w.astype(BF16), vb, preferred_element_type=F32)
                return acc + rs, out
            return step

        edge, inner = make_step(True), make_step(False)
        carry = (jnp.zeros((TQ, 1), F32), jnp.zeros((TQ, HEAD_DIM), F32))
        carry = lax.fori_loop(0, nt - n_in, lambda t, c: edge(nt - 1 - t, c), carry)
        acc, out = lax.fori_loop(0, n_in, lambda t, c: inner(n_in - 1 - t, c), carry)
        o_ref[...] = out.astype(o_ref.dtype)
        tot_ref[...] = acc

    return pl.pallas_call(
        body, grid=(nh, nq),
        in_specs=[pl.BlockSpec((TQ, HEAD_DIM), lambda h, i: (i, h)),
                  pl.BlockSpec((S, HEAD_DIM), lambda h, i: (0, nh + h)),
                  pl.BlockSpec((S, HEAD_DIM), lambda h, i: (0, 2 * nh + h))],
        out_specs=[pl.BlockSpec((TQ, HEAD_DIM), lambda h, i: (i, h)), pl.BlockSpec((None, TQ, 1), lambda h, i: (h, i, 0))],
        out_shape=[jax.ShapeDtypeStruct((S, nh * HEAD_DIM), BF16), jax.ShapeDtypeStruct((nh, S, 1), F32)],
        compiler_params=_params(("parallel", "parallel")), name=name)(proj, proj, proj)


def _sb_bwd(proj, tot, dmix, nh, *, name):
    S = proj.shape[0]
    TQ, TK = min(SB_TQ, S), min(SB_TK, S)
    nq = S // TQ
    scale = HEAD_DIM ** -0.5

    def body(q_ref, k_ref, v_ref, tot_ref, do_ref, dq_ref, dk_ref, dv_ref, dk_acc, dv_acc):
        i = pl.program_id(1)

        @pl.when(i == 0)
        def _():
            dk_acc[...] = jnp.zeros_like(dk_acc)
            dv_acc[...] = jnp.zeros_like(dv_acc)

        q = q_ref[...].astype(BF16)
        do = do_ref[...].astype(BF16)
        tot = tot_ref[...]
        qpos = i * TQ + lax.broadcasted_iota(jnp.int32, (TQ, TK), 0)
        kcol = lax.broadcasted_iota(jnp.int32, (TQ, TK), 1)
        nt = ((i + 1) * TQ + TK - 1) // TK
        n_in = (i * TQ) // TK

        def make_step(masked):
            def step(j, carry):
                pre, g_sum, dq = carry
                off = pl.multiple_of(j * TK, TK)
                kb = k_ref[pl.ds(off, TK), :].astype(BF16)
                vb = v_ref[pl.ds(off, TK), :].astype(BF16)
                z = lax.dot_general(q, kb, _NT, preferred_element_type=F32) * scale
                ls = -_softplus(z)
                if masked:
                    valid = kcol + off < qpos
                    ls = jnp.where(valid, ls, 0.0)
                lb = ls + z
                rs = jnp.sum(ls, axis=1, keepdims=True)
                later, _ = _running_sums(ls, tot - pre - rs, reverse=True)
                w = jnp.exp(lb + later)
                if masked:
                    w = jnp.where(valid, w, 0.0)
                g = lax.dot_general(do, vb, _NT, preferred_element_type=F32) * w
                dls, gs = _running_sums(g, g_sum, reverse=False)
                sig = jnp.exp(lb)
                dz = g * (1.0 - sig) - dls * sig
                if masked:
                    dz = jnp.where(valid, dz, 0.0)
                dz = (dz * scale).astype(BF16)
                dq = dq + jnp.dot(dz, kb, preferred_element_type=F32)
                dk_acc[pl.ds(off, TK), :] += lax.dot_general(dz, q, _TN, preferred_element_type=F32)
                dv_acc[pl.ds(off, TK), :] += lax.dot_general(w.astype(BF16), do, _TN, preferred_element_type=F32)
                return pre + rs, g_sum + gs, dq
            return step

        zero = jnp.zeros((TQ, 1), F32)
        carry = lax.fori_loop(0, n_in, make_step(False), (zero, zero, jnp.zeros((TQ, HEAD_DIM), F32)))
        _, _, dq = lax.fori_loop(n_in, nt, make_step(True), carry)
        dq_ref[...] = dq.astype(dq_ref.dtype)

        @pl.when(i == nq - 1)
        def _():
            dk_ref[...] = dk_acc[...].astype(dk_ref.dtype)
            dv_ref[...] = dv_acc[...].astype(dv_ref.dtype)

    qs = pl.BlockSpec((TQ, HEAD_DIM), lambda h, i: (i, h))
    full = pl.BlockSpec((S, HEAD_DIM), lambda h, i: (0, h))
    o = jax.ShapeDtypeStruct((S, nh * HEAD_DIM), BF16)
    return pl.pallas_call(
        body, grid=(nh, nq),
        in_specs=[qs, pl.BlockSpec((S, HEAD_DIM), lambda h, i: (0, nh + h)),
                  pl.BlockSpec((S, HEAD_DIM), lambda h, i: (0, 2 * nh + h)),
                  pl.BlockSpec((None, TQ, 1), lambda h, i: (h, i, 0)), qs],
        out_specs=[qs, full, full], out_shape=[o, o, o],
        scratch_shapes=[pltpu.VMEM((S, HEAD_DIM), F32), pltpu.VMEM((S, HEAD_DIM), F32)],
        compiler_params=_params(("parallel", "arbitrary")), name=name)(proj, proj, proj, tot, dmix)


def _gdn_qkv_fwd(proj, conv_w, nh, *, name):
    S = proj.shape[0]
    GW = nh * HEAD_DIM
    scale = HEAD_DIM ** -0.5

    def body(x_ref, w_ref, o_ref):
        sec = pl.program_id(0) // nh
        c = _dwconv(x_ref[...], w_ref[...])
        s = c * _sigmoid(c)
        r = lax.rsqrt(jnp.sum(s * s, axis=1, keepdims=True) + EPS)
        fac = jnp.where(sec == 0, scale, 1.0)
        o_ref[...] = jnp.where(sec == 2, s, s * (r * fac))

    return pl.pallas_call(
        body, grid=(3 * nh,),
        in_specs=[pl.BlockSpec((S, HEAD_DIM), lambda j: (0, 3 * nh + j)), pl.BlockSpec((SHORT_CONV, HEAD_DIM), lambda j: (0, j))],
        out_specs=pl.BlockSpec((None, S, HEAD_DIM), lambda j: (j // nh, 0, j % nh)),
        out_shape=jax.ShapeDtypeStruct((3, S, GW), F32),
        compiler_params=_params(("parallel",)), name=name)(proj, conv_w)


def _gdn_qkv_bwd(proj, conv_w, dqkv, nh, *, name):
    S = proj.shape[0]
    GW = nh * HEAD_DIM
    scale = HEAD_DIM ** -0.5

    def body(x_ref, w_ref, d_ref, dx_ref, dw_ref):
        sec = pl.program_id(0) // nh
        x, w = x_ref[...], w_ref[...]
        c = _dwconv(x, w)
        sg = _sigmoid(c)
        s = c * sg
        r = lax.rsqrt(jnp.sum(s * s, axis=1, keepdims=True) + EPS)
        sh = s * r
        d = d_ref[...]
        fac = jnp.where(sec == 0, scale, 1.0)
        dn = (r * fac) * (d - sh * jnp.sum(d * sh, axis=1, keepdims=True))
        ds = jnp.where(sec == 2, d, dn)
        dx, dw = _dwconv_bwd(x, w, ds * (sg * (1.0 + c * (1.0 - sg))))
        dx_ref[...] = dx.astype(dx_ref.dtype)
        dw_ref[...] = dw

    return pl.pallas_call(
        body, grid=(3 * nh,),
        in_specs=[pl.BlockSpec((S, HEAD_DIM), lambda j: (0, 3 * nh + j)), pl.BlockSpec((SHORT_CONV, HEAD_DIM), lambda j: (0, j)),
                  pl.BlockSpec((None, S, HEAD_DIM), lambda j: (j // nh, 0, j % nh))],
        out_specs=[pl.BlockSpec((S, HEAD_DIM), lambda j: (0, j)), pl.BlockSpec((SHORT_CONV, HEAD_DIM), lambda j: (0, j))],
        out_shape=[jax.ShapeDtypeStruct((S, 3 * GW), BF16), jax.ShapeDtypeStruct((SHORT_CONV, 3 * GW), F32)],
        compiler_params=_params(("parallel",)), name=name)(proj, conv_w, dqkv)


def _gdn_gates_fwd(proj, ab, nh, *, name):
    S = proj.shape[0]
    C = CHUNK

    def body(x_ref, ab_ref, o_ref):
        ri = lax.broadcasted_iota(jnp.int32, (C, C), 0)
        ci = lax.broadcasted_iota(jnp.int32, (C, C), 1)
        ltri = (ri >= ci).astype(F32)
        lane = lax.broadcasted_iota(jnp.int32, (C, LANES), 1)
        a_coef = -jnp.exp(ab_ref[0:1, :])
        dt = ab_ref[1:2, :]

        def chunk(n, _):
            rows = pl.ds(pl.multiple_of(n * C, C), C)
            x = x_ref[rows, :]
            beta = _sigmoid(x)
            g = jnp.where(jnp.logical_and(lane >= nh, lane < 2 * nh), a_coef * _softplus(x + dt), 0.0)
            gc = _dot_hi(ltri, pltpu.roll(g, nh, 1))
            o_ref[rows, :] = jnp.where(lane < nh, beta, g) + gc
            return 0

        lax.fori_loop(0, S // C, chunk, 0)

    return pl.pallas_call(
        body, grid=(1,),
        in_specs=[pl.BlockSpec((S, LANES), lambda i: (0, 7 * nh)), pl.BlockSpec((2, LANES), lambda i: (0, 0))],
        out_specs=pl.BlockSpec((S, LANES), lambda i: (0, 0)), out_shape=jax.ShapeDtypeStruct((S, LANES), F32),
        compiler_params=_params(("arbitrary",)), name=name)(proj, ab)


def _gdn_gates_bwd(proj, ab, dgt, nh, *, name):
    S = proj.shape[0]
    C = CHUNK

    def body(x_ref, ab_ref, d_ref, dx_ref, dab_ref):
        ri = lax.broadcasted_iota(jnp.int32, (C, C), 0)
        ci = lax.broadcasted_iota(jnp.int32, (C, C), 1)
        utri = (ri <= ci).astype(F32)
        lane = lax.broadcasted_iota(jnp.int32, (C, LANES), 1)
        is_b = lane < nh
        is_a = jnp.logical_and(lane >= nh, lane < 2 * nh)
        a_coef = -jnp.exp(ab_ref[0:1, :])
        dt = ab_ref[1:2, :]

        def chunk(n, carry):
            da_log, ddt = carry
            rows = pl.ds(pl.multiple_of(n * C, C), C)
            x = x_ref[rows, :]
            d = d_ref[rows, :]
            beta = _sigmoid(x)
            dg = pltpu.roll(_dot_hi(utri, jnp.where(lane >= 2 * nh, d, 0.0)), LANES - nh, 1)
            dg = jnp.where(is_a, dg, 0.0)
            dxa = dg * a_coef * _sigmoid(x + dt)
            dxb = jnp.where(is_b, d * beta * (1.0 - beta), 0.0)
            dx_ref[rows, :] = (dxa + dxb).astype(dx_ref.dtype)
            da_log = da_log + jnp.sum(dg * a_coef * _softplus(x + dt), axis=0, keepdims=True)
            return da_log, ddt + jnp.sum(dxa, axis=0, keepdims=True)

        zero = jnp.zeros((1, LANES), F32)
        da_log, ddt = lax.fori_loop(0, S // C, chunk, (zero, zero))
        dab_ref[0:1, :] = da_log
        dab_ref[1:2, :] = ddt

    return pl.pallas_call(
        body, grid=(1,),
        in_specs=[pl.BlockSpec((S, LANES), lambda i: (0, 7 * nh)), pl.BlockSpec((2, LANES), lambda i: (0, 0)),
                  pl.BlockSpec((S, LANES), lambda i: (0, 0))],
        out_specs=[pl.BlockSpec((S, LANES), lambda i: (0, 0)), pl.BlockSpec((2, LANES), lambda i: (0, 0))],
        out_shape=[jax.ShapeDtypeStruct((S, LANES), BF16), jax.ShapeDtypeStruct((2, LANES), F32)],
        compiler_params=_params(("arbitrary",)), name=name)(proj, ab, dgt)


def _unit_lower_inverse(lmat):
    C = lmat.shape[-1]
    ri = lax.broadcasted_iota(jnp.int32, lmat.shape, lmat.ndim - 2)
    ci = lax.broadcasted_iota(jnp.int32, lmat.shape, lmat.ndim - 1)
    nmat = -lmat
    p = jnp.where(ri == ci, 1.0, 0.0) + nmat
    for _ in range(int(math.log2(C)) - 1):
        nmat = _dot_hi(nmat, nmat)
        p = p + _dot_hi(p, nmat)
    return p


def _gdn_chunk_common(q, k, v, gates, gc_row, h, nh, tinv=None):
    C = CHUNK
    lane = lax.broadcasted_iota(jnp.int32, gates.shape, gates.ndim - 1)
    beta = jnp.sum(jnp.where(lane == h, gates, 0.0), axis=-1, keepdims=True)
    gc = jnp.sum(jnp.where(lane == 2 * nh + h, gates, 0.0), axis=-1, keepdims=True)
    sq = gates.shape[:-1] + (C,)
    ri = lax.broadcasted_iota(jnp.int32, sq, len(sq) - 2)
    ci = lax.broadcasted_iota(jnp.int32, sq, len(sq) - 1)
    incl, strict = ri >= ci, ri > ci
    decay = jnp.where(incl, jnp.exp(jnp.where(incl, gc - gc_row, 0.0)), 0.0)
    egc = jnp.exp(gc)
    kb, vb = k * beta, v * beta
    lmat = jnp.where(strict, _dot(kb, k, _NT) * decay, 0.0)
    kbg = kb * egc
    u = w = None
    if tinv is None:
        tinv = _unit_lower_inverse(lmat)
        u = _dot(tinv, vb)
        w = _dot(tinv, kbg)
    amat = _dot(q, k, _NT) * decay
    glast = gc[..., C - 1:C, :]
    ekt = jnp.exp(glast - gc)
    return dict(q=q, k=k, v=v, beta=beta, decay=decay, egc=egc, kb=kb, vb=vb, lmat=lmat, tinv=tinv, kbg=kbg, u=u, w=w,
                amat=amat, qd=q * egc, ekt=ekt, kt=k * ekt, cd=jnp.exp(glast), strict=strict, incl=incl)


def _gdn_chunk_specs(nh, S, nc):
    return [pl.BlockSpec((3, S, HEAD_DIM), lambda h: (0, 0, h)),
            pl.BlockSpec((S, LANES), lambda h: (0, 0)),
            pl.BlockSpec((None, nc, 1, CHUNK), lambda h: (h, 0, 0, 0))]


def _gdn_state_free(qkv_ref, gates_ref, gr_ref, g, nb, h, nh):
    C = CHUNK
    rows = pl.ds(pl.multiple_of(g * (nb * C), nb * C), nb * C)
    part = lambda x: x.reshape(nb, C, x.shape[-1])
    return rows, _gdn_chunk_common(part(qkv_ref[0, rows, :]), part(qkv_ref[1, rows, :]), part(qkv_ref[2, rows, :]),
                                   part(gates_ref[rows, :]), gr_ref[pl.ds(g * nb, nb)], h, nh)


def _gdn_chunk_fwd(qkv, gates, gc_row, *, name):
    _, S, GW = qkv.shape
    nh, C = GW // HEAD_DIM, CHUNK
    nc = S // C
    nb = min(GDN_CPB, nc)
    flat = lambda x: x.reshape(nb * C, x.shape[-1])

    def body(qkv_ref, gates_ref, gr_ref, o_ref, st_ref, u_s, w_s, a_s, qd_s, kt_s, cd_s):
        h = pl.program_id(0)

        def group(g, _):
            rows, m = _gdn_state_free(qkv_ref, gates_ref, gr_ref, g, nb, h, nh)
            u_s[rows, :] = flat(m["u"])
            w_s[rows, :] = flat(m["w"])
            a_s[rows, :] = flat(m["amat"])
            qd_s[rows, :] = flat(m["qd"])
            kt_s[rows, :] = flat(m["kt"])
            cd_s[pl.ds(g * nb, nb)] = jnp.broadcast_to(m["cd"], (nb, 8, LANES))
            return 0

        lax.fori_loop(0, nc // nb, group, 0)

        def chunk(n, s0):
            rows = pl.ds(pl.multiple_of(n * C, C), C)
            st_ref[n] = s0
            v_new = u_s[rows, :] - _dot(w_s[rows, :], s0)
            o_ref[rows, :] = _dot(qd_s[rows, :], s0) + _dot(a_s[rows, :], v_new)
            return s0 * cd_s[n][0:1, :] + _dot(kt_s[rows, :], v_new, _TN)

        lax.fori_loop(0, nc, chunk, jnp.zeros((HEAD_DIM, HEAD_DIM), F32))

    seq = pltpu.VMEM((S, HEAD_DIM), F32)
    return pl.pallas_call(
        body, grid=(nh,), in_specs=_gdn_chunk_specs(nh, S, nc),
        out_specs=[pl.BlockSpec((S, HEAD_DIM), lambda h: (0, h)),
                   pl.BlockSpec((None, nc, HEAD_DIM, HEAD_DIM), lambda h: (h, 0, 0, 0))],
        out_shape=[jax.ShapeDtypeStruct((S, GW), F32), jax.ShapeDtypeStruct((nh, nc, HEAD_DIM, HEAD_DIM), F32)],
        scratch_shapes=[seq, seq, pltpu.VMEM((S, C), F32), seq, seq, pltpu.VMEM((nc, 8, LANES), F32)],
        compiler_params=_params(("parallel",)), name=name)(qkv, gates, gc_row)


def _gdn_chunk_bwd(qkv, gates, gc_row, states, do, *, name):
    _, S, GW = qkv.shape
    nh, C = GW // HEAD_DIM, CHUNK
    nc = S // C
    nb = min(GDN_CPB, nc)
    flat = lambda x: x.reshape(nb * C, x.shape[-1])
    part = lambda x: x.reshape(nb, C, x.shape[-1])

    def body(qkv_ref, gates_ref, gr_ref, st_ref, do_ref, dqkv_ref, dgt_ref,
             t_s, vn_s, w_s, a_s, qd_s, kt_s, cd_s, dvn_s, dkt_s, dcd_s):
        h = pl.program_id(0)

        def group(g, _):
            rows, m = _gdn_state_free(qkv_ref, gates_ref, gr_ref, g, nb, h, nh)
            t_s[rows, :] = flat(m["tinv"])
            vn_s[rows, :] = flat(m["u"])
            w_s[rows, :] = flat(m["w"])
            a_s[rows, :] = flat(m["amat"])
            qd_s[rows, :] = flat(m["qd"])
            kt_s[rows, :] = flat(m["kt"])
            cd_s[pl.ds(g * nb, nb)] = jnp.broadcast_to(m["cd"], (nb, 8, LANES))
            return 0

        lax.fori_loop(0, nc // nb, group, 0)

        def chunk(t, dsn):
            n = nc - 1 - t
            rows = pl.ds(pl.multiple_of(n * C, C), C)
            s0, dout, w = st_ref[n], do_ref[rows, :], w_s[rows, :]
            v_new = vn_s[rows, :] - _dot(w, s0)
            dvn = _dot(a_s[rows, :], dout, _TN) + _dot(kt_s[rows, :], dsn)
            vn_s[rows, :] = v_new
            dvn_s[rows, :] = dvn
            dkt_s[rows, :] = _dot(v_new, dsn, _NT)
            dcd_s[n] = jnp.zeros((8, LANES), F32) + jnp.sum(dsn * s0)
            return _dot(qd_s[rows, :], dout, _TN) + dsn * cd_s[n][0:1, :] - _dot(w, dvn, _TN)

        lax.fori_loop(0, nc, chunk, jnp.zeros((HEAD_DIM, HEAD_DIM), F32))

        def rest(g, _):
            rows, m = _gdn_state_free_again(qkv_ref, gates_ref, gr_ref, t_s, g, nb, h, nh)
            chunks = pl.ds(g * nb, nb)
            dq, dk, dv, dgt = _gdn_chunk_grad(m, st_ref[chunks], part(vn_s[rows, :]), part(dvn_s[rows, :]),
                                              part(dkt_s[rows, :]), dcd_s[chunks][:, 0:1, 0:1], part(do_ref[rows, :]), h, nh)
            dqkv_ref[0, rows, :] = flat(dq)
            dqkv_ref[1, rows, :] = flat(dk)
            dqkv_ref[2, rows, :] = flat(dv)
            dgt_ref[rows, :] = flat(dgt)
            return 0

        lax.fori_loop(0, nc // nb, rest, 0)

    seq = pltpu.VMEM((S, HEAD_DIM), F32)
    small = pltpu.VMEM((nc, 8, LANES), F32)
    return pl.pallas_call(
        body, grid=(nh,),
        in_specs=_gdn_chunk_specs(nh, S, nc) + [
            pl.BlockSpec((None, nc, HEAD_DIM, HEAD_DIM), lambda h: (h, 0, 0, 0)),
            pl.BlockSpec((S, HEAD_DIM), lambda h: (0, h))],
        out_specs=[pl.BlockSpec((3, S, HEAD_DIM), lambda h: (0, 0, h)), pl.BlockSpec((None, S, LANES), lambda h: (h, 0, 0))],
        out_shape=[jax.ShapeDtypeStruct((3, S, GW), F32), jax.ShapeDtypeStruct((nh, S, LANES), F32)],
        scratch_shapes=[pltpu.VMEM((S, C), F32), seq, seq, pltpu.VMEM((S, C), F32), seq, seq, small, seq, seq, small],
        compiler_params=_params(("parallel",)), name=name)(qkv, gates, gc_row, states, do)


def _gdn_state_free_again(qkv_ref, gates_ref, gr_ref, t_s, g, nb, h, nh):
    C = CHUNK
    rows = pl.ds(pl.multiple_of(g * (nb * C), nb * C), nb * C)
    part = lambda x: x.reshape(nb, C, x.shape[-1])
    m = _gdn_chunk_common(part(qkv_ref[0, rows, :]), part(qkv_ref[1, rows, :]), part(qkv_ref[2, rows, :]),
                          part(gates_ref[rows, :]), gr_ref[pl.ds(g * nb, nb)], h, nh, tinv=part(t_s[rows, :]))
    return rows, m


def _gdn_chunk_grad(m, s0, v_new, dvn, dkt, dcd, dout, h, nh):
    C = CHUNK
    q, k, v, beta, decay, egc = m["q"], m["k"], m["v"], m["beta"], m["decay"], m["egc"]
    tinv, kt, cd = m["tinv"], m["kt"], m["cd"]
    dqd = _dot(dout, s0, _NT)
    damat = jnp.where(m["incl"], _dot(dout, v_new, _NT), 0.0)
    dw = -_dot(dvn, s0, _NT)
    dvb = _dot(tinv, dvn, _TN)
    dkbg = _dot(tinv, dw, _TN)
    dtinv = _dot(dvn, m["vb"], _NT) + _dot(dw, m["kbg"], _NT)
    dl = jnp.where(m["strict"], -_dot_hi(_dot_hi(tinv, dtinv, _TN), tinv, _NT), 0.0)
    dkk = dl * decay
    dqk = damat * decay
    dkb = _dot(dkk, k) + dkbg * egc
    dk = _dot(dkk, m["kb"], _TN) + _dot(dqk, q, _TN) + dkt * m["ekt"] + dkb * beta
    dq = _dot(dqk, k) + dqd * egc
    mm = dl * m["lmat"] + damat * m["amat"]
    ones = jnp.ones(q.shape, F32)
    rk = jnp.sum(dkt * kt, axis=-1, keepdims=True)
    dgc = (_dot_hi(mm, ones) - _dot_hi(mm, ones, _TN) + jnp.sum(dqd * m["qd"], axis=-1, keepdims=True) - rk
           + jnp.sum(dkbg * m["kbg"], axis=-1, keepdims=True))
    dglast = jnp.sum(rk, axis=-2, keepdims=True) + dcd * cd
    rowi = lax.broadcasted_iota(jnp.int32, q.shape, q.ndim - 2)
    lane = lax.broadcasted_iota(jnp.int32, q.shape, q.ndim - 1)
    dgc = dgc + jnp.where(rowi == C - 1, dglast, 0.0)
    dbeta = jnp.sum(dkb * k, axis=-1, keepdims=True) + jnp.sum(dvb * v, axis=-1, keepdims=True)
    dgt = jnp.where(lane == h, dbeta, 0.0) + jnp.where(lane == 2 * nh + h, dgc, 0.0)
    return dq, dk, dvb * beta, dgt


def _gdn_post_fwd(o, proj, ng, nh, *, name):
    S, GW = o.shape

    def body(o_ref, z_ref, g_ref, y_ref):
        ov, z = o_ref[...], z_ref[...]
        rstd = lax.rsqrt(jnp.mean(ov * ov, axis=-1, keepdims=True) + EPS)
        y_ref[...] = (ov * rstd * g_ref[...] * (z * _sigmoid(z))).astype(y_ref.dtype)

    blk = pl.BlockSpec((S, HEAD_DIM), lambda h: (0, h))
    return pl.pallas_call(
        body, grid=(nh,), in_specs=[blk, pl.BlockSpec((S, HEAD_DIM), lambda h: (0, 6 * nh + h)), pl.BlockSpec((1, HEAD_DIM), lambda h: (0, 0))],
        out_specs=blk, out_shape=jax.ShapeDtypeStruct((S, GW), BF16),
        compiler_params=_params(("parallel",)), name=name)(o, proj, ng)


def _gdn_post_bwd(o, proj, ng, dmix, nh, *, name):
    S, GW = o.shape

    def body(o_ref, z_ref, g_ref, d_ref, do_ref, dz_ref, dg_ref):
        ov, z, d = o_ref[...], z_ref[...], d_ref[...].astype(F32)
        rstd = lax.rsqrt(jnp.mean(ov * ov, axis=-1, keepdims=True) + EPS)
        oh = ov * rstd
        sz = _sigmoid(z)
        dy = d * (z * sz)
        dz_ref[...] = (d * (oh * g_ref[...]) * (sz * (1.0 + z * (1.0 - sz)))).astype(dz_ref.dtype)
        t = dy * g_ref[...]
        do_ref[...] = rstd * (t - oh * jnp.mean(t * oh, axis=-1, keepdims=True))
        part = jnp.sum(dy * oh, axis=0, keepdims=True)

        @pl.when(pl.program_id(0) == 0)
        def _():
            dg_ref[...] = part

        @pl.when(pl.program_id(0) > 0)
        def _():
            dg_ref[...] += part

    blk = pl.BlockSpec((S, HEAD_DIM), lambda h: (0, h))
    vec = pl.BlockSpec((1, HEAD_DIM), lambda h: (0, 0))
    return pl.pallas_call(
        body, grid=(nh,),
        in_specs=[blk, pl.BlockSpec((S, HEAD_DIM), lambda h: (0, 6 * nh + h)), vec, pl.BlockSpec((S, HEAD_DIM), lambda h: (0, nh + h))],
        out_specs=[blk, blk, vec],
        out_shape=[jax.ShapeDtypeStruct((S, GW), F32), jax.ShapeDtypeStruct((S, GW), BF16), jax.ShapeDtypeStruct((1, HEAD_DIM), F32)],
        compiler_params=_params(("arbitrary",)), name=name)(o, proj, ng, dmix)


def _gdn_forward(proj, conv_w, ab, ng, nh, tag):
    S = proj.shape[0]
    nc = S // CHUNK
    qkv = _gdn_qkv_fwd(proj, conv_w, nh, name=f"gdn_qkv_fwd{tag}")
    gates = _gdn_gates_fwd(proj, ab, nh, name=f"gdn_gates_fwd{tag}")
    gc_row = gates[:, 2 * nh:3 * nh].T.reshape(nh, nc, 1, CHUNK)
    o, states = _gdn_chunk_fwd(qkv, gates, gc_row, name=f"gdn_chunk_fwd{tag}")
    y = _gdn_post_fwd(o, proj, ng, nh, name=f"gdn_post_fwd{tag}")
    return y, (qkv, gates, gc_row, states, o)


def _gdn_backward(proj, conv_w, ab, ng, saved, dmix, nh, tag):
    qkv, gates, gc_row, states, o = saved
    do, dz, dng = _gdn_post_bwd(o, proj, ng, dmix, nh, name=f"gdn_post_bwd{tag}")
    dqkv, dgt_heads = _gdn_chunk_bwd(qkv, gates, gc_row, states, do, name=f"gdn_chunk_bwd{tag}")
    dx_qkv, dconv = _gdn_qkv_bwd(proj, conv_w, dqkv, nh, name=f"gdn_qkv_bwd{tag}")
    dx_g, dab = _gdn_gates_bwd(proj, ab, jnp.sum(dgt_heads, axis=0), nh, name=f"gdn_gates_bwd{tag}")
    return dx_qkv, dz, dx_g, dconv, dab, dng


def _row_tile(r, cap=128):
    t = cap
    while r % t:
        t //= 2
    assert t >= 8, r
    return t


def _adamw_update(gv, w_ref, m_ref, v_ref, d_ref, m2_ref, v2_ref):
    m2 = ADAM_B1 * m_ref[...] + (1.0 - ADAM_B1) * gv
    v2 = ADAM_B2 * v_ref[...] + (1.0 - ADAM_B2) * (gv * gv)
    m_hat = m2 / (1.0 - ADAM_B1 ** ADAM_STEP)
    v_hat = v2 / (1.0 - ADAM_B2 ** ADAM_STEP)
    d_ref[...] = -ADAM_LR * (m_hat / (jnp.sqrt(v_hat) + ADAM_EPS) + ADAM_WD * w_ref[...])
    m2_ref[...] = m2
    v2_ref[...] = v2


def _adamw(w, g, m, v, *, name):
    L, r, c = w.shape
    tr = _row_tile(r)

    def body(w_ref, g_ref, m_ref, v_ref, d_ref, m2_ref, v2_ref):
        _adamw_update(g_ref[...], w_ref, m_ref, v_ref, d_ref, m2_ref, v2_ref)

    blk = pl.BlockSpec((None, tr, c), lambda l, i: (l, i, 0))
    o = jax.ShapeDtypeStruct(w.shape, F32)
    return pl.pallas_call(
        body, grid=(L, r // tr), in_specs=[blk] * 4, out_specs=[blk] * 3, out_shape=[o, o, o],
        compiler_params=_params(("parallel", "parallel")), name=name)(w, g, m, v)


def _adamw_halves(w, g_own, g_sib, cvec, m, v, *, name):
    L, r, c = w.shape
    tr = _row_tile(r // 2)
    nbh = (r // 2) // tr

    def body(c_ref, w_ref, go_ref, gs_ref, m_ref, v_ref, g_out, d_ref, m2_ref, v2_ref):
        gv = jnp.where(pl.program_id(1) // nbh == c_ref[0], go_ref[...], gs_ref[...])
        g_out[...] = gv
        _adamw_update(gv, w_ref, m_ref, v_ref, d_ref, m2_ref, v2_ref)

    lo = lambda i: jnp.minimum(i, nbh - 1)
    hi = lambda i: jnp.maximum(i - nbh, 0)
    blk = pl.BlockSpec((None, tr, c), lambda l, i, c_ref: (l, i, 0))
    own = pl.BlockSpec((None, tr, c), lambda l, i, c_ref: (l, jnp.where(c_ref[0] == 0, lo(i), hi(i)), 0))
    sib = pl.BlockSpec((None, tr, c), lambda l, i, c_ref: (l, jnp.where(c_ref[0] == 0, hi(i), lo(i)), 0))
    o = jax.ShapeDtypeStruct(w.shape, F32)
    return pl.pallas_call(
        body,
        grid_spec=pltpu.PrefetchScalarGridSpec(
            num_scalar_prefetch=1, grid=(L, r // tr), in_specs=[blk, own, sib, blk, blk], out_specs=[blk] * 4),
        out_shape=[o, o, o, o],
        compiler_params=_params(("parallel", "arbitrary")), name=name)(cvec, w, g_own, g_sib, m, v)


def _sum_partials(p, rb, kc, prev, l, nl, *, name):
    _, _, h, c = rb.shape
    tr = _row_tile(h, 256)
    nb = h // tr
    others = [(0, 1), (1, 0), (1, 1), (2, 0), (2, 1), (3, 0), (3, 1)]

    def body(kc_ref, p_ref, *rest):
        acc = p_ref[...].astype(F32)
        for r_ref in rest[:7]:
            acc = acc + r_ref[...].astype(F32)
        rest[-1][...] = acc

    def slot(ds, dc):
        return pl.BlockSpec((None, None, tr, c), lambda i, kc_ref: ((kc_ref[0] + ds) % 4, (kc_ref[1] + dc) % 2, i, 0))

    in_specs = [pl.BlockSpec((None, tr, c), lambda i, kc_ref: (kc_ref[0], kc_ref[1] * nb + i, 0))]
    in_specs += [slot(ds, dc) for ds, dc in others]
    args = [kc, p] + [rb] * 7
    if prev is not None:
        in_specs.append(pl.BlockSpec(memory_space=pltpu.HBM))
        args.append(prev)
    return pl.pallas_call(
        body,
        grid_spec=pltpu.PrefetchScalarGridSpec(
            num_scalar_prefetch=1, grid=(nb,), in_specs=in_specs,
            out_specs=pl.BlockSpec((None, tr, c), lambda i, kc_ref: (l, i, 0))),
        out_shape=jax.ShapeDtypeStruct((nl, h, c), F32), input_output_aliases={9: 0} if prev is not None else {},
        compiler_params=_params(("parallel",)), name=name)(*args)


_MESH = pl.DeviceIdType.MESH
_HBM = pl.BlockSpec(memory_space=pltpu.HBM)


def _place():
    x, y, c = lax.axis_index("x"), lax.axis_index("y"), lax.axis_index("c")
    return x, y, c, [(1 - x, y), (x, 1 - y), (1 - x, 1 - y)]


def _cast_place(w, l, kvec, *, name):
    _, r, c = w.shape
    tr = _row_tile(r, 256)

    def body(k_ref, w_ref, o_ref):
        o_ref[...] = w_ref[...].astype(o_ref.dtype)

    return pl.pallas_call(
        body,
        grid_spec=pltpu.PrefetchScalarGridSpec(
            num_scalar_prefetch=1, grid=(r // tr,),
            in_specs=[pl.BlockSpec((None, tr, c), lambda i, k_ref: (l, i, 0))],
            out_specs=pl.BlockSpec((None, tr, c), lambda i, k_ref: (k_ref[0], i, 0))),
        out_shape=jax.ShapeDtypeStruct((4, r, c), BF16),
        compiler_params=_params(("parallel",)), name=name)(kvec, w)


_SEM = pl.BlockSpec(memory_space=pltpu.SEMAPHORE)
_ANY = pl.BlockSpec(memory_space=pl.ANY)
_EFFECT = pltpu.SideEffectType.DATAFLOW_SIDE_EFFECTING


def _in_hbm(a):
    return pltpu.with_memory_space_constraint(a, pltpu.HBM)


def _gather_copies(w_refs, send, recv, landing):
    x, y, c, chips = _place()
    k = 2 * x + y
    cps = []
    for a, w in enumerate(w_refs):
        h = w.shape[1] // 2
        for j, (cx, cy) in enumerate(chips):
            cps.append(pltpu.make_async_remote_copy(
                src_ref=w.at[k, pl.ds(c * h, h), :], dst_ref=w.at[2 * cx + cy if landing else k, pl.ds(c * h, h), :],
                send_sem=send.at[3 * a + j], recv_sem=recv.at[3 * a + j], device_id=(cx, cy, c), device_id_type=_MESH))
    return cps


def _gather_start(ws, after, *, name):
    n = len(ws)

    def body(*refs):
        send, recv = refs[n + 1], refs[n + 2]
        o_refs, token = refs[n + 3:2 * n + 3], refs[2 * n + 3]
        for cp in _gather_copies(o_refs, send, recv, landing=False):
            cp.start()
        token[...] = jnp.zeros_like(token)

    out = pl.pallas_call(
        body, in_specs=[_HBM] * n + [_ANY], out_specs=[_SEM, _SEM] + [_HBM] * n + [pl.BlockSpec(memory_space=pltpu.VMEM)],
        out_shape=[pltpu.SemaphoreType.DMA((3 * n,)), pltpu.SemaphoreType.DMA((3 * n,))]
        + [pltpu.HBM(w.shape, w.dtype) for w in ws] + [jax.ShapeDtypeStruct((8, LANES), F32)],
        input_output_aliases={a: 2 + a for a in range(n)},
        compiler_params=pltpu.CompilerParams(has_side_effects=_EFFECT), name=name)(*[_in_hbm(w) for w in ws], after)
    return out[0], out[1], list(out[2:2 + n]), out[2 + n]


def _gather_wait(send, recv, ws, after, *, name):
    n = len(ws)

    def body(*refs):
        for cp in _gather_copies(refs[:n], refs[n], refs[n + 1], landing=True):
            cp.wait_send()
            cp.wait_recv()

    return list(pl.pallas_call(
        body, in_specs=[_HBM] * n + [_SEM, _SEM, _ANY], out_specs=[_HBM] * n,
        out_shape=[pltpu.HBM(w.shape, w.dtype) for w in ws], input_output_aliases={a: a for a in range(n)},
        compiler_params=pltpu.CompilerParams(has_side_effects=_EFFECT), name=name)(*ws, send, recv, after))


def _gather_to_sibling(ws, *, name):
    n = len(ws)

    def body(*refs):
        o_refs = refs[n:2 * n]
        send, recv = refs[2 * n:]
        x, y, c, chips = _place()
        cps = []
        for a in range(n):
            h = o_refs[a].shape[1] // 2
            for j, (cx, cy) in enumerate(chips):
                landed = o_refs[a].at[2 * cx + cy, pl.ds(c * h, h), :]
                cp = pltpu.make_async_remote_copy(
                    src_ref=landed, dst_ref=landed, send_sem=send.at[3 * a + j], recv_sem=recv.at[3 * a + j],
                    device_id=(x, y, 1 - c), device_id_type=_MESH)
                cp.start()
                cps.append(cp)
        for a in range(n):
            h = o_refs[a].shape[1] // 2
            for j, (cx, cy) in enumerate(chips):
                other = o_refs[a].at[2 * cx + cy, pl.ds((1 - c) * h, h), :]
                pltpu.make_async_remote_copy(
                    src_ref=other, dst_ref=other, send_sem=send.at[3 * a + j], recv_sem=recv.at[3 * a + j],
                    device_id=(x, y, c), device_id_type=_MESH).wait_recv()
        for cp in cps:
            cp.wait_send()

    return list(pl.pallas_call(
        body, in_specs=[_HBM] * n, out_specs=[_HBM] * n,
        out_shape=[jax.ShapeDtypeStruct(w.shape, w.dtype) for w in ws],
        input_output_aliases={a: a for a in range(n)},
        scratch_shapes=[pltpu.SemaphoreType.DMA((3 * n,))] * 2, name=name)(*ws))


def _reduce_copies(p_refs, r_refs, send, recv, landing):
    x, y, c = lax.axis_index("x"), lax.axis_index("y"), lax.axis_index("c")
    k = 2 * x + y
    cps = []
    for a, (p, r) in enumerate(zip(p_refs, r_refs)):
        h = p.shape[1] // 2
        for d in range(1, 8):
            px = 1 - x if d & 4 else x
            py = 1 - y if d & 2 else y
            pc = 1 - c if d & 1 else c
            cps.append(pltpu.make_async_remote_copy(
                src_ref=p.at[2 * px + py, pl.ds(pc * h, h), :], dst_ref=r.at[2 * px + py, pc] if landing else r.at[k, c],
                send_sem=send.at[7 * a + d - 1], recv_sem=recv.at[7 * a + d - 1], device_id=(px, py, pc),
                device_id_type=_MESH))
    return cps


def _reduce_start(ps, after, *, name):
    n = len(ps)
    lands = [lax.empty((4, 2, p.shape[1] // 2, p.shape[2]), p.dtype) for p in ps]

    def body(*refs):
        send, recv = refs[2 * n + 1], refs[2 * n + 2]
        p_out, r_out, token = refs[2 * n + 3:3 * n + 3], refs[3 * n + 3:4 * n + 3], refs[4 * n + 3]
        for cp in _reduce_copies(p_out, r_out, send, recv, landing=False):
            cp.start()
        token[...] = jnp.zeros_like(token)

    out = pl.pallas_call(
        body, in_specs=[_HBM] * (2 * n) + [_ANY],
        out_specs=[_SEM, _SEM] + [_HBM] * (2 * n) + [pl.BlockSpec(memory_space=pltpu.VMEM)],
        out_shape=[pltpu.SemaphoreType.DMA((7 * n,)), pltpu.SemaphoreType.DMA((7 * n,))]
        + [pltpu.HBM(a.shape, a.dtype) for a in list(ps) + lands] + [jax.ShapeDtypeStruct((8, LANES), F32)],
        input_output_aliases={a: 2 + a for a in range(2 * n)},
        compiler_params=pltpu.CompilerParams(has_side_effects=_EFFECT), name=name)(
            *[_in_hbm(p) for p in ps], *[_in_hbm(r) for r in lands], after)
    return out[0], out[1], list(out[2:2 + n]), list(out[2 + n:2 + 2 * n]), out[2 + 2 * n]


def _reduce_wait(send, recv, ps, lands, after, *, name):
    n = len(ps)

    def body(*refs):
        for cp in _reduce_copies(refs[:n], refs[n:2 * n], refs[2 * n], refs[2 * n + 1], landing=True):
            cp.wait_send()
            cp.wait_recv()

    out = pl.pallas_call(
        body, in_specs=[_HBM] * (2 * n) + [_SEM, _SEM, _ANY], out_specs=[_HBM] * (2 * n),
        out_shape=[pltpu.HBM(a.shape, a.dtype) for a in list(ps) + list(lands)],
        input_output_aliases={a: a for a in range(2 * n)},
        compiler_params=pltpu.CompilerParams(has_side_effects=_EFFECT), name=name)(*ps, *lands, send, recv, after)
    return list(out[:n]), list(out[n:])


def _swap_with_sibling(gs, *, name):
    n = len(gs)

    def body(*refs):
        g_refs, o_refs = refs[:n], refs[n:2 * n]
        send, recv = refs[2 * n:]
        x, y, c, _ = _place()
        cps = []
        for a in range(n):
            cp = pltpu.make_async_remote_copy(
                src_ref=g_refs[a], dst_ref=o_refs[a], send_sem=send.at[a], recv_sem=recv.at[a],
                device_id=(x, y, 1 - c), device_id_type=_MESH)
            cp.start()
            cps.append(cp)
        for cp in cps:
            cp.wait_recv()
        for cp in cps:
            cp.wait_send()

    return pl.pallas_call(
        body, in_specs=[_HBM] * n, out_specs=[_HBM] * n,
        out_shape=[jax.ShapeDtypeStruct(g.shape, g.dtype) for g in gs],
        scratch_shapes=[pltpu.SemaphoreType.DMA((n,))] * 2, name=name)(*gs)


def _allreduce_small(v, *, name):
    R = v.shape[0]

    def body(v_ref, o_ref, buf, send, recv, loc):
        x, y, c = lax.axis_index("x"), lax.axis_index("y"), lax.axis_index("c")
        me = 4 * x + 2 * y + c
        mine = pltpu.make_async_copy(v_ref, buf.at[me], loc)
        mine.start()
        cps = []
        for d in range(1, 8):
            px = 1 - x if d & 4 else x
            py = 1 - y if d & 2 else y
            pc = 1 - c if d & 1 else c
            cp = pltpu.make_async_remote_copy(
                src_ref=v_ref, dst_ref=buf.at[me], send_sem=send.at[d - 1], recv_sem=recv.at[d - 1],
                device_id=(px, py, pc), device_id_type=_MESH)
            cp.start()
            cps.append((cp, 4 * px + 2 * py + pc))
        for d in range(1, 8):
            cp, peer = cps[d - 1]
            pltpu.make_async_remote_copy(
                src_ref=buf.at[peer], dst_ref=buf.at[peer], send_sem=send.at[d - 1], recv_sem=recv.at[d - 1],
                device_id=(x, y, c), device_id_type=_MESH).wait_recv()
        for cp, _ in cps:
            cp.wait_send()
        mine.wait()
        acc = buf[0]
        for i in range(1, 8):
            acc = acc + buf[i]
        o_ref[...] = acc

    return pl.pallas_call(
        body, in_specs=[pl.BlockSpec(memory_space=pltpu.VMEM)], out_specs=pl.BlockSpec(memory_space=pltpu.VMEM),
        out_shape=jax.ShapeDtypeStruct((R, LANES), F32),
        scratch_shapes=[pltpu.VMEM((8, R, LANES), F32), pltpu.SemaphoreType.DMA((7,)), pltpu.SemaphoreType.DMA((7,)),
                        pltpu.SemaphoreType.DMA],
        compiler_params=pltpu.CompilerParams(vmem_limit_bytes=VMEM_LIMIT), name=name)(v)


def _pack(arrs, row_multiple=8):
    rows = []
    for a in arrs:
        flat = a.reshape(-1)
        flat = jnp.pad(flat, (0, (-flat.shape[0]) % LANES))
        rows.append(flat.reshape(-1, LANES))
    buf = jnp.concatenate(rows, axis=0)
    return jnp.pad(buf, ((0, (-buf.shape[0]) % row_multiple), (0, 0)))


def _unpack(buf, shapes):
    out, r = [], 0
    for s in shapes:
        size = math.prod(s)
        nr = -(-size // LANES)
        out.append(buf[r:r + nr].reshape(-1)[:size].reshape(s))
        r += nr
    return out


def kernel(x, mem, mix_norm, w_in, gdn_conv, gdn_a_log, gdn_dt_bias, gdn_norm, w_out, xattn_norm, mem_norm, w_xq, w_xkv, w_xo, ffn_norm, w_up, ffn_conv, ffn_conv_bias, w_down, final_norm, loss_target, m_mix_norm, m_w_in, m_gdn_conv, m_gdn_a_log, m_gdn_dt_bias, m_gdn_norm, m_w_out, m_xattn_norm, m_mem_norm, m_w_xq, m_w_xkv, m_w_xo, m_ffn_norm, m_w_up, m_ffn_conv, m_ffn_conv_bias, m_w_down, m_final_norm, v_mix_norm, v_w_in, v_gdn_conv, v_gdn_a_log, v_gdn_dt_bias, v_gdn_norm, v_w_out, v_xattn_norm, v_mem_norm, v_w_xq, v_w_xkv, v_w_xo, v_ffn_norm, v_w_up, v_ffn_conv, v_ffn_conv_bias, v_w_down, v_final_norm):
    L = w_in.shape[0]
    _, S, D = x.shape
    nh = D // (2 * HEAD_DIM)
    GW = nh * HEAD_DIM
    n_in = 7 * GW + 2 * nh
    NP = 7 * GW + LANES
    XW = X_HEADS * HEAD_DIM
    F = w_down.shape[1] * 4
    cs_in = w_in.shape[2]
    cs_up = w_up.shape[2]
    cs_xo = w_xo.shape[2]
    tu = _tile(cs_up, 1408)
    per = cs_up // tu
    fper = F // tu
    assert n_in == 4 * cs_in and F % tu == 0 and 2 * F == 4 * cs_up

    xi, yi, ci = lax.axis_index("x"), lax.axis_index("y"), lax.axis_index("c")
    chip = 2 * xi + yi
    cvec = jnp.reshape(ci, (1,)).astype(jnp.int32)

    cs_gc, cs_fc = gdn_conv.shape[2], ffn_conv.shape[2]
    keep = jnp.where(ci == 0, 1.0, 0.0).astype(F32)
    gc_full = lax.dynamic_update_slice(jnp.zeros((L, SHORT_CONV, 4 * cs_gc), F32), gdn_conv * keep, (0, 0, chip * cs_gc))
    fc_full = lax.dynamic_update_slice(jnp.zeros((L, FFN_CONV, 4 * cs_fc), F32), ffn_conv * keep, (0, 0, chip * cs_fc))
    conv_all = _allreduce_small(_pack([gc_full, fc_full]), name="allgather_conv")
    gdn_conv_full, ffn_conv_full = _unpack(conv_all, [gc_full.shape, fc_full.shape])

    big = [w_in, w_out, w_xq, w_xkv, w_xo, w_up, w_down]
    kvec = jnp.reshape(chip, (1,)).astype(jnp.int32)
    kcvec = jnp.stack([chip, ci]).astype(jnp.int32)
    ab = jnp.zeros((L, 2, LANES), F32).at[:, 0, nh:2 * nh].set(gdn_a_log).at[:, 1, nh:2 * nh].set(gdn_dt_bias)

    def vec(p, l):
        return p[l:l + 1]

    xo_fwd_b = pl.BlockSpec((None, XW, cs_xo), lambda i, j, k: (j, 0, 0))
    xo_dg_b = pl.BlockSpec((None, XW, cs_xo), lambda i, j, k: (k, 0, 0))
    xo_wg_o = pl.BlockSpec((None, XW, cs_xo), lambda i, j, k: (j, 0, 0))
    up_fwd_b = pl.BlockSpec((None, D, tu), lambda i, j, k: (j // per, 0, j % per))

    def fwd_layer(l, xc, g_in, rest, token):
        g_mix = vec(mix_norm, l) if token is None else vec(mix_norm, l) + token[0:1, 0:1]
        w_in_l = jnp.concatenate([g_in[0], g_in[1], g_in[2], g_in[3], jnp.zeros((D, NP - n_in), BF16)], axis=1)
        s = dict(x0=xc, w_in=w_in_l)
        s["h"] = _rms_fwd(xc, g_mix, name="rms_mix_fwd")
        s["proj"] = _matmul(s["h"], w_in_l, tn=2432, tk=D, name="mm_in_fwd")
        s["sb"], s["tot"] = _sb_fwd(s["proj"], nh, name="sb_fwd")
        gdn_out, s["gdn"] = _gdn_forward(s["proj"], gdn_conv_full[l], ab[l], vec(gdn_norm, l), nh, "")
        g_out, g_xq, g_xkv, g_xo, g_up, g_down = rest(gdn_out)
        w_out_l, w_xq_l, w_xkv_l, w_down_l = g_out.reshape(2 * GW, D), g_xq.reshape(D, XW), g_xkv.reshape(D, 2 * XW), g_down.reshape(F, D)
        s.update(w_out=w_out_l, w_xq=w_xq_l, w_xkv=w_xkv_l, w_xo=g_xo, w_up=g_up, w_down=w_down_l)
        s["mixed"] = jnp.concatenate([s["sb"], gdn_out], axis=1)
        s["x1"] = _matmul(s["mixed"], w_out_l, res=xc, tm=1024, tn=1024, tk=2 * GW, name="mm_out_fwd")
        s["memn"] = _rms_fwd(mem[0], vec(mem_norm, l), name="rms_mem_fwd")
        s["kv"] = _matmul(s["memn"], w_xkv_l, out_dtype=BF16, tk=D, name="mm_xkv_fwd")
        s["hq"] = _rms_fwd(s["x1"], vec(xattn_norm, l), name="rms_xattn_fwd")
        s["q"] = _matmul(s["hq"], w_xq_l, out_dtype=BF16, tk=D, name="mm_xq_fwd")
        s["xo"] = _xattn_fwd(s["q"], s["kv"], name="xattn_fwd")
        s["x2"] = _matmul(s["xo"], g_xo, res=s["x1"], dims=(S, D, XW), tn=cs_xo, tk=XW, b_spec=xo_fwd_b, name="mm_xo_fwd")
        s["hf"] = _rms_fwd(s["x2"], vec(ffn_norm, l), name="rms_ffn_fwd")
        s["u"] = _matmul(s["hf"], g_up, dims=(S, 2 * F, D), tn=tu, tk=D, b_spec=up_fwd_b, name="mm_up_fwd")
        s["act"] = _ffn_act_fwd(s["u"], ffn_conv_full[l], ffn_conv_bias[l:l + 1], name="ffn_act_fwd")
        x3 = _matmul(s["act"], w_down_l, res=s["x2"], tm=1024, tn=1024, tk=tu, name="mm_down_fwd")
        return x3, s

    def bwd_ffn(l, s, dx3, dx3b):
        dact = _matmul(dx3b, s["w_down"], tb=True, tm=1024, tk=D, name="mm_down_dgrad")
        d_down = _matmul(s["act"], dx3b, ta=True, tn=1024, tk=S, out_dtype=BF16, name="mm_down_wgrad")
        du3, dcw3, dcb3 = _ffn_act_bwd(s["u"], ffn_conv_full[l], ffn_conv_bias[l:l + 1], dact, name="ffn_act_bwd")
        tq, td = _tile(S, 1024), _tile(D, 1024)
        dhf = _matmul(du3, s["w_up"], tb=True, dims=(S, D, 2 * F), tm=tq, tn=td, tk=tu,
                      a_spec=pl.BlockSpec((None, tq, tu), lambda i, j, k: (k // fper, i, k % fper)),
                      b_spec=pl.BlockSpec((None, td, tu), lambda i, j, k: (k // per, j, k % per)), name="mm_up_dgrad")
        d_up = _matmul(s["hf"], du3, ta=True, dims=(D, 2 * F, S), tn=tu, tk=S,
                       b_spec=pl.BlockSpec((None, S, tu), lambda i, j, k: (j // fper, 0, j % fper)),
                       o_spec=pl.BlockSpec((None, _tile(D, 512), tu), lambda i, j, k: (j // per, i, j % per)),
                       out_shape=jax.ShapeDtypeStruct((4, D, cs_up), BF16), name="mm_up_wgrad")
        dx2, dx2b, dg_ffn = _rms_bwd(s["x2"], vec(ffn_norm, l), dhf, dx3, name="rms_bwd")
        small = [dg_ffn, jnp.concatenate([dcw3[0], dcw3[1]], axis=1), jnp.concatenate([dcb3[0], dcb3[1]], axis=1)]
        return dx2, dx2b, {5: d_up, 6: d_down.reshape(4, -1, D)}, small

    def bwd_rest(l, s, dx2, dx2b):
        dxo = _matmul(dx2b, s["w_xo"], tb=True, dims=(S, XW, D), tn=XW, tk=cs_xo, b_spec=xo_dg_b, name="mm_xo_dgrad")
        d_xo = _matmul(s["xo"], dx2b, ta=True, dims=(XW, D, S), tm=XW, tn=cs_xo, tk=S, o_spec=xo_wg_o,
                       out_shape=jax.ShapeDtypeStruct((4, XW, cs_xo), BF16), name="mm_xo_wgrad")
        dq, dk, dv = _xattn_bwd(s["q"], s["kv"], dxo, name="xattn_bwd")
        dkv = jnp.concatenate([dk, dv], axis=1)
        dhq = _matmul(dq, s["w_xq"], tb=True, tk=XW, name="mm_xq_dgrad")
        d_xq = _matmul(s["hq"], dq, ta=True, tk=S, out_dtype=BF16, name="mm_xq_wgrad")
        dmemn = _matmul(dkv, s["w_xkv"], tb=True, tk=2 * XW, name="mm_xkv_dgrad")
        d_xkv = _matmul(s["memn"], dkv, ta=True, tk=mem.shape[1], out_dtype=BF16, name="mm_xkv_wgrad")
        _, _, dg_mem = _rms_bwd(mem[0], vec(mem_norm, l), dmemn, None, name="rms_mem_bwd")
        dx1, dx1b, dg_xattn = _rms_bwd(s["x1"], vec(xattn_norm, l), dhq, dx2, name="rms_bwd")
        dmix = _matmul(dx1b, s["w_out"], tb=True, tm=1024, tn=1024, tk=D, name="mm_out_dgrad")
        d_out = _matmul(s["mixed"], dx1b, ta=True, tm=1024, tn=1024, tk=S, out_dtype=BF16, name="mm_out_wgrad")
        dq_s, dk_s, dv_s = _sb_bwd(s["proj"], s["tot"], dmix, nh, name="sb_bwd")
        dx_qkv, dz, dx_g, dconv, dab, dng = _gdn_backward(s["proj"], gdn_conv_full[l], ab[l], vec(gdn_norm, l), s["gdn"], dmix, nh, "")
        dproj = jnp.concatenate([dq_s, dk_s, dv_s, dx_qkv, dz, dx_g], axis=1)
        dh = _matmul(dproj, s["w_in"], tb=True, tm=1024, tn=1024, tk=2432, name="mm_in_dgrad")
        d_in = _matmul(s["h"], dproj, ta=True, tn=2432, tk=S, out_dtype=BF16, name="mm_in_wgrad")
        dx0, dx0b, dg_mix = _rms_bwd(s["x0"], vec(mix_norm, l), dh, dx1, name="rms_bwd")
        slabs = {0: jnp.stack([d_in[:, i * cs_in:(i + 1) * cs_in] for i in range(4)]), 1: d_out.reshape(4, -1, D),
                 2: d_xq.reshape(4, -1, XW), 3: d_xkv.reshape(4, -1, 2 * XW), 4: d_xo}
        return dx0, dx0b, slabs, [dg_mix, dconv, dab, dng, dg_xattn, dg_mem]

    def start_gather(l, idxs, after, tag):
        placed = [_cast_place(big[i], l, kvec, name=f"cast_place_{l}") for i in idxs]
        return _gather_start(placed, after, name=f"gather_start_{l}{tag}")

    def end_gather(pending, after, l, tag):
        send, recv, ws, _ = pending
        ws = _gather_wait(send, recv, ws, after, name=f"gather_wait_{l}{tag}")
        return _gather_to_sibling(ws, name=f"gather_to_sibling{tag}")

    xc = x[0]
    saved = []
    first = start_gather(0, [0], conv_all, "a")
    second = start_gather(0, list(range(1, 7)), first[3], "b")
    for l in range(L):
        if l == 0:
            g_in = end_gather(first, xc, 0, "a")[0]
            rest = lambda after: end_gather(second, after, 0, "b")
            order = second[3]
        else:
            wts = end_gather(pending, xc, l, "")
            g_in, rest, order = wts[0], (lambda after, wts=wts: wts[1:]), wts[1]
        token = None
        if l + 1 < L:
            pending = start_gather(l + 1, list(range(7)), order, "")
            token = pending[3]
        xc, s = fwd_layer(l, xc, g_in, rest, token)
        saved.append(s)
    loss_blk, dxc, dxcb, dg_final = _loss_head(xc, final_norm[None, :], loss_target[0], name="loss_head")

    def start_reduce(slabs, l, tag):
        idxs = sorted(slabs)
        return idxs, l, tag, _reduce_start([slabs[i] for i in idxs], cvec, name=f"reduce_start_{l}{tag}")

    def finish_reduce(item, sums, after):
        idxs, l, tag, (send, recv, ps, lands, _) = item
        ps, lands = _reduce_wait(send, recv, ps, lands, after, name=f"reduce_wait_{l}{tag}")
        for i, p, rb in zip(idxs, ps, lands):
            sums[i] = _sum_partials(p, rb, kcvec, sums[i], l, L, name=f"sum_partials_{l}")

    sums = [None] * 7
    small_by_layer = [None] * L
    in_flight = []
    for l in reversed(range(L)):
        if in_flight:
            dxcb = dxcb + in_flight[-1][3][4][0, 0].astype(BF16)
        dx2, dx2b, slabs_ffn, small_ffn = bwd_ffn(l, saved[l], dxc, dxcb)
        for item in in_flight:
            finish_reduce(item, sums, dx2)
        in_flight = [start_reduce(slabs_ffn, l, "f")]
        dx2b = dx2b + in_flight[-1][3][4][0, 0].astype(BF16)
        dxc, dxcb, slabs_rest, small_rest = bwd_rest(l, saved[l], dx2, dx2b)
        saved[l] = None
        in_flight.append(start_reduce(slabs_rest, l, "r"))
        small_by_layer[l] = small_rest + small_ffn

    small_flat = [a for l in range(L) for a in small_by_layer[l]] + [dg_final, loss_blk[0:1]]
    red_buf = _allreduce_small(_pack(small_flat), name="allreduce_small")
    red = _unpack(red_buf, [a.shape for a in small_flat])
    per_layer = [red[9 * l:9 * l + 9] for l in range(L)]
    col = lambda i: jnp.concatenate([p[i] for p in per_layer], axis=0)
    stk = lambda i: jnp.stack([p[i] for p in per_layer])
    g_conv_full, g_ab, g_fconv_full = stk(1), stk(2), stk(7)
    grads_small = dict(
        mix_norm=col(0), gdn_conv=lax.dynamic_slice(g_conv_full, (0, 0, chip * cs_gc), (L, SHORT_CONV, cs_gc)),
        gdn_a_log=g_ab[:, 0, nh:2 * nh], gdn_dt_bias=g_ab[:, 1, nh:2 * nh], gdn_norm=col(3), xattn_norm=col(4),
        mem_norm=col(5), ffn_norm=col(6), ffn_conv=lax.dynamic_slice(g_fconv_full, (0, 0, chip * cs_fc), (L, FFN_CONV, cs_fc)),
        ffn_conv_bias=col(8), final_norm=red[-2][0])
    loss = red[-1][0, 0]

    names_small = ["mix_norm", "gdn_conv", "gdn_a_log", "gdn_dt_bias", "gdn_norm", "xattn_norm", "mem_norm", "ffn_norm",
                   "ffn_conv", "ffn_conv_bias", "final_norm"]
    w_small = dict(mix_norm=mix_norm, gdn_conv=gdn_conv, gdn_a_log=gdn_a_log, gdn_dt_bias=gdn_dt_bias, gdn_norm=gdn_norm,
                   xattn_norm=xattn_norm, mem_norm=mem_norm, ffn_norm=ffn_norm, ffn_conv=ffn_conv, ffn_conv_bias=ffn_conv_bias,
                   final_norm=final_norm)
    m_small = dict(mix_norm=m_mix_norm, gdn_conv=m_gdn_conv, gdn_a_log=m_gdn_a_log, gdn_dt_bias=m_gdn_dt_bias, gdn_norm=m_gdn_norm,
                   xattn_norm=m_xattn_norm, mem_norm=m_mem_norm, ffn_norm=m_ffn_norm, ffn_conv=m_ffn_conv,
                   ffn_conv_bias=m_ffn_conv_bias, final_norm=m_final_norm)
    v_small = dict(mix_norm=v_mix_norm, gdn_conv=v_gdn_conv, gdn_a_log=v_gdn_a_log, gdn_dt_bias=v_gdn_dt_bias, gdn_norm=v_gdn_norm,
                   xattn_norm=v_xattn_norm, mem_norm=v_mem_norm, ffn_norm=v_ffn_norm, ffn_conv=v_ffn_conv,
                   ffn_conv_bias=v_ffn_conv_bias, final_norm=v_final_norm)
    shapes_small = [w_small[n].shape for n in names_small]
    packed = [_pack([d[n] for n in names_small], row_multiple=128)[None] for d in (w_small, grads_small, m_small, v_small)]
    upd_small = [_unpack(o[0], shapes_small) for o in _adamw(*packed, name="adamw_small")]
    delta, new_m, new_v = [dict(zip(names_small, u)) for u in upd_small]
    grads = dict(grads_small)
    big_names = ["w_in", "w_out", "w_xq", "w_xkv", "w_xo", "w_up", "w_down"]
    big_m = [m_w_in, m_w_out, m_w_xq, m_w_xkv, m_w_xo, m_w_up, m_w_down]
    big_v = [v_w_in, v_w_out, v_w_xq, v_w_xkv, v_w_xo, v_w_up, v_w_down]
    after = red_buf
    for item in in_flight:
        finish_reduce(item, sums, after)
        idxs, tag = item[0], item[2]
        from_sib = _swap_with_sibling([sums[i] for i in idxs], name=f"swap_halves_{tag}")
        for i, gs in zip(idxs, from_sib):
            n = big_names[i]
            grads[n], delta[n], new_m[n], new_v[n] = _adamw_halves(big[i], sums[i], gs, cvec, big_m[i], big_v[i],
                                                                   name=f"adamw_{n}")
        after = delta[big_names[idxs[-1]]]

    order = ["mix_norm", "w_in", "gdn_conv", "gdn_a_log", "gdn_dt_bias", "gdn_norm", "w_out", "xattn_norm", "mem_norm", "w_xq",
             "w_xkv", "w_xo", "ffn_norm", "w_up", "ffn_conv", "ffn_conv_bias", "w_down", "final_norm"]
    return (loss, dxc[None], *[grads[n] for n in order], *[delta[n] for n in order], *[new_m[n] for n in order],
            *[new_v[n] for n in order])
```

```python
import functools
import math

import jax
import jax.numpy as jnp
from jax import lax
from jax.experimental import pallas as pl
from jax.experimental.pallas import tpu as pltpu

F32 = jnp.float32
BF16 = jnp.bfloat16

HEAD_DIM = 128
CHUNK = 64
GDN_CPB = 16
SB_TQ, SB_TK = 512, 512
SHORT_CONV = 4
FFN_CONV = 3
FFN_ROWS = 128
X_HEADS = 4
EPS = 1e-6
LANES = 128
VMEM_LIMIT = 56 * 2**20

ADAM_LR, ADAM_B1, ADAM_B2, ADAM_EPS, ADAM_WD, ADAM_STEP = 0.001, 0.9, 0.999, 1e-08, 0.01, 10

HI = lax.Precision.HIGH


def _params(sem):
    return pltpu.CompilerParams(dimension_semantics=sem, vmem_limit_bytes=VMEM_LIMIT)


def _tile(n, want):
    if n <= want:
        return n
    t = (want // LANES) * LANES
    while t > LANES and n % t:
        t -= LANES
    assert n % t == 0, (n, want)
    return t


def _sigmoid(x):
    return jax.nn.sigmoid(x)


def _softplus(x):
    return jnp.maximum(x, 0.0) + jnp.log(1.0 + jnp.exp(-jnp.abs(x)))


def _matmul(a, b, *, name, ta=False, tb=False, out_dtype=F32, res=None, tm=512, tn=512, tk=2048,
            dims=None, a_spec=None, b_spec=None, o_spec=None, out_shape=None):
    if dims is None:
        M, K = (a.shape[1], a.shape[0]) if ta else a.shape
        N = b.shape[0] if tb else b.shape[1]
    else:
        M, N, K = dims
    tm, tn, tk = _tile(M, tm), _tile(N, tn), _tile(K, tk)
    nk = K // tk
    dn = (((0 if ta else 1,), (1 if tb else 0,)), ((), ()))

    def body(*refs):
        a_ref, b_ref = refs[0], refs[1]
        r_ref = refs[2] if res is not None else None
        o_ref = refs[3] if res is not None else refs[2]
        p = lax.dot_general(a_ref[...].astype(BF16), b_ref[...].astype(BF16), dn, preferred_element_type=F32)

        def finish(acc):
            if r_ref is not None:
                acc = acc + r_ref[...].astype(F32)
            o_ref[...] = acc.astype(o_ref.dtype)

        if nk == 1:
            finish(p)
        else:
            acc_ref = refs[-1]
            k = pl.program_id(2)

            @pl.when(k == 0)
            def _():
                acc_ref[...] = p

            @pl.when(jnp.logical_and(k > 0, k < nk - 1))
            def _():
                acc_ref[...] += p

            @pl.when(k == nk - 1)
            def _():
                finish(acc_ref[...] + p)

    if a_spec is None:
        a_spec = pl.BlockSpec((tk, tm), lambda i, j, k: (k, i)) if ta else pl.BlockSpec((tm, tk), lambda i, j, k: (i, k))
    if b_spec is None:
        b_spec = pl.BlockSpec((tn, tk), lambda i, j, k: (j, k)) if tb else pl.BlockSpec((tk, tn), lambda i, j, k: (k, j))
    if o_spec is None:
        o_spec = pl.BlockSpec((tm, tn), lambda i, j, k: (i, j))
    if out_shape is None:
        out_shape = jax.ShapeDtypeStruct((M, N), out_dtype)
    in_specs, args = [a_spec, b_spec], [a, b]
    if res is not None:
        in_specs.append(pl.BlockSpec((tm, tn), lambda i, j, k: (i, j)))
        args.append(res)
    return pl.pallas_call(
        body, grid=(M // tm, N // tn, nk), in_specs=in_specs, out_specs=o_spec, out_shape=out_shape,
        scratch_shapes=[pltpu.VMEM((tm, tn), F32)] if nk > 1 else [],
        compiler_params=_params(("parallel", "parallel", "arbitrary")), name=name)(*args)


def _rms_fwd(x, g, *, name):
    R, D = x.shape
    tr = _tile(R, 256)

    def body(x_ref, g_ref, o_ref):
        xv = x_ref[...]
        rstd = lax.rsqrt(jnp.mean(xv * xv, axis=-1, keepdims=True) + EPS)
        o_ref[...] = (xv * rstd * g_ref[...]).astype(o_ref.dtype)

    return pl.pallas_call(
        body, grid=(R // tr,), in_specs=[pl.BlockSpec((tr, D), lambda i: (i, 0)), pl.BlockSpec((1, D), lambda i: (0, 0))],
        out_specs=pl.BlockSpec((tr, D), lambda i: (i, 0)), out_shape=jax.ShapeDtypeStruct((R, D), BF16),
        compiler_params=_params(("parallel",)), name=name)(x, g)


def _rms_bwd(x, g, dh, dres, *, name):
    R, D = x.shape
    tr = _tile(R, 256)

    def body(*refs):
        if dres is None:
            x_ref, g_ref, dh_ref, dx_ref, dxb_ref, dg_ref = refs
        else:
            x_ref, g_ref, dh_ref, dr_ref, dx_ref, dxb_ref, dg_ref = refs
        xv = x_ref[...]
        dhv = dh_ref[...].astype(F32)
        rstd = lax.rsqrt(jnp.mean(xv * xv, axis=-1, keepdims=True) + EPS)
        xhat = xv * rstd
        t = dhv * g_ref[...]
        dx = rstd * (t - xhat * jnp.mean(t * xhat, axis=-1, keepdims=True))
        if dres is not None:
            dx = dx + dr_ref[...]
        dx_ref[...] = dx
        dxb_ref[...] = dx.astype(BF16)
        part = jnp.sum(dhv * xhat, axis=0, keepdims=True)

        @pl.when(pl.program_id(0) == 0)
        def _():
            dg_ref[...] = part

        @pl.when(pl.program_id(0) > 0)
        def _():
            dg_ref[...] += part

    row = pl.BlockSpec((tr, D), lambda i: (i, 0))
    vec = pl.BlockSpec((1, D), lambda i: (0, 0))
    in_specs = [row, vec, row] + ([row] if dres is not None else [])
    args = [x, g, dh] + ([dres] if dres is not None else [])
    return pl.pallas_call(
        body, grid=(R // tr,), in_specs=in_specs, out_specs=[row, row, vec],
        out_shape=[jax.ShapeDtypeStruct((R, D), F32), jax.ShapeDtypeStruct((R, D), BF16), jax.ShapeDtypeStruct((1, D), F32)],
        compiler_params=_params(("arbitrary",)), name=name)(*args)


def _loss_head(x, g, tgt, *, name):
    R, D = x.shape
    tr = _tile(R, 256)

    def body(x_ref, g_ref, t_ref, l_ref, dx_ref, dxb_ref, dg_ref):
        xv = x_ref[...]
        rstd = lax.rsqrt(jnp.mean(xv * xv, axis=-1, keepdims=True) + EPS)
        xhat = xv * rstd
        err = xhat * g_ref[...] - t_ref[...]
        dy = err * (1.0 / D)
        t = dy * g_ref[...]
        dx = rstd * (t - xhat * jnp.mean(t * xhat, axis=-1, keepdims=True))
        dx_ref[...] = dx
        dxb_ref[...] = dx.astype(BF16)
        part = jnp.sum(dy * xhat, axis=0, keepdims=True)
        lpart = jnp.zeros((8, LANES), F32) + 0.5 * jnp.sum(jnp.mean(err * err, axis=-1, keepdims=True))

        @pl.when(pl.program_id(0) == 0)
        def _():
            dg_ref[...] = part
            l_ref[...] = lpart

        @pl.when(pl.program_id(0) > 0)
        def _():
            dg_ref[...] += part
            l_ref[...] += lpart

    row = pl.BlockSpec((tr, D), lambda i: (i, 0))
    vec = pl.BlockSpec((1, D), lambda i: (0, 0))
    return pl.pallas_call(
        body, grid=(R // tr,), in_specs=[row, vec, row],
        out_specs=[pl.BlockSpec((8, LANES), lambda i: (0, 0)), row, row, vec],
        out_shape=[jax.ShapeDtypeStruct((8, LANES), F32), jax.ShapeDtypeStruct((R, D), F32),
                   jax.ShapeDtypeStruct((R, D), BF16), jax.ShapeDtypeStruct((1, D), F32)],
        compiler_params=_params(("arbitrary",)), name=name)(x, g, tgt)


def _shift_down(x, s):
    if s == 0:
        return x
    row = lax.broadcasted_iota(jnp.int32, x.shape, 0)
    return jnp.where(row >= s, pltpu.roll(x, s, 0), 0.0)


def _shift_up(x, s):
    if s == 0:
        return x
    n = x.shape[0]
    row = lax.broadcasted_iota(jnp.int32, x.shape, 0)
    return jnp.where(row < n - s, pltpu.roll(x, n - s, 0), 0.0)


def _dwconv(x, w):
    k = w.shape[0]
    acc = x * w[k - 1:k, :]
    for i in range(k - 1):
        acc = acc + _shift_down(x, k - 1 - i) * w[i:i + 1, :]
    return acc


def _dwconv_bwd(x, w, dc, off=0, rows=None):
    k = w.shape[0]
    keep = slice(off, off + (x.shape[0] if rows is None else rows))
    dx = dc * w[k - 1:k, :]
    dws = []
    for i in range(k - 1):
        s = k - 1 - i
        dx = dx + _shift_up(dc, s) * w[i:i + 1, :]
        dws.append(jnp.sum((dc * _shift_down(x, s))[keep], axis=0, keepdims=True))
    dws.append(jnp.sum((dc * x)[keep], axis=0, keepdims=True))
    return dx[keep], jnp.concatenate(dws, axis=0)


def _ffn_act_fwd(u, cw, cb, *, name):
    S, F2 = u.shape
    F = F2 // 2
    tc = _tile(F, 256)
    nb = F // tc

    def body(ug_ref, uu_ref, wg_ref, wu_ref, bg_ref, bu_ref, o_ref):
        cg = _dwconv(ug_ref[...], wg_ref[...]) + bg_ref[...]
        cu = _dwconv(uu_ref[...], wu_ref[...]) + bu_ref[...]
        o_ref[...] = (cg * _sigmoid(cg) * cu).astype(o_ref.dtype)

    blk = lambda r, off: pl.BlockSpec((r, tc), lambda j: (0, j + off))
    return pl.pallas_call(
        body, grid=(nb,), in_specs=[blk(S, 0), blk(S, nb), blk(FFN_CONV, 0), blk(FFN_CONV, nb), blk(1, 0), blk(1, nb)],
        out_specs=blk(S, 0), out_shape=jax.ShapeDtypeStruct((S, F), BF16),
        compiler_params=_params(("parallel",)), name=name)(u, u, cw, cw, cb, cb)


def _ffn_act_bwd(u, cw, cb, dact, *, name):
    S, F2 = u.shape
    F = F2 // 2
    tc = _tile(F, 128)
    nb = F // tc
    R = min(FFN_ROWS, S)
    W = min(R + 16, S)

    def body(ug_ref, uu_ref, wg_ref, wu_ref, bg_ref, bu_ref, da_ref, du_ref, dw_ref, db_ref):
        wg, wu, bg, bu = wg_ref[...], wu_ref[...], bg_ref[...], bu_ref[...]
        dwg = dwu = jnp.zeros((FFN_CONV, tc), F32)
        dbg = dbu = jnp.zeros((1, tc), F32)
        for r0 in range(0, S, R):
            w0 = min(max(r0 - 8, 0), S - W)
            off = r0 - w0
            ug, uu = ug_ref[w0:w0 + W, :], uu_ref[w0:w0 + W, :]
            cg = _dwconv(ug, wg) + bg
            cu = _dwconv(uu, wu) + bu
            sg = _sigmoid(cg)
            da = da_ref[w0:w0 + W, :].astype(F32)
            dcu = da * (cg * sg)
            dcg = da * cu * (sg * (1.0 + cg * (1.0 - sg)))
            dxg, dg = _dwconv_bwd(ug, wg, dcg, off, R)
            dxu, du = _dwconv_bwd(uu, wu, dcu, off, R)
            du_ref[0, r0:r0 + R, :] = dxg.astype(du_ref.dtype)
            du_ref[1, r0:r0 + R, :] = dxu.astype(du_ref.dtype)
            dwg, dwu = dwg + dg, dwu + du
            dbg = dbg + jnp.sum(dcg[off:off + R], axis=0, keepdims=True)
            dbu = dbu + jnp.sum(dcu[off:off + R], axis=0, keepdims=True)
        dw_ref[0] = dwg
        dw_ref[1] = dwu
        db_ref[0] = dbg
        db_ref[1] = dbu

    blk = lambda r, off: pl.BlockSpec((r, tc), lambda j: (0, j + off))
    blk3 = lambda r: pl.BlockSpec((2, r, tc), lambda j: (0, 0, j))
    return pl.pallas_call(
        body, grid=(nb,),
        in_specs=[blk(S, 0), blk(S, nb), blk(FFN_CONV, 0), blk(FFN_CONV, nb), blk(1, 0), blk(1, nb), blk(S, 0)],
        out_specs=[blk3(S), blk3(FFN_CONV), blk3(1)],
        out_shape=[jax.ShapeDtypeStruct((2, S, F), BF16), jax.ShapeDtypeStruct((2, FFN_CONV, F), F32),
                   jax.ShapeDtypeStruct((2, 1, F), F32)],
        compiler_params=_params(("parallel",)), name=name)(u, u, cw, cw, cb, cb, dact)


def _xattn_fwd(q, kv, *, name):
    S, XW = q.shape
    M = kv.shape[0]
    nh = XW // HEAD_DIM
    tq = _tile(S, 512)
    scale = HEAD_DIM ** -0.5

    def body(q_ref, k_ref, v_ref, o_ref):
        z = lax.dot_general(q_ref[...], k_ref[...], (((1,), (1,)), ((), ())), preferred_element_type=F32) * scale
        e = jnp.exp(z - jnp.max(z, axis=-1, keepdims=True))
        p = e / jnp.sum(e, axis=-1, keepdims=True)
        o_ref[...] = jnp.dot(p.astype(BF16), v_ref[...], preferred_element_type=F32).astype(o_ref.dtype)

    return pl.pallas_call(
        body, grid=(nh, S // tq),
        in_specs=[pl.BlockSpec((tq, HEAD_DIM), lambda h, i: (i, h)), pl.BlockSpec((M, HEAD_DIM), lambda h, i: (0, h)),
                  pl.BlockSpec((M, HEAD_DIM), lambda h, i: (0, nh + h))],
        out_specs=pl.BlockSpec((tq, HEAD_DIM), lambda h, i: (i, h)), out_shape=jax.ShapeDtypeStruct((S, XW), BF16),
        compiler_params=_params(("parallel", "parallel")), name=name)(q, kv, kv)


def _xattn_bwd(q, kv, do, *, name):
    S, XW = q.shape
    M = kv.shape[0]
    nh = XW // HEAD_DIM
    tq = _tile(S, 512)
    scale = HEAD_DIM ** -0.5
    nt = (((1,), (1,)), ((), ()))
    tn = (((0,), (0,)), ((), ()))

    def body(q_ref, k_ref, v_ref, do_ref, dq_ref, dk_ref, dv_ref):
        qv, kvv, vv = q_ref[...], k_ref[...], v_ref[...]
        dov = do_ref[...].astype(BF16)
        z = lax.dot_general(qv, kvv, nt, preferred_element_type=F32) * scale
        e = jnp.exp(z - jnp.max(z, axis=-1, keepdims=True))
        p = e / jnp.sum(e, axis=-1, keepdims=True)
        dp = lax.dot_general(dov, vv, nt, preferred_element_type=F32)
        ds = (p * (dp - jnp.sum(dp * p, axis=-1, keepdims=True)) * scale).astype(BF16)
        dq_ref[...] = jnp.dot(ds, kvv, preferred_element_type=F32).astype(dq_ref.dtype)
        dk = lax.dot_general(ds, qv, tn, preferred_element_type=F32)
        dv = lax.dot_general(p.astype(BF16), dov, tn, preferred_element_type=F32)

        @pl.when(pl.program_id(1) == 0)
        def _():
            dk_ref[...] = dk
            dv_ref[...] = dv

        @pl.when(pl.program_id(1) > 0)
        def _():
            dk_ref[...] += dk
            dv_ref[...] += dv

    qs = pl.BlockSpec((tq, HEAD_DIM), lambda h, i: (i, h))
    ms = pl.BlockSpec((M, HEAD_DIM), lambda h, i: (0, h))
    return pl.pallas_call(
        body, grid=(nh, S // tq),
        in_specs=[qs, ms, pl.BlockSpec((M, HEAD_DIM), lambda h, i: (0, nh + h)), qs],
        out_specs=[qs, ms, ms],
        out_shape=[jax.ShapeDtypeStruct((S, XW), BF16), jax.ShapeDtypeStruct((M, XW), F32), jax.ShapeDtypeStruct((M, XW), F32)],
        compiler_params=_params(("parallel", "arbitrary")), name=name)(q, kv, kv, do)


_NN = (((1,), (0,)), ((), ()))
_NT = (((1,), (1,)), ((), ()))
_TN = (((0,), (0,)), ((), ()))


def _batched(dn, a):
    if a.ndim == 2:
        return dn
    (ca,), (cb,) = dn[0]
    return (((ca + 1,), (cb + 1,)), ((0,), (0,)))


def _dot(a, b, dn=_NN):
    return lax.dot_general(a.astype(BF16), b.astype(BF16), _batched(dn, a), preferred_element_type=F32)


def _dot_hi(a, b, dn=_NN):
    return lax.dot_general(a, b, _batched(dn, a), preferred_element_type=F32, precision=HI)


def _dot_split(a, b01, dn=_NN):
    hi = a.astype(BF16)
    lo = (a - hi.astype(F32)).astype(BF16)
    return (lax.dot_general(hi, b01, dn, preferred_element_type=F32)
            + lax.dot_general(lo, b01, dn, preferred_element_type=F32))


def _running_sums(x, carry, reverse):
    nb = x.shape[1] // LANES
    tri = _after_matrix(LANES, transpose=not reverse)
    blocks = [x[:, b * LANES:(b + 1) * LANES] for b in range(nb)]
    sums = [jnp.sum(blk, axis=1, keepdims=True) for blk in blocks]
    out = [None] * nb
    run = carry
    for b in (reversed(range(nb)) if reverse else range(nb)):
        out[b] = _dot_split(blocks[b], tri) + run
        run = run + sums[b]
    total = sums[0]
    for b in range(1, nb):
        total = total + sums[b]
    return jnp.concatenate(out, axis=1), total


def _after_matrix(n, transpose=False):
    row = lax.broadcasted_iota(jnp.int32, (n, n), 0)
    col = lax.broadcasted_iota(jnp.int32, (n, n), 1)
    return (row < col if transpose else row > col).astype(BF16)


def _sb_fwd(proj, nh, *, name):
    S = proj.shape[0]
    TQ, TK = min(SB_TQ, S), min(SB_TK, S)
    nq = S // TQ
    scale = HEAD_DIM ** -0.5

    def body(q_ref, k_ref, v_ref, o_ref, tot_ref):
        i = pl.program_id(1)
        q = q_ref[...].astype(BF16)
        qpos = i * TQ + lax.broadcasted_iota(jnp.int32, (TQ, TK), 0)
        kcol = lax.broadcasted_iota(jnp.int32, (TQ, TK), 1)
        nt = ((i + 1) * TQ + TK - 1) // TK
        n_in = (i * TQ) // TK

        def make_step(masked):
            def step(j, carry):
                acc, out = carry
                off = pl.multiple_of(j * TK, TK)
                kb = k_ref[pl.ds(off, TK), :].astype(BF16)
                vb = v_ref[pl.ds(off, TK), :].astype(BF16)
                z = lax.dot_general(q, kb, _NT, preferred_element_type=F32) * scale
                ls = -_softplus(z)
                if masked:
                    valid = kcol + off < qpos
                    ls = jnp.where(valid, ls, 0.0)
                later, rs = _running_sums(ls, acc, reverse=True)
                w = jnp.exp(ls + z + later)
                if masked:
                    w = jnp.where(valid, w, 0.0)
                out = out + jnp.dot(w.astype(BF16), vb, preferred_element_type=F32)
                return acc + rs, out
            return step

        edge, inner = make_step(True), make_step(False)
        carry = (jnp.zeros((TQ, 1), F32), jnp.zeros((TQ, HEAD_DIM), F32))
        carry = lax.fori_loop(0, nt - n_in, lambda t, c: edge(nt - 1 - t, c), carry)
        acc, out = lax.fori_loop(0, n_in, lambda t, c: inner(n_in - 1 - t, c), carry)
        o_ref[...] = out.astype(o_ref.dtype)
        tot_ref[...] = acc

    return pl.pallas_call(
        body, grid=(nh, nq),
        in_specs=[pl.BlockSpec((TQ, HEAD_DIM), lambda h, i: (i, h)),
                  pl.BlockSpec((S, HEAD_DIM), lambda h, i: (0, nh + h)),
                  pl.BlockSpec((S, HEAD_DIM), lambda h, i: (0, 2 * nh + h))],
        out_specs=[pl.BlockSpec((TQ, HEAD_DIM), lambda h, i: (i, h)), pl.BlockSpec((None, TQ, 1), lambda h, i: (h, i, 0))],
        out_shape=[jax.ShapeDtypeStruct((S, nh * HEAD_DIM), BF16), jax.ShapeDtypeStruct((nh, S, 1), F32)],
        compiler_params=_params(("parallel", "parallel")), name=name)(proj, proj, proj)


def _sb_bwd(proj, tot, dmix, nh, *, name):
    S = proj.shape[0]
    TQ, TK = min(SB_TQ, S), min(SB_TK, S)
    nq = S // TQ
    scale = HEAD_DIM ** -0.5

    def body(q_ref, k_ref, v_ref, tot_ref, do_ref, dq_ref, dk_ref, dv_ref, dk_acc, dv_acc):
        i = pl.program_id(1)

        @pl.when(i == 0)
        def _():
            dk_acc[...] = jnp.zeros_like(dk_acc)
            dv_acc[...] = jnp.zeros_like(dv_acc)

        q = q_ref[...].astype(BF16)
        do = do_ref[...].astype(BF16)
        tot = tot_ref[...]
        qpos = i * TQ + lax.broadcasted_iota(jnp.int32, (TQ, TK), 0)
        kcol = lax.broadcasted_iota(jnp.int32, (TQ, TK), 1)
        nt = ((i + 1) * TQ + TK - 1) // TK
        n_in = (i * TQ) // TK

        def make_step(masked):
            def step(j, carry):
                pre, g_sum, dq = carry
                off = pl.multiple_of(j * TK, TK)
                kb = k_ref[pl.ds(off, TK), :].astype(BF16)
                vb = v_ref[pl.ds(off, TK), :].astype(BF16)
                z = lax.dot_general(q, kb, _NT, preferred_element_type=F32) * scale
                ls = -_softplus(z)
                if masked:
                    valid = kcol + off < qpos
                    ls = jnp.where(valid, ls, 0.0)
                lb = ls + z
                rs = jnp.sum(ls, axis=1, keepdims=True)
                later, _ = _running_sums(ls, tot - pre - rs, reverse=True)
                w = jnp.exp(lb + later)
                if masked:
                    w = jnp.where(valid, w, 0.0)
                g = lax.dot_general(do, vb, _NT, preferred_element_type=F32) * w
                dls, gs = _running_sums(g, g_sum, reverse=False)
                sig = jnp.exp(lb)
                dz = g * (1.0 - sig) - dls * sig
                if masked:
                    dz = jnp.where(valid, dz, 0.0)
                dz = (dz * scale).astype(BF16)
                dq = dq + jnp.dot(dz, kb, preferred_element_type=F32)
                dk_acc[pl.ds(off, TK), :] += lax.dot_general(dz, q, _TN, preferred_element_type=F32)
                dv_acc[pl.ds(off, TK), :] += lax.dot_general(w.astype(BF16), do, _TN, preferred_element_type=F32)
                return pre + rs, g_sum + gs, dq
            return step

        zero = jnp.zeros((TQ, 1), F32)
        carry = lax.fori_loop(0, n_in, make_step(False), (zero, zero, jnp.zeros((TQ, HEAD_DIM), F32)))
        _, _, dq = lax.fori_loop(n_in, nt, make_step(True), carry)
        dq_ref[...] = dq.astype(dq_ref.dtype)

        @pl.when(i == nq - 1)
        def _():
            dk_ref[...] = dk_acc[...].astype(dk_ref.dtype)
            dv_ref[...] = dv_acc[...].astype(dv_ref.dtype)

    qs = pl.BlockSpec((TQ, HEAD_DIM), lambda h, i: (i, h))
    full = pl.BlockSpec((S, HEAD_DIM), lambda h, i: (0, h))
    o = jax.ShapeDtypeStruct((S, nh * HEAD_DIM), BF16)
    return pl.pallas_call(
        body, grid=(nh, nq),
        in_specs=[qs, pl.BlockSpec((S, HEAD_DIM), lambda h, i: (0, nh + h)),
                  pl.BlockSpec((S, HEAD_DIM), lambda h, i: (0, 2 * nh + h)),
                  pl.BlockSpec((None, TQ, 1), lambda h, i: (h, i, 0)), qs],
        out_specs=[qs, full, full], out_shape=[o, o, o],
        scratch_shapes=[pltpu.VMEM((S, HEAD_DIM), F32), pltpu.VMEM((S, HEAD_DIM), F32)],
        compiler_params=_params(("parallel", "arbitrary")), name=name)(proj, proj, proj, tot, dmix)


def _gdn_qkv_fwd(proj, conv_w, nh, *, name):
    S = proj.shape[0]
    GW = nh * HEAD_DIM
    scale = HEAD_DIM ** -0.5

    def body(x_ref, w_ref, o_ref):
        sec = pl.program_id(0) // nh
        c = _dwconv(x_ref[...], w_ref[...])
        s = c * _sigmoid(c)
        r = lax.rsqrt(jnp.sum(s * s, axis=1, keepdims=True) + EPS)
        fac = jnp.where(sec == 0, scale, 1.0)
        o_ref[...] = jnp.where(sec == 2, s, s * (r * fac))

    return pl.pallas_call(
        body, grid=(3 * nh,),
        in_specs=[pl.BlockSpec((S, HEAD_DIM), lambda j: (0, 3 * nh + j)), pl.BlockSpec((SHORT_CONV, HEAD_DIM), lambda j: (0, j))],
        out_specs=pl.BlockSpec((None, S, HEAD_DIM), lambda j: (j // nh, 0, j % nh)),
        out_shape=jax.ShapeDtypeStruct((3, S, GW), F32),
        compiler_params=_params(("parallel",)), name=name)(proj, conv_w)


def _gdn_qkv_bwd(proj, conv_w, dqkv, nh, *, name):
    S = proj.shape[0]
    GW = nh * HEAD_DIM
    scale = HEAD_DIM ** -0.5

    def body(x_ref, w_ref, d_ref, dx_ref, dw_ref):
        sec = pl.program_id(0) // nh
        x, w = x_ref[...], w_ref[...]
        c = _dwconv(x, w)
        sg = _sigmoid(c)
        s = c * sg
        r = lax.rsqrt(jnp.sum(s * s, axis=1, keepdims=True) + EPS)
        sh = s * r
        d = d_ref[...]
        fac = jnp.where(sec == 0, scale, 1.0)
        dn = (r * fac) * (d - sh * jnp.sum(d * sh, axis=1, keepdims=True))
        ds = jnp.where(sec == 2, d, dn)
        dx, dw = _dwconv_bwd(x, w, ds * (sg * (1.0 + c * (1.0 - sg))))
        dx_ref[...] = dx.astype(dx_ref.dtype)
        dw_ref[...] = dw

    return pl.pallas_call(
        body, grid=(3 * nh,),
        in_specs=[pl.BlockSpec((S, HEAD_DIM), lambda j: (0, 3 * nh + j)), pl.BlockSpec((SHORT_CONV, HEAD_DIM), lambda j: (0, j)),
                  pl.BlockSpec((None, S, HEAD_DIM), lambda j: (j // nh, 0, j % nh))],
        out_specs=[pl.BlockSpec((S, HEAD_DIM), lambda j: (0, j)), pl.BlockSpec((SHORT_CONV, HEAD_DIM), lambda j: (0, j))],
        out_shape=[jax.ShapeDtypeStruct((S, 3 * GW), BF16), jax.ShapeDtypeStruct((SHORT_CONV, 3 * GW), F32)],
        compiler_params=_params(("parallel",)), name=name)(proj, conv_w, dqkv)


def _gdn_gates_fwd(proj, ab, nh, *, name):
    S = proj.shape[0]
    C = CHUNK

    def body(x_ref, ab_ref, o_ref):
        ri = lax.broadcasted_iota(jnp.int32, (C, C), 0)
        ci = lax.broadcasted_iota(jnp.int32, (C, C), 1)
        ltri = (ri >= ci).astype(F32)
        lane = lax.broadcasted_iota(jnp.int32, (C, LANES), 1)
        a_coef = -jnp.exp(ab_ref[0:1, :])
        dt = ab_ref[1:2, :]

        def chunk(n, _):
            rows = pl.ds(pl.multiple_of(n * C, C), C)
            x = x_ref[rows, :]
            beta = _sigmoid(x)
            g = jnp.where(jnp.logical_and(lane >= nh, lane < 2 * nh), a_coef * _softplus(x + dt), 0.0)
            gc = _dot_hi(ltri, pltpu.roll(g, nh, 1))
            o_ref[rows, :] = jnp.where(lane < nh, beta, g) + gc
            return 0

        lax.fori_loop(0, S // C, chunk, 0)

    return pl.pallas_call(
        body, grid=(1,),
        in_specs=[pl.BlockSpec((S, LANES), lambda i: (0, 7 * nh)), pl.BlockSpec((2, LANES), lambda i: (0, 0))],
        out_specs=pl.BlockSpec((S, LANES), lambda i: (0, 0)), out_shape=jax.ShapeDtypeStruct((S, LANES), F32),
        compiler_params=_params(("arbitrary",)), name=name)(proj, ab)


def _gdn_gates_bwd(proj, ab, dgt, nh, *, name):
    S = proj.shape[0]
    C = CHUNK

    def body(x_ref, ab_ref, d_ref, dx_ref, dab_ref):
        ri = lax.broadcasted_iota(jnp.int32, (C, C), 0)
        ci = lax.broadcasted_iota(jnp.int32, (C, C), 1)
        utri = (ri <= ci).astype(F32)
        lane = lax.broadcasted_iota(jnp.int32, (C, LANES), 1)
        is_b = lane < nh
        is_a = jnp.logical_and(lane >= nh, lane < 2 * nh)
        a_coef = -jnp.exp(ab_ref[0:1, :])
        dt = ab_ref[1:2, :]

        def chunk(n, carry):
            da_log, ddt = carry
            rows = pl.ds(pl.multiple_of(n * C, C), C)
            x = x_ref[rows, :]
            d = d_ref[rows, :]
            beta = _sigmoid(x)
            dg = pltpu.roll(_dot_hi(utri, jnp.where(lane >= 2 * nh, d, 0.0)), LANES - nh, 1)
            dg = jnp.where(is_a, dg, 0.0)
            dxa = dg * a_coef * _sigmoid(x + dt)
            dxb = jnp.where(is_b, d * beta * (1.0 - beta), 0.0)
            dx_ref[rows, :] = (dxa + dxb).astype(dx_ref.dtype)
            da_log = da_log + jnp.sum(dg * a_coef * _softplus(x + dt), axis=0, keepdims=True)
            return da_log, ddt + jnp.sum(dxa, axis=0, keepdims=True)

        zero = jnp.zeros((1, LANES), F32)
        da_log, ddt = lax.fori_loop(0, S // C, chunk, (zero, zero))
        dab_ref[0:1, :] = da_log
        dab_ref[1:2, :] = ddt

    return pl.pallas_call(
        body, grid=(1,),
        in_specs=[pl.BlockSpec((S, LANES), lambda i: (0, 7 * nh)), pl.BlockSpec((2, LANES), lambda i: (0, 0)),
                  pl.BlockSpec((S, LANES), lambda i: (0, 0))],
        out_specs=[pl.BlockSpec((S, LANES), lambda i: (0, 0)), pl.BlockSpec((2, LANES), lambda i: (0, 0))],
        out_shape=[jax.ShapeDtypeStruct((S, LANES), BF16), jax.ShapeDtypeStruct((2, LANES), F32)],
        compiler_params=_params(("arbitrary",)), name=name)(proj, ab, dgt)


def _unit_lower_inverse(lmat):
    C = lmat.shape[-1]
    ri = lax.broadcasted_iota(jnp.int32, lmat.shape, lmat.ndim - 2)
    ci = lax.broadcasted_iota(jnp.int32, lmat.shape, lmat.ndim - 1)
    nmat = -lmat
    p = jnp.where(ri == ci, 1.0, 0.0) + nmat
    for _ in range(int(math.log2(C)) - 1):
        nmat = _dot_hi(nmat, nmat)
        p = p + _dot_hi(p, nmat)
    return p


def _gdn_chunk_common(q, k, v, gates, gc_row, h, nh, tinv=None):
    C = CHUNK
    lane = lax.broadcasted_iota(jnp.int32, gates.shape, gates.ndim - 1)
    beta = jnp.sum(jnp.where(lane == h, gates, 0.0), axis=-1, keepdims=True)
    gc = jnp.sum(jnp.where(lane == 2 * nh + h, gates, 0.0), axis=-1, keepdims=True)
    sq = gates.shape[:-1] + (C,)
    ri = lax.broadcasted_iota(jnp.int32, sq, len(sq) - 2)
    ci = lax.broadcasted_iota(jnp.int32, sq, len(sq) - 1)
    incl, strict = ri >= ci, ri > ci
    decay = jnp.where(incl, jnp.exp(jnp.where(incl, gc - gc_row, 0.0)), 0.0)
    egc = jnp.exp(gc)
    kb, vb = k * beta, v * beta
    lmat = jnp.where(strict, _dot(kb, k, _NT) * decay, 0.0)
    kbg = kb * egc
    u = w = None
    if tinv is None:
        tinv = _unit_lower_inverse(lmat)
        u = _dot(tinv, vb)
        w = _dot(tinv, kbg)
    amat = _dot(q, k, _NT) * decay
    glast = gc[..., C - 1:C, :]
    ekt = jnp.exp(glast - gc)
    return dict(q=q, k=k, v=v, beta=beta, decay=decay, egc=egc, kb=kb, vb=vb, lmat=lmat, tinv=tinv, kbg=kbg, u=u, w=w,
                amat=amat, qd=q * egc, ekt=ekt, kt=k * ekt, cd=jnp.exp(glast), strict=strict, incl=incl)


def _gdn_chunk_specs(nh, S, nc):
    return [pl.BlockSpec((3, S, HEAD_DIM), lambda h: (0, 0, h)),
            pl.BlockSpec((S, LANES), lambda h: (0, 0)),
            pl.BlockSpec((None, nc, 1, CHUNK), lambda h: (h, 0, 0, 0))]


def _gdn_state_free(qkv_ref, gates_ref, gr_ref, g, nb, h, nh):
    C = CHUNK
    rows = pl.ds(pl.multiple_of(g * (nb * C), nb * C), nb * C)
    part = lambda x: x.reshape(nb, C, x.shape[-1])
    return rows, _gdn_chunk_common(part(qkv_ref[0, rows, :]), part(qkv_ref[1, rows, :]), part(qkv_ref[2, rows, :]),
                                   part(gates_ref[rows, :]), gr_ref[pl.ds(g * nb, nb)], h, nh)


def _gdn_chunk_fwd(qkv, gates, gc_row, *, name):
    _, S, GW = qkv.shape
    nh, C = GW // HEAD_DIM, CHUNK
    nc = S // C
    nb = min(GDN_CPB, nc)
    flat = lambda x: x.reshape(nb * C, x.shape[-1])

    def body(qkv_ref, gates_ref, gr_ref, o_ref, st_ref, u_s, w_s, a_s, qd_s, kt_s, cd_s):
        h = pl.program_id(0)

        def group(g, _):
            rows, m = _gdn_state_free(qkv_ref, gates_ref, gr_ref, g, nb, h, nh)
            u_s[rows, :] = flat(m["u"])
            w_s[rows, :] = flat(m["w"])
            a_s[rows, :] = flat(m["amat"])
            qd_s[rows, :] = flat(m["qd"])
            kt_s[rows, :] = flat(m["kt"])
            cd_s[pl.ds(g * nb, nb)] = jnp.broadcast_to(m["cd"], (nb, 8, LANES))
            return 0

        lax.fori_loop(0, nc // nb, group, 0)

        def chunk(n, s0):
            rows = pl.ds(pl.multiple_of(n * C, C), C)
            st_ref[n] = s0
            v_new = u_s[rows, :] - _dot(w_s[rows, :], s0)
            o_ref[rows, :] = _dot(qd_s[rows, :], s0) + _dot(a_s[rows, :], v_new)
            return s0 * cd_s[n][0:1, :] + _dot(kt_s[rows, :], v_new, _TN)

        lax.fori_loop(0, nc, chunk, jnp.zeros((HEAD_DIM, HEAD_DIM), F32))

    seq = pltpu.VMEM((S, HEAD_DIM), F32)
    return pl.pallas_call(
        body, grid=(nh,), in_specs=_gdn_chunk_specs(nh, S, nc),
        out_specs=[pl.BlockSpec((S, HEAD_DIM), lambda h: (0, h)),
                   pl.BlockSpec((None, nc, HEAD_DIM, HEAD_DIM), lambda h: (h, 0, 0, 0))],
        out_shape=[jax.ShapeDtypeStruct((S, GW), F32), jax.ShapeDtypeStruct((nh, nc, HEAD_DIM, HEAD_DIM), F32)],
        scratch_shapes=[seq, seq, pltpu.VMEM((S, C), F32), seq, seq, pltpu.VMEM((nc, 8, LANES), F32)],
        compiler_params=_params(("parallel",)), name=name)(qkv, gates, gc_row)


def _gdn_chunk_bwd(qkv, gates, gc_row, states, do, *, name):
    _, S, GW = qkv.shape
    nh, C = GW // HEAD_DIM, CHUNK
    nc = S // C
    nb = min(GDN_CPB, nc)
    flat = lambda x: x.reshape(nb * C, x.shape[-1])
    part = lambda x: x.reshape(nb, C, x.shape[-1])

    def body(qkv_ref, gates_ref, gr_ref, st_ref, do_ref, dqkv_ref, dgt_ref,
             t_s, vn_s, w_s, a_s, qd_s, kt_s, cd_s, dvn_s, dkt_s, dcd_s):
        h = pl.program_id(0)

        def group(g, _):
            rows, m = _gdn_state_free(qkv_ref, gates_ref, gr_ref, g, nb, h, nh)
            t_s[rows, :] = flat(m["tinv"])
            vn_s[rows, :] = flat(m["u"])
            w_s[rows, :] = flat(m["w"])
            a_s[rows, :] = flat(m["amat"])
            qd_s[rows, :] = flat(m["qd"])
            kt_s[rows, :] = flat(m["kt"])
            cd_s[pl.ds(g * nb, nb)] = jnp.broadcast_to(m["cd"], (nb, 8, LANES))
            return 0

        lax.fori_loop(0, nc // nb, group, 0)

        def chunk(t, dsn):
            n = nc - 1 - t
            rows = pl.ds(pl.multiple_of(n * C, C), C)
            s0, dout, w = st_ref[n], do_ref[rows, :], w_s[rows, :]
            v_new = vn_s[rows, :] - _dot(w, s0)
            dvn = _dot(a_s[rows, :], dout, _TN) + _dot(kt_s[rows, :], dsn)
            vn_s[rows, :] = v_new
            dvn_s[rows, :] = dvn
            dkt_s[rows, :] = _dot(v_new, dsn, _NT)
            dcd_s[n] = jnp.zeros((8, LANES), F32) + jnp.sum(dsn * s0)
            return _dot(qd_s[rows, :], dout, _TN) + dsn * cd_s[n][0:1, :] - _dot(w, dvn, _TN)

        lax.fori_loop(0, nc, chunk, jnp.zeros((HEAD_DIM, HEAD_DIM), F32))

        def rest(g, _):
            rows, m = _gdn_state_free_again(qkv_ref, gates_ref, gr_ref, t_s, g, nb, h, nh)
            chunks = pl.ds(g * nb, nb)
            dq, dk, dv, dgt = _gdn_chunk_grad(m, st_ref[chunks], part(vn_s[rows, :]), part(dvn_s[rows, :]),
                                              part(dkt_s[rows, :]), dcd_s[chunks][:, 0:1, 0:1], part(do_ref[rows, :]), h, nh)
            dqkv_ref[0, rows, :] = flat(dq)
            dqkv_ref[1, rows, :] = flat(dk)
            dqkv_ref[2, rows, :] = flat(dv)
            dgt_ref[rows, :] = flat(dgt)
            return 0

        lax.fori_loop(0, nc // nb, rest, 0)

    seq = pltpu.VMEM((S, HEAD_DIM), F32)
    small = pltpu.VMEM((nc, 8, LANES), F32)
    return pl.pallas_call(
        body, grid=(nh,),
        in_specs=_gdn_chunk_specs(nh, S, nc) + [
            pl.BlockSpec((None, nc, HEAD_DIM, HEAD_DIM), lambda h: (h, 0, 0, 0)),
            pl.BlockSpec((S, HEAD_DIM), lambda h: (0, h))],
        out_specs=[pl.BlockSpec((3, S, HEAD_DIM), lambda h: (0, 0, h)), pl.BlockSpec((None, S, LANES), lambda h: (h, 0, 0))],
        out_shape=[jax.ShapeDtypeStruct((3, S, GW), F32), jax.ShapeDtypeStruct((nh, S, LANES), F32)],
        scratch_shapes=[pltpu.VMEM((S, C), F32), seq, seq, pltpu.VMEM((S, C), F32), seq, seq, small, seq, seq, small],
        compiler_params=_params(("parallel",)), name=name)(qkv, gates, gc_row, states, do)


def _gdn_state_free_again(qkv_ref, gates_ref, gr_ref, t_s, g, nb, h, nh):
    C = CHUNK
    rows = pl.ds(pl.multiple_of(g * (nb * C), nb * C), nb * C)
    part = lambda x: x.reshape(nb, C, x.shape[-1])
    m = _gdn_chunk_common(part(qkv_ref[0, rows, :]), part(qkv_ref[1, rows, :]), part(qkv_ref[2, rows, :]),
                          part(gates_ref[rows, :]), gr_ref[pl.ds(g * nb, nb)], h, nh, tinv=part(t_s[rows, :]))
    return rows, m


def _gdn_chunk_grad(m, s0, v_new, dvn, dkt, dcd, dout, h, nh):
    C = CHUNK
    q, k, v, beta, decay, egc = m["q"], m["k"], m["v"], m["beta"], m["decay"], m["egc"]
    tinv, kt, cd = m["tinv"], m["kt"], m["cd"]
    dqd = _dot(dout, s0, _NT)
    damat = jnp.where(m["incl"], _dot(dout, v_new, _NT), 0.0)
    dw = -_dot(dvn, s0, _NT)
    dvb = _dot(tinv, dvn, _TN)
    dkbg = _dot(tinv, dw, _TN)
    dtinv = _dot(dvn, m["vb"], _NT) + _dot(dw, m["kbg"], _NT)
    dl = jnp.where(m["strict"], -_dot_hi(_dot_hi(tinv, dtinv, _TN), tinv, _NT), 0.0)
    dkk = dl * decay
    dqk = damat * decay
    dkb = _dot(dkk, k) + dkbg * egc
    dk = _dot(dkk, m["kb"], _TN) + _dot(dqk, q, _TN) + dkt * m["ekt"] + dkb * beta
    dq = _dot(dqk, k) + dqd * egc
    mm = dl * m["lmat"] + damat * m["amat"]
    ones = jnp.ones(q.shape, F32)
    rk = jnp.sum(dkt * kt, axis=-1, keepdims=True)
    dgc = (_dot_hi(mm, ones) - _dot_hi(mm, ones, _TN) + jnp.sum(dqd * m["qd"], axis=-1, keepdims=True) - rk
           + jnp.sum(dkbg * m["kbg"], axis=-1, keepdims=True))
    dglast = jnp.sum(rk, axis=-2, keepdims=True) + dcd * cd
    rowi = lax.broadcasted_iota(jnp.int32, q.shape, q.ndim - 2)
    lane = lax.broadcasted_iota(jnp.int32, q.shape, q.ndim - 1)
    dgc = dgc + jnp.where(rowi == C - 1, dglast, 0.0)
    dbeta = jnp.sum(dkb * k, axis=-1, keepdims=True) + jnp.sum(dvb * v, axis=-1, keepdims=True)
    dgt = jnp.where(lane == h, dbeta, 0.0) + jnp.where(lane == 2 * nh + h, dgc, 0.0)
    return dq, dk, dvb * beta, dgt


def _gdn_post_fwd(o, proj, ng, nh, *, name):
    S, GW = o.shape

    def body(o_ref, z_ref, g_ref, y_ref):
        ov, z = o_ref[...], z_ref[...]
        rstd = lax.rsqrt(jnp.mean(ov * ov, axis=-1, keepdims=True) + EPS)
        y_ref[...] = (ov * rstd * g_ref[...] * (z * _sigmoid(z))).astype(y_ref.dtype)

    blk = pl.BlockSpec((S, HEAD_DIM), lambda h: (0, h))
    return pl.pallas_call(
        body, grid=(nh,), in_specs=[blk, pl.BlockSpec((S, HEAD_DIM), lambda h: (0, 6 * nh + h)), pl.BlockSpec((1, HEAD_DIM), lambda h: (0, 0))],
        out_specs=blk, out_shape=jax.ShapeDtypeStruct((S, GW), BF16),
        compiler_params=_params(("parallel",)), name=name)(o, proj, ng)


def _gdn_post_bwd(o, proj, ng, dmix, nh, *, name):
    S, GW = o.shape

    def body(o_ref, z_ref, g_ref, d_ref, do_ref, dz_ref, dg_ref):
        ov, z, d = o_ref[...], z_ref[...], d_ref[...].astype(F32)
        rstd = lax.rsqrt(jnp.mean(ov * ov, axis=-1, keepdims=True) + EPS)
        oh = ov * rstd
        sz = _sigmoid(z)
        dy = d * (z * sz)
        dz_ref[...] = (d * (oh * g_ref[...]) * (sz * (1.0 + z * (1.0 - sz)))).astype(dz_ref.dtype)
        t = dy * g_ref[...]
        do_ref[...] = rstd * (t - oh * jnp.mean(t * oh, axis=-1, keepdims=True))
        part = jnp.sum(dy * oh, axis=0, keepdims=True)

        @pl.when(pl.program_id(0) == 0)
        def _():
            dg_ref[...] = part

        @pl.when(pl.program_id(0) > 0)
        def _():
            dg_ref[...] += part

    blk = pl.BlockSpec((S, HEAD_DIM), lambda h: (0, h))
    vec = pl.BlockSpec((1, HEAD_DIM), lambda h: (0, 0))
    return pl.pallas_call(
        body, grid=(nh,),
        in_specs=[blk, pl.BlockSpec((S, HEAD_DIM), lambda h: (0, 6 * nh + h)), vec, pl.BlockSpec((S, HEAD_DIM), lambda h: (0, nh + h))],
        out_specs=[blk, blk, vec],
        out_shape=[jax.ShapeDtypeStruct((S, GW), F32), jax.ShapeDtypeStruct((S, GW), BF16), jax.ShapeDtypeStruct((1, HEAD_DIM), F32)],
        compiler_params=_params(("arbitrary",)), name=name)(o, proj, ng, dmix)


def _gdn_forward(proj, conv_w, ab, ng, nh, tag):
    S = proj.shape[0]
    nc = S // CHUNK
    qkv = _gdn_qkv_fwd(proj, conv_w, nh, name=f"gdn_qkv_fwd{tag}")
    gates = _gdn_gates_fwd(proj, ab, nh, name=f"gdn_gates_fwd{tag}")
    gc_row = gates[:, 2 * nh:3 * nh].T.reshape(nh, nc, 1, CHUNK)
    o, states = _gdn_chunk_fwd(qkv, gates, gc_row, name=f"gdn_chunk_fwd{tag}")
    y = _gdn_post_fwd(o, proj, ng, nh, name=f"gdn_post_fwd{tag}")
    return y, (qkv, gates, gc_row, states, o)


def _gdn_backward(proj, conv_w, ab, ng, saved, dmix, nh, tag):
    qkv, gates, gc_row, states, o = saved
    do, dz, dng = _gdn_post_bwd(o, proj, ng, dmix, nh, name=f"gdn_post_bwd{tag}")
    dqkv, dgt_heads = _gdn_chunk_bwd(qkv, gates, gc_row, states, do, name=f"gdn_chunk_bwd{tag}")
    dx_qkv, dconv = _gdn_qkv_bwd(proj, conv_w, dqkv, nh, name=f"gdn_qkv_bwd{tag}")
    dx_g, dab = _gdn_gates_bwd(proj, ab, jnp.sum(dgt_heads, axis=0), nh, name=f"gdn_gates_bwd{tag}")
    return dx_qkv, dz, dx_g, dconv, dab, dng


def _row_tile(r, cap=128):
    t = cap
    while r % t:
        t //= 2
    assert t >= 8, r
    return t


def _adamw_update(gv, w_ref, m_ref, v_ref, d_ref, m2_ref, v2_ref):
    m2 = ADAM_B1 * m_ref[...] + (1.0 - ADAM_B1) * gv
    v2 = ADAM_B2 * v_ref[...] + (1.0 - ADAM_B2) * (gv * gv)
    m_hat = m2 / (1.0 - ADAM_B1 ** ADAM_STEP)
    v_hat = v2 / (1.0 - ADAM_B2 ** ADAM_STEP)
    d_ref[...] = -ADAM_LR * (m_hat / (jnp.sqrt(v_hat) + ADAM_EPS) + ADAM_WD * w_ref[...])
    m2_ref[...] = m2
    v2_ref[...] = v2


def _adamw(w, g, m, v, *, name):
    L, r, c = w.shape
    tr = _row_tile(r)

    def body(w_ref, g_ref, m_ref, v_ref, d_ref, m2_ref, v2_ref):
        _adamw_update(g_ref[...], w_ref, m_ref, v_ref, d_ref, m2_ref, v2_ref)

    blk = pl.BlockSpec((None, tr, c), lambda l, i: (l, i, 0))
    o = jax.ShapeDtypeStruct(w.shape, F32)
    return pl.pallas_call(
        body, grid=(L, r // tr), in_specs=[blk] * 4, out_specs=[blk] * 3, out_shape=[o, o, o],
        compiler_params=_params(("parallel", "parallel")), name=name)(w, g, m, v)


def _adamw_halves(w, g_own, g_sib, cvec, m, v, *, name):
    L, r, c = w.shape
    tr = _row_tile(r // 2)
    nbh = (r // 2) // tr

    def body(c_ref, w_ref, go_ref, gs_ref, m_ref, v_ref, g_out, d_ref, m2_ref, v2_ref):
        gv = jnp.where(pl.program_id(1) // nbh == c_ref[0], go_ref[...], gs_ref[...])
        g_out[...] = gv
        _adamw_update(gv, w_ref, m_ref, v_ref, d_ref, m2_ref, v2_ref)

    lo = lambda i: jnp.minimum(i, nbh - 1)
    hi = lambda i: jnp.maximum(i - nbh, 0)
    blk = pl.BlockSpec((None, tr, c), lambda l, i, c_ref: (l, i, 0))
    own = pl.BlockSpec((None, tr, c), lambda l, i, c_ref: (l, jnp.where(c_ref[0] == 0, lo(i), hi(i)), 0))
    sib = pl.BlockSpec((None, tr, c), lambda l, i, c_ref: (l, jnp.where(c_ref[0] == 0, hi(i), lo(i)), 0))
    o = jax.ShapeDtypeStruct(w.shape, F32)
    return pl.pallas_call(
        body,
        grid_spec=pltpu.PrefetchScalarGridSpec(
            num_scalar_prefetch=1, grid=(L, r // tr), in_specs=[blk, own, sib, blk, blk], out_specs=[blk] * 4),
        out_shape=[o, o, o, o],
        compiler_params=_params(("parallel", "arbitrary")), name=name)(cvec, w, g_own, g_sib, m, v)


def _sum_partials(p, rb, kc, prev, l, nl, *, name):
    _, _, h, c = rb.shape
    tr = _row_tile(h, 256)
    nb = h // tr
    others = [(0, 1), (1, 0), (1, 1), (2, 0), (2, 1), (3, 0), (3, 1)]

    def body(kc_ref, p_ref, *rest):
        acc = p_ref[...].astype(F32)
        for r_ref in rest[:7]:
            acc = acc + r_ref[...].astype(F32)
        rest[-1][...] = acc

    def slot(ds, dc):
        return pl.BlockSpec((None, None, tr, c), lambda i, kc_ref: ((kc_ref[0] + ds) % 4, (kc_ref[1] + dc) % 2, i, 0))

    in_specs = [pl.BlockSpec((None, tr, c), lambda i, kc_ref: (kc_ref[0], kc_ref[1] * nb + i, 0))]
    in_specs += [slot(ds, dc) for ds, dc in others]
    args = [kc, p] + [rb] * 7
    if prev is not None:
        in_specs.append(pl.BlockSpec(memory_space=pltpu.HBM))
        args.append(prev)
    return pl.pallas_call(
        body,
        grid_spec=pltpu.PrefetchScalarGridSpec(
            num_scalar_prefetch=1, grid=(nb,), in_specs=in_specs,
            out_specs=pl.BlockSpec((None, tr, c), lambda i, kc_ref: (l, i, 0))),
        out_shape=jax.ShapeDtypeStruct((nl, h, c), F32), input_output_aliases={9: 0} if prev is not None else {},
        compiler_params=_params(("parallel",)), name=name)(*args)


_MESH = pl.DeviceIdType.MESH
_HBM = pl.BlockSpec(memory_space=pltpu.HBM)


def _place():
    x, y, c = lax.axis_index("x"), lax.axis_index("y"), lax.axis_index("c")
    return x, y, c, [(1 - x, y), (x, 1 - y), (1 - x, 1 - y)]


def _cast_place(w, l, kvec, *, name):
    _, r, c = w.shape
    tr = _row_tile(r, 256)

    def body(k_ref, w_ref, o_ref):
        o_ref[...] = w_ref[...].astype(o_ref.dtype)

    return pl.pallas_call(
        body,
        grid_spec=pltpu.PrefetchScalarGridSpec(
            num_scalar_prefetch=1, grid=(r // tr,),
            in_specs=[pl.BlockSpec((None, tr, c), lambda i, k_ref: (l, i, 0))],
            out_specs=pl.BlockSpec((None, tr, c), lambda i, k_ref: (k_ref[0], i, 0))),
        out_shape=jax.ShapeDtypeStruct((4, r, c), BF16),
        compiler_params=_params(("parallel",)), name=name)(kvec, w)


_SEM = pl.BlockSpec(memory_space=pltpu.SEMAPHORE)
_ANY = pl.BlockSpec(memory_space=pl.ANY)
_EFFECT = pltpu.SideEffectType.DATAFLOW_SIDE_EFFECTING


def _in_hbm(a):
    return pltpu.with_memory_space_constraint(a, pltpu.HBM)


def _gather_copies(w_refs, send, recv, landing):
    x, y, c, chips = _place()
    k = 2 * x + y
    cps = []
    for a, w in enumerate(w_refs):
        h = w.shape[1] // 2
        for j, (cx, cy) in enumerate(chips):
            cps.append(pltpu.make_async_remote_copy(
                src_ref=w.at[k, pl.ds(c * h, h), :], dst_ref=w.at[2 * cx + cy if landing else k, pl.ds(c * h, h), :],
                send_sem=send.at[3 * a + j], recv_sem=recv.at[3 * a + j], device_id=(cx, cy, c), device_id_type=_MESH))
    return cps


def _gather_start(ws, after, *, name):
    n = len(ws)

    def body(*refs):
        send, recv = refs[n + 1], refs[n + 2]
        o_refs, token = refs[n + 3:2 * n + 3], refs[2 * n + 3]
        for cp in _gather_copies(o_refs, send, recv, landing=False):
            cp.start()
        token[...] = jnp.zeros_like(token)

    out = pl.pallas_call(
        body, in_specs=[_HBM] * n + [_ANY], out_specs=[_SEM, _SEM] + [_HBM] * n + [pl.BlockSpec(memory_space=pltpu.VMEM)],
        out_shape=[pltpu.SemaphoreType.DMA((3 * n,)), pltpu.SemaphoreType.DMA((3 * n,))]
        + [pltpu.HBM(w.shape, w.dtype) for w in ws] + [jax.ShapeDtypeStruct((8, LANES), F32)],
        input_output_aliases={a: 2 + a for a in range(n)},
        compiler_params=pltpu.CompilerParams(has_side_effects=_EFFECT), name=name)(*[_in_hbm(w) for w in ws], after)
    return out[0], out[1], list(out[2:2 + n]), out[2 + n]


def _gather_wait(send, recv, ws, after, *, name):
    n = len(ws)

    def body(*refs):
        for cp in _gather_copies(refs[:n], refs[n], refs[n + 1], landing=True):
            cp.wait_send()
            cp.wait_recv()

    return list(pl.pallas_call(
        body, in_specs=[_HBM] * n + [_SEM, _SEM, _ANY], out_specs=[_HBM] * n,
        out_shape=[pltpu.HBM(w.shape, w.dtype) for w in ws], input_output_aliases={a: a for a in range(n)},
        compiler_params=pltpu.CompilerParams(has_side_effects=_EFFECT), name=name)(*ws, send, recv, after))


def _gather_to_sibling(ws, *, name):
    n = len(ws)

    def body(*refs):
        o_refs = refs[n:2 * n]
        send, recv = refs[2 * n:]
        x, y, c, chips = _place()
        cps = []
        for a in range(n):
            h = o_refs[a].shape[1] // 2
            for j, (cx, cy) in enumerate(chips):
                landed = o_refs[a].at[2 * cx + cy, pl.ds(c * h, h), :]
                cp = pltpu.make_async_remote_copy(
                    src_ref=landed, dst_ref=landed, send_sem=send.at[3 * a + j], recv_sem=recv.at[3 * a + j],
                    device_id=(x, y, 1 - c), device_id_type=_MESH)
                cp.start()
                cps.append(cp)
        for a in range(n):
            h = o_refs[a].shape[1] // 2
            for j, (cx, cy) in enumerate(chips):
                other = o_refs[a].at[2 * cx + cy, pl.ds((1 - c) * h, h), :]
                pltpu.make_async_remote_copy(
                    src_ref=other, dst_ref=other, send_sem=send.at[3 * a + j], recv_sem=recv.at[3 * a + j],
                    device_id=(x, y, c), device_id_type=_MESH).wait_recv()
        for cp in cps:
            cp.wait_send()

    return list(pl.pallas_call(
        body, in_specs=[_HBM] * n, out_specs=[_HBM] * n,
        out_shape=[jax.ShapeDtypeStruct(w.shape, w.dtype) for w in ws],
        input_output_aliases={a: a for a in range(n)},
        scratch_shapes=[pltpu.SemaphoreType.DMA((3 * n,))] * 2, name=name)(*ws))


def _reduce_copies(p_refs, r_refs, send, recv, landing):
    x, y, c = lax.axis_index("x"), lax.axis_index("y"), lax.axis_index("c")
    k = 2 * x + y
    cps = []
    for a, (p, r) in enumerate(zip(p_refs, r_refs)):
        h = p.shape[1] // 2
        for d in range(1, 8):
            px = 1 - x if d & 4 else x
            py = 1 - y if d & 2 else y
            pc = 1 - c if d & 1 else c
            cps.append(pltpu.make_async_remote_copy(
                src_ref=p.at[2 * px + py, pl.ds(pc * h, h), :], dst_ref=r.at[2 * px + py, pc] if landing else r.at[k, c],
                send_sem=send.at[7 * a + d - 1], recv_sem=recv.at[7 * a + d - 1], device_id=(px, py, pc),
                device_id_type=_MESH))
    return cps


def _reduce_start(ps, after, *, name):
    n = len(ps)
    lands = [lax.empty((4, 2, p.shape[1] // 2, p.shape[2]), p.dtype) for p in ps]

    def body(*refs):
        send, recv = refs[2 * n + 1], refs[2 * n + 2]
        p_out, r_out, token = refs[2 * n + 3:3 * n + 3], refs[3 * n + 3:4 * n + 3], refs[4 * n + 3]
        for cp in _reduce_copies(p_out, r_out, send, recv, landing=False):
            cp.start()
        token[...] = jnp.zeros_like(token)

    out = pl.pallas_call(
        body, in_specs=[_HBM] * (2 * n) + [_ANY],
        out_specs=[_SEM, _SEM] + [_HBM] * (2 * n) + [pl.BlockSpec(memory_space=pltpu.VMEM)],
        out_shape=[pltpu.SemaphoreType.DMA((7 * n,)), pltpu.SemaphoreType.DMA((7 * n,))]
        + [pltpu.HBM(a.shape, a.dtype) for a in list(ps) + lands] + [jax.ShapeDtypeStruct((8, LANES), F32)],
        input_output_aliases={a: 2 + a for a in range(2 * n)},
        compiler_params=pltpu.CompilerParams(has_side_effects=_EFFECT), name=name)(
            *[_in_hbm(p) for p in ps], *[_in_hbm(r) for r in lands], after)
    return out[0], out[1], list(out[2:2 + n]), list(out[2 + n:2 + 2 * n]), out[2 + 2 * n]


def _reduce_wait(send, recv, ps, lands, after, *, name):
    n = len(ps)

    def body(*refs):
        for cp in _reduce_copies(refs[:n], refs[n:2 * n], refs[2 * n], refs[2 * n + 1], landing=True):
            cp.wait_send()
            cp.wait_recv()

    out = pl.pallas_call(
        body, in_specs=[_HBM] * (2 * n) + [_SEM, _SEM, _ANY], out_specs=[_HBM] * (2 * n),
        out_shape=[pltpu.HBM(a.shape, a.dtype) for a in list(ps) + list(lands)],
        input_output_aliases={a: a for a in range(2 * n)},
        compiler_params=pltpu.CompilerParams(has_side_effects=_EFFECT), name=name)(*ps, *lands, send, recv, after)
    return list(out[:n]), list(out[n:])


def _swap_with_sibling(gs, *, name):
    n = len(gs)

    def body(*refs):
        g_refs, o_refs = refs[:n], refs[n:2 * n]
        send, recv = refs[2 * n:]
        x, y, c, _ = _place()
        cps = []
        for a in range(n):
            cp = pltpu.make_async_remote_copy(
                src_ref=g_refs[a], dst_ref=o_refs[a], send_sem=send.at[a], recv_sem=recv.at[a],
                device_id=(x, y, 1 - c), device_id_type=_MESH)
            cp.start()
            cps.append(cp)
        for cp in cps:
            cp.wait_recv()
        for cp in cps:
            cp.wait_send()

    return pl.pallas_call(
        body, in_specs=[_HBM] * n, out_specs=[_HBM] * n,
        out_shape=[jax.ShapeDtypeStruct(g.shape, g.dtype) for g in gs],
        scratch_shapes=[pltpu.SemaphoreType.DMA((n,))] * 2, name=name)(*gs)


def _allreduce_small(v, *, name):
    R = v.shape[0]

    def body(v_ref, o_ref, buf, send, recv, loc):
        x, y, c = lax.axis_index("x"), lax.axis_index("y"), lax.axis_index("c")
        me = 4 * x + 2 * y + c
        mine = pltpu.make_async_copy(v_ref, buf.at[me], loc)
        mine.start()
        cps = []
        for d in range(1, 8):
            px = 1 - x if d & 4 else x
            py = 1 - y if d & 2 else y
            pc = 1 - c if d & 1 else c
            cp = pltpu.make_async_remote_copy(
                src_ref=v_ref, dst_ref=buf.at[me], send_sem=send.at[d - 1], recv_sem=recv.at[d - 1],
                device_id=(px, py, pc), device_id_type=_MESH)
            cp.start()
            cps.append((cp, 4 * px + 2 * py + pc))
        for d in range(1, 8):
            cp, peer = cps[d - 1]
            pltpu.make_async_remote_copy(
                src_ref=buf.at[peer], dst_ref=buf.at[peer], send_sem=send.at[d - 1], recv_sem=recv.at[d - 1],
                device_id=(x, y, c), device_id_type=_MESH).wait_recv()
        for cp, _ in cps:
            cp.wait_send()
        mine.wait()
        acc = buf[0]
        for i in range(1, 8):
            acc = acc + buf[i]
        o_ref[...] = acc

    return pl.pallas_call(
        body, in_specs=[pl.BlockSpec(memory_space=pltpu.VMEM)], out_specs=pl.BlockSpec(memory_space=pltpu.VMEM),
        out_shape=jax.ShapeDtypeStruct((R, LANES), F32),
        scratch_shapes=[pltpu.VMEM((8, R, LANES), F32), pltpu.SemaphoreType.DMA((7,)), pltpu.SemaphoreType.DMA((7,)),
                        pltpu.SemaphoreType.DMA],
        compiler_params=pltpu.CompilerParams(vmem_limit_bytes=VMEM_LIMIT), name=name)(v)


def _pack(arrs, row_multiple=8):
    rows = []
    for a in arrs:
        flat = a.reshape(-1)
        flat = jnp.pad(flat, (0, (-flat.shape[0]) % LANES))
        rows.append(flat.reshape(-1, LANES))
    buf = jnp.concatenate(rows, axis=0)
    return jnp.pad(buf, ((0, (-buf.shape[0]) % row_multiple), (0, 0)))


def _unpack(buf, shapes):
    out, r = [], 0
    for s in shapes:
        size = math.prod(s)
        nr = -(-size // LANES)
        out.append(buf[r:r + nr].reshape(-1)[:size].reshape(s))
        r += nr
    return out


def kernel(x, mem, mix_norm, w_in, gdn_conv, gdn_a_log, gdn_dt_bias, gdn_norm, w_out, xattn_norm, mem_norm, w_xq, w_xkv, w_xo, ffn_norm, w_up, ffn_conv, ffn_conv_bias, w_down, final_norm, loss_target, m_mix_norm, m_w_in, m_gdn_conv, m_gdn_a_log, m_gdn_dt_bias, m_gdn_norm, m_w_out, m_xattn_norm, m_mem_norm, m_w_xq, m_w_xkv, m_w_xo, m_ffn_norm, m_w_up, m_ffn_conv, m_ffn_conv_bias, m_w_down, m_final_norm, v_mix_norm, v_w_in, v_gdn_conv, v_gdn_a_log, v_gdn_dt_bias, v_gdn_norm, v_w_out, v_xattn_norm, v_mem_norm, v_w_xq, v_w_xkv, v_w_xo, v_ffn_norm, v_w_up, v_ffn_conv, v_ffn_conv_bias, v_w_down, v_final_norm):
    L = w_in.shape[0]
    _, S, D = x.shape
    nh = D // (2 * HEAD_DIM)
    GW = nh * HEAD_DIM
    n_in = 7 * GW + 2 * nh
    NP = 7 * GW + LANES
    XW = X_HEADS * HEAD_DIM
    F = w_down.shape[1] * 4
    cs_in = w_in.shape[2]
    cs_up = w_up.shape[2]
    cs_xo = w_xo.shape[2]
    tu = _tile(cs_up, 1408)
    per = cs_up // tu
    fper = F // tu
    assert n_in == 4 * cs_in and F % tu == 0 and 2 * F == 4 * cs_up

    xi, yi, ci = lax.axis_index("x"), lax.axis_index("y"), lax.axis_index("c")
    chip = 2 * xi + yi
    cvec = jnp.reshape(ci, (1,)).astype(jnp.int32)

    cs_gc, cs_fc = gdn_conv.shape[2], ffn_conv.shape[2]
    keep = jnp.where(ci == 0, 1.0, 0.0).astype(F32)
    gc_full = lax.dynamic_update_slice(jnp.zeros((L, SHORT_CONV, 4 * cs_gc), F32), gdn_conv * keep, (0, 0, chip * cs_gc))
    fc_full = lax.dynamic_update_slice(jnp.zeros((L, FFN_CONV, 4 * cs_fc), F32), ffn_conv * keep, (0, 0, chip * cs_fc))
    conv_all = _allreduce_small(_pack([gc_full, fc_full]), name="allgather_conv")
    gdn_conv_full, ffn_conv_full = _unpack(conv_all, [gc_full.shape, fc_full.shape])

    big = [w_in, w_out, w_xq, w_xkv, w_xo, w_up, w_down]
    kvec = jnp.reshape(chip, (1,)).astype(jnp.int32)
    kcvec = jnp.stack([chip, ci]).astype(jnp.int32)
    ab = jnp.zeros((L, 2, LANES), F32).at[:, 0, nh:2 * nh].set(gdn_a_log).at[:, 1, nh:2 * nh].set(gdn_dt_bias)

    def vec(p, l):
        return p[l:l + 1]

    xo_fwd_b = pl.BlockSpec((None, XW, cs_xo), lambda i, j, k: (j, 0, 0))
    xo_dg_b = pl.BlockSpec((None, XW, cs_xo), lambda i, j, k: (k, 0, 0))
    xo_wg_o = pl.BlockSpec((None, XW, cs_xo), lambda i, j, k: (j, 0, 0))
    up_fwd_b = pl.BlockSpec((None, D, tu), lambda i, j, k: (j // per, 0, j % per))

    def fwd_layer(l, xc, g_in, rest, token):
        g_mix = vec(mix_norm, l) if token is None else vec(mix_norm, l) + token[0:1, 0:1]
        w_in_l = jnp.concatenate([g_in[0], g_in[1], g_in[2], g_in[3], jnp.zeros((D, NP - n_in), BF16)], axis=1)
        s = dict(x0=xc, w_in=w_in_l)
        s["h"] = _rms_fwd(xc, g_mix, name="rms_mix_fwd")
        s["proj"] = _matmul(s["h"], w_in_l, tn=2432, tk=D, name="mm_in_fwd")
        s["sb"], s["tot"] = _sb_fwd(s["proj"], nh, name="sb_fwd")
        gdn_out, s["gdn"] = _gdn_forward(s["proj"], gdn_conv_full[l], ab[l], vec(gdn_norm, l), nh, "")
        g_out, g_xq, g_xkv, g_xo, g_up, g_down = rest(gdn_out)
        w_out_l, w_xq_l, w_xkv_l, w_down_l = g_out.reshape(2 * GW, D), g_xq.reshape(D, XW), g_xkv.reshape(D, 2 * XW), g_down.reshape(F, D)
        s.update(w_out=w_out_l, w_xq=w_xq_l, w_xkv=w_xkv_l, w_xo=g_xo, w_up=g_up, w_down=w_down_l)
        s["mixed"] = jnp.concatenate([s["sb"], gdn_out], axis=1)
        s["x1"] = _matmul(s["mixed"], w_out_l, res=xc, tm=1024, tn=1024, tk=2 * GW, name="mm_out_fwd")
        s["memn"] = _rms_fwd(mem[0], vec(mem_norm, l), name="rms_mem_fwd")
        s["kv"] = _matmul(s["memn"], w_xkv_l, out_dtype=BF16, tk=D, name="mm_xkv_fwd")
        s["hq"] = _rms_fwd(s["x1"], vec(xattn_norm, l), name="rms_xattn_fwd")
        s["q"] = _matmul(s["hq"], w_xq_l, out_dtype=BF16, tk=D, name="mm_xq_fwd")
        s["xo"] = _xattn_fwd(s["q"], s["kv"], name="xattn_fwd")
        s["x2"] = _matmul(s["xo"], g_xo, res=s["x1"], dims=(S, D, XW), tn=cs_xo, tk=XW, b_spec=xo_fwd_b, name="mm_xo_fwd")
        s["hf"] = _rms_fwd(s["x2"], vec(ffn_norm, l), name="rms_ffn_fwd")
        s["u"] = _matmul(s["hf"], g_up, dims=(S, 2 * F, D), tn=tu, tk=D, b_spec=up_fwd_b, name="mm_up_fwd")
        s["act"] = _ffn_act_fwd(s["u"], ffn_conv_full[l], ffn_conv_bias[l:l + 1], name="ffn_act_fwd")
        x3 = _matmul(s["act"], w_down_l, res=s["x2"], tm=1024, tn=1024, tk=tu, name="mm_down_fwd")
        return x3, s

    def bwd_ffn(l, s, dx3, dx3b):
        dact = _matmul(dx3b, s["w_down"], tb=True, tm=1024, tk=D, name="mm_down_dgrad")
        d_down = _matmul(s["act"], dx3b, ta=True, tn=1024, tk=S, out_dtype=BF16, name="mm_down_wgrad")
        du3, dcw3, dcb3 = _ffn_act_bwd(s["u"], ffn_conv_full[l], ffn_conv_bias[l:l + 1], dact, name="ffn_act_bwd")
        tq, td = _tile(S, 1024), _tile(D, 1024)
        dhf = _matmul(du3, s["w_up"], tb=True, dims=(S, D, 2 * F), tm=tq, tn=td, tk=tu,
                      a_spec=pl.BlockSpec((None, tq, tu), lambda i, j, k: (k // fper, i, k % fper)),
                      b_spec=pl.BlockSpec((None, td, tu), lambda i, j, k: (k // per, j, k % per)), name="mm_up_dgrad")
        d_up = _matmul(s["hf"], du3, ta=True, dims=(D, 2 * F, S), tn=tu, tk=S,
                       b_spec=pl.BlockSpec((None, S, tu), lambda i, j, k: (j // fper, 0, j % fper)),
                       o_spec=pl.BlockSpec((None, _tile(D, 512), tu), lambda i, j, k: (j // per, i, j % per)),
                       out_shape=jax.ShapeDtypeStruct((4, D, cs_up), BF16), name="mm_up_wgrad")
        dx2, dx2b, dg_ffn = _rms_bwd(s["x2"], vec(ffn_norm, l), dhf, dx3, name="rms_bwd")
        small = [dg_ffn, jnp.concatenate([dcw3[0], dcw3[1]], axis=1), jnp.concatenate([dcb3[0], dcb3[1]], axis=1)]
        return dx2, dx2b, {5: d_up, 6: d_down.reshape(4, -1, D)}, small

    def bwd_rest(l, s, dx2, dx2b):
        dxo = _matmul(dx2b, s["w_xo"], tb=True, dims=(S, XW, D), tn=XW, tk=cs_xo, b_spec=xo_dg_b, name="mm_xo_dgrad")
        d_xo = _matmul(s["xo"], dx2b, ta=True, dims=(XW, D, S), tm=XW, tn=cs_xo, tk=S, o_spec=xo_wg_o,
                       out_shape=jax.ShapeDtypeStruct((4, XW, cs_xo), BF16), name="mm_xo_wgrad")
        dq, dk, dv = _xattn_bwd(s["q"], s["kv"], dxo, name="xattn_bwd")
        dkv = jnp.concatenate([dk, dv], axis=1)
        dhq = _matmul(dq, s["w_xq"], tb=True, tk=XW, name="mm_xq_dgrad")
        d_xq = _matmul(s["hq"], dq, ta=True, tk=S, out_dtype=BF16, name="mm_xq_wgrad")
        dmemn = _matmul(dkv, s["w_xkv"], tb=True, tk=2 * XW, name="mm_xkv_dgrad")
        d_xkv = _matmul(s["memn"], dkv, ta=True, tk=mem.shape[1], out_dtype=BF16, name="mm_xkv_wgrad")
        _, _, dg_mem = _rms_bwd(mem[0], vec(mem_norm, l), dmemn, None, name="rms_mem_bwd")
        dx1, dx1b, dg_xattn = _rms_bwd(s["x1"], vec(xattn_norm, l), dhq, dx2, name="rms_bwd")
        dmix = _matmul(dx1b, s["w_out"], tb=True, tm=1024, tn=1024, tk=D, name="mm_out_dgrad")
        d_out = _matmul(s["mixed"], dx1b, ta=True, tm=1024, tn=1024, tk=S, out_dtype=BF16, name="mm_out_wgrad")
        dq_s, dk_s, dv_s = _sb_bwd(s["proj"], s["tot"], dmix, nh, name="sb_bwd")
        dx_qkv, dz, dx_g, dconv, dab, dng = _gdn_backward(s["proj"], gdn_conv_full[l], ab[l], vec(gdn_norm, l), s["gdn"], dmix, nh, "")
        dproj = jnp.concatenate([dq_s, dk_s, dv_s, dx_qkv, dz, dx_g], axis=1)
        dh = _matmul(dproj, s["w_in"], tb=True, tm=1024, tn=1024, tk=2432, name="mm_in_dgrad")
        d_in = _matmul(s["h"], dproj, ta=True, tn=2432, tk=S, out_dtype=BF16, name="mm_in_wgrad")
        dx0, dx0b, dg_mix = _rms_bwd(s["x0"], vec(mix_norm, l), dh, dx1, name="rms_bwd")
        slabs = {0: jnp.stack([d_in[:, i * cs_in:(i + 1) * cs_in] for i in range(4)]), 1: d_out.reshape(4, -1, D),
                 2: d_xq.reshape(4, -1, XW), 3: d_xkv.reshape(4, -1, 2 * XW), 4: d_xo}
        return dx0, dx0b, slabs, [dg_mix, dconv, dab, dng, dg_xattn, dg_mem]

    def start_gather(l, idxs, after, tag):
        placed = [_cast_place(big[i], l, kvec, name=f"cast_place_{l}") for i in idxs]
        return _gather_start(placed, after, name=f"gather_start_{l}{tag}")

    def end_gather(pending, after, l, tag):
        send, recv, ws, _ = pending
        ws = _gather_wait(send, recv, ws, after, name=f"gather_wait_{l}{tag}")
        return _gather_to_sibling(ws, name=f"gather_to_sibling{tag}")

    xc = x[0]
    saved = []
    first = start_gather(0, [0], conv_all, "a")
    second = start_gather(0, list(range(1, 7)), first[3], "b")
    for l in range(L):
        if l == 0:
            g_in = end_gather(first, xc, 0, "a")[0]
            rest = lambda after: end_gather(second, after, 0, "b")
            order = second[3]
        else:
            wts = end_gather(pending, xc, l, "")
            g_in, rest, order = wts[0], (lambda after, wts=wts: wts[1:]), wts[1]
        token = None
        if l + 1 < L:
            pending = start_gather(l + 1, list(range(7)), order, "")
            token = pending[3]
        xc, s = fwd_layer(l, xc, g_in, rest, token)
        saved.append(s)
    loss_blk, dxc, dxcb, dg_final = _loss_head(xc, final_norm[None, :], loss_target[0], name="loss_head")

    def start_reduce(slabs, l, tag):
        idxs = sorted(slabs)
        return idxs, l, tag, _reduce_start([slabs[i] for i in idxs], cvec, name=f"reduce_start_{l}{tag}")

    def finish_reduce(item, sums, after):
        idxs, l, tag, (send, recv, ps, lands, _) = item
        ps, lands = _reduce_wait(send, recv, ps, lands, after, name=f"reduce_wait_{l}{tag}")
        for i, p, rb in zip(idxs, ps, lands):
            sums[i] = _sum_partials(p, rb, kcvec, sums[i], l, L, name=f"sum_partials_{l}")

    sums = [None] * 7
    small_by_layer = [None] * L
    in_flight = []
    for l in reversed(range(L)):
        if in_flight:
            dxcb = dxcb + in_flight[-1][3][4][0, 0].astype(BF16)
        dx2, dx2b, slabs_ffn, small_ffn = bwd_ffn(l, saved[l], dxc, dxcb)
        for item in in_flight:
            finish_reduce(item, sums, dx2)
        in_flight = [start_reduce(slabs_ffn, l, "f")]
        dx2b = dx2b + in_flight[-1][3][4][0, 0].astype(BF16)
        dxc, dxcb, slabs_rest, small_rest = bwd_rest(l, saved[l], dx2, dx2b)
        saved[l] = None
        in_flight.append(start_reduce(slabs_rest, l, "r"))
        small_by_layer[l] = small_rest + small_ffn

    small_flat = [a for l in range(L) for a in small_by_layer[l]] + [dg_final, loss_blk[0:1]]
    red_buf = _allreduce_small(_pack(small_flat), name="allreduce_small")
    red = _unpack(red_buf, [a.shape for a in small_flat])
    per_layer = [red[9 * l:9 * l + 9] for l in range(L)]
    col = lambda i: jnp.concatenate([p[i] for p in per_layer], axis=0)
    stk = lambda i: jnp.stack([p[i] for p in per_layer])
    g_conv_full, g_ab, g_fconv_full = stk(1), stk(2), stk(7)
    grads_small = dict(
        mix_norm=col(0), gdn_conv=lax.dynamic_slice(g_conv_full, (0, 0, chip * cs_gc), (L, SHORT_CONV, cs_gc)),
        gdn_a_log=g_ab[:, 0, nh:2 * nh], gdn_dt_bias=g_ab[:, 1, nh:2 * nh], gdn_norm=col(3), xattn_norm=col(4),
        mem_norm=col(5), ffn_norm=col(6), ffn_conv=lax.dynamic_slice(g_fconv_full, (0, 0, chip * cs_fc), (L, FFN_CONV, cs_fc)),
        ffn_conv_bias=col(8), final_norm=red[-2][0])
    loss = red[-1][0, 0]

    names_small = ["mix_norm", "gdn_conv", "gdn_a_log", "gdn_dt_bias", "gdn_norm", "xattn_norm", "mem_norm", "ffn_norm",
                   "ffn_conv", "ffn_conv_bias", "final_norm"]
    w_small = dict(mix_norm=mix_norm, gdn_conv=gdn_conv, gdn_a_log=gdn_a_log, gdn_dt_bias=gdn_dt_bias, gdn_norm=gdn_norm,
                   xattn_norm=xattn_norm, mem_norm=mem_norm, ffn_norm=ffn_norm, ffn_conv=ffn_conv, ffn_conv_bias=ffn_conv_bias,
                   final_norm=final_norm)
    m_small = dict(mix_norm=m_mix_norm, gdn_conv=m_gdn_conv, gdn_a_log=m_gdn_a_log, gdn_dt_bias=m_gdn_dt_bias, gdn_norm=m_gdn_norm,
                   xattn_norm=m_xattn_norm, mem_norm=m_mem_norm, ffn_norm=m_ffn_norm, ffn_conv=m_ffn_conv,
                   ffn_conv_bias=m_ffn_conv_bias, final_norm=m_final_norm)
    v_small = dict(mix_norm=v_mix_norm, gdn_conv=v_gdn_conv, gdn_a_log=v_gdn_a_log, gdn_dt_bias=v_gdn_dt_bias, gdn_norm=v_gdn_norm,
                   xattn_norm=v_xattn_norm, mem_norm=v_mem_norm, ffn_norm=v_ffn_norm, ffn_conv=v_ffn_conv,
                   ffn_conv_bias=v_ffn_conv_bias, final_norm=v_final_norm)
    shapes_small = [w_small[n].shape for n in names_small]
    packed = [_pack([d[n] for n in names_small], row_multiple=128)[None] for d in (w_small, grads_small, m_small, v_small)]
    upd_small = [_unpack(o[0], shapes_small) for o in _adamw(*packed, name="adamw_small")]
    delta, new_m, new_v = [dict(zip(names_small, u)) for u in upd_small]
    grads = dict(grads_small)
    big_names = ["w_in", "w_out", "w_xq", "w_xkv", "w_xo", "w_up", "w_down"]
    big_m = [m_w_in, m_w_out, m_w_xq, m_w_xkv, m_w_xo, m_w_up, m_w_down]
    big_v = [v_w_in, v_w_out, v_w_xq, v_w_xkv, v_w_xo, v_w_up, v_w_down]
    after = red_buf
    for item in in_flight:
        finish_reduce(item, sums, after)
        idxs, tag = item[0], item[2]
        from_sib = _swap_with_sibling([sums[i] for i in idxs], name=f"swap_halves_{tag}")
        for i, gs in zip(idxs, from_sib):
            n = big_names[i]
            grads[n], delta[n], new_m[n], new_v[n] = _adamw_halves(big[i], sums[i], gs, cvec, big_m[i], big_v[i],
                                                                   name=f"adamw_{n}")
        after = delta[big_names[idxs[-1]]]

    order = ["mix_norm", "w_in", "gdn_conv", "gdn_a_log", "gdn_dt_bias", "gdn_norm", "w_out", "xattn_norm", "mem_norm", "w_xq",
             "w_xkv", "w_xo", "ffn_norm", "w_up", "ffn_conv", "ffn_conv_bias", "w_down", "final_norm"]
    return (loss, dxc[None], *[grads[n] for n in order], *[delta[n] for n in order], *[new_m[n] for n in order],
            *[new_v[n] for n in order])
```
